```python
import math
import jax, jax.numpy as jnp
from jax import lax
import numpy as np


D_MODEL = 1024
BATCH = 8
SEQ = 4096
DEPTH = 4

N_MEM = 256
N_BRANCH = 4
W_BRANCH = D_MODEL // 2
POOL_WINDOWS = (2, 4, 8, 16)
POOL_GROUP = W_BRANCH // len(POOL_WINDOWS)
DIL_GROUPS = ((128, 1), (512, 4), (2048, 16))
ATT_HEADS = 8
ATT_HEAD_DIM = W_BRANCH // ATT_HEADS
SSM_GROUP = 16
SSM_GROUPS = W_BRANCH // SSM_GROUP
SSM_STATE = 64
SGU_CHUNK = 128
SGU_GROUPS = 4
SGU_GROUP_DIM = W_BRANCH // SGU_GROUPS
X_HEADS = 4
X_HEAD_DIM = 128
D_FF = 4 * D_MODEL
REL_BUCKETS = 32
REL_MAX_DIST = 2048
EPS = 1e-6
NEG_INF = -1e30
N_ATT_COLS = 3 * len(DIL_GROUPS) * W_BRANCH
OFF_POOL = 0
OFF_ATT = OFF_POOL + W_BRANCH
OFF_SSM = OFF_ATT + N_ATT_COLS
OFF_SGU = OFF_SSM + W_BRANCH
OFF_GATE = OFF_SGU + 2 * W_BRANCH
IN_WIDTH = OFF_GATE + N_BRANCH * D_MODEL

kernel_name = 'hybrid_gated_parallel_mixer_block'


def rmsnorm(x, g):
    xf = x.astype(jnp.float32)
    y = xf * lax.rsqrt(jnp.mean(xf * xf, axis=-1, keepdims=True) + EPS)
    return (y * g.astype(jnp.float32)).astype(x.dtype)


def _t5_bucket(n):
    exact = REL_BUCKETS // 2
    nf = np.maximum(n, 1).astype(np.float32)
    large = exact + (np.log(nf / exact) / np.log(REL_MAX_DIST / exact) * (REL_BUCKETS - exact)).astype(np.int32)
    large = np.minimum(large, REL_BUCKETS - 1)
    return np.where(n < exact, n, large).astype(np.int32)


def _band_pattern(band, dil):
    i = np.arange(band)[:, None]
    kk = np.arange(2 * band)[None, :]
    dist = band + i - kk
    local = (dist >= 0) & (dist <= band)
    bucket = _t5_bucket(np.clip(dist, 0, band) * dil)
    return local, bucket


def pool_mixer(h, w_pool, scale):
    B, S, _ = h.shape
    hf = h.astype(jnp.float32)
    cs = jnp.pad(jnp.cumsum(hf, axis=1), ((0, 0), (1, 0), (0, 0)))
    t = jnp.arange(S)
    outs = []
    for gi, w in enumerate(POOL_WINDOWS):
        sl = slice(gi * POOL_GROUP, (gi + 1) * POOL_GROUP)
        lo = jnp.maximum(t + 1 - w, 0)
        cnt = jnp.minimum(t + 1, w).astype(jnp.float32)
        mean = (cs[:, 1:, sl] - cs[:, lo, sl]) / cnt[None, :, None]
        outs.append(mean - hf[..., sl])
    p = jnp.stack(outs, axis=2)
    y = jnp.einsum('bsgc,gcd->bsgd', p, w_pool.astype(jnp.float32)).reshape(B, S, W_BRANCH)
    return (y * scale.astype(jnp.float32)).astype(h.dtype)


def _dilated_group(q, k, v, bias, local, band, dil):
    B, S, H, E = q.shape
    L = S // dil
    nb = -(-L // band)
    Lp = nb * band

    def to_sub(a):
        a = a.reshape(B, L, dil, H, E).transpose(0, 2, 1, 3, 4)
        a = jnp.pad(a, ((0, 0), (0, 0), (0, Lp - L), (0, 0), (0, 0)))
        return a.reshape(B, dil, nb, band, H, E)

    def with_prev(a):
        prev = jnp.pad(a, ((0, 0), (0, 0), (1, 0), (0, 0), (0, 0), (0, 0)))[:, :, :-1]
        return jnp.concatenate([prev, a], axis=3)

    qs = to_sub(q)
    kk = with_prev(to_sub(k))
    vv = with_prev(to_sub(v))
    logits = jnp.einsum('brnqhe,brnkhe->brnhqk', qs, kk).astype(jnp.float32) * (E ** -0.5) + bias
    first = (jnp.arange(nb) == 0)[:, None, None] & (jnp.arange(2 * band) < band)[None, None, :]
    valid = local[None] & ~first
    logits = jnp.where(valid[None, None, :, None], logits, NEG_INF)
    m = jnp.max(logits, axis=-1, keepdims=True)
    p = jnp.exp(logits - m)
    s = jnp.sum(p, axis=-1, keepdims=True)
    o = jnp.einsum('brnhqk,brnkhe->brnqhe', (p / s).astype(v.dtype), vv)
    lse = (m + jnp.log(s))[..., 0]
    o = o.reshape(B, dil, Lp, H, E)[:, :, :L].transpose(0, 2, 1, 3, 4).reshape(B, S, H, E)
    lse = lse.transpose(0, 1, 2, 4, 3).reshape(B, dil, Lp, H)[:, :, :L]
    lse = lse.transpose(0, 2, 1, 3).reshape(B, S, H)
    return o, lse


def dilated_attention(qkv, rel_bias):
    B, S, _ = qkv.shape
    ng = len(DIL_GROUPS)
    q, k, v = [a.reshape(B, S, ng, ATT_HEADS, ATT_HEAD_DIM) for a in jnp.split(qkv, 3, axis=-1)]
    outs, lses = [], []
    for g, (win, dil) in enumerate(DIL_GROUPS):
        band = win // dil
        local, bucket = _band_pattern(band, dil)
        bias = rel_bias[jnp.asarray(bucket)][..., g * ATT_HEADS:(g + 1) * ATT_HEADS]
        bias = bias.transpose(2, 0, 1).astype(jnp.float32)
        o, lse = _dilated_group(q[:, :, g], k[:, :, g], v[:, :, g], bias, jnp.asarray(local), band, dil)
        outs.append(o.astype(jnp.float32))
        lses.append(lse)
    wts = jax.nn.softmax(jnp.stack(lses, axis=0), axis=0)
    out = jnp.sum(wts[..., None] * jnp.stack(outs, axis=0), axis=0)
    return out.reshape(B, S, W_BRANCH).astype(qkv.dtype)


def s5_mixer(u, a_re, a_im, log_dt, b_re, b_im, c_re, c_im, d_skip, w_glu, b_glu):
    f32 = jnp.float32
    B, S, _ = u.shape
    uf = u.astype(f32).reshape(B, S, SSM_GROUPS, SSM_GROUP)
    lam_re = jnp.minimum(a_re.astype(f32), -1e-4)
    lam_im = a_im.astype(f32)
    dt = jnp.exp(log_dt.astype(f32))[:, None]
    mag = jnp.exp(lam_re * dt)
    ab_re, ab_im = mag * jnp.cos(lam_im * dt), mag * jnp.sin(lam_im * dt)
    den = lam_re * lam_re + lam_im * lam_im
    f_re = ((ab_re - 1.0) * lam_re + ab_im * lam_im) / den
    f_im = (ab_im * lam_re - (ab_re - 1.0) * lam_im) / den
    br, bi = b_re.astype(f32), b_im.astype(f32)
    bb_re = f_re[..., None] * br - f_im[..., None] * bi
    bb_im = f_re[..., None] * bi + f_im[..., None] * br
    bu_re = jnp.einsum('bsgc,gpc->bsgp', uf, bb_re)
    bu_im = jnp.einsum('bsgc,gpc->bsgp', uf, bb_im)

    def combine(e1, e2):
        a1r, a1i, b1r, b1i = e1
        a2r, a2i, b2r, b2i = e2
        return (a2r * a1r - a2i * a1i, a2r * a1i + a2i * a1r,
                a2r * b1r - a2i * b1i + b2r, a2r * b1i + a2i * b1r + b2i)

    ar = jnp.broadcast_to(ab_re[None, None], (1, S, SSM_GROUPS, SSM_STATE))
    ai = jnp.broadcast_to(ab_im[None, None], (1, S, SSM_GROUPS, SSM_STATE))
    _, _, hr, hi = lax.associative_scan(combine, (ar, ai, bu_re, bu_im), axis=1)
    y = (jnp.einsum('bsgp,gcp->bsgc', hr, c_re.astype(f32))
         - jnp.einsum('bsgp,gcp->bsgc', hi, c_im.astype(f32))
         + uf * d_skip.astype(f32).reshape(SSM_GROUPS, SSM_GROUP))
    g = jax.nn.gelu(y.reshape(B, S, W_BRANCH))
    out = g * jax.nn.sigmoid(g @ w_glu.astype(f32) + b_glu.astype(f32))
    return out.astype(u.dtype)


def sgu_mixer(z, ln_g, ln_b, w_s, b_s):
    B, S, _ = z.shape
    z = jax.nn.gelu(z)
    u, v = jnp.split(z, 2, axis=-1)
    vf = v.astype(jnp.float32)
    mu = jnp.mean(vf, axis=-1, keepdims=True)
    var = jnp.mean(jnp.square(vf - mu), axis=-1, keepdims=True)
    vf = (vf - mu) * lax.rsqrt(var + EPS) * ln_g.astype(jnp.float32) + ln_b.astype(jnp.float32)
    vf = vf.reshape(B, S // SGU_CHUNK, SGU_CHUNK, SGU_GROUPS, SGU_GROUP_DIM)
    tri = jnp.tril(jnp.ones((SGU_CHUNK, SGU_CHUNK), jnp.float32))
    ws = w_s.astype(jnp.float32) * tri[None]
    sv = jnp.einsum('gts,bnsgc->bntgc', ws, vf) + b_s.astype(jnp.float32).T[:, :, None]
    return (u.astype(jnp.float32) * sv.reshape(B, S, W_BRANCH)).astype(z.dtype)


def cross_attn(h, mem_n, w_cq, w_ckv, w_co):
    B, S, _ = h.shape
    q = (h @ w_cq).reshape(B, S, X_HEADS, X_HEAD_DIM)
    k, v = jnp.split(mem_n @ w_ckv, 2, axis=-1)
    k = k.reshape(B, N_MEM, X_HEADS, X_HEAD_DIM)
    v = v.reshape(B, N_MEM, X_HEADS, X_HEAD_DIM)
    logits = jnp.einsum('bshe,bmhe->bhsm', q, k).astype(jnp.float32) * (X_HEAD_DIM ** -0.5)
    p = jax.nn.softmax(logits, axis=-1).astype(h.dtype)
    o = jnp.einsum('bhsm,bmhe->bshe', p, v).reshape(B, S, X_HEADS * X_HEAD_DIM)
    return o @ w_co


def _fwd_setup_inputs(seed: int = 0) -> dict:
    key = jax.random.key(seed)
    ks = iter(jax.random.split(key, 48))
    f32 = jnp.float32
    L = DEPTH

    def nrm(shape, scale):
        return jax.random.normal(next(ks), shape, f32) * scale

    def gain(shape):
        return 1.0 + nrm(shape, 0.02)

    d = {}
    d['x'] = nrm((BATCH, SEQ, D_MODEL), 1.0)
    d['mem'] = nrm((BATCH, N_MEM, D_MODEL), 1.0)
    d['rel_bias'] = nrm((REL_BUCKETS, len(DIL_GROUPS) * ATT_HEADS), 0.5)
    d['g_mix_pre'] = gain((L, D_MODEL))
    d['g_mix_post'] = gain((L, D_MODEL))
    d['w_in'] = nrm((L, D_MODEL, IN_WIDTH), D_MODEL ** -0.5)
    d['gate_b'] = nrm((L, N_BRANCH, D_MODEL), 0.01)
    d['pool_w'] = nrm((L, len(POOL_WINDOWS), POOL_GROUP, POOL_GROUP), POOL_GROUP ** -0.5)
    d['pool_scale'] = 1.0 + nrm((L, W_BRANCH), 0.1)
    n_idx = jnp.arange(SSM_STATE, dtype=f32)
    d['a_re'] = -0.5 + nrm((L, SSM_GROUPS, SSM_STATE), 0.01)
    d['a_im'] = jnp.pi * n_idx + nrm((L, SSM_GROUPS, SSM_STATE), 0.01)
    d['log_dt'] = jax.random.uniform(next(ks), (L, SSM_GROUPS), f32, math.log(1e-3), math.log(1e-1))
    d['b_re'] = nrm((L, SSM_GROUPS, SSM_STATE, SSM_GROUP), (2 * SSM_GROUP) ** -0.5)
    d['b_im'] = nrm((L, SSM_GROUPS, SSM_STATE, SSM_GROUP), (2 * SSM_GROUP) ** -0.5)
    d['c_re'] = nrm((L, SSM_GROUPS, SSM_GROUP, SSM_STATE), (2 * SSM_STATE) ** -0.5)
    d['c_im'] = nrm((L, SSM_GROUPS, SSM_GROUP, SSM_STATE), (2 * SSM_STATE) ** -0.5)
    d['d_skip'] = nrm((L, W_BRANCH), 1.0)
    d['w_glu'] = nrm((L, W_BRANCH, W_BRANCH), W_BRANCH ** -0.5)
    d['b_glu'] = nrm((L, W_BRANCH), 0.01)
    d['sgu_ln_g'] = gain((L, W_BRANCH))
    d['sgu_ln_b'] = nrm((L, W_BRANCH), 0.01)
    d['w_s'] = nrm((L, SGU_GROUPS, SGU_CHUNK, SGU_CHUNK), SGU_CHUNK ** -0.5)
    d['b_s'] = 1.0 + nrm((L, SGU_GROUPS, SGU_CHUNK), 0.01)
    d['w_up'] = nrm((L, N_BRANCH, W_BRANCH, D_MODEL), W_BRANCH ** -0.5)
    d['w_out'] = nrm((L, D_MODEL, D_MODEL), D_MODEL ** -0.5)
    d['g_x_pre'] = gain((L, D_MODEL))
    d['g_x_post'] = gain((L, D_MODEL))
    d['g_mem'] = gain((L, D_MODEL))
    d['w_cq'] = nrm((L, D_MODEL, X_HEADS * X_HEAD_DIM), D_MODEL ** -0.5)
    d['w_ckv'] = nrm((L, D_MODEL, 2 * X_HEADS * X_HEAD_DIM), D_MODEL ** -0.5)
    d['w_co'] = nrm((L, X_HEADS * X_HEAD_DIM, D_MODEL), (X_HEADS * X_HEAD_DIM) ** -0.5)
    d['g_ff_pre'] = gain((L, D_MODEL))
    d['g_ff_post'] = gain((L, D_MODEL))
    d['w_ff1'] = nrm((L, D_MODEL, D_FF), D_MODEL ** -0.5)
    d['w_ff2'] = nrm((L, D_FF, D_MODEL), D_FF ** -0.5)
    return d


def _fwd_reference(x, mem, rel_bias, g_mix_pre, g_mix_post, w_in, gate_b, pool_w, pool_scale,
              a_re, a_im, log_dt, b_re, b_im, c_re, c_im, d_skip, w_glu, b_glu,
              sgu_ln_g, sgu_ln_b, w_s, b_s, w_up, w_out, g_x_pre, g_x_post, g_mem,
              w_cq, w_ckv, w_co, g_ff_pre, g_ff_post, w_ff1, w_ff2):
    B, S, _ = x.shape
    for l in range(DEPTH):
        h = rmsnorm(x, g_mix_pre[l])
        proj = h @ w_in[l]
        a_out = pool_mixer(proj[..., OFF_POOL:OFF_ATT], pool_w[l], pool_scale[l])
        b_out = dilated_attention(proj[..., OFF_ATT:OFF_SSM], rel_bias)
        c_out = s5_mixer(proj[..., OFF_SSM:OFF_SGU], a_re[l], a_im[l], log_dt[l], b_re[l], b_im[l],
                         c_re[l], c_im[l], d_skip[l], w_glu[l], b_glu[l])
        d_out = sgu_mixer(proj[..., OFF_SGU:OFF_GATE], sgu_ln_g[l], sgu_ln_b[l], w_s[l], b_s[l])
        gates = jax.nn.sigmoid(proj[..., OFF_GATE:].reshape(B, S, N_BRANCH, D_MODEL) + gate_b[l])
        branches = (a_out, b_out, c_out, d_out)
        merged = gates[:, :, 0] * (branches[0] @ w_up[l, 0])
        for i in range(1, N_BRANCH):
            merged = merged + gates[:, :, i] * (branches[i] @ w_up[l, i])
        x = x + rmsnorm(merged @ w_out[l], g_mix_post[l])
        h = rmsnorm(x, g_x_pre[l])
        mem_n = rmsnorm(mem, g_mem[l])
        x = x + rmsnorm(cross_attn(h, mem_n, w_cq[l], w_ckv[l], w_co[l]), g_x_post[l])
        h = rmsnorm(x, g_ff_pre[l])
        ff = jnp.square(jax.nn.relu(h @ w_ff1[l])) @ w_ff2[l]
        x = x + rmsnorm(ff, g_ff_post[l])
    return x


import jax as _jax
import jax.numpy as _jnp

TWIN_FORMAT = 'train_step'
FWD_PARAMS = ['x', 'mem', 'rel_bias', 'g_mix_pre', 'g_mix_post', 'w_in', 'gate_b', 'pool_w', 'pool_scale', 'a_re', 'a_im', 'log_dt', 'b_re', 'b_im', 'c_re', 'c_im', 'd_skip', 'w_glu', 'b_glu', 'sgu_ln_g', 'sgu_ln_b', 'w_s', 'b_s', 'w_up', 'w_out', 'g_x_pre', 'g_x_post', 'g_mem', 'w_cq', 'w_ckv', 'w_co', 'g_ff_pre', 'g_ff_post', 'w_ff1', 'w_ff2']
TWIN_WEIGHTS = ['rel_bias', 'g_mix_pre', 'g_mix_post', 'w_in', 'gate_b', 'pool_w', 'pool_scale', 'a_re', 'a_im', 'log_dt', 'b_re', 'b_im', 'c_re', 'c_im', 'd_skip', 'w_glu', 'b_glu', 'sgu_ln_g', 'sgu_ln_b', 'w_s', 'b_s', 'w_up', 'w_out', 'g_x_pre', 'g_x_post', 'g_mem', 'w_cq', 'w_ckv', 'w_co', 'g_ff_pre', 'g_ff_post', 'w_ff1', 'w_ff2']
TWIN_DIFF_INPUT = 'x'
TWIN_INPUTS = ['x', 'mem', 'rel_bias', 'g_mix_pre', 'g_mix_post', 'w_in', 'gate_b', 'pool_w', 'pool_scale', 'a_re', 'a_im', 'log_dt', 'b_re', 'b_im', 'c_re', 'c_im', 'd_skip', 'w_glu', 'b_glu', 'sgu_ln_g', 'sgu_ln_b', 'w_s', 'b_s', 'w_up', 'w_out', 'g_x_pre', 'g_x_post', 'g_mem', 'w_cq', 'w_ckv', 'w_co', 'g_ff_pre', 'g_ff_post', 'w_ff1', 'w_ff2', 'loss_target', 'm_rel_bias', 'm_g_mix_pre', 'm_g_mix_post', 'm_w_in', 'm_gate_b', 'm_pool_w', 'm_pool_scale', 'm_a_re', 'm_a_im', 'm_log_dt', 'm_b_re', 'm_b_im', 'm_c_re', 'm_c_im', 'm_d_skip', 'm_w_glu', 'm_b_glu', 'm_sgu_ln_g', 'm_sgu_ln_b', 'm_w_s', 'm_b_s', 'm_w_up', 'm_w_out', 'm_g_x_pre', 'm_g_x_post', 'm_g_mem', 'm_w_cq', 'm_w_ckv', 'm_w_co', 'm_g_ff_pre', 'm_g_ff_post', 'm_w_ff1', 'm_w_ff2', 'v_rel_bias', 'v_g_mix_pre', 'v_g_mix_post', 'v_w_in', 'v_gate_b', 'v_pool_w', 'v_pool_scale', 'v_a_re', 'v_a_im', 'v_log_dt', 'v_b_re', 'v_b_im', 'v_c_re', 'v_c_im', 'v_d_skip', 'v_w_glu', 'v_b_glu', 'v_sgu_ln_g', 'v_sgu_ln_b', 'v_w_s', 'v_b_s', 'v_w_up', 'v_w_out', 'v_g_x_pre', 'v_g_x_post', 'v_g_mem', 'v_w_cq', 'v_w_ckv', 'v_w_co', 'v_g_ff_pre', 'v_g_ff_post', 'v_w_ff1', 'v_w_ff2']
TWIN_OUTPUTS = ['loss', 'grad_x', 'grad_rel_bias', 'grad_g_mix_pre', 'grad_g_mix_post', 'grad_w_in', 'grad_gate_b', 'grad_pool_w', 'grad_pool_scale', 'grad_a_re', 'grad_a_im', 'grad_log_dt', 'grad_b_re', 'grad_b_im', 'grad_c_re', 'grad_c_im', 'grad_d_skip', 'grad_w_glu', 'grad_b_glu', 'grad_sgu_ln_g', 'grad_sgu_ln_b', 'grad_w_s', 'grad_b_s', 'grad_w_up', 'grad_w_out', 'grad_g_x_pre', 'grad_g_x_post', 'grad_g_mem', 'grad_w_cq', 'grad_w_ckv', 'grad_w_co', 'grad_g_ff_pre', 'grad_g_ff_post', 'grad_w_ff1', 'grad_w_ff2', 'delta_rel_bias', 'delta_g_mix_pre', 'delta_g_mix_post', 'delta_w_in', 'delta_gate_b', 'delta_pool_w', 'delta_pool_scale', 'delta_a_re', 'delta_a_im', 'delta_log_dt', 'delta_b_re', 'delta_b_im', 'delta_c_re', 'delta_c_im', 'delta_d_skip', 'delta_w_glu', 'delta_b_glu', 'delta_sgu_ln_g', 'delta_sgu_ln_b', 'delta_w_s', 'delta_b_s', 'delta_w_up', 'delta_w_out', 'delta_g_x_pre', 'delta_g_x_post', 'delta_g_mem', 'delta_w_cq', 'delta_w_ckv', 'delta_w_co', 'delta_g_ff_pre', 'delta_g_ff_post', 'delta_w_ff1', 'delta_w_ff2', 'new_m_rel_bias', 'new_m_g_mix_pre', 'new_m_g_mix_post', 'new_m_w_in', 'new_m_gate_b', 'new_m_pool_w', 'new_m_pool_scale', 'new_m_a_re', 'new_m_a_im', 'new_m_log_dt', 'new_m_b_re', 'new_m_b_im', 'new_m_c_re', 'new_m_c_im', 'new_m_d_skip', 'new_m_w_glu', 'new_m_b_glu', 'new_m_sgu_ln_g', 'new_m_sgu_ln_b', 'new_m_w_s', 'new_m_b_s', 'new_m_w_up', 'new_m_w_out', 'new_m_g_x_pre', 'new_m_g_x_post', 'new_m_g_mem', 'new_m_w_cq', 'new_m_w_ckv', 'new_m_w_co', 'new_m_g_ff_pre', 'new_m_g_ff_post', 'new_m_w_ff1', 'new_m_w_ff2', 'new_v_rel_bias', 'new_v_g_mix_pre', 'new_v_g_mix_post', 'new_v_w_in', 'new_v_gate_b', 'new_v_pool_w', 'new_v_pool_scale', 'new_v_a_re', 'new_v_a_im', 'new_v_log_dt', 'new_v_b_re', 'new_v_b_im', 'new_v_c_re', 'new_v_c_im', 'new_v_d_skip', 'new_v_w_glu', 'new_v_b_glu', 'new_v_sgu_ln_g', 'new_v_sgu_ln_b', 'new_v_w_s', 'new_v_b_s', 'new_v_w_up', 'new_v_w_out', 'new_v_g_x_pre', 'new_v_g_x_post', 'new_v_g_mem', 'new_v_w_cq', 'new_v_w_ckv', 'new_v_w_co', 'new_v_g_ff_pre', 'new_v_g_ff_post', 'new_v_w_ff1', 'new_v_w_ff2']
TWIN_LEAF_KINDS = {'loss': 'loss', 'grad_x': 'grad_x', 'grad_rel_bias': 'grad_w', 'grad_g_mix_pre': 'grad_w', 'grad_g_mix_post': 'grad_w', 'grad_w_in': 'grad_w', 'grad_gate_b': 'grad_w', 'grad_pool_w': 'grad_w', 'grad_pool_scale': 'grad_w', 'grad_a_re': 'grad_w', 'grad_a_im': 'grad_w', 'grad_log_dt': 'grad_w', 'grad_b_re': 'grad_w', 'grad_b_im': 'grad_w', 'grad_c_re': 'grad_w', 'grad_c_im': 'grad_w', 'grad_d_skip': 'grad_w', 'grad_w_glu': 'grad_w', 'grad_b_glu': 'grad_w', 'grad_sgu_ln_g': 'grad_w', 'grad_sgu_ln_b': 'grad_w', 'grad_w_s': 'grad_w', 'grad_b_s': 'grad_w', 'grad_w_up': 'grad_w', 'grad_w_out': 'grad_w', 'grad_g_x_pre': 'grad_w', 'grad_g_x_post': 'grad_w', 'grad_g_mem': 'grad_w', 'grad_w_cq': 'grad_w', 'grad_w_ckv': 'grad_w', 'grad_w_co': 'grad_w', 'grad_g_ff_pre': 'grad_w', 'grad_g_ff_post': 'grad_w', 'grad_w_ff1': 'grad_w', 'grad_w_ff2': 'grad_w', 'delta_rel_bias': 'delta_w', 'delta_g_mix_pre': 'delta_w', 'delta_g_mix_post': 'delta_w', 'delta_w_in': 'delta_w', 'delta_gate_b': 'delta_w', 'delta_pool_w': 'delta_w', 'delta_pool_scale': 'delta_w', 'delta_a_re': 'delta_w', 'delta_a_im': 'delta_w', 'delta_log_dt': 'delta_w', 'delta_b_re': 'delta_w', 'delta_b_im': 'delta_w', 'delta_c_re': 'delta_w', 'delta_c_im': 'delta_w', 'delta_d_skip': 'delta_w', 'delta_w_glu': 'delta_w', 'delta_b_glu': 'delta_w', 'delta_sgu_ln_g': 'delta_w', 'delta_sgu_ln_b': 'delta_w', 'delta_w_s': 'delta_w', 'delta_b_s': 'delta_w', 'delta_w_up': 'delta_w', 'delta_w_out': 'delta_w', 'delta_g_x_pre': 'delta_w', 'delta_g_x_post': 'delta_w', 'delta_g_mem': 'delta_w', 'delta_w_cq': 'delta_w', 'delta_w_ckv': 'delta_w', 'delta_w_co': 'delta_w', 'delta_g_ff_pre': 'delta_w', 'delta_g_ff_post': 'delta_w', 'delta_w_ff1': 'delta_w', 'delta_w_ff2': 'delta_w', 'new_m_rel_bias': 'new_m', 'new_m_g_mix_pre': 'new_m', 'new_m_g_mix_post': 'new_m', 'new_m_w_in': 'new_m', 'new_m_gate_b': 'new_m', 'new_m_pool_w': 'new_m', 'new_m_pool_scale': 'new_m', 'new_m_a_re': 'new_m', 'new_m_a_im': 'new_m', 'new_m_log_dt': 'new_m', 'new_m_b_re': 'new_m', 'new_m_b_im': 'new_m', 'new_m_c_re': 'new_m', 'new_m_c_im': 'new_m', 'new_m_d_skip': 'new_m', 'new_m_w_glu': 'new_m', 'new_m_b_glu': 'new_m', 'new_m_sgu_ln_g': 'new_m', 'new_m_sgu_ln_b': 'new_m', 'new_m_w_s': 'new_m', 'new_m_b_s': 'new_m', 'new_m_w_up': 'new_m', 'new_m_w_out': 'new_m', 'new_m_g_x_pre': 'new_m', 'new_m_g_x_post': 'new_m', 'new_m_g_mem': 'new_m', 'new_m_w_cq': 'new_m', 'new_m_w_ckv': 'new_m', 'new_m_w_co': 'new_m', 'new_m_g_ff_pre': 'new_m', 'new_m_g_ff_post': 'new_m', 'new_m_w_ff1': 'new_m', 'new_m_w_ff2': 'new_m', 'new_v_rel_bias': 'new_v', 'new_v_g_mix_pre': 'new_v', 'new_v_g_mix_post': 'new_v', 'new_v_w_in': 'new_v', 'new_v_gate_b': 'new_v', 'new_v_pool_w': 'new_v', 'new_v_pool_scale': 'new_v', 'new_v_a_re': 'new_v', 'new_v_a_im': 'new_v', 'new_v_log_dt': 'new_v', 'new_v_b_re': 'new_v', 'new_v_b_im': 'new_v', 'new_v_c_re': 'new_v', 'new_v_c_im': 'new_v', 'new_v_d_skip': 'new_v', 'new_v_w_glu': 'new_v', 'new_v_b_glu': 'new_v', 'new_v_sgu_ln_g': 'new_v', 'new_v_sgu_ln_b': 'new_v', 'new_v_w_s': 'new_v', 'new_v_b_s': 'new_v', 'new_v_w_up': 'new_v', 'new_v_w_out': 'new_v', 'new_v_g_x_pre': 'new_v', 'new_v_g_x_post': 'new_v', 'new_v_g_mem': 'new_v', 'new_v_w_cq': 'new_v', 'new_v_w_ckv': 'new_v', 'new_v_w_co': 'new_v', 'new_v_g_ff_pre': 'new_v', 'new_v_g_ff_post': 'new_v', 'new_v_w_ff1': 'new_v', 'new_v_w_ff2': 'new_v'}


def _forward(args):
    return _fwd_reference(*[args[k] for k in FWD_PARAMS])


def _output_shape():
    out = _jax.eval_shape(lambda: _forward(_fwd_setup_inputs(0)))
    return out.shape, out.dtype

N_MICROBATCH = 1
ADAM_LR = 0.001
ADAM_B1 = 0.9
ADAM_B2 = 0.999
ADAM_EPS = 1e-08
ADAM_WD = 0.01
ADAM_STEP = 10
PER_EXAMPLE_BATCH_AXIS = {'x': 0, 'mem': 0, 'loss_target': 0}
SHARED_INPUTS = []
_WEIGHT_DTYPES = {'rel_bias': _jnp.float32, 'g_mix_pre': _jnp.float32, 'g_mix_post': _jnp.float32, 'w_in': _jnp.float32, 'gate_b': _jnp.float32, 'pool_w': _jnp.float32, 'pool_scale': _jnp.float32, 'a_re': _jnp.float32, 'a_im': _jnp.float32, 'log_dt': _jnp.float32, 'b_re': _jnp.float32, 'b_im': _jnp.float32, 'c_re': _jnp.float32, 'c_im': _jnp.float32, 'd_skip': _jnp.float32, 'w_glu': _jnp.float32, 'b_glu': _jnp.float32, 'sgu_ln_g': _jnp.float32, 'sgu_ln_b': _jnp.float32, 'w_s': _jnp.float32, 'b_s': _jnp.float32, 'w_up': _jnp.float32, 'w_out': _jnp.float32, 'g_x_pre': _jnp.float32, 'g_x_post': _jnp.float32, 'g_mem': _jnp.float32, 'w_cq': _jnp.float32, 'w_ckv': _jnp.float32, 'w_co': _jnp.float32, 'g_ff_pre': _jnp.float32, 'g_ff_post': _jnp.float32, 'w_ff1': _jnp.float32, 'w_ff2': _jnp.float32}
MOMENT_SCALE = {'rel_bias': 3.443234e+00, 'g_mix_pre': 8.323104e+00, 'g_mix_post': 3.513252e+01, 'w_in': 2.530509e+00, 'gate_b': 2.883899e+00, 'pool_w': 2.373889e+00, 'pool_scale': 2.455461e+00, 'a_re': 2.171997e-01, 'a_im': 2.596588e-01, 'log_dt': 3.084435e+01, 'b_re': 2.085022e-01, 'b_im': 2.084766e-01, 'c_re': 4.561375e-01, 'c_im': 4.540814e-01, 'd_skip': 9.984065e+00, 'w_glu': 1.454583e+00, 'b_glu': 4.226072e+00, 'sgu_ln_g': 7.893853e-01, 'sgu_ln_b': 9.302952e-01, 'w_s': 6.772520e-01, 'b_s': 1.177138e+00, 'w_up': 7.017714e+00, 'w_out': 1.388951e+01, 'g_x_pre': 7.603350e+00, 'g_x_post': 4.021909e+01, 'g_mem': 2.252926e+01, 'w_cq': 1.050371e+01, 'w_ckv': 2.178884e+01, 'w_co': 2.059761e+01, 'g_ff_pre': 1.110267e+01, 'g_ff_post': 3.944333e+01, 'w_ff1': 5.471101e+00, 'w_ff2': 2.033397e+01}


def _to_microbatches(a, axis):
    t = _jnp.moveaxis(a, axis, 0)
    t = t.reshape((N_MICROBATCH, t.shape[0] // N_MICROBATCH) + t.shape[1:])
    return _jnp.moveaxis(t, 1, axis + 1)


def setup_inputs(seed: int = 0) -> dict:
    inp = _fwd_setup_inputs(seed)
    key = _jax.random.fold_in(_jax.random.key(seed), 7919)
    shape, _ = _output_shape()
    out = dict(inp)
    out["loss_target"] = _jax.random.normal(_jax.random.fold_in(key, 0), shape, _jnp.float32)
    for i, name in enumerate(TWIN_WEIGHTS):
        w = inp[name].astype(_jnp.float32)
        if MOMENT_SCALE is None:
            s = _jnp.sqrt(_jnp.mean(_jnp.square(w)) + 1e-30)
        else:
            s = MOMENT_SCALE[name]
        km, kv = _jax.random.split(_jax.random.fold_in(key, i + 1))
        out[name] = w
        out["m_" + name] = s * _jax.random.normal(km, w.shape, _jnp.float32)
        out["v_" + name] = (s * s) * _jax.random.uniform(kv, w.shape, _jnp.float32, 0.5, 1.5)
    if N_MICROBATCH > 1:
        for name, axis in PER_EXAMPLE_BATCH_AXIS.items():
            out[name] = _to_microbatches(out[name], axis)
    return {'x': out['x'], 'mem': out['mem'], 'rel_bias': out['rel_bias'], 'g_mix_pre': out['g_mix_pre'], 'g_mix_post': out['g_mix_post'], 'w_in': out['w_in'], 'gate_b': out['gate_b'], 'pool_w': out['pool_w'], 'pool_scale': out['pool_scale'], 'a_re': out['a_re'], 'a_im': out['a_im'], 'log_dt': out['log_dt'], 'b_re': out['b_re'], 'b_im': out['b_im'], 'c_re': out['c_re'], 'c_im': out['c_im'], 'd_skip': out['d_skip'], 'w_glu': out['w_glu'], 'b_glu': out['b_glu'], 'sgu_ln_g': out['sgu_ln_g'], 'sgu_ln_b': out['sgu_ln_b'], 'w_s': out['w_s'], 'b_s': out['b_s'], 'w_up': out['w_up'], 'w_out': out['w_out'], 'g_x_pre': out['g_x_pre'], 'g_x_post': out['g_x_post'], 'g_mem': out['g_mem'], 'w_cq': out['w_cq'], 'w_ckv': out['w_ckv'], 'w_co': out['w_co'], 'g_ff_pre': out['g_ff_pre'], 'g_ff_post': out['g_ff_post'], 'w_ff1': out['w_ff1'], 'w_ff2': out['w_ff2'], 'loss_target': out['loss_target'], 'm_rel_bias': out['m_rel_bias'], 'm_g_mix_pre': out['m_g_mix_pre'], 'm_g_mix_post': out['m_g_mix_post'], 'm_w_in': out['m_w_in'], 'm_gate_b': out['m_gate_b'], 'm_pool_w': out['m_pool_w'], 'm_pool_scale': out['m_pool_scale'], 'm_a_re': out['m_a_re'], 'm_a_im': out['m_a_im'], 'm_log_dt': out['m_log_dt'], 'm_b_re': out['m_b_re'], 'm_b_im': out['m_b_im'], 'm_c_re': out['m_c_re'], 'm_c_im': out['m_c_im'], 'm_d_skip': out['m_d_skip'], 'm_w_glu': out['m_w_glu'], 'm_b_glu': out['m_b_glu'], 'm_sgu_ln_g': out['m_sgu_ln_g'], 'm_sgu_ln_b': out['m_sgu_ln_b'], 'm_w_s': out['m_w_s'], 'm_b_s': out['m_b_s'], 'm_w_up': out['m_w_up'], 'm_w_out': out['m_w_out'], 'm_g_x_pre': out['m_g_x_pre'], 'm_g_x_post': out['m_g_x_post'], 'm_g_mem': out['m_g_mem'], 'm_w_cq': out['m_w_cq'], 'm_w_ckv': out['m_w_ckv'], 'm_w_co': out['m_w_co'], 'm_g_ff_pre': out['m_g_ff_pre'], 'm_g_ff_post': out['m_g_ff_post'], 'm_w_ff1': out['m_w_ff1'], 'm_w_ff2': out['m_w_ff2'], 'v_rel_bias': out['v_rel_bias'], 'v_g_mix_pre': out['v_g_mix_pre'], 'v_g_mix_post': out['v_g_mix_post'], 'v_w_in': out['v_w_in'], 'v_gate_b': out['v_gate_b'], 'v_pool_w': out['v_pool_w'], 'v_pool_scale': out['v_pool_scale'], 'v_a_re': out['v_a_re'], 'v_a_im': out['v_a_im'], 'v_log_dt': out['v_log_dt'], 'v_b_re': out['v_b_re'], 'v_b_im': out['v_b_im'], 'v_c_re': out['v_c_re'], 'v_c_im': out['v_c_im'], 'v_d_skip': out['v_d_skip'], 'v_w_glu': out['v_w_glu'], 'v_b_glu': out['v_b_glu'], 'v_sgu_ln_g': out['v_sgu_ln_g'], 'v_sgu_ln_b': out['v_sgu_ln_b'], 'v_w_s': out['v_w_s'], 'v_b_s': out['v_b_s'], 'v_w_up': out['v_w_up'], 'v_w_out': out['v_w_out'], 'v_g_x_pre': out['v_g_x_pre'], 'v_g_x_post': out['v_g_x_post'], 'v_g_mem': out['v_g_mem'], 'v_w_cq': out['v_w_cq'], 'v_w_ckv': out['v_w_ckv'], 'v_w_co': out['v_w_co'], 'v_g_ff_pre': out['v_g_ff_pre'], 'v_g_ff_post': out['v_g_ff_post'], 'v_w_ff1': out['v_w_ff1'], 'v_w_ff2': out['v_w_ff2']}


def _loss(weights, diff, rest, loss_target):
    with _jax.named_scope("forward"):
        args = {**rest, TWIN_DIFF_INPUT: diff, **{k: w.astype(_WEIGHT_DTYPES[k]) for k, w in weights.items()}}
        y = _forward(args)
    with _jax.named_scope("loss_head"):
        err = _jnp.square(y.astype(_jnp.float32) - loss_target)
        return 0.5 * _jnp.sum(_jnp.mean(err, axis=-1)) if err.ndim else 0.5 * err


def _adamw(w, g, m, v):
    m = ADAM_B1 * m + (1.0 - ADAM_B1) * g
    v = ADAM_B2 * v + (1.0 - ADAM_B2) * _jnp.square(g)
    m_hat = m / (1.0 - ADAM_B1 ** ADAM_STEP)
    v_hat = v / (1.0 - ADAM_B2 ** ADAM_STEP)
    delta = -ADAM_LR * (m_hat / (_jnp.sqrt(v_hat) + ADAM_EPS) + ADAM_WD * w)
    return delta, m, v


def reference(x, mem, rel_bias, g_mix_pre, g_mix_post, w_in, gate_b, pool_w, pool_scale, a_re, a_im, log_dt, b_re, b_im, c_re, c_im, d_skip, w_glu, b_glu, sgu_ln_g, sgu_ln_b, w_s, b_s, w_up, w_out, g_x_pre, g_x_post, g_mem, w_cq, w_ckv, w_co, g_ff_pre, g_ff_post, w_ff1, w_ff2, loss_target, m_rel_bias, m_g_mix_pre, m_g_mix_post, m_w_in, m_gate_b, m_pool_w, m_pool_scale, m_a_re, m_a_im, m_log_dt, m_b_re, m_b_im, m_c_re, m_c_im, m_d_skip, m_w_glu, m_b_glu, m_sgu_ln_g, m_sgu_ln_b, m_w_s, m_b_s, m_w_up, m_w_out, m_g_x_pre, m_g_x_post, m_g_mem, m_w_cq, m_w_ckv, m_w_co, m_g_ff_pre, m_g_ff_post, m_w_ff1, m_w_ff2, v_rel_bias, v_g_mix_pre, v_g_mix_post, v_w_in, v_gate_b, v_pool_w, v_pool_scale, v_a_re, v_a_im, v_log_dt, v_b_re, v_b_im, v_c_re, v_c_im, v_d_skip, v_w_glu, v_b_glu, v_sgu_ln_g, v_sgu_ln_b, v_w_s, v_b_s, v_w_up, v_w_out, v_g_x_pre, v_g_x_post, v_g_mem, v_w_cq, v_w_ckv, v_w_co, v_g_ff_pre, v_g_ff_post, v_w_ff1, v_w_ff2):
    given = dict(x=x, mem=mem, rel_bias=rel_bias, g_mix_pre=g_mix_pre, g_mix_post=g_mix_post, w_in=w_in, gate_b=gate_b, pool_w=pool_w, pool_scale=pool_scale, a_re=a_re, a_im=a_im, log_dt=log_dt, b_re=b_re, b_im=b_im, c_re=c_re, c_im=c_im, d_skip=d_skip, w_glu=w_glu, b_glu=b_glu, sgu_ln_g=sgu_ln_g, sgu_ln_b=sgu_ln_b, w_s=w_s, b_s=b_s, w_up=w_up, w_out=w_out, g_x_pre=g_x_pre, g_x_post=g_x_post, g_mem=g_mem, w_cq=w_cq, w_ckv=w_ckv, w_co=w_co, g_ff_pre=g_ff_pre, g_ff_post=g_ff_post, w_ff1=w_ff1, w_ff2=w_ff2, loss_target=loss_target, m_rel_bias=m_rel_bias, m_g_mix_pre=m_g_mix_pre, m_g_mix_post=m_g_mix_post, m_w_in=m_w_in, m_gate_b=m_gate_b, m_pool_w=m_pool_w, m_pool_scale=m_pool_scale, m_a_re=m_a_re, m_a_im=m_a_im, m_log_dt=m_log_dt, m_b_re=m_b_re, m_b_im=m_b_im, m_c_re=m_c_re, m_c_im=m_c_im, m_d_skip=m_d_skip, m_w_glu=m_w_glu, m_b_glu=m_b_glu, m_sgu_ln_g=m_sgu_ln_g, m_sgu_ln_b=m_sgu_ln_b, m_w_s=m_w_s, m_b_s=m_b_s, m_w_up=m_w_up, m_w_out=m_w_out, m_g_x_pre=m_g_x_pre, m_g_x_post=m_g_x_post, m_g_mem=m_g_mem, m_w_cq=m_w_cq, m_w_ckv=m_w_ckv, m_w_co=m_w_co, m_g_ff_pre=m_g_ff_pre, m_g_ff_post=m_g_ff_post, m_w_ff1=m_w_ff1, m_w_ff2=m_w_ff2, v_rel_bias=v_rel_bias, v_g_mix_pre=v_g_mix_pre, v_g_mix_post=v_g_mix_post, v_w_in=v_w_in, v_gate_b=v_gate_b, v_pool_w=v_pool_w, v_pool_scale=v_pool_scale, v_a_re=v_a_re, v_a_im=v_a_im, v_log_dt=v_log_dt, v_b_re=v_b_re, v_b_im=v_b_im, v_c_re=v_c_re, v_c_im=v_c_im, v_d_skip=v_d_skip, v_w_glu=v_w_glu, v_b_glu=v_b_glu, v_sgu_ln_g=v_sgu_ln_g, v_sgu_ln_b=v_sgu_ln_b, v_w_s=v_w_s, v_b_s=v_b_s, v_w_up=v_w_up, v_w_out=v_w_out, v_g_x_pre=v_g_x_pre, v_g_x_post=v_g_x_post, v_g_mem=v_g_mem, v_w_cq=v_w_cq, v_w_ckv=v_w_ckv, v_w_co=v_w_co, v_g_ff_pre=v_g_ff_pre, v_g_ff_post=v_g_ff_post, v_w_ff1=v_w_ff1, v_w_ff2=v_w_ff2)
    weights = {n: given[n] for n in TWIN_WEIGHTS}
    shared = {n: given[n] for n in SHARED_INPUTS}
    per_example = {n: given[n] for n in ['x', 'mem']}
    grad_fn = _jax.value_and_grad(_loss, argnums=(0, 1))

    def one_microbatch(ex, loss_target):
        ex = dict(ex)
        diff = ex.pop(TWIN_DIFF_INPUT)
        return grad_fn(weights, diff, {**shared, **ex}, loss_target)

    if N_MICROBATCH == 1:
        loss, (grad_w, grad_x) = one_microbatch(per_example, given["loss_target"])
    else:
        def body(carry, xs):
            loss_sum, grad_sum = carry
            l_k, (gw_k, gx_k) = one_microbatch(xs[0], xs[1])
            with _jax.named_scope("update"):
                return (loss_sum + l_k, _jax.tree.map(_jnp.add, grad_sum, gw_k)), gx_k

        init = (_jnp.zeros((), _jnp.float32), _jax.tree.map(_jnp.zeros_like, weights))
        (loss, grad_w), grad_x = _jax.lax.scan(body, init, (per_example, given["loss_target"]))
    with _jax.named_scope("update"):
        delta_w, new_m, new_v = {}, {}, {}
        for n in TWIN_WEIGHTS:
            delta_w[n], new_m[n], new_v[n] = _adamw(weights[n], grad_w[n], given["m_" + n], given["v_" + n])
    return (loss, grad_x, *[grad_w[n] for n in TWIN_WEIGHTS], *[delta_w[n] for n in TWIN_WEIGHTS],
            *[new_m[n] for n in TWIN_WEIGHTS], *[new_v[n] for n in TWIN_WEIGHTS])
```

```python
import functools
import math

import numpy as np
import jax
import jax.numpy as jnp
from jax import lax
from jax.experimental import pallas as pl
from jax.experimental.pallas import tpu as pltpu

F32 = jnp.float32
BF16 = jnp.bfloat16
MXU_DTYPE = jnp.bfloat16
MESH_ID = pl.DeviceIdType.MESH
VMEM_LIMIT_BYTES = 56 * 1024 * 1024

D_MODEL = 1024
DEPTH = 4
N_MEM = 256
W_BRANCH = 512
POOL_WINDOWS = (2, 4, 8, 16)
POOL_HALO = 16
DIL_GROUPS = ((128, 1), (512, 4), (2048, 16))
BAND = 128
ATT_HEADS = 8
ATT_HEAD_DIM = 64
SSM_GROUP = 16
SSM_GROUPS = 32
SSM_STATE = 64
SSM_COLS = SSM_GROUPS * SSM_STATE
SSM_T = 512
SGU_CHUNK = 128
X_HEADS = 4
X_HEAD_DIM = 128
D_FF = 4096
REL_BUCKETS = 32
REL_MAX_DIST = 2048
EPS = 1e-6
NEG_INF = -1e30
OFF_POOL = 0
OFF_ATT = 512
OFF_SSM = OFF_ATT + 9 * W_BRANCH
OFF_SGU = OFF_SSM + W_BRANCH
OFF_GATE = OFF_SGU + 2 * W_BRANCH
IN_WIDTH = OFF_GATE + 4 * D_MODEL

ADAM_LR = 0.001
ADAM_B1 = 0.9
ADAM_B2 = 0.999
ADAM_EPS = 1e-08
ADAM_WD = 0.01
ADAM_STEP = 10

GELU_C = math.sqrt(2.0 / math.pi)

WEIGHT_NAMES = ['rel_bias', 'g_mix_pre', 'g_mix_post', 'w_in', 'gate_b', 'pool_w', 'pool_scale', 'a_re', 'a_im',
                'log_dt', 'b_re', 'b_im', 'c_re', 'c_im', 'd_skip', 'w_glu', 'b_glu', 'sgu_ln_g', 'sgu_ln_b',
                'w_s', 'b_s', 'w_up', 'w_out', 'g_x_pre', 'g_x_post', 'g_mem', 'w_cq', 'w_ckv', 'w_co',
                'g_ff_pre', 'g_ff_post', 'w_ff1', 'w_ff2']
SHARDED = {
    'w_in': ((D_MODEL, IN_WIDTH), 1),
    'gate_b': ((4, D_MODEL), 1),
    'w_glu': ((W_BRANCH, W_BRANCH), 0),
    'w_up': ((4 * W_BRANCH, D_MODEL), 1),
    'w_out': ((D_MODEL, D_MODEL), 0),
    'w_cq': ((D_MODEL, W_BRANCH), 0),
    'w_ckv': ((D_MODEL, D_MODEL), 0),
    'w_co': ((W_BRANCH, D_MODEL), 1),
    'w_ff1': ((D_MODEL, D_FF), 1),
    'w_ff2': ((D_FF, D_MODEL), 0),
}
SHARDED_NAMES = list(SHARDED)
REPLICATED_NAMES = [n for n in WEIGHT_NAMES if n not in SHARDED]


def _cp(n_axes):
    return pltpu.CompilerParams(dimension_semantics=("arbitrary",) * n_axes, vmem_limit_bytes=VMEM_LIMIT_BYTES)


def _dot(a, b, dims="nn"):
    cd = {"nn": ((1,), (0,)), "nt": ((1,), (1,)), "tn": ((0,), (0,))}[dims]
    return lax.dot_general(a.astype(MXU_DTYPE), b.astype(MXU_DTYPE), (cd, ((), ())), preferred_element_type=F32)


def _gelu(x):
    return 0.5 * x * (1.0 + jnp.tanh(GELU_C * (x + 0.044715 * (x * x * x))))


def _gelu_grad(x):
    t = jnp.tanh(GELU_C * (x + 0.044715 * (x * x * x)))
    return 0.5 * (1.0 + t) + 0.5 * x * (1.0 - t * t) * (GELU_C * (1.0 + 3.0 * 0.044715 * (x * x)))


def _sigmoid(x):
    return 1.0 / (1.0 + jnp.exp(-x))


def mm(a, b, dims, *, tm, tn, tk, out_dtypes, name, extras=(), epi=None):
    if dims == "tn":
        K, M = a.shape
        N = b.shape[1]
    else:
        M, K = a.shape
        N = b.shape[1] if dims == "nn" else b.shape[0]
    tm, tn, tk = min(tm, M), min(tn, N), min(tk, K)
    assert M % tm == 0 and N % tn == 0 and K % tk == 0, (name, M, N, K, tm, tn, tk)
    nk = K // tk
    ne, no = len(extras), len(out_dtypes)
    if epi is None:
        epi = lambda acc: (acc,)
    a_spec = (pl.BlockSpec((tk, tm), lambda i, j, k: (k, i)) if dims == "tn"
              else pl.BlockSpec((tm, tk), lambda i, j, k: (i, k)))
    b_spec = (pl.BlockSpec((tn, tk), lambda i, j, k: (j, k)) if dims == "nt"
              else pl.BlockSpec((tk, tn), lambda i, j, k: (k, j)))
    mn_spec = pl.BlockSpec((tm, tn), lambda i, j, k: (i, j))

    def body(a_ref, b_ref, *rest):
        extra_refs, out_refs = rest[:ne], rest[ne:ne + no]
        part = _dot(a_ref[...], b_ref[...], dims)

        def finish(acc):
            for o_ref, r in zip(out_refs, epi(acc, *[e[...] for e in extra_refs])):
                o_ref[...] = r.astype(o_ref.dtype)

        if nk == 1:
            finish(part)
        else:
            acc_ref = rest[-1]
            k = pl.program_id(2)

            @pl.when(k == 0)
            def _():
                acc_ref[...] = part

            @pl.when(k > 0)
            def _():
                acc_ref[...] += part

            @pl.when(k == nk - 1)
            def _():
                finish(acc_ref[...])

    outs = pl.pallas_call(
        body, name=name, grid=(M // tm, N // tn, nk),
        in_specs=[a_spec, b_spec] + [mn_spec] * ne,
        out_specs=[mn_spec] * no,
        out_shape=[jax.ShapeDtypeStruct((M, N), dt) for dt in out_dtypes],
        scratch_shapes=[pltpu.VMEM((tm, tn), F32)] if nk > 1 else [],
        compiler_params=_cp(3),
    )(a, b, *extras)
    return outs[0] if no == 1 else outs


ROW_TILE = 512


def rms_fwd(x, g, out_dtype, name, res=None):
    M, D = x.shape
    tm = min(ROW_TILE, M)

    def body(x_ref, g_ref, *rest):
        o_ref = rest[-1]
        xf = x_ref[...]
        y = xf * lax.rsqrt(jnp.mean(xf * xf, axis=-1, keepdims=True) + EPS) * g_ref[...]
        if res is not None:
            y = y + rest[0][...]
        o_ref[...] = y.astype(o_ref.dtype)

    row = pl.BlockSpec((tm, D), lambda i: (i, 0))
    return pl.pallas_call(
        body, name=name, grid=(M // tm,),
        in_specs=[row, pl.BlockSpec((1, D), lambda i: (0, 0))] + ([row] if res is not None else []),
        out_specs=row, out_shape=jax.ShapeDtypeStruct((M, D), out_dtype), compiler_params=_cp(1),
    )(x, g, *([res] if res is not None else []))


def rms_bwd(x, g, dy, dx_dtype, name, add=None):
    M, D = x.shape
    tm = min(ROW_TILE, M)

    def body(x_ref, g_ref, dy_ref, *rest):
        dx_ref, dg_ref = rest[-2], rest[-1]
        xf = x_ref[...]
        dyf = dy_ref[...].astype(F32)
        r = lax.rsqrt(jnp.mean(xf * xf, axis=-1, keepdims=True) + EPS)
        xn = xf * r
        dxn = dyf * g_ref[...]
        dx = r * (dxn - xn * jnp.mean(dxn * xn, axis=-1, keepdims=True))
        if add is not None:
            dx = dx + rest[0][...]
        dx_ref[...] = dx.astype(dx_ref.dtype)
        dg = jnp.sum(dyf * xn, axis=0, keepdims=True)

        @pl.when(pl.program_id(0) == 0)
        def _():
            dg_ref[...] = dg

        @pl.when(pl.program_id(0) > 0)
        def _():
            dg_ref[...] += dg

    row = pl.BlockSpec((tm, D), lambda i: (i, 0))
    vec = pl.BlockSpec((1, D), lambda i: (0, 0))
    return pl.pallas_call(
        body, name=name, grid=(M // tm,),
        in_specs=[row, vec, row] + ([row] if add is not None else []),
        out_specs=[row, vec],
        out_shape=[jax.ShapeDtypeStruct((M, D), dx_dtype), jax.ShapeDtypeStruct((1, D), F32)],
        compiler_params=_cp(1),
    )(x, g, dy, *([add] if add is not None else []))


def loss_and_grad(y, target):
    M, D = y.shape
    tm = ROW_TILE

    def body(y_ref, t_ref, part_ref, dy_ref):
        e = y_ref[...] - t_ref[...]
        dy_ref[...] = e / D
        part_ref[...] = jnp.broadcast_to(0.5 * jnp.sum(jnp.mean(e * e, axis=-1, keepdims=True), axis=0, keepdims=True),
                                         (8, 128))

    row = pl.BlockSpec((tm, D), lambda i: (i, 0))
    part, dy = pl.pallas_call(
        body, name="loss", grid=(M // tm,), in_specs=[row, row],
        out_specs=[pl.BlockSpec((8, 128), lambda i: (i, 0)), row],
        out_shape=[jax.ShapeDtypeStruct((8 * (M // tm), 128), F32), jax.ShapeDtypeStruct((M, D), F32)],
        compiler_params=_cp(1),
    )(y, target)
    return jnp.sum(part[::8, 0]), dy


POOL_ROWS = 512


def _pool_window_sum(xw, gi, roll_of):
    s1 = xw + pltpu.roll(xw, roll_of(1), 0)
    s2 = s1 + pltpu.roll(s1, roll_of(2), 0)
    s3 = s2 + pltpu.roll(s2, roll_of(4), 0)
    s4 = s3 + pltpu.roll(s3, roll_of(8), 0)
    return jnp.where(gi == 0, s1, jnp.where(gi == 1, s2, jnp.where(gi == 2, s3, s4)))


def _pool_cnt(i, gi):
    rows = lax.broadcasted_iota(jnp.int32, (POOL_ROWS, 128), 0) + i * POOL_ROWS
    w = jnp.where(gi == 0, 2, jnp.where(gi == 1, 4, jnp.where(gi == 2, 8, 16)))
    return jnp.minimum(rows + 1, w).astype(F32)


def pool_fwd(proj, pool_w, scale):
    S = proj.shape[0]
    nchunk = S // POOL_ROWS
    slab = POOL_ROWS + POOL_HALO

    def body(x_ref, w_ref, sc_ref, o_ref, pad_ref):
        gi = pl.program_id(0)
        pad_ref[0:POOL_HALO, :] = jnp.zeros((POOL_HALO, 128), F32)
        pad_ref[POOL_HALO:, :] = x_ref[...]
        for i in range(nchunk):
            xw = pad_ref[i * POOL_ROWS:i * POOL_ROWS + slab, :]
            ssum = _pool_window_sum(xw, gi, lambda d: d)[POOL_HALO:, :]
            p = ssum / _pool_cnt(i, gi) - xw[POOL_HALO:, :]
            o_ref[i * POOL_ROWS:(i + 1) * POOL_ROWS, :] = (_dot(p, w_ref[...]) * sc_ref[...]).astype(o_ref.dtype)

    return pl.pallas_call(
        body, name="pool_fwd", grid=(4,),
        in_specs=[pl.BlockSpec((S, 128), lambda g: (0, OFF_POOL // 128 + g)),
                  pl.BlockSpec((None, 128, 128), lambda g: (g, 0, 0)),
                  pl.BlockSpec((1, 128), lambda g: (0, g))],
        out_specs=pl.BlockSpec((S, 128), lambda g: (0, g)),
        out_shape=jax.ShapeDtypeStruct((S, W_BRANCH), BF16),
        scratch_shapes=[pltpu.VMEM((S + POOL_HALO, 128), F32)],
        compiler_params=_cp(1),
    )(proj, pool_w, scale)


def pool_bwd(proj, pool_w, scale, dy):
    S = proj.shape[0]
    nchunk = S // POOL_ROWS
    slab = POOL_ROWS + POOL_HALO

    def body(x_ref, w_ref, sc_ref, dy_ref, dx_ref, dw_ref, dsc_ref, pad_ref, pad2_ref, dp_ref):
        gi = pl.program_id(0)
        pad_ref[0:POOL_HALO, :] = jnp.zeros((POOL_HALO, 128), F32)
        pad_ref[POOL_HALO:, :] = x_ref[...]
        pad2_ref[S:, :] = jnp.zeros((POOL_HALO, 128), F32)
        dw = jnp.zeros((128, 128), F32)
        dsc = jnp.zeros((1, 128), F32)
        for i in range(nchunk):
            xw = pad_ref[i * POOL_ROWS:i * POOL_ROWS + slab, :]
            cnt = _pool_cnt(i, gi)
            p = _pool_window_sum(xw, gi, lambda d: d)[POOL_HALO:, :] / cnt - xw[POOL_HALO:, :]
            dyc = dy_ref[i * POOL_ROWS:(i + 1) * POOL_ROWS, :]
            dsc = dsc + jnp.sum(dyc * _dot(p, w_ref[...]), axis=0, keepdims=True)
            dys = dyc * sc_ref[...]
            dw = dw + _dot(p, dys, "tn")
            dp = _dot(dys, w_ref[...], "nt")
            dp_ref[i * POOL_ROWS:(i + 1) * POOL_ROWS, :] = dp
            pad2_ref[i * POOL_ROWS:(i + 1) * POOL_ROWS, :] = dp / cnt
        dw_ref[...] = dw
        dsc_ref[...] = dsc
        for i in range(nchunk):
            xw = pad2_ref[i * POOL_ROWS:i * POOL_ROWS + slab, :]
            fsum = _pool_window_sum(xw, gi, lambda d: slab - d)[:POOL_ROWS, :]
            rows = slice(i * POOL_ROWS, (i + 1) * POOL_ROWS)
            dx_ref[rows, :] = (fsum - dp_ref[rows, :]).astype(dx_ref.dtype)

    return pl.pallas_call(
        body, name="pool_bwd", grid=(4,),
        in_specs=[pl.BlockSpec((S, 128), lambda g: (0, OFF_POOL // 128 + g)),
                  pl.BlockSpec((None, 128, 128), lambda g: (g, 0, 0)),
                  pl.BlockSpec((1, 128), lambda g: (0, g)),
                  pl.BlockSpec((S, 128), lambda g: (0, g))],
        out_specs=[pl.BlockSpec((S, 128), lambda g: (0, g)),
                   pl.BlockSpec((None, 128, 128), lambda g: (g, 0, 0)),
                   pl.BlockSpec((1, 128), lambda g: (0, g))],
        out_shape=[jax.ShapeDtypeStruct((S, W_BRANCH), BF16), jax.ShapeDtypeStruct((4, 128, 128), F32),
                   jax.ShapeDtypeStruct((1, W_BRANCH), F32)],
        scratch_shapes=[pltpu.VMEM((S + POOL_HALO, 128), F32), pltpu.VMEM((S + POOL_HALO, 128), F32),
                        pltpu.VMEM((S, 128), F32)],
        compiler_params=_cp(1),
    )(proj, pool_w, scale, dy)


def _t5_bucket(n):
    exact = REL_BUCKETS // 2
    nf = np.maximum(n, 1).astype(np.float32)
    large = exact + (np.log(nf / exact) / np.log(REL_MAX_DIST / exact) * (REL_BUCKETS - exact)).astype(np.int32)
    large = np.minimum(large, REL_BUCKETS - 1)
    return np.where(n < exact, n, large).astype(np.int32)


def _band_onehot(dil):
    i = np.arange(BAND)[:, None]
    kk = np.arange(2 * BAND)[None, :]
    dist = BAND + i - kk
    local = (dist >= 0) & (dist <= BAND)
    bucket = _t5_bucket(np.clip(dist, 0, BAND) * dil)
    onehot = (bucket.reshape(-1, 1) == np.arange(REL_BUCKETS)[None, :]).astype(np.float32)
    return onehot, local


def att_bias(rel_bias, g, dil):
    onehot, local = _band_onehot(dil)
    tab = jnp.dot(jnp.asarray(onehot), rel_bias[:, g * ATT_HEADS:(g + 1) * ATT_HEADS], precision=lax.Precision.HIGHEST)
    bias = tab.reshape(BAND, 2 * BAND, ATT_HEADS).transpose(2, 0, 1)
    return jnp.where(jnp.asarray(local)[None], bias, NEG_INF)


def att_bias_grad(dbias, dil):
    onehot, _ = _band_onehot(dil)
    flat = dbias.transpose(1, 2, 0).reshape(BAND * 2 * BAND, ATT_HEADS)
    return jnp.dot(jnp.asarray(onehot).T, flat, precision=lax.Precision.HIGHEST)


def _rows(r, d):
    return pl.ds(r, BAND, stride=d) if d > 1 else pl.ds(0, BAND)


def _head_lanes():
    return lax.broadcasted_iota(jnp.int32, (BAND, 128), 1) < ATT_HEAD_DIM


def _att_cols(part, g, hp):
    return (OFF_ATT + part * 3 * W_BRANCH + g * W_BRANCH) // 128 + hp


def att_fwd(proj, bias, g, d):
    S = proj.shape[0]
    ch = BAND * d
    nb = S // ch

    def body(q_ref, kc_ref, kp_ref, vc_ref, vp_ref, b_ref, o_ref, l_ref):
        n = pl.program_id(1)
        head0 = _head_lanes()
        first = jnp.logical_and(lax.broadcasted_iota(jnp.int32, (BAND, 2 * BAND), 1) < BAND, n == 0)
        for r in range(d):
            rows = _rows(r, d)
            q = q_ref[rows, :]
            k = jnp.concatenate([kp_ref[rows, :], kc_ref[rows, :]], axis=0).astype(MXU_DTYPE)
            v = jnp.concatenate([vp_ref[rows, :], vc_ref[rows, :]], axis=0).astype(MXU_DTYPE)
            o_h, l_h = [], []
            for hh in range(2):
                qm = jnp.where(head0 if hh == 0 else jnp.logical_not(head0), q, 0.0)
                s = _dot(qm, k, "nt") * (ATT_HEAD_DIM ** -0.5) + b_ref[hh]
                s = jnp.where(first, NEG_INF, s)
                m = jnp.max(s, axis=-1, keepdims=True)
                p = jnp.exp(s - m)
                l = jnp.sum(p, axis=-1, keepdims=True)
                o_h.append(_dot(p / l, v))
                l_h.append(jnp.broadcast_to(m + jnp.log(l), (BAND, 128)))
            o_ref[rows, :] = jnp.where(head0, o_h[0], o_h[1])
            l_ref[rows, :] = jnp.where(head0, l_h[0], l_h[1])

    def col(part):
        return lambda hp, n: (n, _att_cols(part, g, hp))

    def col_prev(part):
        return lambda hp, n: (jnp.maximum(n - 1, 0), _att_cols(part, g, hp))

    blk = (ch, 128)
    out = pl.BlockSpec(blk, lambda hp, n: (n, hp))
    return pl.pallas_call(
        body, name=f"att_fwd_d{d}", grid=(4, nb),
        in_specs=[pl.BlockSpec(blk, col(0)), pl.BlockSpec(blk, col(1)), pl.BlockSpec(blk, col_prev(1)),
                  pl.BlockSpec(blk, col(2)), pl.BlockSpec(blk, col_prev(2)),
                  pl.BlockSpec((2, BAND, 2 * BAND), lambda hp, n: (hp, 0, 0))],
        out_specs=[out, out],
        out_shape=[jax.ShapeDtypeStruct((S, W_BRANCH), F32), jax.ShapeDtypeStruct((S, W_BRANCH), F32)],
        compiler_params=_cp(2),
    )(proj, proj, proj, proj, proj, bias)


def _att_pair(q, k, v, bias, lse_b, do, delta_b, hh, head0, mask=None):
    sel = head0 if hh == 0 else jnp.logical_not(head0)
    s = _dot(jnp.where(sel, q, 0.0), k, "nt") * (ATT_HEAD_DIM ** -0.5) + bias
    if mask is not None:
        s = jnp.where(mask, NEG_INF, s)
    c = hh * ATT_HEAD_DIM
    p = jnp.exp(s - lse_b[:, c:c + 1])
    dp = _dot(jnp.where(sel, do, 0.0), v, "nt")
    return p, p * (dp - delta_b[:, c:c + 1])


def att_bwd_q(proj, bias, lse, wts, dout, cbar, g, d):
    S = proj.shape[0]
    ch = BAND * d
    nb = S // ch

    def body(q_ref, kc_ref, kp_ref, vc_ref, vp_ref, b_ref, l_ref, w_ref, do_ref, cb_ref, dq_ref, db_ref):
        n = pl.program_id(1)
        head0 = _head_lanes()
        first = jnp.logical_and(lax.broadcasted_iota(jnp.int32, (BAND, 2 * BAND), 1) < BAND, n == 0)

        @pl.when(n == 0)
        def _():
            db_ref[...] = jnp.zeros(db_ref.shape, F32)

        for r in range(d):
            rows = _rows(r, d)
            q = q_ref[rows, :]
            k = jnp.concatenate([kp_ref[rows, :], kc_ref[rows, :]], axis=0).astype(MXU_DTYPE)
            v = jnp.concatenate([vp_ref[rows, :], vc_ref[rows, :]], axis=0).astype(MXU_DTYPE)
            w = w_ref[rows, :]
            do = w * do_ref[rows, :]
            delta = w * cb_ref[rows, :]
            lse_b = l_ref[rows, :]
            dq_h = []
            for hh in range(2):
                _, ds = _att_pair(q, k, v, b_ref[hh], lse_b, do, delta, hh, head0, mask=first)
                db_ref[hh] += ds
                dq_h.append(_dot(ds * (ATT_HEAD_DIM ** -0.5), k))
            dq_ref[rows, :] = jnp.where(head0, dq_h[0], dq_h[1]).astype(dq_ref.dtype)

    def col(part):
        return lambda hp, n: (n, _att_cols(part, g, hp))

    def col_prev(part):
        return lambda hp, n: (jnp.maximum(n - 1, 0), _att_cols(part, g, hp))

    blk = (ch, 128)
    cur = pl.BlockSpec(blk, lambda hp, n: (n, hp))
    bias_spec = pl.BlockSpec((2, BAND, 2 * BAND), lambda hp, n: (hp, 0, 0))
    return pl.pallas_call(
        body, name=f"att_bwd_q_d{d}", grid=(4, nb),
        in_specs=[pl.BlockSpec(blk, col(0)), pl.BlockSpec(blk, col(1)), pl.BlockSpec(blk, col_prev(1)),
                  pl.BlockSpec(blk, col(2)), pl.BlockSpec(blk, col_prev(2)), bias_spec, cur, cur, cur, cur],
        out_specs=[cur, bias_spec],
        out_shape=[jax.ShapeDtypeStruct((S, W_BRANCH), F32), jax.ShapeDtypeStruct((ATT_HEADS, BAND, 2 * BAND), F32)],
        compiler_params=_cp(2),
    )(proj, proj, proj, proj, proj, bias, lse, wts, dout, cbar)


def att_bwd_kv(proj, bias, lse, wts, dout, cbar, g, d):
    S = proj.shape[0]
    ch = BAND * d
    nb = S // ch

    def body(k_ref, v_ref, b_ref, q0_ref, l0_ref, w0_ref, do0_ref, cb0_ref,
             q1_ref, l1_ref, w1_ref, do1_ref, cb1_ref, dk_ref, dv_ref):
        j = pl.program_id(1)
        head0 = _head_lanes()
        has_next = j + 1 < nb
        sides = ((q0_ref, l0_ref, w0_ref, do0_ref, cb0_ref, 1), (q1_ref, l1_ref, w1_ref, do1_ref, cb1_ref, 0))
        for r in range(d):
            rows = _rows(r, d)
            k = k_ref[rows, :].astype(MXU_DTYPE)
            v = v_ref[rows, :].astype(MXU_DTYPE)
            dk = jnp.zeros((BAND, 128), F32)
            dv = jnp.zeros((BAND, 128), F32)
            for q_ref, l_ref, w_ref, do_ref, cb_ref, half in sides:
                q = q_ref[rows, :]
                w = w_ref[rows, :]
                do = w * do_ref[rows, :]
                delta = w * cb_ref[rows, :]
                lse_b = l_ref[rows, :]
                for hh in range(2):
                    sel = head0 if hh == 0 else jnp.logical_not(head0)
                    p, ds = _att_pair(q, k, v, b_ref[hh][:, half * BAND:(half + 1) * BAND], lse_b, do, delta, hh, head0)
                    if half == 0:
                        p = jnp.where(has_next, p, 0.0)
                        ds = jnp.where(has_next, ds, 0.0)
                    dk = dk + jnp.where(sel, _dot(ds * (ATT_HEAD_DIM ** -0.5), q, "tn"), 0.0)
                    dv = dv + jnp.where(sel, _dot(p, do, "tn"), 0.0)
            dk_ref[rows, :] = dk.astype(dk_ref.dtype)
            dv_ref[rows, :] = dv.astype(dv_ref.dtype)

    def col(part):
        return lambda hp, j: (j, _att_cols(part, g, hp))

    blk = (ch, 128)
    cur = pl.BlockSpec(blk, lambda hp, j: (j, hp))
    nxt = pl.BlockSpec(blk, lambda hp, j: (jnp.minimum(j + 1, nb - 1), hp))
    q_next = pl.BlockSpec(blk, lambda hp, j: (jnp.minimum(j + 1, nb - 1), _att_cols(0, g, hp)))
    return pl.pallas_call(
        body, name=f"att_bwd_kv_d{d}", grid=(4, nb),
        in_specs=[pl.BlockSpec(blk, col(1)), pl.BlockSpec(blk, col(2)),
                  pl.BlockSpec((2, BAND, 2 * BAND), lambda hp, j: (hp, 0, 0)),
                  pl.BlockSpec(blk, col(0)), cur, cur, cur, cur, q_next, nxt, nxt, nxt, nxt],
        out_specs=[cur, cur],
        out_shape=[jax.ShapeDtypeStruct((S, W_BRANCH), F32), jax.ShapeDtypeStruct((S, W_BRANCH), F32)],
        compiler_params=_cp(2),
    )(proj, proj, bias, proj, lse, wts, dout, cbar, proj, lse, wts, dout, cbar)


def att_combine(os_, lses):
    S = os_[0].shape[0]

    def body(o0, o1, o2, l0, l1, l2, out_ref, w0, w1, w2):
        ls = [l0[...], l1[...], l2[...]]
        m = jnp.maximum(jnp.maximum(ls[0], ls[1]), ls[2])
        es = [jnp.exp(l - m) for l in ls]
        den = es[0] + es[1] + es[2]
        ws = [e / den for e in es]
        out_ref[...] = (ws[0] * o0[...] + ws[1] * o1[...] + ws[2] * o2[...]).astype(out_ref.dtype)
        for w_ref, w in zip((w0, w1, w2), ws):
            w_ref[...] = w

    blk = pl.BlockSpec((ROW_TILE, W_BRANCH), lambda i: (i, 0))
    f = jax.ShapeDtypeStruct((S, W_BRANCH), F32)
    return pl.pallas_call(
        body, name="att_combine", grid=(S // ROW_TILE,), in_specs=[blk] * 6, out_specs=[blk] * 4,
        out_shape=[jax.ShapeDtypeStruct((S, W_BRANCH), BF16), f, f, f], compiler_params=_cp(1),
    )(*os_, *lses)


def _split3(x):
    x1 = x.astype(BF16)
    r1 = x - x1.astype(F32)
    x2 = r1.astype(BF16)
    x3 = (r1 - x2.astype(F32)).astype(BF16)
    return x1, x2, x3


def att_combine_bwd(dout, os_, wts, headsum):
    S = dout.shape[0]

    def body(do_ref, o0, o1, o2, w0, w1, w2, e_ref, cb_ref):
        out = w0[...] * o0[...] + w1[...] * o1[...] + w2[...] * o2[...]
        e = e_ref[...]
        acc = jnp.zeros((ROW_TILE, W_BRANCH), F32)
        for term in _split3(do_ref[...] * out):
            acc = acc + jnp.dot(term, e, preferred_element_type=F32)
        cb_ref[...] = acc

    blk = pl.BlockSpec((ROW_TILE, W_BRANCH), lambda i: (i, 0))
    return pl.pallas_call(
        body, name="att_combine_bwd", grid=(S // ROW_TILE,),
        in_specs=[blk] * 7 + [pl.BlockSpec((W_BRANCH, W_BRANCH), lambda i: (0, 0))], out_specs=blk,
        out_shape=jax.ShapeDtypeStruct((S, W_BRANCH), F32), compiler_params=_cp(1),
    )(dout, *os_, *wts, headsum)


def _cmul(ar, ai, br, bi):
    return ar * br - ai * bi, ar * bi + ai * br


def _scan_steps():
    return int(math.log2(SSM_T))


def s5_fwd(proj, b_re, b_im, a_re, a_im, c_re, c_im, d_skip):
    S = proj.shape[0]
    nt = S // SSM_T

    def body(u_ref, bre_ref, bim_ref, ar_ref, ai_ref, cre_ref, cim_ref, dsk_ref, hr_ref, hi_ref, y_ref, cr_ref, ci_ref):
        t = pl.program_id(1)

        @pl.when(t == 0)
        def _():
            cr_ref[...] = jnp.zeros(cr_ref.shape, F32)
            ci_ref[...] = jnp.zeros(ci_ref.shape, F32)

        u = u_ref[...]
        ar, ai = ar_ref[...], ai_ref[...]
        rows = lax.broadcasted_iota(jnp.int32, (SSM_T, W_BRANCH), 0)
        inr, ini = _cmul(ar, ai, cr_ref[0:1, :], ci_ref[0:1, :])
        xr = _dot(u, bre_ref[...]) + jnp.where(rows == 0, inr, 0.0)
        xi = _dot(u, bim_ref[...]) + jnp.where(rows == 0, ini, 0.0)
        pr, pi = ar, ai
        for k in range(_scan_steps()):
            dd = 1 << k
            sr = jnp.where(rows >= dd, pltpu.roll(xr, dd, 0), 0.0)
            si = jnp.where(rows >= dd, pltpu.roll(xi, dd, 0), 0.0)
            mr, mi = _cmul(pr, pi, sr, si)
            xr, xi = xr + mr, xi + mi
            pr, pi = _cmul(pr, pi, pr, pi)
        hr_ref[...] = xr
        hi_ref[...] = xi
        cr_ref[...] = jnp.broadcast_to(xr[SSM_T - 1:SSM_T, :], cr_ref.shape)
        ci_ref[...] = jnp.broadcast_to(xi[SSM_T - 1:SSM_T, :], ci_ref.shape)
        y_ref[...] = _dot(xr, cre_ref[...]) - _dot(xi, cim_ref[...]) + u * dsk_ref[...]

    u_spec = pl.BlockSpec((SSM_T, 128), lambda j, t: (t, OFF_SSM // 128 + j))
    b_spec = pl.BlockSpec((None, 128, W_BRANCH), lambda j, t: (j, 0, 0))
    a_spec = pl.BlockSpec((1, W_BRANCH), lambda j, t: (0, j))
    c_spec = pl.BlockSpec((None, W_BRANCH, 128), lambda j, t: (j, 0, 0))
    h_spec = pl.BlockSpec((SSM_T, W_BRANCH), lambda j, t: (t, j))
    return pl.pallas_call(
        body, name="s5_fwd", grid=(4, nt),
        in_specs=[u_spec, b_spec, b_spec, a_spec, a_spec, c_spec, c_spec, pl.BlockSpec((1, 128), lambda j, t: (0, j))],
        out_specs=[h_spec, h_spec, pl.BlockSpec((SSM_T, 128), lambda j, t: (t, j))],
        out_shape=[jax.ShapeDtypeStruct((S, SSM_COLS), F32), jax.ShapeDtypeStruct((S, SSM_COLS), F32),
                   jax.ShapeDtypeStruct((S, W_BRANCH), F32)],
        scratch_shapes=[pltpu.VMEM((8, W_BRANCH), F32), pltpu.VMEM((8, W_BRANCH), F32)],
        compiler_params=_cp(2),
    )(proj, b_re, b_im, a_re, a_im, c_re, c_im, d_skip)


def s5_bwd(proj, hr, hi, dy, b_re, b_im, a_re, a_im, c_re, c_im, d_skip):
    S = proj.shape[0]
    nt = S // SSM_T

    def body(u_ref, hr_ref, hi_ref, hpr_ref, hpi_ref, dy_ref, bre_ref, bim_ref, ar_ref, ai_ref, cre_ref, cim_ref,
             dsk_ref, du_ref, dbre_ref, dbim_ref, dar_ref, dai_ref, dcre_ref, dcim_ref, ddsk_ref, gr_ref, gi_ref):
        step = pl.program_id(1)
        t = nt - 1 - step

        @pl.when(step == 0)
        def _():
            gr_ref[...] = jnp.zeros(gr_ref.shape, F32)
            gi_ref[...] = jnp.zeros(gi_ref.shape, F32)
            for ref in (dbre_ref, dbim_ref, dar_ref, dai_ref, dcre_ref, dcim_ref, ddsk_ref):
                ref[...] = jnp.zeros(ref.shape, F32)

        u = u_ref[...]
        dy = dy_ref[...]
        ar, ai = ar_ref[...], ai_ref[...]
        rows = lax.broadcasted_iota(jnp.int32, (SSM_T, W_BRANCH), 0)
        inr, ini = _cmul(ar, -ai, gr_ref[0:1, :], gi_ref[0:1, :])
        xr = _dot(dy, cre_ref[...], "nt") + jnp.where(rows == SSM_T - 1, inr, 0.0)
        xi = -_dot(dy, cim_ref[...], "nt") + jnp.where(rows == SSM_T - 1, ini, 0.0)
        pr, pi = ar, -ai
        for k in range(_scan_steps()):
            dd = 1 << k
            sr = jnp.where(rows < SSM_T - dd, pltpu.roll(xr, SSM_T - dd, 0), 0.0)
            si = jnp.where(rows < SSM_T - dd, pltpu.roll(xi, SSM_T - dd, 0), 0.0)
            mr, mi = _cmul(pr, pi, sr, si)
            xr, xi = xr + mr, xi + mi
            pr, pi = _cmul(pr, pi, pr, pi)
        gr_ref[...] = jnp.broadcast_to(xr[0:1, :], gr_ref.shape)
        gi_ref[...] = jnp.broadcast_to(xi[0:1, :], gi_ref.shape)
        hr_blk, hi_blk = hr_ref[...], hi_ref[...]
        keep = (t > 0).astype(F32)
        hpr = jnp.where(rows >= 1, pltpu.roll(hr_blk, 1, 0), hpr_ref[7:8, :] * keep)
        hpi = jnp.where(rows >= 1, pltpu.roll(hi_blk, 1, 0), hpi_ref[7:8, :] * keep)
        dar_ref[...] += jnp.sum(hpr * xr + hpi * xi, axis=0, keepdims=True)
        dai_ref[...] += jnp.sum(hpr * xi - hpi * xr, axis=0, keepdims=True)
        dcre_ref[...] += _dot(hr_blk, dy, "tn")
        dcim_ref[...] -= _dot(hi_blk, dy, "tn")
        du = dy * dsk_ref[...] + _dot(xr, bre_ref[...], "nt") + _dot(xi, bim_ref[...], "nt")
        du_ref[...] = du.astype(du_ref.dtype)
        dbre_ref[...] += _dot(u, xr, "tn")
        dbim_ref[...] += _dot(u, xi, "tn")
        ddsk_ref[...] += jnp.sum(dy * u, axis=0, keepdims=True)

    def rev(t):
        return nt - 1 - t

    u_spec = pl.BlockSpec((SSM_T, 128), lambda j, t: (rev(t), OFF_SSM // 128 + j))
    h_spec = pl.BlockSpec((SSM_T, W_BRANCH), lambda j, t: (rev(t), j))
    hprev_spec = pl.BlockSpec((8, W_BRANCH), lambda j, t: (jnp.maximum(rev(t) * (SSM_T // 8) - 1, 0), j))
    ch_spec = pl.BlockSpec((SSM_T, 128), lambda j, t: (rev(t), j))
    b_spec = pl.BlockSpec((None, 128, W_BRANCH), lambda j, t: (j, 0, 0))
    a_spec = pl.BlockSpec((1, W_BRANCH), lambda j, t: (0, j))
    c_spec = pl.BlockSpec((None, W_BRANCH, 128), lambda j, t: (j, 0, 0))
    d_spec = pl.BlockSpec((1, 128), lambda j, t: (0, j))
    return pl.pallas_call(
        body, name="s5_bwd", grid=(4, nt),
        in_specs=[u_spec, h_spec, h_spec, hprev_spec, hprev_spec, ch_spec, b_spec, b_spec, a_spec, a_spec,
                  c_spec, c_spec, d_spec],
        out_specs=[ch_spec, b_spec, b_spec, a_spec, a_spec, c_spec, c_spec, d_spec],
        out_shape=[jax.ShapeDtypeStruct((S, W_BRANCH), BF16),
                   jax.ShapeDtypeStruct((4, 128, W_BRANCH), F32), jax.ShapeDtypeStruct((4, 128, W_BRANCH), F32),
                   jax.ShapeDtypeStruct((1, SSM_COLS), F32), jax.ShapeDtypeStruct((1, SSM_COLS), F32),
                   jax.ShapeDtypeStruct((4, W_BRANCH, 128), F32), jax.ShapeDtypeStruct((4, W_BRANCH, 128), F32),
                   jax.ShapeDtypeStruct((1, W_BRANCH), F32)],
        scratch_shapes=[pltpu.VMEM((8, W_BRANCH), F32), pltpu.VMEM((8, W_BRANCH), F32)],
        compiler_params=_cp(2),
    )(proj, hr, hi, hr, hi, dy, b_re, b_im, a_re, a_im, c_re, c_im, d_skip)


def glu_fwd(y, w_glu, b_glu):
    S = y.shape[0]

    def body(y_ref, w_ref, b_ref, o_ref):
        g = _gelu(y_ref[...])
        o_ref[...] = (g * _sigmoid(_dot(g, w_ref[...]) + b_ref[...])).astype(o_ref.dtype)

    blk = pl.BlockSpec((ROW_TILE, W_BRANCH), lambda i: (i, 0))
    return pl.pallas_call(
        body, name="glu_fwd", grid=(S // ROW_TILE,),
        in_specs=[blk, pl.BlockSpec((W_BRANCH, W_BRANCH), lambda i: (0, 0)), pl.BlockSpec((1, W_BRANCH), lambda i: (0, 0))],
        out_specs=blk, out_shape=jax.ShapeDtypeStruct((S, W_BRANCH), BF16), compiler_params=_cp(1),
    )(y, w_glu, b_glu)


def glu_bwd(y, w_glu, b_glu, dout):
    S = y.shape[0]

    def body(y_ref, w_ref, b_ref, do_ref, dy_ref, dw_ref, db_ref):
        yv = y_ref[...]
        do = do_ref[...]
        g = _gelu(yv)
        s = _sigmoid(_dot(g, w_ref[...]) + b_ref[...])
        dz = do * g * s * (1.0 - s)
        dg = do * s + _dot(dz, w_ref[...], "nt")
        dy_ref[...] = dg * _gelu_grad(yv)
        dw = _dot(g, dz, "tn")
        db = jnp.sum(dz, axis=0, keepdims=True)

        @pl.when(pl.program_id(0) == 0)
        def _():
            dw_ref[...] = dw
            db_ref[...] = db

        @pl.when(pl.program_id(0) > 0)
        def _():
            dw_ref[...] += dw
            db_ref[...] += db

    blk = pl.BlockSpec((ROW_TILE, W_BRANCH), lambda i: (i, 0))
    mat = pl.BlockSpec((W_BRANCH, W_BRANCH), lambda i: (0, 0))
    vec = pl.BlockSpec((1, W_BRANCH), lambda i: (0, 0))
    return pl.pallas_call(
        body, name="glu_bwd", grid=(S // ROW_TILE,), in_specs=[blk, mat, vec, blk], out_specs=[blk, mat, vec],
        out_shape=[jax.ShapeDtypeStruct((S, W_BRANCH), F32), jax.ShapeDtypeStruct((W_BRANCH, W_BRANCH), F32),
                   jax.ShapeDtypeStruct((1, W_BRANCH), F32)],
        compiler_params=_cp(1),
    )(y, w_glu, b_glu, dout)


SGU_TILE = 512
SGU_U_BLOCK = OFF_SGU // W_BRANCH
SGU_V_BLOCK = SGU_U_BLOCK + 1


def _sgu_norm(zv):
    v = _gelu(zv)
    mu = jnp.mean(v, axis=-1, keepdims=True)
    vc = v - mu
    rstd = lax.rsqrt(jnp.mean(vc * vc, axis=-1, keepdims=True) + EPS)
    return vc * rstd, rstd


def _tril():
    return lax.broadcasted_iota(jnp.int32, (SGU_CHUNK, SGU_CHUNK), 0) >= lax.broadcasted_iota(jnp.int32, (SGU_CHUNK, SGU_CHUNK), 1)


def sgu_fwd(proj, ln_g, ln_b, w_s, b_s_t):
    S = proj.shape[0]

    def body(zu_ref, zv_ref, g_ref, b_ref, ws_ref, bs_ref, o_ref, vf_ref):
        vn, _ = _sgu_norm(zv_ref[...])
        vf_ref[...] = vn * g_ref[...] + b_ref[...]
        tri = _tril()
        for gi in range(4):
            ws = jnp.where(tri, ws_ref[gi], 0.0)
            cols = slice(gi * 128, (gi + 1) * 128)
            for c in range(SGU_TILE // SGU_CHUNK):
                rows = slice(c * SGU_CHUNK, (c + 1) * SGU_CHUNK)
                sv = _dot(ws, vf_ref[rows, cols]) + bs_ref[:, gi:gi + 1]
                o_ref[rows, cols] = (_gelu(zu_ref[rows, cols]) * sv).astype(o_ref.dtype)

    blk = lambda cb: pl.BlockSpec((SGU_TILE, W_BRANCH), lambda i: (i, cb))
    vec = pl.BlockSpec((1, W_BRANCH), lambda i: (0, 0))
    return pl.pallas_call(
        body, name="sgu_fwd", grid=(S // SGU_TILE,),
        in_specs=[blk(SGU_U_BLOCK), blk(SGU_V_BLOCK), vec, vec, pl.BlockSpec((4, SGU_CHUNK, SGU_CHUNK), lambda i: (0, 0, 0)),
                  pl.BlockSpec((SGU_CHUNK, 4), lambda i: (0, 0))],
        out_specs=blk(0), out_shape=jax.ShapeDtypeStruct((S, W_BRANCH), BF16),
        scratch_shapes=[pltpu.VMEM((SGU_TILE, W_BRANCH), F32)], compiler_params=_cp(1),
    )(proj, proj, ln_g, ln_b, w_s, b_s_t)


def sgu_bwd(proj, ln_g, ln_b, w_s, b_s_t, dout):
    S = proj.shape[0]

    def body(zu_ref, zv_ref, g_ref, b_ref, ws_ref, bs_ref, do_ref, dzu_ref, dzv_ref, dg_ref, db_ref, dws_ref, dbs_ref,
             vf_ref, dvf_ref):
        @pl.when(pl.program_id(0) == 0)
        def _():
            for ref in (dg_ref, db_ref, dws_ref, dbs_ref):
                ref[...] = jnp.zeros(ref.shape, F32)

        vn, rstd = _sgu_norm(zv_ref[...])
        vf_ref[...] = vn * g_ref[...] + b_ref[...]
        tri = _tril()
        lane = lax.broadcasted_iota(jnp.int32, (SGU_CHUNK, 128), 1)
        dbs = jnp.zeros((SGU_CHUNK, 128), F32)
        for gi in range(4):
            ws = jnp.where(tri, ws_ref[gi], 0.0)
            cols = slice(gi * 128, (gi + 1) * 128)
            dws = jnp.zeros((SGU_CHUNK, SGU_CHUNK), F32)
            for c in range(SGU_TILE // SGU_CHUNK):
                rows = slice(c * SGU_CHUNK, (c + 1) * SGU_CHUNK)
                vf = vf_ref[rows, cols]
                zu = zu_ref[rows, cols]
                do = do_ref[rows, cols]
                sv = _dot(ws, vf) + bs_ref[:, gi:gi + 1]
                dzu_ref[rows, cols] = (do * sv * _gelu_grad(zu)).astype(dzu_ref.dtype)
                dsv = do * _gelu(zu)
                dvf_ref[rows, cols] = _dot(ws, dsv, "tn")
                dws = dws + _dot(dsv, vf, "nt")
                dbs = dbs + jnp.where(lane == gi, jnp.sum(dsv, axis=-1, keepdims=True), 0.0)
            dws_ref[gi] += jnp.where(tri, dws, 0.0)
        dbs_ref[...] += dbs
        dvf = dvf_ref[...]
        dg_ref[...] += jnp.sum(dvf * vn, axis=0, keepdims=True)
        db_ref[...] += jnp.sum(dvf, axis=0, keepdims=True)
        dvn = dvf * g_ref[...]
        dv = rstd * (dvn - jnp.mean(dvn, axis=-1, keepdims=True) - vn * jnp.mean(dvn * vn, axis=-1, keepdims=True))
        dzv_ref[...] = (dv * _gelu_grad(zv_ref[...])).astype(dzv_ref.dtype)

    blk = lambda cb: pl.BlockSpec((SGU_TILE, W_BRANCH), lambda i: (i, cb))
    vec = pl.BlockSpec((1, W_BRANCH), lambda i: (0, 0))
    ws_spec = pl.BlockSpec((4, SGU_CHUNK, SGU_CHUNK), lambda i: (0, 0, 0))
    return pl.pallas_call(
        body, name="sgu_bwd", grid=(S // SGU_TILE,),
        in_specs=[blk(SGU_U_BLOCK), blk(SGU_V_BLOCK), vec, vec, ws_spec, pl.BlockSpec((SGU_CHUNK, 4), lambda i: (0, 0)),
                  blk(0)],
        out_specs=[blk(0), blk(0), vec, vec, ws_spec, pl.BlockSpec((SGU_CHUNK, 128), lambda i: (0, 0))],
        out_shape=[jax.ShapeDtypeStruct((S, W_BRANCH), BF16), jax.ShapeDtypeStruct((S, W_BRANCH), BF16),
                   jax.ShapeDtypeStruct((1, W_BRANCH), F32), jax.ShapeDtypeStruct((1, W_BRANCH), F32),
                   jax.ShapeDtypeStruct((4, SGU_CHUNK, SGU_CHUNK), F32), jax.ShapeDtypeStruct((SGU_CHUNK, 128), F32)],
        scratch_shapes=[pltpu.VMEM((SGU_TILE, W_BRANCH), F32), pltpu.VMEM((SGU_TILE, W_BRANCH), F32)],
        compiler_params=_cp(1),
    )(proj, proj, ln_g, ln_b, w_s, b_s_t, dout)


GM_TILE = 512


def _gate_specs(order):
    def spec(i):
        def index(*ids):
            m, n = order(*ids)
            return (m, (OFF_GATE + i * D_MODEL) // GM_TILE + n)
        return pl.BlockSpec((GM_TILE, GM_TILE), index)
    return [spec(i) for i in range(4)]


def merge_fwd(proj, gate_b, branches, w_up):
    S = proj.shape[0]
    order = lambda n, m: (m, n)

    def body(p0, p1, p2, p3, gb_ref, b0, b1, b2, b3, w_ref, o_ref):
        acc = jnp.zeros((GM_TILE, GM_TILE), F32)
        for i, (p_ref, br_ref) in enumerate(zip((p0, p1, p2, p3), (b0, b1, b2, b3))):
            acc = acc + _sigmoid(p_ref[...] + gb_ref[i:i + 1, :]) * _dot(br_ref[...], w_ref[i])
        o_ref[...] = acc.astype(o_ref.dtype)

    br_spec = pl.BlockSpec((GM_TILE, W_BRANCH), lambda n, m: (m, 0))
    return pl.pallas_call(
        body, name="merge_fwd", grid=(D_MODEL // GM_TILE, S // GM_TILE),
        in_specs=_gate_specs(order) + [pl.BlockSpec((4, GM_TILE), lambda n, m: (0, n))] + [br_spec] * 4
        + [pl.BlockSpec((4, W_BRANCH, GM_TILE), lambda n, m: (0, 0, n))],
        out_specs=pl.BlockSpec((GM_TILE, GM_TILE), lambda n, m: (m, n)),
        out_shape=jax.ShapeDtypeStruct((S, D_MODEL), BF16), compiler_params=_cp(2),
    )(proj, proj, proj, proj, gate_b, *branches, w_up)


def merge_bwd(proj, gate_b, branches, w_up, dmerged):
    S = proj.shape[0]
    order = lambda n, m: (m, n)

    def body(p0, p1, p2, p3, gb_ref, b0, b1, b2, b3, w_ref, dm_ref, dp0, dp1, dp2, dp3, du0, du1, du2, du3, dgb_ref):
        dm = dm_ref[...]
        dgb = []
        for i, (p_ref, br_ref, dp_ref, du_ref) in enumerate(
                zip((p0, p1, p2, p3), (b0, b1, b2, b3), (dp0, dp1, dp2, dp3), (du0, du1, du2, du3))):
            gate = _sigmoid(p_ref[...] + gb_ref[i:i + 1, :])
            dpre = dm * _dot(br_ref[...], w_ref[i]) * gate * (1.0 - gate)
            dp_ref[...] = dpre.astype(dp_ref.dtype)
            du_ref[...] = (dm * gate).astype(du_ref.dtype)
            dgb.append(jnp.sum(dpre, axis=0, keepdims=True))
        dgb = jnp.concatenate(dgb, axis=0)

        @pl.when(pl.program_id(1) == 0)
        def _():
            dgb_ref[...] = dgb

        @pl.when(pl.program_id(1) > 0)
        def _():
            dgb_ref[...] += dgb

    br_spec = pl.BlockSpec((GM_TILE, W_BRANCH), lambda n, m: (m, 0))
    mn = pl.BlockSpec((GM_TILE, GM_TILE), lambda n, m: (m, n))
    gb = pl.BlockSpec((4, GM_TILE), lambda n, m: (0, n))
    big = jax.ShapeDtypeStruct((S, D_MODEL), BF16)
    outs = pl.pallas_call(
        body, name="merge_bwd", grid=(D_MODEL // GM_TILE, S // GM_TILE),
        in_specs=_gate_specs(order) + [gb] + [br_spec] * 4
        + [pl.BlockSpec((4, W_BRANCH, GM_TILE), lambda n, m: (0, 0, n)), mn],
        out_specs=[mn] * 8 + [gb], out_shape=[big] * 8 + [jax.ShapeDtypeStruct((4, D_MODEL), F32)],
        compiler_params=_cp(2),
    )(proj, proj, proj, proj, gate_b, *branches, w_up, dmerged)
    return outs[0:4], outs[4:8], outs[8]


def _xatt_probs(q, k):
    s = _dot(q, k, "nt") * (X_HEAD_DIM ** -0.5)
    p = jnp.exp(s - jnp.max(s, axis=-1, keepdims=True))
    return p / jnp.sum(p, axis=-1, keepdims=True)


def xatt_fwd(q, kv):
    S = q.shape[0]

    def body(q_ref, kv_ref, o_ref):
        for h in range(X_HEADS):
            cols = slice(h * X_HEAD_DIM, (h + 1) * X_HEAD_DIM)
            p = _xatt_probs(q_ref[:, cols], kv_ref[:, cols])
            o_ref[:, cols] = _dot(p, kv_ref[:, W_BRANCH + h * X_HEAD_DIM:W_BRANCH + (h + 1) * X_HEAD_DIM]).astype(o_ref.dtype)

    blk = pl.BlockSpec((ROW_TILE, W_BRANCH), lambda i: (i, 0))
    return pl.pallas_call(
        body, name="xatt_fwd", grid=(S // ROW_TILE,),
        in_specs=[blk, pl.BlockSpec((N_MEM, 2 * W_BRANCH), lambda i: (0, 0))], out_specs=blk,
        out_shape=jax.ShapeDtypeStruct((S, W_BRANCH), BF16), compiler_params=_cp(1),
    )(q, kv)


def xatt_bwd(q, kv, do):
    S = q.shape[0]

    def body(q_ref, kv_ref, do_ref, dq_ref, dkv_ref):
        @pl.when(pl.program_id(0) == 0)
        def _():
            dkv_ref[...] = jnp.zeros(dkv_ref.shape, F32)

        for h in range(X_HEADS):
            cols = slice(h * X_HEAD_DIM, (h + 1) * X_HEAD_DIM)
            vcols = slice(W_BRANCH + h * X_HEAD_DIM, W_BRANCH + (h + 1) * X_HEAD_DIM)
            qh, kh, doh = q_ref[:, cols], kv_ref[:, cols], do_ref[:, cols]
            p = _xatt_probs(qh, kh)
            dp = _dot(doh, kv_ref[:, vcols], "nt")
            ds = p * (dp - jnp.sum(dp * p, axis=-1, keepdims=True)) * (X_HEAD_DIM ** -0.5)
            dq_ref[:, cols] = _dot(ds, kh).astype(dq_ref.dtype)
            dkv_ref[:, cols] += _dot(ds, qh, "tn")
            dkv_ref[:, vcols] += _dot(p, doh, "tn")

    blk = pl.BlockSpec((ROW_TILE, W_BRANCH), lambda i: (i, 0))
    kv_spec = pl.BlockSpec((N_MEM, 2 * W_BRANCH), lambda i: (0, 0))
    return pl.pallas_call(
        body, name="xatt_bwd", grid=(S // ROW_TILE,), in_specs=[blk, kv_spec, blk], out_specs=[blk, kv_spec],
        out_shape=[jax.ShapeDtypeStruct((S, W_BRANCH), BF16), jax.ShapeDtypeStruct((N_MEM, 2 * W_BRANCH), F32)],
        compiler_params=_cp(1),
    )(q, kv, do)


def s5_params(a_re, a_im, log_dt, b_re, b_im, c_re, c_im):
    lam_re = jnp.minimum(a_re, -1e-4)
    lam_im = a_im
    dt = jnp.exp(log_dt)[:, None]
    mag = jnp.exp(lam_re * dt)
    ab_re, ab_im = mag * jnp.cos(lam_im * dt), mag * jnp.sin(lam_im * dt)
    den = lam_re * lam_re + lam_im * lam_im
    f_re = ((ab_re - 1.0) * lam_re + ab_im * lam_im) / den
    f_im = (ab_im * lam_re - (ab_re - 1.0) * lam_im) / den
    bb_re = f_re[..., None] * b_re - f_im[..., None] * b_im
    bb_im = f_re[..., None] * b_im + f_im[..., None] * b_re
    eye = jnp.eye(8, dtype=F32)

    def b_blocks(bb):
        t = bb.reshape(4, 8, SSM_STATE, SSM_GROUP).transpose(0, 1, 3, 2)
        return (t[:, :, :, None, :] * eye[None, :, None, :, None]).reshape(4, 128, W_BRANCH)

    def c_blocks(cc):
        t = cc.reshape(4, 8, SSM_GROUP, SSM_STATE).transpose(0, 1, 3, 2)
        return (t[:, :, :, None, :] * eye[None, :, None, :, None]).reshape(4, W_BRANCH, 128)

    return (ab_re.reshape(1, SSM_COLS), ab_im.reshape(1, SSM_COLS), b_blocks(bb_re), b_blocks(bb_im),
            c_blocks(c_re), c_blocks(c_im))


ANY = pl.BlockSpec(memory_space=pl.ANY)


def _chip_index():
    return 2 * lax.axis_index("x") + lax.axis_index("y")


def _peer_chip(j):
    x, y, c = lax.axis_index("x"), lax.axis_index("y"), lax.axis_index("c")
    return ((1 - x) if j & 2 else x, (1 - y) if j & 1 else y, c)


def _piece(ref, axis, s, n):
    size = ref.shape[axis] // n
    idx = [slice(None)] * len(ref.shape)
    idx[axis] = pl.ds(s * size, size)
    return ref.at[tuple(idx)]


def all_gather_chips(shards, axes):
    n = len(shards)

    def body(*refs):
        ins, outs = refs[:n], refs[n:2 * n]
        send, recv, loc = refs[2 * n:]
        q = _chip_index()
        for k in range(4):
            @pl.when(q == k)
            def _():
                local = []
                for t in range(n):
                    local.append(pltpu.make_async_copy(ins[t], _piece(outs[t], axes[t], k, 4), loc.at[t]))
                    local[-1].start()
                    for j in (1, 2, 3):
                        pltpu.make_async_remote_copy(
                            src_ref=ins[t], dst_ref=_piece(outs[t], axes[t], k, 4), send_sem=send.at[3 * t + j - 1],
                            recv_sem=recv.at[3 * t + j - 1], device_id=_peer_chip(j), device_id_type=MESH_ID).start()
                for t in range(n):
                    for j in (1, 2, 3):
                        pltpu.make_async_remote_copy(
                            src_ref=ins[t], dst_ref=_piece(outs[t], axes[t], k ^ j, 4), send_sem=send.at[3 * t + j - 1],
                            recv_sem=recv.at[3 * t + j - 1], device_id=_peer_chip(j), device_id_type=MESH_ID).wait()
                    local[t].wait()

    def full(s, axis):
        shape = list(s.shape)
        shape[axis] *= 4
        return jax.ShapeDtypeStruct(tuple(shape), s.dtype)

    return pl.pallas_call(
        body, name="all_gather_weights", in_specs=[ANY] * n, out_specs=[ANY] * n,
        out_shape=[full(s, a) for s, a in zip(shards, axes)],
        scratch_shapes=[pltpu.SemaphoreType.DMA((3 * n,)), pltpu.SemaphoreType.DMA((3 * n,)), pltpu.SemaphoreType.DMA((n,))],
    )(*shards)


def exchange_chips(groups, axes):
    n = len(groups)
    flat = [a for grp in groups for a in grp]
    starts = np.cumsum([0] + [len(grp) for grp in groups])

    def piece_of(ref, t, s):
        return ref if axes[t] is None else _piece(ref, axes[t], s, 4)

    def body(*refs):
        ins, outs = refs[:len(flat)], refs[len(flat):len(flat) + n]
        send, recv, loc = refs[len(flat) + n:]
        q = _chip_index()
        for k in range(4):
            @pl.when(q == k)
            def _():
                for t in range(n):
                    for l in range(len(groups[t])):
                        src = ins[starts[t] + l]
                        pltpu.make_async_copy(piece_of(src, t, k), outs[t].at[k, l], loc.at[t]).start()
                        for j in (1, 2, 3):
                            pltpu.make_async_remote_copy(
                                src_ref=piece_of(src, t, k ^ j), dst_ref=outs[t].at[k, l], send_sem=send.at[3 * t + j - 1],
                                recv_sem=recv.at[3 * t + j - 1], device_id=_peer_chip(j), device_id_type=MESH_ID).start()
                for t in range(n):
                    for j in (1, 2, 3):
                        pltpu.make_async_remote_copy(
                            src_ref=outs[t].at[k], dst_ref=outs[t].at[k ^ j], send_sem=send.at[3 * t + j - 1],
                            recv_sem=recv.at[3 * t + j - 1], device_id=_peer_chip(j), device_id_type=MESH_ID).wait()
                    pltpu.make_async_copy(outs[t].at[k ^ 1], outs[t].at[k], loc.at[t]).wait()

    def out_shape(t):
        r, c = groups[t][0].shape
        if axes[t] == 0:
            r //= 4
        elif axes[t] == 1:
            c //= 4
        return jax.ShapeDtypeStruct((4, len(groups[t]), r, c), groups[t][0].dtype)

    return pl.pallas_call(
        body, name="exchange_grads", in_specs=[ANY] * len(flat), out_specs=[ANY] * n,
        out_shape=[out_shape(t) for t in range(n)],
        scratch_shapes=[pltpu.SemaphoreType.DMA((3 * n,)), pltpu.SemaphoreType.DMA((3 * n,)), pltpu.SemaphoreType.DMA((n,))],
    )(*flat)


def swap_cores(arrs):
    n = len(arrs)

    def body(*refs):
        ins, outs = refs[:n], refs[n:2 * n]
        send, recv = refs[2 * n:]
        sibling = (lax.axis_index("x"), lax.axis_index("y"), 1 - lax.axis_index("c"))
        copies = [pltpu.make_async_remote_copy(src_ref=ins[t], dst_ref=outs[t], send_sem=send.at[t], recv_sem=recv.at[t],
                                               device_id=sibling, device_id_type=MESH_ID) for t in range(n)]
        for cp in copies:
            cp.start()
        for cp in copies:
            cp.wait()

    return pl.pallas_call(
        body, name="swap_cores", in_specs=[ANY] * n, out_specs=[ANY] * n,
        out_shape=[jax.ShapeDtypeStruct(a.shape, a.dtype) for a in arrs],
        scratch_shapes=[pltpu.SemaphoreType.DMA((n,)), pltpu.SemaphoreType.DMA((n,))],
    )(*arrs)


ELEMENTWISE_BLOCK_BYTES = 1 << 20


def _row_tile(rows, cols):
    want = max(8, ELEMENTWISE_BLOCK_BYTES // (4 * cols))
    fits = [t for t in range(8, min(rows, want) + 1, 8) if rows % t == 0]
    return fits[-1] if fits else rows


def sum4(recv, name):
    _, rows, cols = recv.shape
    tr = _row_tile(rows, cols)

    def body(r_ref, o_ref):
        o_ref[...] = ((r_ref[0] + r_ref[1]) + r_ref[2]) + r_ref[3]

    return pl.pallas_call(
        body, name=name, grid=(rows // tr,), in_specs=[pl.BlockSpec((4, tr, cols), lambda i: (0, i, 0))],
        out_specs=pl.BlockSpec((tr, cols), lambda i: (i, 0)), out_shape=jax.ShapeDtypeStruct((rows, cols), F32),
        compiler_params=_cp(1),
    )(recv)


def adamw(w, ga, gb, m, v, name):
    rows, cols = w.shape
    tr = _row_tile(rows, cols)

    def body(w_ref, ga_ref, gb_ref, m_ref, v_ref, g_ref, d_ref, nm_ref, nv_ref):
        g = ga_ref[...] + gb_ref[...]
        nm = ADAM_B1 * m_ref[...] + (1.0 - ADAM_B1) * g
        nv = ADAM_B2 * v_ref[...] + (1.0 - ADAM_B2) * (g * g)
        m_hat = nm / (1.0 - ADAM_B1 ** ADAM_STEP)
        v_hat = nv / (1.0 - ADAM_B2 ** ADAM_STEP)
        g_ref[...] = g
        nm_ref[...] = nm
        nv_ref[...] = nv
        d_ref[...] = -ADAM_LR * (m_hat / (jnp.sqrt(v_hat) + ADAM_EPS) + ADAM_WD * w_ref[...])

    blk = pl.BlockSpec((tr, cols), lambda i: (i, 0))
    f = jax.ShapeDtypeStruct((rows, cols), F32)
    return pl.pallas_call(
        body, name=name, grid=(rows // tr,), in_specs=[blk] * 5, out_specs=[blk] * 4, out_shape=[f] * 4,
        compiler_params=_cp(1),
    )(w, ga, gb, m, v)


PACK_ALIGN = 1024


def pack_small(arrs):
    parts = []
    for a in arrs:
        flat = a.reshape(-1)
        pad = (-flat.shape[0]) % PACK_ALIGN
        parts.append(jnp.pad(flat, (0, pad)) if pad else flat)
    return jnp.concatenate(parts).reshape(-1, 128)


def unpack_small(packed, shapes):
    out, row = [], 0
    for shape in shapes:
        size = int(np.prod(shape))
        rows = -(-size // PACK_ALIGN) * 8
        out.append(packed[row:row + rows].reshape(-1)[:size].reshape(shape))
        row += rows
    return out


def layer_fwd(x, mem, W, P, biases):
    sv = {"x0": x}
    h1 = rms_fwd(x, P["g_mix_pre"], BF16, "rms_pre")
    proj = mm(h1, W["w_in"], "nn", tm=1024, tn=768, tk=1024, out_dtypes=[F32], name="mm_w_in")
    a_out = pool_fwd(proj, P["pool_w"], P["pool_scale"])
    os_, lses = [], []
    for g, (win, dil) in enumerate(DIL_GROUPS):
        o, lse = att_fwd(proj, biases[g], g, dil)
        os_.append(o)
        lses.append(lse)
    b_out, w0, w1, w2 = att_combine(os_, lses)
    s5p = P["s5"]
    hr, hi, y = s5_fwd(proj, s5p[2], s5p[3], s5p[0], s5p[1], s5p[4], s5p[5], P["d_skip"])
    c_out = glu_fwd(y, W["w_glu"], P["b_glu"])
    d_out = sgu_fwd(proj, P["sgu_ln_g"], P["sgu_ln_b"], P["w_s"], P["b_s_t"])
    branches = (a_out, b_out, c_out, d_out)
    merged = merge_fwd(proj, W["gate_b"], branches, W["w_up"])
    t1 = mm(merged, W["w_out"], "nn", tm=1024, tn=1024, tk=1024, out_dtypes=[F32], name="mm_w_out")
    x1 = rms_fwd(t1, P["g_mix_post"], F32, "rms_post", res=x)
    sv.update(h1=h1, proj=proj, os=os_, lses=lses, wts=(w0, w1, w2), hr=hr, hi=hi, y=y, branches=branches,
              merged=merged, t1=t1, x1=x1)

    h2 = rms_fwd(x1, P["g_x_pre"], BF16, "rms_pre")
    mem_n = rms_fwd(mem, P["g_mem"], BF16, "rms_mem")
    q = mm(h2, W["w_cq"], "nn", tm=1024, tn=512, tk=1024, out_dtypes=[BF16], name="mm_w_cq")
    kv = mm(mem_n, W["w_ckv"], "nn", tm=256, tn=1024, tk=1024, out_dtypes=[BF16], name="mm_w_ckv")
    ox = xatt_fwd(q, kv)
    t2 = mm(ox, W["w_co"], "nn", tm=1024, tn=1024, tk=512, out_dtypes=[F32], name="mm_w_co")
    x2 = rms_fwd(t2, P["g_x_post"], F32, "rms_post", res=x1)
    sv.update(h2=h2, mem_n=mem_n, q=q, kv=kv, ox=ox, t2=t2, x2=x2)

    h3 = rms_fwd(x2, P["g_ff_pre"], BF16, "rms_pre")
    pre, act = mm(h3, W["w_ff1"], "nn", tm=1024, tn=1024, tk=1024, out_dtypes=[F32, BF16], name="mm_w_ff1",
                  epi=lambda acc: (acc, jnp.square(jnp.maximum(acc, 0.0))))
    ff = mm(act, W["w_ff2"], "nn", tm=1024, tn=1024, tk=1024, out_dtypes=[F32], name="mm_w_ff2")
    x3 = rms_fwd(ff, P["g_ff_post"], F32, "rms_post", res=x2)
    sv.update(h3=h3, pre=pre, act=act, ff=ff)
    return x3, sv


def layer_bwd(dx, mem, W, P, biases, sv, headsum):
    G = {}
    dff, G["g_ff_post"] = rms_bwd(sv["ff"], P["g_ff_post"], dx, BF16, "rms_post_bwd")
    G["w_ff2"] = mm(sv["act"], dff, "tn", tm=1024, tn=1024, tk=1024, out_dtypes=[F32], name="mm_dw_ff2")
    dpre = mm(dff, W["w_ff2"], "nt", tm=1024, tn=1024, tk=1024, out_dtypes=[BF16], name="mm_dact", extras=(sv["pre"],),
              epi=lambda acc, pre: (acc * (2.0 * jnp.maximum(pre, 0.0)),))
    G["w_ff1"] = mm(sv["h3"], dpre, "tn", tm=1024, tn=1024, tk=1024, out_dtypes=[F32], name="mm_dw_ff1")
    dh3 = mm(dpre, W["w_ff1"], "nt", tm=1024, tn=1024, tk=1024, out_dtypes=[F32], name="mm_dh3")
    dx2, G["g_ff_pre"] = rms_bwd(sv["x2"], P["g_ff_pre"], dh3, F32, "rms_pre_bwd", add=dx)
    dt2, G["g_x_post"] = rms_bwd(sv["t2"], P["g_x_post"], dx2, BF16, "rms_post_bwd")
    G["w_co"] = mm(sv["ox"], dt2, "tn", tm=512, tn=1024, tk=1024, out_dtypes=[F32], name="mm_dw_co")
    dox = mm(dt2, W["w_co"], "nt", tm=1024, tn=512, tk=1024, out_dtypes=[BF16], name="mm_dox")
    dq, dkv = xatt_bwd(sv["q"], sv["kv"], dox)
    G["w_cq"] = mm(sv["h2"], dq, "tn", tm=1024, tn=512, tk=1024, out_dtypes=[F32], name="mm_dw_cq")
    dh2 = mm(dq, W["w_cq"], "nt", tm=1024, tn=1024, tk=512, out_dtypes=[F32], name="mm_dh2")
    G["w_ckv"] = mm(sv["mem_n"], dkv, "tn", tm=1024, tn=1024, tk=256, out_dtypes=[F32], name="mm_dw_ckv")
    dmem_n = mm(dkv, W["w_ckv"], "nt", tm=256, tn=1024, tk=1024, out_dtypes=[F32], name="mm_dmem")
    _, G["g_mem"] = rms_bwd(mem, P["g_mem"], dmem_n, BF16, "rms_mem_bwd")
    dx1, G["g_x_pre"] = rms_bwd(sv["x1"], P["g_x_pre"], dh2, F32, "rms_pre_bwd", add=dx2)
    proj = sv["proj"]
    dt1, G["g_mix_post"] = rms_bwd(sv["t1"], P["g_mix_post"], dx1, BF16, "rms_post_bwd")
    G["w_out"] = mm(sv["merged"], dt1, "tn", tm=1024, tn=1024, tk=1024, out_dtypes=[F32], name="mm_dw_out")
    dmerged = mm(dt1, W["w_out"], "nt", tm=1024, tn=1024, tk=1024, out_dtypes=[F32], name="mm_dmerged")
    dgates, dups, G["gate_b"] = merge_bwd(proj, W["gate_b"], sv["branches"], W["w_up"], dmerged)
    dbr, dwup = [], []
    for i in range(4):
        dbr.append(mm(dups[i], W["w_up"][i], "nt", tm=1024, tn=512, tk=1024, out_dtypes=[F32], name="mm_dbranch"))
        dwup.append(mm(sv["branches"][i], dups[i], "tn", tm=512, tn=1024, tk=1024, out_dtypes=[F32], name="mm_dw_up"))
    G["w_up"] = jnp.concatenate(dwup, axis=0)
    d_pool, G["pool_w"], G["pool_scale"] = pool_bwd(proj, P["pool_w"], P["pool_scale"], dbr[0])
    cbar = att_combine_bwd(dbr[1], sv["os"], sv["wts"], headsum)
    dqs, dks, dvs, dbias = [], [], [], []
    for g, (win, dil) in enumerate(DIL_GROUPS):
        dq_g, db_g = att_bwd_q(proj, biases[g], sv["lses"][g], sv["wts"][g], dbr[1], cbar, g, dil)
        dk_g, dv_g = att_bwd_kv(proj, biases[g], sv["lses"][g], sv["wts"][g], dbr[1], cbar, g, dil)
        dqs.append(dq_g)
        dks.append(dk_g)
        dvs.append(dv_g)
        dbias.append(db_g)
    G["att_bias"] = dbias
    s5p = P["s5"]
    dy, G["w_glu"], G["b_glu"] = glu_bwd(sv["y"], W["w_glu"], P["b_glu"], dbr[2])
    d_ssm, dbre, dbim, dar, dai, dcre, dcim, G["d_skip"] = s5_bwd(
        proj, sv["hr"], sv["hi"], dy, s5p[2], s5p[3], s5p[0], s5p[1], s5p[4], s5p[5], P["d_skip"])
    G["s5"] = (dar, dai, dbre, dbim, dcre, dcim)
    dzu, dzv, G["sgu_ln_g"], G["sgu_ln_b"], G["w_s"], G["b_s_t"] = sgu_bwd(
        proj, P["sgu_ln_g"], P["sgu_ln_b"], P["w_s"], P["b_s_t"], dbr[3])
    d_qkv = [d.astype(BF16) for d in dqs + dks + dvs]
    dproj = jnp.concatenate([d_pool] + d_qkv + [d_ssm, dzu, dzv] + list(dgates), axis=1)
    G["w_in"] = mm(sv["h1"], dproj, "tn", tm=1024, tn=1536, tk=1024, out_dtypes=[F32], name="mm_dw_in")
    dh1 = mm(dproj, W["w_in"], "nt", tm=1024, tn=1024, tk=1536, out_dtypes=[F32], name="mm_dh1")
    dx0, G["g_mix_pre"] = rms_bwd(sv["x0"], P["g_mix_pre"], dh1, F32, "rms_pre_bwd", add=dx1)
    return dx0, G


def _as3d(name, a):
    shape2d, axis = SHARDED[name]
    rows, cols = shape2d
    if axis == 0:
        rows //= 4
    else:
        cols //= 4
    return a.reshape(DEPTH, rows, cols)


def kernel(x, mem, rel_bias, g_mix_pre, g_mix_post, w_in, gate_b, pool_w, pool_scale, a_re, a_im, log_dt, b_re, b_im, c_re, c_im, d_skip, w_glu, b_glu, sgu_ln_g, sgu_ln_b, w_s, b_s, w_up, w_out, g_x_pre, g_x_post, g_mem, w_cq, w_ckv, w_co, g_ff_pre, g_ff_post, w_ff1, w_ff2, loss_target, m_rel_bias, m_g_mix_pre, m_g_mix_post, m_w_in, m_gate_b, m_pool_w, m_pool_scale, m_a_re, m_a_im, m_log_dt, m_b_re, m_b_im, m_c_re, m_c_im, m_d_skip, m_w_glu, m_b_glu, m_sgu_ln_g, m_sgu_ln_b, m_w_s, m_b_s, m_w_up, m_w_out, m_g_x_pre, m_g_x_post, m_g_mem, m_w_cq, m_w_ckv, m_w_co, m_g_ff_pre, m_g_ff_post, m_w_ff1, m_w_ff2, v_rel_bias, v_g_mix_pre, v_g_mix_post, v_w_in, v_gate_b, v_pool_w, v_pool_scale, v_a_re, v_a_im, v_log_dt, v_b_re, v_b_im, v_c_re, v_c_im, v_d_skip, v_w_glu, v_b_glu, v_sgu_ln_g, v_sgu_ln_b, v_w_s, v_b_s, v_w_up, v_w_out, v_g_x_pre, v_g_x_post, v_g_mem, v_w_cq, v_w_ckv, v_w_co, v_g_ff_pre, v_g_ff_post, v_w_ff1, v_w_ff2):
    env = dict(locals())
    weights = {n: env[n] for n in WEIGHT_NAMES}
    mom_m = {n: env["m_" + n] for n in WEIGHT_NAMES}
    mom_v = {n: env["v_" + n] for n in WEIGHT_NAMES}
    x2d = x.reshape(x.shape[1], D_MODEL)
    mem2d = mem.reshape(N_MEM, D_MODEL)
    target = loss_target.reshape(x2d.shape)

    axes3d = [SHARDED[n][1] + 1 for n in SHARDED_NAMES]
    gathered = all_gather_chips([_as3d(n, weights[n]).astype(MXU_DTYPE if n != "gate_b" else F32) for n in SHARDED_NAMES],
                                axes3d)
    full = dict(zip(SHARDED_NAMES, gathered))
    full["w_up"] = full["w_up"].reshape(DEPTH, 4, W_BRANCH, D_MODEL)

    biases = [att_bias(rel_bias, g, dil) for g, (_, dil) in enumerate(DIL_GROUPS)]
    lanes = np.arange(W_BRANCH) // ATT_HEAD_DIM
    headsum = jnp.asarray(lanes[:, None] == lanes[None, :], dtype=BF16)

    def small_params(l, s5_prepared):
        vec = lambda a: a[l].reshape(1, -1)
        return {
            "g_mix_pre": vec(g_mix_pre), "g_mix_post": vec(g_mix_post), "g_x_pre": vec(g_x_pre), "g_x_post": vec(g_x_post),
            "g_mem": vec(g_mem), "g_ff_pre": vec(g_ff_pre), "g_ff_post": vec(g_ff_post), "pool_w": pool_w[l],
            "pool_scale": vec(pool_scale), "d_skip": vec(d_skip), "b_glu": vec(b_glu), "sgu_ln_g": vec(sgu_ln_g),
            "sgu_ln_b": vec(sgu_ln_b), "w_s": w_s[l], "b_s_t": b_s[l].T, "s5": s5_prepared,
        }

    Ws, Ps, saved, s5_vjps = [], [], [], []
    xl = x2d
    for l in range(DEPTH):
        s5_prepared, s5_vjp = jax.vjp(s5_params, a_re[l], a_im[l], log_dt[l], b_re[l], b_im[l], c_re[l], c_im[l])
        W = {n: full[n][l] for n in SHARDED_NAMES}
        P = small_params(l, s5_prepared)
        xl, sv = layer_fwd(xl, mem2d, W, P, biases)
        Ws.append(W)
        Ps.append(P)
        saved.append(sv)
        s5_vjps.append(s5_vjp)
    loss_local, dx = loss_and_grad(xl, target)
    loss = lax.psum(loss_local, ("x", "y", "c"))

    grads = [None] * DEPTH
    for l in reversed(range(DEPTH)):
        dx, grads[l] = layer_bwd(dx, mem2d, Ws[l], Ps[l], biases, saved[l], headsum)
    grad_x = dx.reshape(x.shape)

    rep = {}
    stack = lambda key, shape: jnp.stack([grads[l][key] for l in range(DEPTH)]).reshape(shape)
    for n in ("g_mix_pre", "g_mix_post", "g_x_pre", "g_x_post", "g_mem", "g_ff_pre", "g_ff_post"):
        rep[n] = stack(n, (DEPTH, D_MODEL))
    for n in ("pool_scale", "d_skip", "b_glu", "sgu_ln_g", "sgu_ln_b"):
        rep[n] = stack(n, (DEPTH, W_BRANCH))
    rep["pool_w"] = stack("pool_w", pool_w.shape)
    rep["w_s"] = stack("w_s", w_s.shape)
    rep["b_s"] = jnp.stack([grads[l]["b_s_t"][:, :4].T for l in range(DEPTH)])
    s5_grads = [s5_vjps[l](tuple(grads[l]["s5"])) for l in range(DEPTH)]
    for i, n in enumerate(("a_re", "a_im", "log_dt", "b_re", "b_im", "c_re", "c_im")):
        rep[n] = jnp.stack([s5_grads[l][i] for l in range(DEPTH)])
    dbias = [sum(grads[l]["att_bias"][g] for l in range(DEPTH)) for g in range(len(DIL_GROUPS))]
    rep["rel_bias"] = jnp.concatenate([att_bias_grad(dbias[g], dil) for g, (_, dil) in enumerate(DIL_GROUPS)], axis=1)
    rep_shapes = [weights[n].shape for n in REPLICATED_NAMES]
    packed_g = pack_small([rep[n] for n in REPLICATED_NAMES])

    groups = [[grads[l][n] for l in range(DEPTH)] for n in SHARDED_NAMES] + [[packed_g]]
    received = exchange_chips(groups, [SHARDED[n][1] for n in SHARDED_NAMES] + [None])
    partial = [sum4(r.reshape(4, -1, r.shape[-1]), "sum_chips") for r in received]
    other = swap_cores(partial)

    out_g, out_d, out_m, out_v = {}, {}, {}, {}
    for t, n in enumerate(SHARDED_NAMES):
        flat = lambda a: a.reshape(partial[t].shape)
        res = adamw(flat(weights[n]), partial[t], other[t], flat(mom_m[n]), flat(mom_v[n]), "adamw")
        out_g[n], out_d[n], out_m[n], out_v[n] = [r.reshape(weights[n].shape) for r in res]
    small = [pack_small([d[n] for n in REPLICATED_NAMES]) for d in (weights, mom_m, mom_v)]
    res = adamw(small[0], partial[-1], other[-1], small[1], small[2], "adamw")
    for d, r in zip((out_g, out_d, out_m, out_v), res):
        d.update(zip(REPLICATED_NAMES, unpack_small(r, rep_shapes)))

    return (loss, grad_x, *[out_g[n] for n in WEIGHT_NAMES], *[out_d[n] for n in WEIGHT_NAMES],
            *[out_m[n] for n in WEIGHT_NAMES], *[out_v[n] for n in WEIGHT_NAMES])
```

```python
import functools
import math

import numpy as np
import jax
import jax.numpy as jnp
from jax import lax
from jax.experimental import pallas as pl
from jax.experimental.pallas import tpu as pltpu

F32 = jnp.float32
BF16 = jnp.bfloat16
MXU_DTYPE = jnp.bfloat16
MESH_ID = pl.DeviceIdType.MESH
VMEM_LIMIT_BYTES = 56 * 1024 * 1024

D_MODEL = 1024
DEPTH = 4
N_MEM = 256
W_BRANCH = 512
POOL_WINDOWS = (2, 4, 8, 16)
POOL_HALO = 16
DIL_GROUPS = ((128, 1), (512, 4), (2048, 16))
BAND = 128
ATT_HEADS = 8
ATT_HEAD_DIM = 64
SSM_GROUP = 16
SSM_GROUPS = 32
SSM_STATE = 64
SSM_COLS = SSM_GROUPS * SSM_STATE
SSM_T = 512
SGU_CHUNK = 128
X_HEADS = 4
X_HEAD_DIM = 128
D_FF = 4096
REL_BUCKETS = 32
REL_MAX_DIST = 2048
EPS = 1e-6
NEG_INF = -1e30
OFF_POOL = 0
OFF_ATT = 512
OFF_SSM = OFF_ATT + 9 * W_BRANCH
OFF_SGU = OFF_SSM + W_BRANCH
OFF_GATE = OFF_SGU + 2 * W_BRANCH
IN_WIDTH = OFF_GATE + 4 * D_MODEL

ADAM_LR = 0.001
ADAM_B1 = 0.9
ADAM_B2 = 0.999
ADAM_EPS = 1e-08
ADAM_WD = 0.01
ADAM_STEP = 10

GELU_C = math.sqrt(2.0 / math.pi)

WEIGHT_NAMES = ['rel_bias', 'g_mix_pre', 'g_mix_post', 'w_in', 'gate_b', 'pool_w', 'pool_scale', 'a_re', 'a_im',
                'log_dt', 'b_re', 'b_im', 'c_re', 'c_im', 'd_skip', 'w_glu', 'b_glu', 'sgu_ln_g', 'sgu_ln_b',
                'w_s', 'b_s', 'w_up', 'w_out', 'g_x_pre', 'g_x_post', 'g_mem', 'w_cq', 'w_ckv', 'w_co',
                'g_ff_pre', 'g_ff_post', 'w_ff1', 'w_ff2']
SHARDED = {
    'w_in': ((D_MODEL, IN_WIDTH), 1),
    'gate_b': ((4, D_MODEL), 1),
    'w_glu': ((W_BRANCH, W_BRANCH), 0),
    'w_up': ((4 * W_BRANCH, D_MODEL), 1),
    'w_out': ((D_MODEL, D_MODEL), 0),
    'w_cq': ((D_MODEL, W_BRANCH), 0),
    'w_ckv': ((D_MODEL, D_MODEL), 0),
    'w_co': ((W_BRANCH, D_MODEL), 1),
    'w_ff1': ((D_MODEL, D_FF), 1),
    'w_ff2': ((D_FF, D_MODEL), 0),
}
SHARDED_NAMES = list(SHARDED)
REPLICATED_NAMES = [n for n in WEIGHT_NAMES if n not in SHARDED]


def _cp(n_axes):
    return pltpu.CompilerParams(dimension_semantics=("arbitrary",) * n_axes, vmem_limit_bytes=VMEM_LIMIT_BYTES)


def _dot(a, b, dims="nn"):
    cd = {"nn": ((1,), (0,)), "nt": ((1,), (1,)), "tn": ((0,), (0,))}[dims]
    return lax.dot_general(a.astype(MXU_DTYPE), b.astype(MXU_DTYPE), (cd, ((), ())), preferred_element_type=F32)


def _gelu(x):
    return 0.5 * x * (1.0 + jnp.tanh(GELU_C * (x + 0.044715 * (x * x * x))))


def _gelu_grad(x):
    t = jnp.tanh(GELU_C * (x + 0.044715 * (x * x * x)))
    return 0.5 * (1.0 + t) + 0.5 * x * (1.0 - t * t) * (GELU_C * (1.0 + 3.0 * 0.044715 * (x * x)))


def _sigmoid(x):
    return 1.0 / (1.0 + jnp.exp(-x))


def mm(a, b, dims, *, tm, tn, tk, out_dtypes, name, extras=(), epi=None):
    if dims == "tn":
        K, M = a.shape
        N = b.shape[1]
    else:
        M, K = a.shape
        N = b.shape[1] if dims == "nn" else b.shape[0]
    tm, tn, tk = min(tm, M), min(tn, N), min(tk, K)
    assert M % tm == 0 and N % tn == 0 and K % tk == 0, (name, M, N, K, tm, tn, tk)
    nk = K // tk
    ne, no = len(extras), len(out_dtypes)
    if epi is None:
        epi = lambda acc: (acc,)
    a_spec = (pl.BlockSpec((tk, tm), lambda i, j, k: (k, i)) if dims == "tn"
              else pl.BlockSpec((tm, tk), lambda i, j, k: (i, k)))
    b_spec = (pl.BlockSpec((tn, tk), lambda i, j, k: (j, k)) if dims == "nt"
              else pl.BlockSpec((tk, tn), lambda i, j, k: (k, j)))
    mn_spec = pl.BlockSpec((tm, tn), lambda i, j, k: (i, j))

    def body(a_ref, b_ref, *rest):
        extra_refs, out_refs = rest[:ne], rest[ne:ne + no]
        part = _dot(a_ref[...], b_ref[...], dims)

        def finish(acc):
            for o_ref, r in zip(out_refs, epi(acc, *[e[...] for e in extra_refs])):
                o_ref[...] = r.astype(o_ref.dtype)

        if nk == 1:
            finish(part)
        else:
            acc_ref = rest[-1]
            k = pl.program_id(2)

            @pl.when(k == 0)
            def _():
                acc_ref[...] = part

            @pl.when(k > 0)
            def _():
                acc_ref[...] += part

            @pl.when(k == nk - 1)
            def _():
                finish(acc_ref[...])

    outs = pl.pallas_call(
        body, name=name, grid=(M // tm, N // tn, nk),
        in_specs=[a_spec, b_spec] + [mn_spec] * ne,
        out_specs=[mn_spec] * no,
        out_shape=[jax.ShapeDtypeStruct((M, N), dt) for dt in out_dtypes],
        scratch_shapes=[pltpu.VMEM((tm, tn), F32)] if nk > 1 else [],
        compiler_params=_cp(3),
    )(a, b, *extras)
    return outs[0] if no == 1 else outs


ROW_TILE = 512


def rms_fwd(x, g, out_dtype, name, res=None):
    M, D = x.shape
    tm = min(ROW_TILE, M)

    def body(x_ref, g_ref, *rest):
        o_ref = rest[-1]
        xf = x_ref[...]
        y = xf * lax.rsqrt(jnp.mean(xf * xf, axis=-1, keepdims=True) + EPS) * g_ref[...]
        if res is not None:
            y = y + rest[0][...]
        o_ref[...] = y.astype(o_ref.dtype)

    row = pl.BlockSpec((tm, D), lambda i: (i, 0))
    return pl.pallas_call(
        body, name=name, grid=(M // tm,),
        in_specs=[row, pl.BlockSpec((1, D), lambda i: (0, 0))] + ([row] if res is not None else []),
        out_specs=row, out_shape=jax.ShapeDtypeStruct((M, D), out_dtype), compiler_params=_cp(1),
    )(x, g, *([res] if res is not None else []))


def rms_bwd(x, g, dy, dx_dtype, name, add=None, after=()):
    M, D = x.shape
    tm = min(ROW_TILE, M)

    def body(x_ref, g_ref, dy_ref, *rest):
        dx_ref, dg_ref = rest[-2], rest[-1]
        xf = x_ref[...]
        dyf = dy_ref[...].astype(F32)
        r = lax.rsqrt(jnp.mean(xf * xf, axis=-1, keepdims=True) + EPS)
        xn = xf * r
        dxn = dyf * g_ref[...]
        dx = r * (dxn - xn * jnp.mean(dxn * xn, axis=-1, keepdims=True))
        if add is not None:
            dx = dx + rest[0][...]
        dx_ref[...] = dx.astype(dx_ref.dtype)
        dg = jnp.sum(dyf * xn, axis=0, keepdims=True)

        @pl.when(pl.program_id(0) == 0)
        def _():
            dg_ref[...] = dg

        @pl.when(pl.program_id(0) > 0)
        def _():
            dg_ref[...] += dg

    row = pl.BlockSpec((tm, D), lambda i: (i, 0))
    vec = pl.BlockSpec((1, D), lambda i: (0, 0))
    return pl.pallas_call(
        body, name=name, grid=(M // tm,),
        in_specs=[row, vec, row] + ([row] if add is not None else []) + [ANY] * len(after),
        out_specs=[row, vec],
        out_shape=[jax.ShapeDtypeStruct((M, D), dx_dtype), jax.ShapeDtypeStruct((1, D), F32)],
        compiler_params=_cp(1),
    )(x, g, dy, *([add] if add is not None else []), *after)


def loss_and_grad(y, target):
    M, D = y.shape
    tm = ROW_TILE

    def body(y_ref, t_ref, part_ref, dy_ref):
        e = y_ref[...] - t_ref[...]
        dy_ref[...] = e / D
        part_ref[...] = jnp.broadcast_to(0.5 * jnp.sum(jnp.mean(e * e, axis=-1, keepdims=True), axis=0, keepdims=True),
                                         (8, 128))

    row = pl.BlockSpec((tm, D), lambda i: (i, 0))
    part, dy = pl.pallas_call(
        body, name="loss", grid=(M // tm,), in_specs=[row, row],
        out_specs=[pl.BlockSpec((8, 128), lambda i: (i, 0)), row],
        out_shape=[jax.ShapeDtypeStruct((8 * (M // tm), 128), F32), jax.ShapeDtypeStruct((M, D), F32)],
        compiler_params=_cp(1),
    )(y, target)
    return jnp.sum(part[::8, 0]), dy


POOL_ROWS = 512


def _pool_window_sum(xw, gi, roll_of):
    s1 = xw + pltpu.roll(xw, roll_of(1), 0)
    s2 = s1 + pltpu.roll(s1, roll_of(2), 0)
    s3 = s2 + pltpu.roll(s2, roll_of(4), 0)
    s4 = s3 + pltpu.roll(s3, roll_of(8), 0)
    return jnp.where(gi == 0, s1, jnp.where(gi == 1, s2, jnp.where(gi == 2, s3, s4)))


def _pool_cnt(i, gi):
    rows = lax.broadcasted_iota(jnp.int32, (POOL_ROWS, 128), 0) + i * POOL_ROWS
    w = jnp.where(gi == 0, 2, jnp.where(gi == 1, 4, jnp.where(gi == 2, 8, 16)))
    return jnp.minimum(rows + 1, w).astype(F32)


def pool_fwd(proj, pool_w, scale):
    S = proj.shape[0]
    nchunk = S // POOL_ROWS
    slab = POOL_ROWS + POOL_HALO

    def body(x_ref, w_ref, sc_ref, o_ref, pad_ref):
        gi = pl.program_id(0)
        pad_ref[0:POOL_HALO, :] = jnp.zeros((POOL_HALO, 128), F32)
        pad_ref[POOL_HALO:, :] = x_ref[...]
        for i in range(nchunk):
            xw = pad_ref[i * POOL_ROWS:i * POOL_ROWS + slab, :]
            ssum = _pool_window_sum(xw, gi, lambda d: d)[POOL_HALO:, :]
            p = ssum / _pool_cnt(i, gi) - xw[POOL_HALO:, :]
            o_ref[i * POOL_ROWS:(i + 1) * POOL_ROWS, :] = (_dot(p, w_ref[...]) * sc_ref[...]).astype(o_ref.dtype)

    return pl.pallas_call(
        body, name="pool_fwd", grid=(4,),
        in_specs=[pl.BlockSpec((S, 128), lambda g: (0, OFF_POOL // 128 + g)),
                  pl.BlockSpec((None, 128, 128), lambda g: (g, 0, 0)),
                  pl.BlockSpec((1, 128), lambda g: (0, g))],
        out_specs=pl.BlockSpec((S, 128), lambda g: (0, g)),
        out_shape=jax.ShapeDtypeStruct((S, W_BRANCH), BF16),
        scratch_shapes=[pltpu.VMEM((S + POOL_HALO, 128), F32)],
        compiler_params=_cp(1),
    )(proj, pool_w, scale)


def pool_bwd(proj, pool_w, scale, dy):
    S = proj.shape[0]
    nchunk = S // POOL_ROWS
    slab = POOL_ROWS + POOL_HALO

    def body(x_ref, w_ref, sc_ref, dy_ref, dx_ref, dw_ref, dsc_ref, pad_ref, pad2_ref, dp_ref):
        gi = pl.program_id(0)
        pad_ref[0:POOL_HALO, :] = jnp.zeros((POOL_HALO, 128), F32)
        pad_ref[POOL_HALO:, :] = x_ref[...]
        pad2_ref[S:, :] = jnp.zeros((POOL_HALO, 128), F32)
        dw = jnp.zeros((128, 128), F32)
        dsc = jnp.zeros((1, 128), F32)
        for i in range(nchunk):
            xw = pad_ref[i * POOL_ROWS:i * POOL_ROWS + slab, :]
            cnt = _pool_cnt(i, gi)
            p = _pool_window_sum(xw, gi, lambda d: d)[POOL_HALO:, :] / cnt - xw[POOL_HALO:, :]
            dyc = dy_ref[i * POOL_ROWS:(i + 1) * POOL_ROWS, :]
            dsc = dsc + jnp.sum(dyc * _dot(p, w_ref[...]), axis=0, keepdims=True)
            dys = dyc * sc_ref[...]
            dw = dw + _dot(p, dys, "tn")
            dp = _dot(dys, w_ref[...], "nt")
            dp_ref[i * POOL_ROWS:(i + 1) * POOL_ROWS, :] = dp
            pad2_ref[i * POOL_ROWS:(i + 1) * POOL_ROWS, :] = dp / cnt
        dw_ref[...] = dw
        dsc_ref[...] = dsc
        for i in range(nchunk):
            xw = pad2_ref[i * POOL_ROWS:i * POOL_ROWS + slab, :]
            fsum = _pool_window_sum(xw, gi, lambda d: slab - d)[:POOL_ROWS, :]
            rows = slice(i * POOL_ROWS, (i + 1) * POOL_ROWS)
            dx_ref[rows, :] = (fsum - dp_ref[rows, :]).astype(dx_ref.dtype)

    return pl.pallas_call(
        body, name="pool_bwd", grid=(4,),
        in_specs=[pl.BlockSpec((S, 128), lambda g: (0, OFF_POOL // 128 + g)),
                  pl.BlockSpec((None, 128, 128), lambda g: (g, 0, 0)),
                  pl.BlockSpec((1, 128), lambda g: (0, g)),
                  pl.BlockSpec((S, 128), lambda g: (0, g))],
        out_specs=[pl.BlockSpec((S, 128), lambda g: (0, g)),
                   pl.BlockSpec((None, 128, 128), lambda g: (g, 0, 0)),
                   pl.BlockSpec((1, 128), lambda g: (0, g))],
        out_shape=[jax.ShapeDtypeStruct((S, W_BRANCH), BF16), jax.ShapeDtypeStruct((4, 128, 128), F32),
                   jax.ShapeDtypeStruct((1, W_BRANCH), F32)],
        scratch_shapes=[pltpu.VMEM((S + POOL_HALO, 128), F32), pltpu.VMEM((S + POOL_HALO, 128), F32),
                        pltpu.VMEM((S, 128), F32)],
        compiler_params=_cp(1),
    )(proj, pool_w, scale, dy)


def _t5_bucket(n):
    exact = REL_BUCKETS // 2
    nf = np.maximum(n, 1).astype(np.float32)
    large = exact + (np.log(nf / exact) / np.log(REL_MAX_DIST / exact) * (REL_BUCKETS - exact)).astype(np.int32)
    large = np.minimum(large, REL_BUCKETS - 1)
    return np.where(n < exact, n, large).astype(np.int32)


def _band_onehot(dil):
    i = np.arange(BAND)[:, None]
    kk = np.arange(2 * BAND)[None, :]
    dist = BAND + i - kk
    local = (dist >= 0) & (dist <= BAND)
    bucket = _t5_bucket(np.clip(dist, 0, BAND) * dil)
    onehot = (bucket.reshape(-1, 1) == np.arange(REL_BUCKETS)[None, :]).astype(np.float32)
    return onehot, local


def att_bias(rel_bias, g, dil):
    onehot, local = _band_onehot(dil)
    tab = jnp.dot(jnp.asarray(onehot), rel_bias[:, g * ATT_HEADS:(g + 1) * ATT_HEADS], precision=lax.Precision.HIGHEST)
    bias = tab.reshape(BAND, 2 * BAND, ATT_HEADS).transpose(2, 0, 1)
    return jnp.where(jnp.asarray(local)[None], bias, NEG_INF)


def att_bias_grad(dbias, dil):
    onehot, _ = _band_onehot(dil)
    flat = dbias.transpose(1, 2, 0).reshape(BAND * 2 * BAND, ATT_HEADS)
    return jnp.dot(jnp.asarray(onehot).T, flat, precision=lax.Precision.HIGHEST)


def _rows(r, d):
    return pl.ds(r, BAND, stride=d) if d > 1 else pl.ds(0, BAND)


def _head_lanes():
    return lax.broadcasted_iota(jnp.int32, (BAND, 128), 1) < ATT_HEAD_DIM


def _att_cols(part, g, hp):
    return (OFF_ATT + part * 3 * W_BRANCH + g * W_BRANCH) // 128 + hp


def att_fwd(proj, bias, g, d):
    S = proj.shape[0]
    ch = BAND * d
    nb = S // ch

    def body(q_ref, kc_ref, kp_ref, vc_ref, vp_ref, b_ref, o_ref, l_ref):
        n = pl.program_id(1)
        head0 = _head_lanes()
        first = jnp.logical_and(lax.broadcasted_iota(jnp.int32, (BAND, 2 * BAND), 1) < BAND, n == 0)
        for r in range(d):
            rows = _rows(r, d)
            q = q_ref[rows, :]
            k = jnp.concatenate([kp_ref[rows, :], kc_ref[rows, :]], axis=0).astype(MXU_DTYPE)
            v = jnp.concatenate([vp_ref[rows, :], vc_ref[rows, :]], axis=0).astype(MXU_DTYPE)
            o_h, l_h = [], []
            for hh in range(2):
                qm = jnp.where(head0 if hh == 0 else jnp.logical_not(head0), q, 0.0)
                s = _dot(qm, k, "nt") * (ATT_HEAD_DIM ** -0.5) + b_ref[hh]
                s = jnp.where(first, NEG_INF, s)
                m = jnp.max(s, axis=-1, keepdims=True)
                p = jnp.exp(s - m)
                l = jnp.sum(p, axis=-1, keepdims=True)
                o_h.append(_dot(p / l, v))
                l_h.append(jnp.broadcast_to(m + jnp.log(l), (BAND, 128)))
            o_ref[rows, :] = jnp.where(head0, o_h[0], o_h[1])
            l_ref[rows, :] = jnp.where(head0, l_h[0], l_h[1])

    def col(part):
        return lambda hp, n: (n, _att_cols(part, g, hp))

    def col_prev(part):
        return lambda hp, n: (jnp.maximum(n - 1, 0), _att_cols(part, g, hp))

    blk = (ch, 128)
    out = pl.BlockSpec(blk, lambda hp, n: (n, hp))
    return pl.pallas_call(
        body, name=f"att_fwd_d{d}", grid=(4, nb),
        in_specs=[pl.BlockSpec(blk, col(0)), pl.BlockSpec(blk, col(1)), pl.BlockSpec(blk, col_prev(1)),
                  pl.BlockSpec(blk, col(2)), pl.BlockSpec(blk, col_prev(2)),
                  pl.BlockSpec((2, BAND, 2 * BAND), lambda hp, n: (hp, 0, 0))],
        out_specs=[out, out],
        out_shape=[jax.ShapeDtypeStruct((S, W_BRANCH), F32), jax.ShapeDtypeStruct((S, W_BRANCH), F32)],
        compiler_params=_cp(2),
    )(proj, proj, proj, proj, proj, bias)


def _att_pair(q, k, v, bias, lse_b, do, delta_b, hh, head0, mask=None):
    sel = head0 if hh == 0 else jnp.logical_not(head0)
    s = _dot(jnp.where(sel, q, 0.0), k, "nt") * (ATT_HEAD_DIM ** -0.5) + bias
    if mask is not None:
        s = jnp.where(mask, NEG_INF, s)
    c = hh * ATT_HEAD_DIM
    p = jnp.exp(s - lse_b[:, c:c + 1])
    dp = _dot(jnp.where(sel, do, 0.0), v, "nt")
    return p, p * (dp - delta_b[:, c:c + 1])


def att_bwd_q(proj, bias, lse, wts, dout, cbar, g, d):
    S = proj.shape[0]
    ch = BAND * d
    nb = S // ch

    def body(q_ref, kc_ref, kp_ref, vc_ref, vp_ref, b_ref, l_ref, w_ref, do_ref, cb_ref, dq_ref, db_ref):
        n = pl.program_id(1)
        head0 = _head_lanes()
        first = jnp.logical_and(lax.broadcasted_iota(jnp.int32, (BAND, 2 * BAND), 1) < BAND, n == 0)

        @pl.when(n == 0)
        def _():
            db_ref[...] = jnp.zeros(db_ref.shape, F32)

        for r in range(d):
            rows = _rows(r, d)
            q = q_ref[rows, :]
            k = jnp.concatenate([kp_ref[rows, :], kc_ref[rows, :]], axis=0).astype(MXU_DTYPE)
            v = jnp.concatenate([vp_ref[rows, :], vc_ref[rows, :]], axis=0).astype(MXU_DTYPE)
            w = w_ref[rows, :]
            do = w * do_ref[rows, :]
            delta = w * cb_ref[rows, :]
            lse_b = l_ref[rows, :]
            dq_h = []
            for hh in range(2):
                _, ds = _att_pair(q, k, v, b_ref[hh], lse_b, do, delta, hh, head0, mask=first)
                db_ref[hh] += ds
                dq_h.append(_dot(ds * (ATT_HEAD_DIM ** -0.5), k))
            dq_ref[rows, :] = jnp.where(head0, dq_h[0], dq_h[1]).astype(dq_ref.dtype)

    def col(part):
        return lambda hp, n: (n, _att_cols(part, g, hp))

    def col_prev(part):
        return lambda hp, n: (jnp.maximum(n - 1, 0), _att_cols(part, g, hp))

    blk = (ch, 128)
    cur = pl.BlockSpec(blk, lambda hp, n: (n, hp))
    bias_spec = pl.BlockSpec((2, BAND, 2 * BAND), lambda hp, n: (hp, 0, 0))
    return pl.pallas_call(
        body, name=f"att_bwd_q_d{d}", grid=(4, nb),
        in_specs=[pl.BlockSpec(blk, col(0)), pl.BlockSpec(blk, col(1)), pl.BlockSpec(blk, col_prev(1)),
                  pl.BlockSpec(blk, col(2)), pl.BlockSpec(blk, col_prev(2)), bias_spec, cur, cur, cur, cur],
        out_specs=[cur, bias_spec],
        out_shape=[jax.ShapeDtypeStruct((S, W_BRANCH), F32), jax.ShapeDtypeStruct((ATT_HEADS, BAND, 2 * BAND), F32)],
        compiler_params=_cp(2),
    )(proj, proj, proj, proj, proj, bias, lse, wts, dout, cbar)


def att_bwd_kv(proj, bias, lse, wts, dout, cbar, g, d):
    S = proj.shape[0]
    ch = BAND * d
    nb = S // ch

    def body(k_ref, v_ref, b_ref, q0_ref, l0_ref, w0_ref, do0_ref, cb0_ref,
             q1_ref, l1_ref, w1_ref, do1_ref, cb1_ref, dk_ref, dv_ref):
        j = pl.program_id(1)
        head0 = _head_lanes()
        has_next = j + 1 < nb
        sides = ((q0_ref, l0_ref, w0_ref, do0_ref, cb0_ref, 1), (q1_ref, l1_ref, w1_ref, do1_ref, cb1_ref, 0))
        for r in range(d):
            rows = _rows(r, d)
            k = k_ref[rows, :].astype(MXU_DTYPE)
            v = v_ref[rows, :].astype(MXU_DTYPE)
            dk = jnp.zeros((BAND, 128), F32)
            dv = jnp.zeros((BAND, 128), F32)
            for q_ref, l_ref, w_ref, do_ref, cb_ref, half in sides:
                q = q_ref[rows, :]
                w = w_ref[rows, :]
                do = w * do_ref[rows, :]
                delta = w * cb_ref[rows, :]
                lse_b = l_ref[rows, :]
                for hh in range(2):
                    sel = head0 if hh == 0 else jnp.logical_not(head0)
                    p, ds = _att_pair(q, k, v, b_ref[hh][:, half * BAND:(half + 1) * BAND], lse_b, do, delta, hh, head0)
                    if half == 0:
                        p = jnp.where(has_next, p, 0.0)
                        ds = jnp.where(has_next, ds, 0.0)
                    dk = dk + jnp.where(sel, _dot(ds * (ATT_HEAD_DIM ** -0.5), q, "tn"), 0.0)
                    dv = dv + jnp.where(sel, _dot(p, do, "tn"), 0.0)
            dk_ref[rows, :] = dk.astype(dk_ref.dtype)
            dv_ref[rows, :] = dv.astype(dv_ref.dtype)

    def col(part):
        return lambda hp, j: (j, _att_cols(part, g, hp))

    blk = (ch, 128)
    cur = pl.BlockSpec(blk, lambda hp, j: (j, hp))
    nxt = pl.BlockSpec(blk, lambda hp, j: (jnp.minimum(j + 1, nb - 1), hp))
    q_next = pl.BlockSpec(blk, lambda hp, j: (jnp.minimum(j + 1, nb - 1), _att_cols(0, g, hp)))
    return pl.pallas_call(
        body, name=f"att_bwd_kv_d{d}", grid=(4, nb),
        in_specs=[pl.BlockSpec(blk, col(1)), pl.BlockSpec(blk, col(2)),
                  pl.BlockSpec((2, BAND, 2 * BAND), lambda hp, j: (hp, 0, 0)),
                  pl.BlockSpec(blk, col(0)), cur, cur, cur, cur, q_next, nxt, nxt, nxt, nxt],
        out_specs=[cur, cur],
        out_shape=[jax.ShapeDtypeStruct((S, W_BRANCH), F32), jax.ShapeDtypeStruct((S, W_BRANCH), F32)],
        compiler_params=_cp(2),
    )(proj, proj, bias, proj, lse, wts, dout, cbar, proj, lse, wts, dout, cbar)


def att_combine(os_, lses):
    S = os_[0].shape[0]

    def body(o0, o1, o2, l0, l1, l2, out_ref, w0, w1, w2):
        ls = [l0[...], l1[...], l2[...]]
        m = jnp.maximum(jnp.maximum(ls[0], ls[1]), ls[2])
        es = [jnp.exp(l - m) for l in ls]
        den = es[0] + es[1] + es[2]
        ws = [e / den for e in es]
        out_ref[...] = (ws[0] * o0[...] + ws[1] * o1[...] + ws[2] * o2[...]).astype(out_ref.dtype)
        for w_ref, w in zip((w0, w1, w2), ws):
            w_ref[...] = w

    blk = pl.BlockSpec((ROW_TILE, W_BRANCH), lambda i: (i, 0))
    f = jax.ShapeDtypeStruct((S, W_BRANCH), F32)
    return pl.pallas_call(
        body, name="att_combine", grid=(S // ROW_TILE,), in_specs=[blk] * 6, out_specs=[blk] * 4,
        out_shape=[jax.ShapeDtypeStruct((S, W_BRANCH), BF16), f, f, f], compiler_params=_cp(1),
    )(*os_, *lses)


def _split3(x):
    x1 = x.astype(BF16)
    r1 = x - x1.astype(F32)
    x2 = r1.astype(BF16)
    x3 = (r1 - x2.astype(F32)).astype(BF16)
    return x1, x2, x3


def att_combine_bwd(dout, os_, wts, headsum):
    S = dout.shape[0]

    def body(do_ref, o0, o1, o2, w0, w1, w2, e_ref, cb_ref):
        out = w0[...] * o0[...] + w1[...] * o1[...] + w2[...] * o2[...]
        e = e_ref[...]
        acc = jnp.zeros((ROW_TILE, W_BRANCH), F32)
        for term in _split3(do_ref[...] * out):
            acc = acc + jnp.dot(term, e, preferred_element_type=F32)
        cb_ref[...] = acc

    blk = pl.BlockSpec((ROW_TILE, W_BRANCH), lambda i: (i, 0))
    return pl.pallas_call(
        body, name="att_combine_bwd", grid=(S // ROW_TILE,),
        in_specs=[blk] * 7 + [pl.BlockSpec((W_BRANCH, W_BRANCH), lambda i: (0, 0))], out_specs=blk,
        out_shape=jax.ShapeDtypeStruct((S, W_BRANCH), F32), compiler_params=_cp(1),
    )(dout, *os_, *wts, headsum)


def _cmul(ar, ai, br, bi):
    return ar * br - ai * bi, ar * bi + ai * br


def _scan_steps():
    return int(math.log2(SSM_T))


def s5_fwd(proj, b_re, b_im, a_re, a_im, c_re, c_im, d_skip):
    S = proj.shape[0]
    nt = S // SSM_T

    def body(u_ref, bre_ref, bim_ref, ar_ref, ai_ref, cre_ref, cim_ref, dsk_ref, hr_ref, hi_ref, y_ref, cr_ref, ci_ref):
        t = pl.program_id(1)

        @pl.when(t == 0)
        def _():
            cr_ref[...] = jnp.zeros(cr_ref.shape, F32)
            ci_ref[...] = jnp.zeros(ci_ref.shape, F32)

        u = u_ref[...]
        ar, ai = ar_ref[...], ai_ref[...]
        rows = lax.broadcasted_iota(jnp.int32, (SSM_T, W_BRANCH), 0)
        inr, ini = _cmul(ar, ai, cr_ref[0:1, :], ci_ref[0:1, :])
        xr = _dot(u, bre_ref[...]) + jnp.where(rows == 0, inr, 0.0)
        xi = _dot(u, bim_ref[...]) + jnp.where(rows == 0, ini, 0.0)
        pr, pi = ar, ai
        for k in range(_scan_steps()):
            dd = 1 << k
            sr = jnp.where(rows >= dd, pltpu.roll(xr, dd, 0), 0.0)
            si = jnp.where(rows >= dd, pltpu.roll(xi, dd, 0), 0.0)
            mr, mi = _cmul(pr, pi, sr, si)
            xr, xi = xr + mr, xi + mi
            pr, pi = _cmul(pr, pi, pr, pi)
        hr_ref[...] = xr
        hi_ref[...] = xi
        cr_ref[...] = jnp.broadcast_to(xr[SSM_T - 1:SSM_T, :], cr_ref.shape)
        ci_ref[...] = jnp.broadcast_to(xi[SSM_T - 1:SSM_T, :], ci_ref.shape)
        y_ref[...] = _dot(xr, cre_ref[...]) - _dot(xi, cim_ref[...]) + u * dsk_ref[...]

    u_spec = pl.BlockSpec((SSM_T, 128), lambda j, t: (t, OFF_SSM // 128 + j))
    b_spec = pl.BlockSpec((None, 128, W_BRANCH), lambda j, t: (j, 0, 0))
    a_spec = pl.BlockSpec((1, W_BRANCH), lambda j, t: (0, j))
    c_spec = pl.BlockSpec((None, W_BRANCH, 128), lambda j, t: (j, 0, 0))
    h_spec = pl.BlockSpec((SSM_T, W_BRANCH), lambda j, t: (t, j))
    return pl.pallas_call(
        body, name="s5_fwd", grid=(4, nt),
        in_specs=[u_spec, b_spec, b_spec, a_spec, a_spec, c_spec, c_spec, pl.BlockSpec((1, 128), lambda j, t: (0, j))],
        out_specs=[h_spec, h_spec, pl.BlockSpec((SSM_T, 128), lambda j, t: (t, j))],
        out_shape=[jax.ShapeDtypeStruct((S, SSM_COLS), F32), jax.ShapeDtypeStruct((S, SSM_COLS), F32),
                   jax.ShapeDtypeStruct((S, W_BRANCH), F32)],
        scratch_shapes=[pltpu.VMEM((8, W_BRANCH), F32), pltpu.VMEM((8, W_BRANCH), F32)],
        compiler_params=_cp(2),
    )(proj, b_re, b_im, a_re, a_im, c_re, c_im, d_skip)


def s5_bwd(proj, hr, hi, dy, b_re, b_im, a_re, a_im, c_re, c_im, d_skip):
    S = proj.shape[0]
    nt = S // SSM_T

    def body(u_ref, hr_ref, hi_ref, hpr_ref, hpi_ref, dy_ref, bre_ref, bim_ref, ar_ref, ai_ref, cre_ref, cim_ref,
             dsk_ref, du_ref, dbre_ref, dbim_ref, dar_ref, dai_ref, dcre_ref, dcim_ref, ddsk_ref, gr_ref, gi_ref):
        step = pl.program_id(1)
        t = nt - 1 - step

        @pl.when(step == 0)
        def _():
            gr_ref[...] = jnp.zeros(gr_ref.shape, F32)
            gi_ref[...] = jnp.zeros(gi_ref.shape, F32)
            for ref in (dbre_ref, dbim_ref, dar_ref, dai_ref, dcre_ref, dcim_ref, ddsk_ref):
                ref[...] = jnp.zeros(ref.shape, F32)

        u = u_ref[...]
        dy = dy_ref[...]
        ar, ai = ar_ref[...], ai_ref[...]
        rows = lax.broadcasted_iota(jnp.int32, (SSM_T, W_BRANCH), 0)
        inr, ini = _cmul(ar, -ai, gr_ref[0:1, :], gi_ref[0:1, :])
        xr = _dot(dy, cre_ref[...], "nt") + jnp.where(rows == SSM_T - 1, inr, 0.0)
        xi = -_dot(dy, cim_ref[...], "nt") + jnp.where(rows == SSM_T - 1, ini, 0.0)
        pr, pi = ar, -ai
        for k in range(_scan_steps()):
            dd = 1 << k
            sr = jnp.where(rows < SSM_T - dd, pltpu.roll(xr, SSM_T - dd, 0), 0.0)
            si = jnp.where(rows < SSM_T - dd, pltpu.roll(xi, SSM_T - dd, 0), 0.0)
            mr, mi = _cmul(pr, pi, sr, si)
            xr, xi = xr + mr, xi + mi
            pr, pi = _cmul(pr, pi, pr, pi)
        gr_ref[...] = jnp.broadcast_to(xr[0:1, :], gr_ref.shape)
        gi_ref[...] = jnp.broadcast_to(xi[0:1, :], gi_ref.shape)
        hr_blk, hi_blk = hr_ref[...], hi_ref[...]
        keep = (t > 0).astype(F32)
        hpr = jnp.where(rows >= 1, pltpu.roll(hr_blk, 1, 0), hpr_ref[7:8, :] * keep)
        hpi = jnp.where(rows >= 1, pltpu.roll(hi_blk, 1, 0), hpi_ref[7:8, :] * keep)
        dar_ref[...] += jnp.sum(hpr * xr + hpi * xi, axis=0, keepdims=True)
        dai_ref[...] += jnp.sum(hpr * xi - hpi * xr, axis=0, keepdims=True)
        dcre_ref[...] += _dot(hr_blk, dy, "tn")
        dcim_ref[...] -= _dot(hi_blk, dy, "tn")
        du = dy * dsk_ref[...] + _dot(xr, bre_ref[...], "nt") + _dot(xi, bim_ref[...], "nt")
        du_ref[...] = du.astype(du_ref.dtype)
        dbre_ref[...] += _dot(u, xr, "tn")
        dbim_ref[...] += _dot(u, xi, "tn")
        ddsk_ref[...] += jnp.sum(dy * u, axis=0, keepdims=True)

    def rev(t):
        return nt - 1 - t

    u_spec = pl.BlockSpec((SSM_T, 128), lambda j, t: (rev(t), OFF_SSM // 128 + j))
    h_spec = pl.BlockSpec((SSM_T, W_BRANCH), lambda j, t: (rev(t), j))
    hprev_spec = pl.BlockSpec((8, W_BRANCH), lambda j, t: (jnp.maximum(rev(t) * (SSM_T // 8) - 1, 0), j))
    ch_spec = pl.BlockSpec((SSM_T, 128), lambda j, t: (rev(t), j))
    b_spec = pl.BlockSpec((None, 128, W_BRANCH), lambda j, t: (j, 0, 0))
    a_spec = pl.BlockSpec((1, W_BRANCH), lambda j, t: (0, j))
    c_spec = pl.BlockSpec((None, W_BRANCH, 128), lambda j, t: (j, 0, 0))
    d_spec = pl.BlockSpec((1, 128), lambda j, t: (0, j))
    return pl.pallas_call(
        body, name="s5_bwd", grid=(4, nt),
        in_specs=[u_spec, h_spec, h_spec, hprev_spec, hprev_spec, ch_spec, b_spec, b_spec, a_spec, a_spec,
                  c_spec, c_spec, d_spec],
        out_specs=[ch_spec, b_spec, b_spec, a_spec, a_spec, c_spec, c_spec, d_spec],
        out_shape=[jax.ShapeDtypeStruct((S, W_BRANCH), BF16),
                   jax.ShapeDtypeStruct((4, 128, W_BRANCH), F32), jax.ShapeDtypeStruct((4, 128, W_BRANCH), F32),
                   jax.ShapeDtypeStruct((1, SSM_COLS), F32), jax.ShapeDtypeStruct((1, SSM_COLS), F32),
                   jax.ShapeDtypeStruct((4, W_BRANCH, 128), F32), jax.ShapeDtypeStruct((4, W_BRANCH, 128), F32),
                   jax.ShapeDtypeStruct((1, W_BRANCH), F32)],
        scratch_shapes=[pltpu.VMEM((8, W_BRANCH), F32), pltpu.VMEM((8, W_BRANCH), F32)],
        compiler_params=_cp(2),
    )(proj, hr, hi, hr, hi, dy, b_re, b_im, a_re, a_im, c_re, c_im, d_skip)


def glu_fwd(y, w_glu, b_glu):
    S = y.shape[0]

    def body(y_ref, w_ref, b_ref, o_ref):
        g = _gelu(y_ref[...])
        o_ref[...] = (g * _sigmoid(_dot(g, w_ref[...]) + b_ref[...])).astype(o_ref.dtype)

    blk = pl.BlockSpec((ROW_TILE, W_BRANCH), lambda i: (i, 0))
    return pl.pallas_call(
        body, name="glu_fwd", grid=(S // ROW_TILE,),
        in_specs=[blk, pl.BlockSpec((W_BRANCH, W_BRANCH), lambda i: (0, 0)), pl.BlockSpec((1, W_BRANCH), lambda i: (0, 0))],
        out_specs=blk, out_shape=jax.ShapeDtypeStruct((S, W_BRANCH), BF16), compiler_params=_cp(1),
    )(y, w_glu, b_glu)


def glu_bwd(y, w_glu, b_glu, dout):
    S = y.shape[0]

    def body(y_ref, w_ref, b_ref, do_ref, dy_ref, dw_ref, db_ref):
        yv = y_ref[...]
        do = do_ref[...]
        g = _gelu(yv)
        s = _sigmoid(_dot(g, w_ref[...]) + b_ref[...])
        dz = do * g * s * (1.0 - s)
        dg = do * s + _dot(dz, w_ref[...], "nt")
        dy_ref[...] = dg * _gelu_grad(yv)
        dw = _dot(g, dz, "tn")
        db = jnp.sum(dz, axis=0, keepdims=True)

        @pl.when(pl.program_id(0) == 0)
        def _():
            dw_ref[...] = dw
            db_ref[...] = db

        @pl.when(pl.program_id(0) > 0)
        def _():
            dw_ref[...] += dw
            db_ref[...] += db

    blk = pl.BlockSpec((ROW_TILE, W_BRANCH), lambda i: (i, 0))
    mat = pl.BlockSpec((W_BRANCH, W_BRANCH), lambda i: (0, 0))
    vec = pl.BlockSpec((1, W_BRANCH), lambda i: (0, 0))
    return pl.pallas_call(
        body, name="glu_bwd", grid=(S // ROW_TILE,), in_specs=[blk, mat, vec, blk], out_specs=[blk, mat, vec],
        out_shape=[jax.ShapeDtypeStruct((S, W_BRANCH), F32), jax.ShapeDtypeStruct((W_BRANCH, W_BRANCH), F32),
                   jax.ShapeDtypeStruct((1, W_BRANCH), F32)],
        compiler_params=_cp(1),
    )(y, w_glu, b_glu, dout)


SGU_TILE = 512
SGU_U_BLOCK = OFF_SGU // W_BRANCH
SGU_V_BLOCK = SGU_U_BLOCK + 1


def _sgu_norm(zv):
    v = _gelu(zv)
    mu = jnp.mean(v, axis=-1, keepdims=True)
    vc = v - mu
    rstd = lax.rsqrt(jnp.mean(vc * vc, axis=-1, keepdims=True) + EPS)
    return vc * rstd, rstd


def _tril():
    return lax.broadcasted_iota(jnp.int32, (SGU_CHUNK, SGU_CHUNK), 0) >= lax.broadcasted_iota(jnp.int32, (SGU_CHUNK, SGU_CHUNK), 1)


def sgu_fwd(proj, ln_g, ln_b, w_s, b_s_t):
    S = proj.shape[0]

    def body(zu_ref, zv_ref, g_ref, b_ref, ws_ref, bs_ref, o_ref, vf_ref):
        vn, _ = _sgu_norm(zv_ref[...])
        vf_ref[...] = vn * g_ref[...] + b_ref[...]
        tri = _tril()
        for gi in range(4):
            ws = jnp.where(tri, ws_ref[gi], 0.0)
            cols = slice(gi * 128, (gi + 1) * 128)
            for c in range(SGU_TILE // SGU_CHUNK):
                rows = slice(c * SGU_CHUNK, (c + 1) * SGU_CHUNK)
                sv = _dot(ws, vf_ref[rows, cols]) + bs_ref[:, gi:gi + 1]
                o_ref[rows, cols] = (_gelu(zu_ref[rows, cols]) * sv).astype(o_ref.dtype)

    blk = lambda cb: pl.BlockSpec((SGU_TILE, W_BRANCH), lambda i: (i, cb))
    vec = pl.BlockSpec((1, W_BRANCH), lambda i: (0, 0))
    return pl.pallas_call(
        body, name="sgu_fwd", grid=(S // SGU_TILE,),
        in_specs=[blk(SGU_U_BLOCK), blk(SGU_V_BLOCK), vec, vec, pl.BlockSpec((4, SGU_CHUNK, SGU_CHUNK), lambda i: (0, 0, 0)),
                  pl.BlockSpec((SGU_CHUNK, 4), lambda i: (0, 0))],
        out_specs=blk(0), out_shape=jax.ShapeDtypeStruct((S, W_BRANCH), BF16),
        scratch_shapes=[pltpu.VMEM((SGU_TILE, W_BRANCH), F32)], compiler_params=_cp(1),
    )(proj, proj, ln_g, ln_b, w_s, b_s_t)


def sgu_bwd(proj, ln_g, ln_b, w_s, b_s_t, dout):
    S = proj.shape[0]

    def body(zu_ref, zv_ref, g_ref, b_ref, ws_ref, bs_ref, do_ref, dzu_ref, dzv_ref, dg_ref, db_ref, dws_ref, dbs_ref,
             vf_ref, dvf_ref):
        @pl.when(pl.program_id(0) == 0)
        def _():
            for ref in (dg_ref, db_ref, dws_ref, dbs_ref):
                ref[...] = jnp.zeros(ref.shape, F32)

        vn, rstd = _sgu_norm(zv_ref[...])
        vf_ref[...] = vn * g_ref[...] + b_ref[...]
        tri = _tril()
        lane = lax.broadcasted_iota(jnp.int32, (SGU_CHUNK, 128), 1)
        dbs = jnp.zeros((SGU_CHUNK, 128), F32)
        for gi in range(4):
            ws = jnp.where(tri, ws_ref[gi], 0.0)
            cols = slice(gi * 128, (gi + 1) * 128)
            dws = jnp.zeros((SGU_CHUNK, SGU_CHUNK), F32)
            for c in range(SGU_TILE // SGU_CHUNK):
                rows = slice(c * SGU_CHUNK, (c + 1) * SGU_CHUNK)
                vf = vf_ref[rows, cols]
                zu = zu_ref[rows, cols]
                do = do_ref[rows, cols]
                sv = _dot(ws, vf) + bs_ref[:, gi:gi + 1]
                dzu_ref[rows, cols] = (do * sv * _gelu_grad(zu)).astype(dzu_ref.dtype)
                dsv = do * _gelu(zu)
                dvf_ref[rows, cols] = _dot(ws, dsv, "tn")
                dws = dws + _dot(dsv, vf, "nt")
                dbs = dbs + jnp.where(lane == gi, jnp.sum(dsv, axis=-1, keepdims=True), 0.0)
            dws_ref[gi] += jnp.where(tri, dws, 0.0)
        dbs_ref[...] += dbs
        dvf = dvf_ref[...]
        dg_ref[...] += jnp.sum(dvf * vn, axis=0, keepdims=True)
        db_ref[...] += jnp.sum(dvf, axis=0, keepdims=True)
        dvn = dvf * g_ref[...]
        dv = rstd * (dvn - jnp.mean(dvn, axis=-1, keepdims=True) - vn * jnp.mean(dvn * vn, axis=-1, keepdims=True))
        dzv_ref[...] = (dv * _gelu_grad(zv_ref[...])).astype(dzv_ref.dtype)

    blk = lambda cb: pl.BlockSpec((SGU_TILE, W_BRANCH), lambda i: (i, cb))
    vec = pl.BlockSpec((1, W_BRANCH), lambda i: (0, 0))
    ws_spec = pl.BlockSpec((4, SGU_CHUNK, SGU_CHUNK), lambda i: (0, 0, 0))
    return pl.pallas_call(
        body, name="sgu_bwd", grid=(S // SGU_TILE,),
        in_specs=[blk(SGU_U_BLOCK), blk(SGU_V_BLOCK), vec, vec, ws_spec, pl.BlockSpec((SGU_CHUNK, 4), lambda i: (0, 0)),
                  blk(0)],
        out_specs=[blk(0), blk(0), vec, vec, ws_spec, pl.BlockSpec((SGU_CHUNK, 128), lambda i: (0, 0))],
        out_shape=[jax.ShapeDtypeStruct((S, W_BRANCH), BF16), jax.ShapeDtypeStruct((S, W_BRANCH), BF16),
                   jax.ShapeDtypeStruct((1, W_BRANCH), F32), jax.ShapeDtypeStruct((1, W_BRANCH), F32),
                   jax.ShapeDtypeStruct((4, SGU_CHUNK, SGU_CHUNK), F32), jax.ShapeDtypeStruct((SGU_CHUNK, 128), F32)],
        scratch_shapes=[pltpu.VMEM((SGU_TILE, W_BRANCH), F32), pltpu.VMEM((SGU_TILE, W_BRANCH), F32)],
        compiler_params=_cp(1),
    )(proj, proj, ln_g, ln_b, w_s, b_s_t, dout)


GM_TILE = 512


def _gate_specs(order):
    def spec(i):
        def index(*ids):
            m, n = order(*ids)
            return (m, (OFF_GATE + i * D_MODEL) // GM_TILE + n)
        return pl.BlockSpec((GM_TILE, GM_TILE), index)
    return [spec(i) for i in range(4)]


def merge_fwd(proj, gate_b, branches, w_up):
    S = proj.shape[0]
    order = lambda n, m: (m, n)

    def body(p0, p1, p2, p3, gb_ref, b0, b1, b2, b3, w_ref, o_ref):
        acc = jnp.zeros((GM_TILE, GM_TILE), F32)
        for i, (p_ref, br_ref) in enumerate(zip((p0, p1, p2, p3), (b0, b1, b2, b3))):
            acc = acc + _sigmoid(p_ref[...] + gb_ref[i:i + 1, :]) * _dot(br_ref[...], w_ref[i])
        o_ref[...] = acc.astype(o_ref.dtype)

    br_spec = pl.BlockSpec((GM_TILE, W_BRANCH), lambda n, m: (m, 0))
    return pl.pallas_call(
        body, name="merge_fwd", grid=(D_MODEL // GM_TILE, S // GM_TILE),
        in_specs=_gate_specs(order) + [pl.BlockSpec((4, GM_TILE), lambda n, m: (0, n))] + [br_spec] * 4
        + [pl.BlockSpec((4, W_BRANCH, GM_TILE), lambda n, m: (0, 0, n))],
        out_specs=pl.BlockSpec((GM_TILE, GM_TILE), lambda n, m: (m, n)),
        out_shape=jax.ShapeDtypeStruct((S, D_MODEL), BF16), compiler_params=_cp(2),
    )(proj, proj, proj, proj, gate_b, *branches, w_up)


def merge_bwd(proj, gate_b, branches, w_up, dmerged):
    S = proj.shape[0]
    order = lambda n, m: (m, n)

    def body(p0, p1, p2, p3, gb_ref, b0, b1, b2, b3, w_ref, dm_ref, dp0, dp1, dp2, dp3, du0, du1, du2, du3, dgb_ref):
        dm = dm_ref[...]
        dgb = []
        for i, (p_ref, br_ref, dp_ref, du_ref) in enumerate(
                zip((p0, p1, p2, p3), (b0, b1, b2, b3), (dp0, dp1, dp2, dp3), (du0, du1, du2, du3))):
            gate = _sigmoid(p_ref[...] + gb_ref[i:i + 1, :])
            dpre = dm * _dot(br_ref[...], w_ref[i]) * gate * (1.0 - gate)
            dp_ref[...] = dpre.astype(dp_ref.dtype)
            du_ref[...] = (dm * gate).astype(du_ref.dtype)
            dgb.append(jnp.sum(dpre, axis=0, keepdims=True))
        dgb = jnp.concatenate(dgb, axis=0)

        @pl.when(pl.program_id(1) == 0)
        def _():
            dgb_ref[...] = dgb

        @pl.when(pl.program_id(1) > 0)
        def _():
            dgb_ref[...] += dgb

    br_spec = pl.BlockSpec((GM_TILE, W_BRANCH), lambda n, m: (m, 0))
    mn = pl.BlockSpec((GM_TILE, GM_TILE), lambda n, m: (m, n))
    gb = pl.BlockSpec((4, GM_TILE), lambda n, m: (0, n))
    big = jax.ShapeDtypeStruct((S, D_MODEL), BF16)
    outs = pl.pallas_call(
        body, name="merge_bwd", grid=(D_MODEL // GM_TILE, S // GM_TILE),
        in_specs=_gate_specs(order) + [gb] + [br_spec] * 4
        + [pl.BlockSpec((4, W_BRANCH, GM_TILE), lambda n, m: (0, 0, n)), mn],
        out_specs=[mn] * 8 + [gb], out_shape=[big] * 8 + [jax.ShapeDtypeStruct((4, D_MODEL), F32)],
        compiler_params=_cp(2),
    )(proj, proj, proj, proj, gate_b, *branches, w_up, dmerged)
    return outs[0:4], outs[4:8], outs[8]


def _xatt_probs(q, k):
    s = _dot(q, k, "nt") * (X_HEAD_DIM ** -0.5)
    p = jnp.exp(s - jnp.max(s, axis=-1, keepdims=True))
    return p / jnp.sum(p, axis=-1, keepdims=True)


def xatt_fwd(q, kv):
    S = q.shape[0]

    def body(q_ref, kv_ref, o_ref):
        for h in range(X_HEADS):
            cols = slice(h * X_HEAD_DIM, (h + 1) * X_HEAD_DIM)
            p = _xatt_probs(q_ref[:, cols], kv_ref[:, cols])
            o_ref[:, cols] = _dot(p, kv_ref[:, W_BRANCH + h * X_HEAD_DIM:W_BRANCH + (h + 1) * X_HEAD_DIM]).astype(o_ref.dtype)

    blk = pl.BlockSpec((ROW_TILE, W_BRANCH), lambda i: (i, 0))
    return pl.pallas_call(
        body, name="xatt_fwd", grid=(S // ROW_TILE,),
        in_specs=[blk, pl.BlockSpec((N_MEM, 2 * W_BRANCH), lambda i: (0, 0))], out_specs=blk,
        out_shape=jax.ShapeDtypeStruct((S, W_BRANCH), BF16), compiler_params=_cp(1),
    )(q, kv)


def xatt_bwd(q, kv, do):
    S = q.shape[0]

    def body(q_ref, kv_ref, do_ref, dq_ref, dkv_ref):
        @pl.when(pl.program_id(0) == 0)
        def _():
            dkv_ref[...] = jnp.zeros(dkv_ref.shape, F32)

        for h in range(X_HEADS):
            cols = slice(h * X_HEAD_DIM, (h + 1) * X_HEAD_DIM)
            vcols = slice(W_BRANCH + h * X_HEAD_DIM, W_BRANCH + (h + 1) * X_HEAD_DIM)
            qh, kh, doh = q_ref[:, cols], kv_ref[:, cols], do_ref[:, cols]
            p = _xatt_probs(qh, kh)
            dp = _dot(doh, kv_ref[:, vcols], "nt")
            ds = p * (dp - jnp.sum(dp * p, axis=-1, keepdims=True)) * (X_HEAD_DIM ** -0.5)
            dq_ref[:, cols] = _dot(ds, kh).astype(dq_ref.dtype)
            dkv_ref[:, cols] += _dot(ds, qh, "tn")
            dkv_ref[:, vcols] += _dot(p, doh, "tn")

    blk = pl.BlockSpec((ROW_TILE, W_BRANCH), lambda i: (i, 0))
    kv_spec = pl.BlockSpec((N_MEM, 2 * W_BRANCH), lambda i: (0, 0))
    return pl.pallas_call(
        body, name="xatt_bwd", grid=(S // ROW_TILE,), in_specs=[blk, kv_spec, blk], out_specs=[blk, kv_spec],
        out_shape=[jax.ShapeDtypeStruct((S, W_BRANCH), BF16), jax.ShapeDtypeStruct((N_MEM, 2 * W_BRANCH), F32)],
        compiler_params=_cp(1),
    )(q, kv, do)


def s5_params(a_re, a_im, log_dt, b_re, b_im, c_re, c_im):
    lam_re = jnp.minimum(a_re, -1e-4)
    lam_im = a_im
    dt = jnp.exp(log_dt)[:, None]
    mag = jnp.exp(lam_re * dt)
    ab_re, ab_im = mag * jnp.cos(lam_im * dt), mag * jnp.sin(lam_im * dt)
    den = lam_re * lam_re + lam_im * lam_im
    f_re = ((ab_re - 1.0) * lam_re + ab_im * lam_im) / den
    f_im = (ab_im * lam_re - (ab_re - 1.0) * lam_im) / den
    bb_re = f_re[..., None] * b_re - f_im[..., None] * b_im
    bb_im = f_re[..., None] * b_im + f_im[..., None] * b_re
    eye = jnp.eye(8, dtype=F32)

    def b_blocks(bb):
        t = bb.reshape(4, 8, SSM_STATE, SSM_GROUP).transpose(0, 1, 3, 2)
        return (t[:, :, :, None, :] * eye[None, :, None, :, None]).reshape(4, 128, W_BRANCH)

    def c_blocks(cc):
        t = cc.reshape(4, 8, SSM_GROUP, SSM_STATE).transpose(0, 1, 3, 2)
        return (t[:, :, :, None, :] * eye[None, :, None, :, None]).reshape(4, W_BRANCH, 128)

    return (ab_re.reshape(1, SSM_COLS), ab_im.reshape(1, SSM_COLS), b_blocks(bb_re), b_blocks(bb_im),
            c_blocks(c_re), c_blocks(c_im))


ANY = pl.BlockSpec(memory_space=pl.ANY)


def _chip_index():
    return 2 * lax.axis_index("x") + lax.axis_index("y")


def _peer_chip(j):
    x, y, c = lax.axis_index("x"), lax.axis_index("y"), lax.axis_index("c")
    return ((1 - x) if j & 2 else x, (1 - y) if j & 1 else y, c)


def _piece(ref, axis, s, n):
    size = ref.shape[axis] // n
    idx = [slice(None)] * len(ref.shape)
    idx[axis] = pl.ds(s * size, size)
    return ref.at[tuple(idx)]


HBM_SPEC = pl.BlockSpec(memory_space=pltpu.HBM)
SEM_SPEC = pl.BlockSpec(memory_space=pltpu.SEMAPHORE)
SIDE_EFFECT = pltpu.SideEffectType.DATAFLOW_SIDE_EFFECTING


def _chip_copies(ins, lands, send, recv, axes, mode, k):
    pairs = []
    for t in range(len(ins)):
        for j in (1, 2, 3):
            if mode == "gather":
                src, dst, arrives = ins[t], _piece(lands[t], axes[t], k, 4), _piece(lands[t], axes[t], k ^ j, 4)
            else:
                src = ins[t] if axes[t] is None else _piece(ins[t], axes[t], k ^ j, 4)
                dst, arrives = lands[t].at[k], lands[t].at[k ^ j]
            sems = dict(send_sem=send.at[3 * t + j - 1], recv_sem=recv.at[3 * t + j - 1], device_id=_peer_chip(j),
                        device_id_type=MESH_ID)
            pairs.append((pltpu.make_async_remote_copy(src_ref=src, dst_ref=dst, **sems),
                          pltpu.make_async_remote_copy(src_ref=src, dst_ref=arrives, **sems)))
    return pairs


def chips_start(ins, lands, axes, mode, name):
    n = len(ins)

    def body(*refs):
        in_refs, land_refs = refs[:n], refs[n:2 * n]
        send, recv, token = refs[2 * n], refs[2 * n + 1], refs[-1]
        q = _chip_index()
        for k in range(4):
            @pl.when(q == k)
            def _():
                for start, _ in _chip_copies(in_refs, land_refs, send, recv, axes, mode, k):
                    start.start()
        token[...] = jnp.zeros(token.shape, token.dtype)

    hbm = lambda a: pltpu.HBM(a.shape, a.dtype)
    outs = pl.pallas_call(
        body, name=name, in_specs=[HBM_SPEC] * (2 * n),
        out_specs=[SEM_SPEC, SEM_SPEC] + [HBM_SPEC] * (2 * n) + [pl.BlockSpec(memory_space=pltpu.VMEM)],
        out_shape=[pltpu.SemaphoreType.DMA((3 * n,)), pltpu.SemaphoreType.DMA((3 * n,))]
        + [hbm(a) for a in ins] + [hbm(a) for a in lands] + [jax.ShapeDtypeStruct((8, 128), F32)],
        input_output_aliases={i: 2 + i for i in range(2 * n)},
        compiler_params=pltpu.CompilerParams(has_side_effects=SIDE_EFFECT),
    )(*[pltpu.with_memory_space_constraint(a, pltpu.HBM) for a in list(ins) + list(lands)])
    return outs[0], outs[1], outs[2:2 + n], outs[2 + n:2 + 2 * n], outs[-1]


def chips_wait(send, recv, ins, lands, axes, mode, name, after=()):
    n = len(ins)

    def body(*refs):
        in_refs, land_refs = refs[:n], refs[n:2 * n]
        send_ref, recv_ref = refs[2 * n], refs[2 * n + 1]
        q = _chip_index()
        for k in range(4):
            @pl.when(q == k)
            def _():
                for _, wait in _chip_copies(in_refs, land_refs, send_ref, recv_ref, axes, mode, k):
                    wait.wait_send()
                    wait.wait_recv()

    hbm = lambda a: pltpu.HBM(a.shape, a.dtype)
    outs = pl.pallas_call(
        body, name=name, in_specs=[HBM_SPEC] * (2 * n) + [SEM_SPEC, SEM_SPEC] + [ANY] * len(after),
        out_specs=[HBM_SPEC] * (2 * n), out_shape=[hbm(a) for a in ins] + [hbm(a) for a in lands],
        input_output_aliases={i: i for i in range(2 * n)},
        compiler_params=pltpu.CompilerParams(has_side_effects=SIDE_EFFECT),
    )(*ins, *lands, send, recv, *after)
    return outs[:n], outs[n:]


def swap_cores(arrs):
    n = len(arrs)

    def body(*refs):
        ins, outs = refs[:n], refs[n:2 * n]
        send, recv = refs[2 * n:]
        sibling = (lax.axis_index("x"), lax.axis_index("y"), 1 - lax.axis_index("c"))
        copies = [pltpu.make_async_remote_copy(src_ref=ins[t], dst_ref=outs[t], send_sem=send.at[t], recv_sem=recv.at[t],
                                               device_id=sibling, device_id_type=MESH_ID) for t in range(n)]
        for cp in copies:
            cp.start()
        for cp in copies:
            cp.wait()

    return pl.pallas_call(
        body, name="swap_cores", in_specs=[ANY] * n, out_specs=[ANY] * n,
        out_shape=[jax.ShapeDtypeStruct(a.shape, a.dtype) for a in arrs],
        scratch_shapes=[pltpu.SemaphoreType.DMA((n,)), pltpu.SemaphoreType.DMA((n,))],
    )(*arrs)


ELEMENTWISE_BLOCK_BYTES = 1 << 20


def _row_tile(rows, cols):
    want = max(8, ELEMENTWISE_BLOCK_BYTES // (4 * cols))
    fits = [t for t in range(8, min(rows, want) + 1, 8) if rows % t == 0]
    return fits[-1] if fits else rows


def sum_chips(recv, own, axis, chip, stacked, l, name):
    _, r, c = recv.shape
    tr = _row_tile(r, c)
    nrt = r // tr

    def body(chip_ref, r_ref, own_ref, stacked_ref, o_ref):
        for k in range(4):
            @pl.when(chip_ref[0] == k)
            def _():
                terms = [own_ref[...] if s == k else r_ref[s] for s in range(4)]
                o_ref[...] = ((terms[0] + terms[1]) + terms[2]) + terms[3]

    own_index = {0: lambda i, q: (q[0] * nrt + i, 0), 1: lambda i, q: (i, q[0]), None: lambda i, q: (i, 0)}[axis]
    return pl.pallas_call(
        body, name=name,
        grid_spec=pltpu.PrefetchScalarGridSpec(
            num_scalar_prefetch=1, grid=(nrt,),
            in_specs=[pl.BlockSpec((4, tr, c), lambda i, q: (0, i, 0)), pl.BlockSpec((tr, c), own_index), ANY],
            out_specs=pl.BlockSpec((None, tr, c), lambda i, q: (l, i, 0))),
        out_shape=jax.ShapeDtypeStruct(stacked.shape, F32), input_output_aliases={3: 0}, compiler_params=_cp(1),
    )(chip, recv, own, stacked)


def adamw(w, ga, gb, m, v, name):
    rows, cols = w.shape
    tr = _row_tile(rows, cols)

    def body(w_ref, ga_ref, gb_ref, m_ref, v_ref, g_ref, d_ref, nm_ref, nv_ref):
        g = ga_ref[...] + gb_ref[...]
        nm = ADAM_B1 * m_ref[...] + (1.0 - ADAM_B1) * g
        nv = ADAM_B2 * v_ref[...] + (1.0 - ADAM_B2) * (g * g)
        m_hat = nm / (1.0 - ADAM_B1 ** ADAM_STEP)
        v_hat = nv / (1.0 - ADAM_B2 ** ADAM_STEP)
        g_ref[...] = g
        nm_ref[...] = nm
        nv_ref[...] = nv
        d_ref[...] = -ADAM_LR * (m_hat / (jnp.sqrt(v_hat) + ADAM_EPS) + ADAM_WD * w_ref[...])

    blk = pl.BlockSpec((tr, cols), lambda i: (i, 0))
    f = jax.ShapeDtypeStruct((rows, cols), F32)
    return pl.pallas_call(
        body, name=name, grid=(rows // tr,), in_specs=[blk] * 5, out_specs=[blk] * 4, out_shape=[f] * 4,
        compiler_params=_cp(1),
    )(w, ga, gb, m, v)


PACK_ALIGN = 1024


def pack_small(arrs):
    parts = []
    for a in arrs:
        flat = a.reshape(-1)
        pad = (-flat.shape[0]) % PACK_ALIGN
        parts.append(jnp.pad(flat, (0, pad)) if pad else flat)
    return jnp.concatenate(parts).reshape(-1, 128)


def unpack_small(packed, shapes):
    out, row = [], 0
    for shape in shapes:
        size = int(np.prod(shape))
        rows = -(-size // PACK_ALIGN) * 8
        out.append(packed[row:row + rows].reshape(-1)[:size].reshape(shape))
        row += rows
    return out


def layer_fwd(x, mem, W, P, biases):
    sv = {"x0": x}
    h1 = rms_fwd(x, P["g_mix_pre"], BF16, "rms_pre")
    proj = mm(h1, W["w_in"], "nn", tm=1024, tn=768, tk=1024, out_dtypes=[F32], name="mm_w_in")
    a_out = pool_fwd(proj, P["pool_w"], P["pool_scale"])
    os_, lses = [], []
    for g, (win, dil) in enumerate(DIL_GROUPS):
        o, lse = att_fwd(proj, biases[g], g, dil)
        os_.append(o)
        lses.append(lse)
    b_out, w0, w1, w2 = att_combine(os_, lses)
    s5p = P["s5"]
    hr, hi, y = s5_fwd(proj, s5p[2], s5p[3], s5p[0], s5p[1], s5p[4], s5p[5], P["d_skip"])
    c_out = glu_fwd(y, W["w_glu"], P["b_glu"])
    d_out = sgu_fwd(proj, P["sgu_ln_g"], P["sgu_ln_b"], P["w_s"], P["b_s_t"])
    branches = (a_out, b_out, c_out, d_out)
    merged = merge_fwd(proj, W["gate_b"], branches, W["w_up"])
    t1 = mm(merged, W["w_out"], "nn", tm=1024, tn=1024, tk=1024, out_dtypes=[F32], name="mm_w_out")
    x1 = rms_fwd(t1, P["g_mix_post"], F32, "rms_post", res=x)
    sv.update(h1=h1, proj=proj, os=os_, lses=lses, wts=(w0, w1, w2), hr=hr, hi=hi, y=y, branches=branches,
              merged=merged, t1=t1, x1=x1)

    h2 = rms_fwd(x1, P["g_x_pre"], BF16, "rms_pre")
    mem_n = rms_fwd(mem, P["g_mem"], BF16, "rms_mem")
    q = mm(h2, W["w_cq"], "nn", tm=1024, tn=512, tk=1024, out_dtypes=[BF16], name="mm_w_cq")
    kv = mm(mem_n, W["w_ckv"], "nn", tm=256, tn=1024, tk=1024, out_dtypes=[BF16], name="mm_w_ckv")
    ox = xatt_fwd(q, kv)
    t2 = mm(ox, W["w_co"], "nn", tm=1024, tn=1024, tk=512, out_dtypes=[F32], name="mm_w_co")
    x2 = rms_fwd(t2, P["g_x_post"], F32, "rms_post", res=x1)
    sv.update(h2=h2, mem_n=mem_n, q=q, kv=kv, ox=ox, t2=t2, x2=x2)

    h3 = rms_fwd(x2, P["g_ff_pre"], BF16, "rms_pre")
    pre, act = mm(h3, W["w_ff1"], "nn", tm=1024, tn=1024, tk=1024, out_dtypes=[F32, BF16], name="mm_w_ff1",
                  epi=lambda acc: (acc, jnp.square(jnp.maximum(acc, 0.0))))
    ff = mm(act, W["w_ff2"], "nn", tm=1024, tn=1024, tk=1024, out_dtypes=[F32], name="mm_w_ff2")
    x3 = rms_fwd(ff, P["g_ff_post"], F32, "rms_post", res=x2)
    sv.update(h3=h3, pre=pre, act=act, ff=ff)
    return x3, sv


def layer_bwd(dx, mem, W, P, biases, sv, headsum, after=()):
    G = {}
    dff, G["g_ff_post"] = rms_bwd(sv["ff"], P["g_ff_post"], dx, BF16, "rms_post_bwd", after=after)
    G["w_ff2"] = mm(sv["act"], dff, "tn", tm=1024, tn=1024, tk=1024, out_dtypes=[F32], name="mm_dw_ff2")
    dpre = mm(dff, W["w_ff2"], "nt", tm=1024, tn=1024, tk=1024, out_dtypes=[BF16], name="mm_dact", extras=(sv["pre"],),
              epi=lambda acc, pre: (acc * (2.0 * jnp.maximum(pre, 0.0)),))
    G["w_ff1"] = mm(sv["h3"], dpre, "tn", tm=1024, tn=1024, tk=1024, out_dtypes=[F32], name="mm_dw_ff1")
    dh3 = mm(dpre, W["w_ff1"], "nt", tm=1024, tn=1024, tk=1024, out_dtypes=[F32], name="mm_dh3")
    dx2, G["g_ff_pre"] = rms_bwd(sv["x2"], P["g_ff_pre"], dh3, F32, "rms_pre_bwd", add=dx)
    dt2, G["g_x_post"] = rms_bwd(sv["t2"], P["g_x_post"], dx2, BF16, "rms_post_bwd")
    G["w_co"] = mm(sv["ox"], dt2, "tn", tm=512, tn=1024, tk=1024, out_dtypes=[F32], name="mm_dw_co")
    dox = mm(dt2, W["w_co"], "nt", tm=1024, tn=512, tk=1024, out_dtypes=[BF16], name="mm_dox")
    dq, dkv = xatt_bwd(sv["q"], sv["kv"], dox)
    G["w_cq"] = mm(sv["h2"], dq, "tn", tm=1024, tn=512, tk=1024, out_dtypes=[F32], name="mm_dw_cq")
    dh2 = mm(dq, W["w_cq"], "nt", tm=1024, tn=1024, tk=512, out_dtypes=[F32], name="mm_dh2")
    G["w_ckv"] = mm(sv["mem_n"], dkv, "tn", tm=1024, tn=1024, tk=256, out_dtypes=[F32], name="mm_dw_ckv")
    dmem_n = mm(dkv, W["w_ckv"], "nt", tm=256, tn=1024, tk=1024, out_dtypes=[F32], name="mm_dmem")
    _, G["g_mem"] = rms_bwd(mem, P["g_mem"], dmem_n, BF16, "rms_mem_bwd")
    dx1, G["g_x_pre"] = rms_bwd(sv["x1"], P["g_x_pre"], dh2, F32, "rms_pre_bwd", add=dx2)
    proj = sv["proj"]
    dt1, G["g_mix_post"] = rms_bwd(sv["t1"], P["g_mix_post"], dx1, BF16, "rms_post_bwd")
    G["w_out"] = mm(sv["merged"], dt1, "tn", tm=1024, tn=1024, tk=1024, out_dtypes=[F32], name="mm_dw_out")
    dmerged = mm(dt1, W["w_out"], "nt", tm=1024, tn=1024, tk=1024, out_dtypes=[F32], name="mm_dmerged")
    dgates, dups, G["gate_b"] = merge_bwd(proj, W["gate_b"], sv["branches"], W["w_up"], dmerged)
    dbr, dwup = [], []
    for i in range(4):
        dbr.append(mm(dups[i], W["w_up"][i], "nt", tm=1024, tn=512, tk=1024, out_dtypes=[F32], name="mm_dbranch"))
        dwup.append(mm(sv["branches"][i], dups[i], "tn", tm=512, tn=1024, tk=1024, out_dtypes=[F32], name="mm_dw_up"))
    G["w_up"] = jnp.concatenate(dwup, axis=0)
    d_pool, G["pool_w"], G["pool_scale"] = pool_bwd(proj, P["pool_w"], P["pool_scale"], dbr[0])
    cbar = att_combine_bwd(dbr[1], sv["os"], sv["wts"], headsum)
    dqs, dks, dvs, dbias = [], [], [], []
    for g, (win, dil) in enumerate(DIL_GROUPS):
        dq_g, db_g = att_bwd_q(proj, biases[g], sv["lses"][g], sv["wts"][g], dbr[1], cbar, g, dil)
        dk_g, dv_g = att_bwd_kv(proj, biases[g], sv["lses"][g], sv["wts"][g], dbr[1], cbar, g, dil)
        dqs.append(dq_g)
        dks.append(dk_g)
        dvs.append(dv_g)
        dbias.append(db_g)
    G["att_bias"] = dbias
    s5p = P["s5"]
    dy, G["w_glu"], G["b_glu"] = glu_bwd(sv["y"], W["w_glu"], P["b_glu"], dbr[2])
    d_ssm, dbre, dbim, dar, dai, dcre, dcim, G["d_skip"] = s5_bwd(
        proj, sv["hr"], sv["hi"], dy, s5p[2], s5p[3], s5p[0], s5p[1], s5p[4], s5p[5], P["d_skip"])
    G["s5"] = (dar, dai, dbre, dbim, dcre, dcim)
    dzu, dzv, G["sgu_ln_g"], G["sgu_ln_b"], G["w_s"], G["b_s_t"] = sgu_bwd(
        proj, P["sgu_ln_g"], P["sgu_ln_b"], P["w_s"], P["b_s_t"], dbr[3])
    d_qkv = [d.astype(BF16) for d in dqs + dks + dvs]
    dproj = jnp.concatenate([d_pool] + d_qkv + [d_ssm, dzu, dzv] + list(dgates), axis=1)
    G["w_in"] = mm(sv["h1"], dproj, "tn", tm=1024, tn=1536, tk=1024, out_dtypes=[F32], name="mm_dw_in")
    dh1 = mm(dproj, W["w_in"], "nt", tm=1024, tn=1024, tk=1536, out_dtypes=[F32], name="mm_dh1")
    dx0, G["g_mix_pre"] = rms_bwd(sv["x0"], P["g_mix_pre"], dh1, F32, "rms_pre_bwd", add=dx1)
    return dx0, G


def _as3d(name, a):
    shape2d, axis = SHARDED[name]
    rows, cols = shape2d
    if axis == 0:
        rows //= 4
    else:
        cols //= 4
    return a.reshape(DEPTH, rows, cols)


def kernel(x, mem, rel_bias, g_mix_pre, g_mix_post, w_in, gate_b, pool_w, pool_scale, a_re, a_im, log_dt, b_re, b_im, c_re, c_im, d_skip, w_glu, b_glu, sgu_ln_g, sgu_ln_b, w_s, b_s, w_up, w_out, g_x_pre, g_x_post, g_mem, w_cq, w_ckv, w_co, g_ff_pre, g_ff_post, w_ff1, w_ff2, loss_target, m_rel_bias, m_g_mix_pre, m_g_mix_post, m_w_in, m_gate_b, m_pool_w, m_pool_scale, m_a_re, m_a_im, m_log_dt, m_b_re, m_b_im, m_c_re, m_c_im, m_d_skip, m_w_glu, m_b_glu, m_sgu_ln_g, m_sgu_ln_b, m_w_s, m_b_s, m_w_up, m_w_out, m_g_x_pre, m_g_x_post, m_g_mem, m_w_cq, m_w_ckv, m_w_co, m_g_ff_pre, m_g_ff_post, m_w_ff1, m_w_ff2, v_rel_bias, v_g_mix_pre, v_g_mix_post, v_w_in, v_gate_b, v_pool_w, v_pool_scale, v_a_re, v_a_im, v_log_dt, v_b_re, v_b_im, v_c_re, v_c_im, v_d_skip, v_w_glu, v_b_glu, v_sgu_ln_g, v_sgu_ln_b, v_w_s, v_b_s, v_w_up, v_w_out, v_g_x_pre, v_g_x_post, v_g_mem, v_w_cq, v_w_ckv, v_w_co, v_g_ff_pre, v_g_ff_post, v_w_ff1, v_w_ff2):
    env = dict(locals())
    weights = {n: env[n] for n in WEIGHT_NAMES}
    mom_m = {n: env["m_" + n] for n in WEIGHT_NAMES}
    mom_v = {n: env["v_" + n] for n in WEIGHT_NAMES}
    x2d = x.reshape(x.shape[1], D_MODEL)
    mem2d = mem.reshape(N_MEM, D_MODEL)
    target = loss_target.reshape(x2d.shape)

    axes = [SHARDED[n][1] for n in SHARDED_NAMES]
    chip = _chip_index().astype(jnp.int32).reshape(1)
    gathers = []
    for l in range(DEPTH):
        shards = [_as3d(n, weights[n])[l].astype(F32 if n == "gate_b" else MXU_DTYPE) for n in SHARDED_NAMES]
        lands = [jnp.concatenate([s] * 4, axis=a) for s, a in zip(shards, axes)]
        gathers.append(chips_start(shards, lands, axes, "gather", f"gather_start_{l}"))

    biases = [att_bias(rel_bias, g, dil) for g, (_, dil) in enumerate(DIL_GROUPS)]
    lanes = np.arange(W_BRANCH) // ATT_HEAD_DIM
    headsum = jnp.asarray(lanes[:, None] == lanes[None, :], dtype=BF16)

    def small_params(l, s5_prepared):
        vec = lambda a: a[l].reshape(1, -1)
        return {
            "g_mix_pre": vec(g_mix_pre), "g_mix_post": vec(g_mix_post), "g_x_pre": vec(g_x_pre), "g_x_post": vec(g_x_post),
            "g_mem": vec(g_mem), "g_ff_pre": vec(g_ff_pre), "g_ff_post": vec(g_ff_post), "pool_w": pool_w[l],
            "pool_scale": vec(pool_scale), "d_skip": vec(d_skip), "b_glu": vec(b_glu), "sgu_ln_g": vec(sgu_ln_g),
            "sgu_ln_b": vec(sgu_ln_b), "w_s": w_s[l], "b_s_t": b_s[l].T, "s5": s5_prepared,
        }

    Ws, Ps, saved, s5_vjps = [], [], [], []
    xl = x2d
    for l in range(DEPTH):
        s5_prepared, s5_vjp = jax.vjp(s5_params, a_re[l], a_im[l], log_dt[l], b_re[l], b_im[l], c_re[l], c_im[l])
        send, recv, shards, lands, _ = gathers[l]
        after = [g[4] for g in gathers[1:]] if l == 0 else [xl]
        _, lands = chips_wait(send, recv, shards, lands, axes, "gather", f"gather_wait_{l}", after=after)
        W = dict(zip(SHARDED_NAMES, lands))
        W["w_up"] = W["w_up"].reshape(4, W_BRANCH, D_MODEL)
        P = small_params(l, s5_prepared)
        xl, sv = layer_fwd(xl, mem2d, W, P, biases)
        Ws.append(W)
        Ps.append(P)
        saved.append(sv)
        s5_vjps.append(s5_vjp)
    loss_local, dx = loss_and_grad(xl, target)
    loss = lax.psum(loss_local, ("x", "y", "c"))

    def scatter_start(srcs, src_axes, name):
        lands = []
        for s, a in zip(srcs, src_axes):
            r, c = s.shape
            lands.append(lax.empty((4, r // 4 if a == 0 else r, c // 4 if a == 1 else c), F32))
        return chips_start(srcs, lands, src_axes, "scatter", name)

    grads, scatters, after = [None] * DEPTH, [None] * DEPTH, ()
    for l in reversed(range(DEPTH)):
        dx, grads[l] = layer_bwd(dx, mem2d, Ws[l], Ps[l], biases, saved[l], headsum, after=after)
        if l > 0:
            scatters[l] = scatter_start([grads[l][n] for n in SHARDED_NAMES], axes, f"grads_start_{l}")
            after = (scatters[l][4],)
    grad_x = dx.reshape(x.shape)

    rep = {}
    stack = lambda key, shape: jnp.stack([grads[l][key] for l in range(DEPTH)]).reshape(shape)
    for n in ("g_mix_pre", "g_mix_post", "g_x_pre", "g_x_post", "g_mem", "g_ff_pre", "g_ff_post"):
        rep[n] = stack(n, (DEPTH, D_MODEL))
    for n in ("pool_scale", "d_skip", "b_glu", "sgu_ln_g", "sgu_ln_b"):
        rep[n] = stack(n, (DEPTH, W_BRANCH))
    rep["pool_w"] = stack("pool_w", pool_w.shape)
    rep["w_s"] = stack("w_s", w_s.shape)
    rep["b_s"] = jnp.stack([grads[l]["b_s_t"][:, :4].T for l in range(DEPTH)])
    s5_grads = [s5_vjps[l](tuple(grads[l]["s5"])) for l in range(DEPTH)]
    for i, n in enumerate(("a_re", "a_im", "log_dt", "b_re", "b_im", "c_re", "c_im")):
        rep[n] = jnp.stack([s5_grads[l][i] for l in range(DEPTH)])
    dbias = [sum(grads[l]["att_bias"][g] for l in range(DEPTH)) for g in range(len(DIL_GROUPS))]
    rep["rel_bias"] = jnp.concatenate([att_bias_grad(dbias[g], dil) for g, (_, dil) in enumerate(DIL_GROUPS)], axis=1)
    rep_shapes = [weights[n].shape for n in REPLICATED_NAMES]
    packed_g = pack_small([rep[n] for n in REPLICATED_NAMES])

    scatters[0] = scatter_start([grads[0][n] for n in SHARDED_NAMES] + [packed_g], axes + [None], "grads_start_0")
    stacked = [None] * (len(SHARDED_NAMES) + 1)
    for l in reversed(range(DEPTH)):
        send, recv, srcs, lands, _ = scatters[l]
        src_axes = axes + [None] if l == 0 else axes
        srcs, lands = chips_wait(send, recv, srcs, lands, src_axes, "scatter", f"grads_wait_{l}")
        for t, (own, arrived, a) in enumerate(zip(srcs, lands, src_axes)):
            if stacked[t] is None:
                stacked[t] = lax.empty((DEPTH if t < len(SHARDED_NAMES) else 1,) + arrived.shape[1:], F32)
            stacked[t] = sum_chips(arrived, own, a, chip, stacked[t], l, "sum_chips")
    partial = [s.reshape(-1, s.shape[-1]) for s in stacked]
    other = swap_cores(partial)

    out_g, out_d, out_m, out_v = {}, {}, {}, {}
    for t, n in enumerate(SHARDED_NAMES):
        flat = lambda a: a.reshape(partial[t].shape)
        res = adamw(flat(weights[n]), partial[t], other[t], flat(mom_m[n]), flat(mom_v[n]), "adamw")
        out_g[n], out_d[n], out_m[n], out_v[n] = [r.reshape(weights[n].shape) for r in res]
    small = [pack_small([d[n] for n in REPLICATED_NAMES]) for d in (weights, mom_m, mom_v)]
    res = adamw(small[0], partial[-1], other[-1], small[1], small[2], "adamw")
    for d, r in zip((out_g, out_d, out_m, out_v), res):
        d.update(zip(REPLICATED_NAMES, unpack_small(r, rep_shapes)))

    return (loss, grad_x, *[out_g[n] for n in WEIGHT_NAMES], *[out_d[n] for n in WEIGHT_NAMES],
            *[out_m[n] for n in WEIGHT_NAMES], *[out_v[n] for n in WEIGHT_NAMES])
```

```python
import functools
import math

import numpy as np
import jax
import jax.numpy as jnp
from jax import lax
from jax.experimental import pallas as pl
from jax.experimental.pallas import tpu as pltpu

F32 = jnp.float32
BF16 = jnp.bfloat16
MXU_DTYPE = jnp.bfloat16
MESH_ID = pl.DeviceIdType.MESH
VMEM_LIMIT_BYTES = 56 * 1024 * 1024

D_MODEL = 1024
DEPTH = 4
N_MEM = 256
W_BRANCH = 512
POOL_WINDOWS = (2, 4, 8, 16)
POOL_HALO = 16
DIL_GROUPS = ((128, 1), (512, 4), (2048, 16))
BAND = 128
ATT_HEADS = 8
ATT_HEAD_DIM = 64
SSM_GROUP = 16
SSM_GROUPS = 32
SSM_STATE = 64
SSM_COLS = SSM_GROUPS * SSM_STATE
SSM_T = 512
SGU_CHUNK = 128
X_HEADS = 4
X_HEAD_DIM = 128
D_FF = 4096
REL_BUCKETS = 32
REL_MAX_DIST = 2048
EPS = 1e-6
NEG_INF = -1e30
OFF_POOL = 0
OFF_ATT = 512
OFF_SSM = OFF_ATT + 9 * W_BRANCH
OFF_SGU = OFF_SSM + W_BRANCH
OFF_GATE = OFF_SGU + 2 * W_BRANCH
IN_WIDTH = OFF_GATE + 4 * D_MODEL

ADAM_LR = 0.001
ADAM_B1 = 0.9
ADAM_B2 = 0.999
ADAM_EPS = 1e-08
ADAM_WD = 0.01
ADAM_STEP = 10

GELU_C = math.sqrt(2.0 / math.pi)

WEIGHT_NAMES = ['rel_bias', 'g_mix_pre', 'g_mix_post', 'w_in', 'gate_b', 'pool_w', 'pool_scale', 'a_re', 'a_im',
                'log_dt', 'b_re', 'b_im', 'c_re', 'c_im', 'd_skip', 'w_glu', 'b_glu', 'sgu_ln_g', 'sgu_ln_b',
                'w_s', 'b_s', 'w_up', 'w_out', 'g_x_pre', 'g_x_post', 'g_mem', 'w_cq', 'w_ckv', 'w_co',
                'g_ff_pre', 'g_ff_post', 'w_ff1', 'w_ff2']
SHARDED = {
    'w_in': ((D_MODEL, IN_WIDTH), 1),
    'gate_b': ((4, D_MODEL), 1),
    'w_glu': ((W_BRANCH, W_BRANCH), 0),
    'w_up': ((4 * W_BRANCH, D_MODEL), 1),
    'w_out': ((D_MODEL, D_MODEL), 0),
    'w_cq': ((D_MODEL, W_BRANCH), 0),
    'w_ckv': ((D_MODEL, D_MODEL), 0),
    'w_co': ((W_BRANCH, D_MODEL), 1),
    'w_ff1': ((D_MODEL, D_FF), 1),
    'w_ff2': ((D_FF, D_MODEL), 0),
}
SHARDED_NAMES = list(SHARDED)
REPLICATED_NAMES = [n for n in WEIGHT_NAMES if n not in SHARDED]


def _cp(n_axes):
    return pltpu.CompilerParams(dimension_semantics=("arbitrary",) * n_axes, vmem_limit_bytes=VMEM_LIMIT_BYTES)


def _dot(a, b, dims="nn"):
    cd = {"nn": ((1,), (0,)), "nt": ((1,), (1,)), "tn": ((0,), (0,))}[dims]
    return lax.dot_general(a.astype(MXU_DTYPE), b.astype(MXU_DTYPE), (cd, ((), ())), preferred_element_type=F32)


def _gelu(x):
    return 0.5 * x * (1.0 + jnp.tanh(GELU_C * (x + 0.044715 * (x * x * x))))


def _gelu_grad(x):
    t = jnp.tanh(GELU_C * (x + 0.044715 * (x * x * x)))
    return 0.5 * (1.0 + t) + 0.5 * x * (1.0 - t * t) * (GELU_C * (1.0 + 3.0 * 0.044715 * (x * x)))


def _sigmoid(x):
    return 1.0 / (1.0 + jnp.exp(-x))


def mm(a, b, dims, *, tm, tn, tk, out_dtypes, name, extras=(), epi=None, after=()):
    if dims == "tn":
        K, M = a.shape
        N = b.shape[1]
    else:
        M, K = a.shape
        N = b.shape[1] if dims == "nn" else b.shape[0]
    tm, tn, tk = min(tm, M), min(tn, N), min(tk, K)
    assert M % tm == 0 and N % tn == 0 and K % tk == 0, (name, M, N, K, tm, tn, tk)
    nk = K // tk
    ne, no = len(extras), len(out_dtypes)
    if epi is None:
        epi = lambda acc: (acc,)
    a_spec = (pl.BlockSpec((tk, tm), lambda i, j, k: (k, i)) if dims == "tn"
              else pl.BlockSpec((tm, tk), lambda i, j, k: (i, k)))
    b_spec = (pl.BlockSpec((tn, tk), lambda i, j, k: (j, k)) if dims == "nt"
              else pl.BlockSpec((tk, tn), lambda i, j, k: (k, j)))
    mn_spec = pl.BlockSpec((tm, tn), lambda i, j, k: (i, j))

    def body(a_ref, b_ref, *rest):
        extra_refs, out_refs = rest[:ne], rest[ne + len(after):ne + len(after) + no]
        part = _dot(a_ref[...], b_ref[...], dims)

        def finish(acc):
            for o_ref, r in zip(out_refs, epi(acc, *[e[...] for e in extra_refs])):
                o_ref[...] = r.astype(o_ref.dtype)

        if nk == 1:
            finish(part)
        else:
            acc_ref = rest[-1]
            k = pl.program_id(2)

            @pl.when(k == 0)
            def _():
                acc_ref[...] = part

            @pl.when(k > 0)
            def _():
                acc_ref[...] += part

            @pl.when(k == nk - 1)
            def _():
                finish(acc_ref[...])

    outs = pl.pallas_call(
        body, name=name, grid=(M // tm, N // tn, nk),
        in_specs=[a_spec, b_spec] + [mn_spec] * ne + [ANY] * len(after),
        out_specs=[mn_spec] * no,
        out_shape=[jax.ShapeDtypeStruct((M, N), dt) for dt in out_dtypes],
        scratch_shapes=[pltpu.VMEM((tm, tn), F32)] if nk > 1 else [],
        compiler_params=_cp(3),
    )(a, b, *extras, *after)
    return outs[0] if no == 1 else outs


ROW_TILE = 512


def rms_fwd(x, g, out_dtype, name, res=None):
    M, D = x.shape
    tm = min(ROW_TILE, M)

    def body(x_ref, g_ref, *rest):
        o_ref = rest[-1]
        xf = x_ref[...]
        y = xf * lax.rsqrt(jnp.mean(xf * xf, axis=-1, keepdims=True) + EPS) * g_ref[...]
        if res is not None:
            y = y + rest[0][...]
        o_ref[...] = y.astype(o_ref.dtype)

    row = pl.BlockSpec((tm, D), lambda i: (i, 0))
    return pl.pallas_call(
        body, name=name, grid=(M // tm,),
        in_specs=[row, pl.BlockSpec((1, D), lambda i: (0, 0))] + ([row] if res is not None else []),
        out_specs=row, out_shape=jax.ShapeDtypeStruct((M, D), out_dtype), compiler_params=_cp(1),
    )(x, g, *([res] if res is not None else []))


def rms_bwd(x, g, dy, dx_dtype, name, add=None, after=()):
    M, D = x.shape
    tm = min(ROW_TILE, M)

    def body(x_ref, g_ref, dy_ref, *rest):
        dx_ref, dg_ref = rest[-2], rest[-1]
        xf = x_ref[...]
        dyf = dy_ref[...].astype(F32)
        r = lax.rsqrt(jnp.mean(xf * xf, axis=-1, keepdims=True) + EPS)
        xn = xf * r
        dxn = dyf * g_ref[...]
        dx = r * (dxn - xn * jnp.mean(dxn * xn, axis=-1, keepdims=True))
        if add is not None:
            dx = dx + rest[0][...]
        dx_ref[...] = dx.astype(dx_ref.dtype)
        dg = jnp.sum(dyf * xn, axis=0, keepdims=True)

        @pl.when(pl.program_id(0) == 0)
        def _():
            dg_ref[...] = dg

        @pl.when(pl.program_id(0) > 0)
        def _():
            dg_ref[...] += dg

    row = pl.BlockSpec((tm, D), lambda i: (i, 0))
    vec = pl.BlockSpec((1, D), lambda i: (0, 0))
    return pl.pallas_call(
        body, name=name, grid=(M // tm,),
        in_specs=[row, vec, row] + ([row] if add is not None else []) + [ANY] * len(after),
        out_specs=[row, vec],
        out_shape=[jax.ShapeDtypeStruct((M, D), dx_dtype), jax.ShapeDtypeStruct((1, D), F32)],
        compiler_params=_cp(1),
    )(x, g, dy, *([add] if add is not None else []), *after)


def loss_and_grad(y, target):
    M, D = y.shape
    tm = ROW_TILE

    def body(y_ref, t_ref, part_ref, dy_ref):
        e = y_ref[...] - t_ref[...]
        dy_ref[...] = e / D
        part_ref[...] = jnp.broadcast_to(0.5 * jnp.sum(jnp.mean(e * e, axis=-1, keepdims=True), axis=0, keepdims=True),
                                         (8, 128))

    row = pl.BlockSpec((tm, D), lambda i: (i, 0))
    part, dy = pl.pallas_call(
        body, name="loss", grid=(M // tm,), in_specs=[row, row],
        out_specs=[pl.BlockSpec((8, 128), lambda i: (i, 0)), row],
        out_shape=[jax.ShapeDtypeStruct((8 * (M // tm), 128), F32), jax.ShapeDtypeStruct((M, D), F32)],
        compiler_params=_cp(1),
    )(y, target)
    return jnp.sum(part[::8, 0]), dy


POOL_ROWS = 512


def _pool_window_sum(xw, gi, roll_of):
    s1 = xw + pltpu.roll(xw, roll_of(1), 0)
    s2 = s1 + pltpu.roll(s1, roll_of(2), 0)
    s3 = s2 + pltpu.roll(s2, roll_of(4), 0)
    s4 = s3 + pltpu.roll(s3, roll_of(8), 0)
    return jnp.where(gi == 0, s1, jnp.where(gi == 1, s2, jnp.where(gi == 2, s3, s4)))


def _pool_cnt(i, gi):
    rows = lax.broadcasted_iota(jnp.int32, (POOL_ROWS, 128), 0) + i * POOL_ROWS
    w = jnp.where(gi == 0, 2, jnp.where(gi == 1, 4, jnp.where(gi == 2, 8, 16)))
    return jnp.minimum(rows + 1, w).astype(F32)


def pool_fwd(proj, pool_w, scale):
    S = proj.shape[0]
    nchunk = S // POOL_ROWS
    slab = POOL_ROWS + POOL_HALO

    def body(x_ref, w_ref, sc_ref, o_ref, pad_ref):
        gi = pl.program_id(0)
        pad_ref[0:POOL_HALO, :] = jnp.zeros((POOL_HALO, 128), F32)
        pad_ref[POOL_HALO:, :] = x_ref[...]
        for i in range(nchunk):
            xw = pad_ref[i * POOL_ROWS:i * POOL_ROWS + slab, :]
            ssum = _pool_window_sum(xw, gi, lambda d: d)[POOL_HALO:, :]
            p = ssum / _pool_cnt(i, gi) - xw[POOL_HALO:, :]
            o_ref[i * POOL_ROWS:(i + 1) * POOL_ROWS, :] = (_dot(p, w_ref[...]) * sc_ref[...]).astype(o_ref.dtype)

    return pl.pallas_call(
        body, name="pool_fwd", grid=(4,),
        in_specs=[pl.BlockSpec((S, 128), lambda g: (0, OFF_POOL // 128 + g)),
                  pl.BlockSpec((None, 128, 128), lambda g: (g, 0, 0)),
                  pl.BlockSpec((1, 128), lambda g: (0, g))],
        out_specs=pl.BlockSpec((S, 128), lambda g: (0, g)),
        out_shape=jax.ShapeDtypeStruct((S, W_BRANCH), BF16),
        scratch_shapes=[pltpu.VMEM((S + POOL_HALO, 128), F32)],
        compiler_params=_cp(1),
    )(proj, pool_w, scale)


def pool_bwd(proj, pool_w, scale, dy):
    S = proj.shape[0]
    nchunk = S // POOL_ROWS
    slab = POOL_ROWS + POOL_HALO

    def body(x_ref, w_ref, sc_ref, dy_ref, dx_ref, dw_ref, dsc_ref, pad_ref, pad2_ref, dp_ref):
        gi = pl.program_id(0)
        pad_ref[0:POOL_HALO, :] = jnp.zeros((POOL_HALO, 128), F32)
        pad_ref[POOL_HALO:, :] = x_ref[...]
        pad2_ref[S:, :] = jnp.zeros((POOL_HALO, 128), F32)
        dw = jnp.zeros((128, 128), F32)
        dsc = jnp.zeros((1, 128), F32)
        for i in range(nchunk):
            xw = pad_ref[i * POOL_ROWS:i * POOL_ROWS + slab, :]
            cnt = _pool_cnt(i, gi)
            p = _pool_window_sum(xw, gi, lambda d: d)[POOL_HALO:, :] / cnt - xw[POOL_HALO:, :]
            dyc = dy_ref[i * POOL_ROWS:(i + 1) * POOL_ROWS, :]
            dsc = dsc + jnp.sum(dyc * _dot(p, w_ref[...]), axis=0, keepdims=True)
            dys = dyc * sc_ref[...]
            dw = dw + _dot(p, dys, "tn")
            dp = _dot(dys, w_ref[...], "nt")
            dp_ref[i * POOL_ROWS:(i + 1) * POOL_ROWS, :] = dp
            pad2_ref[i * POOL_ROWS:(i + 1) * POOL_ROWS, :] = dp / cnt
        dw_ref[...] = dw
        dsc_ref[...] = dsc
        for i in range(nchunk):
            xw = pad2_ref[i * POOL_ROWS:i * POOL_ROWS + slab, :]
            fsum = _pool_window_sum(xw, gi, lambda d: slab - d)[:POOL_ROWS, :]
            rows = slice(i * POOL_ROWS, (i + 1) * POOL_ROWS)
            dx_ref[rows, :] = (fsum - dp_ref[rows, :]).astype(dx_ref.dtype)

    return pl.pallas_call(
        body, name="pool_bwd", grid=(4,),
        in_specs=[pl.BlockSpec((S, 128), lambda g: (0, OFF_POOL // 128 + g)),
                  pl.BlockSpec((None, 128, 128), lambda g: (g, 0, 0)),
                  pl.BlockSpec((1, 128), lambda g: (0, g)),
                  pl.BlockSpec((S, 128), lambda g: (0, g))],
        out_specs=[pl.BlockSpec((S, 128), lambda g: (0, g)),
                   pl.BlockSpec((None, 128, 128), lambda g: (g, 0, 0)),
                   pl.BlockSpec((1, 128), lambda g: (0, g))],
        out_shape=[jax.ShapeDtypeStruct((S, W_BRANCH), BF16), jax.ShapeDtypeStruct((4, 128, 128), F32),
                   jax.ShapeDtypeStruct((1, W_BRANCH), F32)],
        scratch_shapes=[pltpu.VMEM((S + POOL_HALO, 128), F32), pltpu.VMEM((S + POOL_HALO, 128), F32),
                        pltpu.VMEM((S, 128), F32)],
        compiler_params=_cp(1),
    )(proj, pool_w, scale, dy)


def _t5_bucket(n):
    exact = REL_BUCKETS // 2
    nf = np.maximum(n, 1).astype(np.float32)
    large = exact + (np.log(nf / exact) / np.log(REL_MAX_DIST / exact) * (REL_BUCKETS - exact)).astype(np.int32)
    large = np.minimum(large, REL_BUCKETS - 1)
    return np.where(n < exact, n, large).astype(np.int32)


def _band_onehot(dil):
    i = np.arange(BAND)[:, None]
    kk = np.arange(2 * BAND)[None, :]
    dist = BAND + i - kk
    local = (dist >= 0) & (dist <= BAND)
    bucket = _t5_bucket(np.clip(dist, 0, BAND) * dil)
    onehot = (bucket.reshape(-1, 1) == np.arange(REL_BUCKETS)[None, :]).astype(np.float32)
    return onehot, local


def att_bias(rel_bias, g, dil):
    onehot, local = _band_onehot(dil)
    tab = jnp.dot(jnp.asarray(onehot), rel_bias[:, g * ATT_HEADS:(g + 1) * ATT_HEADS], precision=lax.Precision.HIGHEST)
    bias = tab.reshape(BAND, 2 * BAND, ATT_HEADS).transpose(2, 0, 1)
    return jnp.where(jnp.asarray(local)[None], bias, NEG_INF)


def att_bias_grad(dbias, dil):
    onehot, _ = _band_onehot(dil)
    flat = dbias.transpose(1, 2, 0).reshape(BAND * 2 * BAND, ATT_HEADS)
    return jnp.dot(jnp.asarray(onehot).T, flat, precision=lax.Precision.HIGHEST)


def _rows(r, d):
    return pl.ds(r, BAND, stride=d) if d > 1 else pl.ds(0, BAND)


def _head_lanes():
    return lax.broadcasted_iota(jnp.int32, (BAND, 128), 1) < ATT_HEAD_DIM


def _att_cols(part, g, hp):
    return (OFF_ATT + part * 3 * W_BRANCH + g * W_BRANCH) // 128 + hp


def att_fwd(proj, bias, g, d):
    S = proj.shape[0]
    ch = BAND * d
    nb = S // ch

    def body(q_ref, kc_ref, kp_ref, vc_ref, vp_ref, b_ref, o_ref, l_ref):
        n = pl.program_id(1)
        head0 = _head_lanes()
        first = jnp.logical_and(lax.broadcasted_iota(jnp.int32, (BAND, 2 * BAND), 1) < BAND, n == 0)
        for r in range(d):
            rows = _rows(r, d)
            q = q_ref[rows, :]
            k = jnp.concatenate([kp_ref[rows, :], kc_ref[rows, :]], axis=0).astype(MXU_DTYPE)
            v = jnp.concatenate([vp_ref[rows, :], vc_ref[rows, :]], axis=0).astype(MXU_DTYPE)
            o_h, l_h = [], []
            for hh in range(2):
                qm = jnp.where(head0 if hh == 0 else jnp.logical_not(head0), q, 0.0)
                s = _dot(qm, k, "nt") * (ATT_HEAD_DIM ** -0.5) + b_ref[hh]
                s = jnp.where(first, NEG_INF, s)
                m = jnp.max(s, axis=-1, keepdims=True)
                p = jnp.exp(s - m)
                l = jnp.sum(p, axis=-1, keepdims=True)
                o_h.append(_dot(p / l, v))
                l_h.append(jnp.broadcast_to(m + jnp.log(l), (BAND, 128)))
            o_ref[rows, :] = jnp.where(head0, o_h[0], o_h[1])
            l_ref[rows, :] = jnp.where(head0, l_h[0], l_h[1])

    def col(part):
        return lambda hp, n: (n, _att_cols(part, g, hp))

    def col_prev(part):
        return lambda hp, n: (jnp.maximum(n - 1, 0), _att_cols(part, g, hp))

    blk = (ch, 128)
    out = pl.BlockSpec(blk, lambda hp, n: (n, hp))
    return pl.pallas_call(
        body, name=f"att_fwd_d{d}", grid=(4, nb),
        in_specs=[pl.BlockSpec(blk, col(0)), pl.BlockSpec(blk, col(1)), pl.BlockSpec(blk, col_prev(1)),
                  pl.BlockSpec(blk, col(2)), pl.BlockSpec(blk, col_prev(2)),
                  pl.BlockSpec((2, BAND, 2 * BAND), lambda hp, n: (hp, 0, 0))],
        out_specs=[out, out],
        out_shape=[jax.ShapeDtypeStruct((S, W_BRANCH), F32), jax.ShapeDtypeStruct((S, W_BRANCH), F32)],
        compiler_params=_cp(2),
    )(proj, proj, proj, proj, proj, bias)


def _att_pair(q, k, v, bias, lse_b, do, delta_b, hh, head0, mask=None):
    sel = head0 if hh == 0 else jnp.logical_not(head0)
    s = _dot(jnp.where(sel, q, 0.0), k, "nt") * (ATT_HEAD_DIM ** -0.5) + bias
    if mask is not None:
        s = jnp.where(mask, NEG_INF, s)
    c = hh * ATT_HEAD_DIM
    p = jnp.exp(s - lse_b[:, c:c + 1])
    dp = _dot(jnp.where(sel, do, 0.0), v, "nt")
    return p, p * (dp - delta_b[:, c:c + 1])


def att_bwd_q(proj, bias, lse, wts, dout, cbar, g, d):
    S = proj.shape[0]
    ch = BAND * d
    nb = S // ch

    def body(q_ref, kc_ref, kp_ref, vc_ref, vp_ref, b_ref, l_ref, w_ref, do_ref, cb_ref, dq_ref, db_ref):
        n = pl.program_id(1)
        head0 = _head_lanes()
        first = jnp.logical_and(lax.broadcasted_iota(jnp.int32, (BAND, 2 * BAND), 1) < BAND, n == 0)

        @pl.when(n == 0)
        def _():
            db_ref[...] = jnp.zeros(db_ref.shape, F32)

        for r in range(d):
            rows = _rows(r, d)
            q = q_ref[rows, :]
            k = jnp.concatenate([kp_ref[rows, :], kc_ref[rows, :]], axis=0).astype(MXU_DTYPE)
            v = jnp.concatenate([vp_ref[rows, :], vc_ref[rows, :]], axis=0).astype(MXU_DTYPE)
            w = w_ref[rows, :]
            do = w * do_ref[rows, :]
            delta = w * cb_ref[rows, :]
            lse_b = l_ref[rows, :]
            dq_h = []
            for hh in range(2):
                _, ds = _att_pair(q, k, v, b_ref[hh], lse_b, do, delta, hh, head0, mask=first)
                db_ref[hh] += ds
                dq_h.append(_dot(ds * (ATT_HEAD_DIM ** -0.5), k))
            dq_ref[rows, :] = jnp.where(head0, dq_h[0], dq_h[1]).astype(dq_ref.dtype)

    def col(part):
        return lambda hp, n: (n, _att_cols(part, g, hp))

    def col_prev(part):
        return lambda hp, n: (jnp.maximum(n - 1, 0), _att_cols(part, g, hp))

    blk = (ch, 128)
    cur = pl.BlockSpec(blk, lambda hp, n: (n, hp))
    bias_spec = pl.BlockSpec((2, BAND, 2 * BAND), lambda hp, n: (hp, 0, 0))
    return pl.pallas_call(
        body, name=f"att_bwd_q_d{d}", grid=(4, nb),
        in_specs=[pl.BlockSpec(blk, col(0)), pl.BlockSpec(blk, col(1)), pl.BlockSpec(blk, col_prev(1)),
                  pl.BlockSpec(blk, col(2)), pl.BlockSpec(blk, col_prev(2)), bias_spec, cur, cur, cur, cur],
        out_specs=[cur, bias_spec],
        out_shape=[jax.ShapeDtypeStruct((S, W_BRANCH), F32), jax.ShapeDtypeStruct((ATT_HEADS, BAND, 2 * BAND), F32)],
        compiler_params=_cp(2),
    )(proj, proj, proj, proj, proj, bias, lse, wts, dout, cbar)


def att_bwd_kv(proj, bias, lse, wts, dout, cbar, g, d):
    S = proj.shape[0]
    ch = BAND * d
    nb = S // ch

    def body(k_ref, v_ref, b_ref, q0_ref, l0_ref, w0_ref, do0_ref, cb0_ref,
             q1_ref, l1_ref, w1_ref, do1_ref, cb1_ref, dk_ref, dv_ref):
        j = pl.program_id(1)
        head0 = _head_lanes()
        has_next = j + 1 < nb
        sides = ((q0_ref, l0_ref, w0_ref, do0_ref, cb0_ref, 1), (q1_ref, l1_ref, w1_ref, do1_ref, cb1_ref, 0))
        for r in range(d):
            rows = _rows(r, d)
            k = k_ref[rows, :].astype(MXU_DTYPE)
            v = v_ref[rows, :].astype(MXU_DTYPE)
            dk = jnp.zeros((BAND, 128), F32)
            dv = jnp.zeros((BAND, 128), F32)
            for q_ref, l_ref, w_ref, do_ref, cb_ref, half in sides:
                q = q_ref[rows, :]
                w = w_ref[rows, :]
                do = w * do_ref[rows, :]
                delta = w * cb_ref[rows, :]
                lse_b = l_ref[rows, :]
                for hh in range(2):
                    sel = head0 if hh == 0 else jnp.logical_not(head0)
                    p, ds = _att_pair(q, k, v, b_ref[hh][:, half * BAND:(half + 1) * BAND], lse_b, do, delta, hh, head0)
                    if half == 0:
                        p = jnp.where(has_next, p, 0.0)
                        ds = jnp.where(has_next, ds, 0.0)
                    dk = dk + jnp.where(sel, _dot(ds * (ATT_HEAD_DIM ** -0.5), q, "tn"), 0.0)
                    dv = dv + jnp.where(sel, _dot(p, do, "tn"), 0.0)
            dk_ref[rows, :] = dk.astype(dk_ref.dtype)
            dv_ref[rows, :] = dv.astype(dv_ref.dtype)

    def col(part):
        return lambda hp, j: (j, _att_cols(part, g, hp))

    blk = (ch, 128)
    cur = pl.BlockSpec(blk, lambda hp, j: (j, hp))
    nxt = pl.BlockSpec(blk, lambda hp, j: (jnp.minimum(j + 1, nb - 1), hp))
    q_next = pl.BlockSpec(blk, lambda hp, j: (jnp.minimum(j + 1, nb - 1), _att_cols(0, g, hp)))
    return pl.pallas_call(
        body, name=f"att_bwd_kv_d{d}", grid=(4, nb),
        in_specs=[pl.BlockSpec(blk, col(1)), pl.BlockSpec(blk, col(2)),
                  pl.BlockSpec((2, BAND, 2 * BAND), lambda hp, j: (hp, 0, 0)),
                  pl.BlockSpec(blk, col(0)), cur, cur, cur, cur, q_next, nxt, nxt, nxt, nxt],
        out_specs=[cur, cur],
        out_shape=[jax.ShapeDtypeStruct((S, W_BRANCH), F32), jax.ShapeDtypeStruct((S, W_BRANCH), F32)],
        compiler_params=_cp(2),
    )(proj, proj, bias, proj, lse, wts, dout, cbar, proj, lse, wts, dout, cbar)


def att_combine(os_, lses):
    S = os_[0].shape[0]

    def body(o0, o1, o2, l0, l1, l2, out_ref, w0, w1, w2):
        ls = [l0[...], l1[...], l2[...]]
        m = jnp.maximum(jnp.maximum(ls[0], ls[1]), ls[2])
        es = [jnp.exp(l - m) for l in ls]
        den = es[0] + es[1] + es[2]
        ws = [e / den for e in es]
        out_ref[...] = (ws[0] * o0[...] + ws[1] * o1[...] + ws[2] * o2[...]).astype(out_ref.dtype)
        for w_ref, w in zip((w0, w1, w2), ws):
            w_ref[...] = w

    blk = pl.BlockSpec((ROW_TILE, W_BRANCH), lambda i: (i, 0))
    f = jax.ShapeDtypeStruct((S, W_BRANCH), F32)
    return pl.pallas_call(
        body, name="att_combine", grid=(S // ROW_TILE,), in_specs=[blk] * 6, out_specs=[blk] * 4,
        out_shape=[jax.ShapeDtypeStruct((S, W_BRANCH), BF16), f, f, f], compiler_params=_cp(1),
    )(*os_, *lses)


def _split3(x):
    x1 = x.astype(BF16)
    r1 = x - x1.astype(F32)
    x2 = r1.astype(BF16)
    x3 = (r1 - x2.astype(F32)).astype(BF16)
    return x1, x2, x3


def att_combine_bwd(dout, os_, wts, headsum):
    S = dout.shape[0]

    def body(do_ref, o0, o1, o2, w0, w1, w2, e_ref, cb_ref):
        out = w0[...] * o0[...] + w1[...] * o1[...] + w2[...] * o2[...]
        e = e_ref[...]
        acc = jnp.zeros((ROW_TILE, W_BRANCH), F32)
        for term in _split3(do_ref[...] * out):
            acc = acc + jnp.dot(term, e, preferred_element_type=F32)
        cb_ref[...] = acc

    blk = pl.BlockSpec((ROW_TILE, W_BRANCH), lambda i: (i, 0))
    return pl.pallas_call(
        body, name="att_combine_bwd", grid=(S // ROW_TILE,),
        in_specs=[blk] * 7 + [pl.BlockSpec((W_BRANCH, W_BRANCH), lambda i: (0, 0))], out_specs=blk,
        out_shape=jax.ShapeDtypeStruct((S, W_BRANCH), F32), compiler_params=_cp(1),
    )(dout, *os_, *wts, headsum)


def _cmul(ar, ai, br, bi):
    return ar * br - ai * bi, ar * bi + ai * br


def _scan_steps():
    return int(math.log2(SSM_T))


def s5_fwd(proj, b_re, b_im, a_re, a_im, c_re, c_im, d_skip):
    S = proj.shape[0]
    nt = S // SSM_T

    def body(u_ref, bre_ref, bim_ref, ar_ref, ai_ref, cre_ref, cim_ref, dsk_ref, hr_ref, hi_ref, y_ref, cr_ref, ci_ref):
        t = pl.program_id(1)

        @pl.when(t == 0)
        def _():
            cr_ref[...] = jnp.zeros(cr_ref.shape, F32)
            ci_ref[...] = jnp.zeros(ci_ref.shape, F32)

        u = u_ref[...]
        ar, ai = ar_ref[...], ai_ref[...]
        rows = lax.broadcasted_iota(jnp.int32, (SSM_T, W_BRANCH), 0)
        inr, ini = _cmul(ar, ai, cr_ref[0:1, :], ci_ref[0:1, :])
        xr = _dot(u, bre_ref[...]) + jnp.where(rows == 0, inr, 0.0)
        xi = _dot(u, bim_ref[...]) + jnp.where(rows == 0, ini, 0.0)
        pr, pi = ar, ai
        for k in range(_scan_steps()):
            dd = 1 << k
            sr = jnp.where(rows >= dd, pltpu.roll(xr, dd, 0), 0.0)
            si = jnp.where(rows >= dd, pltpu.roll(xi, dd, 0), 0.0)
            mr, mi = _cmul(pr, pi, sr, si)
            xr, xi = xr + mr, xi + mi
            pr, pi = _cmul(pr, pi, pr, pi)
        hr_ref[...] = xr
        hi_ref[...] = xi
        cr_ref[...] = jnp.broadcast_to(xr[SSM_T - 1:SSM_T, :], cr_ref.shape)
        ci_ref[...] = jnp.broadcast_to(xi[SSM_T - 1:SSM_T, :], ci_ref.shape)
        y_ref[...] = _dot(xr, cre_ref[...]) - _dot(xi, cim_ref[...]) + u * dsk_ref[...]

    u_spec = pl.BlockSpec((SSM_T, 128), lambda j, t: (t, OFF_SSM // 128 + j))
    b_spec = pl.BlockSpec((None, 128, W_BRANCH), lambda j, t: (j, 0, 0))
    a_spec = pl.BlockSpec((1, W_BRANCH), lambda j, t: (0, j))
    c_spec = pl.BlockSpec((None, W_BRANCH, 128), lambda j, t: (j, 0, 0))
    h_spec = pl.BlockSpec((SSM_T, W_BRANCH), lambda j, t: (t, j))
    return pl.pallas_call(
        body, name="s5_fwd", grid=(4, nt),
        in_specs=[u_spec, b_spec, b_spec, a_spec, a_spec, c_spec, c_spec, pl.BlockSpec((1, 128), lambda j, t: (0, j))],
        out_specs=[h_spec, h_spec, pl.BlockSpec((SSM_T, 128), lambda j, t: (t, j))],
        out_shape=[jax.ShapeDtypeStruct((S, SSM_COLS), F32), jax.ShapeDtypeStruct((S, SSM_COLS), F32),
                   jax.ShapeDtypeStruct((S, W_BRANCH), F32)],
        scratch_shapes=[pltpu.VMEM((8, W_BRANCH), F32), pltpu.VMEM((8, W_BRANCH), F32)],
        compiler_params=_cp(2),
    )(proj, b_re, b_im, a_re, a_im, c_re, c_im, d_skip)


def s5_bwd(proj, hr, hi, dy, b_re, b_im, a_re, a_im, c_re, c_im, d_skip):
    S = proj.shape[0]
    nt = S // SSM_T

    def body(u_ref, hr_ref, hi_ref, hpr_ref, hpi_ref, dy_ref, bre_ref, bim_ref, ar_ref, ai_ref, cre_ref, cim_ref,
             dsk_ref, du_ref, dbre_ref, dbim_ref, dar_ref, dai_ref, dcre_ref, dcim_ref, ddsk_ref, gr_ref, gi_ref):
        step = pl.program_id(1)
        t = nt - 1 - step

        @pl.when(step == 0)
        def _():
            gr_ref[...] = jnp.zeros(gr_ref.shape, F32)
            gi_ref[...] = jnp.zeros(gi_ref.shape, F32)
            for ref in (dbre_ref, dbim_ref, dar_ref, dai_ref, dcre_ref, dcim_ref, ddsk_ref):
                ref[...] = jnp.zeros(ref.shape, F32)

        u = u_ref[...]
        dy = dy_ref[...]
        ar, ai = ar_ref[...], ai_ref[...]
        rows = lax.broadcasted_iota(jnp.int32, (SSM_T, W_BRANCH), 0)
        inr, ini = _cmul(ar, -ai, gr_ref[0:1, :], gi_ref[0:1, :])
        xr = _dot(dy, cre_ref[...], "nt") + jnp.where(rows == SSM_T - 1, inr, 0.0)
        xi = -_dot(dy, cim_ref[...], "nt") + jnp.where(rows == SSM_T - 1, ini, 0.0)
        pr, pi = ar, -ai
        for k in range(_scan_steps()):
            dd = 1 << k
            sr = jnp.where(rows < SSM_T - dd, pltpu.roll(xr, SSM_T - dd, 0), 0.0)
            si = jnp.where(rows < SSM_T - dd, pltpu.roll(xi, SSM_T - dd, 0), 0.0)
            mr, mi = _cmul(pr, pi, sr, si)
            xr, xi = xr + mr, xi + mi
            pr, pi = _cmul(pr, pi, pr, pi)
        gr_ref[...] = jnp.broadcast_to(xr[0:1, :], gr_ref.shape)
        gi_ref[...] = jnp.broadcast_to(xi[0:1, :], gi_ref.shape)
        hr_blk, hi_blk = hr_ref[...], hi_ref[...]
        keep = (t > 0).astype(F32)
        hpr = jnp.where(rows >= 1, pltpu.roll(hr_blk, 1, 0), hpr_ref[7:8, :] * keep)
        hpi = jnp.where(rows >= 1, pltpu.roll(hi_blk, 1, 0), hpi_ref[7:8, :] * keep)
        dar_ref[...] += jnp.sum(hpr * xr + hpi * xi, axis=0, keepdims=True)
        dai_ref[...] += jnp.sum(hpr * xi - hpi * xr, axis=0, keepdims=True)
        dcre_ref[...] += _dot(hr_blk, dy, "tn")
        dcim_ref[...] -= _dot(hi_blk, dy, "tn")
        du = dy * dsk_ref[...] + _dot(xr, bre_ref[...], "nt") + _dot(xi, bim_ref[...], "nt")
        du_ref[...] = du.astype(du_ref.dtype)
        dbre_ref[...] += _dot(u, xr, "tn")
        dbim_ref[...] += _dot(u, xi, "tn")
        ddsk_ref[...] += jnp.sum(dy * u, axis=0, keepdims=True)

    def rev(t):
        return nt - 1 - t

    u_spec = pl.BlockSpec((SSM_T, 128), lambda j, t: (rev(t), OFF_SSM // 128 + j))
    h_spec = pl.BlockSpec((SSM_T, W_BRANCH), lambda j, t: (rev(t), j))
    hprev_spec = pl.BlockSpec((8, W_BRANCH), lambda j, t: (jnp.maximum(rev(t) * (SSM_T // 8) - 1, 0), j))
    ch_spec = pl.BlockSpec((SSM_T, 128), lambda j, t: (rev(t), j))
    b_spec = pl.BlockSpec((None, 128, W_BRANCH), lambda j, t: (j, 0, 0))
    a_spec = pl.BlockSpec((1, W_BRANCH), lambda j, t: (0, j))
    c_spec = pl.BlockSpec((None, W_BRANCH, 128), lambda j, t: (j, 0, 0))
    d_spec = pl.BlockSpec((1, 128), lambda j, t: (0, j))
    return pl.pallas_call(
        body, name="s5_bwd", grid=(4, nt),
        in_specs=[u_spec, h_spec, h_spec, hprev_spec, hprev_spec, ch_spec, b_spec, b_spec, a_spec, a_spec,
                  c_spec, c_spec, d_spec],
        out_specs=[ch_spec, b_spec, b_spec, a_spec, a_spec, c_spec, c_spec, d_spec],
        out_shape=[jax.ShapeDtypeStruct((S, W_BRANCH), BF16),
                   jax.ShapeDtypeStruct((4, 128, W_BRANCH), F32), jax.ShapeDtypeStruct((4, 128, W_BRANCH), F32),
                   jax.ShapeDtypeStruct((1, SSM_COLS), F32), jax.ShapeDtypeStruct((1, SSM_COLS), F32),
                   jax.ShapeDtypeStruct((4, W_BRANCH, 128), F32), jax.ShapeDtypeStruct((4, W_BRANCH, 128), F32),
                   jax.ShapeDtypeStruct((1, W_BRANCH), F32)],
        scratch_shapes=[pltpu.VMEM((8, W_BRANCH), F32), pltpu.VMEM((8, W_BRANCH), F32)],
        compiler_params=_cp(2),
    )(proj, hr, hi, hr, hi, dy, b_re, b_im, a_re, a_im, c_re, c_im, d_skip)


def glu_fwd(y, w_glu, b_glu):
    S = y.shape[0]

    def body(y_ref, w_ref, b_ref, o_ref):
        g = _gelu(y_ref[...])
        o_ref[...] = (g * _sigmoid(_dot(g, w_ref[...]) + b_ref[...])).astype(o_ref.dtype)

    blk = pl.BlockSpec((ROW_TILE, W_BRANCH), lambda i: (i, 0))
    return pl.pallas_call(
        body, name="glu_fwd", grid=(S // ROW_TILE,),
        in_specs=[blk, pl.BlockSpec((W_BRANCH, W_BRANCH), lambda i: (0, 0)), pl.BlockSpec((1, W_BRANCH), lambda i: (0, 0))],
        out_specs=blk, out_shape=jax.ShapeDtypeStruct((S, W_BRANCH), BF16), compiler_params=_cp(1),
    )(y, w_glu, b_glu)


def glu_bwd(y, w_glu, b_glu, dout):
    S = y.shape[0]

    def body(y_ref, w_ref, b_ref, do_ref, dy_ref, dw_ref, db_ref):
        yv = y_ref[...]
        do = do_ref[...]
        g = _gelu(yv)
        s = _sigmoid(_dot(g, w_ref[...]) + b_ref[...])
        dz = do * g * s * (1.0 - s)
        dg = do * s + _dot(dz, w_ref[...], "nt")
        dy_ref[...] = dg * _gelu_grad(yv)
        dw = _dot(g, dz, "tn")
        db = jnp.sum(dz, axis=0, keepdims=True)

        @pl.when(pl.program_id(0) == 0)
        def _():
            dw_ref[...] = dw
            db_ref[...] = db

        @pl.when(pl.program_id(0) > 0)
        def _():
            dw_ref[...] += dw
            db_ref[...] += db

    blk = pl.BlockSpec((ROW_TILE, W_BRANCH), lambda i: (i, 0))
    mat = pl.BlockSpec((W_BRANCH, W_BRANCH), lambda i: (0, 0))
    vec = pl.BlockSpec((1, W_BRANCH), lambda i: (0, 0))
    return pl.pallas_call(
        body, name="glu_bwd", grid=(S // ROW_TILE,), in_specs=[blk, mat, vec, blk], out_specs=[blk, mat, vec],
        out_shape=[jax.ShapeDtypeStruct((S, W_BRANCH), F32), jax.ShapeDtypeStruct((W_BRANCH, W_BRANCH), F32),
                   jax.ShapeDtypeStruct((1, W_BRANCH), F32)],
        compiler_params=_cp(1),
    )(y, w_glu, b_glu, dout)


SGU_TILE = 512
SGU_U_BLOCK = OFF_SGU // W_BRANCH
SGU_V_BLOCK = SGU_U_BLOCK + 1


def _sgu_norm(zv):
    v = _gelu(zv)
    mu = jnp.mean(v, axis=-1, keepdims=True)
    vc = v - mu
    rstd = lax.rsqrt(jnp.mean(vc * vc, axis=-1, keepdims=True) + EPS)
    return vc * rstd, rstd


def _tril():
    return lax.broadcasted_iota(jnp.int32, (SGU_CHUNK, SGU_CHUNK), 0) >= lax.broadcasted_iota(jnp.int32, (SGU_CHUNK, SGU_CHUNK), 1)


def sgu_fwd(proj, ln_g, ln_b, w_s, b_s_t):
    S = proj.shape[0]

    def body(zu_ref, zv_ref, g_ref, b_ref, ws_ref, bs_ref, o_ref, vf_ref):
        vn, _ = _sgu_norm(zv_ref[...])
        vf_ref[...] = vn * g_ref[...] + b_ref[...]
        tri = _tril()
        for gi in range(4):
            ws = jnp.where(tri, ws_ref[gi], 0.0)
            cols = slice(gi * 128, (gi + 1) * 128)
            for c in range(SGU_TILE // SGU_CHUNK):
                rows = slice(c * SGU_CHUNK, (c + 1) * SGU_CHUNK)
                sv = _dot(ws, vf_ref[rows, cols]) + bs_ref[:, gi:gi + 1]
                o_ref[rows, cols] = (_gelu(zu_ref[rows, cols]) * sv).astype(o_ref.dtype)

    blk = lambda cb: pl.BlockSpec((SGU_TILE, W_BRANCH), lambda i: (i, cb))
    vec = pl.BlockSpec((1, W_BRANCH), lambda i: (0, 0))
    return pl.pallas_call(
        body, name="sgu_fwd", grid=(S // SGU_TILE,),
        in_specs=[blk(SGU_U_BLOCK), blk(SGU_V_BLOCK), vec, vec, pl.BlockSpec((4, SGU_CHUNK, SGU_CHUNK), lambda i: (0, 0, 0)),
                  pl.BlockSpec((SGU_CHUNK, 4), lambda i: (0, 0))],
        out_specs=blk(0), out_shape=jax.ShapeDtypeStruct((S, W_BRANCH), BF16),
        scratch_shapes=[pltpu.VMEM((SGU_TILE, W_BRANCH), F32)], compiler_params=_cp(1),
    )(proj, proj, ln_g, ln_b, w_s, b_s_t)


def sgu_bwd(proj, ln_g, ln_b, w_s, b_s_t, dout):
    S = proj.shape[0]

    def body(zu_ref, zv_ref, g_ref, b_ref, ws_ref, bs_ref, do_ref, dzu_ref, dzv_ref, dg_ref, db_ref, dws_ref, dbs_ref,
             vf_ref, dvf_ref):
        @pl.when(pl.program_id(0) == 0)
        def _():
            for ref in (dg_ref, db_ref, dws_ref, dbs_ref):
                ref[...] = jnp.zeros(ref.shape, F32)

        vn, rstd = _sgu_norm(zv_ref[...])
        vf_ref[...] = vn * g_ref[...] + b_ref[...]
        tri = _tril()
        lane = lax.broadcasted_iota(jnp.int32, (SGU_CHUNK, 128), 1)
        dbs = jnp.zeros((SGU_CHUNK, 128), F32)
        for gi in range(4):
            ws = jnp.where(tri, ws_ref[gi], 0.0)
            cols = slice(gi * 128, (gi + 1) * 128)
            dws = jnp.zeros((SGU_CHUNK, SGU_CHUNK), F32)
            for c in range(SGU_TILE // SGU_CHUNK):
                rows = slice(c * SGU_CHUNK, (c + 1) * SGU_CHUNK)
                vf = vf_ref[rows, cols]
                zu = zu_ref[rows, cols]
                do = do_ref[rows, cols]
                sv = _dot(ws, vf) + bs_ref[:, gi:gi + 1]
                dzu_ref[rows, cols] = (do * sv * _gelu_grad(zu)).astype(dzu_ref.dtype)
                dsv = do * _gelu(zu)
                dvf_ref[rows, cols] = _dot(ws, dsv, "tn")
                dws = dws + _dot(dsv, vf, "nt")
                dbs = dbs + jnp.where(lane == gi, jnp.sum(dsv, axis=-1, keepdims=True), 0.0)
            dws_ref[gi] += jnp.where(tri, dws, 0.0)
        dbs_ref[...] += dbs
        dvf = dvf_ref[...]
        dg_ref[...] += jnp.sum(dvf * vn, axis=0, keepdims=True)
        db_ref[...] += jnp.sum(dvf, axis=0, keepdims=True)
        dvn = dvf * g_ref[...]
        dv = rstd * (dvn - jnp.mean(dvn, axis=-1, keepdims=True) - vn * jnp.mean(dvn * vn, axis=-1, keepdims=True))
        dzv_ref[...] = (dv * _gelu_grad(zv_ref[...])).astype(dzv_ref.dtype)

    blk = lambda cb: pl.BlockSpec((SGU_TILE, W_BRANCH), lambda i: (i, cb))
    vec = pl.BlockSpec((1, W_BRANCH), lambda i: (0, 0))
    ws_spec = pl.BlockSpec((4, SGU_CHUNK, SGU_CHUNK), lambda i: (0, 0, 0))
    return pl.pallas_call(
        body, name="sgu_bwd", grid=(S // SGU_TILE,),
        in_specs=[blk(SGU_U_BLOCK), blk(SGU_V_BLOCK), vec, vec, ws_spec, pl.BlockSpec((SGU_CHUNK, 4), lambda i: (0, 0)),
                  blk(0)],
        out_specs=[blk(0), blk(0), vec, vec, ws_spec, pl.BlockSpec((SGU_CHUNK, 128), lambda i: (0, 0))],
        out_shape=[jax.ShapeDtypeStruct((S, W_BRANCH), BF16), jax.ShapeDtypeStruct((S, W_BRANCH), BF16),
                   jax.ShapeDtypeStruct((1, W_BRANCH), F32), jax.ShapeDtypeStruct((1, W_BRANCH), F32),
                   jax.ShapeDtypeStruct((4, SGU_CHUNK, SGU_CHUNK), F32), jax.ShapeDtypeStruct((SGU_CHUNK, 128), F32)],
        scratch_shapes=[pltpu.VMEM((SGU_TILE, W_BRANCH), F32), pltpu.VMEM((SGU_TILE, W_BRANCH), F32)],
        compiler_params=_cp(1),
    )(proj, proj, ln_g, ln_b, w_s, b_s_t, dout)


GM_TILE = 512


def _gate_specs(order):
    def spec(i):
        def index(*ids):
            m, n = order(*ids)
            return (m, (OFF_GATE + i * D_MODEL) // GM_TILE + n)
        return pl.BlockSpec((GM_TILE, GM_TILE), index)
    return [spec(i) for i in range(4)]


def merge_fwd(proj, gate_b, branches, w_up):
    S = proj.shape[0]
    order = lambda n, m: (m, n)

    def body(p0, p1, p2, p3, gb_ref, b0, b1, b2, b3, w_ref, o_ref):
        acc = jnp.zeros((GM_TILE, GM_TILE), F32)
        for i, (p_ref, br_ref) in enumerate(zip((p0, p1, p2, p3), (b0, b1, b2, b3))):
            acc = acc + _sigmoid(p_ref[...] + gb_ref[i:i + 1, :]) * _dot(br_ref[...], w_ref[i])
        o_ref[...] = acc.astype(o_ref.dtype)

    br_spec = pl.BlockSpec((GM_TILE, W_BRANCH), lambda n, m: (m, 0))
    return pl.pallas_call(
        body, name="merge_fwd", grid=(D_MODEL // GM_TILE, S // GM_TILE),
        in_specs=_gate_specs(order) + [pl.BlockSpec((4, GM_TILE), lambda n, m: (0, n))] + [br_spec] * 4
        + [pl.BlockSpec((4, W_BRANCH, GM_TILE), lambda n, m: (0, 0, n))],
        out_specs=pl.BlockSpec((GM_TILE, GM_TILE), lambda n, m: (m, n)),
        out_shape=jax.ShapeDtypeStruct((S, D_MODEL), BF16), compiler_params=_cp(2),
    )(proj, proj, proj, proj, gate_b, *branches, w_up)


def merge_bwd(proj, gate_b, branches, w_up, dmerged):
    S = proj.shape[0]
    order = lambda n, m: (m, n)

    def body(p0, p1, p2, p3, gb_ref, b0, b1, b2, b3, w_ref, dm_ref, dp0, dp1, dp2, dp3, du0, du1, du2, du3, dgb_ref):
        dm = dm_ref[...]
        dgb = []
        for i, (p_ref, br_ref, dp_ref, du_ref) in enumerate(
                zip((p0, p1, p2, p3), (b0, b1, b2, b3), (dp0, dp1, dp2, dp3), (du0, du1, du2, du3))):
            gate = _sigmoid(p_ref[...] + gb_ref[i:i + 1, :])
            dpre = dm * _dot(br_ref[...], w_ref[i]) * gate * (1.0 - gate)
            dp_ref[...] = dpre.astype(dp_ref.dtype)
            du_ref[...] = (dm * gate).astype(du_ref.dtype)
            dgb.append(jnp.sum(dpre, axis=0, keepdims=True))
        dgb = jnp.concatenate(dgb, axis=0)

        @pl.when(pl.program_id(1) == 0)
        def _():
            dgb_ref[...] = dgb

        @pl.when(pl.program_id(1) > 0)
        def _():
            dgb_ref[...] += dgb

    br_spec = pl.BlockSpec((GM_TILE, W_BRANCH), lambda n, m: (m, 0))
    mn = pl.BlockSpec((GM_TILE, GM_TILE), lambda n, m: (m, n))
    gb = pl.BlockSpec((4, GM_TILE), lambda n, m: (0, n))
    big = jax.ShapeDtypeStruct((S, D_MODEL), BF16)
    outs = pl.pallas_call(
        body, name="merge_bwd", grid=(D_MODEL // GM_TILE, S // GM_TILE),
        in_specs=_gate_specs(order) + [gb] + [br_spec] * 4
        + [pl.BlockSpec((4, W_BRANCH, GM_TILE), lambda n, m: (0, 0, n)), mn],
        out_specs=[mn] * 8 + [gb], out_shape=[big] * 8 + [jax.ShapeDtypeStruct((4, D_MODEL), F32)],
        compiler_params=_cp(2),
    )(proj, proj, proj, proj, gate_b, *branches, w_up, dmerged)
    return outs[0:4], outs[4:8], outs[8]


def _xatt_probs(q, k):
    s = _dot(q, k, "nt") * (X_HEAD_DIM ** -0.5)
    p = jnp.exp(s - jnp.max(s, axis=-1, keepdims=True))
    return p / jnp.sum(p, axis=-1, keepdims=True)


def xatt_fwd(q, kv):
    S = q.shape[0]

    def body(q_ref, kv_ref, o_ref):
        for h in range(X_HEADS):
            cols = slice(h * X_HEAD_DIM, (h + 1) * X_HEAD_DIM)
            p = _xatt_probs(q_ref[:, cols], kv_ref[:, cols])
            o_ref[:, cols] = _dot(p, kv_ref[:, W_BRANCH + h * X_HEAD_DIM:W_BRANCH + (h + 1) * X_HEAD_DIM]).astype(o_ref.dtype)

    blk = pl.BlockSpec((ROW_TILE, W_BRANCH), lambda i: (i, 0))
    return pl.pallas_call(
        body, name="xatt_fwd", grid=(S // ROW_TILE,),
        in_specs=[blk, pl.BlockSpec((N_MEM, 2 * W_BRANCH), lambda i: (0, 0))], out_specs=blk,
        out_shape=jax.ShapeDtypeStruct((S, W_BRANCH), BF16), compiler_params=_cp(1),
    )(q, kv)


def xatt_bwd(q, kv, do):
    S = q.shape[0]

    def body(q_ref, kv_ref, do_ref, dq_ref, dkv_ref):
        @pl.when(pl.program_id(0) == 0)
        def _():
            dkv_ref[...] = jnp.zeros(dkv_ref.shape, F32)

        for h in range(X_HEADS):
            cols = slice(h * X_HEAD_DIM, (h + 1) * X_HEAD_DIM)
            vcols = slice(W_BRANCH + h * X_HEAD_DIM, W_BRANCH + (h + 1) * X_HEAD_DIM)
            qh, kh, doh = q_ref[:, cols], kv_ref[:, cols], do_ref[:, cols]
            p = _xatt_probs(qh, kh)
            dp = _dot(doh, kv_ref[:, vcols], "nt")
            ds = p * (dp - jnp.sum(dp * p, axis=-1, keepdims=True)) * (X_HEAD_DIM ** -0.5)
            dq_ref[:, cols] = _dot(ds, kh).astype(dq_ref.dtype)
            dkv_ref[:, cols] += _dot(ds, qh, "tn")
            dkv_ref[:, vcols] += _dot(p, doh, "tn")

    blk = pl.BlockSpec((ROW_TILE, W_BRANCH), lambda i: (i, 0))
    kv_spec = pl.BlockSpec((N_MEM, 2 * W_BRANCH), lambda i: (0, 0))
    return pl.pallas_call(
        body, name="xatt_bwd", grid=(S // ROW_TILE,), in_specs=[blk, kv_spec, blk], out_specs=[blk, kv_spec],
        out_shape=[jax.ShapeDtypeStruct((S, W_BRANCH), BF16), jax.ShapeDtypeStruct((N_MEM, 2 * W_BRANCH), F32)],
        compiler_params=_cp(1),
    )(q, kv, do)


def s5_params(a_re, a_im, log_dt, b_re, b_im, c_re, c_im):
    lam_re = jnp.minimum(a_re, -1e-4)
    lam_im = a_im
    dt = jnp.exp(log_dt)[:, None]
    mag = jnp.exp(lam_re * dt)
    ab_re, ab_im = mag * jnp.cos(lam_im * dt), mag * jnp.sin(lam_im * dt)
    den = lam_re * lam_re + lam_im * lam_im
    f_re = ((ab_re - 1.0) * lam_re + ab_im * lam_im) / den
    f_im = (ab_im * lam_re - (ab_re - 1.0) * lam_im) / den
    bb_re = f_re[..., None] * b_re - f_im[..., None] * b_im
    bb_im = f_re[..., None] * b_im + f_im[..., None] * b_re
    eye = jnp.eye(8, dtype=F32)

    def b_blocks(bb):
        t = bb.reshape(4, 8, SSM_STATE, SSM_GROUP).transpose(0, 1, 3, 2)
        return (t[:, :, :, None, :] * eye[None, :, None, :, None]).reshape(4, 128, W_BRANCH)

    def c_blocks(cc):
        t = cc.reshape(4, 8, SSM_GROUP, SSM_STATE).transpose(0, 1, 3, 2)
        return (t[:, :, :, None, :] * eye[None, :, None, :, None]).reshape(4, W_BRANCH, 128)

    return (ab_re.reshape(1, SSM_COLS), ab_im.reshape(1, SSM_COLS), b_blocks(bb_re), b_blocks(bb_im),
            c_blocks(c_re), c_blocks(c_im))


ANY = pl.BlockSpec(memory_space=pl.ANY)


def _chip_index():
    return 2 * lax.axis_index("x") + lax.axis_index("y")


def _peer_chip(j):
    x, y, c = lax.axis_index("x"), lax.axis_index("y"), lax.axis_index("c")
    return ((1 - x) if j & 2 else x, (1 - y) if j & 1 else y, c)


def _piece(ref, axis, s, n):
    size = ref.shape[axis] // n
    idx = [slice(None)] * len(ref.shape)
    idx[axis] = pl.ds(s * size, size)
    return ref.at[tuple(idx)]


HBM_SPEC = pl.BlockSpec(memory_space=pltpu.HBM)
SEM_SPEC = pl.BlockSpec(memory_space=pltpu.SEMAPHORE)
SIDE_EFFECT = pltpu.SideEffectType.DATAFLOW_SIDE_EFFECTING


def _chip_copies(ins, lands, send, recv, axes, mode, k):
    pairs = []
    for t in range(len(ins)):
        for j in (1, 2, 3):
            if mode == "gather":
                src, dst, arrives = ins[t], _piece(lands[t], axes[t], k, 4), _piece(lands[t], axes[t], k ^ j, 4)
            else:
                src = ins[t] if axes[t] is None else _piece(ins[t], axes[t], k ^ j, 4)
                dst, arrives = lands[t].at[k], lands[t].at[k ^ j]
            sems = dict(send_sem=send.at[3 * t + j - 1], recv_sem=recv.at[3 * t + j - 1], device_id=_peer_chip(j),
                        device_id_type=MESH_ID)
            pairs.append((pltpu.make_async_remote_copy(src_ref=src, dst_ref=dst, **sems),
                          pltpu.make_async_remote_copy(src_ref=src, dst_ref=arrives, **sems)))
    return pairs


def chips_start(ins, lands, axes, mode, name, after=()):
    n, na = len(ins), len(after)

    def body(*refs):
        in_refs, land_refs = refs[:n], refs[n:2 * n]
        send, recv, token = refs[2 * n + na], refs[2 * n + na + 1], refs[-1]
        q = _chip_index()
        for k in range(4):
            @pl.when(q == k)
            def _():
                for start, _ in _chip_copies(in_refs, land_refs, send, recv, axes, mode, k):
                    start.start()
        token[...] = jnp.zeros(token.shape, token.dtype)

    hbm = lambda a: pltpu.HBM(a.shape, a.dtype)
    outs = pl.pallas_call(
        body, name=name, in_specs=[HBM_SPEC] * (2 * n) + [ANY] * na,
        out_specs=[SEM_SPEC, SEM_SPEC] + [HBM_SPEC] * (2 * n) + [pl.BlockSpec(memory_space=pltpu.VMEM)],
        out_shape=[pltpu.SemaphoreType.DMA((3 * n,)), pltpu.SemaphoreType.DMA((3 * n,))]
        + [hbm(a) for a in ins] + [hbm(a) for a in lands] + [jax.ShapeDtypeStruct((8, 128), F32)],
        input_output_aliases={i: 2 + i for i in range(2 * n)},
        compiler_params=pltpu.CompilerParams(has_side_effects=SIDE_EFFECT),
    )(*[pltpu.with_memory_space_constraint(a, pltpu.HBM) for a in list(ins) + list(lands)], *after)
    return outs[0], outs[1], outs[2:2 + n], outs[2 + n:2 + 2 * n], outs[-1]


def chips_wait(send, recv, ins, lands, axes, mode, name, after=()):
    n = len(ins)

    def body(*refs):
        in_refs, land_refs = refs[:n], refs[n:2 * n]
        send_ref, recv_ref = refs[2 * n], refs[2 * n + 1]
        q = _chip_index()
        for k in range(4):
            @pl.when(q == k)
            def _():
                for _, wait in _chip_copies(in_refs, land_refs, send_ref, recv_ref, axes, mode, k):
                    wait.wait_send()
                    wait.wait_recv()

    hbm = lambda a: pltpu.HBM(a.shape, a.dtype)
    outs = pl.pallas_call(
        body, name=name, in_specs=[HBM_SPEC] * (2 * n) + [SEM_SPEC, SEM_SPEC] + [ANY] * len(after),
        out_specs=[HBM_SPEC] * (2 * n), out_shape=[hbm(a) for a in ins] + [hbm(a) for a in lands],
        input_output_aliases={i: i for i in range(2 * n)},
        compiler_params=pltpu.CompilerParams(has_side_effects=SIDE_EFFECT),
    )(*ins, *lands, send, recv, *after)
    return outs[:n], outs[n:]


def swap_cores(arrs):
    n = len(arrs)

    def body(*refs):
        ins, outs = refs[:n], refs[n:2 * n]
        send, recv = refs[2 * n:]
        sibling = (lax.axis_index("x"), lax.axis_index("y"), 1 - lax.axis_index("c"))
        copies = [pltpu.make_async_remote_copy(src_ref=ins[t], dst_ref=outs[t], send_sem=send.at[t], recv_sem=recv.at[t],
                                               device_id=sibling, device_id_type=MESH_ID) for t in range(n)]
        for cp in copies:
            cp.start()
        for cp in copies:
            cp.wait()

    return pl.pallas_call(
        body, name="swap_cores", in_specs=[ANY] * n, out_specs=[ANY] * n,
        out_shape=[jax.ShapeDtypeStruct(a.shape, a.dtype) for a in arrs],
        scratch_shapes=[pltpu.SemaphoreType.DMA((n,)), pltpu.SemaphoreType.DMA((n,))],
    )(*arrs)


ELEMENTWISE_BLOCK_BYTES = 1 << 20


def _row_tile(rows, cols):
    want = max(8, ELEMENTWISE_BLOCK_BYTES // (4 * cols))
    fits = [t for t in range(8, min(rows, want) + 1, 8) if rows % t == 0]
    return fits[-1] if fits else rows


def sum_chips(recv, own, axis, chip, stacked, l, name):
    _, r, c = recv.shape
    tr = _row_tile(r, c)
    nrt = r // tr

    def body(chip_ref, r_ref, own_ref, stacked_ref, o_ref):
        for k in range(4):
            @pl.when(chip_ref[0] == k)
            def _():
                terms = [own_ref[...] if s == k else r_ref[s] for s in range(4)]
                o_ref[...] = ((terms[0] + terms[1]) + terms[2]) + terms[3]

    own_index = {0: lambda i, q: (q[0] * nrt + i, 0), 1: lambda i, q: (i, q[0]), None: lambda i, q: (i, 0)}[axis]
    return pl.pallas_call(
        body, name=name,
        grid_spec=pltpu.PrefetchScalarGridSpec(
            num_scalar_prefetch=1, grid=(nrt,),
            in_specs=[pl.BlockSpec((4, tr, c), lambda i, q: (0, i, 0)), pl.BlockSpec((tr, c), own_index), ANY],
            out_specs=pl.BlockSpec((None, tr, c), lambda i, q: (l, i, 0))),
        out_shape=jax.ShapeDtypeStruct(stacked.shape, F32), input_output_aliases={3: 0}, compiler_params=_cp(1),
    )(chip, recv, own, stacked)


def adamw(w, ga, gb, m, v, name):
    rows, cols = w.shape
    tr = _row_tile(rows, cols)

    def body(w_ref, ga_ref, gb_ref, m_ref, v_ref, g_ref, d_ref, nm_ref, nv_ref):
        g = ga_ref[...] + gb_ref[...]
        nm = ADAM_B1 * m_ref[...] + (1.0 - ADAM_B1) * g
        nv = ADAM_B2 * v_ref[...] + (1.0 - ADAM_B2) * (g * g)
        m_hat = nm / (1.0 - ADAM_B1 ** ADAM_STEP)
        v_hat = nv / (1.0 - ADAM_B2 ** ADAM_STEP)
        g_ref[...] = g
        nm_ref[...] = nm
        nv_ref[...] = nv
        d_ref[...] = -ADAM_LR * (m_hat / (jnp.sqrt(v_hat) + ADAM_EPS) + ADAM_WD * w_ref[...])

    blk = pl.BlockSpec((tr, cols), lambda i: (i, 0))
    f = jax.ShapeDtypeStruct((rows, cols), F32)
    return pl.pallas_call(
        body, name=name, grid=(rows // tr,), in_specs=[blk] * 5, out_specs=[blk] * 4, out_shape=[f] * 4,
        compiler_params=_cp(1),
    )(w, ga, gb, m, v)


PACK_ALIGN = 1024
PACK_ROWS_ALIGN = 2048


def pack_small(arrs):
    parts = []
    for a in arrs:
        flat = a.reshape(-1)
        pad = (-flat.shape[0]) % PACK_ALIGN
        parts.append(jnp.pad(flat, (0, pad)) if pad else flat)
    rows = sum(p.shape[0] for p in parts) // 128
    parts.append(jnp.zeros(((-rows) % PACK_ROWS_ALIGN * 128,), arrs[0].dtype))
    return jnp.concatenate(parts).reshape(-1, 128)


def unpack_small(packed, shapes):
    out, row = [], 0
    for shape in shapes:
        size = int(np.prod(shape))
        rows = -(-size // PACK_ALIGN) * 8
        out.append(packed[row:row + rows].reshape(-1)[:size].reshape(shape))
        row += rows
    return out


def layer_fwd(x, mem, w_in, rest_of, P, biases):
    sv = {"x0": x}
    h1 = rms_fwd(x, P["g_mix_pre"], BF16, "rms_pre")
    proj = mm(h1, w_in, "nn", tm=1024, tn=768, tk=1024, out_dtypes=[F32], name="mm_w_in")
    a_out = pool_fwd(proj, P["pool_w"], P["pool_scale"])
    os_, lses = [], []
    for g, (win, dil) in enumerate(DIL_GROUPS):
        o, lse = att_fwd(proj, biases[g], g, dil)
        os_.append(o)
        lses.append(lse)
    b_out, w0, w1, w2 = att_combine(os_, lses)
    s5p = P["s5"]
    hr, hi, y = s5_fwd(proj, s5p[2], s5p[3], s5p[0], s5p[1], s5p[4], s5p[5], P["d_skip"])
    d_out = sgu_fwd(proj, P["sgu_ln_g"], P["sgu_ln_b"], P["w_s"], P["b_s_t"])
    W = dict(rest_of(d_out), w_in=w_in)
    c_out = glu_fwd(y, W["w_glu"], P["b_glu"])
    branches = (a_out, b_out, c_out, d_out)
    merged = merge_fwd(proj, W["gate_b"], branches, W["w_up"])
    t1 = mm(merged, W["w_out"], "nn", tm=1024, tn=1024, tk=1024, out_dtypes=[F32], name="mm_w_out")
    x1 = rms_fwd(t1, P["g_mix_post"], F32, "rms_post", res=x)
    sv.update(h1=h1, proj=proj, os=os_, lses=lses, wts=(w0, w1, w2), hr=hr, hi=hi, y=y, branches=branches,
              merged=merged, t1=t1, x1=x1)

    h2 = rms_fwd(x1, P["g_x_pre"], BF16, "rms_pre")
    mem_n = rms_fwd(mem, P["g_mem"], BF16, "rms_mem")
    q = mm(h2, W["w_cq"], "nn", tm=1024, tn=512, tk=1024, out_dtypes=[BF16], name="mm_w_cq")
    kv = mm(mem_n, W["w_ckv"], "nn", tm=256, tn=1024, tk=1024, out_dtypes=[BF16], name="mm_w_ckv")
    ox = xatt_fwd(q, kv)
    t2 = mm(ox, W["w_co"], "nn", tm=1024, tn=1024, tk=512, out_dtypes=[F32], name="mm_w_co")
    x2 = rms_fwd(t2, P["g_x_post"], F32, "rms_post", res=x1)
    sv.update(h2=h2, mem_n=mem_n, q=q, kv=kv, ox=ox, t2=t2, x2=x2)

    h3 = rms_fwd(x2, P["g_ff_pre"], BF16, "rms_pre")
    pre, act = mm(h3, W["w_ff1"], "nn", tm=1024, tn=1024, tk=1024, out_dtypes=[F32, BF16], name="mm_w_ff1",
                  epi=lambda acc: (acc, jnp.square(jnp.maximum(acc, 0.0))))
    ff = mm(act, W["w_ff2"], "nn", tm=1024, tn=1024, tk=1024, out_dtypes=[F32], name="mm_w_ff2")
    x3 = rms_fwd(ff, P["g_ff_post"], F32, "rms_post", res=x2)
    sv.update(h3=h3, pre=pre, act=act, ff=ff, W=W)
    return x3, sv


def layer_bwd(dx, mem, W, P, biases, sv, headsum, emit, after=()):
    G = {}
    dff, G["g_ff_post"] = rms_bwd(sv["ff"], P["g_ff_post"], dx, BF16, "rms_post_bwd", after=after)
    G["w_ff2"] = mm(sv["act"], dff, "tn", tm=1024, tn=1024, tk=1024, out_dtypes=[F32], name="mm_dw_ff2")
    dpre = mm(dff, W["w_ff2"], "nt", tm=1024, tn=1024, tk=1024, out_dtypes=[BF16], name="mm_dact", extras=(sv["pre"],),
              epi=lambda acc, pre: (acc * (2.0 * jnp.maximum(pre, 0.0)),))
    G["w_ff1"] = mm(sv["h3"], dpre, "tn", tm=1024, tn=1024, tk=1024, out_dtypes=[F32], name="mm_dw_ff1")
    sent = emit(("w_ff1", "w_ff2"), G)
    dh3 = mm(dpre, W["w_ff1"], "nt", tm=1024, tn=1024, tk=1024, out_dtypes=[F32], name="mm_dh3", after=sent)
    dx2, G["g_ff_pre"] = rms_bwd(sv["x2"], P["g_ff_pre"], dh3, F32, "rms_pre_bwd", add=dx)
    dt2, G["g_x_post"] = rms_bwd(sv["t2"], P["g_x_post"], dx2, BF16, "rms_post_bwd")
    G["w_co"] = mm(sv["ox"], dt2, "tn", tm=512, tn=1024, tk=1024, out_dtypes=[F32], name="mm_dw_co")
    dox = mm(dt2, W["w_co"], "nt", tm=1024, tn=512, tk=1024, out_dtypes=[BF16], name="mm_dox")
    dq, dkv = xatt_bwd(sv["q"], sv["kv"], dox)
    G["w_cq"] = mm(sv["h2"], dq, "tn", tm=1024, tn=512, tk=1024, out_dtypes=[F32], name="mm_dw_cq")
    dh2 = mm(dq, W["w_cq"], "nt", tm=1024, tn=1024, tk=512, out_dtypes=[F32], name="mm_dh2")
    G["w_ckv"] = mm(sv["mem_n"], dkv, "tn", tm=1024, tn=1024, tk=256, out_dtypes=[F32], name="mm_dw_ckv")
    dmem_n = mm(dkv, W["w_ckv"], "nt", tm=256, tn=1024, tk=1024, out_dtypes=[F32], name="mm_dmem")
    _, G["g_mem"] = rms_bwd(mem, P["g_mem"], dmem_n, BF16, "rms_mem_bwd")
    dx1, G["g_x_pre"] = rms_bwd(sv["x1"], P["g_x_pre"], dh2, F32, "rms_pre_bwd", add=dx2)
    proj = sv["proj"]
    dt1, G["g_mix_post"] = rms_bwd(sv["t1"], P["g_mix_post"], dx1, BF16, "rms_post_bwd")
    G["w_out"] = mm(sv["merged"], dt1, "tn", tm=1024, tn=1024, tk=1024, out_dtypes=[F32], name="mm_dw_out")
    dmerged = mm(dt1, W["w_out"], "nt", tm=1024, tn=1024, tk=1024, out_dtypes=[F32], name="mm_dmerged")
    dgates, dups, G["gate_b"] = merge_bwd(proj, W["gate_b"], sv["branches"], W["w_up"], dmerged)
    dbr, dwup = [], []
    for i in range(4):
        dbr.append(mm(dups[i], W["w_up"][i], "nt", tm=1024, tn=512, tk=1024, out_dtypes=[F32], name="mm_dbranch"))
        dwup.append(mm(sv["branches"][i], dups[i], "tn", tm=512, tn=1024, tk=1024, out_dtypes=[F32], name="mm_dw_up"))
    G["w_up"] = jnp.concatenate(dwup, axis=0)
    d_pool, G["pool_w"], G["pool_scale"] = pool_bwd(proj, P["pool_w"], P["pool_scale"], dbr[0])
    cbar = att_combine_bwd(dbr[1], sv["os"], sv["wts"], headsum)
    dqs, dks, dvs, dbias = [], [], [], []
    for g, (win, dil) in enumerate(DIL_GROUPS):
        dq_g, db_g = att_bwd_q(proj, biases[g], sv["lses"][g], sv["wts"][g], dbr[1], cbar, g, dil)
        dk_g, dv_g = att_bwd_kv(proj, biases[g], sv["lses"][g], sv["wts"][g], dbr[1], cbar, g, dil)
        dqs.append(dq_g)
        dks.append(dk_g)
        dvs.append(dv_g)
        dbias.append(db_g)
    G["att_bias"] = dbias
    s5p = P["s5"]
    dy, G["w_glu"], G["b_glu"] = glu_bwd(sv["y"], W["w_glu"], P["b_glu"], dbr[2])
    d_ssm, dbre, dbim, dar, dai, dcre, dcim, G["d_skip"] = s5_bwd(
        proj, sv["hr"], sv["hi"], dy, s5p[2], s5p[3], s5p[0], s5p[1], s5p[4], s5p[5], P["d_skip"])
    G["s5"] = (dar, dai, dbre, dbim, dcre, dcim)
    dzu, dzv, G["sgu_ln_g"], G["sgu_ln_b"], G["w_s"], G["b_s_t"] = sgu_bwd(
        proj, P["sgu_ln_g"], P["sgu_ln_b"], P["w_s"], P["b_s_t"], dbr[3])
    d_qkv = [d.astype(BF16) for d in dqs + dks + dvs]
    dproj = jnp.concatenate([d_pool] + d_qkv + [d_ssm, dzu, dzv] + list(dgates), axis=1)
    sent = emit(("gate_b", "w_glu", "w_up", "w_out", "w_cq", "w_ckv", "w_co"), G)
    G["w_in"] = mm(sv["h1"], dproj, "tn", tm=1024, tn=1536, tk=1024, out_dtypes=[F32], name="mm_dw_in", after=sent)
    dh1 = mm(dproj, W["w_in"], "nt", tm=1024, tn=1024, tk=1536, out_dtypes=[F32], name="mm_dh1")
    dx0, G["g_mix_pre"] = rms_bwd(sv["x0"], P["g_mix_pre"], dh1, F32, "rms_pre_bwd", add=dx1)
    return dx0, G


def _as3d(name, a):
    shape2d, axis = SHARDED[name]
    rows, cols = shape2d
    if axis == 0:
        rows //= 4
    else:
        cols //= 4
    return a.reshape(DEPTH, rows, cols)


def kernel(x, mem, rel_bias, g_mix_pre, g_mix_post, w_in, gate_b, pool_w, pool_scale, a_re, a_im, log_dt, b_re, b_im, c_re, c_im, d_skip, w_glu, b_glu, sgu_ln_g, sgu_ln_b, w_s, b_s, w_up, w_out, g_x_pre, g_x_post, g_mem, w_cq, w_ckv, w_co, g_ff_pre, g_ff_post, w_ff1, w_ff2, loss_target, m_rel_bias, m_g_mix_pre, m_g_mix_post, m_w_in, m_gate_b, m_pool_w, m_pool_scale, m_a_re, m_a_im, m_log_dt, m_b_re, m_b_im, m_c_re, m_c_im, m_d_skip, m_w_glu, m_b_glu, m_sgu_ln_g, m_sgu_ln_b, m_w_s, m_b_s, m_w_up, m_w_out, m_g_x_pre, m_g_x_post, m_g_mem, m_w_cq, m_w_ckv, m_w_co, m_g_ff_pre, m_g_ff_post, m_w_ff1, m_w_ff2, v_rel_bias, v_g_mix_pre, v_g_mix_post, v_w_in, v_gate_b, v_pool_w, v_pool_scale, v_a_re, v_a_im, v_log_dt, v_b_re, v_b_im, v_c_re, v_c_im, v_d_skip, v_w_glu, v_b_glu, v_sgu_ln_g, v_sgu_ln_b, v_w_s, v_b_s, v_w_up, v_w_out, v_g_x_pre, v_g_x_post, v_g_mem, v_w_cq, v_w_ckv, v_w_co, v_g_ff_pre, v_g_ff_post, v_w_ff1, v_w_ff2):
    env = dict(locals())
    weights = {n: env[n] for n in WEIGHT_NAMES}
    mom_m = {n: env["m_" + n] for n in WEIGHT_NAMES}
    mom_v = {n: env["v_" + n] for n in WEIGHT_NAMES}
    x2d = x.reshape(x.shape[1], D_MODEL)
    mem2d = mem.reshape(N_MEM, D_MODEL)
    target = loss_target.reshape(x2d.shape)

    axis_of = {n: SHARDED[n][1] for n in SHARDED_NAMES}
    chip = _chip_index().astype(jnp.int32).reshape(1)
    rest_names = [n for n in SHARDED_NAMES if n != "w_in"]

    def gather_start(l, names, tag, after=()):
        shards = [_as3d(n, weights[n])[l].astype(F32 if n == "gate_b" else MXU_DTYPE) for n in names]
        ax = [axis_of[n] for n in names]
        lands = [jnp.concatenate([s] * 4, axis=a) for s, a in zip(shards, ax)]
        return (names, ax, tag) + chips_start(shards, lands, ax, "gather", f"gather_start_{tag}", after=after)

    def gather_wait(started, after):
        names, ax, tag, send, recv, shards, lands, _ = started
        _, lands = chips_wait(send, recv, shards, lands, ax, "gather", f"gather_wait_{tag}", after=after)
        W = dict(zip(names, lands))
        if "w_up" in W:
            W["w_up"] = W["w_up"].reshape(4, W_BRANCH, D_MODEL)
        return W

    biases = [att_bias(rel_bias, g, dil) for g, (_, dil) in enumerate(DIL_GROUPS)]
    lanes = np.arange(W_BRANCH) // ATT_HEAD_DIM
    headsum = jnp.asarray(lanes[:, None] == lanes[None, :], dtype=BF16)

    def small_params(l, s5_prepared):
        vec = lambda a: a[l].reshape(1, -1)
        return {
            "g_mix_pre": vec(g_mix_pre), "g_mix_post": vec(g_mix_post), "g_x_pre": vec(g_x_pre), "g_x_post": vec(g_x_post),
            "g_mem": vec(g_mem), "g_ff_pre": vec(g_ff_pre), "g_ff_post": vec(g_ff_post), "pool_w": pool_w[l],
            "pool_scale": vec(pool_scale), "d_skip": vec(d_skip), "b_glu": vec(b_glu), "sgu_ln_g": vec(sgu_ln_g),
            "sgu_ln_b": vec(sgu_ln_b), "w_s": w_s[l], "b_s_t": b_s[l].T, "s5": s5_prepared,
        }

    Ws, Ps, saved, s5_vjps = [], [], [], []
    xl = x2d
    flying = {"next": gather_start(0, ["w_in"], "0_w_in")}
    for l in range(DEPTH):
        s5_prepared, s5_vjp = jax.vjp(s5_params, a_re[l], a_im[l], log_dt[l], b_re[l], b_im[l], c_re[l], c_im[l])
        if l == 0:
            w_in_l = gather_wait(flying["next"], ())["w_in"]
            rest = gather_start(0, rest_names, "0_rest", after=[w_in_l])

            def rest_of(after_value):
                W = gather_wait(rest, [after_value])
                flying["next"] = gather_start(1, SHARDED_NAMES, "1", after=[W["w_out"]])
                return W
        else:
            W_l = gather_wait(flying["next"], [xl])
            w_in_l = W_l["w_in"]
            if l + 1 < DEPTH:
                flying["next"] = gather_start(l + 1, SHARDED_NAMES, str(l + 1), after=[w_in_l])
            rest_of = lambda after_value, W_l=W_l: W_l
        P = small_params(l, s5_prepared)
        xl, sv = layer_fwd(xl, mem2d, w_in_l, rest_of, P, biases)
        Ws.append(sv["W"])
        Ps.append(P)
        saved.append(sv)
        s5_vjps.append(s5_vjp)
    loss_local, dx = loss_and_grad(xl, target)
    loss = lax.psum(loss_local, ("x", "y", "c"))

    scattered = []

    def scatter_start(l, names, srcs):
        ax = [axis_of.get(n) for n in names]
        lands = []
        for s, a in zip(srcs, ax):
            r, c = s.shape
            lands.append(lax.empty((4, r // 4 if a == 0 else r, c // 4 if a == 1 else c), F32))
        tag = f"{l}_{names[0]}"
        send, recv, srcs, lands, token = chips_start(srcs, lands, ax, "scatter", f"grads_start_{tag}")
        scattered.append((l, names, ax, tag, send, recv, srcs, lands))
        return (token,)

    grads, after = [None] * DEPTH, ()
    for l in reversed(range(DEPTH)):
        emit = lambda names, G, l=l: scatter_start(l, list(names), [G[n] for n in names])
        dx, grads[l] = layer_bwd(dx, mem2d, Ws[l], Ps[l], biases, saved[l], headsum, emit, after=after)
        if l > 0:
            after = scatter_start(l, ["w_in"], [grads[l]["w_in"]])
    grad_x = dx.reshape(x.shape)

    rep = {}
    stack = lambda key, shape: jnp.stack([grads[l][key] for l in range(DEPTH)]).reshape(shape)
    for n in ("g_mix_pre", "g_mix_post", "g_x_pre", "g_x_post", "g_mem", "g_ff_pre", "g_ff_post"):
        rep[n] = stack(n, (DEPTH, D_MODEL))
    for n in ("pool_scale", "d_skip", "b_glu", "sgu_ln_g", "sgu_ln_b"):
        rep[n] = stack(n, (DEPTH, W_BRANCH))
    rep["pool_w"] = stack("pool_w", pool_w.shape)
    rep["w_s"] = stack("w_s", w_s.shape)
    rep["b_s"] = jnp.stack([grads[l]["b_s_t"][:, :4].T for l in range(DEPTH)])
    s5_grads = [s5_vjps[l](tuple(grads[l]["s5"])) for l in range(DEPTH)]
    for i, n in enumerate(("a_re", "a_im", "log_dt", "b_re", "b_im", "c_re", "c_im")):
        rep[n] = jnp.stack([s5_grads[l][i] for l in range(DEPTH)])
    dbias = [sum(grads[l]["att_bias"][g] for l in range(DEPTH)) for g in range(len(DIL_GROUPS))]
    rep["rel_bias"] = jnp.concatenate([att_bias_grad(dbias[g], dil) for g, (_, dil) in enumerate(DIL_GROUPS)], axis=1)
    rep_shapes = [weights[n].shape for n in REPLICATED_NAMES]
    packed_g = pack_small([rep[n] for n in REPLICATED_NAMES])

    scatter_start(0, ["w_in", "small"], [grads[0]["w_in"], packed_g])
    stacked = {}
    for l, names, ax, tag, send, recv, srcs, lands in scattered:
        srcs, lands = chips_wait(send, recv, srcs, lands, ax, "scatter", f"grads_wait_{tag}", after=[dx])
        for n, own, arrived, a in zip(names, srcs, lands, ax):
            if n not in stacked:
                stacked[n] = lax.empty((1 if n == "small" else DEPTH,) + arrived.shape[1:], F32)
            stacked[n] = sum_chips(arrived, own, a, chip, stacked[n], 0 if n == "small" else l, "sum_chips")
    partial = [stacked[n].reshape(-1, stacked[n].shape[-1]) for n in SHARDED_NAMES + ["small"]]
    other = swap_cores(partial)

    out_g, out_d, out_m, out_v = {}, {}, {}, {}
    for t, n in enumerate(SHARDED_NAMES):
        flat = lambda a: a.reshape(partial[t].shape)
        res = adamw(flat(weights[n]), partial[t], other[t], flat(mom_m[n]), flat(mom_v[n]), "adamw")
        out_g[n], out_d[n], out_m[n], out_v[n] = [r.reshape(weights[n].shape) for r in res]
    small = [pack_small([d[n] for n in REPLICATED_NAMES]) for d in (weights, mom_m, mom_v)]
    res = adamw(small[0], partial[-1], other[-1], small[1], small[2], "adamw")
    for d, r in zip((out_g, out_d, out_m, out_v), res):
        d.update(zip(REPLICATED_NAMES, unpack_small(r, rep_shapes)))

    return (loss, grad_x, *[out_g[n] for n in WEIGHT_NAMES], *[out_d[n] for n in WEIGHT_NAMES],
            *[out_m[n] for n in WEIGHT_NAMES], *[out_v[n] for n in WEIGHT_NAMES])
```

```python
import functools
import math

import numpy as np
import jax
import jax.numpy as jnp
from jax import lax
from jax.experimental import pallas as pl
from jax.experimental.pallas import tpu as pltpu

F32 = jnp.float32
BF16 = jnp.bfloat16
MXU_DTYPE = jnp.bfloat16
MESH_ID = pl.DeviceIdType.MESH
VMEM_LIMIT_BYTES = 56 * 1024 * 1024

D_MODEL = 1024
DEPTH = 4
N_MEM = 256
W_BRANCH = 512
POOL_WINDOWS = (2, 4, 8, 16)
POOL_HALO = 16
DIL_GROUPS = ((128, 1), (512, 4), (2048, 16))
BAND = 128
ATT_HEADS = 8
ATT_HEAD_DIM = 64
SSM_GROUP = 16
SSM_GROUPS = 32
SSM_STATE = 64
SSM_COLS = SSM_GROUPS * SSM_STATE
SSM_T = 512
SGU_CHUNK = 128
X_HEADS = 4
X_HEAD_DIM = 128
D_FF = 4096
REL_BUCKETS = 32
REL_MAX_DIST = 2048
EPS = 1e-6
NEG_INF = -1e30
OFF_POOL = 0
OFF_ATT = 512
OFF_SSM = OFF_ATT + 9 * W_BRANCH
OFF_SGU = OFF_SSM + W_BRANCH
OFF_GATE = OFF_SGU + 2 * W_BRANCH
IN_WIDTH = OFF_GATE + 4 * D_MODEL

ADAM_LR = 0.001
ADAM_B1 = 0.9
ADAM_B2 = 0.999
ADAM_EPS = 1e-08
ADAM_WD = 0.01
ADAM_STEP = 10

GELU_C = math.sqrt(2.0 / math.pi)

WEIGHT_NAMES = ['rel_bias', 'g_mix_pre', 'g_mix_post', 'w_in', 'gate_b', 'pool_w', 'pool_scale', 'a_re', 'a_im',
                'log_dt', 'b_re', 'b_im', 'c_re', 'c_im', 'd_skip', 'w_glu', 'b_glu', 'sgu_ln_g', 'sgu_ln_b',
                'w_s', 'b_s', 'w_up', 'w_out', 'g_x_pre', 'g_x_post', 'g_mem', 'w_cq', 'w_ckv', 'w_co',
                'g_ff_pre', 'g_ff_post', 'w_ff1', 'w_ff2']
SHARDED = {
    'w_in': ((D_MODEL, IN_WIDTH), 1),
    'gate_b': ((4, D_MODEL), 1),
    'w_glu': ((W_BRANCH, W_BRANCH), 0),
    'w_up': ((4 * W_BRANCH, D_MODEL), 1),
    'w_out': ((D_MODEL, D_MODEL), 0),
    'w_cq': ((D_MODEL, W_BRANCH), 0),
    'w_ckv': ((D_MODEL, D_MODEL), 0),
    'w_co': ((W_BRANCH, D_MODEL), 1),
    'w_ff1': ((D_MODEL, D_FF), 1),
    'w_ff2': ((D_FF, D_MODEL), 0),
}
SHARDED_NAMES = list(SHARDED)
REPLICATED_NAMES = [n for n in WEIGHT_NAMES if n not in SHARDED]


def _cp(n_axes):
    return pltpu.CompilerParams(dimension_semantics=("arbitrary",) * n_axes, vmem_limit_bytes=VMEM_LIMIT_BYTES)


def _dot(a, b, dims="nn"):
    cd = {"nn": ((1,), (0,)), "nt": ((1,), (1,)), "tn": ((0,), (0,))}[dims]
    return lax.dot_general(a.astype(MXU_DTYPE), b.astype(MXU_DTYPE), (cd, ((), ())), preferred_element_type=F32)


def _gelu(x):
    return 0.5 * x * (1.0 + jnp.tanh(GELU_C * (x + 0.044715 * (x * x * x))))


def _gelu_grad(x):
    t = jnp.tanh(GELU_C * (x + 0.044715 * (x * x * x)))
    return 0.5 * (1.0 + t) + 0.5 * x * (1.0 - t * t) * (GELU_C * (1.0 + 3.0 * 0.044715 * (x * x)))


def _sigmoid(x):
    return 1.0 / (1.0 + jnp.exp(-x))


def mm(a, b, dims, *, tm, tn, tk, out_dtypes, name, extras=(), epi=None, after=()):
    if dims == "tn":
        K, M = a.shape
        N = b.shape[1]
    else:
        M, K = a.shape
        N = b.shape[1] if dims == "nn" else b.shape[0]
    tm, tn, tk = min(tm, M), min(tn, N), min(tk, K)
    assert M % tm == 0 and N % tn == 0 and K % tk == 0, (name, M, N, K, tm, tn, tk)
    nk = K // tk
    ne, no = len(extras), len(out_dtypes)
    if epi is None:
        epi = lambda acc: (acc,)
    a_spec = (pl.BlockSpec((tk, tm), lambda i, j, k: (k, i)) if dims == "tn"
              else pl.BlockSpec((tm, tk), lambda i, j, k: (i, k)))
    b_spec = (pl.BlockSpec((tn, tk), lambda i, j, k: (j, k)) if dims == "nt"
              else pl.BlockSpec((tk, tn), lambda i, j, k: (k, j)))
    mn_spec = pl.BlockSpec((tm, tn), lambda i, j, k: (i, j))

    def body(a_ref, b_ref, *rest):
        extra_refs, out_refs = rest[:ne], rest[ne + len(after):ne + len(after) + no]
        part = _dot(a_ref[...], b_ref[...], dims)

        def finish(acc):
            for o_ref, r in zip(out_refs, epi(acc, *[e[...] for e in extra_refs])):
                o_ref[...] = r.astype(o_ref.dtype)

        if nk == 1:
            finish(part)
        else:
            acc_ref = rest[-1]
            k = pl.program_id(2)

            @pl.when(k == 0)
            def _():
                acc_ref[...] = part

            @pl.when(k > 0)
            def _():
                acc_ref[...] += part

            @pl.when(k == nk - 1)
            def _():
                finish(acc_ref[...])

    outs = pl.pallas_call(
        body, name=name, grid=(M // tm, N // tn, nk),
        in_specs=[a_spec, b_spec] + [mn_spec] * ne + [ANY] * len(after),
        out_specs=[mn_spec] * no,
        out_shape=[jax.ShapeDtypeStruct((M, N), dt) for dt in out_dtypes],
        scratch_shapes=[pltpu.VMEM((tm, tn), F32)] if nk > 1 else [],
        compiler_params=_cp(3),
    )(a, b, *extras, *after)
    return outs[0] if no == 1 else outs


ROW_TILE = 512


def rms_fwd(x, g, out_dtype, name, res=None):
    M, D = x.shape
    tm = min(ROW_TILE, M)

    def body(x_ref, g_ref, *rest):
        o_ref = rest[-1]
        xf = x_ref[...]
        y = xf * lax.rsqrt(jnp.mean(xf * xf, axis=-1, keepdims=True) + EPS) * g_ref[...]
        if res is not None:
            y = y + rest[0][...]
        o_ref[...] = y.astype(o_ref.dtype)

    row = pl.BlockSpec((tm, D), lambda i: (i, 0))
    return pl.pallas_call(
        body, name=name, grid=(M // tm,),
        in_specs=[row, pl.BlockSpec((1, D), lambda i: (0, 0))] + ([row] if res is not None else []),
        out_specs=row, out_shape=jax.ShapeDtypeStruct((M, D), out_dtype), compiler_params=_cp(1),
    )(x, g, *([res] if res is not None else []))


def rms_bwd(x, g, dy, dx_dtype, name, add=None, after=()):
    M, D = x.shape
    tm = min(ROW_TILE, M)

    def body(x_ref, g_ref, dy_ref, *rest):
        dx_ref, dg_ref = rest[-2], rest[-1]
        xf = x_ref[...]
        dyf = dy_ref[...].astype(F32)
        r = lax.rsqrt(jnp.mean(xf * xf, axis=-1, keepdims=True) + EPS)
        xn = xf * r
        dxn = dyf * g_ref[...]
        dx = r * (dxn - xn * jnp.mean(dxn * xn, axis=-1, keepdims=True))
        if add is not None:
            dx = dx + rest[0][...]
        dx_ref[...] = dx.astype(dx_ref.dtype)
        dg = jnp.sum(dyf * xn, axis=0, keepdims=True)

        @pl.when(pl.program_id(0) == 0)
        def _():
            dg_ref[...] = dg

        @pl.when(pl.program_id(0) > 0)
        def _():
            dg_ref[...] += dg

    row = pl.BlockSpec((tm, D), lambda i: (i, 0))
    vec = pl.BlockSpec((1, D), lambda i: (0, 0))
    return pl.pallas_call(
        body, name=name, grid=(M // tm,),
        in_specs=[row, vec, row] + ([row] if add is not None else []) + [ANY] * len(after),
        out_specs=[row, vec],
        out_shape=[jax.ShapeDtypeStruct((M, D), dx_dtype), jax.ShapeDtypeStruct((1, D), F32)],
        compiler_params=_cp(1),
    )(x, g, dy, *([add] if add is not None else []), *after)


def loss_and_grad(y, target):
    M, D = y.shape
    tm = ROW_TILE

    def body(y_ref, t_ref, part_ref, dy_ref):
        e = y_ref[...] - t_ref[...]
        dy_ref[...] = e / D
        part_ref[...] = jnp.broadcast_to(0.5 * jnp.sum(jnp.mean(e * e, axis=-1, keepdims=True), axis=0, keepdims=True),
                                         (8, 128))

    row = pl.BlockSpec((tm, D), lambda i: (i, 0))
    part, dy = pl.pallas_call(
        body, name="loss", grid=(M // tm,), in_specs=[row, row],
        out_specs=[pl.BlockSpec((8, 128), lambda i: (i, 0)), row],
        out_shape=[jax.ShapeDtypeStruct((8 * (M // tm), 128), F32), jax.ShapeDtypeStruct((M, D), F32)],
        compiler_params=_cp(1),
    )(y, target)
    return jnp.sum(part[::8, 0]), dy


POOL_ROWS = 512


def _pool_window_sum(xw, gi, roll_of):
    s1 = xw + pltpu.roll(xw, roll_of(1), 0)
    s2 = s1 + pltpu.roll(s1, roll_of(2), 0)
    s3 = s2 + pltpu.roll(s2, roll_of(4), 0)
    s4 = s3 + pltpu.roll(s3, roll_of(8), 0)
    return jnp.where(gi == 0, s1, jnp.where(gi == 1, s2, jnp.where(gi == 2, s3, s4)))


def _pool_cnt(i, gi):
    rows = lax.broadcasted_iota(jnp.int32, (POOL_ROWS, 128), 0) + i * POOL_ROWS
    w = jnp.where(gi == 0, 2, jnp.where(gi == 1, 4, jnp.where(gi == 2, 8, 16)))
    return jnp.minimum(rows + 1, w).astype(F32)


def pool_fwd(proj, pool_w, scale):
    S = proj.shape[0]
    nchunk = S // POOL_ROWS
    slab = POOL_ROWS + POOL_HALO

    def body(x_ref, w_ref, sc_ref, o_ref, pad_ref):
        gi = pl.program_id(0)
        pad_ref[0:POOL_HALO, :] = jnp.zeros((POOL_HALO, 128), F32)
        pad_ref[POOL_HALO:, :] = x_ref[...]
        for i in range(nchunk):
            xw = pad_ref[i * POOL_ROWS:i * POOL_ROWS + slab, :]
            ssum = _pool_window_sum(xw, gi, lambda d: d)[POOL_HALO:, :]
            p = ssum / _pool_cnt(i, gi) - xw[POOL_HALO:, :]
            o_ref[i * POOL_ROWS:(i + 1) * POOL_ROWS, :] = (_dot(p, w_ref[...]) * sc_ref[...]).astype(o_ref.dtype)

    return pl.pallas_call(
        body, name="pool_fwd", grid=(4,),
        in_specs=[pl.BlockSpec((S, 128), lambda g: (0, OFF_POOL // 128 + g)),
                  pl.BlockSpec((None, 128, 128), lambda g: (g, 0, 0)),
                  pl.BlockSpec((1, 128), lambda g: (0, g))],
        out_specs=pl.BlockSpec((S, 128), lambda g: (0, g)),
        out_shape=jax.ShapeDtypeStruct((S, W_BRANCH), BF16),
        scratch_shapes=[pltpu.VMEM((S + POOL_HALO, 128), F32)],
        compiler_params=_cp(1),
    )(proj, pool_w, scale)


def pool_bwd(proj, pool_w, scale, dy):
    S = proj.shape[0]
    nchunk = S // POOL_ROWS
    slab = POOL_ROWS + POOL_HALO

    def body(x_ref, w_ref, sc_ref, dy_ref, dx_ref, dw_ref, dsc_ref, pad_ref, pad2_ref, dp_ref):
        gi = pl.program_id(0)
        pad_ref[0:POOL_HALO, :] = jnp.zeros((POOL_HALO, 128), F32)
        pad_ref[POOL_HALO:, :] = x_ref[...]
        pad2_ref[S:, :] = jnp.zeros((POOL_HALO, 128), F32)
        dw = jnp.zeros((128, 128), F32)
        dsc = jnp.zeros((1, 128), F32)
        for i in range(nchunk):
            xw = pad_ref[i * POOL_ROWS:i * POOL_ROWS + slab, :]
            cnt = _pool_cnt(i, gi)
            p = _pool_window_sum(xw, gi, lambda d: d)[POOL_HALO:, :] / cnt - xw[POOL_HALO:, :]
            dyc = dy_ref[i * POOL_ROWS:(i + 1) * POOL_ROWS, :]
            dsc = dsc + jnp.sum(dyc * _dot(p, w_ref[...]), axis=0, keepdims=True)
            dys = dyc * sc_ref[...]
            dw = dw + _dot(p, dys, "tn")
            dp = _dot(dys, w_ref[...], "nt")
            dp_ref[i * POOL_ROWS:(i + 1) * POOL_ROWS, :] = dp
            pad2_ref[i * POOL_ROWS:(i + 1) * POOL_ROWS, :] = dp / cnt
        dw_ref[...] = dw
        dsc_ref[...] = dsc
        for i in range(nchunk):
            xw = pad2_ref[i * POOL_ROWS:i * POOL_ROWS + slab, :]
            fsum = _pool_window_sum(xw, gi, lambda d: slab - d)[:POOL_ROWS, :]
            rows = slice(i * POOL_ROWS, (i + 1) * POOL_ROWS)
            dx_ref[rows, :] = (fsum - dp_ref[rows, :]).astype(dx_ref.dtype)

    return pl.pallas_call(
        body, name="pool_bwd", grid=(4,),
        in_specs=[pl.BlockSpec((S, 128), lambda g: (0, OFF_POOL // 128 + g)),
                  pl.BlockSpec((None, 128, 128), lambda g: (g, 0, 0)),
                  pl.BlockSpec((1, 128), lambda g: (0, g)),
                  pl.BlockSpec((S, 128), lambda g: (0, g))],
        out_specs=[pl.BlockSpec((S, 128), lambda g: (0, g)),
                   pl.BlockSpec((None, 128, 128), lambda g: (g, 0, 0)),
                   pl.BlockSpec((1, 128), lambda g: (0, g))],
        out_shape=[jax.ShapeDtypeStruct((S, W_BRANCH), BF16), jax.ShapeDtypeStruct((4, 128, 128), F32),
                   jax.ShapeDtypeStruct((1, W_BRANCH), F32)],
        scratch_shapes=[pltpu.VMEM((S + POOL_HALO, 128), F32), pltpu.VMEM((S + POOL_HALO, 128), F32),
                        pltpu.VMEM((S, 128), F32)],
        compiler_params=_cp(1),
    )(proj, pool_w, scale, dy)


def _t5_bucket(n):
    exact = REL_BUCKETS // 2
    nf = np.maximum(n, 1).astype(np.float32)
    large = exact + (np.log(nf / exact) / np.log(REL_MAX_DIST / exact) * (REL_BUCKETS - exact)).astype(np.int32)
    large = np.minimum(large, REL_BUCKETS - 1)
    return np.where(n < exact, n, large).astype(np.int32)


def _band_onehot(dil):
    i = np.arange(BAND)[:, None]
    kk = np.arange(2 * BAND)[None, :]
    dist = BAND + i - kk
    local = (dist >= 0) & (dist <= BAND)
    bucket = _t5_bucket(np.clip(dist, 0, BAND) * dil)
    onehot = (bucket.reshape(-1, 1) == np.arange(REL_BUCKETS)[None, :]).astype(np.float32)
    return onehot, local


def att_bias(rel_bias, g, dil):
    onehot, local = _band_onehot(dil)
    tab = jnp.dot(jnp.asarray(onehot), rel_bias[:, g * ATT_HEADS:(g + 1) * ATT_HEADS], precision=lax.Precision.HIGHEST)
    bias = tab.reshape(BAND, 2 * BAND, ATT_HEADS).transpose(2, 0, 1)
    return jnp.where(jnp.asarray(local)[None], bias, NEG_INF)


def att_bias_grad(dbias, dil):
    onehot, _ = _band_onehot(dil)
    flat = dbias.transpose(1, 2, 0).reshape(BAND * 2 * BAND, ATT_HEADS)
    return jnp.dot(jnp.asarray(onehot).T, flat, precision=lax.Precision.HIGHEST)


def _rows(r, d):
    return pl.ds(r, BAND, stride=d) if d > 1 else pl.ds(0, BAND)


def _head_lanes():
    return lax.broadcasted_iota(jnp.int32, (BAND, 128), 1) < ATT_HEAD_DIM


def _att_cols(part, g, hp):
    return (OFF_ATT + part * 3 * W_BRANCH + g * W_BRANCH) // 128 + hp


def att_fwd(proj, bias, g, d):
    S = proj.shape[0]
    ch = BAND * d
    nb = S // ch

    def body(q_ref, kc_ref, kp_ref, vc_ref, vp_ref, b_ref, o_ref, l_ref):
        n = pl.program_id(1)
        head0 = _head_lanes()
        first = jnp.logical_and(lax.broadcasted_iota(jnp.int32, (BAND, 2 * BAND), 1) < BAND, n == 0)
        for r in range(d):
            rows = _rows(r, d)
            q = q_ref[rows, :]
            k = jnp.concatenate([kp_ref[rows, :], kc_ref[rows, :]], axis=0).astype(MXU_DTYPE)
            v = jnp.concatenate([vp_ref[rows, :], vc_ref[rows, :]], axis=0).astype(MXU_DTYPE)
            o_h, l_h = [], []
            for hh in range(2):
                qm = jnp.where(head0 if hh == 0 else jnp.logical_not(head0), q, 0.0)
                s = _dot(qm, k, "nt") * (ATT_HEAD_DIM ** -0.5) + b_ref[hh]
                s = jnp.where(first, NEG_INF, s)
                m = jnp.max(s, axis=-1, keepdims=True)
                p = jnp.exp(s - m)
                l = jnp.sum(p, axis=-1, keepdims=True)
                o_h.append(_dot(p / l, v))
                l_h.append(jnp.broadcast_to(m + jnp.log(l), (BAND, 128)))
            o_ref[rows, :] = jnp.where(head0, o_h[0], o_h[1])
            l_ref[rows, :] = jnp.where(head0, l_h[0], l_h[1])

    def col(part):
        return lambda hp, n: (n, _att_cols(part, g, hp))

    def col_prev(part):
        return lambda hp, n: (jnp.maximum(n - 1, 0), _att_cols(part, g, hp))

    blk = (ch, 128)
    out = pl.BlockSpec(blk, lambda hp, n: (n, hp))
    return pl.pallas_call(
        body, name=f"att_fwd_d{d}", grid=(4, nb),
        in_specs=[pl.BlockSpec(blk, col(0)), pl.BlockSpec(blk, col(1)), pl.BlockSpec(blk, col_prev(1)),
                  pl.BlockSpec(blk, col(2)), pl.BlockSpec(blk, col_prev(2)),
                  pl.BlockSpec((2, BAND, 2 * BAND), lambda hp, n: (hp, 0, 0))],
        out_specs=[out, out],
        out_shape=[jax.ShapeDtypeStruct((S, W_BRANCH), F32), jax.ShapeDtypeStruct((S, W_BRANCH), F32)],
        compiler_params=_cp(2),
    )(proj, proj, proj, proj, proj, bias)


def _att_pair(q, k, v, bias, lse_b, do, delta_b, hh, head0, mask=None):
    sel = head0 if hh == 0 else jnp.logical_not(head0)
    s = _dot(jnp.where(sel, q, 0.0), k, "nt") * (ATT_HEAD_DIM ** -0.5) + bias
    if mask is not None:
        s = jnp.where(mask, NEG_INF, s)
    c = hh * ATT_HEAD_DIM
    p = jnp.exp(s - lse_b[:, c:c + 1])
    dp = _dot(jnp.where(sel, do, 0.0), v, "nt")
    return p, p * (dp - delta_b[:, c:c + 1])


def att_bwd_q(proj, bias, lse, wts, dout, cbar, g, d):
    S = proj.shape[0]
    ch = BAND * d
    nb = S // ch

    def body(q_ref, kc_ref, kp_ref, vc_ref, vp_ref, b_ref, l_ref, w_ref, do_ref, cb_ref, dq_ref, db_ref):
        n = pl.program_id(1)
        head0 = _head_lanes()
        first = jnp.logical_and(lax.broadcasted_iota(jnp.int32, (BAND, 2 * BAND), 1) < BAND, n == 0)

        @pl.when(n == 0)
        def _():
            db_ref[...] = jnp.zeros(db_ref.shape, F32)

        for r in range(d):
            rows = _rows(r, d)
            q = q_ref[rows, :]
            k = jnp.concatenate([kp_ref[rows, :], kc_ref[rows, :]], axis=0).astype(MXU_DTYPE)
            v = jnp.concatenate([vp_ref[rows, :], vc_ref[rows, :]], axis=0).astype(MXU_DTYPE)
            w = w_ref[rows, :]
            do = w * do_ref[rows, :]
            delta = w * cb_ref[rows, :]
            lse_b = l_ref[rows, :]
            dq_h = []
            for hh in range(2):
                _, ds = _att_pair(q, k, v, b_ref[hh], lse_b, do, delta, hh, head0, mask=first)
                db_ref[hh] += ds
                dq_h.append(_dot(ds * (ATT_HEAD_DIM ** -0.5), k))
            dq_ref[rows, :] = jnp.where(head0, dq_h[0], dq_h[1]).astype(dq_ref.dtype)

    def col(part):
        return lambda hp, n: (n, _att_cols(part, g, hp))

    def col_prev(part):
        return lambda hp, n: (jnp.maximum(n - 1, 0), _att_cols(part, g, hp))

    blk = (ch, 128)
    cur = pl.BlockSpec(blk, lambda hp, n: (n, hp))
    bias_spec = pl.BlockSpec((2, BAND, 2 * BAND), lambda hp, n: (hp, 0, 0))
    return pl.pallas_call(
        body, name=f"att_bwd_q_d{d}", grid=(4, nb),
        in_specs=[pl.BlockSpec(blk, col(0)), pl.BlockSpec(blk, col(1)), pl.BlockSpec(blk, col_prev(1)),
                  pl.BlockSpec(blk, col(2)), pl.BlockSpec(blk, col_prev(2)), bias_spec, cur, cur, cur, cur],
        out_specs=[cur, bias_spec],
        out_shape=[jax.ShapeDtypeStruct((S, W_BRANCH), F32), jax.ShapeDtypeStruct((ATT_HEADS, BAND, 2 * BAND), F32)],
        compiler_params=_cp(2),
    )(proj, proj, proj, proj, proj, bias, lse, wts, dout, cbar)


def att_bwd_kv(proj, bias, lse, wts, dout, cbar, g, d):
    S = proj.shape[0]
    ch = BAND * d
    nb = S // ch

    def body(k_ref, v_ref, b_ref, q0_ref, l0_ref, w0_ref, do0_ref, cb0_ref,
             q1_ref, l1_ref, w1_ref, do1_ref, cb1_ref, dk_ref, dv_ref):
        j = pl.program_id(1)
        head0 = _head_lanes()
        has_next = j + 1 < nb
        sides = ((q0_ref, l0_ref, w0_ref, do0_ref, cb0_ref, 1), (q1_ref, l1_ref, w1_ref, do1_ref, cb1_ref, 0))
        for r in range(d):
            rows = _rows(r, d)
            k = k_ref[rows, :].astype(MXU_DTYPE)
            v = v_ref[rows, :].astype(MXU_DTYPE)
            dk = jnp.zeros((BAND, 128), F32)
            dv = jnp.zeros((BAND, 128), F32)
            for q_ref, l_ref, w_ref, do_ref, cb_ref, half in sides:
                q = q_ref[rows, :]
                w = w_ref[rows, :]
                do = w * do_ref[rows, :]
                delta = w * cb_ref[rows, :]
                lse_b = l_ref[rows, :]
                for hh in range(2):
                    sel = head0 if hh == 0 else jnp.logical_not(head0)
                    p, ds = _att_pair(q, k, v, b_ref[hh][:, half * BAND:(half + 1) * BAND], lse_b, do, delta, hh, head0)
                    if half == 0:
                        p = jnp.where(has_next, p, 0.0)
                        ds = jnp.where(has_next, ds, 0.0)
                    dk = dk + jnp.where(sel, _dot(ds * (ATT_HEAD_DIM ** -0.5), q, "tn"), 0.0)
                    dv = dv + jnp.where(sel, _dot(p, do, "tn"), 0.0)
            dk_ref[rows, :] = dk.astype(dk_ref.dtype)
            dv_ref[rows, :] = dv.astype(dv_ref.dtype)

    def col(part):
        return lambda hp, j: (j, _att_cols(part, g, hp))

    blk = (ch, 128)
    cur = pl.BlockSpec(blk, lambda hp, j: (j, hp))
    nxt = pl.BlockSpec(blk, lambda hp, j: (jnp.minimum(j + 1, nb - 1), hp))
    q_next = pl.BlockSpec(blk, lambda hp, j: (jnp.minimum(j + 1, nb - 1), _att_cols(0, g, hp)))
    return pl.pallas_call(
        body, name=f"att_bwd_kv_d{d}", grid=(4, nb),
        in_specs=[pl.BlockSpec(blk, col(1)), pl.BlockSpec(blk, col(2)),
                  pl.BlockSpec((2, BAND, 2 * BAND), lambda hp, j: (hp, 0, 0)),
                  pl.BlockSpec(blk, col(0)), cur, cur, cur, cur, q_next, nxt, nxt, nxt, nxt],
        out_specs=[cur, cur],
        out_shape=[jax.ShapeDtypeStruct((S, W_BRANCH), F32), jax.ShapeDtypeStruct((S, W_BRANCH), F32)],
        compiler_params=_cp(2),
    )(proj, proj, bias, proj, lse, wts, dout, cbar, proj, lse, wts, dout, cbar)


def att_combine(os_, lses):
    S = os_[0].shape[0]

    def body(o0, o1, o2, l0, l1, l2, out_ref, w0, w1, w2):
        ls = [l0[...], l1[...], l2[...]]
        m = jnp.maximum(jnp.maximum(ls[0], ls[1]), ls[2])
        es = [jnp.exp(l - m) for l in ls]
        den = es[0] + es[1] + es[2]
        ws = [e / den for e in es]
        out_ref[...] = (ws[0] * o0[...] + ws[1] * o1[...] + ws[2] * o2[...]).astype(out_ref.dtype)
        for w_ref, w in zip((w0, w1, w2), ws):
            w_ref[...] = w

    blk = pl.BlockSpec((ROW_TILE, W_BRANCH), lambda i: (i, 0))
    f = jax.ShapeDtypeStruct((S, W_BRANCH), F32)
    return pl.pallas_call(
        body, name="att_combine", grid=(S // ROW_TILE,), in_specs=[blk] * 6, out_specs=[blk] * 4,
        out_shape=[jax.ShapeDtypeStruct((S, W_BRANCH), BF16), f, f, f], compiler_params=_cp(1),
    )(*os_, *lses)


def _split3(x):
    x1 = x.astype(BF16)
    r1 = x - x1.astype(F32)
    x2 = r1.astype(BF16)
    x3 = (r1 - x2.astype(F32)).astype(BF16)
    return x1, x2, x3


def att_combine_bwd(dout, os_, wts, headsum):
    S = dout.shape[0]

    def body(do_ref, o0, o1, o2, w0, w1, w2, e_ref, cb_ref):
        out = w0[...] * o0[...] + w1[...] * o1[...] + w2[...] * o2[...]
        e = e_ref[...]
        acc = jnp.zeros((ROW_TILE, W_BRANCH), F32)
        for term in _split3(do_ref[...] * out):
            acc = acc + jnp.dot(term, e, preferred_element_type=F32)
        cb_ref[...] = acc

    blk = pl.BlockSpec((ROW_TILE, W_BRANCH), lambda i: (i, 0))
    return pl.pallas_call(
        body, name="att_combine_bwd", grid=(S // ROW_TILE,),
        in_specs=[blk] * 7 + [pl.BlockSpec((W_BRANCH, W_BRANCH), lambda i: (0, 0))], out_specs=blk,
        out_shape=jax.ShapeDtypeStruct((S, W_BRANCH), F32), compiler_params=_cp(1),
    )(dout, *os_, *wts, headsum)


def _cmul(ar, ai, br, bi):
    return ar * br - ai * bi, ar * bi + ai * br


def _scan_steps():
    return int(math.log2(SSM_T))


def s5_fwd(proj, b_re, b_im, a_re, a_im, c_re, c_im, d_skip):
    S = proj.shape[0]
    nt = S // SSM_T

    def body(u_ref, bre_ref, bim_ref, ar_ref, ai_ref, cre_ref, cim_ref, dsk_ref, hr_ref, hi_ref, y_ref, cr_ref, ci_ref):
        t = pl.program_id(1)

        @pl.when(t == 0)
        def _():
            cr_ref[...] = jnp.zeros(cr_ref.shape, F32)
            ci_ref[...] = jnp.zeros(ci_ref.shape, F32)

        u = u_ref[...]
        ar, ai = ar_ref[...], ai_ref[...]
        rows = lax.broadcasted_iota(jnp.int32, (SSM_T, W_BRANCH), 0)
        inr, ini = _cmul(ar, ai, cr_ref[0:1, :], ci_ref[0:1, :])
        xr = _dot(u, bre_ref[...]) + jnp.where(rows == 0, inr, 0.0)
        xi = _dot(u, bim_ref[...]) + jnp.where(rows == 0, ini, 0.0)
        pr, pi = ar, ai
        for k in range(_scan_steps()):
            dd = 1 << k
            sr = jnp.where(rows >= dd, pltpu.roll(xr, dd, 0), 0.0)
            si = jnp.where(rows >= dd, pltpu.roll(xi, dd, 0), 0.0)
            mr, mi = _cmul(pr, pi, sr, si)
            xr, xi = xr + mr, xi + mi
            pr, pi = _cmul(pr, pi, pr, pi)
        hr_ref[...] = xr
        hi_ref[...] = xi
        cr_ref[...] = jnp.broadcast_to(xr[SSM_T - 1:SSM_T, :], cr_ref.shape)
        ci_ref[...] = jnp.broadcast_to(xi[SSM_T - 1:SSM_T, :], ci_ref.shape)
        y_ref[...] = _dot(xr, cre_ref[...]) - _dot(xi, cim_ref[...]) + u * dsk_ref[...]

    u_spec = pl.BlockSpec((SSM_T, 128), lambda j, t: (t, OFF_SSM // 128 + j))
    b_spec = pl.BlockSpec((None, 128, W_BRANCH), lambda j, t: (j, 0, 0))
    a_spec = pl.BlockSpec((1, W_BRANCH), lambda j, t: (0, j))
    c_spec = pl.BlockSpec((None, W_BRANCH, 128), lambda j, t: (j, 0, 0))
    h_spec = pl.BlockSpec((SSM_T, W_BRANCH), lambda j, t: (t, j))
    return pl.pallas_call(
        body, name="s5_fwd", grid=(4, nt),
        in_specs=[u_spec, b_spec, b_spec, a_spec, a_spec, c_spec, c_spec, pl.BlockSpec((1, 128), lambda j, t: (0, j))],
        out_specs=[h_spec, h_spec, pl.BlockSpec((SSM_T, 128), lambda j, t: (t, j))],
        out_shape=[jax.ShapeDtypeStruct((S, SSM_COLS), F32), jax.ShapeDtypeStruct((S, SSM_COLS), F32),
                   jax.ShapeDtypeStruct((S, W_BRANCH), F32)],
        scratch_shapes=[pltpu.VMEM((8, W_BRANCH), F32), pltpu.VMEM((8, W_BRANCH), F32)],
        compiler_params=_cp(2),
    )(proj, b_re, b_im, a_re, a_im, c_re, c_im, d_skip)


def s5_bwd(proj, hr, hi, dy, b_re, b_im, a_re, a_im, c_re, c_im, d_skip):
    S = proj.shape[0]
    nt = S // SSM_T

    def body(u_ref, hr_ref, hi_ref, hpr_ref, hpi_ref, dy_ref, bre_ref, bim_ref, ar_ref, ai_ref, cre_ref, cim_ref,
             dsk_ref, du_ref, dbre_ref, dbim_ref, dar_ref, dai_ref, dcre_ref, dcim_ref, ddsk_ref, gr_ref, gi_ref):
        step = pl.program_id(1)
        t = nt - 1 - step

        @pl.when(step == 0)
        def _():
            gr_ref[...] = jnp.zeros(gr_ref.shape, F32)
            gi_ref[...] = jnp.zeros(gi_ref.shape, F32)
            for ref in (dbre_ref, dbim_ref, dar_ref, dai_ref, dcre_ref, dcim_ref, ddsk_ref):
                ref[...] = jnp.zeros(ref.shape, F32)

        u = u_ref[...]
        dy = dy_ref[...]
        ar, ai = ar_ref[...], ai_ref[...]
        rows = lax.broadcasted_iota(jnp.int32, (SSM_T, W_BRANCH), 0)
        inr, ini = _cmul(ar, -ai, gr_ref[0:1, :], gi_ref[0:1, :])
        xr = _dot(dy, cre_ref[...], "nt") + jnp.where(rows == SSM_T - 1, inr, 0.0)
        xi = -_dot(dy, cim_ref[...], "nt") + jnp.where(rows == SSM_T - 1, ini, 0.0)
        pr, pi = ar, -ai
        for k in range(_scan_steps()):
            dd = 1 << k
            sr = jnp.where(rows < SSM_T - dd, pltpu.roll(xr, SSM_T - dd, 0), 0.0)
            si = jnp.where(rows < SSM_T - dd, pltpu.roll(xi, SSM_T - dd, 0), 0.0)
            mr, mi = _cmul(pr, pi, sr, si)
            xr, xi = xr + mr, xi + mi
            pr, pi = _cmul(pr, pi, pr, pi)
        gr_ref[...] = jnp.broadcast_to(xr[0:1, :], gr_ref.shape)
        gi_ref[...] = jnp.broadcast_to(xi[0:1, :], gi_ref.shape)
        hr_blk, hi_blk = hr_ref[...], hi_ref[...]
        keep = (t > 0).astype(F32)
        hpr = jnp.where(rows >= 1, pltpu.roll(hr_blk, 1, 0), hpr_ref[7:8, :] * keep)
        hpi = jnp.where(rows >= 1, pltpu.roll(hi_blk, 1, 0), hpi_ref[7:8, :] * keep)
        dar_ref[...] += jnp.sum(hpr * xr + hpi * xi, axis=0, keepdims=True)
        dai_ref[...] += jnp.sum(hpr * xi - hpi * xr, axis=0, keepdims=True)
        dcre_ref[...] += _dot(hr_blk, dy, "tn")
        dcim_ref[...] -= _dot(hi_blk, dy, "tn")
        du = dy * dsk_ref[...] + _dot(xr, bre_ref[...], "nt") + _dot(xi, bim_ref[...], "nt")
        du_ref[...] = du.astype(du_ref.dtype)
        dbre_ref[...] += _dot(u, xr, "tn")
        dbim_ref[...] += _dot(u, xi, "tn")
        ddsk_ref[...] += jnp.sum(dy * u, axis=0, keepdims=True)

    def rev(t):
        return nt - 1 - t

    u_spec = pl.BlockSpec((SSM_T, 128), lambda j, t: (rev(t), OFF_SSM // 128 + j))
    h_spec = pl.BlockSpec((SSM_T, W_BRANCH), lambda j, t: (rev(t), j))
    hprev_spec = pl.BlockSpec((8, W_BRANCH), lambda j, t: (jnp.maximum(rev(t) * (SSM_T // 8) - 1, 0), j))
    ch_spec = pl.BlockSpec((SSM_T, 128), lambda j, t: (rev(t), j))
    b_spec = pl.BlockSpec((None, 128, W_BRANCH), lambda j, t: (j, 0, 0))
    a_spec = pl.BlockSpec((1, W_BRANCH), lambda j, t: (0, j))
    c_spec = pl.BlockSpec((None, W_BRANCH, 128), lambda j, t: (j, 0, 0))
    d_spec = pl.BlockSpec((1, 128), lambda j, t: (0, j))
    return pl.pallas_call(
        body, name="s5_bwd", grid=(4, nt),
        in_specs=[u_spec, h_spec, h_spec, hprev_spec, hprev_spec, ch_spec, b_spec, b_spec, a_spec, a_spec,
                  c_spec, c_spec, d_spec],
        out_specs=[ch_spec, b_spec, b_spec, a_spec, a_spec, c_spec, c_spec, d_spec],
        out_shape=[jax.ShapeDtypeStruct((S, W_BRANCH), BF16),
                   jax.ShapeDtypeStruct((4, 128, W_BRANCH), F32), jax.ShapeDtypeStruct((4, 128, W_BRANCH), F32),
                   jax.ShapeDtypeStruct((1, SSM_COLS), F32), jax.ShapeDtypeStruct((1, SSM_COLS), F32),
                   jax.ShapeDtypeStruct((4, W_BRANCH, 128), F32), jax.ShapeDtypeStruct((4, W_BRANCH, 128), F32),
                   jax.ShapeDtypeStruct((1, W_BRANCH), F32)],
        scratch_shapes=[pltpu.VMEM((8, W_BRANCH), F32), pltpu.VMEM((8, W_BRANCH), F32)],
        compiler_params=_cp(2),
    )(proj, hr, hi, hr, hi, dy, b_re, b_im, a_re, a_im, c_re, c_im, d_skip)


def glu_fwd(y, w_glu, b_glu):
    S = y.shape[0]

    def body(y_ref, w_ref, b_ref, o_ref):
        g = _gelu(y_ref[...])
        o_ref[...] = (g * _sigmoid(_dot(g, w_ref[...]) + b_ref[...])).astype(o_ref.dtype)

    blk = pl.BlockSpec((ROW_TILE, W_BRANCH), lambda i: (i, 0))
    return pl.pallas_call(
        body, name="glu_fwd", grid=(S // ROW_TILE,),
        in_specs=[blk, pl.BlockSpec((W_BRANCH, W_BRANCH), lambda i: (0, 0)), pl.BlockSpec((1, W_BRANCH), lambda i: (0, 0))],
        out_specs=blk, out_shape=jax.ShapeDtypeStruct((S, W_BRANCH), BF16), compiler_params=_cp(1),
    )(y, w_glu, b_glu)


def glu_bwd(y, w_glu, b_glu, dout):
    S = y.shape[0]

    def body(y_ref, w_ref, b_ref, do_ref, dy_ref, dw_ref, db_ref):
        yv = y_ref[...]
        do = do_ref[...]
        g = _gelu(yv)
        s = _sigmoid(_dot(g, w_ref[...]) + b_ref[...])
        dz = do * g * s * (1.0 - s)
        dg = do * s + _dot(dz, w_ref[...], "nt")
        dy_ref[...] = dg * _gelu_grad(yv)
        dw = _dot(g, dz, "tn")
        db = jnp.sum(dz, axis=0, keepdims=True)

        @pl.when(pl.program_id(0) == 0)
        def _():
            dw_ref[...] = dw
            db_ref[...] = db

        @pl.when(pl.program_id(0) > 0)
        def _():
            dw_ref[...] += dw
            db_ref[...] += db

    blk = pl.BlockSpec((ROW_TILE, W_BRANCH), lambda i: (i, 0))
    mat = pl.BlockSpec((W_BRANCH, W_BRANCH), lambda i: (0, 0))
    vec = pl.BlockSpec((1, W_BRANCH), lambda i: (0, 0))
    return pl.pallas_call(
        body, name="glu_bwd", grid=(S // ROW_TILE,), in_specs=[blk, mat, vec, blk], out_specs=[blk, mat, vec],
        out_shape=[jax.ShapeDtypeStruct((S, W_BRANCH), F32), jax.ShapeDtypeStruct((W_BRANCH, W_BRANCH), F32),
                   jax.ShapeDtypeStruct((1, W_BRANCH), F32)],
        compiler_params=_cp(1),
    )(y, w_glu, b_glu, dout)


SGU_TILE = 512
SGU_U_BLOCK = OFF_SGU // W_BRANCH
SGU_V_BLOCK = SGU_U_BLOCK + 1


def _sgu_norm(zv):
    v = _gelu(zv)
    mu = jnp.mean(v, axis=-1, keepdims=True)
    vc = v - mu
    rstd = lax.rsqrt(jnp.mean(vc * vc, axis=-1, keepdims=True) + EPS)
    return vc * rstd, rstd


def _tril():
    return lax.broadcasted_iota(jnp.int32, (SGU_CHUNK, SGU_CHUNK), 0) >= lax.broadcasted_iota(jnp.int32, (SGU_CHUNK, SGU_CHUNK), 1)


def sgu_fwd(proj, ln_g, ln_b, w_s, b_s_t):
    S = proj.shape[0]

    def body(zu_ref, zv_ref, g_ref, b_ref, ws_ref, bs_ref, o_ref, vf_ref):
        vn, _ = _sgu_norm(zv_ref[...])
        vf_ref[...] = vn * g_ref[...] + b_ref[...]
        tri = _tril()
        for gi in range(4):
            ws = jnp.where(tri, ws_ref[gi], 0.0)
            cols = slice(gi * 128, (gi + 1) * 128)
            for c in range(SGU_TILE // SGU_CHUNK):
                rows = slice(c * SGU_CHUNK, (c + 1) * SGU_CHUNK)
                sv = _dot(ws, vf_ref[rows, cols]) + bs_ref[:, gi:gi + 1]
                o_ref[rows, cols] = (_gelu(zu_ref[rows, cols]) * sv).astype(o_ref.dtype)

    blk = lambda cb: pl.BlockSpec((SGU_TILE, W_BRANCH), lambda i: (i, cb))
    vec = pl.BlockSpec((1, W_BRANCH), lambda i: (0, 0))
    return pl.pallas_call(
        body, name="sgu_fwd", grid=(S // SGU_TILE,),
        in_specs=[blk(SGU_U_BLOCK), blk(SGU_V_BLOCK), vec, vec, pl.BlockSpec((4, SGU_CHUNK, SGU_CHUNK), lambda i: (0, 0, 0)),
                  pl.BlockSpec((SGU_CHUNK, 4), lambda i: (0, 0))],
        out_specs=blk(0), out_shape=jax.ShapeDtypeStruct((S, W_BRANCH), BF16),
        scratch_shapes=[pltpu.VMEM((SGU_TILE, W_BRANCH), F32)], compiler_params=_cp(1),
    )(proj, proj, ln_g, ln_b, w_s, b_s_t)


def sgu_bwd(proj, ln_g, ln_b, w_s, b_s_t, dout):
    S = proj.shape[0]

    def body(zu_ref, zv_ref, g_ref, b_ref, ws_ref, bs_ref, do_ref, dzu_ref, dzv_ref, dg_ref, db_ref, dws_ref, dbs_ref,
             vf_ref, dvf_ref):
        @pl.when(pl.program_id(0) == 0)
        def _():
            for ref in (dg_ref, db_ref, dws_ref, dbs_ref):
                ref[...] = jnp.zeros(ref.shape, F32)

        vn, rstd = _sgu_norm(zv_ref[...])
        vf_ref[...] = vn * g_ref[...] + b_ref[...]
        tri = _tril()
        lane = lax.broadcasted_iota(jnp.int32, (SGU_CHUNK, 128), 1)
        dbs = jnp.zeros((SGU_CHUNK, 128), F32)
        for gi in range(4):
            ws = jnp.where(tri, ws_ref[gi], 0.0)
            cols = slice(gi * 128, (gi + 1) * 128)
            dws = jnp.zeros((SGU_CHUNK, SGU_CHUNK), F32)
            for c in range(SGU_TILE // SGU_CHUNK):
                rows = slice(c * SGU_CHUNK, (c + 1) * SGU_CHUNK)
                vf = vf_ref[rows, cols]
                zu = zu_ref[rows, cols]
                do = do_ref[rows, cols]
                sv = _dot(ws, vf) + bs_ref[:, gi:gi + 1]
                dzu_ref[rows, cols] = (do * sv * _gelu_grad(zu)).astype(dzu_ref.dtype)
                dsv = do * _gelu(zu)
                dvf_ref[rows, cols] = _dot(ws, dsv, "tn")
                dws = dws + _dot(dsv, vf, "nt")
                dbs = dbs + jnp.where(lane == gi, jnp.sum(dsv, axis=-1, keepdims=True), 0.0)
            dws_ref[gi] += jnp.where(tri, dws, 0.0)
        dbs_ref[...] += dbs
        dvf = dvf_ref[...]
        dg_ref[...] += jnp.sum(dvf * vn, axis=0, keepdims=True)
        db_ref[...] += jnp.sum(dvf, axis=0, keepdims=True)
        dvn = dvf * g_ref[...]
        dv = rstd * (dvn - jnp.mean(dvn, axis=-1, keepdims=True) - vn * jnp.mean(dvn * vn, axis=-1, keepdims=True))
        dzv_ref[...] = (dv * _gelu_grad(zv_ref[...])).astype(dzv_ref.dtype)

    blk = lambda cb: pl.BlockSpec((SGU_TILE, W_BRANCH), lambda i: (i, cb))
    vec = pl.BlockSpec((1, W_BRANCH), lambda i: (0, 0))
    ws_spec = pl.BlockSpec((4, SGU_CHUNK, SGU_CHUNK), lambda i: (0, 0, 0))
    return pl.pallas_call(
        body, name="sgu_bwd", grid=(S // SGU_TILE,),
        in_specs=[blk(SGU_U_BLOCK), blk(SGU_V_BLOCK), vec, vec, ws_spec, pl.BlockSpec((SGU_CHUNK, 4), lambda i: (0, 0)),
                  blk(0)],
        out_specs=[blk(0), blk(0), vec, vec, ws_spec, pl.BlockSpec((SGU_CHUNK, 128), lambda i: (0, 0))],
        out_shape=[jax.ShapeDtypeStruct((S, W_BRANCH), BF16), jax.ShapeDtypeStruct((S, W_BRANCH), BF16),
                   jax.ShapeDtypeStruct((1, W_BRANCH), F32), jax.ShapeDtypeStruct((1, W_BRANCH), F32),
                   jax.ShapeDtypeStruct((4, SGU_CHUNK, SGU_CHUNK), F32), jax.ShapeDtypeStruct((SGU_CHUNK, 128), F32)],
        scratch_shapes=[pltpu.VMEM((SGU_TILE, W_BRANCH), F32), pltpu.VMEM((SGU_TILE, W_BRANCH), F32)],
        compiler_params=_cp(1),
    )(proj, proj, ln_g, ln_b, w_s, b_s_t, dout)


GM_TILE = 512


def _gate_specs(order):
    def spec(i):
        def index(*ids):
            m, n = order(*ids)
            return (m, (OFF_GATE + i * D_MODEL) // GM_TILE + n)
        return pl.BlockSpec((GM_TILE, GM_TILE), index)
    return [spec(i) for i in range(4)]


def merge_fwd(proj, gate_b, branches, w_up):
    S = proj.shape[0]
    order = lambda n, m: (m, n)

    def body(p0, p1, p2, p3, gb_ref, b0, b1, b2, b3, w_ref, o_ref):
        acc = jnp.zeros((GM_TILE, GM_TILE), F32)
        for i, (p_ref, br_ref) in enumerate(zip((p0, p1, p2, p3), (b0, b1, b2, b3))):
            acc = acc + _sigmoid(p_ref[...] + gb_ref[i:i + 1, :]) * _dot(br_ref[...], w_ref[i])
        o_ref[...] = acc.astype(o_ref.dtype)

    br_spec = pl.BlockSpec((GM_TILE, W_BRANCH), lambda n, m: (m, 0))
    return pl.pallas_call(
        body, name="merge_fwd", grid=(D_MODEL // GM_TILE, S // GM_TILE),
        in_specs=_gate_specs(order) + [pl.BlockSpec((4, GM_TILE), lambda n, m: (0, n))] + [br_spec] * 4
        + [pl.BlockSpec((4, W_BRANCH, GM_TILE), lambda n, m: (0, 0, n))],
        out_specs=pl.BlockSpec((GM_TILE, GM_TILE), lambda n, m: (m, n)),
        out_shape=jax.ShapeDtypeStruct((S, D_MODEL), BF16), compiler_params=_cp(2),
    )(proj, proj, proj, proj, gate_b, *branches, w_up)


def merge_bwd(proj, gate_b, branches, w_up, dmerged):
    S = proj.shape[0]
    order = lambda n, m: (m, n)

    def body(p0, p1, p2, p3, gb_ref, b0, b1, b2, b3, w_ref, dm_ref, dp0, dp1, dp2, dp3, du0, du1, du2, du3, dgb_ref):
        dm = dm_ref[...]
        dgb = []
        for i, (p_ref, br_ref, dp_ref, du_ref) in enumerate(
                zip((p0, p1, p2, p3), (b0, b1, b2, b3), (dp0, dp1, dp2, dp3), (du0, du1, du2, du3))):
            gate = _sigmoid(p_ref[...] + gb_ref[i:i + 1, :])
            dpre = dm * _dot(br_ref[...], w_ref[i]) * gate * (1.0 - gate)
            dp_ref[...] = dpre.astype(dp_ref.dtype)
            du_ref[...] = (dm * gate).astype(du_ref.dtype)
            dgb.append(jnp.sum(dpre, axis=0, keepdims=True))
        dgb = jnp.concatenate(dgb, axis=0)

        @pl.when(pl.program_id(1) == 0)
        def _():
            dgb_ref[...] = dgb

        @pl.when(pl.program_id(1) > 0)
        def _():
            dgb_ref[...] += dgb

    br_spec = pl.BlockSpec((GM_TILE, W_BRANCH), lambda n, m: (m, 0))
    mn = pl.BlockSpec((GM_TILE, GM_TILE), lambda n, m: (m, n))
    gb = pl.BlockSpec((4, GM_TILE), lambda n, m: (0, n))
    big = jax.ShapeDtypeStruct((S, D_MODEL), BF16)
    outs = pl.pallas_call(
        body, name="merge_bwd", grid=(D_MODEL // GM_TILE, S // GM_TILE),
        in_specs=_gate_specs(order) + [gb] + [br_spec] * 4
        + [pl.BlockSpec((4, W_BRANCH, GM_TILE), lambda n, m: (0, 0, n)), mn],
        out_specs=[mn] * 8 + [gb], out_shape=[big] * 8 + [jax.ShapeDtypeStruct((4, D_MODEL), F32)],
        compiler_params=_cp(2),
    )(proj, proj, proj, proj, gate_b, *branches, w_up, dmerged)
    return outs[0:4], outs[4:8], outs[8]


def _xatt_probs(q, k):
    s = _dot(q, k, "nt") * (X_HEAD_DIM ** -0.5)
    p = jnp.exp(s - jnp.max(s, axis=-1, keepdims=True))
    return p / jnp.sum(p, axis=-1, keepdims=True)


def xatt_fwd(q, kv):
    S = q.shape[0]

    def body(q_ref, kv_ref, o_ref):
        for h in range(X_HEADS):
            cols = slice(h * X_HEAD_DIM, (h + 1) * X_HEAD_DIM)
            p = _xatt_probs(q_ref[:, cols], kv_ref[:, cols])
            o_ref[:, cols] = _dot(p, kv_ref[:, W_BRANCH + h * X_HEAD_DIM:W_BRANCH + (h + 1) * X_HEAD_DIM]).astype(o_ref.dtype)

    blk = pl.BlockSpec((ROW_TILE, W_BRANCH), lambda i: (i, 0))
    return pl.pallas_call(
        body, name="xatt_fwd", grid=(S // ROW_TILE,),
        in_specs=[blk, pl.BlockSpec((N_MEM, 2 * W_BRANCH), lambda i: (0, 0))], out_specs=blk,
        out_shape=jax.ShapeDtypeStruct((S, W_BRANCH), BF16), compiler_params=_cp(1),
    )(q, kv)


def xatt_bwd(q, kv, do):
    S = q.shape[0]

    def body(q_ref, kv_ref, do_ref, dq_ref, dkv_ref):
        @pl.when(pl.program_id(0) == 0)
        def _():
            dkv_ref[...] = jnp.zeros(dkv_ref.shape, F32)

        for h in range(X_HEADS):
            cols = slice(h * X_HEAD_DIM, (h + 1) * X_HEAD_DIM)
            vcols = slice(W_BRANCH + h * X_HEAD_DIM, W_BRANCH + (h + 1) * X_HEAD_DIM)
            qh, kh, doh = q_ref[:, cols], kv_ref[:, cols], do_ref[:, cols]
            p = _xatt_probs(qh, kh)
            dp = _dot(doh, kv_ref[:, vcols], "nt")
            ds = p * (dp - jnp.sum(dp * p, axis=-1, keepdims=True)) * (X_HEAD_DIM ** -0.5)
            dq_ref[:, cols] = _dot(ds, kh).astype(dq_ref.dtype)
            dkv_ref[:, cols] += _dot(ds, qh, "tn")
            dkv_ref[:, vcols] += _dot(p, doh, "tn")

    blk = pl.BlockSpec((ROW_TILE, W_BRANCH), lambda i: (i, 0))
    kv_spec = pl.BlockSpec((N_MEM, 2 * W_BRANCH), lambda i: (0, 0))
    return pl.pallas_call(
        body, name="xatt_bwd", grid=(S // ROW_TILE,), in_specs=[blk, kv_spec, blk], out_specs=[blk, kv_spec],
        out_shape=[jax.ShapeDtypeStruct((S, W_BRANCH), BF16), jax.ShapeDtypeStruct((N_MEM, 2 * W_BRANCH), F32)],
        compiler_params=_cp(1),
    )(q, kv, do)


def s5_params(a_re, a_im, log_dt, b_re, b_im, c_re, c_im):
    lam_re = jnp.minimum(a_re, -1e-4)
    lam_im = a_im
    dt = jnp.exp(log_dt)[:, None]
    mag = jnp.exp(lam_re * dt)
    ab_re, ab_im = mag * jnp.cos(lam_im * dt), mag * jnp.sin(lam_im * dt)
    den = lam_re * lam_re + lam_im * lam_im
    f_re = ((ab_re - 1.0) * lam_re + ab_im * lam_im) / den
    f_im = (ab_im * lam_re - (ab_re - 1.0) * lam_im) / den
    bb_re = f_re[..., None] * b_re - f_im[..., None] * b_im
    bb_im = f_re[..., None] * b_im + f_im[..., None] * b_re
    eye = jnp.eye(8, dtype=F32)

    def b_blocks(bb):
        t = bb.reshape(4, 8, SSM_STATE, SSM_GROUP).transpose(0, 1, 3, 2)
        return (t[:, :, :, None, :] * eye[None, :, None, :, None]).reshape(4, 128, W_BRANCH)

    def c_blocks(cc):
        t = cc.reshape(4, 8, SSM_GROUP, SSM_STATE).transpose(0, 1, 3, 2)
        return (t[:, :, :, None, :] * eye[None, :, None, :, None]).reshape(4, W_BRANCH, 128)

    return (ab_re.reshape(1, SSM_COLS), ab_im.reshape(1, SSM_COLS), b_blocks(bb_re), b_blocks(bb_im),
            c_blocks(c_re), c_blocks(c_im))


ANY = pl.BlockSpec(memory_space=pl.ANY)


def _chip_index():
    return 2 * lax.axis_index("x") + lax.axis_index("y")


def _peer_chip(j):
    x, y, c = lax.axis_index("x"), lax.axis_index("y"), lax.axis_index("c")
    return ((1 - x) if j & 2 else x, (1 - y) if j & 1 else y, c)


def _piece(ref, axis, s, n):
    size = ref.shape[axis] // n
    idx = [slice(None)] * len(ref.shape)
    idx[axis] = pl.ds(s * size, size)
    return ref.at[tuple(idx)]


HBM_SPEC = pl.BlockSpec(memory_space=pltpu.HBM)
SEM_SPEC = pl.BlockSpec(memory_space=pltpu.SEMAPHORE)
SIDE_EFFECT = pltpu.SideEffectType.DATAFLOW_SIDE_EFFECTING


def _chip_copies(ins, lands, send, recv, axes, mode, k):
    pairs = []
    for t in range(len(ins)):
        for j in (1, 2, 3):
            if mode == "gather":
                src, dst, arrives = ins[t], _piece(lands[t], axes[t], k, 4), _piece(lands[t], axes[t], k ^ j, 4)
            else:
                src = ins[t] if axes[t] is None else _piece(ins[t], axes[t], k ^ j, 4)
                dst, arrives = lands[t].at[k], lands[t].at[k ^ j]
            sems = dict(send_sem=send.at[3 * t + j - 1], recv_sem=recv.at[3 * t + j - 1], device_id=_peer_chip(j),
                        device_id_type=MESH_ID)
            pairs.append((pltpu.make_async_remote_copy(src_ref=src, dst_ref=dst, **sems),
                          pltpu.make_async_remote_copy(src_ref=src, dst_ref=arrives, **sems)))
    return pairs


def chips_start(ins, lands, axes, mode, name, after=()):
    n, na = len(ins), len(after)

    def body(*refs):
        in_refs, land_refs = refs[:n], refs[n:2 * n]
        send, recv, token = refs[2 * n + na], refs[2 * n + na + 1], refs[-1]
        q = _chip_index()
        for k in range(4):
            @pl.when(q == k)
            def _():
                for start, _ in _chip_copies(in_refs, land_refs, send, recv, axes, mode, k):
                    start.start()
        token[...] = jnp.zeros(token.shape, token.dtype)

    hbm = lambda a: pltpu.HBM(a.shape, a.dtype)
    outs = pl.pallas_call(
        body, name=name, in_specs=[HBM_SPEC] * (2 * n) + [ANY] * na,
        out_specs=[SEM_SPEC, SEM_SPEC] + [HBM_SPEC] * (2 * n) + [pl.BlockSpec(memory_space=pltpu.VMEM)],
        out_shape=[pltpu.SemaphoreType.DMA((3 * n,)), pltpu.SemaphoreType.DMA((3 * n,))]
        + [hbm(a) for a in ins] + [hbm(a) for a in lands] + [jax.ShapeDtypeStruct((8, 128), F32)],
        input_output_aliases={i: 2 + i for i in range(2 * n)},
        compiler_params=pltpu.CompilerParams(has_side_effects=SIDE_EFFECT),
    )(*[pltpu.with_memory_space_constraint(a, pltpu.HBM) for a in list(ins) + list(lands)], *after)
    return outs[0], outs[1], outs[2:2 + n], outs[2 + n:2 + 2 * n], outs[-1]


def chips_wait(send, recv, ins, lands, axes, mode, name, after=()):
    n = len(ins)

    def body(*refs):
        in_refs, land_refs = refs[:n], refs[n:2 * n]
        send_ref, recv_ref = refs[2 * n], refs[2 * n + 1]
        q = _chip_index()
        for k in range(4):
            @pl.when(q == k)
            def _():
                for _, wait in _chip_copies(in_refs, land_refs, send_ref, recv_ref, axes, mode, k):
                    wait.wait_send()
                    wait.wait_recv()

    hbm = lambda a: pltpu.HBM(a.shape, a.dtype)
    outs = pl.pallas_call(
        body, name=name, in_specs=[HBM_SPEC] * (2 * n) + [SEM_SPEC, SEM_SPEC] + [ANY] * len(after),
        out_specs=[HBM_SPEC] * (2 * n), out_shape=[hbm(a) for a in ins] + [hbm(a) for a in lands],
        input_output_aliases={i: i for i in range(2 * n)},
        compiler_params=pltpu.CompilerParams(has_side_effects=SIDE_EFFECT),
    )(*ins, *lands, send, recv, *after)
    return outs[:n], outs[n:]


def swap_cores(arrs, name):
    n = len(arrs)

    def body(*refs):
        ins, outs = refs[:n], refs[n:2 * n]
        send, recv = refs[2 * n:]
        sibling = (lax.axis_index("x"), lax.axis_index("y"), 1 - lax.axis_index("c"))
        copies = [pltpu.make_async_remote_copy(src_ref=ins[t], dst_ref=outs[t], send_sem=send.at[t], recv_sem=recv.at[t],
                                               device_id=sibling, device_id_type=MESH_ID) for t in range(n)]
        for cp in copies:
            cp.start()
        for cp in copies:
            cp.wait()

    return pl.pallas_call(
        body, name=name, in_specs=[ANY] * n, out_specs=[ANY] * n,
        out_shape=[jax.ShapeDtypeStruct(a.shape, a.dtype) for a in arrs],
        scratch_shapes=[pltpu.SemaphoreType.DMA((n,)), pltpu.SemaphoreType.DMA((n,))],
    )(*arrs)


ELEMENTWISE_BLOCK_BYTES = 1 << 20


def _row_tile(rows, cols):
    want = max(8, ELEMENTWISE_BLOCK_BYTES // (4 * cols))
    fits = [t for t in range(8, min(rows, want) + 1, 8) if rows % t == 0]
    return fits[-1] if fits else rows


def sum_chips(recv, own, axis, chip, stacked, l, name):
    _, r, c = recv.shape
    tr = _row_tile(r, c)
    nrt = r // tr

    def body(chip_ref, r_ref, own_ref, stacked_ref, o_ref):
        for k in range(4):
            @pl.when(chip_ref[0] == k)
            def _():
                terms = [own_ref[...] if s == k else r_ref[s] for s in range(4)]
                o_ref[...] = ((terms[0] + terms[1]) + terms[2]) + terms[3]

    own_index = {0: lambda i, q: (q[0] * nrt + i, 0), 1: lambda i, q: (i, q[0]), None: lambda i, q: (i, 0)}[axis]
    return pl.pallas_call(
        body, name=name,
        grid_spec=pltpu.PrefetchScalarGridSpec(
            num_scalar_prefetch=1, grid=(nrt,),
            in_specs=[pl.BlockSpec((4, tr, c), lambda i, q: (0, i, 0)), pl.BlockSpec((tr, c), own_index), ANY],
            out_specs=pl.BlockSpec((None, tr, c), lambda i, q: (l, i, 0))),
        out_shape=jax.ShapeDtypeStruct(stacked.shape, F32), input_output_aliases={3: 0}, compiler_params=_cp(1),
    )(chip, recv, own, stacked)


def adamw(w, ga, gb, m, v, name):
    rows, cols = w.shape
    tr = _row_tile(rows, cols)

    def body(w_ref, ga_ref, gb_ref, m_ref, v_ref, g_ref, d_ref, nm_ref, nv_ref):
        g = ga_ref[...] + gb_ref[...]
        nm = ADAM_B1 * m_ref[...] + (1.0 - ADAM_B1) * g
        nv = ADAM_B2 * v_ref[...] + (1.0 - ADAM_B2) * (g * g)
        m_hat = nm / (1.0 - ADAM_B1 ** ADAM_STEP)
        v_hat = nv / (1.0 - ADAM_B2 ** ADAM_STEP)
        g_ref[...] = g
        nm_ref[...] = nm
        nv_ref[...] = nv
        d_ref[...] = -ADAM_LR * (m_hat / (jnp.sqrt(v_hat) + ADAM_EPS) + ADAM_WD * w_ref[...])

    blk = pl.BlockSpec((tr, cols), lambda i: (i, 0))
    f = jax.ShapeDtypeStruct((rows, cols), F32)
    return pl.pallas_call(
        body, name=name, grid=(rows // tr,), in_specs=[blk] * 5, out_specs=[blk] * 4, out_shape=[f] * 4,
        compiler_params=_cp(1),
    )(w, ga, gb, m, v)


PACK_ALIGN = 1024
PACK_ROWS_ALIGN = 2048


def pack_small(arrs):
    parts = []
    for a in arrs:
        flat = a.reshape(-1)
        pad = (-flat.shape[0]) % PACK_ALIGN
        parts.append(jnp.pad(flat, (0, pad)) if pad else flat)
    rows = sum(p.shape[0] for p in parts) // 128
    parts.append(jnp.zeros(((-rows) % PACK_ROWS_ALIGN * 128,), arrs[0].dtype))
    return jnp.concatenate(parts).reshape(-1, 128)


def unpack_small(packed, shapes):
    out, row = [], 0
    for shape in shapes:
        size = int(np.prod(shape))
        rows = -(-size // PACK_ALIGN) * 8
        out.append(packed[row:row + rows].reshape(-1)[:size].reshape(shape))
        row += rows
    return out


def layer_fwd(x, mem, w_in, rest_of, P, biases, after=()):
    sv = {"x0": x}
    h1 = rms_fwd(x, P["g_mix_pre"], BF16, "rms_pre")
    proj = mm(h1, w_in, "nn", tm=1024, tn=768, tk=1024, out_dtypes=[F32], name="mm_w_in", after=after)
    a_out = pool_fwd(proj, P["pool_w"], P["pool_scale"])
    os_, lses = [], []
    for g, (win, dil) in enumerate(DIL_GROUPS):
        o, lse = att_fwd(proj, biases[g], g, dil)
        os_.append(o)
        lses.append(lse)
    b_out, w0, w1, w2 = att_combine(os_, lses)
    s5p = P["s5"]
    hr, hi, y = s5_fwd(proj, s5p[2], s5p[3], s5p[0], s5p[1], s5p[4], s5p[5], P["d_skip"])
    d_out = sgu_fwd(proj, P["sgu_ln_g"], P["sgu_ln_b"], P["w_s"], P["b_s_t"])
    W, after_rest = rest_of(d_out)
    W = dict(W, w_in=w_in)
    c_out = glu_fwd(y, W["w_glu"], P["b_glu"])
    branches = (a_out, b_out, c_out, d_out)
    merged = merge_fwd(proj, W["gate_b"], branches, W["w_up"])
    t1 = mm(merged, W["w_out"], "nn", tm=1024, tn=1024, tk=1024, out_dtypes=[F32], name="mm_w_out", after=after_rest)
    x1 = rms_fwd(t1, P["g_mix_post"], F32, "rms_post", res=x)
    sv.update(h1=h1, proj=proj, os=os_, lses=lses, wts=(w0, w1, w2), hr=hr, hi=hi, y=y, branches=branches,
              merged=merged, t1=t1, x1=x1)

    h2 = rms_fwd(x1, P["g_x_pre"], BF16, "rms_pre")
    mem_n = rms_fwd(mem, P["g_mem"], BF16, "rms_mem")
    q = mm(h2, W["w_cq"], "nn", tm=1024, tn=512, tk=1024, out_dtypes=[BF16], name="mm_w_cq")
    kv = mm(mem_n, W["w_ckv"], "nn", tm=256, tn=1024, tk=1024, out_dtypes=[BF16], name="mm_w_ckv")
    ox = xatt_fwd(q, kv)
    t2 = mm(ox, W["w_co"], "nn", tm=1024, tn=1024, tk=512, out_dtypes=[F32], name="mm_w_co")
    x2 = rms_fwd(t2, P["g_x_post"], F32, "rms_post", res=x1)
    sv.update(h2=h2, mem_n=mem_n, q=q, kv=kv, ox=ox, t2=t2, x2=x2)

    h3 = rms_fwd(x2, P["g_ff_pre"], BF16, "rms_pre")
    pre, act = mm(h3, W["w_ff1"], "nn", tm=1024, tn=1024, tk=1024, out_dtypes=[F32, BF16], name="mm_w_ff1",
                  epi=lambda acc: (acc, jnp.square(jnp.maximum(acc, 0.0))))
    ff = mm(act, W["w_ff2"], "nn", tm=1024, tn=1024, tk=1024, out_dtypes=[F32], name="mm_w_ff2")
    x3 = rms_fwd(ff, P["g_ff_post"], F32, "rms_post", res=x2)
    sv.update(h3=h3, pre=pre, act=act, ff=ff, W=W)
    return x3, sv


def layer_bwd(dx, mem, W, P, biases, sv, headsum, emit, after=()):
    G = {}
    dff, G["g_ff_post"] = rms_bwd(sv["ff"], P["g_ff_post"], dx, BF16, "rms_post_bwd", after=after)
    G["w_ff2"] = mm(sv["act"], dff, "tn", tm=1024, tn=1024, tk=1024, out_dtypes=[F32], name="mm_dw_ff2")
    dpre = mm(dff, W["w_ff2"], "nt", tm=1024, tn=1024, tk=1024, out_dtypes=[BF16], name="mm_dact", extras=(sv["pre"],),
              epi=lambda acc, pre: (acc * (2.0 * jnp.maximum(pre, 0.0)),))
    G["w_ff1"] = mm(sv["h3"], dpre, "tn", tm=1024, tn=1024, tk=1024, out_dtypes=[F32], name="mm_dw_ff1")
    sent = emit(("w_ff1", "w_ff2"), G)
    dh3 = mm(dpre, W["w_ff1"], "nt", tm=1024, tn=1024, tk=1024, out_dtypes=[F32], name="mm_dh3", after=sent)
    dx2, G["g_ff_pre"] = rms_bwd(sv["x2"], P["g_ff_pre"], dh3, F32, "rms_pre_bwd", add=dx)
    dt2, G["g_x_post"] = rms_bwd(sv["t2"], P["g_x_post"], dx2, BF16, "rms_post_bwd")
    G["w_co"] = mm(sv["ox"], dt2, "tn", tm=512, tn=1024, tk=1024, out_dtypes=[F32], name="mm_dw_co")
    dox = mm(dt2, W["w_co"], "nt", tm=1024, tn=512, tk=1024, out_dtypes=[BF16], name="mm_dox")
    dq, dkv = xatt_bwd(sv["q"], sv["kv"], dox)
    G["w_cq"] = mm(sv["h2"], dq, "tn", tm=1024, tn=512, tk=1024, out_dtypes=[F32], name="mm_dw_cq")
    dh2 = mm(dq, W["w_cq"], "nt", tm=1024, tn=1024, tk=512, out_dtypes=[F32], name="mm_dh2")
    G["w_ckv"] = mm(sv["mem_n"], dkv, "tn", tm=1024, tn=1024, tk=256, out_dtypes=[F32], name="mm_dw_ckv")
    dmem_n = mm(dkv, W["w_ckv"], "nt", tm=256, tn=1024, tk=1024, out_dtypes=[F32], name="mm_dmem")
    _, G["g_mem"] = rms_bwd(mem, P["g_mem"], dmem_n, BF16, "rms_mem_bwd")
    dx1, G["g_x_pre"] = rms_bwd(sv["x1"], P["g_x_pre"], dh2, F32, "rms_pre_bwd", add=dx2)
    proj = sv["proj"]
    dt1, G["g_mix_post"] = rms_bwd(sv["t1"], P["g_mix_post"], dx1, BF16, "rms_post_bwd")
    G["w_out"] = mm(sv["merged"], dt1, "tn", tm=1024, tn=1024, tk=1024, out_dtypes=[F32], name="mm_dw_out")
    dmerged = mm(dt1, W["w_out"], "nt", tm=1024, tn=1024, tk=1024, out_dtypes=[F32], name="mm_dmerged")
    dgates, dups, G["gate_b"] = merge_bwd(proj, W["gate_b"], sv["branches"], W["w_up"], dmerged)
    dbr, dwup = [], []
    for i in range(4):
        dbr.append(mm(dups[i], W["w_up"][i], "nt", tm=1024, tn=512, tk=1024, out_dtypes=[F32], name="mm_dbranch"))
        dwup.append(mm(sv["branches"][i], dups[i], "tn", tm=512, tn=1024, tk=1024, out_dtypes=[F32], name="mm_dw_up"))
    G["w_up"] = jnp.concatenate(dwup, axis=0)
    d_pool, G["pool_w"], G["pool_scale"] = pool_bwd(proj, P["pool_w"], P["pool_scale"], dbr[0])
    cbar = att_combine_bwd(dbr[1], sv["os"], sv["wts"], headsum)
    dqs, dks, dvs, dbias = [], [], [], []
    for g, (win, dil) in enumerate(DIL_GROUPS):
        dq_g, db_g = att_bwd_q(proj, biases[g], sv["lses"][g], sv["wts"][g], dbr[1], cbar, g, dil)
        dk_g, dv_g = att_bwd_kv(proj, biases[g], sv["lses"][g], sv["wts"][g], dbr[1], cbar, g, dil)
        dqs.append(dq_g)
        dks.append(dk_g)
        dvs.append(dv_g)
        dbias.append(db_g)
    G["att_bias"] = dbias
    s5p = P["s5"]
    dy, G["w_glu"], G["b_glu"] = glu_bwd(sv["y"], W["w_glu"], P["b_glu"], dbr[2])
    d_ssm, dbre, dbim, dar, dai, dcre, dcim, G["d_skip"] = s5_bwd(
        proj, sv["hr"], sv["hi"], dy, s5p[2], s5p[3], s5p[0], s5p[1], s5p[4], s5p[5], P["d_skip"])
    G["s5"] = (dar, dai, dbre, dbim, dcre, dcim)
    dzu, dzv, G["sgu_ln_g"], G["sgu_ln_b"], G["w_s"], G["b_s_t"] = sgu_bwd(
        proj, P["sgu_ln_g"], P["sgu_ln_b"], P["w_s"], P["b_s_t"], dbr[3])
    d_qkv = [d.astype(BF16) for d in dqs + dks + dvs]
    dproj = jnp.concatenate([d_pool] + d_qkv + [d_ssm, dzu, dzv] + list(dgates), axis=1)
    sent = emit(("gate_b", "w_glu", "w_up", "w_out", "w_cq", "w_ckv", "w_co"), G)
    G["w_in"] = mm(sv["h1"], dproj, "tn", tm=1024, tn=1536, tk=1024, out_dtypes=[F32], name="mm_dw_in", after=sent)
    sent = emit(("w_in",), G)
    dh1 = mm(dproj, W["w_in"], "nt", tm=1024, tn=1024, tk=1536, out_dtypes=[F32], name="mm_dh1", after=sent)
    dx0, G["g_mix_pre"] = rms_bwd(sv["x0"], P["g_mix_pre"], dh1, F32, "rms_pre_bwd", add=dx1)
    return dx0, G


def _as3d(name, a):
    shape2d, axis = SHARDED[name]
    rows, cols = shape2d
    if axis == 0:
        rows //= 4
    else:
        cols //= 4
    return a.reshape(DEPTH, rows, cols)


def kernel(x, mem, rel_bias, g_mix_pre, g_mix_post, w_in, gate_b, pool_w, pool_scale, a_re, a_im, log_dt, b_re, b_im, c_re, c_im, d_skip, w_glu, b_glu, sgu_ln_g, sgu_ln_b, w_s, b_s, w_up, w_out, g_x_pre, g_x_post, g_mem, w_cq, w_ckv, w_co, g_ff_pre, g_ff_post, w_ff1, w_ff2, loss_target, m_rel_bias, m_g_mix_pre, m_g_mix_post, m_w_in, m_gate_b, m_pool_w, m_pool_scale, m_a_re, m_a_im, m_log_dt, m_b_re, m_b_im, m_c_re, m_c_im, m_d_skip, m_w_glu, m_b_glu, m_sgu_ln_g, m_sgu_ln_b, m_w_s, m_b_s, m_w_up, m_w_out, m_g_x_pre, m_g_x_post, m_g_mem, m_w_cq, m_w_ckv, m_w_co, m_g_ff_pre, m_g_ff_post, m_w_ff1, m_w_ff2, v_rel_bias, v_g_mix_pre, v_g_mix_post, v_w_in, v_gate_b, v_pool_w, v_pool_scale, v_a_re, v_a_im, v_log_dt, v_b_re, v_b_im, v_c_re, v_c_im, v_d_skip, v_w_glu, v_b_glu, v_sgu_ln_g, v_sgu_ln_b, v_w_s, v_b_s, v_w_up, v_w_out, v_g_x_pre, v_g_x_post, v_g_mem, v_w_cq, v_w_ckv, v_w_co, v_g_ff_pre, v_g_ff_post, v_w_ff1, v_w_ff2):
    env = dict(locals())
    weights = {n: env[n] for n in WEIGHT_NAMES}
    mom_m = {n: env["m_" + n] for n in WEIGHT_NAMES}
    mom_v = {n: env["v_" + n] for n in WEIGHT_NAMES}
    x2d = x.reshape(x.shape[1], D_MODEL)
    mem2d = mem.reshape(N_MEM, D_MODEL)
    target = loss_target.reshape(x2d.shape)

    axis_of = {n: SHARDED[n][1] for n in SHARDED_NAMES}
    chip = _chip_index().astype(jnp.int32).reshape(1)
    rest_names = [n for n in SHARDED_NAMES if n != "w_in"]

    def gather_start(l, names, tag, after=()):
        shards = [_as3d(n, weights[n])[l].astype(F32 if n == "gate_b" else MXU_DTYPE) for n in names]
        ax = [axis_of[n] for n in names]
        lands = [jnp.concatenate([s] * 4, axis=a) for s, a in zip(shards, ax)]
        return (names, ax, tag) + chips_start(shards, lands, ax, "gather", f"gather_start_{tag}", after=after)

    def gather_wait(started, after):
        names, ax, tag, send, recv, shards, lands, _ = started
        _, lands = chips_wait(send, recv, shards, lands, ax, "gather", f"gather_wait_{tag}", after=after)
        W = dict(zip(names, lands))
        if "w_up" in W:
            W["w_up"] = W["w_up"].reshape(4, W_BRANCH, D_MODEL)
        return W

    biases = [att_bias(rel_bias, g, dil) for g, (_, dil) in enumerate(DIL_GROUPS)]
    lanes = np.arange(W_BRANCH) // ATT_HEAD_DIM
    headsum = jnp.asarray(lanes[:, None] == lanes[None, :], dtype=BF16)

    def small_params(l, s5_prepared):
        vec = lambda a: a[l].reshape(1, -1)
        return {
            "g_mix_pre": vec(g_mix_pre), "g_mix_post": vec(g_mix_post), "g_x_pre": vec(g_x_pre), "g_x_post": vec(g_x_post),
            "g_mem": vec(g_mem), "g_ff_pre": vec(g_ff_pre), "g_ff_post": vec(g_ff_post), "pool_w": pool_w[l],
            "pool_scale": vec(pool_scale), "d_skip": vec(d_skip), "b_glu": vec(b_glu), "sgu_ln_g": vec(sgu_ln_g),
            "sgu_ln_b": vec(sgu_ln_b), "w_s": w_s[l], "b_s_t": b_s[l].T, "s5": s5_prepared,
        }

    Ws, Ps, saved, s5_vjps = [], [], [], []
    xl = x2d
    flying = {"next": gather_start(0, ["w_in"], "0_w_in")}
    for l in range(DEPTH):
        s5_prepared, s5_vjp = jax.vjp(s5_params, a_re[l], a_im[l], log_dt[l], b_re[l], b_im[l], c_re[l], c_im[l])
        token_of = lambda started: (started[7],)
        if l == 0:
            w_in_l = gather_wait(flying["next"], ())["w_in"]
            rest = gather_start(0, rest_names, "0_rest", after=[w_in_l])
            first_after = token_of(rest)

            def rest_of(value):
                W = gather_wait(rest, [value])
                flying["next"] = gather_start(1, SHARDED_NAMES, "1", after=[W["w_out"]])
                return W, token_of(flying["next"])
        else:
            W_l = gather_wait(flying["next"], [xl])
            w_in_l, first_after = W_l["w_in"], ()
            if l + 1 < DEPTH:
                flying["next"] = gather_start(l + 1, SHARDED_NAMES, str(l + 1), after=[w_in_l])
                first_after = token_of(flying["next"])
            rest_of = lambda value, W_l=W_l: (W_l, ())
        P = small_params(l, s5_prepared)
        xl, sv = layer_fwd(xl, mem2d, w_in_l, rest_of, P, biases, after=first_after)
        Ws.append(sv["W"])
        Ps.append(P)
        saved.append(sv)
        s5_vjps.append(s5_vjp)
    loss_local, dx = loss_and_grad(xl, target)
    loss = lax.psum(loss_local, ("x", "y", "c"))

    scattered = []

    def scatter_start(l, names, srcs):
        ax = [axis_of.get(n) for n in names]
        lands = []
        for s, a in zip(srcs, ax):
            r, c = s.shape
            lands.append(lax.empty((4, r // 4 if a == 0 else r, c // 4 if a == 1 else c), F32))
        tag = f"{l}_{names[0]}"
        send, recv, srcs, lands, token = chips_start(srcs, lands, ax, "scatter", f"grads_start_{tag}")
        scattered.append((l, names, ax, tag, send, recv, srcs, lands))
        return (token,)

    grads = [None] * DEPTH
    for l in reversed(range(DEPTH)):
        emit = lambda names, G, l=l: scatter_start(l, list(names), [G[n] for n in names])
        dx, grads[l] = layer_bwd(dx, mem2d, Ws[l], Ps[l], biases, saved[l], headsum, emit)
    grad_x = dx.reshape(x.shape)

    rep = {}
    stack = lambda key, shape: jnp.stack([grads[l][key] for l in range(DEPTH)]).reshape(shape)
    for n in ("g_mix_pre", "g_mix_post", "g_x_pre", "g_x_post", "g_mem", "g_ff_pre", "g_ff_post"):
        rep[n] = stack(n, (DEPTH, D_MODEL))
    for n in ("pool_scale", "d_skip", "b_glu", "sgu_ln_g", "sgu_ln_b"):
        rep[n] = stack(n, (DEPTH, W_BRANCH))
    rep["pool_w"] = stack("pool_w", pool_w.shape)
    rep["w_s"] = stack("w_s", w_s.shape)
    rep["b_s"] = jnp.stack([grads[l]["b_s_t"][:, :4].T for l in range(DEPTH)])
    s5_grads = [s5_vjps[l](tuple(grads[l]["s5"])) for l in range(DEPTH)]
    for i, n in enumerate(("a_re", "a_im", "log_dt", "b_re", "b_im", "c_re", "c_im")):
        rep[n] = jnp.stack([s5_grads[l][i] for l in range(DEPTH)])
    dbias = [sum(grads[l]["att_bias"][g] for l in range(DEPTH)) for g in range(len(DIL_GROUPS))]
    rep["rel_bias"] = jnp.concatenate([att_bias_grad(dbias[g], dil) for g, (_, dil) in enumerate(DIL_GROUPS)], axis=1)
    rep_shapes = [weights[n].shape for n in REPLICATED_NAMES]
    packed_g = pack_small([rep[n] for n in REPLICATED_NAMES])

    scatter_start(0, ["small"], [packed_g])
    stacked = {}

    def collect(record, after):
        l, names, ax, tag, send, recv, srcs, lands = record
        srcs, lands = chips_wait(send, recv, srcs, lands, ax, "scatter", f"grads_wait_{tag}", after=after)
        for n, own, arrived, a in zip(names, srcs, lands, ax):
            if n not in stacked:
                stacked[n] = lax.empty((1 if n == "small" else DEPTH,) + arrived.shape[1:], F32)
            stacked[n] = sum_chips(arrived, own, a, chip, stacked[n], 0 if n == "small" else l, "sum_chips")

    out_g, out_d, out_m, out_v = {}, {}, {}, {}
    small = [pack_small([d[n] for n in REPLICATED_NAMES]) for d in (weights, mom_m, mom_v)]

    def update(names, tag):
        partial = [stacked[n].reshape(-1, stacked[n].shape[-1]) for n in names]
        other = swap_cores(partial, f"swap_cores_{tag}")
        for n, mine, theirs in zip(names, partial, other):
            if n == "small":
                res = adamw(small[0], mine, theirs, small[1], small[2], "adamw")
                for d, r in zip((out_g, out_d, out_m, out_v), res):
                    d.update(zip(REPLICATED_NAMES, unpack_small(r, rep_shapes)))
            else:
                flat = lambda a: a.reshape(mine.shape)
                res = adamw(flat(weights[n]), mine, theirs, flat(mom_m[n]), flat(mom_v[n]), "adamw")
                out_g[n], out_d[n], out_m[n], out_v[n] = [r.reshape(weights[n].shape) for r in res]

    late = [r for r in scattered if r[1] == ["small"] or (r[0] == 0 and r[1] == ["w_in"])]
    for record in scattered:
        if not any(record is r for r in late):
            collect(record, [dx])
    update(rest_names, "rest")
    collect(late[0], [out_d[n] for n in rest_names])
    update(["w_in"], "w_in")
    collect(late[1], [out_d["w_in"]])
    update(["small"], "small")

    return (loss, grad_x, *[out_g[n] for n in WEIGHT_NAMES], *[out_d[n] for n in WEIGHT_NAMES],
            *[out_m[n] for n in WEIGHT_NAMES], *[out_v[n] for n in WEIGHT_NAMES])
```

```python
import functools
import math

import numpy as np
import jax
import jax.numpy as jnp
from jax import lax
from jax.experimental import pallas as pl
from jax.experimental.pallas import tpu as pltpu

F32 = jnp.float32
BF16 = jnp.bfloat16
MXU_DTYPE = jnp.bfloat16
MESH_ID = pl.DeviceIdType.MESH
VMEM_LIMIT_BYTES = 56 * 1024 * 1024

D_MODEL = 1024
DEPTH = 4
N_MEM = 256
W_BRANCH = 512
POOL_WINDOWS = (2, 4, 8, 16)
POOL_HALO = 16
DIL_GROUPS = ((128, 1), (512, 4), (2048, 16))
BAND = 128
ATT_HEADS = 8
ATT_HEAD_DIM = 64
SSM_GROUP = 16
SSM_GROUPS = 32
SSM_STATE = 64
SSM_COLS = SSM_GROUPS * SSM_STATE
SSM_T = 512
SGU_CHUNK = 128
X_HEADS = 4
X_HEAD_DIM = 128
D_FF = 4096
REL_BUCKETS = 32
REL_MAX_DIST = 2048
EPS = 1e-6
NEG_INF = -1e30
OFF_POOL = 0
OFF_ATT = 512
OFF_SSM = OFF_ATT + 9 * W_BRANCH
OFF_SGU = OFF_SSM + W_BRANCH
OFF_GATE = OFF_SGU + 2 * W_BRANCH
IN_WIDTH = OFF_GATE + 4 * D_MODEL

ADAM_LR = 0.001
ADAM_B1 = 0.9
ADAM_B2 = 0.999
ADAM_EPS = 1e-08
ADAM_WD = 0.01
ADAM_STEP = 10

GELU_C = math.sqrt(2.0 / math.pi)

WEIGHT_NAMES = ['rel_bias', 'g_mix_pre', 'g_mix_post', 'w_in', 'gate_b', 'pool_w', 'pool_scale', 'a_re', 'a_im',
                'log_dt', 'b_re', 'b_im', 'c_re', 'c_im', 'd_skip', 'w_glu', 'b_glu', 'sgu_ln_g', 'sgu_ln_b',
                'w_s', 'b_s', 'w_up', 'w_out', 'g_x_pre', 'g_x_post', 'g_mem', 'w_cq', 'w_ckv', 'w_co',
                'g_ff_pre', 'g_ff_post', 'w_ff1', 'w_ff2']
SHARDED = {
    'w_in': ((D_MODEL, IN_WIDTH), 1),
    'gate_b': ((4, D_MODEL), 1),
    'w_glu': ((W_BRANCH, W_BRANCH), 0),
    'w_up': ((4 * W_BRANCH, D_MODEL), 1),
    'w_out': ((D_MODEL, D_MODEL), 0),
    'w_cq': ((D_MODEL, W_BRANCH), 0),
    'w_ckv': ((D_MODEL, D_MODEL), 0),
    'w_co': ((W_BRANCH, D_MODEL), 1),
    'w_ff1': ((D_MODEL, D_FF), 1),
    'w_ff2': ((D_FF, D_MODEL), 0),
}
SHARDED_NAMES = list(SHARDED)
REPLICATED_NAMES = [n for n in WEIGHT_NAMES if n not in SHARDED]


def _cp(n_axes):
    return pltpu.CompilerParams(dimension_semantics=("arbitrary",) * n_axes, vmem_limit_bytes=VMEM_LIMIT_BYTES)


def _dot(a, b, dims="nn"):
    cd = {"nn": ((1,), (0,)), "nt": ((1,), (1,)), "tn": ((0,), (0,))}[dims]
    return lax.dot_general(a.astype(MXU_DTYPE), b.astype(MXU_DTYPE), (cd, ((), ())), preferred_element_type=F32)


def _gelu(x):
    return 0.5 * x * (1.0 + jnp.tanh(GELU_C * (x + 0.044715 * (x * x * x))))


def _gelu_grad(x):
    t = jnp.tanh(GELU_C * (x + 0.044715 * (x * x * x)))
    return 0.5 * (1.0 + t) + 0.5 * x * (1.0 - t * t) * (GELU_C * (1.0 + 3.0 * 0.044715 * (x * x)))


def _sigmoid(x):
    return 1.0 / (1.0 + jnp.exp(-x))


def mm(a, b, dims, *, tm, tn, tk, out_dtypes, name, extras=(), epi=None, after=()):
    if dims == "tn":
        K, M = a.shape
        N = b.shape[1]
    else:
        M, K = a.shape
        N = b.shape[1] if dims == "nn" else b.shape[0]
    tm, tn, tk = min(tm, M), min(tn, N), min(tk, K)
    assert M % tm == 0 and N % tn == 0 and K % tk == 0, (name, M, N, K, tm, tn, tk)
    nk = K // tk
    ne, no = len(extras), len(out_dtypes)
    if epi is None:
        epi = lambda acc: (acc,)
    a_spec = (pl.BlockSpec((tk, tm), lambda i, j, k: (k, i)) if dims == "tn"
              else pl.BlockSpec((tm, tk), lambda i, j, k: (i, k)))
    b_spec = (pl.BlockSpec((tn, tk), lambda i, j, k: (j, k)) if dims == "nt"
              else pl.BlockSpec((tk, tn), lambda i, j, k: (k, j)))
    mn_spec = pl.BlockSpec((tm, tn), lambda i, j, k: (i, j))

    def body(a_ref, b_ref, *rest):
        extra_refs, out_refs = rest[:ne], rest[ne + len(after):ne + len(after) + no]
        part = _dot(a_ref[...], b_ref[...], dims)

        def finish(acc):
            for o_ref, r in zip(out_refs, epi(acc, *[e[...] for e in extra_refs])):
                o_ref[...] = r.astype(o_ref.dtype)

        if nk == 1:
            finish(part)
        else:
            acc_ref = rest[-1]
            k = pl.program_id(2)

            @pl.when(k == 0)
            def _():
                acc_ref[...] = part

            @pl.when(k > 0)
            def _():
                acc_ref[...] += part

            @pl.when(k == nk - 1)
            def _():
                finish(acc_ref[...])

    outs = pl.pallas_call(
        body, name=name, grid=(M // tm, N // tn, nk),
        in_specs=[a_spec, b_spec] + [mn_spec] * ne + [ANY] * len(after),
        out_specs=[mn_spec] * no,
        out_shape=[jax.ShapeDtypeStruct((M, N), dt) for dt in out_dtypes],
        scratch_shapes=[pltpu.VMEM((tm, tn), F32)] if nk > 1 else [],
        compiler_params=_cp(3),
    )(a, b, *extras, *after)
    return outs[0] if no == 1 else outs


ROW_TILE = 512


def rms_fwd(x, g, out_dtype, name, res=None):
    M, D = x.shape
    tm = min(ROW_TILE, M)

    def body(x_ref, g_ref, *rest):
        o_ref = rest[-1]
        xf = x_ref[...]
        y = xf * lax.rsqrt(jnp.mean(xf * xf, axis=-1, keepdims=True) + EPS) * g_ref[...]
        if res is not None:
            y = y + rest[0][...]
        o_ref[...] = y.astype(o_ref.dtype)

    row = pl.BlockSpec((tm, D), lambda i: (i, 0))
    return pl.pallas_call(
        body, name=name, grid=(M // tm,),
        in_specs=[row, pl.BlockSpec((1, D), lambda i: (0, 0))] + ([row] if res is not None else []),
        out_specs=row, out_shape=jax.ShapeDtypeStruct((M, D), out_dtype), compiler_params=_cp(1),
    )(x, g, *([res] if res is not None else []))


def rms_bwd(x, g, dy, dx_dtype, name, add=None, after=()):
    M, D = x.shape
    tm = min(ROW_TILE, M)

    def body(x_ref, g_ref, dy_ref, *rest):
        dx_ref, dg_ref = rest[-2], rest[-1]
        xf = x_ref[...]
        dyf = dy_ref[...].astype(F32)
        r = lax.rsqrt(jnp.mean(xf * xf, axis=-1, keepdims=True) + EPS)
        xn = xf * r
        dxn = dyf * g_ref[...]
        dx = r * (dxn - xn * jnp.mean(dxn * xn, axis=-1, keepdims=True))
        if add is not None:
            dx = dx + rest[0][...]
        dx_ref[...] = dx.astype(dx_ref.dtype)
        dg = jnp.sum(dyf * xn, axis=0, keepdims=True)

        @pl.when(pl.program_id(0) == 0)
        def _():
            dg_ref[...] = dg

        @pl.when(pl.program_id(0) > 0)
        def _():
            dg_ref[...] += dg

    row = pl.BlockSpec((tm, D), lambda i: (i, 0))
    vec = pl.BlockSpec((1, D), lambda i: (0, 0))
    return pl.pallas_call(
        body, name=name, grid=(M // tm,),
        in_specs=[row, vec, row] + ([row] if add is not None else []) + [ANY] * len(after),
        out_specs=[row, vec],
        out_shape=[jax.ShapeDtypeStruct((M, D), dx_dtype), jax.ShapeDtypeStruct((1, D), F32)],
        compiler_params=_cp(1),
    )(x, g, dy, *([add] if add is not None else []), *after)


def loss_and_grad(y, target):
    M, D = y.shape
    tm = ROW_TILE

    def body(y_ref, t_ref, part_ref, dy_ref):
        e = y_ref[...] - t_ref[...]
        dy_ref[...] = e / D
        part_ref[...] = jnp.broadcast_to(0.5 * jnp.sum(jnp.mean(e * e, axis=-1, keepdims=True), axis=0, keepdims=True),
                                         (8, 128))

    row = pl.BlockSpec((tm, D), lambda i: (i, 0))
    part, dy = pl.pallas_call(
        body, name="loss", grid=(M // tm,), in_specs=[row, row],
        out_specs=[pl.BlockSpec((8, 128), lambda i: (i, 0)), row],
        out_shape=[jax.ShapeDtypeStruct((8 * (M // tm), 128), F32), jax.ShapeDtypeStruct((M, D), F32)],
        compiler_params=_cp(1),
    )(y, target)
    return jnp.sum(part[::8, 0]), dy


POOL_ROWS = 512


def _pool_window_sum(xw, gi, roll_of):
    s1 = xw + pltpu.roll(xw, roll_of(1), 0)
    s2 = s1 + pltpu.roll(s1, roll_of(2), 0)
    s3 = s2 + pltpu.roll(s2, roll_of(4), 0)
    s4 = s3 + pltpu.roll(s3, roll_of(8), 0)
    return jnp.where(gi == 0, s1, jnp.where(gi == 1, s2, jnp.where(gi == 2, s3, s4)))


def _pool_cnt(i, gi):
    rows = lax.broadcasted_iota(jnp.int32, (POOL_ROWS, 128), 0) + i * POOL_ROWS
    w = jnp.where(gi == 0, 2, jnp.where(gi == 1, 4, jnp.where(gi == 2, 8, 16)))
    return jnp.minimum(rows + 1, w).astype(F32)


def pool_fwd(proj, pool_w, scale):
    S = proj.shape[0]
    nchunk = S // POOL_ROWS
    slab = POOL_ROWS + POOL_HALO

    def body(x_ref, w_ref, sc_ref, o_ref, pad_ref):
        gi = pl.program_id(0)
        pad_ref[0:POOL_HALO, :] = jnp.zeros((POOL_HALO, 128), F32)
        pad_ref[POOL_HALO:, :] = x_ref[...]
        for i in range(nchunk):
            xw = pad_ref[i * POOL_ROWS:i * POOL_ROWS + slab, :]
            ssum = _pool_window_sum(xw, gi, lambda d: d)[POOL_HALO:, :]
            p = ssum / _pool_cnt(i, gi) - xw[POOL_HALO:, :]
            o_ref[i * POOL_ROWS:(i + 1) * POOL_ROWS, :] = (_dot(p, w_ref[...]) * sc_ref[...]).astype(o_ref.dtype)

    return pl.pallas_call(
        body, name="pool_fwd", grid=(4,),
        in_specs=[pl.BlockSpec((S, 128), lambda g: (0, OFF_POOL // 128 + g)),
                  pl.BlockSpec((None, 128, 128), lambda g: (g, 0, 0)),
                  pl.BlockSpec((1, 128), lambda g: (0, g))],
        out_specs=pl.BlockSpec((S, 128), lambda g: (0, g)),
        out_shape=jax.ShapeDtypeStruct((S, W_BRANCH), BF16),
        scratch_shapes=[pltpu.VMEM((S + POOL_HALO, 128), F32)],
        compiler_params=_cp(1),
    )(proj, pool_w, scale)


def pool_bwd(proj, pool_w, scale, dy):
    S = proj.shape[0]
    nchunk = S // POOL_ROWS
    slab = POOL_ROWS + POOL_HALO

    def body(x_ref, w_ref, sc_ref, dy_ref, dx_ref, dw_ref, dsc_ref, pad_ref, pad2_ref, dp_ref):
        gi = pl.program_id(0)
        pad_ref[0:POOL_HALO, :] = jnp.zeros((POOL_HALO, 128), F32)
        pad_ref[POOL_HALO:, :] = x_ref[...]
        pad2_ref[S:, :] = jnp.zeros((POOL_HALO, 128), F32)
        dw = jnp.zeros((128, 128), F32)
        dsc = jnp.zeros((1, 128), F32)
        for i in range(nchunk):
            xw = pad_ref[i * POOL_ROWS:i * POOL_ROWS + slab, :]
            cnt = _pool_cnt(i, gi)
            p = _pool_window_sum(xw, gi, lambda d: d)[POOL_HALO:, :] / cnt - xw[POOL_HALO:, :]
            dyc = dy_ref[i * POOL_ROWS:(i + 1) * POOL_ROWS, :]
            dsc = dsc + jnp.sum(dyc * _dot(p, w_ref[...]), axis=0, keepdims=True)
            dys = dyc * sc_ref[...]
            dw = dw + _dot(p, dys, "tn")
            dp = _dot(dys, w_ref[...], "nt")
            dp_ref[i * POOL_ROWS:(i + 1) * POOL_ROWS, :] = dp
            pad2_ref[i * POOL_ROWS:(i + 1) * POOL_ROWS, :] = dp / cnt
        dw_ref[...] = dw
        dsc_ref[...] = dsc
        for i in range(nchunk):
            xw = pad2_ref[i * POOL_ROWS:i * POOL_ROWS + slab, :]
            fsum = _pool_window_sum(xw, gi, lambda d: slab - d)[:POOL_ROWS, :]
            rows = slice(i * POOL_ROWS, (i + 1) * POOL_ROWS)
            dx_ref[rows, :] = (fsum - dp_ref[rows, :]).astype(dx_ref.dtype)

    return pl.pallas_call(
        body, name="pool_bwd", grid=(4,),
        in_specs=[pl.BlockSpec((S, 128), lambda g: (0, OFF_POOL // 128 + g)),
                  pl.BlockSpec((None, 128, 128), lambda g: (g, 0, 0)),
                  pl.BlockSpec((1, 128), lambda g: (0, g)),
                  pl.BlockSpec((S, 128), lambda g: (0, g))],
        out_specs=[pl.BlockSpec((S, 128), lambda g: (0, g)),
                   pl.BlockSpec((None, 128, 128), lambda g: (g, 0, 0)),
                   pl.BlockSpec((1, 128), lambda g: (0, g))],
        out_shape=[jax.ShapeDtypeStruct((S, W_BRANCH), BF16), jax.ShapeDtypeStruct((4, 128, 128), F32),
                   jax.ShapeDtypeStruct((1, W_BRANCH), F32)],
        scratch_shapes=[pltpu.VMEM((S + POOL_HALO, 128), F32), pltpu.VMEM((S + POOL_HALO, 128), F32),
                        pltpu.VMEM((S, 128), F32)],
        compiler_params=_cp(1),
    )(proj, pool_w, scale, dy)


def _t5_bucket(n):
    exact = REL_BUCKETS // 2
    nf = np.maximum(n, 1).astype(np.float32)
    large = exact + (np.log(nf / exact) / np.log(REL_MAX_DIST / exact) * (REL_BUCKETS - exact)).astype(np.int32)
    large = np.minimum(large, REL_BUCKETS - 1)
    return np.where(n < exact, n, large).astype(np.int32)


def _band_onehot(dil):
    i = np.arange(BAND)[:, None]
    kk = np.arange(2 * BAND)[None, :]
    dist = BAND + i - kk
    local = (dist >= 0) & (dist <= BAND)
    bucket = _t5_bucket(np.clip(dist, 0, BAND) * dil)
    onehot = (bucket.reshape(-1, 1) == np.arange(REL_BUCKETS)[None, :]).astype(np.float32)
    return onehot, local


def att_bias(rel_bias, g, dil):
    onehot, local = _band_onehot(dil)
    tab = jnp.dot(jnp.asarray(onehot), rel_bias[:, g * ATT_HEADS:(g + 1) * ATT_HEADS], precision=lax.Precision.HIGHEST)
    bias = tab.reshape(BAND, 2 * BAND, ATT_HEADS).transpose(2, 0, 1)
    return jnp.where(jnp.asarray(local)[None], bias, NEG_INF)


def att_bias_grad(dbias, dil):
    onehot, _ = _band_onehot(dil)
    flat = dbias.transpose(1, 2, 0).reshape(BAND * 2 * BAND, ATT_HEADS)
    return jnp.dot(jnp.asarray(onehot).T, flat, precision=lax.Precision.HIGHEST)


def _head_lanes():
    return lax.broadcasted_iota(jnp.int32, (BAND, 128), 1) < ATT_HEAD_DIM


def _att_cols(part, g, hp):
    return (OFF_ATT + part * 3 * W_BRANCH + g * W_BRANCH) // 128 + hp


def _att_pair(q, k, v, bias, lse_b, do, delta_b, hh, head0, mask=None):
    sel = head0 if hh == 0 else jnp.logical_not(head0)
    s = _dot(jnp.where(sel, q, 0.0), k, "nt") * (ATT_HEAD_DIM ** -0.5) + bias
    if mask is not None:
        s = jnp.where(mask, NEG_INF, s)
    c = hh * ATT_HEAD_DIM
    p = jnp.exp(s - lse_b[:, c:c + 1])
    dp = _dot(jnp.where(sel, do, 0.0), v, "nt")
    return p, p * (dp - delta_b[:, c:c + 1])


ATT_BLOCKS = {1: 8, 4: 2, 16: 1}


def _att_rows(r, i, d):
    return pl.ds(r + d * BAND * i, BAND, stride=d) if d > 1 else pl.ds(BAND * i, BAND)


def _att_specs(g, d, nq):
    ch, pb = BAND * d * nq, BAND * d
    cur = lambda part: pl.BlockSpec((ch, 128), lambda hp, n: (n, _att_cols(part, g, hp)))
    prev = lambda part: pl.BlockSpec((pb, 128), lambda hp, n: (jnp.maximum(n * nq - 1, 0), _att_cols(part, g, hp)))
    return [cur(0), cur(1), prev(1), cur(2), prev(2)]


def _att_keys(cur_ref, prev_ref, r, i, d):
    before = cur_ref[_att_rows(r, i - 1, d), :] if i > 0 else prev_ref[_att_rows(r, 0, d), :]
    return jnp.concatenate([before, cur_ref[_att_rows(r, i, d), :]], axis=0).astype(MXU_DTYPE)


def att_fwd(proj, bias, g, d):
    S = proj.shape[0]
    nq = ATT_BLOCKS[d]
    ch = BAND * d * nq

    def body(q_ref, kc_ref, kp_ref, vc_ref, vp_ref, b_ref, o_ref, l_ref):
        n = pl.program_id(1)
        head0 = _head_lanes()
        first = jnp.logical_and(lax.broadcasted_iota(jnp.int32, (BAND, 2 * BAND), 1) < BAND, n == 0)
        for r in range(d):
            for i in range(nq):
                rows = _att_rows(r, i, d)
                q = q_ref[rows, :]
                k = _att_keys(kc_ref, kp_ref, r, i, d)
                v = _att_keys(vc_ref, vp_ref, r, i, d)
                o_h, l_h = [], []
                for hh in range(2):
                    qm = jnp.where(head0 if hh == 0 else jnp.logical_not(head0), q, 0.0)
                    s = _dot(qm, k, "nt") * (ATT_HEAD_DIM ** -0.5) + b_ref[hh]
                    if i == 0:
                        s = jnp.where(first, NEG_INF, s)
                    m = jnp.max(s, axis=-1, keepdims=True)
                    p = jnp.exp(s - m)
                    l = jnp.sum(p, axis=-1, keepdims=True)
                    o_h.append(_dot(p / l, v))
                    l_h.append(jnp.broadcast_to(m + jnp.log(l), (BAND, 128)))
                o_ref[rows, :] = jnp.where(head0, o_h[0], o_h[1])
                l_ref[rows, :] = jnp.where(head0, l_h[0], l_h[1])

    out = pl.BlockSpec((ch, 128), lambda hp, n: (n, hp))
    return pl.pallas_call(
        body, name=f"att_fwd_d{d}", grid=(4, S // ch),
        in_specs=_att_specs(g, d, nq) + [pl.BlockSpec((2, BAND, 2 * BAND), lambda hp, n: (hp, 0, 0))],
        out_specs=[out, out],
        out_shape=[jax.ShapeDtypeStruct((S, W_BRANCH), F32), jax.ShapeDtypeStruct((S, W_BRANCH), F32)],
        compiler_params=_cp(2),
    )(proj, proj, proj, proj, proj, bias)


def att_bwd(proj, bias, lse, wts, dout, cbar, g, d):
    S = proj.shape[0]
    nq = ATT_BLOCKS[d]
    ch, pb = BAND * d * nq, BAND * d
    nb = S // ch
    scale = ATT_HEAD_DIM ** -0.5

    def body(q_ref, kc_ref, kp_ref, vc_ref, vp_ref, b_ref, l_ref, w_ref, do_ref, cb_ref,
             dq_ref, dk_ref, dv_ref, ek_ref, ev_ref, db_ref):
        n = pl.program_id(1)
        head0 = _head_lanes()
        first = jnp.logical_and(lax.broadcasted_iota(jnp.int32, (BAND, 2 * BAND), 1) < BAND, n == 0)

        @pl.when(n == 0)
        def _():
            db_ref[...] = jnp.zeros(db_ref.shape, F32)

        for r in range(d):
            own_k = own_v = None
            for i in range(nq):
                rows = _att_rows(r, i, d)
                q = q_ref[rows, :]
                k = _att_keys(kc_ref, kp_ref, r, i, d)
                v = _att_keys(vc_ref, vp_ref, r, i, d)
                w = w_ref[rows, :]
                do = w * do_ref[rows, :]
                delta = w * cb_ref[rows, :]
                lse_b = l_ref[rows, :]
                dq_h, dk_h, dv_h = [], [], []
                for hh in range(2):
                    p, ds = _att_pair(q, k, v, b_ref[hh], lse_b, do, delta, hh, head0, mask=first if i == 0 else None)
                    db_ref[hh] += ds
                    ds = ds * scale
                    dq_h.append(_dot(ds, k))
                    dk_h.append(_dot(ds, q, "tn"))
                    dv_h.append(_dot(p, do, "tn"))
                dq_ref[rows, :] = jnp.where(head0, dq_h[0], dq_h[1])
                head0_keys = jnp.concatenate([head0, head0], axis=0)
                dk2 = jnp.where(head0_keys, dk_h[0], dk_h[1])
                dv2 = jnp.where(head0_keys, dv_h[0], dv_h[1])
                if i == 0:
                    ek_ref[_att_rows(r, 0, d), :] = dk2[:BAND]
                    ev_ref[_att_rows(r, 0, d), :] = dv2[:BAND]
                else:
                    dk_ref[_att_rows(r, i - 1, d), :] = own_k + dk2[:BAND]
                    dv_ref[_att_rows(r, i - 1, d), :] = own_v + dv2[:BAND]
                own_k, own_v = dk2[BAND:], dv2[BAND:]
            dk_ref[_att_rows(r, nq - 1, d), :] = own_k
            dv_ref[_att_rows(r, nq - 1, d), :] = own_v

    cur = pl.BlockSpec((ch, 128), lambda hp, n: (n, hp))
    edge = pl.BlockSpec((pb, 128), lambda hp, n: (n, hp))
    bias_spec = pl.BlockSpec((2, BAND, 2 * BAND), lambda hp, n: (hp, 0, 0))
    big = jax.ShapeDtypeStruct((S, W_BRANCH), F32)
    small = jax.ShapeDtypeStruct((nb * pb, W_BRANCH), F32)
    dq, dk, dv, ek, ev, db = pl.pallas_call(
        body, name=f"att_bwd_d{d}", grid=(4, nb),
        in_specs=_att_specs(g, d, nq) + [bias_spec, cur, cur, cur, cur],
        out_specs=[cur, cur, cur, edge, edge, bias_spec],
        out_shape=[big, big, big, small, small, jax.ShapeDtypeStruct((ATT_HEADS, BAND, 2 * BAND), F32)],
        compiler_params=_cp(2),
    )(proj, proj, proj, proj, proj, bias, lse, wts, dout, cbar)

    def with_edges(main, edges):
        if nb == 1:
            return main
        main = main.reshape(nb, ch, W_BRANCH)
        add = jnp.pad(edges.reshape(nb, pb, W_BRANCH)[1:], ((0, 1), (ch - pb, 0), (0, 0)))
        return (main + add).reshape(S, W_BRANCH)

    return dq, with_edges(dk, ek), with_edges(dv, ev), db


def att_combine(os_, lses):
    S = os_[0].shape[0]

    def body(o0, o1, o2, l0, l1, l2, out_ref, w0, w1, w2):
        ls = [l0[...], l1[...], l2[...]]
        m = jnp.maximum(jnp.maximum(ls[0], ls[1]), ls[2])
        es = [jnp.exp(l - m) for l in ls]
        den = es[0] + es[1] + es[2]
        ws = [e / den for e in es]
        out_ref[...] = (ws[0] * o0[...] + ws[1] * o1[...] + ws[2] * o2[...]).astype(out_ref.dtype)
        for w_ref, w in zip((w0, w1, w2), ws):
            w_ref[...] = w

    blk = pl.BlockSpec((ROW_TILE, W_BRANCH), lambda i: (i, 0))
    f = jax.ShapeDtypeStruct((S, W_BRANCH), F32)
    return pl.pallas_call(
        body, name="att_combine", grid=(S // ROW_TILE,), in_specs=[blk] * 6, out_specs=[blk] * 4,
        out_shape=[jax.ShapeDtypeStruct((S, W_BRANCH), BF16), f, f, f], compiler_params=_cp(1),
    )(*os_, *lses)


def _split3(x):
    x1 = x.astype(BF16)
    r1 = x - x1.astype(F32)
    x2 = r1.astype(BF16)
    x3 = (r1 - x2.astype(F32)).astype(BF16)
    return x1, x2, x3


def att_combine_bwd(dout, os_, wts, headsum):
    S = dout.shape[0]

    def body(do_ref, o0, o1, o2, w0, w1, w2, e_ref, cb_ref):
        out = w0[...] * o0[...] + w1[...] * o1[...] + w2[...] * o2[...]
        e = e_ref[...]
        acc = jnp.zeros((ROW_TILE, W_BRANCH), F32)
        for term in _split3(do_ref[...] * out):
            acc = acc + jnp.dot(term, e, preferred_element_type=F32)
        cb_ref[...] = acc

    blk = pl.BlockSpec((ROW_TILE, W_BRANCH), lambda i: (i, 0))
    return pl.pallas_call(
        body, name="att_combine_bwd", grid=(S // ROW_TILE,),
        in_specs=[blk] * 7 + [pl.BlockSpec((W_BRANCH, W_BRANCH), lambda i: (0, 0))], out_specs=blk,
        out_shape=jax.ShapeDtypeStruct((S, W_BRANCH), F32), compiler_params=_cp(1),
    )(dout, *os_, *wts, headsum)


def _cmul(ar, ai, br, bi):
    return ar * br - ai * bi, ar * bi + ai * br


def _scan_steps():
    return int(math.log2(SSM_T))


def s5_fwd(proj, b_re, b_im, a_re, a_im, c_re, c_im, d_skip):
    S = proj.shape[0]
    nt = S // SSM_T

    def body(u_ref, bre_ref, bim_ref, ar_ref, ai_ref, cre_ref, cim_ref, dsk_ref, hr_ref, hi_ref, y_ref, cr_ref, ci_ref):
        t = pl.program_id(1)

        @pl.when(t == 0)
        def _():
            cr_ref[...] = jnp.zeros(cr_ref.shape, F32)
            ci_ref[...] = jnp.zeros(ci_ref.shape, F32)

        u = u_ref[...]
        ar, ai = ar_ref[...], ai_ref[...]
        rows = lax.broadcasted_iota(jnp.int32, (SSM_T, W_BRANCH), 0)
        inr, ini = _cmul(ar, ai, cr_ref[0:1, :], ci_ref[0:1, :])
        xr = _dot(u, bre_ref[...]) + jnp.where(rows == 0, inr, 0.0)
        xi = _dot(u, bim_ref[...]) + jnp.where(rows == 0, ini, 0.0)
        pr, pi = ar, ai
        for k in range(_scan_steps()):
            dd = 1 << k
            sr = jnp.where(rows >= dd, pltpu.roll(xr, dd, 0), 0.0)
            si = jnp.where(rows >= dd, pltpu.roll(xi, dd, 0), 0.0)
            mr, mi = _cmul(pr, pi, sr, si)
            xr, xi = xr + mr, xi + mi
            pr, pi = _cmul(pr, pi, pr, pi)
        hr_ref[...] = xr
        hi_ref[...] = xi
        cr_ref[...] = jnp.broadcast_to(xr[SSM_T - 1:SSM_T, :], cr_ref.shape)
        ci_ref[...] = jnp.broadcast_to(xi[SSM_T - 1:SSM_T, :], ci_ref.shape)
        y_ref[...] = _dot(xr, cre_ref[...]) - _dot(xi, cim_ref[...]) + u * dsk_ref[...]

    u_spec = pl.BlockSpec((SSM_T, 128), lambda j, t: (t, OFF_SSM // 128 + j))
    b_spec = pl.BlockSpec((None, 128, W_BRANCH), lambda j, t: (j, 0, 0))
    a_spec = pl.BlockSpec((1, W_BRANCH), lambda j, t: (0, j))
    c_spec = pl.BlockSpec((None, W_BRANCH, 128), lambda j, t: (j, 0, 0))
    h_spec = pl.BlockSpec((SSM_T, W_BRANCH), lambda j, t: (t, j))
    return pl.pallas_call(
        body, name="s5_fwd", grid=(4, nt),
        in_specs=[u_spec, b_spec, b_spec, a_spec, a_spec, c_spec, c_spec, pl.BlockSpec((1, 128), lambda j, t: (0, j))],
        out_specs=[h_spec, h_spec, pl.BlockSpec((SSM_T, 128), lambda j, t: (t, j))],
        out_shape=[jax.ShapeDtypeStruct((S, SSM_COLS), F32), jax.ShapeDtypeStruct((S, SSM_COLS), F32),
                   jax.ShapeDtypeStruct((S, W_BRANCH), F32)],
        scratch_shapes=[pltpu.VMEM((8, W_BRANCH), F32), pltpu.VMEM((8, W_BRANCH), F32)],
        compiler_params=_cp(2),
    )(proj, b_re, b_im, a_re, a_im, c_re, c_im, d_skip)


def s5_bwd(proj, hr, hi, dy, b_re, b_im, a_re, a_im, c_re, c_im, d_skip):
    S = proj.shape[0]
    nt = S // SSM_T

    def body(u_ref, hr_ref, hi_ref, hpr_ref, hpi_ref, dy_ref, bre_ref, bim_ref, ar_ref, ai_ref, cre_ref, cim_ref,
             dsk_ref, du_ref, dbre_ref, dbim_ref, dar_ref, dai_ref, dcre_ref, dcim_ref, ddsk_ref, gr_ref, gi_ref):
        step = pl.program_id(1)
        t = nt - 1 - step

        @pl.when(step == 0)
        def _():
            gr_ref[...] = jnp.zeros(gr_ref.shape, F32)
            gi_ref[...] = jnp.zeros(gi_ref.shape, F32)
            for ref in (dbre_ref, dbim_ref, dar_ref, dai_ref, dcre_ref, dcim_ref, ddsk_ref):
                ref[...] = jnp.zeros(ref.shape, F32)

        u = u_ref[...]
        dy = dy_ref[...]
        ar, ai = ar_ref[...], ai_ref[...]
        rows = lax.broadcasted_iota(jnp.int32, (SSM_T, W_BRANCH), 0)
        inr, ini = _cmul(ar, -ai, gr_ref[0:1, :], gi_ref[0:1, :])
        xr = _dot(dy, cre_ref[...], "nt") + jnp.where(rows == SSM_T - 1, inr, 0.0)
        xi = -_dot(dy, cim_ref[...], "nt") + jnp.where(rows == SSM_T - 1, ini, 0.0)
        pr, pi = ar, -ai
        for k in range(_scan_steps()):
            dd = 1 << k
            sr = jnp.where(rows < SSM_T - dd, pltpu.roll(xr, SSM_T - dd, 0), 0.0)
            si = jnp.where(rows < SSM_T - dd, pltpu.roll(xi, SSM_T - dd, 0), 0.0)
            mr, mi = _cmul(pr, pi, sr, si)
            xr, xi = xr + mr, xi + mi
            pr, pi = _cmul(pr, pi, pr, pi)
        gr_ref[...] = jnp.broadcast_to(xr[0:1, :], gr_ref.shape)
        gi_ref[...] = jnp.broadcast_to(xi[0:1, :], gi_ref.shape)
        hr_blk, hi_blk = hr_ref[...], hi_ref[...]
        keep = (t > 0).astype(F32)
        hpr = jnp.where(rows >= 1, pltpu.roll(hr_blk, 1, 0), hpr_ref[7:8, :] * keep)
        hpi = jnp.where(rows >= 1, pltpu.roll(hi_blk, 1, 0), hpi_ref[7:8, :] * keep)
        dar_ref[...] += jnp.sum(hpr * xr + hpi * xi, axis=0, keepdims=True)
        dai_ref[...] += jnp.sum(hpr * xi - hpi * xr, axis=0, keepdims=True)
        dcre_ref[...] += _dot(hr_blk, dy, "tn")
        dcim_ref[...] -= _dot(hi_blk, dy, "tn")
        du = dy * dsk_ref[...] + _dot(xr, bre_ref[...], "nt") + _dot(xi, bim_ref[...], "nt")
        du_ref[...] = du.astype(du_ref.dtype)
        dbre_ref[...] += _dot(u, xr, "tn")
        dbim_ref[...] += _dot(u, xi, "tn")
        ddsk_ref[...] += jnp.sum(dy * u, axis=0, keepdims=True)

    def rev(t):
        return nt - 1 - t

    u_spec = pl.BlockSpec((SSM_T, 128), lambda j, t: (rev(t), OFF_SSM // 128 + j))
    h_spec = pl.BlockSpec((SSM_T, W_BRANCH), lambda j, t: (rev(t), j))
    hprev_spec = pl.BlockSpec((8, W_BRANCH), lambda j, t: (jnp.maximum(rev(t) * (SSM_T // 8) - 1, 0), j))
    ch_spec = pl.BlockSpec((SSM_T, 128), lambda j, t: (rev(t), j))
    b_spec = pl.BlockSpec((None, 128, W_BRANCH), lambda j, t: (j, 0, 0))
    a_spec = pl.BlockSpec((1, W_BRANCH), lambda j, t: (0, j))
    c_spec = pl.BlockSpec((None, W_BRANCH, 128), lambda j, t: (j, 0, 0))
    d_spec = pl.BlockSpec((1, 128), lambda j, t: (0, j))
    return pl.pallas_call(
        body, name="s5_bwd", grid=(4, nt),
        in_specs=[u_spec, h_spec, h_spec, hprev_spec, hprev_spec, ch_spec, b_spec, b_spec, a_spec, a_spec,
                  c_spec, c_spec, d_spec],
        out_specs=[ch_spec, b_spec, b_spec, a_spec, a_spec, c_spec, c_spec, d_spec],
        out_shape=[jax.ShapeDtypeStruct((S, W_BRANCH), BF16),
                   jax.ShapeDtypeStruct((4, 128, W_BRANCH), F32), jax.ShapeDtypeStruct((4, 128, W_BRANCH), F32),
                   jax.ShapeDtypeStruct((1, SSM_COLS), F32), jax.ShapeDtypeStruct((1, SSM_COLS), F32),
                   jax.ShapeDtypeStruct((4, W_BRANCH, 128), F32), jax.ShapeDtypeStruct((4, W_BRANCH, 128), F32),
                   jax.ShapeDtypeStruct((1, W_BRANCH), F32)],
        scratch_shapes=[pltpu.VMEM((8, W_BRANCH), F32), pltpu.VMEM((8, W_BRANCH), F32)],
        compiler_params=_cp(2),
    )(proj, hr, hi, hr, hi, dy, b_re, b_im, a_re, a_im, c_re, c_im, d_skip)


def glu_fwd(y, w_glu, b_glu):
    S = y.shape[0]

    def body(y_ref, w_ref, b_ref, o_ref):
        g = _gelu(y_ref[...])
        o_ref[...] = (g * _sigmoid(_dot(g, w_ref[...]) + b_ref[...])).astype(o_ref.dtype)

    blk = pl.BlockSpec((ROW_TILE, W_BRANCH), lambda i: (i, 0))
    return pl.pallas_call(
        body, name="glu_fwd", grid=(S // ROW_TILE,),
        in_specs=[blk, pl.BlockSpec((W_BRANCH, W_BRANCH), lambda i: (0, 0)), pl.BlockSpec((1, W_BRANCH), lambda i: (0, 0))],
        out_specs=blk, out_shape=jax.ShapeDtypeStruct((S, W_BRANCH), BF16), compiler_params=_cp(1),
    )(y, w_glu, b_glu)


def glu_bwd(y, w_glu, b_glu, dout):
    S = y.shape[0]

    def body(y_ref, w_ref, b_ref, do_ref, dy_ref, dw_ref, db_ref):
        yv = y_ref[...]
        do = do_ref[...]
        g = _gelu(yv)
        s = _sigmoid(_dot(g, w_ref[...]) + b_ref[...])
        dz = do * g * s * (1.0 - s)
        dg = do * s + _dot(dz, w_ref[...], "nt")
        dy_ref[...] = dg * _gelu_grad(yv)
        dw = _dot(g, dz, "tn")
        db = jnp.sum(dz, axis=0, keepdims=True)

        @pl.when(pl.program_id(0) == 0)
        def _():
            dw_ref[...] = dw
            db_ref[...] = db

        @pl.when(pl.program_id(0) > 0)
        def _():
            dw_ref[...] += dw
            db_ref[...] += db

    blk = pl.BlockSpec((ROW_TILE, W_BRANCH), lambda i: (i, 0))
    mat = pl.BlockSpec((W_BRANCH, W_BRANCH), lambda i: (0, 0))
    vec = pl.BlockSpec((1, W_BRANCH), lambda i: (0, 0))
    return pl.pallas_call(
        body, name="glu_bwd", grid=(S // ROW_TILE,), in_specs=[blk, mat, vec, blk], out_specs=[blk, mat, vec],
        out_shape=[jax.ShapeDtypeStruct((S, W_BRANCH), F32), jax.ShapeDtypeStruct((W_BRANCH, W_BRANCH), F32),
                   jax.ShapeDtypeStruct((1, W_BRANCH), F32)],
        compiler_params=_cp(1),
    )(y, w_glu, b_glu, dout)


SGU_TILE = 512
SGU_U_BLOCK = OFF_SGU // W_BRANCH
SGU_V_BLOCK = SGU_U_BLOCK + 1


def _sgu_norm(zv):
    v = _gelu(zv)
    mu = jnp.mean(v, axis=-1, keepdims=True)
    vc = v - mu
    rstd = lax.rsqrt(jnp.mean(vc * vc, axis=-1, keepdims=True) + EPS)
    return vc * rstd, rstd


def _tril():
    return lax.broadcasted_iota(jnp.int32, (SGU_CHUNK, SGU_CHUNK), 0) >= lax.broadcasted_iota(jnp.int32, (SGU_CHUNK, SGU_CHUNK), 1)


def sgu_fwd(proj, ln_g, ln_b, w_s, b_s_t):
    S = proj.shape[0]

    def body(zu_ref, zv_ref, g_ref, b_ref, ws_ref, bs_ref, o_ref, vf_ref):
        vn, _ = _sgu_norm(zv_ref[...])
        vf_ref[...] = vn * g_ref[...] + b_ref[...]
        tri = _tril()
        for gi in range(4):
            ws = jnp.where(tri, ws_ref[gi], 0.0)
            cols = slice(gi * 128, (gi + 1) * 128)
            for c in range(SGU_TILE // SGU_CHUNK):
                rows = slice(c * SGU_CHUNK, (c + 1) * SGU_CHUNK)
                sv = _dot(ws, vf_ref[rows, cols]) + bs_ref[:, gi:gi + 1]
                o_ref[rows, cols] = (_gelu(zu_ref[rows, cols]) * sv).astype(o_ref.dtype)

    blk = lambda cb: pl.BlockSpec((SGU_TILE, W_BRANCH), lambda i: (i, cb))
    vec = pl.BlockSpec((1, W_BRANCH), lambda i: (0, 0))
    return pl.pallas_call(
        body, name="sgu_fwd", grid=(S // SGU_TILE,),
        in_specs=[blk(SGU_U_BLOCK), blk(SGU_V_BLOCK), vec, vec, pl.BlockSpec((4, SGU_CHUNK, SGU_CHUNK), lambda i: (0, 0, 0)),
                  pl.BlockSpec((SGU_CHUNK, 4), lambda i: (0, 0))],
        out_specs=blk(0), out_shape=jax.ShapeDtypeStruct((S, W_BRANCH), BF16),
        scratch_shapes=[pltpu.VMEM((SGU_TILE, W_BRANCH), F32)], compiler_params=_cp(1),
    )(proj, proj, ln_g, ln_b, w_s, b_s_t)


def sgu_bwd(proj, ln_g, ln_b, w_s, b_s_t, dout):
    S = proj.shape[0]

    def body(zu_ref, zv_ref, g_ref, b_ref, ws_ref, bs_ref, do_ref, dzu_ref, dzv_ref, dg_ref, db_ref, dws_ref, dbs_ref,
             vf_ref, dvf_ref):
        @pl.when(pl.program_id(0) == 0)
        def _():
            for ref in (dg_ref, db_ref, dws_ref, dbs_ref):
                ref[...] = jnp.zeros(ref.shape, F32)

        vn, rstd = _sgu_norm(zv_ref[...])
        vf_ref[...] = vn * g_ref[...] + b_ref[...]
        tri = _tril()
        lane = lax.broadcasted_iota(jnp.int32, (SGU_CHUNK, 128), 1)
        dbs = jnp.zeros((SGU_CHUNK, 128), F32)
        for gi in range(4):
            ws = jnp.where(tri, ws_ref[gi], 0.0)
            cols = slice(gi * 128, (gi + 1) * 128)
            dws = jnp.zeros((SGU_CHUNK, SGU_CHUNK), F32)
            for c in range(SGU_TILE // SGU_CHUNK):
                rows = slice(c * SGU_CHUNK, (c + 1) * SGU_CHUNK)
                vf = vf_ref[rows, cols]
                zu = zu_ref[rows, cols]
                do = do_ref[rows, cols]
                sv = _dot(ws, vf) + bs_ref[:, gi:gi + 1]
                dzu_ref[rows, cols] = (do * sv * _gelu_grad(zu)).astype(dzu_ref.dtype)
                dsv = do * _gelu(zu)
                dvf_ref[rows, cols] = _dot(ws, dsv, "tn")
                dws = dws + _dot(dsv, vf, "nt")
                dbs = dbs + jnp.where(lane == gi, jnp.sum(dsv, axis=-1, keepdims=True), 0.0)
            dws_ref[gi] += jnp.where(tri, dws, 0.0)
        dbs_ref[...] += dbs
        dvf = dvf_ref[...]
        dg_ref[...] += jnp.sum(dvf * vn, axis=0, keepdims=True)
        db_ref[...] += jnp.sum(dvf, axis=0, keepdims=True)
        dvn = dvf * g_ref[...]
        dv = rstd * (dvn - jnp.mean(dvn, axis=-1, keepdims=True) - vn * jnp.mean(dvn * vn, axis=-1, keepdims=True))
        dzv_ref[...] = (dv * _gelu_grad(zv_ref[...])).astype(dzv_ref.dtype)

    blk = lambda cb: pl.BlockSpec((SGU_TILE, W_BRANCH), lambda i: (i, cb))
    vec = pl.BlockSpec((1, W_BRANCH), lambda i: (0, 0))
    ws_spec = pl.BlockSpec((4, SGU_CHUNK, SGU_CHUNK), lambda i: (0, 0, 0))
    return pl.pallas_call(
        body, name="sgu_bwd", grid=(S // SGU_TILE,),
        in_specs=[blk(SGU_U_BLOCK), blk(SGU_V_BLOCK), vec, vec, ws_spec, pl.BlockSpec((SGU_CHUNK, 4), lambda i: (0, 0)),
                  blk(0)],
        out_specs=[blk(0), blk(0), vec, vec, ws_spec, pl.BlockSpec((SGU_CHUNK, 128), lambda i: (0, 0))],
        out_shape=[jax.ShapeDtypeStruct((S, W_BRANCH), BF16), jax.ShapeDtypeStruct((S, W_BRANCH), BF16),
                   jax.ShapeDtypeStruct((1, W_BRANCH), F32), jax.ShapeDtypeStruct((1, W_BRANCH), F32),
                   jax.ShapeDtypeStruct((4, SGU_CHUNK, SGU_CHUNK), F32), jax.ShapeDtypeStruct((SGU_CHUNK, 128), F32)],
        scratch_shapes=[pltpu.VMEM((SGU_TILE, W_BRANCH), F32), pltpu.VMEM((SGU_TILE, W_BRANCH), F32)],
        compiler_params=_cp(1),
    )(proj, proj, ln_g, ln_b, w_s, b_s_t, dout)


GM_TILE = 512


def _gate_specs(order):
    def spec(i):
        def index(*ids):
            m, n = order(*ids)
            return (m, (OFF_GATE + i * D_MODEL) // GM_TILE + n)
        return pl.BlockSpec((GM_TILE, GM_TILE), index)
    return [spec(i) for i in range(4)]


def merge_fwd(proj, gate_b, branches, w_up):
    S = proj.shape[0]
    order = lambda n, m: (m, n)

    def body(p0, p1, p2, p3, gb_ref, b0, b1, b2, b3, w_ref, o_ref):
        acc = jnp.zeros((GM_TILE, GM_TILE), F32)
        for i, (p_ref, br_ref) in enumerate(zip((p0, p1, p2, p3), (b0, b1, b2, b3))):
            acc = acc + _sigmoid(p_ref[...] + gb_ref[i:i + 1, :]) * _dot(br_ref[...], w_ref[i])
        o_ref[...] = acc.astype(o_ref.dtype)

    br_spec = pl.BlockSpec((GM_TILE, W_BRANCH), lambda n, m: (m, 0))
    return pl.pallas_call(
        body, name="merge_fwd", grid=(D_MODEL // GM_TILE, S // GM_TILE),
        in_specs=_gate_specs(order) + [pl.BlockSpec((4, GM_TILE), lambda n, m: (0, n))] + [br_spec] * 4
        + [pl.BlockSpec((4, W_BRANCH, GM_TILE), lambda n, m: (0, 0, n))],
        out_specs=pl.BlockSpec((GM_TILE, GM_TILE), lambda n, m: (m, n)),
        out_shape=jax.ShapeDtypeStruct((S, D_MODEL), BF16), compiler_params=_cp(2),
    )(proj, proj, proj, proj, gate_b, *branches, w_up)


def merge_bwd(proj, gate_b, branches, w_up, dmerged):
    S = proj.shape[0]
    order = lambda n, m: (m, n)

    def body(p0, p1, p2, p3, gb_ref, b0, b1, b2, b3, w_ref, dm_ref, dp0, dp1, dp2, dp3, du0, du1, du2, du3, dgb_ref):
        dm = dm_ref[...]
        dgb = []
        for i, (p_ref, br_ref, dp_ref, du_ref) in enumerate(
                zip((p0, p1, p2, p3), (b0, b1, b2, b3), (dp0, dp1, dp2, dp3), (du0, du1, du2, du3))):
            gate = _sigmoid(p_ref[...] + gb_ref[i:i + 1, :])
            dpre = dm * _dot(br_ref[...], w_ref[i]) * gate * (1.0 - gate)
            dp_ref[...] = dpre.astype(dp_ref.dtype)
            du_ref[...] = (dm * gate).astype(du_ref.dtype)
            dgb.append(jnp.sum(dpre, axis=0, keepdims=True))
        dgb = jnp.concatenate(dgb, axis=0)

        @pl.when(pl.program_id(1) == 0)
        def _():
            dgb_ref[...] = dgb

        @pl.when(pl.program_id(1) > 0)
        def _():
            dgb_ref[...] += dgb

    br_spec = pl.BlockSpec((GM_TILE, W_BRANCH), lambda n, m: (m, 0))
    mn = pl.BlockSpec((GM_TILE, GM_TILE), lambda n, m: (m, n))
    gb = pl.BlockSpec((4, GM_TILE), lambda n, m: (0, n))
    big = jax.ShapeDtypeStruct((S, D_MODEL), BF16)
    outs = pl.pallas_call(
        body, name="merge_bwd", grid=(D_MODEL // GM_TILE, S // GM_TILE),
        in_specs=_gate_specs(order) + [gb] + [br_spec] * 4
        + [pl.BlockSpec((4, W_BRANCH, GM_TILE), lambda n, m: (0, 0, n)), mn],
        out_specs=[mn] * 8 + [gb], out_shape=[big] * 8 + [jax.ShapeDtypeStruct((4, D_MODEL), F32)],
        compiler_params=_cp(2),
    )(proj, proj, proj, proj, gate_b, *branches, w_up, dmerged)
    return outs[0:4], outs[4:8], outs[8]


def _xatt_probs(q, k):
    s = _dot(q, k, "nt") * (X_HEAD_DIM ** -0.5)
    p = jnp.exp(s - jnp.max(s, axis=-1, keepdims=True))
    return p / jnp.sum(p, axis=-1, keepdims=True)


def xatt_fwd(q, kv):
    S = q.shape[0]

    def body(q_ref, kv_ref, o_ref):
        for h in range(X_HEADS):
            cols = slice(h * X_HEAD_DIM, (h + 1) * X_HEAD_DIM)
            p = _xatt_probs(q_ref[:, cols], kv_ref[:, cols])
            o_ref[:, cols] = _dot(p, kv_ref[:, W_BRANCH + h * X_HEAD_DIM:W_BRANCH + (h + 1) * X_HEAD_DIM]).astype(o_ref.dtype)

    blk = pl.BlockSpec((ROW_TILE, W_BRANCH), lambda i: (i, 0))
    return pl.pallas_call(
        body, name="xatt_fwd", grid=(S // ROW_TILE,),
        in_specs=[blk, pl.BlockSpec((N_MEM, 2 * W_BRANCH), lambda i: (0, 0))], out_specs=blk,
        out_shape=jax.ShapeDtypeStruct((S, W_BRANCH), BF16), compiler_params=_cp(1),
    )(q, kv)


def xatt_bwd(q, kv, do):
    S = q.shape[0]

    def body(q_ref, kv_ref, do_ref, dq_ref, dkv_ref):
        @pl.when(pl.program_id(0) == 0)
        def _():
            dkv_ref[...] = jnp.zeros(dkv_ref.shape, F32)

        for h in range(X_HEADS):
            cols = slice(h * X_HEAD_DIM, (h + 1) * X_HEAD_DIM)
            vcols = slice(W_BRANCH + h * X_HEAD_DIM, W_BRANCH + (h + 1) * X_HEAD_DIM)
            qh, kh, doh = q_ref[:, cols], kv_ref[:, cols], do_ref[:, cols]
            p = _xatt_probs(qh, kh)
            dp = _dot(doh, kv_ref[:, vcols], "nt")
            ds = p * (dp - jnp.sum(dp * p, axis=-1, keepdims=True)) * (X_HEAD_DIM ** -0.5)
            dq_ref[:, cols] = _dot(ds, kh).astype(dq_ref.dtype)
            dkv_ref[:, cols] += _dot(ds, qh, "tn")
            dkv_ref[:, vcols] += _dot(p, doh, "tn")

    blk = pl.BlockSpec((ROW_TILE, W_BRANCH), lambda i: (i, 0))
    kv_spec = pl.BlockSpec((N_MEM, 2 * W_BRANCH), lambda i: (0, 0))
    return pl.pallas_call(
        body, name="xatt_bwd", grid=(S // ROW_TILE,), in_specs=[blk, kv_spec, blk], out_specs=[blk, kv_spec],
        out_shape=[jax.ShapeDtypeStruct((S, W_BRANCH), BF16), jax.ShapeDtypeStruct((N_MEM, 2 * W_BRANCH), F32)],
        compiler_params=_cp(1),
    )(q, kv, do)


def s5_params(a_re, a_im, log_dt, b_re, b_im, c_re, c_im):
    lam_re = jnp.minimum(a_re, -1e-4)
    lam_im = a_im
    dt = jnp.exp(log_dt)[:, None]
    mag = jnp.exp(lam_re * dt)
    ab_re, ab_im = mag * jnp.cos(lam_im * dt), mag * jnp.sin(lam_im * dt)
    den = lam_re * lam_re + lam_im * lam_im
    f_re = ((ab_re - 1.0) * lam_re + ab_im * lam_im) / den
    f_im = (ab_im * lam_re - (ab_re - 1.0) * lam_im) / den
    bb_re = f_re[..., None] * b_re - f_im[..., None] * b_im
    bb_im = f_re[..., None] * b_im + f_im[..., None] * b_re
    eye = jnp.eye(8, dtype=F32)

    def b_blocks(bb):
        t = bb.reshape(4, 8, SSM_STATE, SSM_GROUP).transpose(0, 1, 3, 2)
        return (t[:, :, :, None, :] * eye[None, :, None, :, None]).reshape(4, 128, W_BRANCH)

    def c_blocks(cc):
        t = cc.reshape(4, 8, SSM_GROUP, SSM_STATE).transpose(0, 1, 3, 2)
        return (t[:, :, :, None, :] * eye[None, :, None, :, None]).reshape(4, W_BRANCH, 128)

    return (ab_re.reshape(1, SSM_COLS), ab_im.reshape(1, SSM_COLS), b_blocks(bb_re), b_blocks(bb_im),
            c_blocks(c_re), c_blocks(c_im))


ANY = pl.BlockSpec(memory_space=pl.ANY)


def _chip_index():
    return 2 * lax.axis_index("x") + lax.axis_index("y")


def _peer_chip(j):
    x, y, c = lax.axis_index("x"), lax.axis_index("y"), lax.axis_index("c")
    return ((1 - x) if j & 2 else x, (1 - y) if j & 1 else y, c)


def _piece(ref, axis, s, n):
    size = ref.shape[axis] // n
    idx = [slice(None)] * len(ref.shape)
    idx[axis] = pl.ds(s * size, size)
    return ref.at[tuple(idx)]


HBM_SPEC = pl.BlockSpec(memory_space=pltpu.HBM)
SEM_SPEC = pl.BlockSpec(memory_space=pltpu.SEMAPHORE)
SIDE_EFFECT = pltpu.SideEffectType.DATAFLOW_SIDE_EFFECTING


def _chip_copies(ins, lands, send, recv, axes, mode, k, arriving):
    copies = []
    for t in range(len(ins)):
        for j in (1, 2, 3):
            place = k ^ j if arriving else k
            if mode == "gather":
                src, dst = ins[t], _piece(lands[t], axes[t], place, 4)
            else:
                src = ins[t] if axes[t] is None else _piece(ins[t], axes[t], k ^ j, 4)
                dst = lands[t].at[place]
            copies.append(pltpu.make_async_remote_copy(
                src_ref=src, dst_ref=dst, send_sem=send.at[3 * t + j - 1], recv_sem=recv.at[3 * t + j - 1],
                device_id=_peer_chip(j), device_id_type=MESH_ID))
    return copies


def chips_start(ins, lands, axes, mode, name, after=()):
    n, na = len(ins), len(after)

    def body(*refs):
        in_refs, land_refs = refs[:n], refs[n:2 * n]
        send, recv, token = refs[2 * n + na], refs[2 * n + na + 1], refs[-1]
        q = _chip_index()
        for k in range(4):
            @pl.when(q == k)
            def _():
                for copy in _chip_copies(in_refs, land_refs, send, recv, axes, mode, k, arriving=False):
                    copy.start()
        token[...] = jnp.zeros(token.shape, token.dtype)

    hbm = lambda a: pltpu.HBM(a.shape, a.dtype)
    outs = pl.pallas_call(
        body, name=name, in_specs=[HBM_SPEC] * (2 * n) + [ANY] * na,
        out_specs=[SEM_SPEC, SEM_SPEC] + [HBM_SPEC] * (2 * n) + [pl.BlockSpec(memory_space=pltpu.VMEM)],
        out_shape=[pltpu.SemaphoreType.DMA((3 * n,)), pltpu.SemaphoreType.DMA((3 * n,))]
        + [hbm(a) for a in ins] + [hbm(a) for a in lands] + [jax.ShapeDtypeStruct((8, 128), F32)],
        input_output_aliases={i: 2 + i for i in range(2 * n)},
        compiler_params=pltpu.CompilerParams(has_side_effects=SIDE_EFFECT),
    )(*[pltpu.with_memory_space_constraint(a, pltpu.HBM) for a in list(ins) + list(lands)], *after)
    return outs[0], outs[1], outs[2:2 + n], outs[2 + n:2 + 2 * n], outs[-1]


def chips_wait(send, recv, ins, lands, axes, mode, name, after=()):
    n = len(ins)

    def body(*refs):
        in_refs, land_refs = refs[:n], refs[n:2 * n]
        send_ref, recv_ref = refs[2 * n], refs[2 * n + 1]
        q = _chip_index()
        for k in range(4):
            @pl.when(q == k)
            def _():
                for copy in _chip_copies(in_refs, land_refs, send_ref, recv_ref, axes, mode, k, arriving=True):
                    copy.wait_send()
                    copy.wait_recv()

    hbm = lambda a: pltpu.HBM(a.shape, a.dtype)
    outs = pl.pallas_call(
        body, name=name, in_specs=[HBM_SPEC] * (2 * n) + [SEM_SPEC, SEM_SPEC] + [ANY] * len(after),
        out_specs=[HBM_SPEC] * (2 * n), out_shape=[hbm(a) for a in ins] + [hbm(a) for a in lands],
        input_output_aliases={i: i for i in range(2 * n)},
        compiler_params=pltpu.CompilerParams(has_side_effects=SIDE_EFFECT),
    )(*ins, *lands, send, recv, *after)
    return outs[:n], outs[n:]


def swap_cores(arrs, name):
    n = len(arrs)

    def body(*refs):
        ins, outs = refs[:n], refs[n:2 * n]
        send, recv = refs[2 * n:]
        sibling = (lax.axis_index("x"), lax.axis_index("y"), 1 - lax.axis_index("c"))
        copies = [pltpu.make_async_remote_copy(src_ref=ins[t], dst_ref=outs[t], send_sem=send.at[t], recv_sem=recv.at[t],
                                               device_id=sibling, device_id_type=MESH_ID) for t in range(n)]
        for cp in copies:
            cp.start()
        for cp in copies:
            cp.wait()

    return pl.pallas_call(
        body, name=name, in_specs=[ANY] * n, out_specs=[ANY] * n,
        out_shape=[jax.ShapeDtypeStruct(a.shape, a.dtype) for a in arrs],
        scratch_shapes=[pltpu.SemaphoreType.DMA((n,)), pltpu.SemaphoreType.DMA((n,))],
    )(*arrs)


ELEMENTWISE_BLOCK_BYTES = 1 << 20


def _row_tile(rows, cols):
    want = max(8, ELEMENTWISE_BLOCK_BYTES // (4 * cols))
    fits = [t for t in range(8, min(rows, want) + 1, 8) if rows % t == 0]
    return fits[-1] if fits else rows


def sum_chips(recv, own, axis, chip, stacked, l, name):
    _, r, c = recv.shape
    tr = _row_tile(r, c)
    nrt = r // tr

    def body(chip_ref, r_ref, own_ref, stacked_ref, o_ref):
        for k in range(4):
            @pl.when(chip_ref[0] == k)
            def _():
                terms = [own_ref[...] if s == k else r_ref[s] for s in range(4)]
                o_ref[...] = ((terms[0] + terms[1]) + terms[2]) + terms[3]

    own_index = {0: lambda i, q: (q[0] * nrt + i, 0), 1: lambda i, q: (i, q[0]), None: lambda i, q: (i, 0)}[axis]
    return pl.pallas_call(
        body, name=name,
        grid_spec=pltpu.PrefetchScalarGridSpec(
            num_scalar_prefetch=1, grid=(nrt,),
            in_specs=[pl.BlockSpec((4, tr, c), lambda i, q: (0, i, 0)), pl.BlockSpec((tr, c), own_index), ANY],
            out_specs=pl.BlockSpec((None, tr, c), lambda i, q: (l, i, 0))),
        out_shape=jax.ShapeDtypeStruct(stacked.shape, F32), input_output_aliases={3: 0}, compiler_params=_cp(1),
    )(chip, recv, own, stacked)


def adamw(w, ga, gb, m, v, name):
    rows, cols = w.shape
    tr = _row_tile(rows, cols)

    def body(w_ref, ga_ref, gb_ref, m_ref, v_ref, g_ref, d_ref, nm_ref, nv_ref):
        g = ga_ref[...] + gb_ref[...]
        nm = ADAM_B1 * m_ref[...] + (1.0 - ADAM_B1) * g
        nv = ADAM_B2 * v_ref[...] + (1.0 - ADAM_B2) * (g * g)
        m_hat = nm / (1.0 - ADAM_B1 ** ADAM_STEP)
        v_hat = nv / (1.0 - ADAM_B2 ** ADAM_STEP)
        g_ref[...] = g
        nm_ref[...] = nm
        nv_ref[...] = nv
        d_ref[...] = -ADAM_LR * (m_hat / (jnp.sqrt(v_hat) + ADAM_EPS) + ADAM_WD * w_ref[...])

    blk = pl.BlockSpec((tr, cols), lambda i: (i, 0))
    f = jax.ShapeDtypeStruct((rows, cols), F32)
    return pl.pallas_call(
        body, name=name, grid=(rows // tr,), in_specs=[blk] * 5, out_specs=[blk] * 4, out_shape=[f] * 4,
        compiler_params=_cp(1),
    )(w, ga, gb, m, v)


PACK_ALIGN = 1024
PACK_ROWS_ALIGN = 2048


def pack_small(arrs):
    parts = []
    for a in arrs:
        flat = a.reshape(-1)
        pad = (-flat.shape[0]) % PACK_ALIGN
        parts.append(jnp.pad(flat, (0, pad)) if pad else flat)
    rows = sum(p.shape[0] for p in parts) // 128
    parts.append(jnp.zeros(((-rows) % PACK_ROWS_ALIGN * 128,), arrs[0].dtype))
    return jnp.concatenate(parts).reshape(-1, 128)


def unpack_small(packed, shapes):
    out, row = [], 0
    for shape in shapes:
        size = int(np.prod(shape))
        rows = -(-size // PACK_ALIGN) * 8
        out.append(packed[row:row + rows].reshape(-1)[:size].reshape(shape))
        row += rows
    return out


def layer_fwd(x, mem, w_in, rest_of, P, biases, after=()):
    sv = {"x0": x}
    h1 = rms_fwd(x, P["g_mix_pre"], BF16, "rms_pre")
    proj = mm(h1, w_in, "nn", tm=1024, tn=768, tk=1024, out_dtypes=[F32], name="mm_w_in", after=after)
    a_out = pool_fwd(proj, P["pool_w"], P["pool_scale"])
    os_, lses = [], []
    for g, (win, dil) in enumerate(DIL_GROUPS):
        o, lse = att_fwd(proj, biases[g], g, dil)
        os_.append(o)
        lses.append(lse)
    b_out, w0, w1, w2 = att_combine(os_, lses)
    s5p = P["s5"]
    hr, hi, y = s5_fwd(proj, s5p[2], s5p[3], s5p[0], s5p[1], s5p[4], s5p[5], P["d_skip"])
    d_out = sgu_fwd(proj, P["sgu_ln_g"], P["sgu_ln_b"], P["w_s"], P["b_s_t"])
    W, after_rest = rest_of(d_out)
    W = dict(W, w_in=w_in)
    c_out = glu_fwd(y, W["w_glu"], P["b_glu"])
    branches = (a_out, b_out, c_out, d_out)
    merged = merge_fwd(proj, W["gate_b"], branches, W["w_up"])
    t1 = mm(merged, W["w_out"], "nn", tm=1024, tn=1024, tk=1024, out_dtypes=[F32], name="mm_w_out", after=after_rest)
    x1 = rms_fwd(t1, P["g_mix_post"], F32, "rms_post", res=x)
    sv.update(h1=h1, proj=proj, os=os_, lses=lses, wts=(w0, w1, w2), hr=hr, hi=hi, y=y, branches=branches,
              merged=merged, t1=t1, x1=x1)

    h2 = rms_fwd(x1, P["g_x_pre"], BF16, "rms_pre")
    mem_n = rms_fwd(mem, P["g_mem"], BF16, "rms_mem")
    q = mm(h2, W["w_cq"], "nn", tm=1024, tn=512, tk=1024, out_dtypes=[BF16], name="mm_w_cq")
    kv = mm(mem_n, W["w_ckv"], "nn", tm=256, tn=1024, tk=1024, out_dtypes=[BF16], name="mm_w_ckv")
    ox = xatt_fwd(q, kv)
    t2 = mm(ox, W["w_co"], "nn", tm=1024, tn=1024, tk=512, out_dtypes=[F32], name="mm_w_co")
    x2 = rms_fwd(t2, P["g_x_post"], F32, "rms_post", res=x1)
    sv.update(h2=h2, mem_n=mem_n, q=q, kv=kv, ox=ox, t2=t2, x2=x2)

    h3 = rms_fwd(x2, P["g_ff_pre"], BF16, "rms_pre")
    pre, act = mm(h3, W["w_ff1"], "nn", tm=1024, tn=1024, tk=1024, out_dtypes=[F32, BF16], name="mm_w_ff1",
                  epi=lambda acc: (acc, jnp.square(jnp.maximum(acc, 0.0))))
    ff = mm(act, W["w_ff2"], "nn", tm=1024, tn=1024, tk=1024, out_dtypes=[F32], name="mm_w_ff2")
    x3 = rms_fwd(ff, P["g_ff_post"], F32, "rms_post", res=x2)
    sv.update(h3=h3, pre=pre, act=act, ff=ff, W=W)
    return x3, sv


def layer_bwd(dx, mem, W, P, biases, sv, headsum, emit, after=()):
    G = {}
    dff, G["g_ff_post"] = rms_bwd(sv["ff"], P["g_ff_post"], dx, BF16, "rms_post_bwd", after=after)
    G["w_ff2"] = mm(sv["act"], dff, "tn", tm=1024, tn=1024, tk=1024, out_dtypes=[F32], name="mm_dw_ff2")
    dpre = mm(dff, W["w_ff2"], "nt", tm=1024, tn=1024, tk=1024, out_dtypes=[BF16], name="mm_dact", extras=(sv["pre"],),
              epi=lambda acc, pre: (acc * (2.0 * jnp.maximum(pre, 0.0)),))
    G["w_ff1"] = mm(sv["h3"], dpre, "tn", tm=1024, tn=1024, tk=1024, out_dtypes=[F32], name="mm_dw_ff1")
    sent = emit(("w_ff1", "w_ff2"), G)
    dh3 = mm(dpre, W["w_ff1"], "nt", tm=1024, tn=1024, tk=1024, out_dtypes=[F32], name="mm_dh3", after=sent)
    dx2, G["g_ff_pre"] = rms_bwd(sv["x2"], P["g_ff_pre"], dh3, F32, "rms_pre_bwd", add=dx)
    dt2, G["g_x_post"] = rms_bwd(sv["t2"], P["g_x_post"], dx2, BF16, "rms_post_bwd")
    G["w_co"] = mm(sv["ox"], dt2, "tn", tm=512, tn=1024, tk=1024, out_dtypes=[F32], name="mm_dw_co")
    dox = mm(dt2, W["w_co"], "nt", tm=1024, tn=512, tk=1024, out_dtypes=[BF16], name="mm_dox")
    dq, dkv = xatt_bwd(sv["q"], sv["kv"], dox)
    G["w_cq"] = mm(sv["h2"], dq, "tn", tm=1024, tn=512, tk=1024, out_dtypes=[F32], name="mm_dw_cq")
    dh2 = mm(dq, W["w_cq"], "nt", tm=1024, tn=1024, tk=512, out_dtypes=[F32], name="mm_dh2")
    G["w_ckv"] = mm(sv["mem_n"], dkv, "tn", tm=1024, tn=1024, tk=256, out_dtypes=[F32], name="mm_dw_ckv")
    dmem_n = mm(dkv, W["w_ckv"], "nt", tm=256, tn=1024, tk=1024, out_dtypes=[F32], name="mm_dmem")
    _, G["g_mem"] = rms_bwd(mem, P["g_mem"], dmem_n, BF16, "rms_mem_bwd")
    dx1, G["g_x_pre"] = rms_bwd(sv["x1"], P["g_x_pre"], dh2, F32, "rms_pre_bwd", add=dx2)
    proj = sv["proj"]
    dt1, G["g_mix_post"] = rms_bwd(sv["t1"], P["g_mix_post"], dx1, BF16, "rms_post_bwd")
    G["w_out"] = mm(sv["merged"], dt1, "tn", tm=1024, tn=1024, tk=1024, out_dtypes=[F32], name="mm_dw_out")
    dmerged = mm(dt1, W["w_out"], "nt", tm=1024, tn=1024, tk=1024, out_dtypes=[F32], name="mm_dmerged")
    dgates, dups, G["gate_b"] = merge_bwd(proj, W["gate_b"], sv["branches"], W["w_up"], dmerged)
    dbr, dwup = [], []
    for i in range(4):
        dbr.append(mm(dups[i], W["w_up"][i], "nt", tm=1024, tn=512, tk=1024, out_dtypes=[F32], name="mm_dbranch"))
        dwup.append(mm(sv["branches"][i], dups[i], "tn", tm=512, tn=1024, tk=1024, out_dtypes=[F32], name="mm_dw_up"))
    G["w_up"] = jnp.concatenate(dwup, axis=0)
    d_pool, G["pool_w"], G["pool_scale"] = pool_bwd(proj, P["pool_w"], P["pool_scale"], dbr[0])
    cbar = att_combine_bwd(dbr[1], sv["os"], sv["wts"], headsum)
    dqs, dks, dvs, dbias = [], [], [], []
    for g, (win, dil) in enumerate(DIL_GROUPS):
        dq_g, dk_g, dv_g, db_g = att_bwd(proj, biases[g], sv["lses"][g], sv["wts"][g], dbr[1], cbar, g, dil)
        dqs.append(dq_g)
        dks.append(dk_g)
        dvs.append(dv_g)
        dbias.append(db_g)
    G["att_bias"] = dbias
    s5p = P["s5"]
    dy, G["w_glu"], G["b_glu"] = glu_bwd(sv["y"], W["w_glu"], P["b_glu"], dbr[2])
    d_ssm, dbre, dbim, dar, dai, dcre, dcim, G["d_skip"] = s5_bwd(
        proj, sv["hr"], sv["hi"], dy, s5p[2], s5p[3], s5p[0], s5p[1], s5p[4], s5p[5], P["d_skip"])
    G["s5"] = (dar, dai, dbre, dbim, dcre, dcim)
    dzu, dzv, G["sgu_ln_g"], G["sgu_ln_b"], G["w_s"], G["b_s_t"] = sgu_bwd(
        proj, P["sgu_ln_g"], P["sgu_ln_b"], P["w_s"], P["b_s_t"], dbr[3])
    d_qkv = [d.astype(BF16) for d in dqs + dks + dvs]
    dproj = jnp.concatenate([d_pool] + d_qkv + [d_ssm, dzu, dzv] + list(dgates), axis=1)
    sent = emit(("gate_b", "w_glu", "w_up", "w_out", "w_cq", "w_ckv", "w_co"), G)
    G["w_in"] = mm(sv["h1"], dproj, "tn", tm=1024, tn=1536, tk=1024, out_dtypes=[F32], name="mm_dw_in", after=sent)
    sent = emit(("w_in",), G)
    dh1 = mm(dproj, W["w_in"], "nt", tm=1024, tn=1024, tk=1536, out_dtypes=[F32], name="mm_dh1", after=sent)
    dx0, G["g_mix_pre"] = rms_bwd(sv["x0"], P["g_mix_pre"], dh1, F32, "rms_pre_bwd", add=dx1)
    return dx0, G


def _as3d(name, a):
    shape2d, axis = SHARDED[name]
    rows, cols = shape2d
    if axis == 0:
        rows //= 4
    else:
        cols //= 4
    return a.reshape(DEPTH, rows, cols)


def kernel(x, mem, rel_bias, g_mix_pre, g_mix_post, w_in, gate_b, pool_w, pool_scale, a_re, a_im, log_dt, b_re, b_im, c_re, c_im, d_skip, w_glu, b_glu, sgu_ln_g, sgu_ln_b, w_s, b_s, w_up, w_out, g_x_pre, g_x_post, g_mem, w_cq, w_ckv, w_co, g_ff_pre, g_ff_post, w_ff1, w_ff2, loss_target, m_rel_bias, m_g_mix_pre, m_g_mix_post, m_w_in, m_gate_b, m_pool_w, m_pool_scale, m_a_re, m_a_im, m_log_dt, m_b_re, m_b_im, m_c_re, m_c_im, m_d_skip, m_w_glu, m_b_glu, m_sgu_ln_g, m_sgu_ln_b, m_w_s, m_b_s, m_w_up, m_w_out, m_g_x_pre, m_g_x_post, m_g_mem, m_w_cq, m_w_ckv, m_w_co, m_g_ff_pre, m_g_ff_post, m_w_ff1, m_w_ff2, v_rel_bias, v_g_mix_pre, v_g_mix_post, v_w_in, v_gate_b, v_pool_w, v_pool_scale, v_a_re, v_a_im, v_log_dt, v_b_re, v_b_im, v_c_re, v_c_im, v_d_skip, v_w_glu, v_b_glu, v_sgu_ln_g, v_sgu_ln_b, v_w_s, v_b_s, v_w_up, v_w_out, v_g_x_pre, v_g_x_post, v_g_mem, v_w_cq, v_w_ckv, v_w_co, v_g_ff_pre, v_g_ff_post, v_w_ff1, v_w_ff2):
    env = dict(locals())
    weights = {n: env[n] for n in WEIGHT_NAMES}
    mom_m = {n: env["m_" + n] for n in WEIGHT_NAMES}
    mom_v = {n: env["v_" + n] for n in WEIGHT_NAMES}
    x2d = x.reshape(x.shape[1], D_MODEL)
    mem2d = mem.reshape(N_MEM, D_MODEL)
    target = loss_target.reshape(x2d.shape)

    axis_of = {n: SHARDED[n][1] for n in SHARDED_NAMES}
    chip = _chip_index().astype(jnp.int32).reshape(1)
    rest_names = [n for n in SHARDED_NAMES if n != "w_in"]

    def gather_start(l, names, tag, after=()):
        shards = [_as3d(n, weights[n])[l].astype(F32 if n == "gate_b" else MXU_DTYPE) for n in names]
        ax = [axis_of[n] for n in names]
        lands = [jnp.concatenate([s] * 4, axis=a) for s, a in zip(shards, ax)]
        return (names, ax, tag) + chips_start(shards, lands, ax, "gather", f"gather_start_{tag}", after=after)

    def gather_wait(started, after):
        names, ax, tag, send, recv, shards, lands, _ = started
        _, lands = chips_wait(send, recv, shards, lands, ax, "gather", f"gather_wait_{tag}", after=after)
        W = dict(zip(names, lands))
        if "w_up" in W:
            W["w_up"] = W["w_up"].reshape(4, W_BRANCH, D_MODEL)
        return W

    biases = [att_bias(rel_bias, g, dil) for g, (_, dil) in enumerate(DIL_GROUPS)]
    lanes = np.arange(W_BRANCH) // ATT_HEAD_DIM
    headsum = jnp.asarray(lanes[:, None] == lanes[None, :], dtype=BF16)
    small = [pack_small([d[n] for n in REPLICATED_NAMES]) for d in (weights, mom_m, mom_v)]

    def small_params(l, s5_prepared):
        vec = lambda a: a[l].reshape(1, -1)
        return {
            "g_mix_pre": vec(g_mix_pre), "g_mix_post": vec(g_mix_post), "g_x_pre": vec(g_x_pre), "g_x_post": vec(g_x_post),
            "g_mem": vec(g_mem), "g_ff_pre": vec(g_ff_pre), "g_ff_post": vec(g_ff_post), "pool_w": pool_w[l],
            "pool_scale": vec(pool_scale), "d_skip": vec(d_skip), "b_glu": vec(b_glu), "sgu_ln_g": vec(sgu_ln_g),
            "sgu_ln_b": vec(sgu_ln_b), "w_s": w_s[l], "b_s_t": b_s[l].T, "s5": s5_prepared,
        }

    Ws, Ps, saved, s5_vjps = [], [], [], []
    xl = x2d
    flying = {"next": gather_start(0, ["w_in"], "0_w_in")}
    for l in range(DEPTH):
        s5_prepared, s5_vjp = jax.vjp(s5_params, a_re[l], a_im[l], log_dt[l], b_re[l], b_im[l], c_re[l], c_im[l])
        token_of = lambda started: (started[7],)
        if l == 0:
            w_in_l = gather_wait(flying["next"], [*biases, *small])["w_in"]
            rest = gather_start(0, rest_names, "0_rest", after=[w_in_l])
            first_after = token_of(rest)

            def rest_of(value):
                W = gather_wait(rest, [value])
                flying["next"] = gather_start(1, SHARDED_NAMES, "1", after=[W["w_out"]])
                return W, token_of(flying["next"])
        else:
            W_l = gather_wait(flying["next"], [xl])
            w_in_l, first_after = W_l["w_in"], ()
            if l + 1 < DEPTH:
                flying["next"] = gather_start(l + 1, SHARDED_NAMES, str(l + 1), after=[w_in_l])
                first_after = token_of(flying["next"])
            rest_of = lambda value, W_l=W_l: (W_l, ())
        P = small_params(l, s5_prepared)
        xl, sv = layer_fwd(xl, mem2d, w_in_l, rest_of, P, biases, after=first_after)
        Ws.append(sv["W"])
        Ps.append(P)
        saved.append(sv)
        s5_vjps.append(s5_vjp)
    loss_local, dx = loss_and_grad(xl, target)
    loss = lax.psum(loss_local, ("x", "y", "c"))

    scattered = []

    def scatter_start(l, names, srcs):
        ax = [axis_of.get(n) for n in names]
        lands = []
        for s, a in zip(srcs, ax):
            r, c = s.shape
            lands.append(lax.empty((4, r // 4 if a == 0 else r, c // 4 if a == 1 else c), F32))
        tag = f"{l}_{names[0]}"
        send, recv, srcs, lands, token = chips_start(srcs, lands, ax, "scatter", f"grads_start_{tag}")
        scattered.append((l, names, ax, tag, send, recv, srcs, lands))
        return (token,)

    grads = [None] * DEPTH
    for l in reversed(range(DEPTH)):
        emit = lambda names, G, l=l: scatter_start(l, list(names), [G[n] for n in names])
        dx, grads[l] = layer_bwd(dx, mem2d, Ws[l], Ps[l], biases, saved[l], headsum, emit)
    grad_x = dx.reshape(x.shape)

    rep = {}
    stack = lambda key, shape: jnp.stack([grads[l][key] for l in range(DEPTH)]).reshape(shape)
    for n in ("g_mix_pre", "g_mix_post", "g_x_pre", "g_x_post", "g_mem", "g_ff_pre", "g_ff_post"):
        rep[n] = stack(n, (DEPTH, D_MODEL))
    for n in ("pool_scale", "d_skip", "b_glu", "sgu_ln_g", "sgu_ln_b"):
        rep[n] = stack(n, (DEPTH, W_BRANCH))
    rep["pool_w"] = stack("pool_w", pool_w.shape)
    rep["w_s"] = stack("w_s", w_s.shape)
    rep["b_s"] = jnp.stack([grads[l]["b_s_t"][:, :4].T for l in range(DEPTH)])
    s5_grads = [s5_vjps[l](tuple(grads[l]["s5"])) for l in range(DEPTH)]
    for i, n in enumerate(("a_re", "a_im", "log_dt", "b_re", "b_im", "c_re", "c_im")):
        rep[n] = jnp.stack([s5_grads[l][i] for l in range(DEPTH)])
    dbias = [sum(grads[l]["att_bias"][g] for l in range(DEPTH)) for g in range(len(DIL_GROUPS))]
    rep["rel_bias"] = jnp.concatenate([att_bias_grad(dbias[g], dil) for g, (_, dil) in enumerate(DIL_GROUPS)], axis=1)
    rep_shapes = [weights[n].shape for n in REPLICATED_NAMES]
    packed_g = pack_small([rep[n] for n in REPLICATED_NAMES])

    small_sent = scatter_start(0, ["small"], [packed_g])
    stacked = {}

    def collect(record, after):
        l, names, ax, tag, send, recv, srcs, lands = record
        srcs, lands = chips_wait(send, recv, srcs, lands, ax, "scatter", f"grads_wait_{tag}", after=after)
        for n, own, arrived, a in zip(names, srcs, lands, ax):
            if n not in stacked:
                stacked[n] = lax.empty((1 if n == "small" else DEPTH,) + arrived.shape[1:], F32)
            stacked[n] = sum_chips(arrived, own, a, chip, stacked[n], 0 if n == "small" else l, "sum_chips")

    out_g, out_d, out_m, out_v = {}, {}, {}, {}

    def update(names, tag):
        partial = [stacked[n].reshape(-1, stacked[n].shape[-1]) for n in names]
        other = swap_cores(partial, f"swap_cores_{tag}")
        for n, mine, theirs in zip(names, partial, other):
            if n == "small":
                res = adamw(small[0], mine, theirs, small[1], small[2], "adamw")
                for d, r in zip((out_g, out_d, out_m, out_v), res):
                    d.update(zip(REPLICATED_NAMES, unpack_small(r, rep_shapes)))
            else:
                flat = lambda a: a.reshape(mine.shape)
                res = adamw(flat(weights[n]), mine, theirs, flat(mom_m[n]), flat(mom_v[n]), "adamw")
                out_g[n], out_d[n], out_m[n], out_v[n] = [r.reshape(weights[n].shape) for r in res]

    late = [r for r in scattered if r[1] == ["small"] or (r[0] == 0 and r[1] == ["w_in"])]
    for record in scattered:
        if not any(record is r for r in late):
            collect(record, [dx, *small_sent])
    update(rest_names, "rest")
    collect(late[0], [out_d[n] for n in rest_names])
    update(["w_in"], "w_in")
    collect(late[1], [out_d["w_in"]])
    update(["small"], "small")

    return (loss, grad_x, *[out_g[n] for n in WEIGHT_NAMES], *[out_d[n] for n in WEIGHT_NAMES],
            *[out_m[n] for n in WEIGHT_NAMES], *[out_v[n] for n in WEIGHT_NAMES])
```

```python
import functools
import math

import numpy as np
import jax
import jax.numpy as jnp
from jax import lax
from jax.experimental import pallas as pl
from jax.experimental.pallas import tpu as pltpu

F32 = jnp.float32
BF16 = jnp.bfloat16
MXU_DTYPE = jnp.bfloat16
MESH_ID = pl.DeviceIdType.MESH
VMEM_LIMIT_BYTES = 56 * 1024 * 1024

D_MODEL = 1024
DEPTH = 4
N_MEM = 256
W_BRANCH = 512
POOL_WINDOWS = (2, 4, 8, 16)
POOL_HALO = 16
DIL_GROUPS = ((128, 1), (512, 4), (2048, 16))
BAND = 128
ATT_HEADS = 8
ATT_HEAD_DIM = 64
SSM_GROUP = 16
SSM_GROUPS = 32
SSM_STATE = 64
SSM_COLS = SSM_GROUPS * SSM_STATE
SSM_T = 512
SGU_CHUNK = 128
X_HEADS = 4
X_HEAD_DIM = 128
D_FF = 4096
REL_BUCKETS = 32
REL_MAX_DIST = 2048
EPS = 1e-6
NEG_INF = -1e30
OFF_POOL = 0
OFF_ATT = 512
OFF_SSM = OFF_ATT + 9 * W_BRANCH
OFF_SGU = OFF_SSM + W_BRANCH
OFF_GATE = OFF_SGU + 2 * W_BRANCH
IN_WIDTH = OFF_GATE + 4 * D_MODEL

ADAM_LR = 0.001
ADAM_B1 = 0.9
ADAM_B2 = 0.999
ADAM_EPS = 1e-08
ADAM_WD = 0.01
ADAM_STEP = 10

GELU_C = math.sqrt(2.0 / math.pi)

WEIGHT_NAMES = ['rel_bias', 'g_mix_pre', 'g_mix_post', 'w_in', 'gate_b', 'pool_w', 'pool_scale', 'a_re', 'a_im',
                'log_dt', 'b_re', 'b_im', 'c_re', 'c_im', 'd_skip', 'w_glu', 'b_glu', 'sgu_ln_g', 'sgu_ln_b',
                'w_s', 'b_s', 'w_up', 'w_out', 'g_x_pre', 'g_x_post', 'g_mem', 'w_cq', 'w_ckv', 'w_co',
                'g_ff_pre', 'g_ff_post', 'w_ff1', 'w_ff2']
SHARDED = {
    'w_in': ((D_MODEL, IN_WIDTH), 1),
    'gate_b': ((4, D_MODEL), 1),
    'w_glu': ((W_BRANCH, W_BRANCH), 0),
    'w_up': ((4 * W_BRANCH, D_MODEL), 1),
    'w_out': ((D_MODEL, D_MODEL), 0),
    'w_cq': ((D_MODEL, W_BRANCH), 0),
    'w_ckv': ((D_MODEL, D_MODEL), 0),
    'w_co': ((W_BRANCH, D_MODEL), 1),
    'w_ff1': ((D_MODEL, D_FF), 1),
    'w_ff2': ((D_FF, D_MODEL), 0),
}
SHARDED_NAMES = list(SHARDED)
REPLICATED_NAMES = [n for n in WEIGHT_NAMES if n not in SHARDED]


def _cp(n_axes):
    return pltpu.CompilerParams(dimension_semantics=("arbitrary",) * n_axes, vmem_limit_bytes=VMEM_LIMIT_BYTES)


def _dot(a, b, dims="nn"):
    cd = {"nn": ((1,), (0,)), "nt": ((1,), (1,)), "tn": ((0,), (0,))}[dims]
    return lax.dot_general(a.astype(MXU_DTYPE), b.astype(MXU_DTYPE), (cd, ((), ())), preferred_element_type=F32)


def _gelu(x):
    return 0.5 * x * (1.0 + jnp.tanh(GELU_C * (x + 0.044715 * (x * x * x))))


def _gelu_grad(x):
    t = jnp.tanh(GELU_C * (x + 0.044715 * (x * x * x)))
    return 0.5 * (1.0 + t) + 0.5 * x * (1.0 - t * t) * (GELU_C * (1.0 + 3.0 * 0.044715 * (x * x)))


def _sigmoid(x):
    return 1.0 / (1.0 + jnp.exp(-x))


def mm(a, b, dims, *, tm, tn, tk, out_dtypes, name, extras=(), vecs=(), epi=None, after=()):
    if dims == "tn":
        K, M = a.shape
        N = b.shape[1]
    else:
        M, K = a.shape
        N = b.shape[1] if dims == "nn" else b.shape[0]
    tm, tn, tk = min(tm, M), min(tn, N), min(tk, K)
    assert M % tm == 0 and N % tn == 0 and K % tk == 0, (name, M, N, K, tm, tn, tk)
    nk = K // tk
    ne, no = len(extras) + len(vecs), len(out_dtypes)
    if epi is None:
        epi = lambda acc: (acc,)
    a_spec = (pl.BlockSpec((tk, tm), lambda i, j, k: (k, i)) if dims == "tn"
              else pl.BlockSpec((tm, tk), lambda i, j, k: (i, k)))
    b_spec = (pl.BlockSpec((tn, tk), lambda i, j, k: (j, k)) if dims == "nt"
              else pl.BlockSpec((tk, tn), lambda i, j, k: (k, j)))
    mn_spec = pl.BlockSpec((tm, tn), lambda i, j, k: (i, j))
    vec_spec = pl.BlockSpec((1, tn), lambda i, j, k: (0, j))

    def body(a_ref, b_ref, *rest):
        extra_refs, out_refs = rest[:ne], rest[ne + len(after):ne + len(after) + no]
        part = _dot(a_ref[...], b_ref[...], dims)

        def finish(acc):
            for o_ref, r in zip(out_refs, epi(acc, *[e[...] for e in extra_refs])):
                o_ref[...] = r.astype(o_ref.dtype)

        if nk == 1:
            finish(part)
        else:
            acc_ref = rest[-1]
            k = pl.program_id(2)

            @pl.when(k == 0)
            def _():
                acc_ref[...] = part

            @pl.when(k > 0)
            def _():
                acc_ref[...] += part

            @pl.when(k == nk - 1)
            def _():
                finish(acc_ref[...])

    outs = pl.pallas_call(
        body, name=name, grid=(M // tm, N // tn, nk),
        in_specs=[a_spec, b_spec] + [mn_spec] * len(extras) + [vec_spec] * len(vecs) + [ANY] * len(after),
        out_specs=[mn_spec] * no,
        out_shape=[jax.ShapeDtypeStruct((M, N), dt) for dt in out_dtypes],
        scratch_shapes=[pltpu.VMEM((tm, tn), F32)] if nk > 1 else [],
        compiler_params=_cp(3),
    )(a, b, *extras, *vecs, *after)
    return outs[0] if no == 1 else outs


ROW_TILE = 512


def rms_fwd(x, g, out_dtype, name, res=None):
    M, D = x.shape
    tm = min(ROW_TILE, M)

    def body(x_ref, g_ref, *rest):
        o_ref = rest[-1]
        xf = x_ref[...]
        y = xf * lax.rsqrt(jnp.mean(xf * xf, axis=-1, keepdims=True) + EPS) * g_ref[...]
        if res is not None:
            y = y + rest[0][...]
        o_ref[...] = y.astype(o_ref.dtype)

    row = pl.BlockSpec((tm, D), lambda i: (i, 0))
    return pl.pallas_call(
        body, name=name, grid=(M // tm,),
        in_specs=[row, pl.BlockSpec((1, D), lambda i: (0, 0))] + ([row] if res is not None else []),
        out_specs=row, out_shape=jax.ShapeDtypeStruct((M, D), out_dtype), compiler_params=_cp(1),
    )(x, g, *([res] if res is not None else []))


def rms_bwd(x, g, dy, dx_dtype, name, add=None, after=()):
    M, D = x.shape
    tm = min(ROW_TILE, M)

    def body(x_ref, g_ref, dy_ref, *rest):
        dx_ref, dg_ref = rest[-2], rest[-1]
        xf = x_ref[...]
        dyf = dy_ref[...].astype(F32)
        r = lax.rsqrt(jnp.mean(xf * xf, axis=-1, keepdims=True) + EPS)
        xn = xf * r
        dxn = dyf * g_ref[...]
        dx = r * (dxn - xn * jnp.mean(dxn * xn, axis=-1, keepdims=True))
        if add is not None:
            dx = dx + rest[0][...]
        dx_ref[...] = dx.astype(dx_ref.dtype)
        dg = jnp.sum(dyf * xn, axis=0, keepdims=True)

        @pl.when(pl.program_id(0) == 0)
        def _():
            dg_ref[...] = dg

        @pl.when(pl.program_id(0) > 0)
        def _():
            dg_ref[...] += dg

    row = pl.BlockSpec((tm, D), lambda i: (i, 0))
    vec = pl.BlockSpec((1, D), lambda i: (0, 0))
    return pl.pallas_call(
        body, name=name, grid=(M // tm,),
        in_specs=[row, vec, row] + ([row] if add is not None else []) + [ANY] * len(after),
        out_specs=[row, vec],
        out_shape=[jax.ShapeDtypeStruct((M, D), dx_dtype), jax.ShapeDtypeStruct((1, D), F32)],
        compiler_params=_cp(1),
    )(x, g, dy, *([add] if add is not None else []), *after)


def loss_and_grad(y, target):
    M, D = y.shape
    tm = ROW_TILE

    def body(y_ref, t_ref, part_ref, dy_ref):
        e = y_ref[...] - t_ref[...]
        dy_ref[...] = e / D
        part_ref[...] = jnp.broadcast_to(0.5 * jnp.sum(jnp.mean(e * e, axis=-1, keepdims=True), axis=0, keepdims=True),
                                         (8, 128))

    row = pl.BlockSpec((tm, D), lambda i: (i, 0))
    part, dy = pl.pallas_call(
        body, name="loss", grid=(M // tm,), in_specs=[row, row],
        out_specs=[pl.BlockSpec((8, 128), lambda i: (i, 0)), row],
        out_shape=[jax.ShapeDtypeStruct((8 * (M // tm), 128), F32), jax.ShapeDtypeStruct((M, D), F32)],
        compiler_params=_cp(1),
    )(y, target)
    return jnp.sum(part[::8, 0]), dy


POOL_ROWS = 512


def _pool_window_sum(xw, gi, roll_of):
    s1 = xw + pltpu.roll(xw, roll_of(1), 0)
    s2 = s1 + pltpu.roll(s1, roll_of(2), 0)
    s3 = s2 + pltpu.roll(s2, roll_of(4), 0)
    s4 = s3 + pltpu.roll(s3, roll_of(8), 0)
    return jnp.where(gi == 0, s1, jnp.where(gi == 1, s2, jnp.where(gi == 2, s3, s4)))


def _pool_cnt(i, gi):
    rows = lax.broadcasted_iota(jnp.int32, (POOL_ROWS, 128), 0) + i * POOL_ROWS
    w = jnp.where(gi == 0, 2, jnp.where(gi == 1, 4, jnp.where(gi == 2, 8, 16)))
    return jnp.minimum(rows + 1, w).astype(F32)


def pool_fwd(proj, pool_w, scale):
    S = proj.shape[0]
    nchunk = S // POOL_ROWS
    slab = POOL_ROWS + POOL_HALO

    def body(x_ref, w_ref, sc_ref, o_ref, pad_ref):
        gi = pl.program_id(0)
        pad_ref[0:POOL_HALO, :] = jnp.zeros((POOL_HALO, 128), F32)
        pad_ref[POOL_HALO:, :] = x_ref[...]
        for i in range(nchunk):
            xw = pad_ref[i * POOL_ROWS:i * POOL_ROWS + slab, :]
            ssum = _pool_window_sum(xw, gi, lambda d: d)[POOL_HALO:, :]
            p = ssum / _pool_cnt(i, gi) - xw[POOL_HALO:, :]
            o_ref[i * POOL_ROWS:(i + 1) * POOL_ROWS, :] = (_dot(p, w_ref[...]) * sc_ref[...]).astype(o_ref.dtype)

    return pl.pallas_call(
        body, name="pool_fwd", grid=(4,),
        in_specs=[pl.BlockSpec((S, 128), lambda g: (0, OFF_POOL // 128 + g)),
                  pl.BlockSpec((None, 128, 128), lambda g: (g, 0, 0)),
                  pl.BlockSpec((1, 128), lambda g: (0, g))],
        out_specs=pl.BlockSpec((S, 128), lambda g: (0, g)),
        out_shape=jax.ShapeDtypeStruct((S, W_BRANCH), BF16),
        scratch_shapes=[pltpu.VMEM((S + POOL_HALO, 128), F32)],
        compiler_params=_cp(1),
    )(proj, pool_w, scale)


def pool_bwd(proj, pool_w, scale, dy):
    S = proj.shape[0]
    nchunk = S // POOL_ROWS
    slab = POOL_ROWS + POOL_HALO

    def body(x_ref, w_ref, sc_ref, dy_ref, dx_ref, dw_ref, dsc_ref, pad_ref, pad2_ref, dp_ref):
        gi = pl.program_id(0)
        pad_ref[0:POOL_HALO, :] = jnp.zeros((POOL_HALO, 128), F32)
        pad_ref[POOL_HALO:, :] = x_ref[...]
        pad2_ref[S:, :] = jnp.zeros((POOL_HALO, 128), F32)
        dw = jnp.zeros((128, 128), F32)
        dsc = jnp.zeros((1, 128), F32)
        for i in range(nchunk):
            xw = pad_ref[i * POOL_ROWS:i * POOL_ROWS + slab, :]
            cnt = _pool_cnt(i, gi)
            p = _pool_window_sum(xw, gi, lambda d: d)[POOL_HALO:, :] / cnt - xw[POOL_HALO:, :]
            dyc = dy_ref[i * POOL_ROWS:(i + 1) * POOL_ROWS, :]
            dsc = dsc + jnp.sum(dyc * _dot(p, w_ref[...]), axis=0, keepdims=True)
            dys = dyc * sc_ref[...]
            dw = dw + _dot(p, dys, "tn")
            dp = _dot(dys, w_ref[...], "nt")
            dp_ref[i * POOL_ROWS:(i + 1) * POOL_ROWS, :] = dp
            pad2_ref[i * POOL_ROWS:(i + 1) * POOL_ROWS, :] = dp / cnt
        dw_ref[...] = dw
        dsc_ref[...] = dsc
        for i in range(nchunk):
            xw = pad2_ref[i * POOL_ROWS:i * POOL_ROWS + slab, :]
            fsum = _pool_window_sum(xw, gi, lambda d: slab - d)[:POOL_ROWS, :]
            rows = slice(i * POOL_ROWS, (i + 1) * POOL_ROWS)
            dx_ref[rows, :] = (fsum - dp_ref[rows, :]).astype(dx_ref.dtype)

    return pl.pallas_call(
        body, name="pool_bwd", grid=(4,),
        in_specs=[pl.BlockSpec((S, 128), lambda g: (0, OFF_POOL // 128 + g)),
                  pl.BlockSpec((None, 128, 128), lambda g: (g, 0, 0)),
                  pl.BlockSpec((1, 128), lambda g: (0, g)),
                  pl.BlockSpec((S, 128), lambda g: (0, g))],
        out_specs=[pl.BlockSpec((S, 128), lambda g: (0, g)),
                   pl.BlockSpec((None, 128, 128), lambda g: (g, 0, 0)),
                   pl.BlockSpec((1, 128), lambda g: (0, g))],
        out_shape=[jax.ShapeDtypeStruct((S, W_BRANCH), BF16), jax.ShapeDtypeStruct((4, 128, 128), F32),
                   jax.ShapeDtypeStruct((1, W_BRANCH), F32)],
        scratch_shapes=[pltpu.VMEM((S + POOL_HALO, 128), F32), pltpu.VMEM((S + POOL_HALO, 128), F32),
                        pltpu.VMEM((S, 128), F32)],
        compiler_params=_cp(1),
    )(proj, pool_w, scale, dy)


def _t5_bucket(n):
    exact = REL_BUCKETS // 2
    nf = np.maximum(n, 1).astype(np.float32)
    large = exact + (np.log(nf / exact) / np.log(REL_MAX_DIST / exact) * (REL_BUCKETS - exact)).astype(np.int32)
    large = np.minimum(large, REL_BUCKETS - 1)
    return np.where(n < exact, n, large).astype(np.int32)


def _band_onehot(dil):
    i = np.arange(BAND)[:, None]
    kk = np.arange(2 * BAND)[None, :]
    dist = BAND + i - kk
    local = (dist >= 0) & (dist <= BAND)
    bucket = _t5_bucket(np.clip(dist, 0, BAND) * dil)
    onehot = (bucket.reshape(-1, 1) == np.arange(REL_BUCKETS)[None, :]).astype(np.float32)
    return onehot, local


def att_bias(rel_bias, g, dil):
    onehot, local = _band_onehot(dil)
    tab = jnp.dot(jnp.asarray(onehot), rel_bias[:, g * ATT_HEADS:(g + 1) * ATT_HEADS], precision=lax.Precision.HIGHEST)
    bias = tab.reshape(BAND, 2 * BAND, ATT_HEADS).transpose(2, 0, 1)
    return jnp.where(jnp.asarray(local)[None], bias, NEG_INF)


def att_bias_grad(dbias, dil):
    onehot, _ = _band_onehot(dil)
    flat = dbias.transpose(1, 2, 0).reshape(BAND * 2 * BAND, ATT_HEADS)
    return jnp.dot(jnp.asarray(onehot).T, flat, precision=lax.Precision.HIGHEST)


def _head_lanes():
    return lax.broadcasted_iota(jnp.int32, (BAND, 128), 1) < ATT_HEAD_DIM


def _att_cols(part, g, hp):
    return (OFF_ATT + part * 3 * W_BRANCH + g * W_BRANCH) // 128 + hp


def _att_pair(q, k, v, bias, lse_b, do, delta_b, hh, head0, mask=None):
    sel = head0 if hh == 0 else jnp.logical_not(head0)
    s = _dot(jnp.where(sel, q, 0.0), k, "nt") * (ATT_HEAD_DIM ** -0.5) + bias
    if mask is not None:
        s = jnp.where(mask, NEG_INF, s)
    c = hh * ATT_HEAD_DIM
    p = jnp.exp(s - lse_b[:, c:c + 1])
    dp = _dot(jnp.where(sel, do, 0.0), v, "nt")
    return p, p * (dp - delta_b[:, c:c + 1])


ATT_BLOCKS = {1: 8, 4: 2, 16: 1}


def _att_rows(r, i, d):
    return pl.ds(r + d * BAND * i, BAND, stride=d) if d > 1 else pl.ds(BAND * i, BAND)


def _att_specs(g, d, nq):
    ch, pb = BAND * d * nq, BAND * d
    cur = lambda part: pl.BlockSpec((ch, 128), lambda hp, n: (n, _att_cols(part, g, hp)))
    prev = lambda part: pl.BlockSpec((pb, 128), lambda hp, n: (jnp.maximum(n * nq - 1, 0), _att_cols(part, g, hp)))
    return [cur(0), cur(1), prev(1), cur(2), prev(2)]


def _att_keys(cur_ref, prev_ref, r, i, d):
    before = cur_ref[_att_rows(r, i - 1, d), :] if i > 0 else prev_ref[_att_rows(r, 0, d), :]
    return jnp.concatenate([before, cur_ref[_att_rows(r, i, d), :]], axis=0).astype(MXU_DTYPE)


def att_fwd(proj, bias, g, d):
    S = proj.shape[0]
    nq = ATT_BLOCKS[d]
    ch = BAND * d * nq

    def body(q_ref, kc_ref, kp_ref, vc_ref, vp_ref, b_ref, o_ref, l_ref):
        n = pl.program_id(1)
        head0 = _head_lanes()
        first = jnp.logical_and(lax.broadcasted_iota(jnp.int32, (BAND, 2 * BAND), 1) < BAND, n == 0)
        for r in range(d):
            for i in range(nq):
                rows = _att_rows(r, i, d)
                q = q_ref[rows, :]
                k = _att_keys(kc_ref, kp_ref, r, i, d)
                v = _att_keys(vc_ref, vp_ref, r, i, d)
                o_h, l_h = [], []
                for hh in range(2):
                    qm = jnp.where(head0 if hh == 0 else jnp.logical_not(head0), q, 0.0)
                    s = _dot(qm, k, "nt") * (ATT_HEAD_DIM ** -0.5) + b_ref[hh]
                    if i == 0:
                        s = jnp.where(first, NEG_INF, s)
                    m = jnp.max(s, axis=-1, keepdims=True)
                    p = jnp.exp(s - m)
                    l = jnp.sum(p, axis=-1, keepdims=True)
                    o_h.append(_dot(p / l, v))
                    l_h.append(jnp.broadcast_to(m + jnp.log(l), (BAND, 128)))
                o_ref[rows, :] = jnp.where(head0, o_h[0], o_h[1])
                l_ref[rows, :] = jnp.where(head0, l_h[0], l_h[1])

    out = pl.BlockSpec((ch, 128), lambda hp, n: (n, hp))
    return pl.pallas_call(
        body, name=f"att_fwd_d{d}", grid=(4, S // ch),
        in_specs=_att_specs(g, d, nq) + [pl.BlockSpec((2, BAND, 2 * BAND), lambda hp, n: (hp, 0, 0))],
        out_specs=[out, out],
        out_shape=[jax.ShapeDtypeStruct((S, W_BRANCH), F32), jax.ShapeDtypeStruct((S, W_BRANCH), F32)],
        compiler_params=_cp(2),
    )(proj, proj, proj, proj, proj, bias)


def att_bwd(proj, bias, lse, wts, dout, cbar, g, d):
    S = proj.shape[0]
    nq = ATT_BLOCKS[d]
    ch, pb = BAND * d * nq, BAND * d
    nb = S // ch
    scale = ATT_HEAD_DIM ** -0.5

    def body(q_ref, kc_ref, kp_ref, vc_ref, vp_ref, b_ref, l_ref, w_ref, do_ref, cb_ref,
             dq_ref, dk_ref, dv_ref, ek_ref, ev_ref, db_ref):
        n = pl.program_id(1)
        head0 = _head_lanes()
        first = jnp.logical_and(lax.broadcasted_iota(jnp.int32, (BAND, 2 * BAND), 1) < BAND, n == 0)

        @pl.when(n == 0)
        def _():
            db_ref[...] = jnp.zeros(db_ref.shape, F32)

        for r in range(d):
            own_k = own_v = None
            for i in range(nq):
                rows = _att_rows(r, i, d)
                q = q_ref[rows, :]
                k = _att_keys(kc_ref, kp_ref, r, i, d)
                v = _att_keys(vc_ref, vp_ref, r, i, d)
                w = w_ref[rows, :]
                do = w * do_ref[rows, :]
                delta = w * cb_ref[rows, :]
                lse_b = l_ref[rows, :]
                dq_h, dk_h, dv_h = [], [], []
                for hh in range(2):
                    p, ds = _att_pair(q, k, v, b_ref[hh], lse_b, do, delta, hh, head0, mask=first if i == 0 else None)
                    db_ref[hh] += ds
                    ds = ds * scale
                    dq_h.append(_dot(ds, k))
                    dk_h.append(_dot(ds, q, "tn"))
                    dv_h.append(_dot(p, do, "tn"))
                dq_ref[rows, :] = jnp.where(head0, dq_h[0], dq_h[1])
                head0_keys = jnp.concatenate([head0, head0], axis=0)
                dk2 = jnp.where(head0_keys, dk_h[0], dk_h[1])
                dv2 = jnp.where(head0_keys, dv_h[0], dv_h[1])
                if i == 0:
                    ek_ref[_att_rows(r, 0, d), :] = dk2[:BAND]
                    ev_ref[_att_rows(r, 0, d), :] = dv2[:BAND]
                else:
                    dk_ref[_att_rows(r, i - 1, d), :] = own_k + dk2[:BAND]
                    dv_ref[_att_rows(r, i - 1, d), :] = own_v + dv2[:BAND]
                own_k, own_v = dk2[BAND:], dv2[BAND:]
            dk_ref[_att_rows(r, nq - 1, d), :] = own_k
            dv_ref[_att_rows(r, nq - 1, d), :] = own_v

    cur = pl.BlockSpec((ch, 128), lambda hp, n: (n, hp))
    edge = pl.BlockSpec((pb, 128), lambda hp, n: (n, hp))
    bias_spec = pl.BlockSpec((2, BAND, 2 * BAND), lambda hp, n: (hp, 0, 0))
    big = jax.ShapeDtypeStruct((S, W_BRANCH), F32)
    small = jax.ShapeDtypeStruct((nb * pb, W_BRANCH), F32)
    dq, dk, dv, ek, ev, db = pl.pallas_call(
        body, name=f"att_bwd_d{d}", grid=(4, nb),
        in_specs=_att_specs(g, d, nq) + [bias_spec, cur, cur, cur, cur],
        out_specs=[cur, cur, cur, edge, edge, bias_spec],
        out_shape=[big, big, big, small, small, jax.ShapeDtypeStruct((ATT_HEADS, BAND, 2 * BAND), F32)],
        compiler_params=_cp(2),
    )(proj, proj, proj, proj, proj, bias, lse, wts, dout, cbar)

    def with_edges(main, edges):
        if nb == 1:
            return main
        main = main.reshape(nb, ch, W_BRANCH)
        add = jnp.pad(edges.reshape(nb, pb, W_BRANCH)[1:], ((0, 1), (ch - pb, 0), (0, 0)))
        return (main + add).reshape(S, W_BRANCH)

    return dq, with_edges(dk, ek), with_edges(dv, ev), db


def att_combine(os_, lses):
    S = os_[0].shape[0]

    def body(o0, o1, o2, l0, l1, l2, out_ref, w0, w1, w2):
        ls = [l0[...], l1[...], l2[...]]
        m = jnp.maximum(jnp.maximum(ls[0], ls[1]), ls[2])
        es = [jnp.exp(l - m) for l in ls]
        den = es[0] + es[1] + es[2]
        ws = [e / den for e in es]
        out_ref[...] = (ws[0] * o0[...] + ws[1] * o1[...] + ws[2] * o2[...]).astype(out_ref.dtype)
        for w_ref, w in zip((w0, w1, w2), ws):
            w_ref[...] = w

    blk = pl.BlockSpec((ROW_TILE, W_BRANCH), lambda i: (i, 0))
    f = jax.ShapeDtypeStruct((S, W_BRANCH), F32)
    return pl.pallas_call(
        body, name="att_combine", grid=(S // ROW_TILE,), in_specs=[blk] * 6, out_specs=[blk] * 4,
        out_shape=[jax.ShapeDtypeStruct((S, W_BRANCH), BF16), f, f, f], compiler_params=_cp(1),
    )(*os_, *lses)


def _split3(x):
    x1 = x.astype(BF16)
    r1 = x - x1.astype(F32)
    x2 = r1.astype(BF16)
    x3 = (r1 - x2.astype(F32)).astype(BF16)
    return x1, x2, x3


def att_combine_bwd(dout, os_, wts, headsum):
    S = dout.shape[0]

    def body(do_ref, o0, o1, o2, w0, w1, w2, e_ref, cb_ref):
        out = w0[...] * o0[...] + w1[...] * o1[...] + w2[...] * o2[...]
        e = e_ref[...]
        acc = jnp.zeros((ROW_TILE, W_BRANCH), F32)
        for term in _split3(do_ref[...] * out):
            acc = acc + jnp.dot(term, e, preferred_element_type=F32)
        cb_ref[...] = acc

    blk = pl.BlockSpec((ROW_TILE, W_BRANCH), lambda i: (i, 0))
    return pl.pallas_call(
        body, name="att_combine_bwd", grid=(S // ROW_TILE,),
        in_specs=[blk] * 7 + [pl.BlockSpec((W_BRANCH, W_BRANCH), lambda i: (0, 0))], out_specs=blk,
        out_shape=jax.ShapeDtypeStruct((S, W_BRANCH), F32), compiler_params=_cp(1),
    )(dout, *os_, *wts, headsum)


def _cmul(ar, ai, br, bi):
    return ar * br - ai * bi, ar * bi + ai * br


SCAN_ROWS = 8
SCAN_GROUPS = SSM_T // SCAN_ROWS


def _log_scan(xr, xi, mr, mi, rows, n, steps, reverse):
    total = xr.shape[0]
    for k in range(steps):
        dd = 1 << k
        keep = rows < n - dd if reverse else rows >= dd
        shift = total - dd if reverse else dd
        ar, ai = _cmul(mr, mi, jnp.where(keep, pltpu.roll(xr, shift, 0), 0.0), jnp.where(keep, pltpu.roll(xi, shift, 0), 0.0))
        xr, xi = xr + ar, xi + ai
        mr, mi = _cmul(mr, mi, mr, mi)
    return xr, xi, mr, mi


def _scan_scratch(n_results):
    return ([pltpu.VMEM((W_BRANCH // 128, SSM_T, 128), F32)] * 2 + [pltpu.VMEM((SCAN_GROUPS, W_BRANCH), F32)] * 2
            + [pltpu.VMEM((SSM_T, W_BRANCH), F32)] * n_results)


def _block_scan(xr, xi, mr, mi, reverse, yr_ref, yi_ref, er_ref, ei_ref, hr_ref, hi_ref):
    cols = xr.shape[1]
    rows = lax.broadcasted_iota(jnp.int32, (SSM_T, cols), 0)
    yr, yi, m8r, m8i = _log_scan(xr, xi, mr, mi, rows & (SCAN_ROWS - 1), SCAN_ROWS, 3, reverse)
    lane_blocks = range(cols // 128)
    for c in lane_blocks:
        yr_ref[c] = yr[:, c * 128:(c + 1) * 128]
        yi_ref[c] = yi[:, c * 128:(c + 1) * 128]
    wide = lambda ref, rows_: jnp.concatenate([ref[c, rows_, :] for c in lane_blocks], axis=1)
    end = pl.ds(0 if reverse else SCAN_ROWS - 1, SCAN_GROUPS, stride=SCAN_ROWS)
    groups = lax.broadcasted_iota(jnp.int32, (SCAN_GROUPS, cols), 0)
    er, ei, _, _ = _log_scan(wide(yr_ref, end), wide(yi_ref, end), m8r, m8i, groups, SCAN_GROUPS,
                             int(math.log2(SCAN_GROUPS)), reverse)
    er_ref[...] = er
    ei_ref[...] = ei
    j = lax.broadcasted_iota(jnp.int32, (SCAN_ROWS, cols), 0)
    dist = SCAN_ROWS - j if reverse else j + 1
    tr, ti = jnp.ones((SCAN_ROWS, cols), F32), jnp.zeros((SCAN_ROWS, cols), F32)
    br, bi = mr, mi
    for bit in range(4):
        nr, ni = _cmul(tr, ti, br, bi)
        take = ((dist >> bit) & 1) == 1
        tr, ti = jnp.where(take, nr, tr), jnp.where(take, ni, ti)
        br, bi = _cmul(br, bi, br, bi)
    for g in range(SCAN_GROUPS):
        before = g + 1 if reverse else g - 1
        rows_g = slice(g * SCAN_ROWS, (g + 1) * SCAN_ROWS)
        if 0 <= before < SCAN_GROUPS:
            ar, ai = _cmul(tr, ti, er_ref[before:before + 1, :], ei_ref[before:before + 1, :])
            hr_ref[rows_g, :] = wide(yr_ref, rows_g) + ar
            hi_ref[rows_g, :] = wide(yi_ref, rows_g) + ai
        else:
            hr_ref[rows_g, :] = wide(yr_ref, rows_g)
            hi_ref[rows_g, :] = wide(yi_ref, rows_g)
    last = 0 if reverse else SCAN_GROUPS - 1
    return er_ref[last:last + 1, :], ei_ref[last:last + 1, :]


def s5_fwd(proj, b_re, b_im, a_re, a_im, c_re, c_im, d_skip):
    S = proj.shape[0]
    nt = S // SSM_T

    def body(u_ref, bre_ref, bim_ref, ar_ref, ai_ref, cre_ref, cim_ref, dsk_ref, hr_ref, hi_ref, y_ref, cr_ref, ci_ref,
             yr_ref, yi_ref, er_ref, ei_ref):
        t = pl.program_id(1)

        @pl.when(t == 0)
        def _():
            cr_ref[...] = jnp.zeros(cr_ref.shape, F32)
            ci_ref[...] = jnp.zeros(ci_ref.shape, F32)

        u = u_ref[...]
        ar, ai = ar_ref[...], ai_ref[...]
        rows = lax.broadcasted_iota(jnp.int32, (SSM_T, W_BRANCH), 0)
        inr, ini = _cmul(ar, ai, cr_ref[0:1, :], ci_ref[0:1, :])
        xr = _dot(u, bre_ref[...]) + jnp.where(rows == 0, inr, 0.0)
        xi = _dot(u, bim_ref[...]) + jnp.where(rows == 0, ini, 0.0)
        endr, endi = _block_scan(xr, xi, ar, ai, False, yr_ref, yi_ref, er_ref, ei_ref, hr_ref, hi_ref)
        cr_ref[...] = jnp.broadcast_to(endr, cr_ref.shape)
        ci_ref[...] = jnp.broadcast_to(endi, ci_ref.shape)
        xr, xi = hr_ref[...], hi_ref[...]
        y_ref[...] = _dot(xr, cre_ref[...]) - _dot(xi, cim_ref[...]) + u * dsk_ref[...]

    u_spec = pl.BlockSpec((SSM_T, 128), lambda j, t: (t, OFF_SSM // 128 + j))
    b_spec = pl.BlockSpec((None, 128, W_BRANCH), lambda j, t: (j, 0, 0))
    a_spec = pl.BlockSpec((1, W_BRANCH), lambda j, t: (0, j))
    c_spec = pl.BlockSpec((None, W_BRANCH, 128), lambda j, t: (j, 0, 0))
    h_spec = pl.BlockSpec((SSM_T, W_BRANCH), lambda j, t: (t, j))
    return pl.pallas_call(
        body, name="s5_fwd", grid=(4, nt),
        in_specs=[u_spec, b_spec, b_spec, a_spec, a_spec, c_spec, c_spec, pl.BlockSpec((1, 128), lambda j, t: (0, j))],
        out_specs=[h_spec, h_spec, pl.BlockSpec((SSM_T, 128), lambda j, t: (t, j))],
        out_shape=[jax.ShapeDtypeStruct((S, SSM_COLS), F32), jax.ShapeDtypeStruct((S, SSM_COLS), F32),
                   jax.ShapeDtypeStruct((S, W_BRANCH), F32)],
        scratch_shapes=[pltpu.VMEM((8, W_BRANCH), F32)] * 2 + _scan_scratch(0),
        compiler_params=_cp(2),
    )(proj, b_re, b_im, a_re, a_im, c_re, c_im, d_skip)


def s5_bwd(proj, hr, hi, dy, b_re, b_im, a_re, a_im, c_re, c_im, d_skip):
    S = proj.shape[0]
    nt = S // SSM_T

    def body(u_ref, hr_ref, hi_ref, hpr_ref, hpi_ref, dy_ref, bre_ref, bim_ref, ar_ref, ai_ref, cre_ref, cim_ref,
             dsk_ref, du_ref, dbre_ref, dbim_ref, dar_ref, dai_ref, dcre_ref, dcim_ref, ddsk_ref, gr_ref, gi_ref,
             yr_ref, yi_ref, er_ref, ei_ref, sr_ref, si_ref):
        step = pl.program_id(1)
        t = nt - 1 - step

        @pl.when(step == 0)
        def _():
            gr_ref[...] = jnp.zeros(gr_ref.shape, F32)
            gi_ref[...] = jnp.zeros(gi_ref.shape, F32)
            for ref in (dbre_ref, dbim_ref, dar_ref, dai_ref, dcre_ref, dcim_ref, ddsk_ref):
                ref[...] = jnp.zeros(ref.shape, F32)

        u = u_ref[...]
        dy = dy_ref[...]
        ar, ai = ar_ref[...], ai_ref[...]
        rows = lax.broadcasted_iota(jnp.int32, (SSM_T, W_BRANCH), 0)
        inr, ini = _cmul(ar, -ai, gr_ref[0:1, :], gi_ref[0:1, :])
        xr = _dot(dy, cre_ref[...], "nt") + jnp.where(rows == SSM_T - 1, inr, 0.0)
        xi = -_dot(dy, cim_ref[...], "nt") + jnp.where(rows == SSM_T - 1, ini, 0.0)
        endr, endi = _block_scan(xr, xi, ar, -ai, True, yr_ref, yi_ref, er_ref, ei_ref, sr_ref, si_ref)
        gr_ref[...] = jnp.broadcast_to(endr, gr_ref.shape)
        gi_ref[...] = jnp.broadcast_to(endi, gi_ref.shape)
        xr, xi = sr_ref[...], si_ref[...]
        hr_blk, hi_blk = hr_ref[...], hi_ref[...]
        keep = (t > 0).astype(F32)
        hpr = jnp.where(rows >= 1, pltpu.roll(hr_blk, 1, 0), hpr_ref[7:8, :] * keep)
        hpi = jnp.where(rows >= 1, pltpu.roll(hi_blk, 1, 0), hpi_ref[7:8, :] * keep)
        dar_ref[...] += jnp.sum(hpr * xr + hpi * xi, axis=0, keepdims=True)
        dai_ref[...] += jnp.sum(hpr * xi - hpi * xr, axis=0, keepdims=True)
        dcre_ref[...] += _dot(hr_blk, dy, "tn")
        dcim_ref[...] -= _dot(hi_blk, dy, "tn")
        du = dy * dsk_ref[...] + _dot(xr, bre_ref[...], "nt") + _dot(xi, bim_ref[...], "nt")
        du_ref[...] = du.astype(du_ref.dtype)
        dbre_ref[...] += _dot(u, xr, "tn")
        dbim_ref[...] += _dot(u, xi, "tn")
        ddsk_ref[...] += jnp.sum(dy * u, axis=0, keepdims=True)

    def rev(t):
        return nt - 1 - t

    u_spec = pl.BlockSpec((SSM_T, 128), lambda j, t: (rev(t), OFF_SSM // 128 + j))
    h_spec = pl.BlockSpec((SSM_T, W_BRANCH), lambda j, t: (rev(t), j))
    hprev_spec = pl.BlockSpec((8, W_BRANCH), lambda j, t: (jnp.maximum(rev(t) * (SSM_T // 8) - 1, 0), j))
    ch_spec = pl.BlockSpec((SSM_T, 128), lambda j, t: (rev(t), j))
    b_spec = pl.BlockSpec((None, 128, W_BRANCH), lambda j, t: (j, 0, 0))
    a_spec = pl.BlockSpec((1, W_BRANCH), lambda j, t: (0, j))
    c_spec = pl.BlockSpec((None, W_BRANCH, 128), lambda j, t: (j, 0, 0))
    d_spec = pl.BlockSpec((1, 128), lambda j, t: (0, j))
    return pl.pallas_call(
        body, name="s5_bwd", grid=(4, nt),
        in_specs=[u_spec, h_spec, h_spec, hprev_spec, hprev_spec, ch_spec, b_spec, b_spec, a_spec, a_spec,
                  c_spec, c_spec, d_spec],
        out_specs=[ch_spec, b_spec, b_spec, a_spec, a_spec, c_spec, c_spec, d_spec],
        out_shape=[jax.ShapeDtypeStruct((S, W_BRANCH), BF16),
                   jax.ShapeDtypeStruct((4, 128, W_BRANCH), F32), jax.ShapeDtypeStruct((4, 128, W_BRANCH), F32),
                   jax.ShapeDtypeStruct((1, SSM_COLS), F32), jax.ShapeDtypeStruct((1, SSM_COLS), F32),
                   jax.ShapeDtypeStruct((4, W_BRANCH, 128), F32), jax.ShapeDtypeStruct((4, W_BRANCH, 128), F32),
                   jax.ShapeDtypeStruct((1, W_BRANCH), F32)],
        scratch_shapes=[pltpu.VMEM((8, W_BRANCH), F32)] * 2 + _scan_scratch(2),
        compiler_params=_cp(2),
    )(proj, hr, hi, hr, hi, dy, b_re, b_im, a_re, a_im, c_re, c_im, d_skip)


def glu_fwd(y, w_glu, b_glu):
    S = y.shape[0]

    def body(y_ref, w_ref, b_ref, o_ref):
        g = _gelu(y_ref[...])
        o_ref[...] = (g * _sigmoid(_dot(g, w_ref[...]) + b_ref[...])).astype(o_ref.dtype)

    blk = pl.BlockSpec((ROW_TILE, W_BRANCH), lambda i: (i, 0))
    return pl.pallas_call(
        body, name="glu_fwd", grid=(S // ROW_TILE,),
        in_specs=[blk, pl.BlockSpec((W_BRANCH, W_BRANCH), lambda i: (0, 0)), pl.BlockSpec((1, W_BRANCH), lambda i: (0, 0))],
        out_specs=blk, out_shape=jax.ShapeDtypeStruct((S, W_BRANCH), BF16), compiler_params=_cp(1),
    )(y, w_glu, b_glu)


def glu_bwd(y, w_glu, b_glu, dout):
    S = y.shape[0]

    def body(y_ref, w_ref, b_ref, do_ref, dy_ref, dw_ref, db_ref):
        yv = y_ref[...]
        do = do_ref[...]
        g = _gelu(yv)
        s = _sigmoid(_dot(g, w_ref[...]) + b_ref[...])
        dz = do * g * s * (1.0 - s)
        dg = do * s + _dot(dz, w_ref[...], "nt")
        dy_ref[...] = dg * _gelu_grad(yv)
        dw = _dot(g, dz, "tn")
        db = jnp.sum(dz, axis=0, keepdims=True)

        @pl.when(pl.program_id(0) == 0)
        def _():
            dw_ref[...] = dw
            db_ref[...] = db

        @pl.when(pl.program_id(0) > 0)
        def _():
            dw_ref[...] += dw
            db_ref[...] += db

    blk = pl.BlockSpec((ROW_TILE, W_BRANCH), lambda i: (i, 0))
    mat = pl.BlockSpec((W_BRANCH, W_BRANCH), lambda i: (0, 0))
    vec = pl.BlockSpec((1, W_BRANCH), lambda i: (0, 0))
    return pl.pallas_call(
        body, name="glu_bwd", grid=(S // ROW_TILE,), in_specs=[blk, mat, vec, blk], out_specs=[blk, mat, vec],
        out_shape=[jax.ShapeDtypeStruct((S, W_BRANCH), F32), jax.ShapeDtypeStruct((W_BRANCH, W_BRANCH), F32),
                   jax.ShapeDtypeStruct((1, W_BRANCH), F32)],
        compiler_params=_cp(1),
    )(y, w_glu, b_glu, dout)


SGU_TILE = 512
SGU_U_BLOCK = OFF_SGU // W_BRANCH
SGU_V_BLOCK = SGU_U_BLOCK + 1


def _sgu_norm(zv):
    v = _gelu(zv)
    mu = jnp.mean(v, axis=-1, keepdims=True)
    vc = v - mu
    rstd = lax.rsqrt(jnp.mean(vc * vc, axis=-1, keepdims=True) + EPS)
    return vc * rstd, rstd


def _tril():
    return lax.broadcasted_iota(jnp.int32, (SGU_CHUNK, SGU_CHUNK), 0) >= lax.broadcasted_iota(jnp.int32, (SGU_CHUNK, SGU_CHUNK), 1)


def sgu_fwd(proj, ln_g, ln_b, w_s, b_s_t):
    S = proj.shape[0]

    def body(zu_ref, zv_ref, g_ref, b_ref, ws_ref, bs_ref, o_ref, vf_ref):
        vn, _ = _sgu_norm(zv_ref[...])
        vf_ref[...] = vn * g_ref[...] + b_ref[...]
        tri = _tril()
        for gi in range(4):
            ws = jnp.where(tri, ws_ref[gi], 0.0)
            cols = slice(gi * 128, (gi + 1) * 128)
            for c in range(SGU_TILE // SGU_CHUNK):
                rows = slice(c * SGU_CHUNK, (c + 1) * SGU_CHUNK)
                sv = _dot(ws, vf_ref[rows, cols]) + bs_ref[:, gi:gi + 1]
                o_ref[rows, cols] = (_gelu(zu_ref[rows, cols]) * sv).astype(o_ref.dtype)

    blk = lambda cb: pl.BlockSpec((SGU_TILE, W_BRANCH), lambda i: (i, cb))
    vec = pl.BlockSpec((1, W_BRANCH), lambda i: (0, 0))
    return pl.pallas_call(
        body, name="sgu_fwd", grid=(S // SGU_TILE,),
        in_specs=[blk(SGU_U_BLOCK), blk(SGU_V_BLOCK), vec, vec, pl.BlockSpec((4, SGU_CHUNK, SGU_CHUNK), lambda i: (0, 0, 0)),
                  pl.BlockSpec((SGU_CHUNK, 4), lambda i: (0, 0))],
        out_specs=blk(0), out_shape=jax.ShapeDtypeStruct((S, W_BRANCH), BF16),
        scratch_shapes=[pltpu.VMEM((SGU_TILE, W_BRANCH), F32)], compiler_params=_cp(1),
    )(proj, proj, ln_g, ln_b, w_s, b_s_t)


def sgu_bwd(proj, ln_g, ln_b, w_s, b_s_t, dout):
    S = proj.shape[0]

    def body(zu_ref, zv_ref, g_ref, b_ref, ws_ref, bs_ref, do_ref, dzu_ref, dzv_ref, dg_ref, db_ref, dws_ref, dbs_ref,
             vf_ref, dvf_ref):
        @pl.when(pl.program_id(0) == 0)
        def _():
            for ref in (dg_ref, db_ref, dws_ref, dbs_ref):
                ref[...] = jnp.zeros(ref.shape, F32)

        vn, rstd = _sgu_norm(zv_ref[...])
        vf_ref[...] = vn * g_ref[...] + b_ref[...]
        tri = _tril()
        lane = lax.broadcasted_iota(jnp.int32, (SGU_CHUNK, 128), 1)
        dbs = jnp.zeros((SGU_CHUNK, 128), F32)
        for gi in range(4):
            ws = jnp.where(tri, ws_ref[gi], 0.0)
            cols = slice(gi * 128, (gi + 1) * 128)
            dws = jnp.zeros((SGU_CHUNK, SGU_CHUNK), F32)
            for c in range(SGU_TILE // SGU_CHUNK):
                rows = slice(c * SGU_CHUNK, (c + 1) * SGU_CHUNK)
                vf = vf_ref[rows, cols]
                zu = zu_ref[rows, cols]
                do = do_ref[rows, cols]
                sv = _dot(ws, vf) + bs_ref[:, gi:gi + 1]
                dzu_ref[rows, cols] = (do * sv * _gelu_grad(zu)).astype(dzu_ref.dtype)
                dsv = do * _gelu(zu)
                dvf_ref[rows, cols] = _dot(ws, dsv, "tn")
                dws = dws + _dot(dsv, vf, "nt")
                dbs = dbs + jnp.where(lane == gi, jnp.sum(dsv, axis=-1, keepdims=True), 0.0)
            dws_ref[gi] += jnp.where(tri, dws, 0.0)
        dbs_ref[...] += dbs
        dvf = dvf_ref[...]
        dg_ref[...] += jnp.sum(dvf * vn, axis=0, keepdims=True)
        db_ref[...] += jnp.sum(dvf, axis=0, keepdims=True)
        dvn = dvf * g_ref[...]
        dv = rstd * (dvn - jnp.mean(dvn, axis=-1, keepdims=True) - vn * jnp.mean(dvn * vn, axis=-1, keepdims=True))
        dzv_ref[...] = (dv * _gelu_grad(zv_ref[...])).astype(dzv_ref.dtype)

    blk = lambda cb: pl.BlockSpec((SGU_TILE, W_BRANCH), lambda i: (i, cb))
    vec = pl.BlockSpec((1, W_BRANCH), lambda i: (0, 0))
    ws_spec = pl.BlockSpec((4, SGU_CHUNK, SGU_CHUNK), lambda i: (0, 0, 0))
    return pl.pallas_call(
        body, name="sgu_bwd", grid=(S // SGU_TILE,),
        in_specs=[blk(SGU_U_BLOCK), blk(SGU_V_BLOCK), vec, vec, ws_spec, pl.BlockSpec((SGU_CHUNK, 4), lambda i: (0, 0)),
                  blk(0)],
        out_specs=[blk(0), blk(0), vec, vec, ws_spec, pl.BlockSpec((SGU_CHUNK, 128), lambda i: (0, 0))],
        out_shape=[jax.ShapeDtypeStruct((S, W_BRANCH), BF16), jax.ShapeDtypeStruct((S, W_BRANCH), BF16),
                   jax.ShapeDtypeStruct((1, W_BRANCH), F32), jax.ShapeDtypeStruct((1, W_BRANCH), F32),
                   jax.ShapeDtypeStruct((4, SGU_CHUNK, SGU_CHUNK), F32), jax.ShapeDtypeStruct((SGU_CHUNK, 128), F32)],
        scratch_shapes=[pltpu.VMEM((SGU_TILE, W_BRANCH), F32), pltpu.VMEM((SGU_TILE, W_BRANCH), F32)],
        compiler_params=_cp(1),
    )(proj, proj, ln_g, ln_b, w_s, b_s_t, dout)


GM_TILE = 512


def _gate_specs(order):
    def spec(i):
        def index(*ids):
            m, n = order(*ids)
            return (m, (OFF_GATE + i * D_MODEL) // GM_TILE + n)
        return pl.BlockSpec((GM_TILE, GM_TILE), index)
    return [spec(i) for i in range(4)]


def merge_fwd(proj, gate_b, branches, w_up):
    S = proj.shape[0]
    order = lambda n, m: (m, n)

    def body(p0, p1, p2, p3, gb_ref, b0, b1, b2, b3, w_ref, o_ref):
        acc = jnp.zeros((GM_TILE, GM_TILE), F32)
        for i, (p_ref, br_ref) in enumerate(zip((p0, p1, p2, p3), (b0, b1, b2, b3))):
            acc = acc + _sigmoid(p_ref[...] + gb_ref[i:i + 1, :]) * _dot(br_ref[...], w_ref[i])
        o_ref[...] = acc.astype(o_ref.dtype)

    br_spec = pl.BlockSpec((GM_TILE, W_BRANCH), lambda n, m: (m, 0))
    return pl.pallas_call(
        body, name="merge_fwd", grid=(D_MODEL // GM_TILE, S // GM_TILE),
        in_specs=_gate_specs(order) + [pl.BlockSpec((4, GM_TILE), lambda n, m: (0, n))] + [br_spec] * 4
        + [pl.BlockSpec((4, W_BRANCH, GM_TILE), lambda n, m: (0, 0, n))],
        out_specs=pl.BlockSpec((GM_TILE, GM_TILE), lambda n, m: (m, n)),
        out_shape=jax.ShapeDtypeStruct((S, D_MODEL), BF16), compiler_params=_cp(2),
    )(proj, proj, proj, proj, gate_b, *branches, w_up)


def merge_bwd(proj, gate_b, branches, w_up, dmerged):
    S = proj.shape[0]
    order = lambda n, m: (m, n)

    def body(p0, p1, p2, p3, gb_ref, b0, b1, b2, b3, w_ref, dm_ref, dp0, dp1, dp2, dp3, du0, du1, du2, du3, dgb_ref):
        dm = dm_ref[...]
        dgb = []
        for i, (p_ref, br_ref, dp_ref, du_ref) in enumerate(
                zip((p0, p1, p2, p3), (b0, b1, b2, b3), (dp0, dp1, dp2, dp3), (du0, du1, du2, du3))):
            gate = _sigmoid(p_ref[...] + gb_ref[i:i + 1, :])
            dpre = dm * _dot(br_ref[...], w_ref[i]) * gate * (1.0 - gate)
            dp_ref[...] = dpre.astype(dp_ref.dtype)
            du_ref[...] = (dm * gate).astype(du_ref.dtype)
            dgb.append(jnp.sum(dpre, axis=0, keepdims=True))
        dgb = jnp.concatenate(dgb, axis=0)

        @pl.when(pl.program_id(1) == 0)
        def _():
            dgb_ref[...] = dgb

        @pl.when(pl.program_id(1) > 0)
        def _():
            dgb_ref[...] += dgb

    br_spec = pl.BlockSpec((GM_TILE, W_BRANCH), lambda n, m: (m, 0))
    mn = pl.BlockSpec((GM_TILE, GM_TILE), lambda n, m: (m, n))
    gb = pl.BlockSpec((4, GM_TILE), lambda n, m: (0, n))
    big = jax.ShapeDtypeStruct((S, D_MODEL), BF16)
    outs = pl.pallas_call(
        body, name="merge_bwd", grid=(D_MODEL // GM_TILE, S // GM_TILE),
        in_specs=_gate_specs(order) + [gb] + [br_spec] * 4
        + [pl.BlockSpec((4, W_BRANCH, GM_TILE), lambda n, m: (0, 0, n)), mn],
        out_specs=[mn] * 8 + [gb], out_shape=[big] * 8 + [jax.ShapeDtypeStruct((4, D_MODEL), F32)],
        compiler_params=_cp(2),
    )(proj, proj, proj, proj, gate_b, *branches, w_up, dmerged)
    return outs[0:4], outs[4:8], outs[8]


def _xatt_probs(q, k):
    s = _dot(q, k, "nt") * (X_HEAD_DIM ** -0.5)
    p = jnp.exp(s - jnp.max(s, axis=-1, keepdims=True))
    return p / jnp.sum(p, axis=-1, keepdims=True)


def xatt_fwd(q, kv):
    S = q.shape[0]

    def body(q_ref, kv_ref, o_ref):
        for h in range(X_HEADS):
            cols = slice(h * X_HEAD_DIM, (h + 1) * X_HEAD_DIM)
            p = _xatt_probs(q_ref[:, cols], kv_ref[:, cols])
            o_ref[:, cols] = _dot(p, kv_ref[:, W_BRANCH + h * X_HEAD_DIM:W_BRANCH + (h + 1) * X_HEAD_DIM]).astype(o_ref.dtype)

    blk = pl.BlockSpec((ROW_TILE, W_BRANCH), lambda i: (i, 0))
    return pl.pallas_call(
        body, name="xatt_fwd", grid=(S // ROW_TILE,),
        in_specs=[blk, pl.BlockSpec((N_MEM, 2 * W_BRANCH), lambda i: (0, 0))], out_specs=blk,
        out_shape=jax.ShapeDtypeStruct((S, W_BRANCH), BF16), compiler_params=_cp(1),
    )(q, kv)


def xatt_bwd(q, kv, do):
    S = q.shape[0]

    def body(q_ref, kv_ref, do_ref, dq_ref, dkv_ref):
        @pl.when(pl.program_id(0) == 0)
        def _():
            dkv_ref[...] = jnp.zeros(dkv_ref.shape, F32)

        for h in range(X_HEADS):
            cols = slice(h * X_HEAD_DIM, (h + 1) * X_HEAD_DIM)
            vcols = slice(W_BRANCH + h * X_HEAD_DIM, W_BRANCH + (h + 1) * X_HEAD_DIM)
            qh, kh, doh = q_ref[:, cols], kv_ref[:, cols], do_ref[:, cols]
            p = _xatt_probs(qh, kh)
            dp = _dot(doh, kv_ref[:, vcols], "nt")
            ds = p * (dp - jnp.sum(dp * p, axis=-1, keepdims=True)) * (X_HEAD_DIM ** -0.5)
            dq_ref[:, cols] = _dot(ds, kh).astype(dq_ref.dtype)
            dkv_ref[:, cols] += _dot(ds, qh, "tn")
            dkv_ref[:, vcols] += _dot(p, doh, "tn")

    blk = pl.BlockSpec((ROW_TILE, W_BRANCH), lambda i: (i, 0))
    kv_spec = pl.BlockSpec((N_MEM, 2 * W_BRANCH), lambda i: (0, 0))
    return pl.pallas_call(
        body, name="xatt_bwd", grid=(S // ROW_TILE,), in_specs=[blk, kv_spec, blk], out_specs=[blk, kv_spec],
        out_shape=[jax.ShapeDtypeStruct((S, W_BRANCH), BF16), jax.ShapeDtypeStruct((N_MEM, 2 * W_BRANCH), F32)],
        compiler_params=_cp(1),
    )(q, kv, do)


def s5_params(a_re, a_im, log_dt, b_re, b_im, c_re, c_im):
    lam_re = jnp.minimum(a_re, -1e-4)
    lam_im = a_im
    dt = jnp.exp(log_dt)[:, None]
    mag = jnp.exp(lam_re * dt)
    ab_re, ab_im = mag * jnp.cos(lam_im * dt), mag * jnp.sin(lam_im * dt)
    den = lam_re * lam_re + lam_im * lam_im
    f_re = ((ab_re - 1.0) * lam_re + ab_im * lam_im) / den
    f_im = (ab_im * lam_re - (ab_re - 1.0) * lam_im) / den
    bb_re = f_re[..., None] * b_re - f_im[..., None] * b_im
    bb_im = f_re[..., None] * b_im + f_im[..., None] * b_re
    eye = jnp.eye(8, dtype=F32)

    def b_blocks(bb):
        t = bb.reshape(4, 8, SSM_STATE, SSM_GROUP).transpose(0, 1, 3, 2)
        return (t[:, :, :, None, :] * eye[None, :, None, :, None]).reshape(4, 128, W_BRANCH)

    def c_blocks(cc):
        t = cc.reshape(4, 8, SSM_GROUP, SSM_STATE).transpose(0, 1, 3, 2)
        return (t[:, :, :, None, :] * eye[None, :, None, :, None]).reshape(4, W_BRANCH, 128)

    return (ab_re.reshape(1, SSM_COLS), ab_im.reshape(1, SSM_COLS), b_blocks(bb_re), b_blocks(bb_im),
            c_blocks(c_re), c_blocks(c_im))


ANY = pl.BlockSpec(memory_space=pl.ANY)


def _chip_index():
    return 2 * lax.axis_index("x") + lax.axis_index("y")


def _peer_chip(j):
    x, y, c = lax.axis_index("x"), lax.axis_index("y"), lax.axis_index("c")
    return ((1 - x) if j & 2 else x, (1 - y) if j & 1 else y, c)


def _piece(ref, axis, s, n):
    size = ref.shape[axis] // n
    idx = [slice(None)] * len(ref.shape)
    idx[axis] = pl.ds(s * size, size)
    return ref.at[tuple(idx)]


HBM_SPEC = pl.BlockSpec(memory_space=pltpu.HBM)
SEM_SPEC = pl.BlockSpec(memory_space=pltpu.SEMAPHORE)
SIDE_EFFECT = pltpu.SideEffectType.DATAFLOW_SIDE_EFFECTING


def _chip_copies(ins, lands, send, recv, axes, mode, k, arriving):
    copies = []
    for t in range(len(ins)):
        for j in (1, 2, 3):
            place = k ^ j if arriving else k
            if mode == "gather":
                src, dst = ins[t], _piece(lands[t], axes[t], place, 4)
            else:
                src = ins[t] if axes[t] is None else _piece(ins[t], axes[t], k ^ j, 4)
                dst = lands[t].at[place]
            copies.append(pltpu.make_async_remote_copy(
                src_ref=src, dst_ref=dst, send_sem=send.at[3 * t + j - 1], recv_sem=recv.at[3 * t + j - 1],
                device_id=_peer_chip(j), device_id_type=MESH_ID))
    return copies


def chips_start(ins, lands, axes, mode, name, after=()):
    n, na = len(ins), len(after)

    def body(*refs):
        in_refs, land_refs = refs[:n], refs[n:2 * n]
        send, recv, token = refs[2 * n + na], refs[2 * n + na + 1], refs[-1]
        q = _chip_index()
        for k in range(4):
            @pl.when(q == k)
            def _():
                for copy in _chip_copies(in_refs, land_refs, send, recv, axes, mode, k, arriving=False):
                    copy.start()
        token[...] = jnp.zeros(token.shape, token.dtype)

    hbm = lambda a: pltpu.HBM(a.shape, a.dtype)
    outs = pl.pallas_call(
        body, name=name, in_specs=[HBM_SPEC] * (2 * n) + [ANY] * na,
        out_specs=[SEM_SPEC, SEM_SPEC] + [HBM_SPEC] * (2 * n) + [pl.BlockSpec(memory_space=pltpu.VMEM)],
        out_shape=[pltpu.SemaphoreType.DMA((3 * n,)), pltpu.SemaphoreType.DMA((3 * n,))]
        + [hbm(a) for a in ins] + [hbm(a) for a in lands] + [jax.ShapeDtypeStruct((8, 128), F32)],
        input_output_aliases={i: 2 + i for i in range(2 * n)},
        compiler_params=pltpu.CompilerParams(has_side_effects=SIDE_EFFECT),
    )(*[pltpu.with_memory_space_constraint(a, pltpu.HBM) for a in list(ins) + list(lands)], *after)
    return outs[0], outs[1], outs[2:2 + n], outs[2 + n:2 + 2 * n], outs[-1]


def chips_wait(send, recv, ins, lands, axes, mode, name, after=()):
    n = len(ins)

    def body(*refs):
        in_refs, land_refs = refs[:n], refs[n:2 * n]
        send_ref, recv_ref = refs[2 * n], refs[2 * n + 1]
        q = _chip_index()
        for k in range(4):
            @pl.when(q == k)
            def _():
                for copy in _chip_copies(in_refs, land_refs, send_ref, recv_ref, axes, mode, k, arriving=True):
                    copy.wait_send()
                    copy.wait_recv()

    hbm = lambda a: pltpu.HBM(a.shape, a.dtype)
    outs = pl.pallas_call(
        body, name=name, in_specs=[HBM_SPEC] * (2 * n) + [SEM_SPEC, SEM_SPEC] + [ANY] * len(after),
        out_specs=[HBM_SPEC] * (2 * n), out_shape=[hbm(a) for a in ins] + [hbm(a) for a in lands],
        input_output_aliases={i: i for i in range(2 * n)},
        compiler_params=pltpu.CompilerParams(has_side_effects=SIDE_EFFECT),
    )(*ins, *lands, send, recv, *after)
    return outs[:n], outs[n:]


def swap_cores(arrs, name):
    n = len(arrs)

    def body(*refs):
        ins, outs = refs[:n], refs[n:2 * n]
        send, recv = refs[2 * n:]
        sibling = (lax.axis_index("x"), lax.axis_index("y"), 1 - lax.axis_index("c"))
        copies = [pltpu.make_async_remote_copy(src_ref=ins[t], dst_ref=outs[t], send_sem=send.at[t], recv_sem=recv.at[t],
                                               device_id=sibling, device_id_type=MESH_ID) for t in range(n)]
        for cp in copies:
            cp.start()
        for cp in copies:
            cp.wait()

    return pl.pallas_call(
        body, name=name, in_specs=[ANY] * n, out_specs=[ANY] * n,
        out_shape=[jax.ShapeDtypeStruct(a.shape, a.dtype) for a in arrs],
        scratch_shapes=[pltpu.SemaphoreType.DMA((n,)), pltpu.SemaphoreType.DMA((n,))],
    )(*arrs)


ELEMENTWISE_BLOCK_BYTES = 1 << 20


def _row_tile(rows, cols):
    want = max(8, ELEMENTWISE_BLOCK_BYTES // (4 * cols))
    fits = [t for t in range(8, min(rows, want) + 1, 8) if rows % t == 0]
    return fits[-1] if fits else rows


def sum_chips(recv, own, axis, chip, stacked, l, name):
    _, r, c = recv.shape
    tr = _row_tile(r, c)
    nrt = r // tr

    def body(chip_ref, r_ref, own_ref, stacked_ref, o_ref):
        for k in range(4):
            @pl.when(chip_ref[0] == k)
            def _():
                terms = [own_ref[...] if s == k else r_ref[s] for s in range(4)]
                o_ref[...] = ((terms[0] + terms[1]) + terms[2]) + terms[3]

    own_index = {0: lambda i, q: (q[0] * nrt + i, 0), 1: lambda i, q: (i, q[0]), None: lambda i, q: (i, 0)}[axis]
    return pl.pallas_call(
        body, name=name,
        grid_spec=pltpu.PrefetchScalarGridSpec(
            num_scalar_prefetch=1, grid=(nrt,),
            in_specs=[pl.BlockSpec((4, tr, c), lambda i, q: (0, i, 0)), pl.BlockSpec((tr, c), own_index), ANY],
            out_specs=pl.BlockSpec((None, tr, c), lambda i, q: (l, i, 0))),
        out_shape=jax.ShapeDtypeStruct(stacked.shape, F32), input_output_aliases={3: 0}, compiler_params=_cp(1),
    )(chip, recv, own, stacked)


def adamw(w, ga, gb, m, v, name):
    rows, cols = w.shape
    tr = _row_tile(rows, cols)

    def body(w_ref, ga_ref, gb_ref, m_ref, v_ref, g_ref, d_ref, nm_ref, nv_ref):
        g = ga_ref[...] + gb_ref[...]
        nm = ADAM_B1 * m_ref[...] + (1.0 - ADAM_B1) * g
        nv = ADAM_B2 * v_ref[...] + (1.0 - ADAM_B2) * (g * g)
        m_hat = nm / (1.0 - ADAM_B1 ** ADAM_STEP)
        v_hat = nv / (1.0 - ADAM_B2 ** ADAM_STEP)
        g_ref[...] = g
        nm_ref[...] = nm
        nv_ref[...] = nv
        d_ref[...] = -ADAM_LR * (m_hat / (jnp.sqrt(v_hat) + ADAM_EPS) + ADAM_WD * w_ref[...])

    blk = pl.BlockSpec((tr, cols), lambda i: (i, 0))
    f = jax.ShapeDtypeStruct((rows, cols), F32)
    return pl.pallas_call(
        body, name=name, grid=(rows // tr,), in_specs=[blk] * 5, out_specs=[blk] * 4, out_shape=[f] * 4,
        compiler_params=_cp(1),
    )(w, ga, gb, m, v)


PACK_ALIGN = 1024
PACK_ROWS_ALIGN = 2048


def pack_small(arrs):
    parts = []
    for a in arrs:
        flat = a.reshape(-1)
        pad = (-flat.shape[0]) % PACK_ALIGN
        parts.append(jnp.pad(flat, (0, pad)) if pad else flat)
    rows = sum(p.shape[0] for p in parts) // 128
    parts.append(jnp.zeros(((-rows) % PACK_ROWS_ALIGN * 128,), arrs[0].dtype))
    return jnp.concatenate(parts).reshape(-1, 128)


def unpack_small(packed, shapes):
    out, row = [], 0
    for shape in shapes:
        size = int(np.prod(shape))
        rows = -(-size // PACK_ALIGN) * 8
        out.append(packed[row:row + rows].reshape(-1)[:size].reshape(shape))
        row += rows
    return out


def _norm_epilogue(with_next):
    def epi(acc, res, g_post, *g_pre):
        x_new = acc * lax.rsqrt(jnp.mean(acc * acc, axis=-1, keepdims=True) + EPS) * g_post + res
        if not with_next:
            return acc, x_new
        return acc, x_new, x_new * lax.rsqrt(jnp.mean(x_new * x_new, axis=-1, keepdims=True) + EPS) * g_pre[0]
    return epi


def layer_fwd(x, h1, mem, w_in, rest_of, P, biases, g_next, after=()):
    sv = {"x0": x}
    post = dict(tm=512, tn=D_MODEL)
    proj = mm(h1, w_in, "nn", tm=1024, tn=768, tk=1024, out_dtypes=[F32], name="mm_w_in", after=after)
    a_out = pool_fwd(proj, P["pool_w"], P["pool_scale"])
    os_, lses = [], []
    for g, (win, dil) in enumerate(DIL_GROUPS):
        o, lse = att_fwd(proj, biases[g], g, dil)
        os_.append(o)
        lses.append(lse)
    b_out, w0, w1, w2 = att_combine(os_, lses)
    s5p = P["s5"]
    hr, hi, y = s5_fwd(proj, s5p[2], s5p[3], s5p[0], s5p[1], s5p[4], s5p[5], P["d_skip"])
    d_out = sgu_fwd(proj, P["sgu_ln_g"], P["sgu_ln_b"], P["w_s"], P["b_s_t"])
    W, after_rest = rest_of("mixer", d_out)
    W = dict(W, w_in=w_in)
    c_out = glu_fwd(y, W["w_glu"], P["b_glu"])
    branches = (a_out, b_out, c_out, d_out)
    merged = merge_fwd(proj, W["gate_b"], branches, W["w_up"])
    t1, x1, h2 = mm(merged, W["w_out"], "nn", tk=1024, out_dtypes=[F32, F32, BF16], name="mm_w_out", extras=(x,),
                    vecs=(P["g_mix_post"], P["g_x_pre"]), epi=_norm_epilogue(True), after=after_rest, **post)
    sv.update(h1=h1, proj=proj, os=os_, lses=lses, wts=(w0, w1, w2), hr=hr, hi=hi, y=y, branches=branches,
              merged=merged, t1=t1, x1=x1)

    mem_n = rms_fwd(mem, P["g_mem"], BF16, "rms_mem")
    q = mm(h2, W["w_cq"], "nn", tm=1024, tn=512, tk=1024, out_dtypes=[BF16], name="mm_w_cq")
    kv = mm(mem_n, W["w_ckv"], "nn", tm=256, tn=1024, tk=1024, out_dtypes=[BF16], name="mm_w_ckv")
    ox = xatt_fwd(q, kv)
    t2, x2, h3 = mm(ox, W["w_co"], "nn", tk=512, out_dtypes=[F32, F32, BF16], name="mm_w_co", extras=(x1,),
                    vecs=(P["g_x_post"], P["g_ff_pre"]), epi=_norm_epilogue(True), **post)
    sv.update(h2=h2, mem_n=mem_n, q=q, kv=kv, ox=ox, t2=t2, x2=x2)

    W_ff, after_ff = rest_of("mlp", h3)
    W = dict(W, **W_ff)
    pre, act = mm(h3, W["w_ff1"], "nn", tm=1024, tn=1024, tk=1024, out_dtypes=[F32, BF16], name="mm_w_ff1",
                  epi=lambda acc: (acc, jnp.square(jnp.maximum(acc, 0.0))), after=after_ff)
    if g_next is None:
        (ff, x3), h_next = mm(act, W["w_ff2"], "nn", tk=1024, out_dtypes=[F32, F32], name="mm_w_ff2_last", extras=(x2,),
                              vecs=(P["g_ff_post"],), epi=_norm_epilogue(False), **post), None
    else:
        ff, x3, h_next = mm(act, W["w_ff2"], "nn", tk=1024, out_dtypes=[F32, F32, BF16], name="mm_w_ff2", extras=(x2,),
                            vecs=(P["g_ff_post"], g_next), epi=_norm_epilogue(True), **post)
    sv.update(h3=h3, pre=pre, act=act, ff=ff, W=W)
    return x3, h_next, sv


def layer_bwd(dx, mem, W, P, biases, sv, headsum, emit, after=()):
    G = {}
    dff, G["g_ff_post"] = rms_bwd(sv["ff"], P["g_ff_post"], dx, BF16, "rms_post_bwd", after=after)
    G["w_ff2"] = mm(sv["act"], dff, "tn", tm=1024, tn=1024, tk=1024, out_dtypes=[F32], name="mm_dw_ff2")
    dpre = mm(dff, W["w_ff2"], "nt", tm=1024, tn=1024, tk=1024, out_dtypes=[BF16], name="mm_dact", extras=(sv["pre"],),
              epi=lambda acc, pre: (acc * (2.0 * jnp.maximum(pre, 0.0)),))
    G["w_ff1"] = mm(sv["h3"], dpre, "tn", tm=1024, tn=1024, tk=1024, out_dtypes=[F32], name="mm_dw_ff1")
    sent = emit(("w_ff1", "w_ff2"), G)
    dh3 = mm(dpre, W["w_ff1"], "nt", tm=1024, tn=1024, tk=1024, out_dtypes=[F32], name="mm_dh3", after=sent)
    dx2, G["g_ff_pre"] = rms_bwd(sv["x2"], P["g_ff_pre"], dh3, F32, "rms_pre_bwd", add=dx)
    dt2, G["g_x_post"] = rms_bwd(sv["t2"], P["g_x_post"], dx2, BF16, "rms_post_bwd")
    G["w_co"] = mm(sv["ox"], dt2, "tn", tm=512, tn=1024, tk=1024, out_dtypes=[F32], name="mm_dw_co")
    dox = mm(dt2, W["w_co"], "nt", tm=1024, tn=512, tk=1024, out_dtypes=[BF16], name="mm_dox")
    dq, dkv = xatt_bwd(sv["q"], sv["kv"], dox)
    G["w_cq"] = mm(sv["h2"], dq, "tn", tm=1024, tn=512, tk=1024, out_dtypes=[F32], name="mm_dw_cq")
    dh2 = mm(dq, W["w_cq"], "nt", tm=1024, tn=1024, tk=512, out_dtypes=[F32], name="mm_dh2")
    G["w_ckv"] = mm(sv["mem_n"], dkv, "tn", tm=1024, tn=1024, tk=256, out_dtypes=[F32], name="mm_dw_ckv")
    dmem_n = mm(dkv, W["w_ckv"], "nt", tm=256, tn=1024, tk=1024, out_dtypes=[F32], name="mm_dmem")
    _, G["g_mem"] = rms_bwd(mem, P["g_mem"], dmem_n, BF16, "rms_mem_bwd")
    dx1, G["g_x_pre"] = rms_bwd(sv["x1"], P["g_x_pre"], dh2, F32, "rms_pre_bwd", add=dx2)
    proj = sv["proj"]
    dt1, G["g_mix_post"] = rms_bwd(sv["t1"], P["g_mix_post"], dx1, BF16, "rms_post_bwd")
    G["w_out"] = mm(sv["merged"], dt1, "tn", tm=1024, tn=1024, tk=1024, out_dtypes=[F32], name="mm_dw_out")
    dmerged = mm(dt1, W["w_out"], "nt", tm=1024, tn=1024, tk=1024, out_dtypes=[F32], name="mm_dmerged")
    dgates, dups, G["gate_b"] = merge_bwd(proj, W["gate_b"], sv["branches"], W["w_up"], dmerged)
    dbr, dwup = [], []
    for i in range(4):
        dbr.append(mm(dups[i], W["w_up"][i], "nt", tm=1024, tn=512, tk=1024, out_dtypes=[F32], name="mm_dbranch"))
        dwup.append(mm(sv["branches"][i], dups[i], "tn", tm=512, tn=1024, tk=1024, out_dtypes=[F32], name="mm_dw_up"))
    G["w_up"] = jnp.concatenate(dwup, axis=0)
    d_pool, G["pool_w"], G["pool_scale"] = pool_bwd(proj, P["pool_w"], P["pool_scale"], dbr[0])
    cbar = att_combine_bwd(dbr[1], sv["os"], sv["wts"], headsum)
    dqs, dks, dvs, dbias = [], [], [], []
    for g, (win, dil) in enumerate(DIL_GROUPS):
        dq_g, dk_g, dv_g, db_g = att_bwd(proj, biases[g], sv["lses"][g], sv["wts"][g], dbr[1], cbar, g, dil)
        dqs.append(dq_g)
        dks.append(dk_g)
        dvs.append(dv_g)
        dbias.append(db_g)
    G["att_bias"] = dbias
    s5p = P["s5"]
    dy, G["w_glu"], G["b_glu"] = glu_bwd(sv["y"], W["w_glu"], P["b_glu"], dbr[2])
    d_ssm, dbre, dbim, dar, dai, dcre, dcim, G["d_skip"] = s5_bwd(
        proj, sv["hr"], sv["hi"], dy, s5p[2], s5p[3], s5p[0], s5p[1], s5p[4], s5p[5], P["d_skip"])
    G["s5"] = (dar, dai, dbre, dbim, dcre, dcim)
    dzu, dzv, G["sgu_ln_g"], G["sgu_ln_b"], G["w_s"], G["b_s_t"] = sgu_bwd(
        proj, P["sgu_ln_g"], P["sgu_ln_b"], P["w_s"], P["b_s_t"], dbr[3])
    d_qkv = [d.astype(BF16) for d in dqs + dks + dvs]
    dproj = jnp.concatenate([d_pool] + d_qkv + [d_ssm, dzu, dzv] + list(dgates), axis=1)
    sent = emit(("gate_b", "w_glu", "w_up", "w_out", "w_cq", "w_ckv", "w_co"), G)
    G["w_in"] = mm(sv["h1"], dproj, "tn", tm=1024, tn=1536, tk=1024, out_dtypes=[F32], name="mm_dw_in", after=sent)
    sent = emit(("w_in",), G)
    dh1 = mm(dproj, W["w_in"], "nt", tm=1024, tn=1024, tk=1536, out_dtypes=[F32], name="mm_dh1", after=sent)
    dx0, G["g_mix_pre"] = rms_bwd(sv["x0"], P["g_mix_pre"], dh1, F32, "rms_pre_bwd", add=dx1)
    return dx0, G


def _as3d(name, a):
    shape2d, axis = SHARDED[name]
    rows, cols = shape2d
    if axis == 0:
        rows //= 4
    else:
        cols //= 4
    return a.reshape(DEPTH, rows, cols)


def kernel(x, mem, rel_bias, g_mix_pre, g_mix_post, w_in, gate_b, pool_w, pool_scale, a_re, a_im, log_dt, b_re, b_im, c_re, c_im, d_skip, w_glu, b_glu, sgu_ln_g, sgu_ln_b, w_s, b_s, w_up, w_out, g_x_pre, g_x_post, g_mem, w_cq, w_ckv, w_co, g_ff_pre, g_ff_post, w_ff1, w_ff2, loss_target, m_rel_bias, m_g_mix_pre, m_g_mix_post, m_w_in, m_gate_b, m_pool_w, m_pool_scale, m_a_re, m_a_im, m_log_dt, m_b_re, m_b_im, m_c_re, m_c_im, m_d_skip, m_w_glu, m_b_glu, m_sgu_ln_g, m_sgu_ln_b, m_w_s, m_b_s, m_w_up, m_w_out, m_g_x_pre, m_g_x_post, m_g_mem, m_w_cq, m_w_ckv, m_w_co, m_g_ff_pre, m_g_ff_post, m_w_ff1, m_w_ff2, v_rel_bias, v_g_mix_pre, v_g_mix_post, v_w_in, v_gate_b, v_pool_w, v_pool_scale, v_a_re, v_a_im, v_log_dt, v_b_re, v_b_im, v_c_re, v_c_im, v_d_skip, v_w_glu, v_b_glu, v_sgu_ln_g, v_sgu_ln_b, v_w_s, v_b_s, v_w_up, v_w_out, v_g_x_pre, v_g_x_post, v_g_mem, v_w_cq, v_w_ckv, v_w_co, v_g_ff_pre, v_g_ff_post, v_w_ff1, v_w_ff2):
    env = dict(locals())
    weights = {n: env[n] for n in WEIGHT_NAMES}
    mom_m = {n: env["m_" + n] for n in WEIGHT_NAMES}
    mom_v = {n: env["v_" + n] for n in WEIGHT_NAMES}
    x2d = x.reshape(x.shape[1], D_MODEL)
    mem2d = mem.reshape(N_MEM, D_MODEL)
    target = loss_target.reshape(x2d.shape)

    axis_of = {n: SHARDED[n][1] for n in SHARDED_NAMES}
    chip = _chip_index().astype(jnp.int32).reshape(1)
    rest_names = [n for n in SHARDED_NAMES if n != "w_in"]
    mlp_names = ["w_ff1", "w_ff2"]

    def gather_start(l, names, tag, after=()):
        shards = [_as3d(n, weights[n])[l].astype(F32 if n == "gate_b" else MXU_DTYPE) for n in names]
        ax = [axis_of[n] for n in names]
        lands = [jnp.concatenate([s] * 4, axis=a) for s, a in zip(shards, ax)]
        return (names, ax, tag) + chips_start(shards, lands, ax, "gather", f"gather_start_{tag}", after=after)

    def gather_wait(started, after):
        names, ax, tag, send, recv, shards, lands, _ = started
        _, lands = chips_wait(send, recv, shards, lands, ax, "gather", f"gather_wait_{tag}", after=after)
        W = dict(zip(names, lands))
        if "w_up" in W:
            W["w_up"] = W["w_up"].reshape(4, W_BRANCH, D_MODEL)
        return W

    biases = [att_bias(rel_bias, g, dil) for g, (_, dil) in enumerate(DIL_GROUPS)]
    lanes = np.arange(W_BRANCH) // ATT_HEAD_DIM
    headsum = jnp.asarray(lanes[:, None] == lanes[None, :], dtype=BF16)
    small = [pack_small([d[n] for n in REPLICATED_NAMES]) for d in (weights, mom_m, mom_v)]

    def small_params(l, s5_prepared):
        vec = lambda a: a[l].reshape(1, -1)
        return {
            "g_mix_pre": vec(g_mix_pre), "g_mix_post": vec(g_mix_post), "g_x_pre": vec(g_x_pre), "g_x_post": vec(g_x_post),
            "g_mem": vec(g_mem), "g_ff_pre": vec(g_ff_pre), "g_ff_post": vec(g_ff_post), "pool_w": pool_w[l],
            "pool_scale": vec(pool_scale), "d_skip": vec(d_skip), "b_glu": vec(b_glu), "sgu_ln_g": vec(sgu_ln_g),
            "sgu_ln_b": vec(sgu_ln_b), "w_s": w_s[l], "b_s_t": b_s[l].T, "s5": s5_prepared,
        }

    Ws, Ps, saved, s5_vjps = [], [], [], []
    xl = x2d
    hl = rms_fwd(x2d, g_mix_pre[0].reshape(1, -1), BF16, "rms_pre")
    flying = {"next": gather_start(0, ["w_in"], "0_w_in")}
    for l in range(DEPTH):
        s5_prepared, s5_vjp = jax.vjp(s5_params, a_re[l], a_im[l], log_dt[l], b_re[l], b_im[l], c_re[l], c_im[l])
        token_of = lambda started: (started[7],)
        if l == 0:
            w_in_l = gather_wait(flying["next"], [*biases, *small, hl])["w_in"]
            flying["mixer"] = gather_start(0, [n for n in rest_names if n not in mlp_names], "0_mixer", after=[w_in_l])
            first_after = token_of(flying["mixer"])

            def rest_of(stage, value):
                if stage == "mixer":
                    W = gather_wait(flying["mixer"], [value])
                    flying["mlp"] = gather_start(0, mlp_names, "0_mlp", after=[W["w_out"]])
                    return W, token_of(flying["mlp"])
                W = gather_wait(flying["mlp"], [value])
                flying["next"] = gather_start(1, SHARDED_NAMES, "1", after=[W["w_ff1"]])
                return W, token_of(flying["next"])
        else:
            W_l = gather_wait(flying["next"], [xl])
            w_in_l, first_after = W_l["w_in"], ()
            if l + 1 < DEPTH:
                flying["next"] = gather_start(l + 1, SHARDED_NAMES, str(l + 1), after=[w_in_l])
                first_after = token_of(flying["next"])
            rest_of = lambda stage, value, W_l=W_l: (W_l if stage == "mixer" else {}, ())
        P = small_params(l, s5_prepared)
        g_next = g_mix_pre[l + 1].reshape(1, -1) if l + 1 < DEPTH else None
        xl, hl, sv = layer_fwd(xl, hl, mem2d, w_in_l, rest_of, P, biases, g_next, after=first_after)
        Ws.append(sv["W"])
        Ps.append(P)
        saved.append(sv)
        s5_vjps.append(s5_vjp)
    loss_local, dx = loss_and_grad(xl, target)
    loss = lax.psum(loss_local, ("x", "y", "c"))

    scattered = []

    def scatter_start(l, names, srcs):
        ax = [axis_of.get(n) for n in names]
        lands = []
        for s, a in zip(srcs, ax):
            r, c = s.shape
            lands.append(lax.empty((4, r // 4 if a == 0 else r, c // 4 if a == 1 else c), F32))
        tag = f"{l}_{names[0]}"
        send, recv, srcs, lands, token = chips_start(srcs, lands, ax, "scatter", f"grads_start_{tag}")
        scattered.append((l, names, ax, tag, send, recv, srcs, lands))
        return (token,)

    grads = [None] * DEPTH
    for l in reversed(range(DEPTH)):
        emit = lambda names, G, l=l: scatter_start(l, list(names), [G[n] for n in names])
        dx, grads[l] = layer_bwd(dx, mem2d, Ws[l], Ps[l], biases, saved[l], headsum, emit)
    grad_x = dx.reshape(x.shape)

    rep = {}
    stack = lambda key, shape: jnp.stack([grads[l][key] for l in range(DEPTH)]).reshape(shape)
    for n in ("g_mix_pre", "g_mix_post", "g_x_pre", "g_x_post", "g_mem", "g_ff_pre", "g_ff_post"):
        rep[n] = stack(n, (DEPTH, D_MODEL))
    for n in ("pool_scale", "d_skip", "b_glu", "sgu_ln_g", "sgu_ln_b"):
        rep[n] = stack(n, (DEPTH, W_BRANCH))
    rep["pool_w"] = stack("pool_w", pool_w.shape)
    rep["w_s"] = stack("w_s", w_s.shape)
    rep["b_s"] = jnp.stack([grads[l]["b_s_t"][:, :4].T for l in range(DEPTH)])
    s5_grads = [s5_vjps[l](tuple(grads[l]["s5"])) for l in range(DEPTH)]
    for i, n in enumerate(("a_re", "a_im", "log_dt", "b_re", "b_im", "c_re", "c_im")):
        rep[n] = jnp.stack([s5_grads[l][i] for l in range(DEPTH)])
    dbias = [sum(grads[l]["att_bias"][g] for l in range(DEPTH)) for g in range(len(DIL_GROUPS))]
    rep["rel_bias"] = jnp.concatenate([att_bias_grad(dbias[g], dil) for g, (_, dil) in enumerate(DIL_GROUPS)], axis=1)
    rep_shapes = [weights[n].shape for n in REPLICATED_NAMES]
    packed_g = pack_small([rep[n] for n in REPLICATED_NAMES])

    small_sent = scatter_start(0, ["small"], [packed_g])
    stacked = {}

    def collect(record, after):
        l, names, ax, tag, send, recv, srcs, lands = record
        srcs, lands = chips_wait(send, recv, srcs, lands, ax, "scatter", f"grads_wait_{tag}", after=after)
        for n, own, arrived, a in zip(names, srcs, lands, ax):
            if n not in stacked:
                stacked[n] = lax.empty((1 if n == "small" else DEPTH,) + arrived.shape[1:], F32)
            stacked[n] = sum_chips(arrived, own, a, chip, stacked[n], 0 if n == "small" else l, "sum_chips")

    out_g, out_d, out_m, out_v = {}, {}, {}, {}

    def update(names, tag):
        partial = [stacked[n].reshape(-1, stacked[n].shape[-1]) for n in names]
        other = swap_cores(partial, f"swap_cores_{tag}")
        for n, mine, theirs in zip(names, partial, other):
            if n == "small":
                res = adamw(small[0], mine, theirs, small[1], small[2], "adamw")
                for d, r in zip((out_g, out_d, out_m, out_v), res):
                    d.update(zip(REPLICATED_NAMES, unpack_small(r, rep_shapes)))
            else:
                flat = lambda a: a.reshape(mine.shape)
                res = adamw(flat(weights[n]), mine, theirs, flat(mom_m[n]), flat(mom_v[n]), "adamw")
                out_g[n], out_d[n], out_m[n], out_v[n] = [r.reshape(weights[n].shape) for r in res]

    late = [r for r in scattered if r[1] == ["small"] or (r[0] == 0 and r[1] == ["w_in"])]
    for record in scattered:
        if not any(record is r for r in late):
            collect(record, [dx, *small_sent])
    update(rest_names, "rest")
    collect(late[0], [out_d[n] for n in rest_names])
    update(["w_in"], "w_in")
    collect(late[1], [out_d["w_in"]])
    update(["small"], "small")

    return (loss, grad_x, *[out_g[n] for n in WEIGHT_NAMES], *[out_d[n] for n in WEIGHT_NAMES],
            *[out_m[n] for n in WEIGHT_NAMES], *[out_v[n] for n in WEIGHT_NAMES])
```

```python
import functools
import math

import numpy as np
import jax
import jax.numpy as jnp
from jax import lax
from jax.experimental import pallas as pl
from jax.experimental.pallas import tpu as pltpu

F32 = jnp.float32
BF16 = jnp.bfloat16
MXU_DTYPE = jnp.bfloat16
MESH_ID = pl.DeviceIdType.MESH
VMEM_LIMIT_BYTES = 56 * 1024 * 1024

D_MODEL = 1024
DEPTH = 4
N_MEM = 256
W_BRANCH = 512
POOL_WINDOWS = (2, 4, 8, 16)
POOL_HALO = 16
DIL_GROUPS = ((128, 1), (512, 4), (2048, 16))
BAND = 128
ATT_HEADS = 8
ATT_HEAD_DIM = 64
SSM_GROUP = 16
SSM_GROUPS = 32
SSM_STATE = 64
SSM_COLS = SSM_GROUPS * SSM_STATE
SSM_T = 512
SGU_CHUNK = 128
X_HEADS = 4
X_HEAD_DIM = 128
D_FF = 4096
REL_BUCKETS = 32
REL_MAX_DIST = 2048
EPS = 1e-6
NEG_INF = -1e30
OFF_POOL = 0
OFF_ATT = 512
OFF_SSM = OFF_ATT + 9 * W_BRANCH
OFF_SGU = OFF_SSM + W_BRANCH
OFF_GATE = OFF_SGU + 2 * W_BRANCH
IN_WIDTH = OFF_GATE + 4 * D_MODEL

ADAM_LR = 0.001
ADAM_B1 = 0.9
ADAM_B2 = 0.999
ADAM_EPS = 1e-08
ADAM_WD = 0.01
ADAM_STEP = 10

GELU_C = math.sqrt(2.0 / math.pi)

WEIGHT_NAMES = ['rel_bias', 'g_mix_pre', 'g_mix_post', 'w_in', 'gate_b', 'pool_w', 'pool_scale', 'a_re', 'a_im',
                'log_dt', 'b_re', 'b_im', 'c_re', 'c_im', 'd_skip', 'w_glu', 'b_glu', 'sgu_ln_g', 'sgu_ln_b',
                'w_s', 'b_s', 'w_up', 'w_out', 'g_x_pre', 'g_x_post', 'g_mem', 'w_cq', 'w_ckv', 'w_co',
                'g_ff_pre', 'g_ff_post', 'w_ff1', 'w_ff2']
SHARDED = {
    'w_in': ((D_MODEL, IN_WIDTH), 1),
    'gate_b': ((4, D_MODEL), 1),
    'w_glu': ((W_BRANCH, W_BRANCH), 0),
    'w_up': ((4 * W_BRANCH, D_MODEL), 1),
    'w_out': ((D_MODEL, D_MODEL), 0),
    'w_cq': ((D_MODEL, W_BRANCH), 0),
    'w_ckv': ((D_MODEL, D_MODEL), 0),
    'w_co': ((W_BRANCH, D_MODEL), 1),
    'w_ff1': ((D_MODEL, D_FF), 1),
    'w_ff2': ((D_FF, D_MODEL), 0),
}
SHARDED_NAMES = list(SHARDED)
REPLICATED_NAMES = [n for n in WEIGHT_NAMES if n not in SHARDED]


def _cp(n_axes):
    return pltpu.CompilerParams(dimension_semantics=("arbitrary",) * n_axes, vmem_limit_bytes=VMEM_LIMIT_BYTES)


def _dot(a, b, dims="nn"):
    cd = {"nn": ((1,), (0,)), "nt": ((1,), (1,)), "tn": ((0,), (0,))}[dims]
    return lax.dot_general(a.astype(MXU_DTYPE), b.astype(MXU_DTYPE), (cd, ((), ())), preferred_element_type=F32)


def _gelu(x):
    return 0.5 * x * (1.0 + jnp.tanh(GELU_C * (x + 0.044715 * (x * x * x))))


def _gelu_grad(x):
    t = jnp.tanh(GELU_C * (x + 0.044715 * (x * x * x)))
    return 0.5 * (1.0 + t) + 0.5 * x * (1.0 - t * t) * (GELU_C * (1.0 + 3.0 * 0.044715 * (x * x)))


def _sigmoid(x):
    return 1.0 / (1.0 + jnp.exp(-x))


def mm(a, b, dims, *, tm, tn, tk, out_dtypes, name, extras=(), vecs=(), epi=None, n_sums=0, after=()):
    if dims == "tn":
        K, M = a.shape
        N = b.shape[1]
    else:
        M, K = a.shape
        N = b.shape[1] if dims == "nn" else b.shape[0]
    tm, tn, tk = min(tm, M), min(tn, N), min(tk, K)
    assert M % tm == 0 and N % tn == 0 and K % tk == 0, (name, M, N, K, tm, tn, tk)
    assert n_sums == 0 or tn == N, name
    nk = K // tk
    ne, no = len(extras) + len(vecs), len(out_dtypes)
    if epi is None:
        epi = lambda acc: (acc,)
    a_spec = (pl.BlockSpec((tk, tm), lambda i, j, k: (k, i)) if dims == "tn"
              else pl.BlockSpec((tm, tk), lambda i, j, k: (i, k)))
    b_spec = (pl.BlockSpec((tn, tk), lambda i, j, k: (j, k)) if dims == "nt"
              else pl.BlockSpec((tk, tn), lambda i, j, k: (k, j)))
    mn_spec = pl.BlockSpec((tm, tn), lambda i, j, k: (i, j))
    vec_spec = pl.BlockSpec((1, tn), lambda i, j, k: (0, j))

    def body(a_ref, b_ref, *rest):
        first_out = ne + len(after)
        extra_refs, out_refs = rest[:ne], rest[first_out:first_out + no]
        sum_refs = rest[first_out + no:first_out + no + n_sums]
        part = _dot(a_ref[...], b_ref[...], dims)

        def finish(acc):
            results = epi(acc, *[e[...] for e in extra_refs])
            for o_ref, r in zip(out_refs, results[:no]):
                o_ref[...] = r.astype(o_ref.dtype)
            for s_ref, r in zip(sum_refs, results[no:]):
                @pl.when(pl.program_id(0) == 0)
                def _():
                    s_ref[...] = r

                @pl.when(pl.program_id(0) > 0)
                def _():
                    s_ref[...] += r

        if nk == 1:
            finish(part)
        else:
            acc_ref = rest[-1]
            k = pl.program_id(2)

            @pl.when(k == 0)
            def _():
                acc_ref[...] = part

            @pl.when(k > 0)
            def _():
                acc_ref[...] += part

            @pl.when(k == nk - 1)
            def _():
                finish(acc_ref[...])

    outs = pl.pallas_call(
        body, name=name, grid=(M // tm, N // tn, nk),
        in_specs=[a_spec, b_spec] + [mn_spec] * len(extras) + [vec_spec] * len(vecs) + [ANY] * len(after),
        out_specs=[mn_spec] * no + [vec_spec] * n_sums,
        out_shape=[jax.ShapeDtypeStruct((M, N), dt) for dt in out_dtypes] + [jax.ShapeDtypeStruct((1, N), F32)] * n_sums,
        scratch_shapes=[pltpu.VMEM((tm, tn), F32)] if nk > 1 else [],
        compiler_params=_cp(3),
    )(a, b, *extras, *vecs, *after)
    return outs[0] if no + n_sums == 1 else outs


ROW_TILE = 512


def rms_fwd(x, g, out_dtype, name, res=None):
    M, D = x.shape
    tm = min(ROW_TILE, M)

    def body(x_ref, g_ref, *rest):
        o_ref = rest[-1]
        xf = x_ref[...]
        y = xf * lax.rsqrt(jnp.mean(xf * xf, axis=-1, keepdims=True) + EPS) * g_ref[...]
        if res is not None:
            y = y + rest[0][...]
        o_ref[...] = y.astype(o_ref.dtype)

    row = pl.BlockSpec((tm, D), lambda i: (i, 0))
    return pl.pallas_call(
        body, name=name, grid=(M // tm,),
        in_specs=[row, pl.BlockSpec((1, D), lambda i: (0, 0))] + ([row] if res is not None else []),
        out_specs=row, out_shape=jax.ShapeDtypeStruct((M, D), out_dtype), compiler_params=_cp(1),
    )(x, g, *([res] if res is not None else []))


def rms_bwd(x, g, dy, dx_dtype, name, add=None, after=()):
    M, D = x.shape
    tm = min(ROW_TILE, M)

    def body(x_ref, g_ref, dy_ref, *rest):
        dx_ref, dg_ref = rest[-2], rest[-1]
        xf = x_ref[...]
        dyf = dy_ref[...].astype(F32)
        r = lax.rsqrt(jnp.mean(xf * xf, axis=-1, keepdims=True) + EPS)
        xn = xf * r
        dxn = dyf * g_ref[...]
        dx = r * (dxn - xn * jnp.mean(dxn * xn, axis=-1, keepdims=True))
        if add is not None:
            dx = dx + rest[0][...]
        dx_ref[...] = dx.astype(dx_ref.dtype)
        dg = jnp.sum(dyf * xn, axis=0, keepdims=True)

        @pl.when(pl.program_id(0) == 0)
        def _():
            dg_ref[...] = dg

        @pl.when(pl.program_id(0) > 0)
        def _():
            dg_ref[...] += dg

    row = pl.BlockSpec((tm, D), lambda i: (i, 0))
    vec = pl.BlockSpec((1, D), lambda i: (0, 0))
    return pl.pallas_call(
        body, name=name, grid=(M // tm,),
        in_specs=[row, vec, row] + ([row] if add is not None else []) + [ANY] * len(after),
        out_specs=[row, vec],
        out_shape=[jax.ShapeDtypeStruct((M, D), dx_dtype), jax.ShapeDtypeStruct((1, D), F32)],
        compiler_params=_cp(1),
    )(x, g, dy, *([add] if add is not None else []), *after)


def loss_and_grad(y, target):
    M, D = y.shape
    tm = ROW_TILE

    def body(y_ref, t_ref, part_ref, dy_ref):
        e = y_ref[...] - t_ref[...]
        dy_ref[...] = e / D
        part_ref[...] = jnp.broadcast_to(0.5 * jnp.sum(jnp.mean(e * e, axis=-1, keepdims=True), axis=0, keepdims=True),
                                         (8, 128))

    row = pl.BlockSpec((tm, D), lambda i: (i, 0))
    part, dy = pl.pallas_call(
        body, name="loss", grid=(M // tm,), in_specs=[row, row],
        out_specs=[pl.BlockSpec((8, 128), lambda i: (i, 0)), row],
        out_shape=[jax.ShapeDtypeStruct((8 * (M // tm), 128), F32), jax.ShapeDtypeStruct((M, D), F32)],
        compiler_params=_cp(1),
    )(y, target)
    return jnp.sum(part[::8, 0]), dy


POOL_ROWS = 512


def _pool_window_sum(xw, gi, roll_of):
    s1 = xw + pltpu.roll(xw, roll_of(1), 0)
    s2 = s1 + pltpu.roll(s1, roll_of(2), 0)
    s3 = s2 + pltpu.roll(s2, roll_of(4), 0)
    s4 = s3 + pltpu.roll(s3, roll_of(8), 0)
    return jnp.where(gi == 0, s1, jnp.where(gi == 1, s2, jnp.where(gi == 2, s3, s4)))


def _pool_cnt(i, gi):
    rows = lax.broadcasted_iota(jnp.int32, (POOL_ROWS, 128), 0) + i * POOL_ROWS
    w = jnp.where(gi == 0, 2, jnp.where(gi == 1, 4, jnp.where(gi == 2, 8, 16)))
    return jnp.minimum(rows + 1, w).astype(F32)


def pool_fwd(proj, pool_w, scale):
    S = proj.shape[0]
    nchunk = S // POOL_ROWS
    slab = POOL_ROWS + POOL_HALO

    def body(x_ref, w_ref, sc_ref, o_ref, pad_ref):
        gi = pl.program_id(0)
        pad_ref[0:POOL_HALO, :] = jnp.zeros((POOL_HALO, 128), F32)
        pad_ref[POOL_HALO:, :] = x_ref[...]
        for i in range(nchunk):
            xw = pad_ref[i * POOL_ROWS:i * POOL_ROWS + slab, :]
            ssum = _pool_window_sum(xw, gi, lambda d: d)[POOL_HALO:, :]
            p = ssum / _pool_cnt(i, gi) - xw[POOL_HALO:, :]
            o_ref[i * POOL_ROWS:(i + 1) * POOL_ROWS, :] = (_dot(p, w_ref[...]) * sc_ref[...]).astype(o_ref.dtype)

    return pl.pallas_call(
        body, name="pool_fwd", grid=(4,),
        in_specs=[pl.BlockSpec((S, 128), lambda g: (0, OFF_POOL // 128 + g)),
                  pl.BlockSpec((None, 128, 128), lambda g: (g, 0, 0)),
                  pl.BlockSpec((1, 128), lambda g: (0, g))],
        out_specs=pl.BlockSpec((S, 128), lambda g: (0, g)),
        out_shape=jax.ShapeDtypeStruct((S, W_BRANCH), BF16),
        scratch_shapes=[pltpu.VMEM((S + POOL_HALO, 128), F32)],
        compiler_params=_cp(1),
    )(proj, pool_w, scale)


def pool_bwd(proj, pool_w, scale, dy):
    S = proj.shape[0]
    nchunk = S // POOL_ROWS
    slab = POOL_ROWS + POOL_HALO

    def body(x_ref, w_ref, sc_ref, dy_ref, dx_ref, dw_ref, dsc_ref, pad_ref, pad2_ref, dp_ref):
        gi = pl.program_id(0)
        pad_ref[0:POOL_HALO, :] = jnp.zeros((POOL_HALO, 128), F32)
        pad_ref[POOL_HALO:, :] = x_ref[...]
        pad2_ref[S:, :] = jnp.zeros((POOL_HALO, 128), F32)
        dw = jnp.zeros((128, 128), F32)
        dsc = jnp.zeros((1, 128), F32)
        for i in range(nchunk):
            xw = pad_ref[i * POOL_ROWS:i * POOL_ROWS + slab, :]
            cnt = _pool_cnt(i, gi)
            p = _pool_window_sum(xw, gi, lambda d: d)[POOL_HALO:, :] / cnt - xw[POOL_HALO:, :]
            dyc = dy_ref[i * POOL_ROWS:(i + 1) * POOL_ROWS, :]
            dsc = dsc + jnp.sum(dyc * _dot(p, w_ref[...]), axis=0, keepdims=True)
            dys = dyc * sc_ref[...]
            dw = dw + _dot(p, dys, "tn")
            dp = _dot(dys, w_ref[...], "nt")
            dp_ref[i * POOL_ROWS:(i + 1) * POOL_ROWS, :] = dp
            pad2_ref[i * POOL_ROWS:(i + 1) * POOL_ROWS, :] = dp / cnt
        dw_ref[...] = dw
        dsc_ref[...] = dsc
        for i in range(nchunk):
            xw = pad2_ref[i * POOL_ROWS:i * POOL_ROWS + slab, :]
            fsum = _pool_window_sum(xw, gi, lambda d: slab - d)[:POOL_ROWS, :]
            rows = slice(i * POOL_ROWS, (i + 1) * POOL_ROWS)
            dx_ref[rows, :] = (fsum - dp_ref[rows, :]).astype(dx_ref.dtype)

    return pl.pallas_call(
        body, name="pool_bwd", grid=(4,),
        in_specs=[pl.BlockSpec((S, 128), lambda g: (0, OFF_POOL // 128 + g)),
                  pl.BlockSpec((None, 128, 128), lambda g: (g, 0, 0)),
                  pl.BlockSpec((1, 128), lambda g: (0, g)),
                  pl.BlockSpec((S, 128), lambda g: (0, g))],
        out_specs=[pl.BlockSpec((S, 128), lambda g: (0, g)),
                   pl.BlockSpec((None, 128, 128), lambda g: (g, 0, 0)),
                   pl.BlockSpec((1, 128), lambda g: (0, g))],
        out_shape=[jax.ShapeDtypeStruct((S, W_BRANCH), BF16), jax.ShapeDtypeStruct((4, 128, 128), F32),
                   jax.ShapeDtypeStruct((1, W_BRANCH), F32)],
        scratch_shapes=[pltpu.VMEM((S + POOL_HALO, 128), F32), pltpu.VMEM((S + POOL_HALO, 128), F32),
                        pltpu.VMEM((S, 128), F32)],
        compiler_params=_cp(1),
    )(proj, pool_w, scale, dy)


def _t5_bucket(n):
    exact = REL_BUCKETS // 2
    nf = np.maximum(n, 1).astype(np.float32)
    large = exact + (np.log(nf / exact) / np.log(REL_MAX_DIST / exact) * (REL_BUCKETS - exact)).astype(np.int32)
    large = np.minimum(large, REL_BUCKETS - 1)
    return np.where(n < exact, n, large).astype(np.int32)


def _band_onehot(dil):
    i = np.arange(BAND)[:, None]
    kk = np.arange(2 * BAND)[None, :]
    dist = BAND + i - kk
    local = (dist >= 0) & (dist <= BAND)
    bucket = _t5_bucket(np.clip(dist, 0, BAND) * dil)
    onehot = (bucket.reshape(-1, 1) == np.arange(REL_BUCKETS)[None, :]).astype(np.float32)
    return onehot, local


def att_bias(rel_bias, g, dil):
    onehot, local = _band_onehot(dil)
    tab = jnp.dot(jnp.asarray(onehot), rel_bias[:, g * ATT_HEADS:(g + 1) * ATT_HEADS], precision=lax.Precision.HIGHEST)
    bias = tab.reshape(BAND, 2 * BAND, ATT_HEADS).transpose(2, 0, 1)
    return jnp.where(jnp.asarray(local)[None], bias, NEG_INF)


def att_bias_grad(dbias, dil):
    onehot, _ = _band_onehot(dil)
    flat = dbias.transpose(1, 2, 0).reshape(BAND * 2 * BAND, ATT_HEADS)
    return jnp.dot(jnp.asarray(onehot).T, flat, precision=lax.Precision.HIGHEST)


def _head_lanes():
    return lax.broadcasted_iota(jnp.int32, (BAND, 128), 1) < ATT_HEAD_DIM


def _att_cols(part, g, hp):
    return (OFF_ATT + part * 3 * W_BRANCH + g * W_BRANCH) // 128 + hp


def _att_pair(q, k, v, bias, lse_b, do, delta_b, hh, head0, mask=None):
    sel = head0 if hh == 0 else jnp.logical_not(head0)
    s = _dot(jnp.where(sel, q, 0.0), k, "nt") * (ATT_HEAD_DIM ** -0.5) + bias
    if mask is not None:
        s = jnp.where(mask, NEG_INF, s)
    c = hh * ATT_HEAD_DIM
    p = jnp.exp(s - lse_b[:, c:c + 1])
    dp = _dot(jnp.where(sel, do, 0.0), v, "nt")
    return p, p * (dp - delta_b[:, c:c + 1])


ATT_BLOCKS = {1: 8, 4: 2, 16: 1}


def _att_rows(r, i, d):
    return pl.ds(r + d * BAND * i, BAND, stride=d) if d > 1 else pl.ds(BAND * i, BAND)


def _att_specs(g, d, nq):
    ch, pb = BAND * d * nq, BAND * d
    cur = lambda part: pl.BlockSpec((ch, 128), lambda hp, n: (n, _att_cols(part, g, hp)))
    prev = lambda part: pl.BlockSpec((pb, 128), lambda hp, n: (jnp.maximum(n * nq - 1, 0), _att_cols(part, g, hp)))
    return [cur(0), cur(1), prev(1), cur(2), prev(2)]


def _att_keys(cur_ref, prev_ref, r, i, d):
    before = cur_ref[_att_rows(r, i - 1, d), :] if i > 0 else prev_ref[_att_rows(r, 0, d), :]
    return jnp.concatenate([before, cur_ref[_att_rows(r, i, d), :]], axis=0).astype(MXU_DTYPE)


def att_fwd(proj, bias, g, d):
    S = proj.shape[0]
    nq = ATT_BLOCKS[d]
    ch = BAND * d * nq

    def body(q_ref, kc_ref, kp_ref, vc_ref, vp_ref, b_ref, o_ref, l_ref):
        n = pl.program_id(1)
        head0 = _head_lanes()
        first = jnp.logical_and(lax.broadcasted_iota(jnp.int32, (BAND, 2 * BAND), 1) < BAND, n == 0)
        for r in range(d):
            for i in range(nq):
                rows = _att_rows(r, i, d)
                q = q_ref[rows, :]
                k = _att_keys(kc_ref, kp_ref, r, i, d)
                v = _att_keys(vc_ref, vp_ref, r, i, d)
                o_h, l_h = [], []
                for hh in range(2):
                    qm = jnp.where(head0 if hh == 0 else jnp.logical_not(head0), q, 0.0)
                    s = _dot(qm, k, "nt") * (ATT_HEAD_DIM ** -0.5) + b_ref[hh]
                    if i == 0:
                        s = jnp.where(first, NEG_INF, s)
                    m = jnp.max(s, axis=-1, keepdims=True)
                    p = jnp.exp(s - m)
                    l = jnp.sum(p, axis=-1, keepdims=True)
                    o_h.append(_dot(p / l, v))
                    l_h.append(jnp.broadcast_to(m + jnp.log(l), (BAND, 128)))
                o_ref[rows, :] = jnp.where(head0, o_h[0], o_h[1])
                l_ref[rows, :] = jnp.where(head0, l_h[0], l_h[1])

    out = pl.BlockSpec((ch, 128), lambda hp, n: (n, hp))
    return pl.pallas_call(
        body, name=f"att_fwd_d{d}", grid=(4, S // ch),
        in_specs=_att_specs(g, d, nq) + [pl.BlockSpec((2, BAND, 2 * BAND), lambda hp, n: (hp, 0, 0))],
        out_specs=[out, out],
        out_shape=[jax.ShapeDtypeStruct((S, W_BRANCH), F32), jax.ShapeDtypeStruct((S, W_BRANCH), F32)],
        compiler_params=_cp(2),
    )(proj, proj, proj, proj, proj, bias)


def att_bwd(proj, bias, lse, wts, dout, cbar, g, d):
    S = proj.shape[0]
    nq = ATT_BLOCKS[d]
    ch, pb = BAND * d * nq, BAND * d
    nb = S // ch
    scale = ATT_HEAD_DIM ** -0.5

    def body(q_ref, kc_ref, kp_ref, vc_ref, vp_ref, b_ref, l_ref, w_ref, do_ref, cb_ref,
             dq_ref, dk_ref, dv_ref, ek_ref, ev_ref, db_ref):
        n = pl.program_id(1)
        head0 = _head_lanes()
        first = jnp.logical_and(lax.broadcasted_iota(jnp.int32, (BAND, 2 * BAND), 1) < BAND, n == 0)

        @pl.when(n == 0)
        def _():
            db_ref[...] = jnp.zeros(db_ref.shape, F32)

        for r in range(d):
            own_k = own_v = None
            for i in range(nq):
                rows = _att_rows(r, i, d)
                q = q_ref[rows, :]
                k = _att_keys(kc_ref, kp_ref, r, i, d)
                v = _att_keys(vc_ref, vp_ref, r, i, d)
                w = w_ref[rows, :]
                do = w * do_ref[rows, :]
                delta = w * cb_ref[rows, :]
                lse_b = l_ref[rows, :]
                dq_h, dk_h, dv_h = [], [], []
                for hh in range(2):
                    p, ds = _att_pair(q, k, v, b_ref[hh], lse_b, do, delta, hh, head0, mask=first if i == 0 else None)
                    db_ref[hh] += ds
                    ds = ds * scale
                    dq_h.append(_dot(ds, k))
                    dk_h.append(_dot(ds, q, "tn"))
                    dv_h.append(_dot(p, do, "tn"))
                dq_ref[rows, :] = jnp.where(head0, dq_h[0], dq_h[1])
                head0_keys = jnp.concatenate([head0, head0], axis=0)
                dk2 = jnp.where(head0_keys, dk_h[0], dk_h[1])
                dv2 = jnp.where(head0_keys, dv_h[0], dv_h[1])
                if i == 0:
                    ek_ref[_att_rows(r, 0, d), :] = dk2[:BAND]
                    ev_ref[_att_rows(r, 0, d), :] = dv2[:BAND]
                else:
                    dk_ref[_att_rows(r, i - 1, d), :] = own_k + dk2[:BAND]
                    dv_ref[_att_rows(r, i - 1, d), :] = own_v + dv2[:BAND]
                own_k, own_v = dk2[BAND:], dv2[BAND:]
            dk_ref[_att_rows(r, nq - 1, d), :] = own_k
            dv_ref[_att_rows(r, nq - 1, d), :] = own_v

    cur = pl.BlockSpec((ch, 128), lambda hp, n: (n, hp))
    edge = pl.BlockSpec((pb, 128), lambda hp, n: (n, hp))
    bias_spec = pl.BlockSpec((2, BAND, 2 * BAND), lambda hp, n: (hp, 0, 0))
    big = jax.ShapeDtypeStruct((S, W_BRANCH), F32)
    small = jax.ShapeDtypeStruct((nb * pb, W_BRANCH), F32)
    dq, dk, dv, ek, ev, db = pl.pallas_call(
        body, name=f"att_bwd_d{d}", grid=(4, nb),
        in_specs=_att_specs(g, d, nq) + [bias_spec, cur, cur, cur, cur],
        out_specs=[cur, cur, cur, edge, edge, bias_spec],
        out_shape=[big, big, big, small, small, jax.ShapeDtypeStruct((ATT_HEADS, BAND, 2 * BAND), F32)],
        compiler_params=_cp(2),
    )(proj, proj, proj, proj, proj, bias, lse, wts, dout, cbar)

    def with_edges(main, edges):
        if nb == 1:
            return main
        main = main.reshape(nb, ch, W_BRANCH)
        add = jnp.pad(edges.reshape(nb, pb, W_BRANCH)[1:], ((0, 1), (ch - pb, 0), (0, 0)))
        return (main + add).reshape(S, W_BRANCH)

    return dq, with_edges(dk, ek), with_edges(dv, ev), db


def att_combine(os_, lses):
    S = os_[0].shape[0]

    def body(o0, o1, o2, l0, l1, l2, out_ref, w0, w1, w2):
        ls = [l0[...], l1[...], l2[...]]
        m = jnp.maximum(jnp.maximum(ls[0], ls[1]), ls[2])
        es = [jnp.exp(l - m) for l in ls]
        den = es[0] + es[1] + es[2]
        ws = [e / den for e in es]
        out_ref[...] = (ws[0] * o0[...] + ws[1] * o1[...] + ws[2] * o2[...]).astype(out_ref.dtype)
        for w_ref, w in zip((w0, w1, w2), ws):
            w_ref[...] = w

    blk = pl.BlockSpec((ROW_TILE, W_BRANCH), lambda i: (i, 0))
    f = jax.ShapeDtypeStruct((S, W_BRANCH), F32)
    return pl.pallas_call(
        body, name="att_combine", grid=(S // ROW_TILE,), in_specs=[blk] * 6, out_specs=[blk] * 4,
        out_shape=[jax.ShapeDtypeStruct((S, W_BRANCH), BF16), f, f, f], compiler_params=_cp(1),
    )(*os_, *lses)


def _split3(x):
    x1 = x.astype(BF16)
    r1 = x - x1.astype(F32)
    x2 = r1.astype(BF16)
    x3 = (r1 - x2.astype(F32)).astype(BF16)
    return x1, x2, x3


def att_combine_bwd(dout, os_, wts, headsum):
    S = dout.shape[0]

    def body(do_ref, o0, o1, o2, w0, w1, w2, e_ref, cb_ref):
        out = w0[...] * o0[...] + w1[...] * o1[...] + w2[...] * o2[...]
        e = e_ref[...]
        acc = jnp.zeros((ROW_TILE, W_BRANCH), F32)
        for term in _split3(do_ref[...] * out):
            acc = acc + jnp.dot(term, e, preferred_element_type=F32)
        cb_ref[...] = acc

    blk = pl.BlockSpec((ROW_TILE, W_BRANCH), lambda i: (i, 0))
    return pl.pallas_call(
        body, name="att_combine_bwd", grid=(S // ROW_TILE,),
        in_specs=[blk] * 7 + [pl.BlockSpec((W_BRANCH, W_BRANCH), lambda i: (0, 0))], out_specs=blk,
        out_shape=jax.ShapeDtypeStruct((S, W_BRANCH), F32), compiler_params=_cp(1),
    )(dout, *os_, *wts, headsum)


def _cmul(ar, ai, br, bi):
    return ar * br - ai * bi, ar * bi + ai * br


SCAN_ROWS = 8
SCAN_GROUPS = SSM_T // SCAN_ROWS


def _log_scan(xr, xi, mr, mi, rows, n, steps, reverse):
    total = xr.shape[0]
    for k in range(steps):
        dd = 1 << k
        keep = rows < n - dd if reverse else rows >= dd
        shift = total - dd if reverse else dd
        ar, ai = _cmul(mr, mi, jnp.where(keep, pltpu.roll(xr, shift, 0), 0.0), jnp.where(keep, pltpu.roll(xi, shift, 0), 0.0))
        xr, xi = xr + ar, xi + ai
        mr, mi = _cmul(mr, mi, mr, mi)
    return xr, xi, mr, mi


def _scan_scratch(n_results):
    return ([pltpu.VMEM((W_BRANCH // 128, SSM_T, 128), F32)] * 2 + [pltpu.VMEM((SCAN_GROUPS, W_BRANCH), F32)] * 2
            + [pltpu.VMEM((SSM_T, W_BRANCH), F32)] * n_results)


def _block_scan(xr, xi, mr, mi, reverse, yr_ref, yi_ref, er_ref, ei_ref, hr_ref, hi_ref):
    cols = xr.shape[1]
    rows = lax.broadcasted_iota(jnp.int32, (SSM_T, cols), 0)
    yr, yi, m8r, m8i = _log_scan(xr, xi, mr, mi, rows & (SCAN_ROWS - 1), SCAN_ROWS, 3, reverse)
    lane_blocks = range(cols // 128)
    for c in lane_blocks:
        yr_ref[c] = yr[:, c * 128:(c + 1) * 128]
        yi_ref[c] = yi[:, c * 128:(c + 1) * 128]
    wide = lambda ref, rows_: jnp.concatenate([ref[c, rows_, :] for c in lane_blocks], axis=1)
    end = pl.ds(0 if reverse else SCAN_ROWS - 1, SCAN_GROUPS, stride=SCAN_ROWS)
    groups = lax.broadcasted_iota(jnp.int32, (SCAN_GROUPS, cols), 0)
    er, ei, _, _ = _log_scan(wide(yr_ref, end), wide(yi_ref, end), m8r, m8i, groups, SCAN_GROUPS,
                             int(math.log2(SCAN_GROUPS)), reverse)
    er_ref[...] = er
    ei_ref[...] = ei
    j = lax.broadcasted_iota(jnp.int32, (SCAN_ROWS, cols), 0)
    dist = SCAN_ROWS - j if reverse else j + 1
    tr, ti = jnp.ones((SCAN_ROWS, cols), F32), jnp.zeros((SCAN_ROWS, cols), F32)
    br, bi = mr, mi
    for bit in range(4):
        nr, ni = _cmul(tr, ti, br, bi)
        take = ((dist >> bit) & 1) == 1
        tr, ti = jnp.where(take, nr, tr), jnp.where(take, ni, ti)
        br, bi = _cmul(br, bi, br, bi)
    for g in range(SCAN_GROUPS):
        before = g + 1 if reverse else g - 1
        rows_g = slice(g * SCAN_ROWS, (g + 1) * SCAN_ROWS)
        if 0 <= before < SCAN_GROUPS:
            ar, ai = _cmul(tr, ti, er_ref[before:before + 1, :], ei_ref[before:before + 1, :])
            hr_ref[rows_g, :] = wide(yr_ref, rows_g) + ar
            hi_ref[rows_g, :] = wide(yi_ref, rows_g) + ai
        else:
            hr_ref[rows_g, :] = wide(yr_ref, rows_g)
            hi_ref[rows_g, :] = wide(yi_ref, rows_g)
    last = 0 if reverse else SCAN_GROUPS - 1
    return er_ref[last:last + 1, :], ei_ref[last:last + 1, :]


def s5_fwd(proj, b_re, b_im, a_re, a_im, c_re, c_im, d_skip):
    S = proj.shape[0]
    nt = S // SSM_T

    def body(u_ref, bre_ref, bim_ref, ar_ref, ai_ref, cre_ref, cim_ref, dsk_ref, hr_ref, hi_ref, y_ref, cr_ref, ci_ref,
             yr_ref, yi_ref, er_ref, ei_ref):
        t = pl.program_id(1)

        @pl.when(t == 0)
        def _():
            cr_ref[...] = jnp.zeros(cr_ref.shape, F32)
            ci_ref[...] = jnp.zeros(ci_ref.shape, F32)

        u = u_ref[...]
        ar, ai = ar_ref[...], ai_ref[...]
        rows = lax.broadcasted_iota(jnp.int32, (SSM_T, W_BRANCH), 0)
        inr, ini = _cmul(ar, ai, cr_ref[0:1, :], ci_ref[0:1, :])
        xr = _dot(u, bre_ref[...]) + jnp.where(rows == 0, inr, 0.0)
        xi = _dot(u, bim_ref[...]) + jnp.where(rows == 0, ini, 0.0)
        endr, endi = _block_scan(xr, xi, ar, ai, False, yr_ref, yi_ref, er_ref, ei_ref, hr_ref, hi_ref)
        cr_ref[...] = jnp.broadcast_to(endr, cr_ref.shape)
        ci_ref[...] = jnp.broadcast_to(endi, ci_ref.shape)
        xr, xi = hr_ref[...], hi_ref[...]
        y_ref[...] = _dot(xr, cre_ref[...]) - _dot(xi, cim_ref[...]) + u * dsk_ref[...]

    u_spec = pl.BlockSpec((SSM_T, 128), lambda j, t: (t, OFF_SSM // 128 + j))
    b_spec = pl.BlockSpec((None, 128, W_BRANCH), lambda j, t: (j, 0, 0))
    a_spec = pl.BlockSpec((1, W_BRANCH), lambda j, t: (0, j))
    c_spec = pl.BlockSpec((None, W_BRANCH, 128), lambda j, t: (j, 0, 0))
    h_spec = pl.BlockSpec((SSM_T, W_BRANCH), lambda j, t: (t, j))
    return pl.pallas_call(
        body, name="s5_fwd", grid=(4, nt),
        in_specs=[u_spec, b_spec, b_spec, a_spec, a_spec, c_spec, c_spec, pl.BlockSpec((1, 128), lambda j, t: (0, j))],
        out_specs=[h_spec, h_spec, pl.BlockSpec((SSM_T, 128), lambda j, t: (t, j))],
        out_shape=[jax.ShapeDtypeStruct((S, SSM_COLS), F32), jax.ShapeDtypeStruct((S, SSM_COLS), F32),
                   jax.ShapeDtypeStruct((S, W_BRANCH), F32)],
        scratch_shapes=[pltpu.VMEM((8, W_BRANCH), F32)] * 2 + _scan_scratch(0),
        compiler_params=_cp(2),
    )(proj, b_re, b_im, a_re, a_im, c_re, c_im, d_skip)


def s5_bwd(proj, hr, hi, dy, b_re, b_im, a_re, a_im, c_re, c_im, d_skip):
    S = proj.shape[0]
    nt = S // SSM_T

    def body(u_ref, hr_ref, hi_ref, hpr_ref, hpi_ref, dy_ref, bre_ref, bim_ref, ar_ref, ai_ref, cre_ref, cim_ref,
             dsk_ref, du_ref, dbre_ref, dbim_ref, dar_ref, dai_ref, dcre_ref, dcim_ref, ddsk_ref, gr_ref, gi_ref,
             yr_ref, yi_ref, er_ref, ei_ref, sr_ref, si_ref):
        step = pl.program_id(1)
        t = nt - 1 - step

        @pl.when(step == 0)
        def _():
            gr_ref[...] = jnp.zeros(gr_ref.shape, F32)
            gi_ref[...] = jnp.zeros(gi_ref.shape, F32)
            for ref in (dbre_ref, dbim_ref, dar_ref, dai_ref, dcre_ref, dcim_ref, ddsk_ref):
                ref[...] = jnp.zeros(ref.shape, F32)

        u = u_ref[...]
        dy = dy_ref[...]
        ar, ai = ar_ref[...], ai_ref[...]
        rows = lax.broadcasted_iota(jnp.int32, (SSM_T, W_BRANCH), 0)
        inr, ini = _cmul(ar, -ai, gr_ref[0:1, :], gi_ref[0:1, :])
        xr = _dot(dy, cre_ref[...], "nt") + jnp.where(rows == SSM_T - 1, inr, 0.0)
        xi = -_dot(dy, cim_ref[...], "nt") + jnp.where(rows == SSM_T - 1, ini, 0.0)
        endr, endi = _block_scan(xr, xi, ar, -ai, True, yr_ref, yi_ref, er_ref, ei_ref, sr_ref, si_ref)
        gr_ref[...] = jnp.broadcast_to(endr, gr_ref.shape)
        gi_ref[...] = jnp.broadcast_to(endi, gi_ref.shape)
        xr, xi = sr_ref[...], si_ref[...]
        hr_blk, hi_blk = hr_ref[...], hi_ref[...]
        keep = (t > 0).astype(F32)
        hpr = jnp.where(rows >= 1, pltpu.roll(hr_blk, 1, 0), hpr_ref[7:8, :] * keep)
        hpi = jnp.where(rows >= 1, pltpu.roll(hi_blk, 1, 0), hpi_ref[7:8, :] * keep)
        dar_ref[...] += jnp.sum(hpr * xr + hpi * xi, axis=0, keepdims=True)
        dai_ref[...] += jnp.sum(hpr * xi - hpi * xr, axis=0, keepdims=True)
        dcre_ref[...] += _dot(hr_blk, dy, "tn")
        dcim_ref[...] -= _dot(hi_blk, dy, "tn")
        du = dy * dsk_ref[...] + _dot(xr, bre_ref[...], "nt") + _dot(xi, bim_ref[...], "nt")
        du_ref[...] = du.astype(du_ref.dtype)
        dbre_ref[...] += _dot(u, xr, "tn")
        dbim_ref[...] += _dot(u, xi, "tn")
        ddsk_ref[...] += jnp.sum(dy * u, axis=0, keepdims=True)

    def rev(t):
        return nt - 1 - t

    u_spec = pl.BlockSpec((SSM_T, 128), lambda j, t: (rev(t), OFF_SSM // 128 + j))
    h_spec = pl.BlockSpec((SSM_T, W_BRANCH), lambda j, t: (rev(t), j))
    hprev_spec = pl.BlockSpec((8, W_BRANCH), lambda j, t: (jnp.maximum(rev(t) * (SSM_T // 8) - 1, 0), j))
    ch_spec = pl.BlockSpec((SSM_T, 128), lambda j, t: (rev(t), j))
    b_spec = pl.BlockSpec((None, 128, W_BRANCH), lambda j, t: (j, 0, 0))
    a_spec = pl.BlockSpec((1, W_BRANCH), lambda j, t: (0, j))
    c_spec = pl.BlockSpec((None, W_BRANCH, 128), lambda j, t: (j, 0, 0))
    d_spec = pl.BlockSpec((1, 128), lambda j, t: (0, j))
    return pl.pallas_call(
        body, name="s5_bwd", grid=(4, nt),
        in_specs=[u_spec, h_spec, h_spec, hprev_spec, hprev_spec, ch_spec, b_spec, b_spec, a_spec, a_spec,
                  c_spec, c_spec, d_spec],
        out_specs=[ch_spec, b_spec, b_spec, a_spec, a_spec, c_spec, c_spec, d_spec],
        out_shape=[jax.ShapeDtypeStruct((S, W_BRANCH), BF16),
                   jax.ShapeDtypeStruct((4, 128, W_BRANCH), F32), jax.ShapeDtypeStruct((4, 128, W_BRANCH), F32),
                   jax.ShapeDtypeStruct((1, SSM_COLS), F32), jax.ShapeDtypeStruct((1, SSM_COLS), F32),
                   jax.ShapeDtypeStruct((4, W_BRANCH, 128), F32), jax.ShapeDtypeStruct((4, W_BRANCH, 128), F32),
                   jax.ShapeDtypeStruct((1, W_BRANCH), F32)],
        scratch_shapes=[pltpu.VMEM((8, W_BRANCH), F32)] * 2 + _scan_scratch(2),
        compiler_params=_cp(2),
    )(proj, hr, hi, hr, hi, dy, b_re, b_im, a_re, a_im, c_re, c_im, d_skip)


def glu_fwd(y, w_glu, b_glu):
    S = y.shape[0]

    def body(y_ref, w_ref, b_ref, o_ref):
        g = _gelu(y_ref[...])
        o_ref[...] = (g * _sigmoid(_dot(g, w_ref[...]) + b_ref[...])).astype(o_ref.dtype)

    blk = pl.BlockSpec((ROW_TILE, W_BRANCH), lambda i: (i, 0))
    return pl.pallas_call(
        body, name="glu_fwd", grid=(S // ROW_TILE,),
        in_specs=[blk, pl.BlockSpec((W_BRANCH, W_BRANCH), lambda i: (0, 0)), pl.BlockSpec((1, W_BRANCH), lambda i: (0, 0))],
        out_specs=blk, out_shape=jax.ShapeDtypeStruct((S, W_BRANCH), BF16), compiler_params=_cp(1),
    )(y, w_glu, b_glu)


def glu_bwd(y, w_glu, b_glu, dout):
    S = y.shape[0]

    def body(y_ref, w_ref, b_ref, do_ref, dy_ref, dw_ref, db_ref):
        yv = y_ref[...]
        do = do_ref[...]
        g = _gelu(yv)
        s = _sigmoid(_dot(g, w_ref[...]) + b_ref[...])
        dz = do * g * s * (1.0 - s)
        dg = do * s + _dot(dz, w_ref[...], "nt")
        dy_ref[...] = dg * _gelu_grad(yv)
        dw = _dot(g, dz, "tn")
        db = jnp.sum(dz, axis=0, keepdims=True)

        @pl.when(pl.program_id(0) == 0)
        def _():
            dw_ref[...] = dw
            db_ref[...] = db

        @pl.when(pl.program_id(0) > 0)
        def _():
            dw_ref[...] += dw
            db_ref[...] += db

    blk = pl.BlockSpec((ROW_TILE, W_BRANCH), lambda i: (i, 0))
    mat = pl.BlockSpec((W_BRANCH, W_BRANCH), lambda i: (0, 0))
    vec = pl.BlockSpec((1, W_BRANCH), lambda i: (0, 0))
    return pl.pallas_call(
        body, name="glu_bwd", grid=(S // ROW_TILE,), in_specs=[blk, mat, vec, blk], out_specs=[blk, mat, vec],
        out_shape=[jax.ShapeDtypeStruct((S, W_BRANCH), F32), jax.ShapeDtypeStruct((W_BRANCH, W_BRANCH), F32),
                   jax.ShapeDtypeStruct((1, W_BRANCH), F32)],
        compiler_params=_cp(1),
    )(y, w_glu, b_glu, dout)


SGU_TILE = 512
SGU_U_BLOCK = OFF_SGU // W_BRANCH
SGU_V_BLOCK = SGU_U_BLOCK + 1


def _sgu_norm(zv):
    v = _gelu(zv)
    mu = jnp.mean(v, axis=-1, keepdims=True)
    vc = v - mu
    rstd = lax.rsqrt(jnp.mean(vc * vc, axis=-1, keepdims=True) + EPS)
    return vc * rstd, rstd


def _tril():
    return lax.broadcasted_iota(jnp.int32, (SGU_CHUNK, SGU_CHUNK), 0) >= lax.broadcasted_iota(jnp.int32, (SGU_CHUNK, SGU_CHUNK), 1)


def sgu_fwd(proj, ln_g, ln_b, w_s, b_s_t):
    S = proj.shape[0]

    def body(zu_ref, zv_ref, g_ref, b_ref, ws_ref, bs_ref, o_ref, vf_ref):
        vn, _ = _sgu_norm(zv_ref[...])
        vf_ref[...] = vn * g_ref[...] + b_ref[...]
        tri = _tril()
        for gi in range(4):
            ws = jnp.where(tri, ws_ref[gi], 0.0)
            cols = slice(gi * 128, (gi + 1) * 128)
            for c in range(SGU_TILE // SGU_CHUNK):
                rows = slice(c * SGU_CHUNK, (c + 1) * SGU_CHUNK)
                sv = _dot(ws, vf_ref[rows, cols]) + bs_ref[:, gi:gi + 1]
                o_ref[rows, cols] = (_gelu(zu_ref[rows, cols]) * sv).astype(o_ref.dtype)

    blk = lambda cb: pl.BlockSpec((SGU_TILE, W_BRANCH), lambda i: (i, cb))
    vec = pl.BlockSpec((1, W_BRANCH), lambda i: (0, 0))
    return pl.pallas_call(
        body, name="sgu_fwd", grid=(S // SGU_TILE,),
        in_specs=[blk(SGU_U_BLOCK), blk(SGU_V_BLOCK), vec, vec, pl.BlockSpec((4, SGU_CHUNK, SGU_CHUNK), lambda i: (0, 0, 0)),
                  pl.BlockSpec((SGU_CHUNK, 4), lambda i: (0, 0))],
        out_specs=blk(0), out_shape=jax.ShapeDtypeStruct((S, W_BRANCH), BF16),
        scratch_shapes=[pltpu.VMEM((SGU_TILE, W_BRANCH), F32)], compiler_params=_cp(1),
    )(proj, proj, ln_g, ln_b, w_s, b_s_t)


def sgu_bwd(proj, ln_g, ln_b, w_s, b_s_t, dout):
    S = proj.shape[0]

    def body(zu_ref, zv_ref, g_ref, b_ref, ws_ref, bs_ref, do_ref, dzu_ref, dzv_ref, dg_ref, db_ref, dws_ref, dbs_ref,
             vf_ref, dvf_ref):
        @pl.when(pl.program_id(0) == 0)
        def _():
            for ref in (dg_ref, db_ref, dws_ref, dbs_ref):
                ref[...] = jnp.zeros(ref.shape, F32)

        vn, rstd = _sgu_norm(zv_ref[...])
        vf_ref[...] = vn * g_ref[...] + b_ref[...]
        tri = _tril()
        lane = lax.broadcasted_iota(jnp.int32, (SGU_CHUNK, 128), 1)
        dbs = jnp.zeros((SGU_CHUNK, 128), F32)
        for gi in range(4):
            ws = jnp.where(tri, ws_ref[gi], 0.0)
            cols = slice(gi * 128, (gi + 1) * 128)
            dws = jnp.zeros((SGU_CHUNK, SGU_CHUNK), F32)
            for c in range(SGU_TILE // SGU_CHUNK):
                rows = slice(c * SGU_CHUNK, (c + 1) * SGU_CHUNK)
                vf = vf_ref[rows, cols]
                zu = zu_ref[rows, cols]
                do = do_ref[rows, cols]
                sv = _dot(ws, vf) + bs_ref[:, gi:gi + 1]
                dzu_ref[rows, cols] = (do * sv * _gelu_grad(zu)).astype(dzu_ref.dtype)
                dsv = do * _gelu(zu)
                dvf_ref[rows, cols] = _dot(ws, dsv, "tn")
                dws = dws + _dot(dsv, vf, "nt")
                dbs = dbs + jnp.where(lane == gi, jnp.sum(dsv, axis=-1, keepdims=True), 0.0)
            dws_ref[gi] += jnp.where(tri, dws, 0.0)
        dbs_ref[...] += dbs
        dvf = dvf_ref[...]
        dg_ref[...] += jnp.sum(dvf * vn, axis=0, keepdims=True)
        db_ref[...] += jnp.sum(dvf, axis=0, keepdims=True)
        dvn = dvf * g_ref[...]
        dv = rstd * (dvn - jnp.mean(dvn, axis=-1, keepdims=True) - vn * jnp.mean(dvn * vn, axis=-1, keepdims=True))
        dzv_ref[...] = (dv * _gelu_grad(zv_ref[...])).astype(dzv_ref.dtype)

    blk = lambda cb: pl.BlockSpec((SGU_TILE, W_BRANCH), lambda i: (i, cb))
    vec = pl.BlockSpec((1, W_BRANCH), lambda i: (0, 0))
    ws_spec = pl.BlockSpec((4, SGU_CHUNK, SGU_CHUNK), lambda i: (0, 0, 0))
    return pl.pallas_call(
        body, name="sgu_bwd", grid=(S // SGU_TILE,),
        in_specs=[blk(SGU_U_BLOCK), blk(SGU_V_BLOCK), vec, vec, ws_spec, pl.BlockSpec((SGU_CHUNK, 4), lambda i: (0, 0)),
                  blk(0)],
        out_specs=[blk(0), blk(0), vec, vec, ws_spec, pl.BlockSpec((SGU_CHUNK, 128), lambda i: (0, 0))],
        out_shape=[jax.ShapeDtypeStruct((S, W_BRANCH), BF16), jax.ShapeDtypeStruct((S, W_BRANCH), BF16),
                   jax.ShapeDtypeStruct((1, W_BRANCH), F32), jax.ShapeDtypeStruct((1, W_BRANCH), F32),
                   jax.ShapeDtypeStruct((4, SGU_CHUNK, SGU_CHUNK), F32), jax.ShapeDtypeStruct((SGU_CHUNK, 128), F32)],
        scratch_shapes=[pltpu.VMEM((SGU_TILE, W_BRANCH), F32), pltpu.VMEM((SGU_TILE, W_BRANCH), F32)],
        compiler_params=_cp(1),
    )(proj, proj, ln_g, ln_b, w_s, b_s_t, dout)


GM_TILE = 512


def _gate_specs(order):
    def spec(i):
        def index(*ids):
            m, n = order(*ids)
            return (m, (OFF_GATE + i * D_MODEL) // GM_TILE + n)
        return pl.BlockSpec((GM_TILE, GM_TILE), index)
    return [spec(i) for i in range(4)]


def merge_fwd(proj, gate_b, branches, w_up):
    S = proj.shape[0]
    order = lambda n, m: (m, n)

    def body(p0, p1, p2, p3, gb_ref, b0, b1, b2, b3, w_ref, o_ref):
        acc = jnp.zeros((GM_TILE, GM_TILE), F32)
        for i, (p_ref, br_ref) in enumerate(zip((p0, p1, p2, p3), (b0, b1, b2, b3))):
            acc = acc + _sigmoid(p_ref[...] + gb_ref[i:i + 1, :]) * _dot(br_ref[...], w_ref[i])
        o_ref[...] = acc.astype(o_ref.dtype)

    br_spec = pl.BlockSpec((GM_TILE, W_BRANCH), lambda n, m: (m, 0))
    return pl.pallas_call(
        body, name="merge_fwd", grid=(D_MODEL // GM_TILE, S // GM_TILE),
        in_specs=_gate_specs(order) + [pl.BlockSpec((4, GM_TILE), lambda n, m: (0, n))] + [br_spec] * 4
        + [pl.BlockSpec((4, W_BRANCH, GM_TILE), lambda n, m: (0, 0, n))],
        out_specs=pl.BlockSpec((GM_TILE, GM_TILE), lambda n, m: (m, n)),
        out_shape=jax.ShapeDtypeStruct((S, D_MODEL), BF16), compiler_params=_cp(2),
    )(proj, proj, proj, proj, gate_b, *branches, w_up)


def merge_bwd(proj, gate_b, branches, w_up, dmerged):
    S = proj.shape[0]
    order = lambda n, m: (m, n)

    def body(p0, p1, p2, p3, gb_ref, b0, b1, b2, b3, w_ref, dm_ref, dp0, dp1, dp2, dp3, du0, du1, du2, du3, dgb_ref):
        dm = dm_ref[...]
        dgb = []
        for i, (p_ref, br_ref, dp_ref, du_ref) in enumerate(
                zip((p0, p1, p2, p3), (b0, b1, b2, b3), (dp0, dp1, dp2, dp3), (du0, du1, du2, du3))):
            gate = _sigmoid(p_ref[...] + gb_ref[i:i + 1, :])
            dpre = dm * _dot(br_ref[...], w_ref[i]) * gate * (1.0 - gate)
            dp_ref[...] = dpre.astype(dp_ref.dtype)
            du_ref[...] = (dm * gate).astype(du_ref.dtype)
            dgb.append(jnp.sum(dpre, axis=0, keepdims=True))
        dgb = jnp.concatenate(dgb, axis=0)

        @pl.when(pl.program_id(1) == 0)
        def _():
            dgb_ref[...] = dgb

        @pl.when(pl.program_id(1) > 0)
        def _():
            dgb_ref[...] += dgb

    br_spec = pl.BlockSpec((GM_TILE, W_BRANCH), lambda n, m: (m, 0))
    mn = pl.BlockSpec((GM_TILE, GM_TILE), lambda n, m: (m, n))
    gb = pl.BlockSpec((4, GM_TILE), lambda n, m: (0, n))
    big = jax.ShapeDtypeStruct((S, D_MODEL), BF16)
    outs = pl.pallas_call(
        body, name="merge_bwd", grid=(D_MODEL // GM_TILE, S // GM_TILE),
        in_specs=_gate_specs(order) + [gb] + [br_spec] * 4
        + [pl.BlockSpec((4, W_BRANCH, GM_TILE), lambda n, m: (0, 0, n)), mn],
        out_specs=[mn] * 8 + [gb], out_shape=[big] * 8 + [jax.ShapeDtypeStruct((4, D_MODEL), F32)],
        compiler_params=_cp(2),
    )(proj, proj, proj, proj, gate_b, *branches, w_up, dmerged)
    return outs[0:4], outs[4:8], outs[8]


def _xatt_probs(q, k):
    s = _dot(q, k, "nt") * (X_HEAD_DIM ** -0.5)
    p = jnp.exp(s - jnp.max(s, axis=-1, keepdims=True))
    return p / jnp.sum(p, axis=-1, keepdims=True)


def xatt_fwd(q, kv):
    S = q.shape[0]

    def body(q_ref, kv_ref, o_ref):
        for h in range(X_HEADS):
            cols = slice(h * X_HEAD_DIM, (h + 1) * X_HEAD_DIM)
            p = _xatt_probs(q_ref[:, cols], kv_ref[:, cols])
            o_ref[:, cols] = _dot(p, kv_ref[:, W_BRANCH + h * X_HEAD_DIM:W_BRANCH + (h + 1) * X_HEAD_DIM]).astype(o_ref.dtype)

    blk = pl.BlockSpec((ROW_TILE, W_BRANCH), lambda i: (i, 0))
    return pl.pallas_call(
        body, name="xatt_fwd", grid=(S // ROW_TILE,),
        in_specs=[blk, pl.BlockSpec((N_MEM, 2 * W_BRANCH), lambda i: (0, 0))], out_specs=blk,
        out_shape=jax.ShapeDtypeStruct((S, W_BRANCH), BF16), compiler_params=_cp(1),
    )(q, kv)


def xatt_bwd(q, kv, do):
    S = q.shape[0]

    def body(q_ref, kv_ref, do_ref, dq_ref, dkv_ref):
        @pl.when(pl.program_id(0) == 0)
        def _():
            dkv_ref[...] = jnp.zeros(dkv_ref.shape, F32)

        for h in range(X_HEADS):
            cols = slice(h * X_HEAD_DIM, (h + 1) * X_HEAD_DIM)
            vcols = slice(W_BRANCH + h * X_HEAD_DIM, W_BRANCH + (h + 1) * X_HEAD_DIM)
            qh, kh, doh = q_ref[:, cols], kv_ref[:, cols], do_ref[:, cols]
            p = _xatt_probs(qh, kh)
            dp = _dot(doh, kv_ref[:, vcols], "nt")
            ds = p * (dp - jnp.sum(dp * p, axis=-1, keepdims=True)) * (X_HEAD_DIM ** -0.5)
            dq_ref[:, cols] = _dot(ds, kh).astype(dq_ref.dtype)
            dkv_ref[:, cols] += _dot(ds, qh, "tn")
            dkv_ref[:, vcols] += _dot(p, doh, "tn")

    blk = pl.BlockSpec((ROW_TILE, W_BRANCH), lambda i: (i, 0))
    kv_spec = pl.BlockSpec((N_MEM, 2 * W_BRANCH), lambda i: (0, 0))
    return pl.pallas_call(
        body, name="xatt_bwd", grid=(S // ROW_TILE,), in_specs=[blk, kv_spec, blk], out_specs=[blk, kv_spec],
        out_shape=[jax.ShapeDtypeStruct((S, W_BRANCH), BF16), jax.ShapeDtypeStruct((N_MEM, 2 * W_BRANCH), F32)],
        compiler_params=_cp(1),
    )(q, kv, do)


def s5_params(a_re, a_im, log_dt, b_re, b_im, c_re, c_im):
    lam_re = jnp.minimum(a_re, -1e-4)
    lam_im = a_im
    dt = jnp.exp(log_dt)[:, None]
    mag = jnp.exp(lam_re * dt)
    ab_re, ab_im = mag * jnp.cos(lam_im * dt), mag * jnp.sin(lam_im * dt)
    den = lam_re * lam_re + lam_im * lam_im
    f_re = ((ab_re - 1.0) * lam_re + ab_im * lam_im) / den
    f_im = (ab_im * lam_re - (ab_re - 1.0) * lam_im) / den
    bb_re = f_re[..., None] * b_re - f_im[..., None] * b_im
    bb_im = f_re[..., None] * b_im + f_im[..., None] * b_re
    eye = jnp.eye(8, dtype=F32)

    def b_blocks(bb):
        t = bb.reshape(4, 8, SSM_STATE, SSM_GROUP).transpose(0, 1, 3, 2)
        return (t[:, :, :, None, :] * eye[None, :, None, :, None]).reshape(4, 128, W_BRANCH)

    def c_blocks(cc):
        t = cc.reshape(4, 8, SSM_GROUP, SSM_STATE).transpose(0, 1, 3, 2)
        return (t[:, :, :, None, :] * eye[None, :, None, :, None]).reshape(4, W_BRANCH, 128)

    return (ab_re.reshape(1, SSM_COLS), ab_im.reshape(1, SSM_COLS), b_blocks(bb_re), b_blocks(bb_im),
            c_blocks(c_re), c_blocks(c_im))


ANY = pl.BlockSpec(memory_space=pl.ANY)


def _chip_index():
    return 2 * lax.axis_index("x") + lax.axis_index("y")


def _peer_chip(j):
    x, y, c = lax.axis_index("x"), lax.axis_index("y"), lax.axis_index("c")
    return ((1 - x) if j & 2 else x, (1 - y) if j & 1 else y, c)


def _piece(ref, axis, s, n):
    size = ref.shape[axis] // n
    idx = [slice(None)] * len(ref.shape)
    idx[axis] = pl.ds(s * size, size)
    return ref.at[tuple(idx)]


HBM_SPEC = pl.BlockSpec(memory_space=pltpu.HBM)
SEM_SPEC = pl.BlockSpec(memory_space=pltpu.SEMAPHORE)
SIDE_EFFECT = pltpu.SideEffectType.DATAFLOW_SIDE_EFFECTING


def _chip_copies(ins, lands, send, recv, axes, mode, k, arriving):
    copies = []
    for t in range(len(ins)):
        for j in (1, 2, 3):
            place = k ^ j if arriving else k
            if mode == "gather":
                src, dst = ins[t], _piece(lands[t], axes[t], place, 4)
            else:
                src = ins[t] if axes[t] is None else _piece(ins[t], axes[t], k ^ j, 4)
                dst = lands[t].at[place]
            copies.append(pltpu.make_async_remote_copy(
                src_ref=src, dst_ref=dst, send_sem=send.at[3 * t + j - 1], recv_sem=recv.at[3 * t + j - 1],
                device_id=_peer_chip(j), device_id_type=MESH_ID))
    return copies


def chips_start(ins, lands, axes, mode, name, after=()):
    n, na = len(ins), len(after)

    def body(*refs):
        in_refs, land_refs = refs[:n], refs[n:2 * n]
        send, recv, token = refs[2 * n + na], refs[2 * n + na + 1], refs[-1]
        q = _chip_index()
        for k in range(4):
            @pl.when(q == k)
            def _():
                for copy in _chip_copies(in_refs, land_refs, send, recv, axes, mode, k, arriving=False):
                    copy.start()
        token[...] = jnp.zeros(token.shape, token.dtype)

    hbm = lambda a: pltpu.HBM(a.shape, a.dtype)
    outs = pl.pallas_call(
        body, name=name, in_specs=[HBM_SPEC] * (2 * n) + [ANY] * na,
        out_specs=[SEM_SPEC, SEM_SPEC] + [HBM_SPEC] * (2 * n) + [pl.BlockSpec(memory_space=pltpu.VMEM)],
        out_shape=[pltpu.SemaphoreType.DMA((3 * n,)), pltpu.SemaphoreType.DMA((3 * n,))]
        + [hbm(a) for a in ins] + [hbm(a) for a in lands] + [jax.ShapeDtypeStruct((8, 128), F32)],
        input_output_aliases={i: 2 + i for i in range(2 * n)},
        compiler_params=pltpu.CompilerParams(has_side_effects=SIDE_EFFECT),
    )(*[pltpu.with_memory_space_constraint(a, pltpu.HBM) for a in list(ins) + list(lands)], *after)
    return outs[0], outs[1], outs[2:2 + n], outs[2 + n:2 + 2 * n], outs[-1]


def chips_wait(send, recv, ins, lands, axes, mode, name, after=()):
    n = len(ins)

    def body(*refs):
        in_refs, land_refs = refs[:n], refs[n:2 * n]
        send_ref, recv_ref = refs[2 * n], refs[2 * n + 1]
        q = _chip_index()
        for k in range(4):
            @pl.when(q == k)
            def _():
                for copy in _chip_copies(in_refs, land_refs, send_ref, recv_ref, axes, mode, k, arriving=True):
                    copy.wait_send()
                    copy.wait_recv()

    hbm = lambda a: pltpu.HBM(a.shape, a.dtype)
    outs = pl.pallas_call(
        body, name=name, in_specs=[HBM_SPEC] * (2 * n) + [SEM_SPEC, SEM_SPEC] + [ANY] * len(after),
        out_specs=[HBM_SPEC] * (2 * n), out_shape=[hbm(a) for a in ins] + [hbm(a) for a in lands],
        input_output_aliases={i: i for i in range(2 * n)},
        compiler_params=pltpu.CompilerParams(has_side_effects=SIDE_EFFECT),
    )(*ins, *lands, send, recv, *after)
    return outs[:n], outs[n:]


def swap_cores(arrs, name):
    n = len(arrs)

    def body(*refs):
        ins, outs = refs[:n], refs[n:2 * n]
        send, recv = refs[2 * n:]
        sibling = (lax.axis_index("x"), lax.axis_index("y"), 1 - lax.axis_index("c"))
        copies = [pltpu.make_async_remote_copy(src_ref=ins[t], dst_ref=outs[t], send_sem=send.at[t], recv_sem=recv.at[t],
                                               device_id=sibling, device_id_type=MESH_ID) for t in range(n)]
        for cp in copies:
            cp.start()
        for cp in copies:
            cp.wait()

    return pl.pallas_call(
        body, name=name, in_specs=[ANY] * n, out_specs=[ANY] * n,
        out_shape=[jax.ShapeDtypeStruct(a.shape, a.dtype) for a in arrs],
        scratch_shapes=[pltpu.SemaphoreType.DMA((n,)), pltpu.SemaphoreType.DMA((n,))],
    )(*arrs)


ELEMENTWISE_BLOCK_BYTES = 1 << 20


def _row_tile(rows, cols):
    want = max(8, ELEMENTWISE_BLOCK_BYTES // (4 * 128 * -(-cols // 128)))
    fits = [t for t in range(8, min(rows, want) + 1, 8) if rows % t == 0]
    return fits[-1] if fits else rows


def sum_chips(recv, own, axis, chip, stacked, l, name):
    _, r, c = recv.shape
    tr = _row_tile(r, c)
    nrt = r // tr

    def body(chip_ref, r_ref, own_ref, stacked_ref, o_ref):
        for k in range(4):
            @pl.when(chip_ref[0] == k)
            def _():
                terms = [own_ref[...] if s == k else r_ref[s] for s in range(4)]
                o_ref[...] = ((terms[0] + terms[1]) + terms[2]) + terms[3]

    own_index = {0: lambda i, q: (q[0] * nrt + i, 0), 1: lambda i, q: (i, q[0]), None: lambda i, q: (i, 0)}[axis]
    return pl.pallas_call(
        body, name=name,
        grid_spec=pltpu.PrefetchScalarGridSpec(
            num_scalar_prefetch=1, grid=(nrt,),
            in_specs=[pl.BlockSpec((4, tr, c), lambda i, q: (0, i, 0)), pl.BlockSpec((tr, c), own_index), ANY],
            out_specs=pl.BlockSpec((None, tr, c), lambda i, q: (l, i, 0))),
        out_shape=jax.ShapeDtypeStruct(stacked.shape, F32), input_output_aliases={3: 0}, compiler_params=_cp(1),
    )(chip, recv, own, stacked)


def adamw(w, ga, gb, m, v, name):
    rows, cols = w.shape
    tr = _row_tile(rows, cols)

    def body(w_ref, ga_ref, gb_ref, m_ref, v_ref, g_ref, d_ref, nm_ref, nv_ref):
        g = ga_ref[...] + gb_ref[...]
        nm = ADAM_B1 * m_ref[...] + (1.0 - ADAM_B1) * g
        nv = ADAM_B2 * v_ref[...] + (1.0 - ADAM_B2) * (g * g)
        m_hat = nm / (1.0 - ADAM_B1 ** ADAM_STEP)
        v_hat = nv / (1.0 - ADAM_B2 ** ADAM_STEP)
        g_ref[...] = g
        nm_ref[...] = nm
        nv_ref[...] = nv
        d_ref[...] = -ADAM_LR * (m_hat / (jnp.sqrt(v_hat) + ADAM_EPS) + ADAM_WD * w_ref[...])

    blk = pl.BlockSpec((tr, cols), lambda i: (i, 0))
    f = jax.ShapeDtypeStruct((rows, cols), F32)
    return pl.pallas_call(
        body, name=name, grid=(rows // tr,), in_specs=[blk] * 5, out_specs=[blk] * 4, out_shape=[f] * 4,
        compiler_params=_cp(1),
    )(w, ga, gb, m, v)


PACK_ALIGN = 1024
PACK_ROWS_ALIGN = 2048


def pack_small(arrs):
    parts = []
    for a in arrs:
        flat = a.reshape(-1)
        pad = (-flat.shape[0]) % PACK_ALIGN
        parts.append(jnp.pad(flat, (0, pad)) if pad else flat)
    rows = sum(p.shape[0] for p in parts) // 128
    parts.append(jnp.zeros(((-rows) % PACK_ROWS_ALIGN * 128,), arrs[0].dtype))
    return jnp.concatenate(parts).reshape(-1, 128)


def unpack_small(packed, shapes):
    out, row = [], 0
    for shape in shapes:
        size = int(np.prod(shape))
        rows = -(-size // PACK_ALIGN) * 8
        out.append(packed[row:row + rows].reshape(-1)[:size].reshape(shape))
        row += rows
    return out


def _norm_epilogue(with_next):
    def epi(acc, res, g_post, *g_pre):
        x_new = acc * lax.rsqrt(jnp.mean(acc * acc, axis=-1, keepdims=True) + EPS) * g_post + res
        if not with_next:
            return acc, x_new
        return acc, x_new, x_new * lax.rsqrt(jnp.mean(x_new * x_new, axis=-1, keepdims=True) + EPS) * g_pre[0]
    return epi


def layer_fwd(x, h1, mem, w_in, rest_of, P, biases, g_next, after=()):
    sv = {"x0": x}
    post = dict(tm=512, tn=D_MODEL)
    proj = mm(h1, w_in, "nn", tm=1024, tn=768, tk=1024, out_dtypes=[F32], name="mm_w_in", after=after)
    a_out = pool_fwd(proj, P["pool_w"], P["pool_scale"])
    os_, lses = [], []
    for g, (win, dil) in enumerate(DIL_GROUPS):
        o, lse = att_fwd(proj, biases[g], g, dil)
        os_.append(o)
        lses.append(lse)
    b_out, w0, w1, w2 = att_combine(os_, lses)
    s5p = P["s5"]
    hr, hi, y = s5_fwd(proj, s5p[2], s5p[3], s5p[0], s5p[1], s5p[4], s5p[5], P["d_skip"])
    d_out = sgu_fwd(proj, P["sgu_ln_g"], P["sgu_ln_b"], P["w_s"], P["b_s_t"])
    W, after_rest = rest_of("mixer", d_out)
    W = dict(W, w_in=w_in)
    c_out = glu_fwd(y, W["w_glu"], P["b_glu"])
    branches = (a_out, b_out, c_out, d_out)
    merged = merge_fwd(proj, W["gate_b"], branches, W["w_up"])
    t1, x1, h2 = mm(merged, W["w_out"], "nn", tk=1024, out_dtypes=[F32, F32, BF16], name="mm_w_out", extras=(x,),
                    vecs=(P["g_mix_post"], P["g_x_pre"]), epi=_norm_epilogue(True), after=after_rest, **post)
    sv.update(h1=h1, proj=proj, os=os_, lses=lses, wts=(w0, w1, w2), hr=hr, hi=hi, y=y, branches=branches,
              merged=merged, t1=t1, x1=x1)

    mem_n = rms_fwd(mem, P["g_mem"], BF16, "rms_mem")
    q = mm(h2, W["w_cq"], "nn", tm=1024, tn=512, tk=1024, out_dtypes=[BF16], name="mm_w_cq")
    kv = mm(mem_n, W["w_ckv"], "nn", tm=256, tn=1024, tk=1024, out_dtypes=[BF16], name="mm_w_ckv")
    ox = xatt_fwd(q, kv)
    t2, x2, h3 = mm(ox, W["w_co"], "nn", tk=512, out_dtypes=[F32, F32, BF16], name="mm_w_co", extras=(x1,),
                    vecs=(P["g_x_post"], P["g_ff_pre"]), epi=_norm_epilogue(True), **post)
    sv.update(h2=h2, mem_n=mem_n, q=q, kv=kv, ox=ox, t2=t2, x2=x2)

    W_ff, after_ff = rest_of("mlp", h3)
    W = dict(W, **W_ff)
    pre, act = mm(h3, W["w_ff1"], "nn", tm=1024, tn=1024, tk=1024, out_dtypes=[F32, BF16], name="mm_w_ff1",
                  epi=lambda acc: (acc, jnp.square(jnp.maximum(acc, 0.0))), after=after_ff)
    if g_next is None:
        (ff, x3), h_next = mm(act, W["w_ff2"], "nn", tk=1024, out_dtypes=[F32, F32], name="mm_w_ff2_last", extras=(x2,),
                              vecs=(P["g_ff_post"],), epi=_norm_epilogue(False), tm=1024, tn=D_MODEL), None
    else:
        ff, x3, h_next = mm(act, W["w_ff2"], "nn", tk=1024, out_dtypes=[F32, F32, BF16], name="mm_w_ff2", extras=(x2,),
                            vecs=(P["g_ff_post"], g_next), epi=_norm_epilogue(True), tm=1024, tn=D_MODEL)
    sv.update(h3=h3, pre=pre, act=act, ff=ff, W=W)
    return x3, h_next, sv


def _pre_norm_bwd_epilogue(dh, x, add, g):
    r = lax.rsqrt(jnp.mean(x * x, axis=-1, keepdims=True) + EPS)
    xn = x * r
    dxn = dh * g
    return r * (dxn - xn * jnp.mean(dxn * xn, axis=-1, keepdims=True)) + add, jnp.sum(dh * xn, axis=0, keepdims=True)


def layer_bwd(dx, mem, W, P, biases, sv, headsum, emit, after=()):
    G = {}
    dff, G["g_ff_post"] = rms_bwd(sv["ff"], P["g_ff_post"], dx, BF16, "rms_post_bwd", after=after)
    G["w_ff2"] = mm(sv["act"], dff, "tn", tm=1024, tn=1024, tk=1024, out_dtypes=[F32], name="mm_dw_ff2")
    dpre = mm(dff, W["w_ff2"], "nt", tm=1024, tn=1024, tk=1024, out_dtypes=[BF16], name="mm_dact", extras=(sv["pre"],),
              epi=lambda acc, pre: (acc * (2.0 * jnp.maximum(pre, 0.0)),))
    G["w_ff1"] = mm(sv["h3"], dpre, "tn", tm=1024, tn=1024, tk=1024, out_dtypes=[F32], name="mm_dw_ff1")
    sent = emit(("w_ff1", "w_ff2"), G)
    pre_bwd = dict(tm=1024, tn=D_MODEL, out_dtypes=[F32], epi=_pre_norm_bwd_epilogue, n_sums=1)
    dx2, G["g_ff_pre"] = mm(dpre, W["w_ff1"], "nt", tk=1024, name="mm_dh3", extras=(sv["x2"], dx), vecs=(P["g_ff_pre"],),
                            after=sent, **pre_bwd)
    dt2, G["g_x_post"] = rms_bwd(sv["t2"], P["g_x_post"], dx2, BF16, "rms_post_bwd")
    G["w_co"] = mm(sv["ox"], dt2, "tn", tm=512, tn=1024, tk=1024, out_dtypes=[F32], name="mm_dw_co")
    dox = mm(dt2, W["w_co"], "nt", tm=1024, tn=512, tk=1024, out_dtypes=[BF16], name="mm_dox")
    dq, dkv = xatt_bwd(sv["q"], sv["kv"], dox)
    G["w_cq"] = mm(sv["h2"], dq, "tn", tm=1024, tn=512, tk=1024, out_dtypes=[F32], name="mm_dw_cq")
    G["w_ckv"] = mm(sv["mem_n"], dkv, "tn", tm=1024, tn=1024, tk=256, out_dtypes=[F32], name="mm_dw_ckv")
    dmem_n = mm(dkv, W["w_ckv"], "nt", tm=256, tn=1024, tk=1024, out_dtypes=[F32], name="mm_dmem")
    _, G["g_mem"] = rms_bwd(mem, P["g_mem"], dmem_n, BF16, "rms_mem_bwd")
    dx1, G["g_x_pre"] = mm(dq, W["w_cq"], "nt", tk=512, name="mm_dh2", extras=(sv["x1"], dx2), vecs=(P["g_x_pre"],),
                           **pre_bwd)
    proj = sv["proj"]
    dt1, G["g_mix_post"] = rms_bwd(sv["t1"], P["g_mix_post"], dx1, BF16, "rms_post_bwd")
    G["w_out"] = mm(sv["merged"], dt1, "tn", tm=1024, tn=1024, tk=1024, out_dtypes=[F32], name="mm_dw_out")
    dmerged = mm(dt1, W["w_out"], "nt", tm=1024, tn=1024, tk=1024, out_dtypes=[F32], name="mm_dmerged")
    dgates, dups, G["gate_b"] = merge_bwd(proj, W["gate_b"], sv["branches"], W["w_up"], dmerged)
    dbr, dwup = [], []
    for i in range(4):
        dbr.append(mm(dups[i], W["w_up"][i], "nt", tm=1024, tn=512, tk=1024, out_dtypes=[F32], name="mm_dbranch"))
        dwup.append(mm(sv["branches"][i], dups[i], "tn", tm=512, tn=1024, tk=1024, out_dtypes=[F32], name="mm_dw_up"))
    G["w_up"] = jnp.concatenate(dwup, axis=0)
    d_pool, G["pool_w"], G["pool_scale"] = pool_bwd(proj, P["pool_w"], P["pool_scale"], dbr[0])
    cbar = att_combine_bwd(dbr[1], sv["os"], sv["wts"], headsum)
    dqs, dks, dvs, dbias = [], [], [], []
    for g, (win, dil) in enumerate(DIL_GROUPS):
        dq_g, dk_g, dv_g, db_g = att_bwd(proj, biases[g], sv["lses"][g], sv["wts"][g], dbr[1], cbar, g, dil)
        dqs.append(dq_g)
        dks.append(dk_g)
        dvs.append(dv_g)
        dbias.append(db_g)
    G["att_bias"] = dbias
    s5p = P["s5"]
    dy, G["w_glu"], G["b_glu"] = glu_bwd(sv["y"], W["w_glu"], P["b_glu"], dbr[2])
    d_ssm, dbre, dbim, dar, dai, dcre, dcim, G["d_skip"] = s5_bwd(
        proj, sv["hr"], sv["hi"], dy, s5p[2], s5p[3], s5p[0], s5p[1], s5p[4], s5p[5], P["d_skip"])
    G["s5"] = (dar, dai, dbre, dbim, dcre, dcim)
    dzu, dzv, G["sgu_ln_g"], G["sgu_ln_b"], G["w_s"], G["b_s_t"] = sgu_bwd(
        proj, P["sgu_ln_g"], P["sgu_ln_b"], P["w_s"], P["b_s_t"], dbr[3])
    d_qkv = [d.astype(BF16) for d in dqs + dks + dvs]
    dproj = jnp.concatenate([d_pool] + d_qkv + [d_ssm, dzu, dzv] + list(dgates), axis=1)
    sent = emit(("gate_b", "w_glu", "w_up", "w_out", "w_cq", "w_ckv", "w_co"), G)
    G["w_in"] = mm(sv["h1"], dproj, "tn", tm=1024, tn=1536, tk=1024, out_dtypes=[F32], name="mm_dw_in", after=sent)
    sent = emit(("w_in",), G)
    dx0, G["g_mix_pre"] = mm(dproj, W["w_in"], "nt", tk=1536, name="mm_dh1", extras=(sv["x0"], dx1), vecs=(P["g_mix_pre"],),
                             after=sent, **pre_bwd)
    return dx0, G


def _as3d(name, a):
    shape2d, axis = SHARDED[name]
    rows, cols = shape2d
    if axis == 0:
        rows //= 4
    else:
        cols //= 4
    return a.reshape(DEPTH, rows, cols)


def kernel(x, mem, rel_bias, g_mix_pre, g_mix_post, w_in, gate_b, pool_w, pool_scale, a_re, a_im, log_dt, b_re, b_im, c_re, c_im, d_skip, w_glu, b_glu, sgu_ln_g, sgu_ln_b, w_s, b_s, w_up, w_out, g_x_pre, g_x_post, g_mem, w_cq, w_ckv, w_co, g_ff_pre, g_ff_post, w_ff1, w_ff2, loss_target, m_rel_bias, m_g_mix_pre, m_g_mix_post, m_w_in, m_gate_b, m_pool_w, m_pool_scale, m_a_re, m_a_im, m_log_dt, m_b_re, m_b_im, m_c_re, m_c_im, m_d_skip, m_w_glu, m_b_glu, m_sgu_ln_g, m_sgu_ln_b, m_w_s, m_b_s, m_w_up, m_w_out, m_g_x_pre, m_g_x_post, m_g_mem, m_w_cq, m_w_ckv, m_w_co, m_g_ff_pre, m_g_ff_post, m_w_ff1, m_w_ff2, v_rel_bias, v_g_mix_pre, v_g_mix_post, v_w_in, v_gate_b, v_pool_w, v_pool_scale, v_a_re, v_a_im, v_log_dt, v_b_re, v_b_im, v_c_re, v_c_im, v_d_skip, v_w_glu, v_b_glu, v_sgu_ln_g, v_sgu_ln_b, v_w_s, v_b_s, v_w_up, v_w_out, v_g_x_pre, v_g_x_post, v_g_mem, v_w_cq, v_w_ckv, v_w_co, v_g_ff_pre, v_g_ff_post, v_w_ff1, v_w_ff2):
    env = dict(locals())
    weights = {n: env[n] for n in WEIGHT_NAMES}
    mom_m = {n: env["m_" + n] for n in WEIGHT_NAMES}
    mom_v = {n: env["v_" + n] for n in WEIGHT_NAMES}
    x2d = x.reshape(x.shape[1], D_MODEL)
    mem2d = mem.reshape(N_MEM, D_MODEL)
    target = loss_target.reshape(x2d.shape)

    axis_of = {n: SHARDED[n][1] for n in SHARDED_NAMES}
    chip = _chip_index().astype(jnp.int32).reshape(1)
    rest_names = [n for n in SHARDED_NAMES if n != "w_in"]

    def gather_start(l, names, tag, after=()):
        shards = [_as3d(n, weights[n])[l].astype(F32 if n == "gate_b" else MXU_DTYPE) for n in names]
        ax = [axis_of[n] for n in names]
        lands = [jnp.concatenate([s] * 4, axis=a) for s, a in zip(shards, ax)]
        return (names, ax, tag) + chips_start(shards, lands, ax, "gather", f"gather_start_{tag}", after=after)

    def gather_wait(started, after):
        names, ax, tag, send, recv, shards, lands, _ = started
        _, lands = chips_wait(send, recv, shards, lands, ax, "gather", f"gather_wait_{tag}", after=after)
        W = dict(zip(names, lands))
        if "w_up" in W:
            W["w_up"] = W["w_up"].reshape(4, W_BRANCH, D_MODEL)
        return W

    biases = [att_bias(rel_bias, g, dil) for g, (_, dil) in enumerate(DIL_GROUPS)]
    lanes = np.arange(W_BRANCH) // ATT_HEAD_DIM
    headsum = jnp.asarray(lanes[:, None] == lanes[None, :], dtype=BF16)

    def small_params(l, s5_prepared):
        vec = lambda a: a[l].reshape(1, -1)
        return {
            "g_mix_pre": vec(g_mix_pre), "g_mix_post": vec(g_mix_post), "g_x_pre": vec(g_x_pre), "g_x_post": vec(g_x_post),
            "g_mem": vec(g_mem), "g_ff_pre": vec(g_ff_pre), "g_ff_post": vec(g_ff_post), "pool_w": pool_w[l],
            "pool_scale": vec(pool_scale), "d_skip": vec(d_skip), "b_glu": vec(b_glu), "sgu_ln_g": vec(sgu_ln_g),
            "sgu_ln_b": vec(sgu_ln_b), "w_s": w_s[l], "b_s_t": b_s[l].T, "s5": s5_prepared,
        }

    Ws, Ps, saved, s5_vjps = [], [], [], []
    xl = x2d
    hl = rms_fwd(x2d, g_mix_pre[0].reshape(1, -1), BF16, "rms_pre")
    flying = {"next": gather_start(0, ["w_in"], "0_w_in")}
    for l in range(DEPTH):
        s5_prepared, s5_vjp = jax.vjp(s5_params, a_re[l], a_im[l], log_dt[l], b_re[l], b_im[l], c_re[l], c_im[l])
        token_of = lambda started: (started[7],)
        if l == 0:
            w_in_l = gather_wait(flying["next"], [*biases, hl])["w_in"]
            flying["rest"] = gather_start(0, rest_names, "0_rest", after=[w_in_l])
            first_after = token_of(flying["rest"])

            def rest_of(stage, value):
                if stage != "mixer":
                    return {}, ()
                W = gather_wait(flying["rest"], [value])
                flying["next"] = gather_start(1, SHARDED_NAMES, "1", after=[W["w_out"]])
                return W, token_of(flying["next"])
        else:
            W_l = gather_wait(flying["next"], [xl])
            w_in_l, first_after = W_l["w_in"], ()
            if l + 1 < DEPTH:
                flying["next"] = gather_start(l + 1, SHARDED_NAMES, str(l + 1), after=[w_in_l])
                first_after = token_of(flying["next"])
            rest_of = lambda stage, value, W_l=W_l: (W_l if stage == "mixer" else {}, ())
        P = small_params(l, s5_prepared)
        g_next = g_mix_pre[l + 1].reshape(1, -1) if l + 1 < DEPTH else None
        xl, hl, sv = layer_fwd(xl, hl, mem2d, w_in_l, rest_of, P, biases, g_next, after=first_after)
        Ws.append(sv["W"])
        Ps.append(P)
        saved.append(sv)
        s5_vjps.append(s5_vjp)
    loss_local, dx = loss_and_grad(xl, target)
    loss = lax.psum(loss_local, ("x", "y", "c"))

    scattered = []

    def scatter_start(l, names, srcs):
        ax = [axis_of.get(n) for n in names]
        lands = []
        for s, a in zip(srcs, ax):
            r, c = s.shape
            lands.append(lax.empty((4, r // 4 if a == 0 else r, c // 4 if a == 1 else c), F32))
        tag = f"{l}_{names[0]}"
        send, recv, srcs, lands, token = chips_start(srcs, lands, ax, "scatter", f"grads_start_{tag}")
        scattered.append((l, names, ax, tag, send, recv, srcs, lands))
        return (token,)

    grads = [None] * DEPTH
    for l in reversed(range(DEPTH)):
        emit = lambda names, G, l=l: scatter_start(l, list(names), [G[n] for n in names])
        dx, grads[l] = layer_bwd(dx, mem2d, Ws[l], Ps[l], biases, saved[l], headsum, emit)
    grad_x = dx.reshape(x.shape)

    rep = {}
    stack = lambda key, shape: jnp.stack([grads[l][key] for l in range(DEPTH)]).reshape(shape)
    for n in ("g_mix_pre", "g_mix_post", "g_x_pre", "g_x_post", "g_mem", "g_ff_pre", "g_ff_post"):
        rep[n] = stack(n, (DEPTH, D_MODEL))
    for n in ("pool_scale", "d_skip", "b_glu", "sgu_ln_g", "sgu_ln_b"):
        rep[n] = stack(n, (DEPTH, W_BRANCH))
    rep["pool_w"] = stack("pool_w", pool_w.shape)
    rep["w_s"] = stack("w_s", w_s.shape)
    rep["b_s"] = jnp.stack([grads[l]["b_s_t"][:, :4].T for l in range(DEPTH)])
    s5_grads = [s5_vjps[l](tuple(grads[l]["s5"])) for l in range(DEPTH)]
    for i, n in enumerate(("a_re", "a_im", "log_dt", "b_re", "b_im", "c_re", "c_im")):
        rep[n] = jnp.stack([s5_grads[l][i] for l in range(DEPTH)])
    dbias = [sum(grads[l]["att_bias"][g] for l in range(DEPTH)) for g in range(len(DIL_GROUPS))]
    rep["rel_bias"] = jnp.concatenate([att_bias_grad(dbias[g], dil) for g, (_, dil) in enumerate(DIL_GROUPS)], axis=1)
    rep_shapes = [weights[n].shape for n in REPLICATED_NAMES]
    packed_g = pack_small([rep[n] for n in REPLICATED_NAMES])

    small_sent = scatter_start(0, ["small"], [packed_g])
    stacked = {}

    def collect(record, after):
        l, names, ax, tag, send, recv, srcs, lands = record
        srcs, lands = chips_wait(send, recv, srcs, lands, ax, "scatter", f"grads_wait_{tag}", after=after)
        for n, own, arrived, a in zip(names, srcs, lands, ax):
            if n not in stacked:
                stacked[n] = lax.empty((1 if n == "small" else DEPTH,) + arrived.shape[1:], F32)
            stacked[n] = sum_chips(arrived, own, a, chip, stacked[n], 0 if n == "small" else l, "sum_chips")

    out_g, out_d, out_m, out_v = {}, {}, {}, {}

    def update(names, tag):
        partial = [stacked[n].reshape(-1, stacked[n].shape[-1]) for n in names]
        other = swap_cores(partial, f"swap_cores_{tag}")
        for n, mine, theirs in zip(names, partial, other):
            if n == "small":
                for name, ga, gb in zip(REPLICATED_NAMES, unpack_small(mine, rep_shapes), unpack_small(theirs, rep_shapes)):
                    rows_of = lambda a: a.reshape(-1, a.shape[-1])
                    res = adamw(rows_of(weights[name]), rows_of(ga), rows_of(gb), rows_of(mom_m[name]),
                                rows_of(mom_v[name]), "adamw_small")
                    out_g[name], out_d[name], out_m[name], out_v[name] = [r.reshape(weights[name].shape) for r in res]
            else:
                flat = lambda a: a.reshape(mine.shape)
                res = adamw(flat(weights[n]), mine, theirs, flat(mom_m[n]), flat(mom_v[n]), "adamw")
                out_g[n], out_d[n], out_m[n], out_v[n] = [r.reshape(weights[n].shape) for r in res]

    late = [r for r in scattered if r[1] == ["small"] or (r[0] == 0 and r[1] == ["w_in"])]
    for record in scattered:
        if not any(record is r for r in late):
            collect(record, [dx, *small_sent])
    update(rest_names, "rest")
    collect(late[0], [out_d[n] for n in rest_names])
    update(["w_in"], "w_in")
    collect(late[1], [out_d["w_in"]])
    update(["small"], "small")

    return (loss, grad_x, *[out_g[n] for n in WEIGHT_NAMES], *[out_d[n] for n in WEIGHT_NAMES],
            *[out_m[n] for n in WEIGHT_NAMES], *[out_v[n] for n in WEIGHT_NAMES])
```

```python
import functools
import math

import numpy as np
import jax
import jax.numpy as jnp
from jax import lax
from jax.experimental import pallas as pl
from jax.experimental.pallas import tpu as pltpu

F32 = jnp.float32
BF16 = jnp.bfloat16
MXU_DTYPE = jnp.bfloat16
MESH_ID = pl.DeviceIdType.MESH
VMEM_LIMIT_BYTES = 56 * 1024 * 1024

D_MODEL = 1024
DEPTH = 4
N_MEM = 256
W_BRANCH = 512
POOL_WINDOWS = (2, 4, 8, 16)
POOL_HALO = 16
DIL_GROUPS = ((128, 1), (512, 4), (2048, 16))
BAND = 128
ATT_HEADS = 8
ATT_HEAD_DIM = 64
SSM_GROUP = 16
SSM_GROUPS = 32
SSM_STATE = 64
SSM_COLS = SSM_GROUPS * SSM_STATE
SSM_T = 512
SGU_CHUNK = 128
X_HEADS = 4
X_HEAD_DIM = 128
D_FF = 4096
REL_BUCKETS = 32
REL_MAX_DIST = 2048
EPS = 1e-6
NEG_INF = -1e30
OFF_POOL = 0
OFF_ATT = 512
OFF_SSM = OFF_ATT + 9 * W_BRANCH
OFF_SGU = OFF_SSM + W_BRANCH
OFF_GATE = OFF_SGU + 2 * W_BRANCH
IN_WIDTH = OFF_GATE + 4 * D_MODEL

ADAM_LR = 0.001
ADAM_B1 = 0.9
ADAM_B2 = 0.999
ADAM_EPS = 1e-08
ADAM_WD = 0.01
ADAM_STEP = 10

GELU_C = math.sqrt(2.0 / math.pi)

WEIGHT_NAMES = ['rel_bias', 'g_mix_pre', 'g_mix_post', 'w_in', 'gate_b', 'pool_w', 'pool_scale', 'a_re', 'a_im',
                'log_dt', 'b_re', 'b_im', 'c_re', 'c_im', 'd_skip', 'w_glu', 'b_glu', 'sgu_ln_g', 'sgu_ln_b',
                'w_s', 'b_s', 'w_up', 'w_out', 'g_x_pre', 'g_x_post', 'g_mem', 'w_cq', 'w_ckv', 'w_co',
                'g_ff_pre', 'g_ff_post', 'w_ff1', 'w_ff2']
SHARDED = {
    'w_in': ((D_MODEL, IN_WIDTH), 1),
    'gate_b': ((4, D_MODEL), 1),
    'w_glu': ((W_BRANCH, W_BRANCH), 0),
    'w_up': ((4 * W_BRANCH, D_MODEL), 1),
    'w_out': ((D_MODEL, D_MODEL), 0),
    'w_cq': ((D_MODEL, W_BRANCH), 0),
    'w_ckv': ((D_MODEL, D_MODEL), 0),
    'w_co': ((W_BRANCH, D_MODEL), 1),
    'w_ff1': ((D_MODEL, D_FF), 1),
    'w_ff2': ((D_FF, D_MODEL), 0),
}
SHARDED_NAMES = list(SHARDED)
REPLICATED_NAMES = [n for n in WEIGHT_NAMES if n not in SHARDED]


def _cp(n_axes):
    return pltpu.CompilerParams(dimension_semantics=("arbitrary",) * n_axes, vmem_limit_bytes=VMEM_LIMIT_BYTES)


def _dot(a, b, dims="nn"):
    cd = {"nn": ((1,), (0,)), "nt": ((1,), (1,)), "tn": ((0,), (0,))}[dims]
    return lax.dot_general(a.astype(MXU_DTYPE), b.astype(MXU_DTYPE), (cd, ((), ())), preferred_element_type=F32)


def _gelu(x):
    return 0.5 * x * (1.0 + jnp.tanh(GELU_C * (x + 0.044715 * (x * x * x))))


def _gelu_grad(x):
    t = jnp.tanh(GELU_C * (x + 0.044715 * (x * x * x)))
    return 0.5 * (1.0 + t) + 0.5 * x * (1.0 - t * t) * (GELU_C * (1.0 + 3.0 * 0.044715 * (x * x)))


def _sigmoid(x):
    return 1.0 / (1.0 + jnp.exp(-x))


MM_TILES = {
    "mm_w_in": (2048, 1536, 1024), "mm_dw_in": (1024, 1536, 2048), "mm_dh1": (1024, 1024, 1536),
    "mm_w_ff1": (2048, 1024, 1024), "mm_w_ff2": (1024, 1024, 2048), "mm_w_ff2_last": (1024, 1024, 2048),
    "mm_dw_ff2": (1024, 1024, 2048), "mm_dact": (2048, 1024, 1024), "mm_dw_ff1": (1024, 1024, 2048),
    "mm_dh3": (1024, 1024, 2048), "mm_dh2": (1024, 1024, 512),
}


def mm(a, b, dims, *, out_dtypes, name, tm=None, tn=None, tk=None, extras=(), vecs=(), epi=None, n_sums=0, after=()):
    if dims == "tn":
        K, M = a.shape
        N = b.shape[1]
    else:
        M, K = a.shape
        N = b.shape[1] if dims == "nn" else b.shape[0]
    if tm is None:
        tm, tn, tk = MM_TILES[name]
    tm, tn, tk = min(tm, M), min(tn, N), min(tk, K)
    assert M % tm == 0 and N % tn == 0 and K % tk == 0, (name, M, N, K, tm, tn, tk)
    assert n_sums == 0 or tn == N, name
    nk = K // tk
    ne, no = len(extras) + len(vecs), len(out_dtypes)
    if epi is None:
        epi = lambda acc: (acc,)
    a_spec = (pl.BlockSpec((tk, tm), lambda i, j, k: (k, i)) if dims == "tn"
              else pl.BlockSpec((tm, tk), lambda i, j, k: (i, k)))
    b_spec = (pl.BlockSpec((tn, tk), lambda i, j, k: (j, k)) if dims == "nt"
              else pl.BlockSpec((tk, tn), lambda i, j, k: (k, j)))
    mn_spec = pl.BlockSpec((tm, tn), lambda i, j, k: (i, j))
    vec_spec = pl.BlockSpec((1, tn), lambda i, j, k: (0, j))

    def body(a_ref, b_ref, *rest):
        first_out = ne + len(after)
        extra_refs, out_refs = rest[:ne], rest[first_out:first_out + no]
        sum_refs = rest[first_out + no:first_out + no + n_sums]
        part = _dot(a_ref[...], b_ref[...], dims)

        def finish(acc):
            results = epi(acc, *[e[...] for e in extra_refs])
            for o_ref, r in zip(out_refs, results[:no]):
                o_ref[...] = r.astype(o_ref.dtype)
            for s_ref, r in zip(sum_refs, results[no:]):
                @pl.when(pl.program_id(0) == 0)
                def _():
                    s_ref[...] = r

                @pl.when(pl.program_id(0) > 0)
                def _():
                    s_ref[...] += r

        if nk == 1:
            finish(part)
        else:
            acc_ref = rest[-1]
            k = pl.program_id(2)

            @pl.when(k == 0)
            def _():
                acc_ref[...] = part

            @pl.when(k > 0)
            def _():
                acc_ref[...] += part

            @pl.when(k == nk - 1)
            def _():
                finish(acc_ref[...])

    outs = pl.pallas_call(
        body, name=name, grid=(M // tm, N // tn, nk),
        in_specs=[a_spec, b_spec] + [mn_spec] * len(extras) + [vec_spec] * len(vecs) + [ANY] * len(after),
        out_specs=[mn_spec] * no + [vec_spec] * n_sums,
        out_shape=[jax.ShapeDtypeStruct((M, N), dt) for dt in out_dtypes] + [jax.ShapeDtypeStruct((1, N), F32)] * n_sums,
        scratch_shapes=[pltpu.VMEM((tm, tn), F32)] if nk > 1 else [],
        compiler_params=_cp(3),
    )(a, b, *extras, *vecs, *after)
    return outs[0] if no + n_sums == 1 else outs


ROW_TILE = 512


def rms_fwd(x, g, out_dtype, name, res=None):
    M, D = x.shape
    tm = min(ROW_TILE, M)

    def body(x_ref, g_ref, *rest):
        o_ref = rest[-1]
        xf = x_ref[...]
        y = xf * lax.rsqrt(jnp.mean(xf * xf, axis=-1, keepdims=True) + EPS) * g_ref[...]
        if res is not None:
            y = y + rest[0][...]
        o_ref[...] = y.astype(o_ref.dtype)

    row = pl.BlockSpec((tm, D), lambda i: (i, 0))
    return pl.pallas_call(
        body, name=name, grid=(M // tm,),
        in_specs=[row, pl.BlockSpec((1, D), lambda i: (0, 0))] + ([row] if res is not None else []),
        out_specs=row, out_shape=jax.ShapeDtypeStruct((M, D), out_dtype), compiler_params=_cp(1),
    )(x, g, *([res] if res is not None else []))


def rms_bwd(x, g, dy, dx_dtype, name, add=None, after=()):
    M, D = x.shape
    tm = min(ROW_TILE, M)

    def body(x_ref, g_ref, dy_ref, *rest):
        dx_ref, dg_ref = rest[-2], rest[-1]
        xf = x_ref[...]
        dyf = dy_ref[...].astype(F32)
        r = lax.rsqrt(jnp.mean(xf * xf, axis=-1, keepdims=True) + EPS)
        xn = xf * r
        dxn = dyf * g_ref[...]
        dx = r * (dxn - xn * jnp.mean(dxn * xn, axis=-1, keepdims=True))
        if add is not None:
            dx = dx + rest[0][...]
        dx_ref[...] = dx.astype(dx_ref.dtype)
        dg = jnp.sum(dyf * xn, axis=0, keepdims=True)

        @pl.when(pl.program_id(0) == 0)
        def _():
            dg_ref[...] = dg

        @pl.when(pl.program_id(0) > 0)
        def _():
            dg_ref[...] += dg

    row = pl.BlockSpec((tm, D), lambda i: (i, 0))
    vec = pl.BlockSpec((1, D), lambda i: (0, 0))
    return pl.pallas_call(
        body, name=name, grid=(M // tm,),
        in_specs=[row, vec, row] + ([row] if add is not None else []) + [ANY] * len(after),
        out_specs=[row, vec],
        out_shape=[jax.ShapeDtypeStruct((M, D), dx_dtype), jax.ShapeDtypeStruct((1, D), F32)],
        compiler_params=_cp(1),
    )(x, g, dy, *([add] if add is not None else []), *after)


def loss_and_grad(y, target):
    M, D = y.shape
    tm = ROW_TILE

    def body(y_ref, t_ref, part_ref, dy_ref):
        e = y_ref[...] - t_ref[...]
        dy_ref[...] = e / D
        part_ref[...] = jnp.broadcast_to(0.5 * jnp.sum(jnp.mean(e * e, axis=-1, keepdims=True), axis=0, keepdims=True),
                                         (8, 128))

    row = pl.BlockSpec((tm, D), lambda i: (i, 0))
    part, dy = pl.pallas_call(
        body, name="loss", grid=(M // tm,), in_specs=[row, row],
        out_specs=[pl.BlockSpec((8, 128), lambda i: (i, 0)), row],
        out_shape=[jax.ShapeDtypeStruct((8 * (M // tm), 128), F32), jax.ShapeDtypeStruct((M, D), F32)],
        compiler_params=_cp(1),
    )(y, target)
    return jnp.sum(part[::8, 0]), dy


POOL_ROWS = 512


def _pool_window_sum(xw, gi, roll_of):
    s1 = xw + pltpu.roll(xw, roll_of(1), 0)
    s2 = s1 + pltpu.roll(s1, roll_of(2), 0)
    s3 = s2 + pltpu.roll(s2, roll_of(4), 0)
    s4 = s3 + pltpu.roll(s3, roll_of(8), 0)
    return jnp.where(gi == 0, s1, jnp.where(gi == 1, s2, jnp.where(gi == 2, s3, s4)))


def _pool_cnt(i, gi):
    rows = lax.broadcasted_iota(jnp.int32, (POOL_ROWS, 128), 0) + i * POOL_ROWS
    w = jnp.where(gi == 0, 2, jnp.where(gi == 1, 4, jnp.where(gi == 2, 8, 16)))
    return jnp.minimum(rows + 1, w).astype(F32)


def pool_fwd(proj, pool_w, scale):
    S = proj.shape[0]
    nchunk = S // POOL_ROWS
    slab = POOL_ROWS + POOL_HALO

    def body(x_ref, w_ref, sc_ref, o_ref, pad_ref):
        gi = pl.program_id(0)
        pad_ref[0:POOL_HALO, :] = jnp.zeros((POOL_HALO, 128), F32)
        pad_ref[POOL_HALO:, :] = x_ref[...]
        for i in range(nchunk):
            xw = pad_ref[i * POOL_ROWS:i * POOL_ROWS + slab, :]
            ssum = _pool_window_sum(xw, gi, lambda d: d)[POOL_HALO:, :]
            p = ssum / _pool_cnt(i, gi) - xw[POOL_HALO:, :]
            o_ref[i * POOL_ROWS:(i + 1) * POOL_ROWS, :] = (_dot(p, w_ref[...]) * sc_ref[...]).astype(o_ref.dtype)

    return pl.pallas_call(
        body, name="pool_fwd", grid=(4,),
        in_specs=[pl.BlockSpec((S, 128), lambda g: (0, OFF_POOL // 128 + g)),
                  pl.BlockSpec((None, 128, 128), lambda g: (g, 0, 0)),
                  pl.BlockSpec((1, 128), lambda g: (0, g))],
        out_specs=pl.BlockSpec((S, 128), lambda g: (0, g)),
        out_shape=jax.ShapeDtypeStruct((S, W_BRANCH), BF16),
        scratch_shapes=[pltpu.VMEM((S + POOL_HALO, 128), F32)],
        compiler_params=_cp(1),
    )(proj, pool_w, scale)


def pool_bwd(proj, pool_w, scale, dy):
    S = proj.shape[0]
    nchunk = S // POOL_ROWS
    slab = POOL_ROWS + POOL_HALO

    def body(x_ref, w_ref, sc_ref, dy_ref, dx_ref, dw_ref, dsc_ref, pad_ref, pad2_ref, dp_ref):
        gi = pl.program_id(0)
        pad_ref[0:POOL_HALO, :] = jnp.zeros((POOL_HALO, 128), F32)
        pad_ref[POOL_HALO:, :] = x_ref[...]
        pad2_ref[S:, :] = jnp.zeros((POOL_HALO, 128), F32)
        dw = jnp.zeros((128, 128), F32)
        dsc = jnp.zeros((1, 128), F32)
        for i in range(nchunk):
            xw = pad_ref[i * POOL_ROWS:i * POOL_ROWS + slab, :]
            cnt = _pool_cnt(i, gi)
            p = _pool_window_sum(xw, gi, lambda d: d)[POOL_HALO:, :] / cnt - xw[POOL_HALO:, :]
            dyc = dy_ref[i * POOL_ROWS:(i + 1) * POOL_ROWS, :]
            dsc = dsc + jnp.sum(dyc * _dot(p, w_ref[...]), axis=0, keepdims=True)
            dys = dyc * sc_ref[...]
            dw = dw + _dot(p, dys, "tn")
            dp = _dot(dys, w_ref[...], "nt")
            dp_ref[i * POOL_ROWS:(i + 1) * POOL_ROWS, :] = dp
            pad2_ref[i * POOL_ROWS:(i + 1) * POOL_ROWS, :] = dp / cnt
        dw_ref[...] = dw
        dsc_ref[...] = dsc
        for i in range(nchunk):
            xw = pad2_ref[i * POOL_ROWS:i * POOL_ROWS + slab, :]
            fsum = _pool_window_sum(xw, gi, lambda d: slab - d)[:POOL_ROWS, :]
            rows = slice(i * POOL_ROWS, (i + 1) * POOL_ROWS)
            dx_ref[rows, :] = (fsum - dp_ref[rows, :]).astype(dx_ref.dtype)

    return pl.pallas_call(
        body, name="pool_bwd", grid=(4,),
        in_specs=[pl.BlockSpec((S, 128), lambda g: (0, OFF_POOL // 128 + g)),
                  pl.BlockSpec((None, 128, 128), lambda g: (g, 0, 0)),
                  pl.BlockSpec((1, 128), lambda g: (0, g)),
                  pl.BlockSpec((S, 128), lambda g: (0, g))],
        out_specs=[pl.BlockSpec((S, 128), lambda g: (0, g)),
                   pl.BlockSpec((None, 128, 128), lambda g: (g, 0, 0)),
                   pl.BlockSpec((1, 128), lambda g: (0, g))],
        out_shape=[jax.ShapeDtypeStruct((S, W_BRANCH), BF16), jax.ShapeDtypeStruct((4, 128, 128), F32),
                   jax.ShapeDtypeStruct((1, W_BRANCH), F32)],
        scratch_shapes=[pltpu.VMEM((S + POOL_HALO, 128), F32), pltpu.VMEM((S + POOL_HALO, 128), F32),
                        pltpu.VMEM((S, 128), F32)],
        compiler_params=_cp(1),
    )(proj, pool_w, scale, dy)


def _t5_bucket(n):
    exact = REL_BUCKETS // 2
    nf = np.maximum(n, 1).astype(np.float32)
    large = exact + (np.log(nf / exact) / np.log(REL_MAX_DIST / exact) * (REL_BUCKETS - exact)).astype(np.int32)
    large = np.minimum(large, REL_BUCKETS - 1)
    return np.where(n < exact, n, large).astype(np.int32)


def _band_onehot(dil):
    i = np.arange(BAND)[:, None]
    kk = np.arange(2 * BAND)[None, :]
    dist = BAND + i - kk
    local = (dist >= 0) & (dist <= BAND)
    bucket = _t5_bucket(np.clip(dist, 0, BAND) * dil)
    onehot = (bucket.reshape(-1, 1) == np.arange(REL_BUCKETS)[None, :]).astype(np.float32)
    return onehot, local


def att_bias(rel_bias, g, dil):
    onehot, local = _band_onehot(dil)
    tab = jnp.dot(jnp.asarray(onehot), rel_bias[:, g * ATT_HEADS:(g + 1) * ATT_HEADS], precision=lax.Precision.HIGHEST)
    bias = tab.reshape(BAND, 2 * BAND, ATT_HEADS).transpose(2, 0, 1)
    return jnp.where(jnp.asarray(local)[None], bias, NEG_INF)


def att_bias_grad(dbias, dil):
    onehot, _ = _band_onehot(dil)
    flat = dbias.transpose(1, 2, 0).reshape(BAND * 2 * BAND, ATT_HEADS)
    return jnp.dot(jnp.asarray(onehot).T, flat, precision=lax.Precision.HIGHEST)


def _head_lanes():
    return lax.broadcasted_iota(jnp.int32, (BAND, 128), 1) < ATT_HEAD_DIM


def _att_cols(part, g, hp):
    return (OFF_ATT + part * 3 * W_BRANCH + g * W_BRANCH) // 128 + hp


def _att_pair(q, k, v, bias, lse_b, do, delta_b, hh, head0, mask=None):
    sel = head0 if hh == 0 else jnp.logical_not(head0)
    s = _dot(jnp.where(sel, q, 0.0), k, "nt") * (ATT_HEAD_DIM ** -0.5) + bias
    if mask is not None:
        s = jnp.where(mask, NEG_INF, s)
    c = hh * ATT_HEAD_DIM
    p = jnp.exp(s - lse_b[:, c:c + 1])
    dp = _dot(jnp.where(sel, do, 0.0), v, "nt")
    return p, p * (dp - delta_b[:, c:c + 1])


ATT_BLOCKS = {1: 8, 4: 2, 16: 1}


def _att_rows(r, i, d):
    return pl.ds(r + d * BAND * i, BAND, stride=d) if d > 1 else pl.ds(BAND * i, BAND)


def _att_specs(g, d, nq):
    ch, pb = BAND * d * nq, BAND * d
    cur = lambda part: pl.BlockSpec((ch, 128), lambda hp, n: (n, _att_cols(part, g, hp)))
    prev = lambda part: pl.BlockSpec((pb, 128), lambda hp, n: (jnp.maximum(n * nq - 1, 0), _att_cols(part, g, hp)))
    return [cur(0), cur(1), prev(1), cur(2), prev(2)]


def _att_keys(cur_ref, prev_ref, r, i, d):
    before = cur_ref[_att_rows(r, i - 1, d), :] if i > 0 else prev_ref[_att_rows(r, 0, d), :]
    return jnp.concatenate([before, cur_ref[_att_rows(r, i, d), :]], axis=0).astype(MXU_DTYPE)


def att_fwd(proj, bias, g, d):
    S = proj.shape[0]
    nq = ATT_BLOCKS[d]
    ch = BAND * d * nq

    def body(q_ref, kc_ref, kp_ref, vc_ref, vp_ref, b_ref, o_ref, l_ref):
        n = pl.program_id(1)
        head0 = _head_lanes()
        first = jnp.logical_and(lax.broadcasted_iota(jnp.int32, (BAND, 2 * BAND), 1) < BAND, n == 0)
        for r in range(d):
            for i in range(nq):
                rows = _att_rows(r, i, d)
                q = q_ref[rows, :]
                k = _att_keys(kc_ref, kp_ref, r, i, d)
                v = _att_keys(vc_ref, vp_ref, r, i, d)
                o_h, l_h = [], []
                for hh in range(2):
                    qm = jnp.where(head0 if hh == 0 else jnp.logical_not(head0), q, 0.0)
                    s = _dot(qm, k, "nt") * (ATT_HEAD_DIM ** -0.5) + b_ref[hh]
                    if i == 0:
                        s = jnp.where(first, NEG_INF, s)
                    m = jnp.max(s, axis=-1, keepdims=True)
                    p = jnp.exp(s - m)
                    l = jnp.sum(p, axis=-1, keepdims=True)
                    o_h.append(_dot(p / l, v))
                    l_h.append(jnp.broadcast_to(m + jnp.log(l), (BAND, 128)))
                o_ref[rows, :] = jnp.where(head0, o_h[0], o_h[1])
                l_ref[rows, :] = jnp.where(head0, l_h[0], l_h[1])

    out = pl.BlockSpec((ch, 128), lambda hp, n: (n, hp))
    return pl.pallas_call(
        body, name=f"att_fwd_d{d}", grid=(4, S // ch),
        in_specs=_att_specs(g, d, nq) + [pl.BlockSpec((2, BAND, 2 * BAND), lambda hp, n: (hp, 0, 0))],
        out_specs=[out, out],
        out_shape=[jax.ShapeDtypeStruct((S, W_BRANCH), F32), jax.ShapeDtypeStruct((S, W_BRANCH), F32)],
        compiler_params=_cp(2),
    )(proj, proj, proj, proj, proj, bias)


def att_bwd(proj, bias, lse, wts, dout, cbar, g, d):
    S = proj.shape[0]
    nq = ATT_BLOCKS[d]
    ch, pb = BAND * d * nq, BAND * d
    nb = S // ch
    scale = ATT_HEAD_DIM ** -0.5

    def body(q_ref, kc_ref, kp_ref, vc_ref, vp_ref, b_ref, l_ref, w_ref, do_ref, cb_ref,
             dq_ref, dk_ref, dv_ref, ek_ref, ev_ref, db_ref):
        n = pl.program_id(1)
        head0 = _head_lanes()
        first = jnp.logical_and(lax.broadcasted_iota(jnp.int32, (BAND, 2 * BAND), 1) < BAND, n == 0)

        @pl.when(n == 0)
        def _():
            db_ref[...] = jnp.zeros(db_ref.shape, F32)

        for r in range(d):
            own_k = own_v = None
            for i in range(nq):
                rows = _att_rows(r, i, d)
                q = q_ref[rows, :]
                k = _att_keys(kc_ref, kp_ref, r, i, d)
                v = _att_keys(vc_ref, vp_ref, r, i, d)
                w = w_ref[rows, :]
                do = w * do_ref[rows, :]
                delta = w * cb_ref[rows, :]
                lse_b = l_ref[rows, :]
                dq_h, dk_h, dv_h = [], [], []
                for hh in range(2):
                    p, ds = _att_pair(q, k, v, b_ref[hh], lse_b, do, delta, hh, head0, mask=first if i == 0 else None)
                    db_ref[hh] += ds
                    ds = ds * scale
                    dq_h.append(_dot(ds, k))
                    dk_h.append(_dot(ds, q, "tn"))
                    dv_h.append(_dot(p, do, "tn"))
                dq_ref[rows, :] = jnp.where(head0, dq_h[0], dq_h[1])
                head0_keys = jnp.concatenate([head0, head0], axis=0)
                dk2 = jnp.where(head0_keys, dk_h[0], dk_h[1])
                dv2 = jnp.where(head0_keys, dv_h[0], dv_h[1])
                if i == 0:
                    ek_ref[_att_rows(r, 0, d), :] = dk2[:BAND]
                    ev_ref[_att_rows(r, 0, d), :] = dv2[:BAND]
                else:
                    dk_ref[_att_rows(r, i - 1, d), :] = own_k + dk2[:BAND]
                    dv_ref[_att_rows(r, i - 1, d), :] = own_v + dv2[:BAND]
                own_k, own_v = dk2[BAND:], dv2[BAND:]
            dk_ref[_att_rows(r, nq - 1, d), :] = own_k
            dv_ref[_att_rows(r, nq - 1, d), :] = own_v

    cur = pl.BlockSpec((ch, 128), lambda hp, n: (n, hp))
    edge = pl.BlockSpec((pb, 128), lambda hp, n: (n, hp))
    bias_spec = pl.BlockSpec((2, BAND, 2 * BAND), lambda hp, n: (hp, 0, 0))
    big = jax.ShapeDtypeStruct((S, W_BRANCH), F32)
    small = jax.ShapeDtypeStruct((nb * pb, W_BRANCH), F32)
    dq, dk, dv, ek, ev, db = pl.pallas_call(
        body, name=f"att_bwd_d{d}", grid=(4, nb),
        in_specs=_att_specs(g, d, nq) + [bias_spec, cur, cur, cur, cur],
        out_specs=[cur, cur, cur, edge, edge, bias_spec],
        out_shape=[big, big, big, small, small, jax.ShapeDtypeStruct((ATT_HEADS, BAND, 2 * BAND), F32)],
        compiler_params=_cp(2),
    )(proj, proj, proj, proj, proj, bias, lse, wts, dout, cbar)

    def with_edges(main, edges):
        if nb == 1:
            return main
        main = main.reshape(nb, ch, W_BRANCH)
        add = jnp.pad(edges.reshape(nb, pb, W_BRANCH)[1:], ((0, 1), (ch - pb, 0), (0, 0)))
        return (main + add).reshape(S, W_BRANCH)

    return dq, with_edges(dk, ek), with_edges(dv, ev), db


def att_combine(os_, lses):
    S = os_[0].shape[0]

    def body(o0, o1, o2, l0, l1, l2, out_ref, w0, w1, w2):
        ls = [l0[...], l1[...], l2[...]]
        m = jnp.maximum(jnp.maximum(ls[0], ls[1]), ls[2])
        es = [jnp.exp(l - m) for l in ls]
        den = es[0] + es[1] + es[2]
        ws = [e / den for e in es]
        out_ref[...] = (ws[0] * o0[...] + ws[1] * o1[...] + ws[2] * o2[...]).astype(out_ref.dtype)
        for w_ref, w in zip((w0, w1, w2), ws):
            w_ref[...] = w

    blk = pl.BlockSpec((ROW_TILE, W_BRANCH), lambda i: (i, 0))
    f = jax.ShapeDtypeStruct((S, W_BRANCH), F32)
    return pl.pallas_call(
        body, name="att_combine", grid=(S // ROW_TILE,), in_specs=[blk] * 6, out_specs=[blk] * 4,
        out_shape=[jax.ShapeDtypeStruct((S, W_BRANCH), BF16), f, f, f], compiler_params=_cp(1),
    )(*os_, *lses)


def _split3(x):
    x1 = x.astype(BF16)
    r1 = x - x1.astype(F32)
    x2 = r1.astype(BF16)
    x3 = (r1 - x2.astype(F32)).astype(BF16)
    return x1, x2, x3


def att_combine_bwd(dout, os_, wts, headsum):
    S = dout.shape[0]

    def body(do_ref, o0, o1, o2, w0, w1, w2, e_ref, cb_ref):
        out = w0[...] * o0[...] + w1[...] * o1[...] + w2[...] * o2[...]
        e = e_ref[...]
        acc = jnp.zeros((ROW_TILE, W_BRANCH), F32)
        for term in _split3(do_ref[...] * out):
            acc = acc + jnp.dot(term, e, preferred_element_type=F32)
        cb_ref[...] = acc

    blk = pl.BlockSpec((ROW_TILE, W_BRANCH), lambda i: (i, 0))
    return pl.pallas_call(
        body, name="att_combine_bwd", grid=(S // ROW_TILE,),
        in_specs=[blk] * 7 + [pl.BlockSpec((W_BRANCH, W_BRANCH), lambda i: (0, 0))], out_specs=blk,
        out_shape=jax.ShapeDtypeStruct((S, W_BRANCH), F32), compiler_params=_cp(1),
    )(dout, *os_, *wts, headsum)


def _cmul(ar, ai, br, bi):
    return ar * br - ai * bi, ar * bi + ai * br


SCAN_ROWS = 8
SCAN_GROUPS = SSM_T // SCAN_ROWS


def _log_scan(xr, xi, mr, mi, rows, n, steps, reverse):
    total = xr.shape[0]
    for k in range(steps):
        dd = 1 << k
        keep = rows < n - dd if reverse else rows >= dd
        shift = total - dd if reverse else dd
        ar, ai = _cmul(mr, mi, jnp.where(keep, pltpu.roll(xr, shift, 0), 0.0), jnp.where(keep, pltpu.roll(xi, shift, 0), 0.0))
        xr, xi = xr + ar, xi + ai
        mr, mi = _cmul(mr, mi, mr, mi)
    return xr, xi, mr, mi


def _scan_scratch(n_results):
    return ([pltpu.VMEM((W_BRANCH // 128, SSM_T, 128), F32)] * 2 + [pltpu.VMEM((SCAN_GROUPS, W_BRANCH), F32)] * 2
            + [pltpu.VMEM((SSM_T, W_BRANCH), F32)] * n_results)


def _block_scan(xr, xi, mr, mi, reverse, yr_ref, yi_ref, er_ref, ei_ref, hr_ref, hi_ref):
    cols = xr.shape[1]
    rows = lax.broadcasted_iota(jnp.int32, (SSM_T, cols), 0)
    yr, yi, m8r, m8i = _log_scan(xr, xi, mr, mi, rows & (SCAN_ROWS - 1), SCAN_ROWS, 3, reverse)
    lane_blocks = range(cols // 128)
    for c in lane_blocks:
        yr_ref[c] = yr[:, c * 128:(c + 1) * 128]
        yi_ref[c] = yi[:, c * 128:(c + 1) * 128]
    wide = lambda ref, rows_: jnp.concatenate([ref[c, rows_, :] for c in lane_blocks], axis=1)
    end = pl.ds(0 if reverse else SCAN_ROWS - 1, SCAN_GROUPS, stride=SCAN_ROWS)
    groups = lax.broadcasted_iota(jnp.int32, (SCAN_GROUPS, cols), 0)
    er, ei, _, _ = _log_scan(wide(yr_ref, end), wide(yi_ref, end), m8r, m8i, groups, SCAN_GROUPS,
                             int(math.log2(SCAN_GROUPS)), reverse)
    er_ref[...] = er
    ei_ref[...] = ei
    j = lax.broadcasted_iota(jnp.int32, (SCAN_ROWS, cols), 0)
    dist = SCAN_ROWS - j if reverse else j + 1
    tr, ti = jnp.ones((SCAN_ROWS, cols), F32), jnp.zeros((SCAN_ROWS, cols), F32)
    br, bi = mr, mi
    for bit in range(4):
        nr, ni = _cmul(tr, ti, br, bi)
        take = ((dist >> bit) & 1) == 1
        tr, ti = jnp.where(take, nr, tr), jnp.where(take, ni, ti)
        br, bi = _cmul(br, bi, br, bi)
    for g in range(SCAN_GROUPS):
        before = g + 1 if reverse else g - 1
        rows_g = slice(g * SCAN_ROWS, (g + 1) * SCAN_ROWS)
        if 0 <= before < SCAN_GROUPS:
            ar, ai = _cmul(tr, ti, er_ref[before:before + 1, :], ei_ref[before:before + 1, :])
            hr_ref[rows_g, :] = wide(yr_ref, rows_g) + ar
            hi_ref[rows_g, :] = wide(yi_ref, rows_g) + ai
        else:
            hr_ref[rows_g, :] = wide(yr_ref, rows_g)
            hi_ref[rows_g, :] = wide(yi_ref, rows_g)
    last = 0 if reverse else SCAN_GROUPS - 1
    return er_ref[last:last + 1, :], ei_ref[last:last + 1, :]


def s5_fwd(proj, b_re, b_im, a_re, a_im, c_re, c_im, d_skip):
    S = proj.shape[0]
    nt = S // SSM_T

    def body(u_ref, bre_ref, bim_ref, ar_ref, ai_ref, cre_ref, cim_ref, dsk_ref, hr_ref, hi_ref, y_ref, cr_ref, ci_ref,
             yr_ref, yi_ref, er_ref, ei_ref):
        t = pl.program_id(1)

        @pl.when(t == 0)
        def _():
            cr_ref[...] = jnp.zeros(cr_ref.shape, F32)
            ci_ref[...] = jnp.zeros(ci_ref.shape, F32)

        u = u_ref[...]
        ar, ai = ar_ref[...], ai_ref[...]
        rows = lax.broadcasted_iota(jnp.int32, (SSM_T, W_BRANCH), 0)
        inr, ini = _cmul(ar, ai, cr_ref[0:1, :], ci_ref[0:1, :])
        xr = _dot(u, bre_ref[...]) + jnp.where(rows == 0, inr, 0.0)
        xi = _dot(u, bim_ref[...]) + jnp.where(rows == 0, ini, 0.0)
        endr, endi = _block_scan(xr, xi, ar, ai, False, yr_ref, yi_ref, er_ref, ei_ref, hr_ref, hi_ref)
        cr_ref[...] = jnp.broadcast_to(endr, cr_ref.shape)
        ci_ref[...] = jnp.broadcast_to(endi, ci_ref.shape)
        xr, xi = hr_ref[...], hi_ref[...]
        y_ref[...] = _dot(xr, cre_ref[...]) - _dot(xi, cim_ref[...]) + u * dsk_ref[...]

    u_spec = pl.BlockSpec((SSM_T, 128), lambda j, t: (t, OFF_SSM // 128 + j))
    b_spec = pl.BlockSpec((None, 128, W_BRANCH), lambda j, t: (j, 0, 0))
    a_spec = pl.BlockSpec((1, W_BRANCH), lambda j, t: (0, j))
    c_spec = pl.BlockSpec((None, W_BRANCH, 128), lambda j, t: (j, 0, 0))
    h_spec = pl.BlockSpec((SSM_T, W_BRANCH), lambda j, t: (t, j))
    return pl.pallas_call(
        body, name="s5_fwd", grid=(4, nt),
        in_specs=[u_spec, b_spec, b_spec, a_spec, a_spec, c_spec, c_spec, pl.BlockSpec((1, 128), lambda j, t: (0, j))],
        out_specs=[h_spec, h_spec, pl.BlockSpec((SSM_T, 128), lambda j, t: (t, j))],
        out_shape=[jax.ShapeDtypeStruct((S, SSM_COLS), F32), jax.ShapeDtypeStruct((S, SSM_COLS), F32),
                   jax.ShapeDtypeStruct((S, W_BRANCH), F32)],
        scratch_shapes=[pltpu.VMEM((8, W_BRANCH), F32)] * 2 + _scan_scratch(0),
        compiler_params=_cp(2),
    )(proj, b_re, b_im, a_re, a_im, c_re, c_im, d_skip)


def s5_bwd(proj, hr, hi, dy, b_re, b_im, a_re, a_im, c_re, c_im, d_skip):
    S = proj.shape[0]
    nt = S // SSM_T

    def body(u_ref, hr_ref, hi_ref, hpr_ref, hpi_ref, dy_ref, bre_ref, bim_ref, ar_ref, ai_ref, cre_ref, cim_ref,
             dsk_ref, du_ref, dbre_ref, dbim_ref, dar_ref, dai_ref, dcre_ref, dcim_ref, ddsk_ref, gr_ref, gi_ref,
             yr_ref, yi_ref, er_ref, ei_ref, sr_ref, si_ref):
        step = pl.program_id(1)
        t = nt - 1 - step

        @pl.when(step == 0)
        def _():
            gr_ref[...] = jnp.zeros(gr_ref.shape, F32)
            gi_ref[...] = jnp.zeros(gi_ref.shape, F32)
            for ref in (dbre_ref, dbim_ref, dar_ref, dai_ref, dcre_ref, dcim_ref, ddsk_ref):
                ref[...] = jnp.zeros(ref.shape, F32)

        u = u_ref[...]
        dy = dy_ref[...]
        ar, ai = ar_ref[...], ai_ref[...]
        rows = lax.broadcasted_iota(jnp.int32, (SSM_T, W_BRANCH), 0)
        inr, ini = _cmul(ar, -ai, gr_ref[0:1, :], gi_ref[0:1, :])
        xr = _dot(dy, cre_ref[...], "nt") + jnp.where(rows == SSM_T - 1, inr, 0.0)
        xi = -_dot(dy, cim_ref[...], "nt") + jnp.where(rows == SSM_T - 1, ini, 0.0)
        endr, endi = _block_scan(xr, xi, ar, -ai, True, yr_ref, yi_ref, er_ref, ei_ref, sr_ref, si_ref)
        gr_ref[...] = jnp.broadcast_to(endr, gr_ref.shape)
        gi_ref[...] = jnp.broadcast_to(endi, gi_ref.shape)
        xr, xi = sr_ref[...], si_ref[...]
        hr_blk, hi_blk = hr_ref[...], hi_ref[...]
        keep = (t > 0).astype(F32)
        hpr = jnp.where(rows >= 1, pltpu.roll(hr_blk, 1, 0), hpr_ref[7:8, :] * keep)
        hpi = jnp.where(rows >= 1, pltpu.roll(hi_blk, 1, 0), hpi_ref[7:8, :] * keep)
        dar_ref[...] += jnp.sum(hpr * xr + hpi * xi, axis=0, keepdims=True)
        dai_ref[...] += jnp.sum(hpr * xi - hpi * xr, axis=0, keepdims=True)
        dcre_ref[...] += _dot(hr_blk, dy, "tn")
        dcim_ref[...] -= _dot(hi_blk, dy, "tn")
        du = dy * dsk_ref[...] + _dot(xr, bre_ref[...], "nt") + _dot(xi, bim_ref[...], "nt")
        du_ref[...] = du.astype(du_ref.dtype)
        dbre_ref[...] += _dot(u, xr, "tn")
        dbim_ref[...] += _dot(u, xi, "tn")
        ddsk_ref[...] += jnp.sum(dy * u, axis=0, keepdims=True)

    def rev(t):
        return nt - 1 - t

    u_spec = pl.BlockSpec((SSM_T, 128), lambda j, t: (rev(t), OFF_SSM // 128 + j))
    h_spec = pl.BlockSpec((SSM_T, W_BRANCH), lambda j, t: (rev(t), j))
    hprev_spec = pl.BlockSpec((8, W_BRANCH), lambda j, t: (jnp.maximum(rev(t) * (SSM_T // 8) - 1, 0), j))
    ch_spec = pl.BlockSpec((SSM_T, 128), lambda j, t: (rev(t), j))
    b_spec = pl.BlockSpec((None, 128, W_BRANCH), lambda j, t: (j, 0, 0))
    a_spec = pl.BlockSpec((1, W_BRANCH), lambda j, t: (0, j))
    c_spec = pl.BlockSpec((None, W_BRANCH, 128), lambda j, t: (j, 0, 0))
    d_spec = pl.BlockSpec((1, 128), lambda j, t: (0, j))
    return pl.pallas_call(
        body, name="s5_bwd", grid=(4, nt),
        in_specs=[u_spec, h_spec, h_spec, hprev_spec, hprev_spec, ch_spec, b_spec, b_spec, a_spec, a_spec,
                  c_spec, c_spec, d_spec],
        out_specs=[ch_spec, b_spec, b_spec, a_spec, a_spec, c_spec, c_spec, d_spec],
        out_shape=[jax.ShapeDtypeStruct((S, W_BRANCH), BF16),
                   jax.ShapeDtypeStruct((4, 128, W_BRANCH), F32), jax.ShapeDtypeStruct((4, 128, W_BRANCH), F32),
                   jax.ShapeDtypeStruct((1, SSM_COLS), F32), jax.ShapeDtypeStruct((1, SSM_COLS), F32),
                   jax.ShapeDtypeStruct((4, W_BRANCH, 128), F32), jax.ShapeDtypeStruct((4, W_BRANCH, 128), F32),
                   jax.ShapeDtypeStruct((1, W_BRANCH), F32)],
        scratch_shapes=[pltpu.VMEM((8, W_BRANCH), F32)] * 2 + _scan_scratch(2),
        compiler_params=_cp(2),
    )(proj, hr, hi, hr, hi, dy, b_re, b_im, a_re, a_im, c_re, c_im, d_skip)


def glu_fwd(y, w_glu, b_glu):
    S = y.shape[0]

    def body(y_ref, w_ref, b_ref, o_ref):
        g = _gelu(y_ref[...])
        o_ref[...] = (g * _sigmoid(_dot(g, w_ref[...]) + b_ref[...])).astype(o_ref.dtype)

    blk = pl.BlockSpec((ROW_TILE, W_BRANCH), lambda i: (i, 0))
    return pl.pallas_call(
        body, name="glu_fwd", grid=(S // ROW_TILE,),
        in_specs=[blk, pl.BlockSpec((W_BRANCH, W_BRANCH), lambda i: (0, 0)), pl.BlockSpec((1, W_BRANCH), lambda i: (0, 0))],
        out_specs=blk, out_shape=jax.ShapeDtypeStruct((S, W_BRANCH), BF16), compiler_params=_cp(1),
    )(y, w_glu, b_glu)


def glu_bwd(y, w_glu, b_glu, dout):
    S = y.shape[0]

    def body(y_ref, w_ref, b_ref, do_ref, dy_ref, dw_ref, db_ref):
        yv = y_ref[...]
        do = do_ref[...]
        g = _gelu(yv)
        s = _sigmoid(_dot(g, w_ref[...]) + b_ref[...])
        dz = do * g * s * (1.0 - s)
        dg = do * s + _dot(dz, w_ref[...], "nt")
        dy_ref[...] = dg * _gelu_grad(yv)
        dw = _dot(g, dz, "tn")
        db = jnp.sum(dz, axis=0, keepdims=True)

        @pl.when(pl.program_id(0) == 0)
        def _():
            dw_ref[...] = dw
            db_ref[...] = db

        @pl.when(pl.program_id(0) > 0)
        def _():
            dw_ref[...] += dw
            db_ref[...] += db

    blk = pl.BlockSpec((ROW_TILE, W_BRANCH), lambda i: (i, 0))
    mat = pl.BlockSpec((W_BRANCH, W_BRANCH), lambda i: (0, 0))
    vec = pl.BlockSpec((1, W_BRANCH), lambda i: (0, 0))
    return pl.pallas_call(
        body, name="glu_bwd", grid=(S // ROW_TILE,), in_specs=[blk, mat, vec, blk], out_specs=[blk, mat, vec],
        out_shape=[jax.ShapeDtypeStruct((S, W_BRANCH), F32), jax.ShapeDtypeStruct((W_BRANCH, W_BRANCH), F32),
                   jax.ShapeDtypeStruct((1, W_BRANCH), F32)],
        compiler_params=_cp(1),
    )(y, w_glu, b_glu, dout)


SGU_TILE = 512
SGU_U_BLOCK = OFF_SGU // W_BRANCH
SGU_V_BLOCK = SGU_U_BLOCK + 1


def _sgu_norm(zv):
    v = _gelu(zv)
    mu = jnp.mean(v, axis=-1, keepdims=True)
    vc = v - mu
    rstd = lax.rsqrt(jnp.mean(vc * vc, axis=-1, keepdims=True) + EPS)
    return vc * rstd, rstd


def _tril():
    return lax.broadcasted_iota(jnp.int32, (SGU_CHUNK, SGU_CHUNK), 0) >= lax.broadcasted_iota(jnp.int32, (SGU_CHUNK, SGU_CHUNK), 1)


def sgu_fwd(proj, ln_g, ln_b, w_s, b_s_t):
    S = proj.shape[0]

    def body(zu_ref, zv_ref, g_ref, b_ref, ws_ref, bs_ref, o_ref, vf_ref):
        vn, _ = _sgu_norm(zv_ref[...])
        vf_ref[...] = vn * g_ref[...] + b_ref[...]
        tri = _tril()
        for gi in range(4):
            ws = jnp.where(tri, ws_ref[gi], 0.0)
            cols = slice(gi * 128, (gi + 1) * 128)
            for c in range(SGU_TILE // SGU_CHUNK):
                rows = slice(c * SGU_CHUNK, (c + 1) * SGU_CHUNK)
                sv = _dot(ws, vf_ref[rows, cols]) + bs_ref[:, gi:gi + 1]
                o_ref[rows, cols] = (_gelu(zu_ref[rows, cols]) * sv).astype(o_ref.dtype)

    blk = lambda cb: pl.BlockSpec((SGU_TILE, W_BRANCH), lambda i: (i, cb))
    vec = pl.BlockSpec((1, W_BRANCH), lambda i: (0, 0))
    return pl.pallas_call(
        body, name="sgu_fwd", grid=(S // SGU_TILE,),
        in_specs=[blk(SGU_U_BLOCK), blk(SGU_V_BLOCK), vec, vec, pl.BlockSpec((4, SGU_CHUNK, SGU_CHUNK), lambda i: (0, 0, 0)),
                  pl.BlockSpec((SGU_CHUNK, 4), lambda i: (0, 0))],
        out_specs=blk(0), out_shape=jax.ShapeDtypeStruct((S, W_BRANCH), BF16),
        scratch_shapes=[pltpu.VMEM((SGU_TILE, W_BRANCH), F32)], compiler_params=_cp(1),
    )(proj, proj, ln_g, ln_b, w_s, b_s_t)


def sgu_bwd(proj, ln_g, ln_b, w_s, b_s_t, dout):
    S = proj.shape[0]

    def body(zu_ref, zv_ref, g_ref, b_ref, ws_ref, bs_ref, do_ref, dzu_ref, dzv_ref, dg_ref, db_ref, dws_ref, dbs_ref,
             vf_ref, dvf_ref):
        @pl.when(pl.program_id(0) == 0)
        def _():
            for ref in (dg_ref, db_ref, dws_ref, dbs_ref):
                ref[...] = jnp.zeros(ref.shape, F32)

        vn, rstd = _sgu_norm(zv_ref[...])
        vf_ref[...] = vn * g_ref[...] + b_ref[...]
        tri = _tril()
        lane = lax.broadcasted_iota(jnp.int32, (SGU_CHUNK, 128), 1)
        dbs = jnp.zeros((SGU_CHUNK, 128), F32)
        for gi in range(4):
            ws = jnp.where(tri, ws_ref[gi], 0.0)
            cols = slice(gi * 128, (gi + 1) * 128)
            dws = jnp.zeros((SGU_CHUNK, SGU_CHUNK), F32)
            for c in range(SGU_TILE // SGU_CHUNK):
                rows = slice(c * SGU_CHUNK, (c + 1) * SGU_CHUNK)
                vf = vf_ref[rows, cols]
                zu = zu_ref[rows, cols]
                do = do_ref[rows, cols]
                sv = _dot(ws, vf) + bs_ref[:, gi:gi + 1]
                dzu_ref[rows, cols] = (do * sv * _gelu_grad(zu)).astype(dzu_ref.dtype)
                dsv = do * _gelu(zu)
                dvf_ref[rows, cols] = _dot(ws, dsv, "tn")
                dws = dws + _dot(dsv, vf, "nt")
                dbs = dbs + jnp.where(lane == gi, jnp.sum(dsv, axis=-1, keepdims=True), 0.0)
            dws_ref[gi] += jnp.where(tri, dws, 0.0)
        dbs_ref[...] += dbs
        dvf = dvf_ref[...]
        dg_ref[...] += jnp.sum(dvf * vn, axis=0, keepdims=True)
        db_ref[...] += jnp.sum(dvf, axis=0, keepdims=True)
        dvn = dvf * g_ref[...]
        dv = rstd * (dvn - jnp.mean(dvn, axis=-1, keepdims=True) - vn * jnp.mean(dvn * vn, axis=-1, keepdims=True))
        dzv_ref[...] = (dv * _gelu_grad(zv_ref[...])).astype(dzv_ref.dtype)

    blk = lambda cb: pl.BlockSpec((SGU_TILE, W_BRANCH), lambda i: (i, cb))
    vec = pl.BlockSpec((1, W_BRANCH), lambda i: (0, 0))
    ws_spec = pl.BlockSpec((4, SGU_CHUNK, SGU_CHUNK), lambda i: (0, 0, 0))
    return pl.pallas_call(
        body, name="sgu_bwd", grid=(S // SGU_TILE,),
        in_specs=[blk(SGU_U_BLOCK), blk(SGU_V_BLOCK), vec, vec, ws_spec, pl.BlockSpec((SGU_CHUNK, 4), lambda i: (0, 0)),
                  blk(0)],
        out_specs=[blk(0), blk(0), vec, vec, ws_spec, pl.BlockSpec((SGU_CHUNK, 128), lambda i: (0, 0))],
        out_shape=[jax.ShapeDtypeStruct((S, W_BRANCH), BF16), jax.ShapeDtypeStruct((S, W_BRANCH), BF16),
                   jax.ShapeDtypeStruct((1, W_BRANCH), F32), jax.ShapeDtypeStruct((1, W_BRANCH), F32),
                   jax.ShapeDtypeStruct((4, SGU_CHUNK, SGU_CHUNK), F32), jax.ShapeDtypeStruct((SGU_CHUNK, 128), F32)],
        scratch_shapes=[pltpu.VMEM((SGU_TILE, W_BRANCH), F32), pltpu.VMEM((SGU_TILE, W_BRANCH), F32)],
        compiler_params=_cp(1),
    )(proj, proj, ln_g, ln_b, w_s, b_s_t, dout)


GM_TILE = 512


def _gate_specs(order):
    def spec(i):
        def index(*ids):
            m, n = order(*ids)
            return (m, (OFF_GATE + i * D_MODEL) // GM_TILE + n)
        return pl.BlockSpec((GM_TILE, GM_TILE), index)
    return [spec(i) for i in range(4)]


def merge_fwd(proj, gate_b, branches, w_up):
    S = proj.shape[0]
    order = lambda n, m: (m, n)

    def body(p0, p1, p2, p3, gb_ref, b0, b1, b2, b3, w_ref, o_ref):
        acc = jnp.zeros((GM_TILE, GM_TILE), F32)
        for i, (p_ref, br_ref) in enumerate(zip((p0, p1, p2, p3), (b0, b1, b2, b3))):
            acc = acc + _sigmoid(p_ref[...] + gb_ref[i:i + 1, :]) * _dot(br_ref[...], w_ref[i])
        o_ref[...] = acc.astype(o_ref.dtype)

    br_spec = pl.BlockSpec((GM_TILE, W_BRANCH), lambda n, m: (m, 0))
    return pl.pallas_call(
        body, name="merge_fwd", grid=(D_MODEL // GM_TILE, S // GM_TILE),
        in_specs=_gate_specs(order) + [pl.BlockSpec((4, GM_TILE), lambda n, m: (0, n))] + [br_spec] * 4
        + [pl.BlockSpec((4, W_BRANCH, GM_TILE), lambda n, m: (0, 0, n))],
        out_specs=pl.BlockSpec((GM_TILE, GM_TILE), lambda n, m: (m, n)),
        out_shape=jax.ShapeDtypeStruct((S, D_MODEL), BF16), compiler_params=_cp(2),
    )(proj, proj, proj, proj, gate_b, *branches, w_up)


def merge_bwd(proj, gate_b, branches, w_up, dmerged):
    S = proj.shape[0]
    order = lambda n, m: (m, n)

    def body(p0, p1, p2, p3, gb_ref, b0, b1, b2, b3, w_ref, dm_ref, dp0, dp1, dp2, dp3, du0, du1, du2, du3, dgb_ref):
        dm = dm_ref[...]
        dgb = []
        for i, (p_ref, br_ref, dp_ref, du_ref) in enumerate(
                zip((p0, p1, p2, p3), (b0, b1, b2, b3), (dp0, dp1, dp2, dp3), (du0, du1, du2, du3))):
            gate = _sigmoid(p_ref[...] + gb_ref[i:i + 1, :])
            dpre = dm * _dot(br_ref[...], w_ref[i]) * gate * (1.0 - gate)
            dp_ref[...] = dpre.astype(dp_ref.dtype)
            du_ref[...] = (dm * gate).astype(du_ref.dtype)
            dgb.append(jnp.sum(dpre, axis=0, keepdims=True))
        dgb = jnp.concatenate(dgb, axis=0)

        @pl.when(pl.program_id(1) == 0)
        def _():
            dgb_ref[...] = dgb

        @pl.when(pl.program_id(1) > 0)
        def _():
            dgb_ref[...] += dgb

    br_spec = pl.BlockSpec((GM_TILE, W_BRANCH), lambda n, m: (m, 0))
    mn = pl.BlockSpec((GM_TILE, GM_TILE), lambda n, m: (m, n))
    gb = pl.BlockSpec((4, GM_TILE), lambda n, m: (0, n))
    big = jax.ShapeDtypeStruct((S, D_MODEL), BF16)
    outs = pl.pallas_call(
        body, name="merge_bwd", grid=(D_MODEL // GM_TILE, S // GM_TILE),
        in_specs=_gate_specs(order) + [gb] + [br_spec] * 4
        + [pl.BlockSpec((4, W_BRANCH, GM_TILE), lambda n, m: (0, 0, n)), mn],
        out_specs=[mn] * 8 + [gb], out_shape=[big] * 8 + [jax.ShapeDtypeStruct((4, D_MODEL), F32)],
        compiler_params=_cp(2),
    )(proj, proj, proj, proj, gate_b, *branches, w_up, dmerged)
    return outs[0:4], outs[4:8], outs[8]


def _xatt_probs(q, k):
    s = _dot(q, k, "nt") * (X_HEAD_DIM ** -0.5)
    p = jnp.exp(s - jnp.max(s, axis=-1, keepdims=True))
    return p / jnp.sum(p, axis=-1, keepdims=True)


def xatt_fwd(q, kv):
    S = q.shape[0]

    def body(q_ref, kv_ref, o_ref):
        for h in range(X_HEADS):
            cols = slice(h * X_HEAD_DIM, (h + 1) * X_HEAD_DIM)
            p = _xatt_probs(q_ref[:, cols], kv_ref[:, cols])
            o_ref[:, cols] = _dot(p, kv_ref[:, W_BRANCH + h * X_HEAD_DIM:W_BRANCH + (h + 1) * X_HEAD_DIM]).astype(o_ref.dtype)

    blk = pl.BlockSpec((ROW_TILE, W_BRANCH), lambda i: (i, 0))
    return pl.pallas_call(
        body, name="xatt_fwd", grid=(S // ROW_TILE,),
        in_specs=[blk, pl.BlockSpec((N_MEM, 2 * W_BRANCH), lambda i: (0, 0))], out_specs=blk,
        out_shape=jax.ShapeDtypeStruct((S, W_BRANCH), BF16), compiler_params=_cp(1),
    )(q, kv)


def xatt_bwd(q, kv, do):
    S = q.shape[0]

    def body(q_ref, kv_ref, do_ref, dq_ref, dkv_ref):
        @pl.when(pl.program_id(0) == 0)
        def _():
            dkv_ref[...] = jnp.zeros(dkv_ref.shape, F32)

        for h in range(X_HEADS):
            cols = slice(h * X_HEAD_DIM, (h + 1) * X_HEAD_DIM)
            vcols = slice(W_BRANCH + h * X_HEAD_DIM, W_BRANCH + (h + 1) * X_HEAD_DIM)
            qh, kh, doh = q_ref[:, cols], kv_ref[:, cols], do_ref[:, cols]
            p = _xatt_probs(qh, kh)
            dp = _dot(doh, kv_ref[:, vcols], "nt")
            ds = p * (dp - jnp.sum(dp * p, axis=-1, keepdims=True)) * (X_HEAD_DIM ** -0.5)
            dq_ref[:, cols] = _dot(ds, kh).astype(dq_ref.dtype)
            dkv_ref[:, cols] += _dot(ds, qh, "tn")
            dkv_ref[:, vcols] += _dot(p, doh, "tn")

    blk = pl.BlockSpec((ROW_TILE, W_BRANCH), lambda i: (i, 0))
    kv_spec = pl.BlockSpec((N_MEM, 2 * W_BRANCH), lambda i: (0, 0))
    return pl.pallas_call(
        body, name="xatt_bwd", grid=(S // ROW_TILE,), in_specs=[blk, kv_spec, blk], out_specs=[blk, kv_spec],
        out_shape=[jax.ShapeDtypeStruct((S, W_BRANCH), BF16), jax.ShapeDtypeStruct((N_MEM, 2 * W_BRANCH), F32)],
        compiler_params=_cp(1),
    )(q, kv, do)


def s5_params(a_re, a_im, log_dt, b_re, b_im, c_re, c_im):
    lam_re = jnp.minimum(a_re, -1e-4)
    lam_im = a_im
    dt = jnp.exp(log_dt)[:, None]
    mag = jnp.exp(lam_re * dt)
    ab_re, ab_im = mag * jnp.cos(lam_im * dt), mag * jnp.sin(lam_im * dt)
    den = lam_re * lam_re + lam_im * lam_im
    f_re = ((ab_re - 1.0) * lam_re + ab_im * lam_im) / den
    f_im = (ab_im * lam_re - (ab_re - 1.0) * lam_im) / den
    bb_re = f_re[..., None] * b_re - f_im[..., None] * b_im
    bb_im = f_re[..., None] * b_im + f_im[..., None] * b_re
    eye = jnp.eye(8, dtype=F32)

    def b_blocks(bb):
        t = bb.reshape(4, 8, SSM_STATE, SSM_GROUP).transpose(0, 1, 3, 2)
        return (t[:, :, :, None, :] * eye[None, :, None, :, None]).reshape(4, 128, W_BRANCH)

    def c_blocks(cc):
        t = cc.reshape(4, 8, SSM_GROUP, SSM_STATE).transpose(0, 1, 3, 2)
        return (t[:, :, :, None, :] * eye[None, :, None, :, None]).reshape(4, W_BRANCH, 128)

    return (ab_re.reshape(1, SSM_COLS), ab_im.reshape(1, SSM_COLS), b_blocks(bb_re), b_blocks(bb_im),
            c_blocks(c_re), c_blocks(c_im))


ANY = pl.BlockSpec(memory_space=pl.ANY)


def _chip_index():
    return 2 * lax.axis_index("x") + lax.axis_index("y")


def _peer_chip(j):
    x, y, c = lax.axis_index("x"), lax.axis_index("y"), lax.axis_index("c")
    return ((1 - x) if j & 2 else x, (1 - y) if j & 1 else y, c)


def _piece(ref, axis, s, n):
    size = ref.shape[axis] // n
    idx = [slice(None)] * len(ref.shape)
    idx[axis] = pl.ds(s * size, size)
    return ref.at[tuple(idx)]


HBM_SPEC = pl.BlockSpec(memory_space=pltpu.HBM)
SEM_SPEC = pl.BlockSpec(memory_space=pltpu.SEMAPHORE)
SIDE_EFFECT = pltpu.SideEffectType.DATAFLOW_SIDE_EFFECTING


def _chip_copies(ins, lands, send, recv, axes, mode, k, arriving):
    copies = []
    for t in range(len(ins)):
        for j in (1, 2, 3):
            place = k ^ j if arriving else k
            if mode == "gather":
                src, dst = ins[t], _piece(lands[t], axes[t], place, 4)
            else:
                src = ins[t] if axes[t] is None else _piece(ins[t], axes[t], k ^ j, 4)
                dst = lands[t].at[place]
            copies.append(pltpu.make_async_remote_copy(
                src_ref=src, dst_ref=dst, send_sem=send.at[3 * t + j - 1], recv_sem=recv.at[3 * t + j - 1],
                device_id=_peer_chip(j), device_id_type=MESH_ID))
    return copies


def chips_start(ins, lands, axes, mode, name, after=()):
    n, na = len(ins), len(after)

    def body(*refs):
        in_refs, land_refs = refs[:n], refs[n:2 * n]
        send, recv, token = refs[2 * n + na], refs[2 * n + na + 1], refs[-1]
        q = _chip_index()
        for k in range(4):
            @pl.when(q == k)
            def _():
                for copy in _chip_copies(in_refs, land_refs, send, recv, axes, mode, k, arriving=False):
                    copy.start()
        token[...] = jnp.zeros(token.shape, token.dtype)

    hbm = lambda a: pltpu.HBM(a.shape, a.dtype)
    outs = pl.pallas_call(
        body, name=name, in_specs=[HBM_SPEC] * (2 * n) + [ANY] * na,
        out_specs=[SEM_SPEC, SEM_SPEC] + [HBM_SPEC] * (2 * n) + [pl.BlockSpec(memory_space=pltpu.VMEM)],
        out_shape=[pltpu.SemaphoreType.DMA((3 * n,)), pltpu.SemaphoreType.DMA((3 * n,))]
        + [hbm(a) for a in ins] + [hbm(a) for a in lands] + [jax.ShapeDtypeStruct((8, 128), F32)],
        input_output_aliases={i: 2 + i for i in range(2 * n)},
        compiler_params=pltpu.CompilerParams(has_side_effects=SIDE_EFFECT),
    )(*[pltpu.with_memory_space_constraint(a, pltpu.HBM) for a in list(ins) + list(lands)], *after)
    return outs[0], outs[1], outs[2:2 + n], outs[2 + n:2 + 2 * n], outs[-1]


def chips_wait(send, recv, ins, lands, axes, mode, name, after=()):
    n = len(ins)

    def body(*refs):
        in_refs, land_refs = refs[:n], refs[n:2 * n]
        send_ref, recv_ref = refs[2 * n], refs[2 * n + 1]
        q = _chip_index()
        for k in range(4):
            @pl.when(q == k)
            def _():
                for copy in _chip_copies(in_refs, land_refs, send_ref, recv_ref, axes, mode, k, arriving=True):
                    copy.wait_send()
                    copy.wait_recv()

    hbm = lambda a: pltpu.HBM(a.shape, a.dtype)
    outs = pl.pallas_call(
        body, name=name, in_specs=[HBM_SPEC] * (2 * n) + [SEM_SPEC, SEM_SPEC] + [ANY] * len(after),
        out_specs=[HBM_SPEC] * (2 * n), out_shape=[hbm(a) for a in ins] + [hbm(a) for a in lands],
        input_output_aliases={i: i for i in range(2 * n)},
        compiler_params=pltpu.CompilerParams(has_side_effects=SIDE_EFFECT),
    )(*ins, *lands, send, recv, *after)
    return outs[:n], outs[n:]


def swap_cores(arrs, name):
    n = len(arrs)

    def body(*refs):
        ins, outs = refs[:n], refs[n:2 * n]
        send, recv = refs[2 * n:]
        sibling = (lax.axis_index("x"), lax.axis_index("y"), 1 - lax.axis_index("c"))
        copies = [pltpu.make_async_remote_copy(src_ref=ins[t], dst_ref=outs[t], send_sem=send.at[t], recv_sem=recv.at[t],
                                               device_id=sibling, device_id_type=MESH_ID) for t in range(n)]
        for cp in copies:
            cp.start()
        for cp in copies:
            cp.wait()

    return pl.pallas_call(
        body, name=name, in_specs=[ANY] * n, out_specs=[ANY] * n,
        out_shape=[jax.ShapeDtypeStruct(a.shape, a.dtype) for a in arrs],
        scratch_shapes=[pltpu.SemaphoreType.DMA((n,)), pltpu.SemaphoreType.DMA((n,))],
    )(*arrs)


ELEMENTWISE_BLOCK_BYTES = 1 << 20


def _row_tile(rows, cols):
    want = max(8, ELEMENTWISE_BLOCK_BYTES // (4 * 128 * -(-cols // 128)))
    fits = [t for t in range(8, min(rows, want) + 1, 8) if rows % t == 0]
    return fits[-1] if fits else rows


def sum_chips(recv, own, axis, chip, stacked, l, name):
    _, r, c = recv.shape
    tr = _row_tile(r, c)
    nrt = r // tr

    def body(chip_ref, r_ref, own_ref, stacked_ref, o_ref):
        for k in range(4):
            @pl.when(chip_ref[0] == k)
            def _():
                terms = [own_ref[...] if s == k else r_ref[s] for s in range(4)]
                o_ref[...] = ((terms[0] + terms[1]) + terms[2]) + terms[3]

    own_index = {0: lambda i, q: (q[0] * nrt + i, 0), 1: lambda i, q: (i, q[0]), None: lambda i, q: (i, 0)}[axis]
    return pl.pallas_call(
        body, name=name,
        grid_spec=pltpu.PrefetchScalarGridSpec(
            num_scalar_prefetch=1, grid=(nrt,),
            in_specs=[pl.BlockSpec((4, tr, c), lambda i, q: (0, i, 0)), pl.BlockSpec((tr, c), own_index), ANY],
            out_specs=pl.BlockSpec((None, tr, c), lambda i, q: (l, i, 0))),
        out_shape=jax.ShapeDtypeStruct(stacked.shape, F32), input_output_aliases={3: 0}, compiler_params=_cp(1),
    )(chip, recv, own, stacked)


def adamw(w, ga, gb, m, v, name):
    rows, cols = w.shape
    tr = _row_tile(rows, cols)

    def body(w_ref, ga_ref, gb_ref, m_ref, v_ref, g_ref, d_ref, nm_ref, nv_ref):
        g = ga_ref[...] + gb_ref[...]
        nm = ADAM_B1 * m_ref[...] + (1.0 - ADAM_B1) * g
        nv = ADAM_B2 * v_ref[...] + (1.0 - ADAM_B2) * (g * g)
        m_hat = nm / (1.0 - ADAM_B1 ** ADAM_STEP)
        v_hat = nv / (1.0 - ADAM_B2 ** ADAM_STEP)
        g_ref[...] = g
        nm_ref[...] = nm
        nv_ref[...] = nv
        d_ref[...] = -ADAM_LR * (m_hat / (jnp.sqrt(v_hat) + ADAM_EPS) + ADAM_WD * w_ref[...])

    blk = pl.BlockSpec((tr, cols), lambda i: (i, 0))
    f = jax.ShapeDtypeStruct((rows, cols), F32)
    return pl.pallas_call(
        body, name=name, grid=(rows // tr,), in_specs=[blk] * 5, out_specs=[blk] * 4, out_shape=[f] * 4,
        compiler_params=_cp(1),
    )(w, ga, gb, m, v)


PACK_ALIGN = 1024
PACK_ROWS_ALIGN = 2048


def pack_small(arrs):
    parts = []
    for a in arrs:
        flat = a.reshape(-1)
        pad = (-flat.shape[0]) % PACK_ALIGN
        parts.append(jnp.pad(flat, (0, pad)) if pad else flat)
    rows = sum(p.shape[0] for p in parts) // 128
    parts.append(jnp.zeros(((-rows) % PACK_ROWS_ALIGN * 128,), arrs[0].dtype))
    return jnp.concatenate(parts).reshape(-1, 128)


def unpack_small(packed, shapes):
    out, row = [], 0
    for shape in shapes:
        size = int(np.prod(shape))
        rows = -(-size // PACK_ALIGN) * 8
        out.append(packed[row:row + rows].reshape(-1)[:size].reshape(shape))
        row += rows
    return out


def _norm_epilogue(with_next):
    def epi(acc, res, g_post, *g_pre):
        x_new = acc * lax.rsqrt(jnp.mean(acc * acc, axis=-1, keepdims=True) + EPS) * g_post + res
        if not with_next:
            return acc, x_new
        return acc, x_new, x_new * lax.rsqrt(jnp.mean(x_new * x_new, axis=-1, keepdims=True) + EPS) * g_pre[0]
    return epi


def layer_fwd(x, h1, mem, w_in, rest_of, P, biases, g_next, after=()):
    sv = {"x0": x}
    post = dict(tm=512, tn=D_MODEL)
    proj = mm(h1, w_in, "nn", out_dtypes=[F32], name="mm_w_in", after=after)
    a_out = pool_fwd(proj, P["pool_w"], P["pool_scale"])
    os_, lses = [], []
    for g, (win, dil) in enumerate(DIL_GROUPS):
        o, lse = att_fwd(proj, biases[g], g, dil)
        os_.append(o)
        lses.append(lse)
    b_out, w0, w1, w2 = att_combine(os_, lses)
    s5p = P["s5"]
    hr, hi, y = s5_fwd(proj, s5p[2], s5p[3], s5p[0], s5p[1], s5p[4], s5p[5], P["d_skip"])
    d_out = sgu_fwd(proj, P["sgu_ln_g"], P["sgu_ln_b"], P["w_s"], P["b_s_t"])
    W, after_rest = rest_of("mixer", d_out)
    W = dict(W, w_in=w_in)
    c_out = glu_fwd(y, W["w_glu"], P["b_glu"])
    branches = (a_out, b_out, c_out, d_out)
    merged = merge_fwd(proj, W["gate_b"], branches, W["w_up"])
    t1, x1, h2 = mm(merged, W["w_out"], "nn", tk=1024, out_dtypes=[F32, F32, BF16], name="mm_w_out", extras=(x,),
                    vecs=(P["g_mix_post"], P["g_x_pre"]), epi=_norm_epilogue(True), after=after_rest, **post)
    sv.update(h1=h1, proj=proj, os=os_, lses=lses, wts=(w0, w1, w2), hr=hr, hi=hi, y=y, branches=branches,
              merged=merged, t1=t1, x1=x1)

    mem_n = rms_fwd(mem, P["g_mem"], BF16, "rms_mem")
    q = mm(h2, W["w_cq"], "nn", tm=1024, tn=512, tk=1024, out_dtypes=[BF16], name="mm_w_cq")
    kv = mm(mem_n, W["w_ckv"], "nn", tm=256, tn=1024, tk=1024, out_dtypes=[BF16], name="mm_w_ckv")
    ox = xatt_fwd(q, kv)
    t2, x2, h3 = mm(ox, W["w_co"], "nn", tk=512, out_dtypes=[F32, F32, BF16], name="mm_w_co", extras=(x1,),
                    vecs=(P["g_x_post"], P["g_ff_pre"]), epi=_norm_epilogue(True), **post)
    sv.update(h2=h2, mem_n=mem_n, q=q, kv=kv, ox=ox, t2=t2, x2=x2)

    W_ff, after_ff = rest_of("mlp", h3)
    W = dict(W, **W_ff)
    pre, act = mm(h3, W["w_ff1"], "nn", out_dtypes=[F32, BF16], name="mm_w_ff1",
                  epi=lambda acc: (acc, jnp.square(jnp.maximum(acc, 0.0))), after=after_ff)
    if g_next is None:
        (ff, x3), h_next = mm(act, W["w_ff2"], "nn", out_dtypes=[F32, F32], name="mm_w_ff2_last", extras=(x2,),
                              vecs=(P["g_ff_post"],), epi=_norm_epilogue(False)), None
    else:
        ff, x3, h_next = mm(act, W["w_ff2"], "nn", out_dtypes=[F32, F32, BF16], name="mm_w_ff2", extras=(x2,),
                            vecs=(P["g_ff_post"], g_next), epi=_norm_epilogue(True))
    sv.update(h3=h3, pre=pre, act=act, ff=ff, W=W)
    return x3, h_next, sv


def _pre_norm_bwd_epilogue(dh, x, add, g):
    r = lax.rsqrt(jnp.mean(x * x, axis=-1, keepdims=True) + EPS)
    xn = x * r
    dxn = dh * g
    return r * (dxn - xn * jnp.mean(dxn * xn, axis=-1, keepdims=True)) + add, jnp.sum(dh * xn, axis=0, keepdims=True)


def layer_bwd(dx, mem, W, P, biases, sv, headsum, emit, after=()):
    G = {}
    dff, G["g_ff_post"] = rms_bwd(sv["ff"], P["g_ff_post"], dx, BF16, "rms_post_bwd", after=after)
    G["w_ff2"] = mm(sv["act"], dff, "tn", out_dtypes=[F32], name="mm_dw_ff2")
    dpre = mm(dff, W["w_ff2"], "nt", out_dtypes=[BF16], name="mm_dact", extras=(sv["pre"],),
              epi=lambda acc, pre: (acc * (2.0 * jnp.maximum(pre, 0.0)),))
    G["w_ff1"] = mm(sv["h3"], dpre, "tn", out_dtypes=[F32], name="mm_dw_ff1")
    sent = emit(("w_ff1", "w_ff2"), G)
    pre_bwd = dict(out_dtypes=[F32], epi=_pre_norm_bwd_epilogue, n_sums=1)
    dx2, G["g_ff_pre"] = mm(dpre, W["w_ff1"], "nt", name="mm_dh3", extras=(sv["x2"], dx), vecs=(P["g_ff_pre"],),
                            after=sent, **pre_bwd)
    dt2, G["g_x_post"] = rms_bwd(sv["t2"], P["g_x_post"], dx2, BF16, "rms_post_bwd")
    G["w_co"] = mm(sv["ox"], dt2, "tn", tm=512, tn=1024, tk=1024, out_dtypes=[F32], name="mm_dw_co")
    dox = mm(dt2, W["w_co"], "nt", tm=1024, tn=512, tk=1024, out_dtypes=[BF16], name="mm_dox")
    dq, dkv = xatt_bwd(sv["q"], sv["kv"], dox)
    G["w_cq"] = mm(sv["h2"], dq, "tn", tm=1024, tn=512, tk=1024, out_dtypes=[F32], name="mm_dw_cq")
    G["w_ckv"] = mm(sv["mem_n"], dkv, "tn", tm=1024, tn=1024, tk=256, out_dtypes=[F32], name="mm_dw_ckv")
    dmem_n = mm(dkv, W["w_ckv"], "nt", tm=256, tn=1024, tk=1024, out_dtypes=[F32], name="mm_dmem")
    _, G["g_mem"] = rms_bwd(mem, P["g_mem"], dmem_n, BF16, "rms_mem_bwd")
    dx1, G["g_x_pre"] = mm(dq, W["w_cq"], "nt", name="mm_dh2", extras=(sv["x1"], dx2), vecs=(P["g_x_pre"],),
                           **pre_bwd)
    proj = sv["proj"]
    dt1, G["g_mix_post"] = rms_bwd(sv["t1"], P["g_mix_post"], dx1, BF16, "rms_post_bwd")
    G["w_out"] = mm(sv["merged"], dt1, "tn", tm=1024, tn=1024, tk=1024, out_dtypes=[F32], name="mm_dw_out")
    dmerged = mm(dt1, W["w_out"], "nt", tm=1024, tn=1024, tk=1024, out_dtypes=[F32], name="mm_dmerged")
    dgates, dups, G["gate_b"] = merge_bwd(proj, W["gate_b"], sv["branches"], W["w_up"], dmerged)
    dbr, dwup = [], []
    for i in range(4):
        dbr.append(mm(dups[i], W["w_up"][i], "nt", tm=1024, tn=512, tk=1024, out_dtypes=[F32], name="mm_dbranch"))
        dwup.append(mm(sv["branches"][i], dups[i], "tn", tm=512, tn=1024, tk=1024, out_dtypes=[F32], name="mm_dw_up"))
    G["w_up"] = jnp.concatenate(dwup, axis=0)
    d_pool, G["pool_w"], G["pool_scale"] = pool_bwd(proj, P["pool_w"], P["pool_scale"], dbr[0])
    cbar = att_combine_bwd(dbr[1], sv["os"], sv["wts"], headsum)
    dqs, dks, dvs, dbias = [], [], [], []
    for g, (win, dil) in enumerate(DIL_GROUPS):
        dq_g, dk_g, dv_g, db_g = att_bwd(proj, biases[g], sv["lses"][g], sv["wts"][g], dbr[1], cbar, g, dil)
        dqs.append(dq_g)
        dks.append(dk_g)
        dvs.append(dv_g)
        dbias.append(db_g)
    G["att_bias"] = dbias
    s5p = P["s5"]
    dy, G["w_glu"], G["b_glu"] = glu_bwd(sv["y"], W["w_glu"], P["b_glu"], dbr[2])
    d_ssm, dbre, dbim, dar, dai, dcre, dcim, G["d_skip"] = s5_bwd(
        proj, sv["hr"], sv["hi"], dy, s5p[2], s5p[3], s5p[0], s5p[1], s5p[4], s5p[5], P["d_skip"])
    G["s5"] = (dar, dai, dbre, dbim, dcre, dcim)
    dzu, dzv, G["sgu_ln_g"], G["sgu_ln_b"], G["w_s"], G["b_s_t"] = sgu_bwd(
        proj, P["sgu_ln_g"], P["sgu_ln_b"], P["w_s"], P["b_s_t"], dbr[3])
    d_qkv = [d.astype(BF16) for d in dqs + dks + dvs]
    dproj = jnp.concatenate([d_pool] + d_qkv + [d_ssm, dzu, dzv] + list(dgates), axis=1)
    sent = emit(("gate_b", "w_glu", "w_up", "w_out", "w_cq", "w_ckv", "w_co"), G)
    G["w_in"] = mm(sv["h1"], dproj, "tn", out_dtypes=[F32], name="mm_dw_in", after=sent)
    sent = emit(("w_in",), G)
    dx0, G["g_mix_pre"] = mm(dproj, W["w_in"], "nt", name="mm_dh1", extras=(sv["x0"], dx1), vecs=(P["g_mix_pre"],),
                             after=sent, **pre_bwd)
    return dx0, G


def _as3d(name, a):
    shape2d, axis = SHARDED[name]
    rows, cols = shape2d
    if axis == 0:
        rows //= 4
    else:
        cols //= 4
    return a.reshape(DEPTH, rows, cols)


def kernel(x, mem, rel_bias, g_mix_pre, g_mix_post, w_in, gate_b, pool_w, pool_scale, a_re, a_im, log_dt, b_re, b_im, c_re, c_im, d_skip, w_glu, b_glu, sgu_ln_g, sgu_ln_b, w_s, b_s, w_up, w_out, g_x_pre, g_x_post, g_mem, w_cq, w_ckv, w_co, g_ff_pre, g_ff_post, w_ff1, w_ff2, loss_target, m_rel_bias, m_g_mix_pre, m_g_mix_post, m_w_in, m_gate_b, m_pool_w, m_pool_scale, m_a_re, m_a_im, m_log_dt, m_b_re, m_b_im, m_c_re, m_c_im, m_d_skip, m_w_glu, m_b_glu, m_sgu_ln_g, m_sgu_ln_b, m_w_s, m_b_s, m_w_up, m_w_out, m_g_x_pre, m_g_x_post, m_g_mem, m_w_cq, m_w_ckv, m_w_co, m_g_ff_pre, m_g_ff_post, m_w_ff1, m_w_ff2, v_rel_bias, v_g_mix_pre, v_g_mix_post, v_w_in, v_gate_b, v_pool_w, v_pool_scale, v_a_re, v_a_im, v_log_dt, v_b_re, v_b_im, v_c_re, v_c_im, v_d_skip, v_w_glu, v_b_glu, v_sgu_ln_g, v_sgu_ln_b, v_w_s, v_b_s, v_w_up, v_w_out, v_g_x_pre, v_g_x_post, v_g_mem, v_w_cq, v_w_ckv, v_w_co, v_g_ff_pre, v_g_ff_post, v_w_ff1, v_w_ff2):
    env = dict(locals())
    weights = {n: env[n] for n in WEIGHT_NAMES}
    mom_m = {n: env["m_" + n] for n in WEIGHT_NAMES}
    mom_v = {n: env["v_" + n] for n in WEIGHT_NAMES}
    x2d = x.reshape(x.shape[1], D_MODEL)
    mem2d = mem.reshape(N_MEM, D_MODEL)
    target = loss_target.reshape(x2d.shape)

    axis_of = {n: SHARDED[n][1] for n in SHARDED_NAMES}
    chip = _chip_index().astype(jnp.int32).reshape(1)
    rest_names = [n for n in SHARDED_NAMES if n != "w_in"]

    def gather_start(l, names, tag, after=()):
        shards = [_as3d(n, weights[n])[l].astype(F32 if n == "gate_b" else MXU_DTYPE) for n in names]
        ax = [axis_of[n] for n in names]
        lands = [jnp.concatenate([s] * 4, axis=a) for s, a in zip(shards, ax)]
        return (names, ax, tag) + chips_start(shards, lands, ax, "gather", f"gather_start_{tag}", after=after)

    def gather_wait(started, after):
        names, ax, tag, send, recv, shards, lands, _ = started
        _, lands = chips_wait(send, recv, shards, lands, ax, "gather", f"gather_wait_{tag}", after=after)
        W = dict(zip(names, lands))
        if "w_up" in W:
            W["w_up"] = W["w_up"].reshape(4, W_BRANCH, D_MODEL)
        return W

    biases = [att_bias(rel_bias, g, dil) for g, (_, dil) in enumerate(DIL_GROUPS)]
    lanes = np.arange(W_BRANCH) // ATT_HEAD_DIM
    headsum = jnp.asarray(lanes[:, None] == lanes[None, :], dtype=BF16)

    def small_params(l, s5_prepared):
        vec = lambda a: a[l].reshape(1, -1)
        return {
            "g_mix_pre": vec(g_mix_pre), "g_mix_post": vec(g_mix_post), "g_x_pre": vec(g_x_pre), "g_x_post": vec(g_x_post),
            "g_mem": vec(g_mem), "g_ff_pre": vec(g_ff_pre), "g_ff_post": vec(g_ff_post), "pool_w": pool_w[l],
            "pool_scale": vec(pool_scale), "d_skip": vec(d_skip), "b_glu": vec(b_glu), "sgu_ln_g": vec(sgu_ln_g),
            "sgu_ln_b": vec(sgu_ln_b), "w_s": w_s[l], "b_s_t": b_s[l].T, "s5": s5_prepared,
        }

    Ws, Ps, saved, s5_vjps = [], [], [], []
    xl = x2d
    hl = rms_fwd(x2d, g_mix_pre[0].reshape(1, -1), BF16, "rms_pre")
    flying = {"next": gather_start(0, ["w_in"], "0_w_in")}
    for l in range(DEPTH):
        s5_prepared, s5_vjp = jax.vjp(s5_params, a_re[l], a_im[l], log_dt[l], b_re[l], b_im[l], c_re[l], c_im[l])
        token_of = lambda started: (started[7],)
        if l == 0:
            w_in_l = gather_wait(flying["next"], [*biases, hl])["w_in"]
            flying["rest"] = gather_start(0, rest_names, "0_rest", after=[w_in_l])
            first_after = token_of(flying["rest"])

            def rest_of(stage, value):
                if stage != "mixer":
                    return {}, ()
                W = gather_wait(flying["rest"], [value])
                flying["next"] = gather_start(1, SHARDED_NAMES, "1", after=[W["w_out"]])
                return W, token_of(flying["next"])
        else:
            W_l = gather_wait(flying["next"], [xl])
            w_in_l, first_after = W_l["w_in"], ()
            if l + 1 < DEPTH:
                flying["next"] = gather_start(l + 1, SHARDED_NAMES, str(l + 1), after=[w_in_l])
                first_after = token_of(flying["next"])
            rest_of = lambda stage, value, W_l=W_l: (W_l if stage == "mixer" else {}, ())
        P = small_params(l, s5_prepared)
        g_next = g_mix_pre[l + 1].reshape(1, -1) if l + 1 < DEPTH else None
        xl, hl, sv = layer_fwd(xl, hl, mem2d, w_in_l, rest_of, P, biases, g_next, after=first_after)
        Ws.append(sv["W"])
        Ps.append(P)
        saved.append(sv)
        s5_vjps.append(s5_vjp)
    loss_local, dx = loss_and_grad(xl, target)
    loss = lax.psum(loss_local, ("x", "y", "c"))

    scattered = []

    def scatter_start(l, names, srcs):
        ax = [axis_of.get(n) for n in names]
        lands = []
        for s, a in zip(srcs, ax):
            r, c = s.shape
            lands.append(lax.empty((4, r // 4 if a == 0 else r, c // 4 if a == 1 else c), F32))
        tag = f"{l}_{names[0]}"
        send, recv, srcs, lands, token = chips_start(srcs, lands, ax, "scatter", f"grads_start_{tag}")
        scattered.append((l, names, ax, tag, send, recv, srcs, lands))
        return (token,)

    grads = [None] * DEPTH
    for l in reversed(range(DEPTH)):
        emit = lambda names, G, l=l: scatter_start(l, list(names), [G[n] for n in names])
        dx, grads[l] = layer_bwd(dx, mem2d, Ws[l], Ps[l], biases, saved[l], headsum, emit)
    grad_x = dx.reshape(x.shape)

    rep = {}
    stack = lambda key, shape: jnp.stack([grads[l][key] for l in range(DEPTH)]).reshape(shape)
    for n in ("g_mix_pre", "g_mix_post", "g_x_pre", "g_x_post", "g_mem", "g_ff_pre", "g_ff_post"):
        rep[n] = stack(n, (DEPTH, D_MODEL))
    for n in ("pool_scale", "d_skip", "b_glu", "sgu_ln_g", "sgu_ln_b"):
        rep[n] = stack(n, (DEPTH, W_BRANCH))
    rep["pool_w"] = stack("pool_w", pool_w.shape)
    rep["w_s"] = stack("w_s", w_s.shape)
    rep["b_s"] = jnp.stack([grads[l]["b_s_t"][:, :4].T for l in range(DEPTH)])
    s5_grads = [s5_vjps[l](tuple(grads[l]["s5"])) for l in range(DEPTH)]
    for i, n in enumerate(("a_re", "a_im", "log_dt", "b_re", "b_im", "c_re", "c_im")):
        rep[n] = jnp.stack([s5_grads[l][i] for l in range(DEPTH)])
    dbias = [sum(grads[l]["att_bias"][g] for l in range(DEPTH)) for g in range(len(DIL_GROUPS))]
    rep["rel_bias"] = jnp.concatenate([att_bias_grad(dbias[g], dil) for g, (_, dil) in enumerate(DIL_GROUPS)], axis=1)
    rep_shapes = [weights[n].shape for n in REPLICATED_NAMES]
    packed_g = pack_small([rep[n] for n in REPLICATED_NAMES])

    small_sent = scatter_start(0, ["small"], [packed_g])
    stacked = {}

    def collect(record, after):
        l, names, ax, tag, send, recv, srcs, lands = record
        srcs, lands = chips_wait(send, recv, srcs, lands, ax, "scatter", f"grads_wait_{tag}", after=after)
        for n, own, arrived, a in zip(names, srcs, lands, ax):
            if n not in stacked:
                stacked[n] = lax.empty((1 if n == "small" else DEPTH,) + arrived.shape[1:], F32)
            stacked[n] = sum_chips(arrived, own, a, chip, stacked[n], 0 if n == "small" else l, "sum_chips")

    out_g, out_d, out_m, out_v = {}, {}, {}, {}

    def update(names, tag):
        partial = [stacked[n].reshape(-1, stacked[n].shape[-1]) for n in names]
        other = swap_cores(partial, f"swap_cores_{tag}")
        for n, mine, theirs in zip(names, partial, other):
            if n == "small":
                for name, ga, gb in zip(REPLICATED_NAMES, unpack_small(mine, rep_shapes), unpack_small(theirs, rep_shapes)):
                    rows_of = lambda a: a.reshape(-1, a.shape[-1])
                    res = adamw(rows_of(weights[name]), rows_of(ga), rows_of(gb), rows_of(mom_m[name]),
                                rows_of(mom_v[name]), "adamw_small")
                    out_g[name], out_d[name], out_m[name], out_v[name] = [r.reshape(weights[name].shape) for r in res]
            else:
                flat = lambda a: a.reshape(mine.shape)
                res = adamw(flat(weights[n]), mine, theirs, flat(mom_m[n]), flat(mom_v[n]), "adamw")
                out_g[n], out_d[n], out_m[n], out_v[n] = [r.reshape(weights[n].shape) for r in res]

    late = [r for r in scattered if r[1] == ["small"] or (r[0] == 0 and r[1] == ["w_in"])]
    for record in scattered:
        if not any(record is r for r in late):
            collect(record, [dx, *small_sent])
    update(rest_names, "rest")
    collect(late[0], [out_d[n] for n in rest_names])
    update(["w_in"], "w_in")
    collect(late[1], [out_d["w_in"]])
    update(["small"], "small")

    return (loss, grad_x, *[out_g[n] for n in WEIGHT_NAMES], *[out_d[n] for n in WEIGHT_NAMES],
            *[out_m[n] for n in WEIGHT_NAMES], *[out_v[n] for n in WEIGHT_NAMES])
```

```python
import functools
import math

import numpy as np
import jax
import jax.numpy as jnp
from jax import lax
from jax.experimental import pallas as pl
from jax.experimental.pallas import tpu as pltpu

F32 = jnp.float32
BF16 = jnp.bfloat16
MXU_DTYPE = jnp.bfloat16
MESH_ID = pl.DeviceIdType.MESH
VMEM_LIMIT_BYTES = 56 * 1024 * 1024

D_MODEL = 1024
DEPTH = 4
N_MEM = 256
W_BRANCH = 512
POOL_WINDOWS = (2, 4, 8, 16)
POOL_HALO = 16
DIL_GROUPS = ((128, 1), (512, 4), (2048, 16))
BAND = 128
ATT_HEADS = 8
ATT_HEAD_DIM = 64
SSM_GROUP = 16
SSM_GROUPS = 32
SSM_STATE = 64
SSM_COLS = SSM_GROUPS * SSM_STATE
SSM_T = 512
SGU_CHUNK = 128
X_HEADS = 4
X_HEAD_DIM = 128
D_FF = 4096
REL_BUCKETS = 32
REL_MAX_DIST = 2048
EPS = 1e-6
NEG_INF = -1e30
OFF_POOL = 0
OFF_ATT = 512
OFF_SSM = OFF_ATT + 9 * W_BRANCH
OFF_SGU = OFF_SSM + W_BRANCH
OFF_GATE = OFF_SGU + 2 * W_BRANCH
IN_WIDTH = OFF_GATE + 4 * D_MODEL

ADAM_LR = 0.001
ADAM_B1 = 0.9
ADAM_B2 = 0.999
ADAM_EPS = 1e-08
ADAM_WD = 0.01
ADAM_STEP = 10

GELU_C = math.sqrt(2.0 / math.pi)

WEIGHT_NAMES = ['rel_bias', 'g_mix_pre', 'g_mix_post', 'w_in', 'gate_b', 'pool_w', 'pool_scale', 'a_re', 'a_im',
                'log_dt', 'b_re', 'b_im', 'c_re', 'c_im', 'd_skip', 'w_glu', 'b_glu', 'sgu_ln_g', 'sgu_ln_b',
                'w_s', 'b_s', 'w_up', 'w_out', 'g_x_pre', 'g_x_post', 'g_mem', 'w_cq', 'w_ckv', 'w_co',
                'g_ff_pre', 'g_ff_post', 'w_ff1', 'w_ff2']
SHARDED = {
    'w_in': ((D_MODEL, IN_WIDTH), 1),
    'gate_b': ((4, D_MODEL), 1),
    'w_glu': ((W_BRANCH, W_BRANCH), 0),
    'w_up': ((4 * W_BRANCH, D_MODEL), 1),
    'w_out': ((D_MODEL, D_MODEL), 0),
    'w_cq': ((D_MODEL, W_BRANCH), 0),
    'w_ckv': ((D_MODEL, D_MODEL), 0),
    'w_co': ((W_BRANCH, D_MODEL), 1),
    'w_ff1': ((D_MODEL, D_FF), 1),
    'w_ff2': ((D_FF, D_MODEL), 0),
}
SHARDED_NAMES = list(SHARDED)
REPLICATED_NAMES = [n for n in WEIGHT_NAMES if n not in SHARDED]


def _cp(n_axes):
    return pltpu.CompilerParams(dimension_semantics=("arbitrary",) * n_axes, vmem_limit_bytes=VMEM_LIMIT_BYTES)


def _dot(a, b, dims="nn"):
    cd = {"nn": ((1,), (0,)), "nt": ((1,), (1,)), "tn": ((0,), (0,))}[dims]
    return lax.dot_general(a.astype(MXU_DTYPE), b.astype(MXU_DTYPE), (cd, ((), ())), preferred_element_type=F32)


def _gelu(x):
    return 0.5 * x * (1.0 + jnp.tanh(GELU_C * (x + 0.044715 * (x * x * x))))


def _gelu_grad(x):
    t = jnp.tanh(GELU_C * (x + 0.044715 * (x * x * x)))
    return 0.5 * (1.0 + t) + 0.5 * x * (1.0 - t * t) * (GELU_C * (1.0 + 3.0 * 0.044715 * (x * x)))


def _sigmoid(x):
    return 1.0 / (1.0 + jnp.exp(-x))


MM_TILES = {
    "mm_w_in": (2048, 1536, 1024), "mm_dw_in": (1024, 1536, 2048), "mm_dh1": (1024, 1024, 1536),
    "mm_w_ff1": (2048, 1024, 1024), "mm_w_ff2": (1024, 1024, 2048), "mm_w_ff2_last": (1024, 1024, 2048),
    "mm_dw_ff2": (1024, 1024, 2048), "mm_dact": (2048, 1024, 1024), "mm_dw_ff1": (1024, 1024, 2048),
    "mm_dh3": (1024, 1024, 2048), "mm_dh2": (1024, 1024, 512),
}


def mm(a, b, dims, *, out_dtypes, name, tm=None, tn=None, tk=None, extras=(), vecs=(), epi=None, n_sums=0, after=()):
    if dims == "tn":
        K, M = a.shape
        N = b.shape[1]
    else:
        M, K = a.shape
        N = b.shape[1] if dims == "nn" else b.shape[0]
    if tm is None:
        tm, tn, tk = MM_TILES[name]
    tm, tn, tk = min(tm, M), min(tn, N), min(tk, K)
    assert M % tm == 0 and N % tn == 0 and K % tk == 0, (name, M, N, K, tm, tn, tk)
    assert n_sums == 0 or tn == N, name
    nk = K // tk
    ne, no = len(extras) + len(vecs), len(out_dtypes)
    if epi is None:
        epi = lambda acc: (acc,)
    a_spec = (pl.BlockSpec((tk, tm), lambda i, j, k: (k, i)) if dims == "tn"
              else pl.BlockSpec((tm, tk), lambda i, j, k: (i, k)))
    b_spec = (pl.BlockSpec((tn, tk), lambda i, j, k: (j, k)) if dims == "nt"
              else pl.BlockSpec((tk, tn), lambda i, j, k: (k, j)))
    mn_spec = pl.BlockSpec((tm, tn), lambda i, j, k: (i, j))
    vec_spec = pl.BlockSpec((1, tn), lambda i, j, k: (0, j))

    def body(a_ref, b_ref, *rest):
        first_out = ne + len(after)
        extra_refs, out_refs = rest[:ne], rest[first_out:first_out + no]
        sum_refs = rest[first_out + no:first_out + no + n_sums]
        part = _dot(a_ref[...], b_ref[...], dims)

        def finish(acc):
            results = epi(acc, *[e[...] for e in extra_refs])
            for o_ref, r in zip(out_refs, results[:no]):
                o_ref[...] = r.astype(o_ref.dtype)
            for s_ref, r in zip(sum_refs, results[no:]):
                @pl.when(pl.program_id(0) == 0)
                def _():
                    s_ref[...] = r

                @pl.when(pl.program_id(0) > 0)
                def _():
                    s_ref[...] += r

        if nk == 1:
            finish(part)
        else:
            acc_ref = rest[-1]
            k = pl.program_id(2)

            @pl.when(k == 0)
            def _():
                acc_ref[...] = part

            @pl.when(k > 0)
            def _():
                acc_ref[...] += part

            @pl.when(k == nk - 1)
            def _():
                finish(acc_ref[...])

    outs = pl.pallas_call(
        body, name=name, grid=(M // tm, N // tn, nk),
        in_specs=[a_spec, b_spec] + [mn_spec] * len(extras) + [vec_spec] * len(vecs) + [ANY] * len(after),
        out_specs=[mn_spec] * no + [vec_spec] * n_sums,
        out_shape=[jax.ShapeDtypeStruct((M, N), dt) for dt in out_dtypes] + [jax.ShapeDtypeStruct((1, N), F32)] * n_sums,
        scratch_shapes=[pltpu.VMEM((tm, tn), F32)] if nk > 1 else [],
        compiler_params=_cp(3),
    )(a, b, *extras, *vecs, *after)
    return outs[0] if no + n_sums == 1 else outs


ROW_TILE = 512


def rms_fwd(x, g, out_dtype, name, res=None):
    M, D = x.shape
    tm = min(ROW_TILE, M)

    def body(x_ref, g_ref, *rest):
        o_ref = rest[-1]
        xf = x_ref[...]
        y = xf * lax.rsqrt(jnp.mean(xf * xf, axis=-1, keepdims=True) + EPS) * g_ref[...]
        if res is not None:
            y = y + rest[0][...]
        o_ref[...] = y.astype(o_ref.dtype)

    row = pl.BlockSpec((tm, D), lambda i: (i, 0))
    return pl.pallas_call(
        body, name=name, grid=(M // tm,),
        in_specs=[row, pl.BlockSpec((1, D), lambda i: (0, 0))] + ([row] if res is not None else []),
        out_specs=row, out_shape=jax.ShapeDtypeStruct((M, D), out_dtype), compiler_params=_cp(1),
    )(x, g, *([res] if res is not None else []))


def rms_bwd(x, g, dy, dx_dtype, name, add=None, after=()):
    M, D = x.shape
    tm = min(ROW_TILE, M)

    def body(x_ref, g_ref, dy_ref, *rest):
        dx_ref, dg_ref = rest[-2], rest[-1]
        xf = x_ref[...]
        dyf = dy_ref[...].astype(F32)
        r = lax.rsqrt(jnp.mean(xf * xf, axis=-1, keepdims=True) + EPS)
        xn = xf * r
        dxn = dyf * g_ref[...]
        dx = r * (dxn - xn * jnp.mean(dxn * xn, axis=-1, keepdims=True))
        if add is not None:
            dx = dx + rest[0][...]
        dx_ref[...] = dx.astype(dx_ref.dtype)
        dg = jnp.sum(dyf * xn, axis=0, keepdims=True)

        @pl.when(pl.program_id(0) == 0)
        def _():
            dg_ref[...] = dg

        @pl.when(pl.program_id(0) > 0)
        def _():
            dg_ref[...] += dg

    row = pl.BlockSpec((tm, D), lambda i: (i, 0))
    vec = pl.BlockSpec((1, D), lambda i: (0, 0))
    return pl.pallas_call(
        body, name=name, grid=(M // tm,),
        in_specs=[row, vec, row] + ([row] if add is not None else []) + [ANY] * len(after),
        out_specs=[row, vec],
        out_shape=[jax.ShapeDtypeStruct((M, D), dx_dtype), jax.ShapeDtypeStruct((1, D), F32)],
        compiler_params=_cp(1),
    )(x, g, dy, *([add] if add is not None else []), *after)


def loss_and_grad(y, target):
    M, D = y.shape
    tm = ROW_TILE

    def body(y_ref, t_ref, part_ref, dy_ref):
        e = y_ref[...] - t_ref[...]
        dy_ref[...] = e / D
        part_ref[...] = jnp.broadcast_to(0.5 * jnp.sum(jnp.mean(e * e, axis=-1, keepdims=True), axis=0, keepdims=True),
                                         (8, 128))

    row = pl.BlockSpec((tm, D), lambda i: (i, 0))
    part, dy = pl.pallas_call(
        body, name="loss", grid=(M // tm,), in_specs=[row, row],
        out_specs=[pl.BlockSpec((8, 128), lambda i: (i, 0)), row],
        out_shape=[jax.ShapeDtypeStruct((8 * (M // tm), 128), F32), jax.ShapeDtypeStruct((M, D), F32)],
        compiler_params=_cp(1),
    )(y, target)
    return jnp.sum(part[::8, 0]), dy


POOL_ROWS = 512


def _pool_window_sum(xw, gi, roll_of):
    s1 = xw + pltpu.roll(xw, roll_of(1), 0)
    s2 = s1 + pltpu.roll(s1, roll_of(2), 0)
    s3 = s2 + pltpu.roll(s2, roll_of(4), 0)
    s4 = s3 + pltpu.roll(s3, roll_of(8), 0)
    return jnp.where(gi == 0, s1, jnp.where(gi == 1, s2, jnp.where(gi == 2, s3, s4)))


def _pool_cnt(i, gi):
    rows = lax.broadcasted_iota(jnp.int32, (POOL_ROWS, 128), 0) + i * POOL_ROWS
    w = jnp.where(gi == 0, 2, jnp.where(gi == 1, 4, jnp.where(gi == 2, 8, 16)))
    return jnp.minimum(rows + 1, w).astype(F32)


def pool_fwd(proj, pool_w, scale):
    S = proj.shape[0]
    nchunk = S // POOL_ROWS
    slab = POOL_ROWS + POOL_HALO

    def body(x_ref, w_ref, sc_ref, o_ref, pad_ref):
        gi = pl.program_id(0)
        pad_ref[0:POOL_HALO, :] = jnp.zeros((POOL_HALO, 128), F32)
        pad_ref[POOL_HALO:, :] = x_ref[...]
        for i in range(nchunk):
            xw = pad_ref[i * POOL_ROWS:i * POOL_ROWS + slab, :]
            ssum = _pool_window_sum(xw, gi, lambda d: d)[POOL_HALO:, :]
            p = ssum / _pool_cnt(i, gi) - xw[POOL_HALO:, :]
            o_ref[i * POOL_ROWS:(i + 1) * POOL_ROWS, :] = (_dot(p, w_ref[...]) * sc_ref[...]).astype(o_ref.dtype)

    return pl.pallas_call(
        body, name="pool_fwd", grid=(4,),
        in_specs=[pl.BlockSpec((S, 128), lambda g: (0, OFF_POOL // 128 + g)),
                  pl.BlockSpec((None, 128, 128), lambda g: (g, 0, 0)),
                  pl.BlockSpec((1, 128), lambda g: (0, g))],
        out_specs=pl.BlockSpec((S, 128), lambda g: (0, g)),
        out_shape=jax.ShapeDtypeStruct((S, W_BRANCH), BF16),
        scratch_shapes=[pltpu.VMEM((S + POOL_HALO, 128), F32)],
        compiler_params=_cp(1),
    )(proj, pool_w, scale)


def pool_bwd(proj, pool_w, scale, dy):
    S = proj.shape[0]
    nchunk = S // POOL_ROWS
    slab = POOL_ROWS + POOL_HALO

    def body(x_ref, w_ref, sc_ref, dy_ref, dx_ref, dw_ref, dsc_ref, pad_ref, pad2_ref, dp_ref):
        gi = pl.program_id(0)
        pad_ref[0:POOL_HALO, :] = jnp.zeros((POOL_HALO, 128), F32)
        pad_ref[POOL_HALO:, :] = x_ref[...]
        pad2_ref[S:, :] = jnp.zeros((POOL_HALO, 128), F32)
        dw = jnp.zeros((128, 128), F32)
        dsc = jnp.zeros((1, 128), F32)
        for i in range(nchunk):
            xw = pad_ref[i * POOL_ROWS:i * POOL_ROWS + slab, :]
            cnt = _pool_cnt(i, gi)
            p = _pool_window_sum(xw, gi, lambda d: d)[POOL_HALO:, :] / cnt - xw[POOL_HALO:, :]
            dyc = dy_ref[i * POOL_ROWS:(i + 1) * POOL_ROWS, :]
            dsc = dsc + jnp.sum(dyc * _dot(p, w_ref[...]), axis=0, keepdims=True)
            dys = dyc * sc_ref[...]
            dw = dw + _dot(p, dys, "tn")
            dp = _dot(dys, w_ref[...], "nt")
            dp_ref[i * POOL_ROWS:(i + 1) * POOL_ROWS, :] = dp
            pad2_ref[i * POOL_ROWS:(i + 1) * POOL_ROWS, :] = dp / cnt
        dw_ref[...] = dw
        dsc_ref[...] = dsc
        for i in range(nchunk):
            xw = pad2_ref[i * POOL_ROWS:i * POOL_ROWS + slab, :]
            fsum = _pool_window_sum(xw, gi, lambda d: slab - d)[:POOL_ROWS, :]
            rows = slice(i * POOL_ROWS, (i + 1) * POOL_ROWS)
            dx_ref[rows, :] = (fsum - dp_ref[rows, :]).astype(dx_ref.dtype)

    return pl.pallas_call(
        body, name="pool_bwd", grid=(4,),
        in_specs=[pl.BlockSpec((S, 128), lambda g: (0, OFF_POOL // 128 + g)),
                  pl.BlockSpec((None, 128, 128), lambda g: (g, 0, 0)),
                  pl.BlockSpec((1, 128), lambda g: (0, g)),
                  pl.BlockSpec((S, 128), lambda g: (0, g))],
        out_specs=[pl.BlockSpec((S, 128), lambda g: (0, g)),
                   pl.BlockSpec((None, 128, 128), lambda g: (g, 0, 0)),
                   pl.BlockSpec((1, 128), lambda g: (0, g))],
        out_shape=[jax.ShapeDtypeStruct((S, W_BRANCH), BF16), jax.ShapeDtypeStruct((4, 128, 128), F32),
                   jax.ShapeDtypeStruct((1, W_BRANCH), F32)],
        scratch_shapes=[pltpu.VMEM((S + POOL_HALO, 128), F32), pltpu.VMEM((S + POOL_HALO, 128), F32),
                        pltpu.VMEM((S, 128), F32)],
        compiler_params=_cp(1),
    )(proj, pool_w, scale, dy)


def _t5_bucket(n):
    exact = REL_BUCKETS // 2
    nf = np.maximum(n, 1).astype(np.float32)
    large = exact + (np.log(nf / exact) / np.log(REL_MAX_DIST / exact) * (REL_BUCKETS - exact)).astype(np.int32)
    large = np.minimum(large, REL_BUCKETS - 1)
    return np.where(n < exact, n, large).astype(np.int32)


def _band_onehot(dil):
    i = np.arange(BAND)[:, None]
    kk = np.arange(2 * BAND)[None, :]
    dist = BAND + i - kk
    local = (dist >= 0) & (dist <= BAND)
    bucket = _t5_bucket(np.clip(dist, 0, BAND) * dil)
    onehot = (bucket.reshape(-1, 1) == np.arange(REL_BUCKETS)[None, :]).astype(np.float32)
    return onehot, local


def att_bias(rel_bias, g, dil):
    onehot, local = _band_onehot(dil)
    tab = jnp.dot(jnp.asarray(onehot), rel_bias[:, g * ATT_HEADS:(g + 1) * ATT_HEADS], precision=lax.Precision.HIGHEST)
    bias = tab.reshape(BAND, 2 * BAND, ATT_HEADS).transpose(2, 0, 1)
    return jnp.where(jnp.asarray(local)[None], bias, NEG_INF)


def att_bias_grad(dbias, dil):
    onehot, _ = _band_onehot(dil)
    flat = dbias.transpose(1, 2, 0).reshape(BAND * 2 * BAND, ATT_HEADS)
    return jnp.dot(jnp.asarray(onehot).T, flat, precision=lax.Precision.HIGHEST)


def _head_lanes():
    return lax.broadcasted_iota(jnp.int32, (BAND, 128), 1) < ATT_HEAD_DIM


def _att_cols(part, g, hp):
    return (OFF_ATT + part * 3 * W_BRANCH + g * W_BRANCH) // 128 + hp


def _att_pair(q, k, v, bias, lse_b, do, delta_b, hh, head0, mask=None):
    sel = head0 if hh == 0 else jnp.logical_not(head0)
    s = _dot(jnp.where(sel, q, 0.0), k, "nt") * (ATT_HEAD_DIM ** -0.5) + bias
    if mask is not None:
        s = jnp.where(mask, NEG_INF, s)
    c = hh * ATT_HEAD_DIM
    p = jnp.exp(s - lse_b[:, c:c + 1])
    dp = _dot(jnp.where(sel, do, 0.0), v, "nt")
    return p, p * (dp - delta_b[:, c:c + 1])


ATT_BLOCKS = {1: 32, 4: 8, 16: 2}


def _att_rows(r, i, d):
    return pl.ds(r + d * BAND * i, BAND, stride=d) if d > 1 else pl.ds(BAND * i, BAND)


def _att_specs(g, d, nq):
    ch, pb = BAND * d * nq, BAND * d
    cur = lambda part: pl.BlockSpec((ch, 128), lambda hp, n: (n, _att_cols(part, g, hp)))
    prev = lambda part: pl.BlockSpec((pb, 128), lambda hp, n: (jnp.maximum(n * nq - 1, 0), _att_cols(part, g, hp)))
    return [cur(0), cur(1), prev(1), cur(2), prev(2)]


def _att_keys(cur_ref, prev_ref, r, i, d):
    before = cur_ref[_att_rows(r, i - 1, d), :] if i > 0 else prev_ref[_att_rows(r, 0, d), :]
    return jnp.concatenate([before, cur_ref[_att_rows(r, i, d), :]], axis=0).astype(MXU_DTYPE)


def att_fwd(proj, bias, g, d):
    S = proj.shape[0]
    nq = ATT_BLOCKS[d]
    ch = BAND * d * nq

    def body(q_ref, kc_ref, kp_ref, vc_ref, vp_ref, b_ref, o_ref, l_ref):
        n = pl.program_id(1)
        head0 = _head_lanes()
        first = jnp.logical_and(lax.broadcasted_iota(jnp.int32, (BAND, 2 * BAND), 1) < BAND, n == 0)
        for r in range(d):
            for i in range(nq):
                rows = _att_rows(r, i, d)
                q = q_ref[rows, :]
                k = _att_keys(kc_ref, kp_ref, r, i, d)
                v = _att_keys(vc_ref, vp_ref, r, i, d)
                o_h, l_h = [], []
                for hh in range(2):
                    qm = jnp.where(head0 if hh == 0 else jnp.logical_not(head0), q, 0.0)
                    s = _dot(qm, k, "nt") * (ATT_HEAD_DIM ** -0.5) + b_ref[hh]
                    if i == 0:
                        s = jnp.where(first, NEG_INF, s)
                    m = jnp.max(s, axis=-1, keepdims=True)
                    p = jnp.exp(s - m)
                    l = jnp.sum(p, axis=-1, keepdims=True)
                    o_h.append(_dot(p / l, v))
                    l_h.append(jnp.broadcast_to(m + jnp.log(l), (BAND, 128)))
                o_ref[rows, :] = jnp.where(head0, o_h[0], o_h[1])
                l_ref[rows, :] = jnp.where(head0, l_h[0], l_h[1])

    out = pl.BlockSpec((ch, 128), lambda hp, n: (n, hp))
    return pl.pallas_call(
        body, name=f"att_fwd_d{d}", grid=(4, S // ch),
        in_specs=_att_specs(g, d, nq) + [pl.BlockSpec((2, BAND, 2 * BAND), lambda hp, n: (hp, 0, 0))],
        out_specs=[out, out],
        out_shape=[jax.ShapeDtypeStruct((S, W_BRANCH), F32), jax.ShapeDtypeStruct((S, W_BRANCH), F32)],
        compiler_params=_cp(2),
    )(proj, proj, proj, proj, proj, bias)


def att_bwd(proj, bias, lse, wts, dout, cbar, g, d):
    S = proj.shape[0]
    nq = ATT_BLOCKS[d]
    ch, pb = BAND * d * nq, BAND * d
    nb = S // ch
    scale = ATT_HEAD_DIM ** -0.5

    def body(q_ref, kc_ref, kp_ref, vc_ref, vp_ref, b_ref, l_ref, w_ref, do_ref, cb_ref,
             dq_ref, dk_ref, dv_ref, ek_ref, ev_ref, db_ref):
        n = pl.program_id(1)
        head0 = _head_lanes()
        first = jnp.logical_and(lax.broadcasted_iota(jnp.int32, (BAND, 2 * BAND), 1) < BAND, n == 0)

        @pl.when(n == 0)
        def _():
            db_ref[...] = jnp.zeros(db_ref.shape, F32)

        for r in range(d):
            own_k = own_v = None
            for i in range(nq):
                rows = _att_rows(r, i, d)
                q = q_ref[rows, :]
                k = _att_keys(kc_ref, kp_ref, r, i, d)
                v = _att_keys(vc_ref, vp_ref, r, i, d)
                w = w_ref[rows, :]
                do = w * do_ref[rows, :]
                delta = w * cb_ref[rows, :]
                lse_b = l_ref[rows, :]
                dq_h, dk_h, dv_h = [], [], []
                for hh in range(2):
                    p, ds = _att_pair(q, k, v, b_ref[hh], lse_b, do, delta, hh, head0, mask=first if i == 0 else None)
                    db_ref[hh] += ds
                    ds = ds * scale
                    dq_h.append(_dot(ds, k))
                    dk_h.append(_dot(ds, q, "tn"))
                    dv_h.append(_dot(p, do, "tn"))
                dq_ref[rows, :] = jnp.where(head0, dq_h[0], dq_h[1])
                head0_keys = jnp.concatenate([head0, head0], axis=0)
                dk2 = jnp.where(head0_keys, dk_h[0], dk_h[1])
                dv2 = jnp.where(head0_keys, dv_h[0], dv_h[1])
                if i == 0:
                    ek_ref[_att_rows(r, 0, d), :] = dk2[:BAND]
                    ev_ref[_att_rows(r, 0, d), :] = dv2[:BAND]
                else:
                    dk_ref[_att_rows(r, i - 1, d), :] = own_k + dk2[:BAND]
                    dv_ref[_att_rows(r, i - 1, d), :] = own_v + dv2[:BAND]
                own_k, own_v = dk2[BAND:], dv2[BAND:]
            dk_ref[_att_rows(r, nq - 1, d), :] = own_k
            dv_ref[_att_rows(r, nq - 1, d), :] = own_v

    cur = pl.BlockSpec((ch, 128), lambda hp, n: (n, hp))
    edge = pl.BlockSpec((pb, 128), lambda hp, n: (n, hp))
    bias_spec = pl.BlockSpec((2, BAND, 2 * BAND), lambda hp, n: (hp, 0, 0))
    big = jax.ShapeDtypeStruct((S, W_BRANCH), F32)
    small = jax.ShapeDtypeStruct((nb * pb, W_BRANCH), F32)
    dq, dk, dv, ek, ev, db = pl.pallas_call(
        body, name=f"att_bwd_d{d}", grid=(4, nb),
        in_specs=_att_specs(g, d, nq) + [bias_spec, cur, cur, cur, cur],
        out_specs=[cur, cur, cur, edge, edge, bias_spec],
        out_shape=[big, big, big, small, small, jax.ShapeDtypeStruct((ATT_HEADS, BAND, 2 * BAND), F32)],
        compiler_params=_cp(2),
    )(proj, proj, proj, proj, proj, bias, lse, wts, dout, cbar)

    def with_edges(main, edges):
        if nb == 1:
            return main
        main = main.reshape(nb, ch, W_BRANCH)
        add = jnp.pad(edges.reshape(nb, pb, W_BRANCH)[1:], ((0, 1), (ch - pb, 0), (0, 0)))
        return (main + add).reshape(S, W_BRANCH)

    return dq, with_edges(dk, ek), with_edges(dv, ev), db


def att_combine(os_, lses):
    S = os_[0].shape[0]

    def body(o0, o1, o2, l0, l1, l2, out_ref, w0, w1, w2):
        ls = [l0[...], l1[...], l2[...]]
        m = jnp.maximum(jnp.maximum(ls[0], ls[1]), ls[2])
        es = [jnp.exp(l - m) for l in ls]
        den = es[0] + es[1] + es[2]
        ws = [e / den for e in es]
        out_ref[...] = (ws[0] * o0[...] + ws[1] * o1[...] + ws[2] * o2[...]).astype(out_ref.dtype)
        for w_ref, w in zip((w0, w1, w2), ws):
            w_ref[...] = w

    blk = pl.BlockSpec((ROW_TILE, W_BRANCH), lambda i: (i, 0))
    f = jax.ShapeDtypeStruct((S, W_BRANCH), F32)
    return pl.pallas_call(
        body, name="att_combine", grid=(S // ROW_TILE,), in_specs=[blk] * 6, out_specs=[blk] * 4,
        out_shape=[jax.ShapeDtypeStruct((S, W_BRANCH), BF16), f, f, f], compiler_params=_cp(1),
    )(*os_, *lses)


def _split3(x):
    x1 = x.astype(BF16)
    r1 = x - x1.astype(F32)
    x2 = r1.astype(BF16)
    x3 = (r1 - x2.astype(F32)).astype(BF16)
    return x1, x2, x3


def att_combine_bwd(dout, os_, wts, headsum):
    S = dout.shape[0]

    def body(do_ref, o0, o1, o2, w0, w1, w2, e_ref, cb_ref):
        out = w0[...] * o0[...] + w1[...] * o1[...] + w2[...] * o2[...]
        e = e_ref[...]
        acc = jnp.zeros((ROW_TILE, W_BRANCH), F32)
        for term in _split3(do_ref[...] * out):
            acc = acc + jnp.dot(term, e, preferred_element_type=F32)
        cb_ref[...] = acc

    blk = pl.BlockSpec((ROW_TILE, W_BRANCH), lambda i: (i, 0))
    return pl.pallas_call(
        body, name="att_combine_bwd", grid=(S // ROW_TILE,),
        in_specs=[blk] * 7 + [pl.BlockSpec((W_BRANCH, W_BRANCH), lambda i: (0, 0))], out_specs=blk,
        out_shape=jax.ShapeDtypeStruct((S, W_BRANCH), F32), compiler_params=_cp(1),
    )(dout, *os_, *wts, headsum)


def _cmul(ar, ai, br, bi):
    return ar * br - ai * bi, ar * bi + ai * br


SCAN_ROWS = 8
SCAN_GROUPS = SSM_T // SCAN_ROWS


def _log_scan(xr, xi, mr, mi, rows, n, steps, reverse):
    total = xr.shape[0]
    for k in range(steps):
        dd = 1 << k
        keep = rows < n - dd if reverse else rows >= dd
        shift = total - dd if reverse else dd
        ar, ai = _cmul(mr, mi, jnp.where(keep, pltpu.roll(xr, shift, 0), 0.0), jnp.where(keep, pltpu.roll(xi, shift, 0), 0.0))
        xr, xi = xr + ar, xi + ai
        mr, mi = _cmul(mr, mi, mr, mi)
    return xr, xi, mr, mi


def _scan_scratch(n_results):
    return ([pltpu.VMEM((W_BRANCH // 128, SSM_T, 128), F32)] * 2 + [pltpu.VMEM((SCAN_GROUPS, W_BRANCH), F32)] * 2
            + [pltpu.VMEM((SSM_T, W_BRANCH), F32)] * n_results)


def _block_scan(xr, xi, mr, mi, reverse, yr_ref, yi_ref, er_ref, ei_ref, hr_ref, hi_ref):
    cols = xr.shape[1]
    rows = lax.broadcasted_iota(jnp.int32, (SSM_T, cols), 0)
    yr, yi, m8r, m8i = _log_scan(xr, xi, mr, mi, rows & (SCAN_ROWS - 1), SCAN_ROWS, 3, reverse)
    lane_blocks = range(cols // 128)
    for c in lane_blocks:
        yr_ref[c] = yr[:, c * 128:(c + 1) * 128]
        yi_ref[c] = yi[:, c * 128:(c + 1) * 128]
    wide = lambda ref, rows_: jnp.concatenate([ref[c, rows_, :] for c in lane_blocks], axis=1)
    end = pl.ds(0 if reverse else SCAN_ROWS - 1, SCAN_GROUPS, stride=SCAN_ROWS)
    groups = lax.broadcasted_iota(jnp.int32, (SCAN_GROUPS, cols), 0)
    er, ei, _, _ = _log_scan(wide(yr_ref, end), wide(yi_ref, end), m8r, m8i, groups, SCAN_GROUPS,
                             int(math.log2(SCAN_GROUPS)), reverse)
    er_ref[...] = er
    ei_ref[...] = ei
    j = lax.broadcasted_iota(jnp.int32, (SCAN_ROWS, cols), 0)
    dist = SCAN_ROWS - j if reverse else j + 1
    tr, ti = jnp.ones((SCAN_ROWS, cols), F32), jnp.zeros((SCAN_ROWS, cols), F32)
    br, bi = mr, mi
    for bit in range(4):
        nr, ni = _cmul(tr, ti, br, bi)
        take = ((dist >> bit) & 1) == 1
        tr, ti = jnp.where(take, nr, tr), jnp.where(take, ni, ti)
        br, bi = _cmul(br, bi, br, bi)
    for g in range(SCAN_GROUPS):
        before = g + 1 if reverse else g - 1
        rows_g = slice(g * SCAN_ROWS, (g + 1) * SCAN_ROWS)
        if 0 <= before < SCAN_GROUPS:
            ar, ai = _cmul(tr, ti, er_ref[before:before + 1, :], ei_ref[before:before + 1, :])
            hr_ref[rows_g, :] = wide(yr_ref, rows_g) + ar
            hi_ref[rows_g, :] = wide(yi_ref, rows_g) + ai
        else:
            hr_ref[rows_g, :] = wide(yr_ref, rows_g)
            hi_ref[rows_g, :] = wide(yi_ref, rows_g)
    last = 0 if reverse else SCAN_GROUPS - 1
    return er_ref[last:last + 1, :], ei_ref[last:last + 1, :]


def s5_fwd(proj, b_re, b_im, a_re, a_im, c_re, c_im, d_skip):
    S = proj.shape[0]
    nt = S // SSM_T

    def body(u_ref, bre_ref, bim_ref, ar_ref, ai_ref, cre_ref, cim_ref, dsk_ref, hr_ref, hi_ref, y_ref, cr_ref, ci_ref,
             yr_ref, yi_ref, er_ref, ei_ref):
        t = pl.program_id(1)

        @pl.when(t == 0)
        def _():
            cr_ref[...] = jnp.zeros(cr_ref.shape, F32)
            ci_ref[...] = jnp.zeros(ci_ref.shape, F32)

        u = u_ref[...]
        ar, ai = ar_ref[...], ai_ref[...]
        rows = lax.broadcasted_iota(jnp.int32, (SSM_T, W_BRANCH), 0)
        inr, ini = _cmul(ar, ai, cr_ref[0:1, :], ci_ref[0:1, :])
        xr = _dot(u, bre_ref[...]) + jnp.where(rows == 0, inr, 0.0)
        xi = _dot(u, bim_ref[...]) + jnp.where(rows == 0, ini, 0.0)
        endr, endi = _block_scan(xr, xi, ar, ai, False, yr_ref, yi_ref, er_ref, ei_ref, hr_ref, hi_ref)
        cr_ref[...] = jnp.broadcast_to(endr, cr_ref.shape)
        ci_ref[...] = jnp.broadcast_to(endi, ci_ref.shape)
        xr, xi = hr_ref[...], hi_ref[...]
        y_ref[...] = _dot(xr, cre_ref[...]) - _dot(xi, cim_ref[...]) + u * dsk_ref[...]

    u_spec = pl.BlockSpec((SSM_T, 128), lambda j, t: (t, OFF_SSM // 128 + j))
    b_spec = pl.BlockSpec((None, 128, W_BRANCH), lambda j, t: (j, 0, 0))
    a_spec = pl.BlockSpec((1, W_BRANCH), lambda j, t: (0, j))
    c_spec = pl.BlockSpec((None, W_BRANCH, 128), lambda j, t: (j, 0, 0))
    h_spec = pl.BlockSpec((SSM_T, W_BRANCH), lambda j, t: (t, j))
    return pl.pallas_call(
        body, name="s5_fwd", grid=(4, nt),
        in_specs=[u_spec, b_spec, b_spec, a_spec, a_spec, c_spec, c_spec, pl.BlockSpec((1, 128), lambda j, t: (0, j))],
        out_specs=[h_spec, h_spec, pl.BlockSpec((SSM_T, 128), lambda j, t: (t, j))],
        out_shape=[jax.ShapeDtypeStruct((S, SSM_COLS), F32), jax.ShapeDtypeStruct((S, SSM_COLS), F32),
                   jax.ShapeDtypeStruct((S, W_BRANCH), F32)],
        scratch_shapes=[pltpu.VMEM((8, W_BRANCH), F32)] * 2 + _scan_scratch(0),
        compiler_params=_cp(2),
    )(proj, b_re, b_im, a_re, a_im, c_re, c_im, d_skip)


def s5_bwd(proj, hr, hi, dy, b_re, b_im, a_re, a_im, c_re, c_im, d_skip):
    S = proj.shape[0]
    nt = S // SSM_T

    def body(u_ref, hr_ref, hi_ref, hpr_ref, hpi_ref, dy_ref, bre_ref, bim_ref, ar_ref, ai_ref, cre_ref, cim_ref,
             dsk_ref, du_ref, dbre_ref, dbim_ref, dar_ref, dai_ref, dcre_ref, dcim_ref, ddsk_ref, gr_ref, gi_ref,
             yr_ref, yi_ref, er_ref, ei_ref, sr_ref, si_ref):
        step = pl.program_id(1)
        t = nt - 1 - step

        @pl.when(step == 0)
        def _():
            gr_ref[...] = jnp.zeros(gr_ref.shape, F32)
            gi_ref[...] = jnp.zeros(gi_ref.shape, F32)
            for ref in (dbre_ref, dbim_ref, dar_ref, dai_ref, dcre_ref, dcim_ref, ddsk_ref):
                ref[...] = jnp.zeros(ref.shape, F32)

        u = u_ref[...]
        dy = dy_ref[...]
        ar, ai = ar_ref[...], ai_ref[...]
        rows = lax.broadcasted_iota(jnp.int32, (SSM_T, W_BRANCH), 0)
        inr, ini = _cmul(ar, -ai, gr_ref[0:1, :], gi_ref[0:1, :])
        xr = _dot(dy, cre_ref[...], "nt") + jnp.where(rows == SSM_T - 1, inr, 0.0)
        xi = -_dot(dy, cim_ref[...], "nt") + jnp.where(rows == SSM_T - 1, ini, 0.0)
        endr, endi = _block_scan(xr, xi, ar, -ai, True, yr_ref, yi_ref, er_ref, ei_ref, sr_ref, si_ref)
        gr_ref[...] = jnp.broadcast_to(endr, gr_ref.shape)
        gi_ref[...] = jnp.broadcast_to(endi, gi_ref.shape)
        xr, xi = sr_ref[...], si_ref[...]
        hr_blk, hi_blk = hr_ref[...], hi_ref[...]
        keep = (t > 0).astype(F32)
        hpr = jnp.where(rows >= 1, pltpu.roll(hr_blk, 1, 0), hpr_ref[7:8, :] * keep)
        hpi = jnp.where(rows >= 1, pltpu.roll(hi_blk, 1, 0), hpi_ref[7:8, :] * keep)
        dar_ref[...] += jnp.sum(hpr * xr + hpi * xi, axis=0, keepdims=True)
        dai_ref[...] += jnp.sum(hpr * xi - hpi * xr, axis=0, keepdims=True)
        dcre_ref[...] += _dot(hr_blk, dy, "tn")
        dcim_ref[...] -= _dot(hi_blk, dy, "tn")
        du = dy * dsk_ref[...] + _dot(xr, bre_ref[...], "nt") + _dot(xi, bim_ref[...], "nt")
        du_ref[...] = du.astype(du_ref.dtype)
        dbre_ref[...] += _dot(u, xr, "tn")
        dbim_ref[...] += _dot(u, xi, "tn")
        ddsk_ref[...] += jnp.sum(dy * u, axis=0, keepdims=True)

    def rev(t):
        return nt - 1 - t

    u_spec = pl.BlockSpec((SSM_T, 128), lambda j, t: (rev(t), OFF_SSM // 128 + j))
    h_spec = pl.BlockSpec((SSM_T, W_BRANCH), lambda j, t: (rev(t), j))
    hprev_spec = pl.BlockSpec((8, W_BRANCH), lambda j, t: (jnp.maximum(rev(t) * (SSM_T // 8) - 1, 0), j))
    ch_spec = pl.BlockSpec((SSM_T, 128), lambda j, t: (rev(t), j))
    b_spec = pl.BlockSpec((None, 128, W_BRANCH), lambda j, t: (j, 0, 0))
    a_spec = pl.BlockSpec((1, W_BRANCH), lambda j, t: (0, j))
    c_spec = pl.BlockSpec((None, W_BRANCH, 128), lambda j, t: (j, 0, 0))
    d_spec = pl.BlockSpec((1, 128), lambda j, t: (0, j))
    return pl.pallas_call(
        body, name="s5_bwd", grid=(4, nt),
        in_specs=[u_spec, h_spec, h_spec, hprev_spec, hprev_spec, ch_spec, b_spec, b_spec, a_spec, a_spec,
                  c_spec, c_spec, d_spec],
        out_specs=[ch_spec, b_spec, b_spec, a_spec, a_spec, c_spec, c_spec, d_spec],
        out_shape=[jax.ShapeDtypeStruct((S, W_BRANCH), BF16),
                   jax.ShapeDtypeStruct((4, 128, W_BRANCH), F32), jax.ShapeDtypeStruct((4, 128, W_BRANCH), F32),
                   jax.ShapeDtypeStruct((1, SSM_COLS), F32), jax.ShapeDtypeStruct((1, SSM_COLS), F32),
                   jax.ShapeDtypeStruct((4, W_BRANCH, 128), F32), jax.ShapeDtypeStruct((4, W_BRANCH, 128), F32),
                   jax.ShapeDtypeStruct((1, W_BRANCH), F32)],
        scratch_shapes=[pltpu.VMEM((8, W_BRANCH), F32)] * 2 + _scan_scratch(2),
        compiler_params=_cp(2),
    )(proj, hr, hi, hr, hi, dy, b_re, b_im, a_re, a_im, c_re, c_im, d_skip)


def glu_fwd(y, w_glu, b_glu):
    S = y.shape[0]

    def body(y_ref, w_ref, b_ref, o_ref):
        g = _gelu(y_ref[...])
        o_ref[...] = (g * _sigmoid(_dot(g, w_ref[...]) + b_ref[...])).astype(o_ref.dtype)

    blk = pl.BlockSpec((ROW_TILE, W_BRANCH), lambda i: (i, 0))
    return pl.pallas_call(
        body, name="glu_fwd", grid=(S // ROW_TILE,),
        in_specs=[blk, pl.BlockSpec((W_BRANCH, W_BRANCH), lambda i: (0, 0)), pl.BlockSpec((1, W_BRANCH), lambda i: (0, 0))],
        out_specs=blk, out_shape=jax.ShapeDtypeStruct((S, W_BRANCH), BF16), compiler_params=_cp(1),
    )(y, w_glu, b_glu)


def glu_bwd(y, w_glu, b_glu, dout):
    S = y.shape[0]

    def body(y_ref, w_ref, b_ref, do_ref, dy_ref, dw_ref, db_ref):
        yv = y_ref[...]
        do = do_ref[...]
        g = _gelu(yv)
        s = _sigmoid(_dot(g, w_ref[...]) + b_ref[...])
        dz = do * g * s * (1.0 - s)
        dg = do * s + _dot(dz, w_ref[...], "nt")
        dy_ref[...] = dg * _gelu_grad(yv)
        dw = _dot(g, dz, "tn")
        db = jnp.sum(dz, axis=0, keepdims=True)

        @pl.when(pl.program_id(0) == 0)
        def _():
            dw_ref[...] = dw
            db_ref[...] = db

        @pl.when(pl.program_id(0) > 0)
        def _():
            dw_ref[...] += dw
            db_ref[...] += db

    blk = pl.BlockSpec((ROW_TILE, W_BRANCH), lambda i: (i, 0))
    mat = pl.BlockSpec((W_BRANCH, W_BRANCH), lambda i: (0, 0))
    vec = pl.BlockSpec((1, W_BRANCH), lambda i: (0, 0))
    return pl.pallas_call(
        body, name="glu_bwd", grid=(S // ROW_TILE,), in_specs=[blk, mat, vec, blk], out_specs=[blk, mat, vec],
        out_shape=[jax.ShapeDtypeStruct((S, W_BRANCH), F32), jax.ShapeDtypeStruct((W_BRANCH, W_BRANCH), F32),
                   jax.ShapeDtypeStruct((1, W_BRANCH), F32)],
        compiler_params=_cp(1),
    )(y, w_glu, b_glu, dout)


SGU_TILE = 512
SGU_U_BLOCK = OFF_SGU // W_BRANCH
SGU_V_BLOCK = SGU_U_BLOCK + 1


def _sgu_norm(zv):
    v = _gelu(zv)
    mu = jnp.mean(v, axis=-1, keepdims=True)
    vc = v - mu
    rstd = lax.rsqrt(jnp.mean(vc * vc, axis=-1, keepdims=True) + EPS)
    return vc * rstd, rstd


def _tril():
    return lax.broadcasted_iota(jnp.int32, (SGU_CHUNK, SGU_CHUNK), 0) >= lax.broadcasted_iota(jnp.int32, (SGU_CHUNK, SGU_CHUNK), 1)


def sgu_fwd(proj, ln_g, ln_b, w_s, b_s_t):
    S = proj.shape[0]

    def body(zu_ref, zv_ref, g_ref, b_ref, ws_ref, bs_ref, o_ref, vf_ref):
        vn, _ = _sgu_norm(zv_ref[...])
        vf_ref[...] = vn * g_ref[...] + b_ref[...]
        tri = _tril()
        for gi in range(4):
            ws = jnp.where(tri, ws_ref[gi], 0.0)
            cols = slice(gi * 128, (gi + 1) * 128)
            for c in range(SGU_TILE // SGU_CHUNK):
                rows = slice(c * SGU_CHUNK, (c + 1) * SGU_CHUNK)
                sv = _dot(ws, vf_ref[rows, cols]) + bs_ref[:, gi:gi + 1]
                o_ref[rows, cols] = (_gelu(zu_ref[rows, cols]) * sv).astype(o_ref.dtype)

    blk = lambda cb: pl.BlockSpec((SGU_TILE, W_BRANCH), lambda i: (i, cb))
    vec = pl.BlockSpec((1, W_BRANCH), lambda i: (0, 0))
    return pl.pallas_call(
        body, name="sgu_fwd", grid=(S // SGU_TILE,),
        in_specs=[blk(SGU_U_BLOCK), blk(SGU_V_BLOCK), vec, vec, pl.BlockSpec((4, SGU_CHUNK, SGU_CHUNK), lambda i: (0, 0, 0)),
                  pl.BlockSpec((SGU_CHUNK, 4), lambda i: (0, 0))],
        out_specs=blk(0), out_shape=jax.ShapeDtypeStruct((S, W_BRANCH), BF16),
        scratch_shapes=[pltpu.VMEM((SGU_TILE, W_BRANCH), F32)], compiler_params=_cp(1),
    )(proj, proj, ln_g, ln_b, w_s, b_s_t)


def sgu_bwd(proj, ln_g, ln_b, w_s, b_s_t, dout):
    S = proj.shape[0]

    def body(zu_ref, zv_ref, g_ref, b_ref, ws_ref, bs_ref, do_ref, dzu_ref, dzv_ref, dg_ref, db_ref, dws_ref, dbs_ref,
             vf_ref, dvf_ref):
        @pl.when(pl.program_id(0) == 0)
        def _():
            for ref in (dg_ref, db_ref, dws_ref, dbs_ref):
                ref[...] = jnp.zeros(ref.shape, F32)

        vn, rstd = _sgu_norm(zv_ref[...])
        vf_ref[...] = vn * g_ref[...] + b_ref[...]
        tri = _tril()
        lane = lax.broadcasted_iota(jnp.int32, (SGU_CHUNK, 128), 1)
        dbs = jnp.zeros((SGU_CHUNK, 128), F32)
        for gi in range(4):
            ws = jnp.where(tri, ws_ref[gi], 0.0)
            cols = slice(gi * 128, (gi + 1) * 128)
            dws = jnp.zeros((SGU_CHUNK, SGU_CHUNK), F32)
            for c in range(SGU_TILE // SGU_CHUNK):
                rows = slice(c * SGU_CHUNK, (c + 1) * SGU_CHUNK)
                vf = vf_ref[rows, cols]
                zu = zu_ref[rows, cols]
                do = do_ref[rows, cols]
                sv = _dot(ws, vf) + bs_ref[:, gi:gi + 1]
                dzu_ref[rows, cols] = (do * sv * _gelu_grad(zu)).astype(dzu_ref.dtype)
                dsv = do * _gelu(zu)
                dvf_ref[rows, cols] = _dot(ws, dsv, "tn")
                dws = dws + _dot(dsv, vf, "nt")
                dbs = dbs + jnp.where(lane == gi, jnp.sum(dsv, axis=-1, keepdims=True), 0.0)
            dws_ref[gi] += jnp.where(tri, dws, 0.0)
        dbs_ref[...] += dbs
        dvf = dvf_ref[...]
        dg_ref[...] += jnp.sum(dvf * vn, axis=0, keepdims=True)
        db_ref[...] += jnp.sum(dvf, axis=0, keepdims=True)
        dvn = dvf * g_ref[...]
        dv = rstd * (dvn - jnp.mean(dvn, axis=-1, keepdims=True) - vn * jnp.mean(dvn * vn, axis=-1, keepdims=True))
        dzv_ref[...] = (dv * _gelu_grad(zv_ref[...])).astype(dzv_ref.dtype)

    blk = lambda cb: pl.BlockSpec((SGU_TILE, W_BRANCH), lambda i: (i, cb))
    vec = pl.BlockSpec((1, W_BRANCH), lambda i: (0, 0))
    ws_spec = pl.BlockSpec((4, SGU_CHUNK, SGU_CHUNK), lambda i: (0, 0, 0))
    return pl.pallas_call(
        body, name="sgu_bwd", grid=(S // SGU_TILE,),
        in_specs=[blk(SGU_U_BLOCK), blk(SGU_V_BLOCK), vec, vec, ws_spec, pl.BlockSpec((SGU_CHUNK, 4), lambda i: (0, 0)),
                  blk(0)],
        out_specs=[blk(0), blk(0), vec, vec, ws_spec, pl.BlockSpec((SGU_CHUNK, 128), lambda i: (0, 0))],
        out_shape=[jax.ShapeDtypeStruct((S, W_BRANCH), BF16), jax.ShapeDtypeStruct((S, W_BRANCH), BF16),
                   jax.ShapeDtypeStruct((1, W_BRANCH), F32), jax.ShapeDtypeStruct((1, W_BRANCH), F32),
                   jax.ShapeDtypeStruct((4, SGU_CHUNK, SGU_CHUNK), F32), jax.ShapeDtypeStruct((SGU_CHUNK, 128), F32)],
        scratch_shapes=[pltpu.VMEM((SGU_TILE, W_BRANCH), F32), pltpu.VMEM((SGU_TILE, W_BRANCH), F32)],
        compiler_params=_cp(1),
    )(proj, proj, ln_g, ln_b, w_s, b_s_t, dout)


GM_TILE = 512


def _gate_specs(order):
    def spec(i):
        def index(*ids):
            m, n = order(*ids)
            return (m, (OFF_GATE + i * D_MODEL) // GM_TILE + n)
        return pl.BlockSpec((GM_TILE, GM_TILE), index)
    return [spec(i) for i in range(4)]


def merge_fwd(proj, gate_b, branches, w_up):
    S = proj.shape[0]
    order = lambda n, m: (m, n)

    def body(p0, p1, p2, p3, gb_ref, b0, b1, b2, b3, w_ref, o_ref):
        acc = jnp.zeros((GM_TILE, GM_TILE), F32)
        for i, (p_ref, br_ref) in enumerate(zip((p0, p1, p2, p3), (b0, b1, b2, b3))):
            acc = acc + _sigmoid(p_ref[...] + gb_ref[i:i + 1, :]) * _dot(br_ref[...], w_ref[i])
        o_ref[...] = acc.astype(o_ref.dtype)

    br_spec = pl.BlockSpec((GM_TILE, W_BRANCH), lambda n, m: (m, 0))
    return pl.pallas_call(
        body, name="merge_fwd", grid=(D_MODEL // GM_TILE, S // GM_TILE),
        in_specs=_gate_specs(order) + [pl.BlockSpec((4, GM_TILE), lambda n, m: (0, n))] + [br_spec] * 4
        + [pl.BlockSpec((4, W_BRANCH, GM_TILE), lambda n, m: (0, 0, n))],
        out_specs=pl.BlockSpec((GM_TILE, GM_TILE), lambda n, m: (m, n)),
        out_shape=jax.ShapeDtypeStruct((S, D_MODEL), BF16), compiler_params=_cp(2),
    )(proj, proj, proj, proj, gate_b, *branches, w_up)


def merge_bwd(proj, gate_b, branches, w_up, dmerged):
    S = proj.shape[0]
    order = lambda n, m: (m, n)

    def body(p0, p1, p2, p3, gb_ref, b0, b1, b2, b3, w_ref, dm_ref, dp0, dp1, dp2, dp3, du0, du1, du2, du3, dgb_ref):
        dm = dm_ref[...]
        dgb = []
        for i, (p_ref, br_ref, dp_ref, du_ref) in enumerate(
                zip((p0, p1, p2, p3), (b0, b1, b2, b3), (dp0, dp1, dp2, dp3), (du0, du1, du2, du3))):
            gate = _sigmoid(p_ref[...] + gb_ref[i:i + 1, :])
            dpre = dm * _dot(br_ref[...], w_ref[i]) * gate * (1.0 - gate)
            dp_ref[...] = dpre.astype(dp_ref.dtype)
            du_ref[...] = (dm * gate).astype(du_ref.dtype)
            dgb.append(jnp.sum(dpre, axis=0, keepdims=True))
        dgb = jnp.concatenate(dgb, axis=0)

        @pl.when(pl.program_id(1) == 0)
        def _():
            dgb_ref[...] = dgb

        @pl.when(pl.program_id(1) > 0)
        def _():
            dgb_ref[...] += dgb

    br_spec = pl.BlockSpec((GM_TILE, W_BRANCH), lambda n, m: (m, 0))
    mn = pl.BlockSpec((GM_TILE, GM_TILE), lambda n, m: (m, n))
    gb = pl.BlockSpec((4, GM_TILE), lambda n, m: (0, n))
    big = jax.ShapeDtypeStruct((S, D_MODEL), BF16)
    outs = pl.pallas_call(
        body, name="merge_bwd", grid=(D_MODEL // GM_TILE, S // GM_TILE),
        in_specs=_gate_specs(order) + [gb] + [br_spec] * 4
        + [pl.BlockSpec((4, W_BRANCH, GM_TILE), lambda n, m: (0, 0, n)), mn],
        out_specs=[mn] * 8 + [gb], out_shape=[big] * 8 + [jax.ShapeDtypeStruct((4, D_MODEL), F32)],
        compiler_params=_cp(2),
    )(proj, proj, proj, proj, gate_b, *branches, w_up, dmerged)
    return outs[0:4], outs[4:8], outs[8]


def _xatt_probs(q, k):
    s = _dot(q, k, "nt") * (X_HEAD_DIM ** -0.5)
    p = jnp.exp(s - jnp.max(s, axis=-1, keepdims=True))
    return p / jnp.sum(p, axis=-1, keepdims=True)


def xatt_fwd(q, kv):
    S = q.shape[0]

    def body(q_ref, kv_ref, o_ref):
        for h in range(X_HEADS):
            cols = slice(h * X_HEAD_DIM, (h + 1) * X_HEAD_DIM)
            p = _xatt_probs(q_ref[:, cols], kv_ref[:, cols])
            o_ref[:, cols] = _dot(p, kv_ref[:, W_BRANCH + h * X_HEAD_DIM:W_BRANCH + (h + 1) * X_HEAD_DIM]).astype(o_ref.dtype)

    blk = pl.BlockSpec((ROW_TILE, W_BRANCH), lambda i: (i, 0))
    return pl.pallas_call(
        body, name="xatt_fwd", grid=(S // ROW_TILE,),
        in_specs=[blk, pl.BlockSpec((N_MEM, 2 * W_BRANCH), lambda i: (0, 0))], out_specs=blk,
        out_shape=jax.ShapeDtypeStruct((S, W_BRANCH), BF16), compiler_params=_cp(1),
    )(q, kv)


def xatt_bwd(q, kv, do):
    S = q.shape[0]

    def body(q_ref, kv_ref, do_ref, dq_ref, dkv_ref):
        @pl.when(pl.program_id(0) == 0)
        def _():
            dkv_ref[...] = jnp.zeros(dkv_ref.shape, F32)

        for h in range(X_HEADS):
            cols = slice(h * X_HEAD_DIM, (h + 1) * X_HEAD_DIM)
            vcols = slice(W_BRANCH + h * X_HEAD_DIM, W_BRANCH + (h + 1) * X_HEAD_DIM)
            qh, kh, doh = q_ref[:, cols], kv_ref[:, cols], do_ref[:, cols]
            p = _xatt_probs(qh, kh)
            dp = _dot(doh, kv_ref[:, vcols], "nt")
            ds = p * (dp - jnp.sum(dp * p, axis=-1, keepdims=True)) * (X_HEAD_DIM ** -0.5)
            dq_ref[:, cols] = _dot(ds, kh).astype(dq_ref.dtype)
            dkv_ref[:, cols] += _dot(ds, qh, "tn")
            dkv_ref[:, vcols] += _dot(p, doh, "tn")

    blk = pl.BlockSpec((ROW_TILE, W_BRANCH), lambda i: (i, 0))
    kv_spec = pl.BlockSpec((N_MEM, 2 * W_BRANCH), lambda i: (0, 0))
    return pl.pallas_call(
        body, name="xatt_bwd", grid=(S // ROW_TILE,), in_specs=[blk, kv_spec, blk], out_specs=[blk, kv_spec],
        out_shape=[jax.ShapeDtypeStruct((S, W_BRANCH), BF16), jax.ShapeDtypeStruct((N_MEM, 2 * W_BRANCH), F32)],
        compiler_params=_cp(1),
    )(q, kv, do)


def s5_params(a_re, a_im, log_dt, b_re, b_im, c_re, c_im):
    lam_re = jnp.minimum(a_re, -1e-4)
    lam_im = a_im
    dt = jnp.exp(log_dt)[:, None]
    mag = jnp.exp(lam_re * dt)
    ab_re, ab_im = mag * jnp.cos(lam_im * dt), mag * jnp.sin(lam_im * dt)
    den = lam_re * lam_re + lam_im * lam_im
    f_re = ((ab_re - 1.0) * lam_re + ab_im * lam_im) / den
    f_im = (ab_im * lam_re - (ab_re - 1.0) * lam_im) / den
    bb_re = f_re[..., None] * b_re - f_im[..., None] * b_im
    bb_im = f_re[..., None] * b_im + f_im[..., None] * b_re
    eye = jnp.eye(8, dtype=F32)

    def b_blocks(bb):
        t = bb.reshape(4, 8, SSM_STATE, SSM_GROUP).transpose(0, 1, 3, 2)
        return (t[:, :, :, None, :] * eye[None, :, None, :, None]).reshape(4, 128, W_BRANCH)

    def c_blocks(cc):
        t = cc.reshape(4, 8, SSM_GROUP, SSM_STATE).transpose(0, 1, 3, 2)
        return (t[:, :, :, None, :] * eye[None, :, None, :, None]).reshape(4, W_BRANCH, 128)

    return (ab_re.reshape(1, SSM_COLS), ab_im.reshape(1, SSM_COLS), b_blocks(bb_re), b_blocks(bb_im),
            c_blocks(c_re), c_blocks(c_im))


ANY = pl.BlockSpec(memory_space=pl.ANY)


def _chip_index():
    return 2 * lax.axis_index("x") + lax.axis_index("y")


def _peer_chip(j):
    x, y, c = lax.axis_index("x"), lax.axis_index("y"), lax.axis_index("c")
    return ((1 - x) if j & 2 else x, (1 - y) if j & 1 else y, c)


def _piece(ref, axis, s, n):
    size = ref.shape[axis] // n
    idx = [slice(None)] * len(ref.shape)
    idx[axis] = pl.ds(s * size, size)
    return ref.at[tuple(idx)]


HBM_SPEC = pl.BlockSpec(memory_space=pltpu.HBM)
SEM_SPEC = pl.BlockSpec(memory_space=pltpu.SEMAPHORE)
SIDE_EFFECT = pltpu.SideEffectType.DATAFLOW_SIDE_EFFECTING


def _chip_copies(ins, lands, send, recv, axes, mode, k, arriving):
    copies = []
    for t in range(len(ins)):
        for j in (1, 2, 3):
            place = k ^ j if arriving else k
            if mode == "gather":
                src, dst = ins[t], _piece(lands[t], axes[t], place, 4)
            else:
                src = ins[t] if axes[t] is None else _piece(ins[t], axes[t], k ^ j, 4)
                dst = lands[t].at[place]
            copies.append(pltpu.make_async_remote_copy(
                src_ref=src, dst_ref=dst, send_sem=send.at[3 * t + j - 1], recv_sem=recv.at[3 * t + j - 1],
                device_id=_peer_chip(j), device_id_type=MESH_ID))
    return copies


def _own_copies(ins, lands, send, axes, mode, k):
    if mode != "gather":
        return []
    n = len(ins)
    return [pltpu.make_async_copy(ins[t], _piece(lands[t], axes[t], k, 4), send.at[3 * n + t]) for t in range(n)]


def chips_start(ins, lands, axes, mode, name, after=()):
    n, na = len(ins), len(after)

    def body(*refs):
        in_refs, land_refs = refs[:n], refs[n:2 * n]
        send, recv, token = refs[2 * n + na], refs[2 * n + na + 1], refs[-1]
        q = _chip_index()
        for k in range(4):
            @pl.when(q == k)
            def _():
                for copy in _chip_copies(in_refs, land_refs, send, recv, axes, mode, k, arriving=False):
                    copy.start()
                for copy in _own_copies(in_refs, land_refs, send, axes, mode, k):
                    copy.start()
        token[...] = jnp.zeros(token.shape, token.dtype)

    hbm = lambda a: pltpu.HBM(a.shape, a.dtype)
    outs = pl.pallas_call(
        body, name=name, in_specs=[HBM_SPEC] * (2 * n) + [ANY] * na,
        out_specs=[SEM_SPEC, SEM_SPEC] + [HBM_SPEC] * (2 * n) + [pl.BlockSpec(memory_space=pltpu.VMEM)],
        out_shape=[pltpu.SemaphoreType.DMA((4 * n,)), pltpu.SemaphoreType.DMA((3 * n,))]
        + [hbm(a) for a in ins] + [hbm(a) for a in lands] + [jax.ShapeDtypeStruct((8, 128), F32)],
        input_output_aliases={i: 2 + i for i in range(2 * n)},
        compiler_params=pltpu.CompilerParams(has_side_effects=SIDE_EFFECT),
    )(*[pltpu.with_memory_space_constraint(a, pltpu.HBM) for a in list(ins) + list(lands)], *after)
    return outs[0], outs[1], outs[2:2 + n], outs[2 + n:2 + 2 * n], outs[-1]


def chips_wait(send, recv, ins, lands, axes, mode, name, after=()):
    n = len(ins)

    def body(*refs):
        in_refs, land_refs = refs[:n], refs[n:2 * n]
        send_ref, recv_ref = refs[2 * n], refs[2 * n + 1]
        q = _chip_index()
        for k in range(4):
            @pl.when(q == k)
            def _():
                for copy in _chip_copies(in_refs, land_refs, send_ref, recv_ref, axes, mode, k, arriving=True):
                    copy.wait_send()
                    copy.wait_recv()
                for copy in _own_copies(in_refs, land_refs, send_ref, axes, mode, k):
                    copy.wait()

    hbm = lambda a: pltpu.HBM(a.shape, a.dtype)
    outs = pl.pallas_call(
        body, name=name, in_specs=[HBM_SPEC] * (2 * n) + [SEM_SPEC, SEM_SPEC] + [ANY] * len(after),
        out_specs=[HBM_SPEC] * (2 * n), out_shape=[hbm(a) for a in ins] + [hbm(a) for a in lands],
        input_output_aliases={i: i for i in range(2 * n)},
        compiler_params=pltpu.CompilerParams(has_side_effects=SIDE_EFFECT),
    )(*ins, *lands, send, recv, *after)
    return outs[:n], outs[n:]


def swap_cores(arrs, name):
    n = len(arrs)

    def body(*refs):
        ins, outs = refs[:n], refs[n:2 * n]
        send, recv = refs[2 * n:]
        sibling = (lax.axis_index("x"), lax.axis_index("y"), 1 - lax.axis_index("c"))
        copies = [pltpu.make_async_remote_copy(src_ref=ins[t], dst_ref=outs[t], send_sem=send.at[t], recv_sem=recv.at[t],
                                               device_id=sibling, device_id_type=MESH_ID) for t in range(n)]
        for cp in copies:
            cp.start()
        for cp in copies:
            cp.wait()

    return pl.pallas_call(
        body, name=name, in_specs=[ANY] * n, out_specs=[ANY] * n,
        out_shape=[jax.ShapeDtypeStruct(a.shape, a.dtype) for a in arrs],
        scratch_shapes=[pltpu.SemaphoreType.DMA((n,)), pltpu.SemaphoreType.DMA((n,))],
    )(*arrs)


ELEMENTWISE_BLOCK_BYTES = 1 << 20


def _row_tile(rows, cols):
    want = max(8, ELEMENTWISE_BLOCK_BYTES // (4 * 128 * -(-cols // 128)))
    fits = [t for t in range(8, min(rows, want) + 1, 8) if rows % t == 0]
    return fits[-1] if fits else rows


def sum_chips(recv, own, axis, chip, stacked, l, name):
    _, r, c = recv.shape
    tr = _row_tile(r, c)
    nrt = r // tr

    def body(chip_ref, r_ref, own_ref, stacked_ref, o_ref):
        for k in range(4):
            @pl.when(chip_ref[0] == k)
            def _():
                terms = [own_ref[...] if s == k else r_ref[s] for s in range(4)]
                o_ref[...] = ((terms[0] + terms[1]) + terms[2]) + terms[3]

    own_index = {0: lambda i, q: (q[0] * nrt + i, 0), 1: lambda i, q: (i, q[0]), None: lambda i, q: (i, 0)}[axis]
    return pl.pallas_call(
        body, name=name,
        grid_spec=pltpu.PrefetchScalarGridSpec(
            num_scalar_prefetch=1, grid=(nrt,),
            in_specs=[pl.BlockSpec((4, tr, c), lambda i, q: (0, i, 0)), pl.BlockSpec((tr, c), own_index), ANY],
            out_specs=pl.BlockSpec((None, tr, c), lambda i, q: (l, i, 0))),
        out_shape=jax.ShapeDtypeStruct(stacked.shape, F32), input_output_aliases={3: 0}, compiler_params=_cp(1),
    )(chip, recv, own, stacked)


def adamw(w, ga, gb, m, v, name):
    rows, cols = w.shape
    tr = _row_tile(rows, cols)

    def body(w_ref, ga_ref, gb_ref, m_ref, v_ref, g_ref, d_ref, nm_ref, nv_ref):
        g = ga_ref[...] + gb_ref[...]
        nm = ADAM_B1 * m_ref[...] + (1.0 - ADAM_B1) * g
        nv = ADAM_B2 * v_ref[...] + (1.0 - ADAM_B2) * (g * g)
        m_hat = nm / (1.0 - ADAM_B1 ** ADAM_STEP)
        v_hat = nv / (1.0 - ADAM_B2 ** ADAM_STEP)
        g_ref[...] = g
        nm_ref[...] = nm
        nv_ref[...] = nv
        d_ref[...] = -ADAM_LR * (m_hat / (jnp.sqrt(v_hat) + ADAM_EPS) + ADAM_WD * w_ref[...])

    blk = pl.BlockSpec((tr, cols), lambda i: (i, 0))
    f = jax.ShapeDtypeStruct((rows, cols), F32)
    return pl.pallas_call(
        body, name=name, grid=(rows // tr,), in_specs=[blk] * 5, out_specs=[blk] * 4, out_shape=[f] * 4,
        compiler_params=_cp(1),
    )(w, ga, gb, m, v)


PACK_ALIGN = 1024
PACK_ROWS_ALIGN = 2048


def pack_small(arrs):
    parts = []
    for a in arrs:
        flat = a.reshape(-1)
        pad = (-flat.shape[0]) % PACK_ALIGN
        parts.append(jnp.pad(flat, (0, pad)) if pad else flat)
    rows = sum(p.shape[0] for p in parts) // 128
    parts.append(jnp.zeros(((-rows) % PACK_ROWS_ALIGN * 128,), arrs[0].dtype))
    return jnp.concatenate(parts).reshape(-1, 128)


def unpack_small(packed, shapes):
    out, row = [], 0
    for shape in shapes:
        size = int(np.prod(shape))
        rows = -(-size // PACK_ALIGN) * 8
        out.append(packed[row:row + rows].reshape(-1)[:size].reshape(shape))
        row += rows
    return out


def _norm_epilogue(with_next):
    def epi(acc, res, g_post, *g_pre):
        x_new = acc * lax.rsqrt(jnp.mean(acc * acc, axis=-1, keepdims=True) + EPS) * g_post + res
        if not with_next:
            return acc, x_new
        return acc, x_new, x_new * lax.rsqrt(jnp.mean(x_new * x_new, axis=-1, keepdims=True) + EPS) * g_pre[0]
    return epi


def layer_fwd(x, h1, mem, w_in, rest_of, P, biases, g_next, after=()):
    sv = {"x0": x}
    post = dict(tm=512, tn=D_MODEL)
    proj = mm(h1, w_in, "nn", out_dtypes=[F32], name="mm_w_in", after=after)
    a_out = pool_fwd(proj, P["pool_w"], P["pool_scale"])
    os_, lses = [], []
    for g, (win, dil) in enumerate(DIL_GROUPS):
        o, lse = att_fwd(proj, biases[g], g, dil)
        os_.append(o)
        lses.append(lse)
    b_out, w0, w1, w2 = att_combine(os_, lses)
    s5p = P["s5"]
    hr, hi, y = s5_fwd(proj, s5p[2], s5p[3], s5p[0], s5p[1], s5p[4], s5p[5], P["d_skip"])
    d_out = sgu_fwd(proj, P["sgu_ln_g"], P["sgu_ln_b"], P["w_s"], P["b_s_t"])
    W, after_rest = rest_of("mixer", d_out)
    W = dict(W, w_in=w_in)
    c_out = glu_fwd(y, W["w_glu"], P["b_glu"])
    branches = (a_out, b_out, c_out, d_out)
    merged = merge_fwd(proj, W["gate_b"], branches, W["w_up"])
    t1, x1, h2 = mm(merged, W["w_out"], "nn", tk=1024, out_dtypes=[F32, F32, BF16], name="mm_w_out", extras=(x,),
                    vecs=(P["g_mix_post"], P["g_x_pre"]), epi=_norm_epilogue(True), after=after_rest, **post)
    sv.update(h1=h1, proj=proj, os=os_, lses=lses, wts=(w0, w1, w2), hr=hr, hi=hi, y=y, branches=branches,
              merged=merged, t1=t1, x1=x1)

    mem_n = rms_fwd(mem, P["g_mem"], BF16, "rms_mem")
    q = mm(h2, W["w_cq"], "nn", tm=1024, tn=512, tk=1024, out_dtypes=[BF16], name="mm_w_cq")
    kv = mm(mem_n, W["w_ckv"], "nn", tm=256, tn=1024, tk=1024, out_dtypes=[BF16], name="mm_w_ckv")
    ox = xatt_fwd(q, kv)
    t2, x2, h3 = mm(ox, W["w_co"], "nn", tk=512, out_dtypes=[F32, F32, BF16], name="mm_w_co", extras=(x1,),
                    vecs=(P["g_x_post"], P["g_ff_pre"]), epi=_norm_epilogue(True), **post)
    sv.update(h2=h2, mem_n=mem_n, q=q, kv=kv, ox=ox, t2=t2, x2=x2)

    W_ff, after_ff = rest_of("mlp", h3)
    W = dict(W, **W_ff)
    pre, act = mm(h3, W["w_ff1"], "nn", out_dtypes=[F32, BF16], name="mm_w_ff1",
                  epi=lambda acc: (acc, jnp.square(jnp.maximum(acc, 0.0))), after=after_ff)
    if g_next is None:
        (ff, x3), h_next = mm(act, W["w_ff2"], "nn", out_dtypes=[F32, F32], name="mm_w_ff2_last", extras=(x2,),
                              vecs=(P["g_ff_post"],), epi=_norm_epilogue(False)), None
    else:
        ff, x3, h_next = mm(act, W["w_ff2"], "nn", out_dtypes=[F32, F32, BF16], name="mm_w_ff2", extras=(x2,),
                            vecs=(P["g_ff_post"], g_next), epi=_norm_epilogue(True))
    sv.update(h3=h3, pre=pre, act=act, ff=ff, W=W)
    return x3, h_next, sv


def _pre_norm_bwd_epilogue(dh, x, add, g):
    r = lax.rsqrt(jnp.mean(x * x, axis=-1, keepdims=True) + EPS)
    xn = x * r
    dxn = dh * g
    return r * (dxn - xn * jnp.mean(dxn * xn, axis=-1, keepdims=True)) + add, jnp.sum(dh * xn, axis=0, keepdims=True)


def layer_bwd(dx, mem, W, P, biases, sv, headsum, emit, after=()):
    G = {}
    dff, G["g_ff_post"] = rms_bwd(sv["ff"], P["g_ff_post"], dx, BF16, "rms_post_bwd", after=after)
    G["w_ff2"] = mm(sv["act"], dff, "tn", out_dtypes=[F32], name="mm_dw_ff2")
    dpre = mm(dff, W["w_ff2"], "nt", out_dtypes=[BF16], name="mm_dact", extras=(sv["pre"],),
              epi=lambda acc, pre: (acc * (2.0 * jnp.maximum(pre, 0.0)),))
    G["w_ff1"] = mm(sv["h3"], dpre, "tn", out_dtypes=[F32], name="mm_dw_ff1")
    sent = emit(("w_ff1", "w_ff2"), G)
    pre_bwd = dict(out_dtypes=[F32], epi=_pre_norm_bwd_epilogue, n_sums=1)
    dx2, G["g_ff_pre"] = mm(dpre, W["w_ff1"], "nt", name="mm_dh3", extras=(sv["x2"], dx), vecs=(P["g_ff_pre"],),
                            after=sent, **pre_bwd)
    dt2, G["g_x_post"] = rms_bwd(sv["t2"], P["g_x_post"], dx2, BF16, "rms_post_bwd")
    G["w_co"] = mm(sv["ox"], dt2, "tn", tm=512, tn=1024, tk=1024, out_dtypes=[F32], name="mm_dw_co")
    dox = mm(dt2, W["w_co"], "nt", tm=1024, tn=512, tk=1024, out_dtypes=[BF16], name="mm_dox")
    dq, dkv = xatt_bwd(sv["q"], sv["kv"], dox)
    G["w_cq"] = mm(sv["h2"], dq, "tn", tm=1024, tn=512, tk=1024, out_dtypes=[F32], name="mm_dw_cq")
    G["w_ckv"] = mm(sv["mem_n"], dkv, "tn", tm=1024, tn=1024, tk=256, out_dtypes=[F32], name="mm_dw_ckv")
    dmem_n = mm(dkv, W["w_ckv"], "nt", tm=256, tn=1024, tk=1024, out_dtypes=[F32], name="mm_dmem")
    _, G["g_mem"] = rms_bwd(mem, P["g_mem"], dmem_n, BF16, "rms_mem_bwd")
    dx1, G["g_x_pre"] = mm(dq, W["w_cq"], "nt", name="mm_dh2", extras=(sv["x1"], dx2), vecs=(P["g_x_pre"],),
                           **pre_bwd)
    proj = sv["proj"]
    dt1, G["g_mix_post"] = rms_bwd(sv["t1"], P["g_mix_post"], dx1, BF16, "rms_post_bwd")
    G["w_out"] = mm(sv["merged"], dt1, "tn", tm=1024, tn=1024, tk=1024, out_dtypes=[F32], name="mm_dw_out")
    dmerged = mm(dt1, W["w_out"], "nt", tm=1024, tn=1024, tk=1024, out_dtypes=[F32], name="mm_dmerged")
    dgates, dups, G["gate_b"] = merge_bwd(proj, W["gate_b"], sv["branches"], W["w_up"], dmerged)
    dbr, dwup = [], []
    for i in range(4):
        dbr.append(mm(dups[i], W["w_up"][i], "nt", tm=1024, tn=512, tk=1024, out_dtypes=[F32], name="mm_dbranch"))
        dwup.append(mm(sv["branches"][i], dups[i], "tn", tm=512, tn=1024, tk=1024, out_dtypes=[F32], name="mm_dw_up"))
    G["w_up"] = jnp.concatenate(dwup, axis=0)
    d_pool, G["pool_w"], G["pool_scale"] = pool_bwd(proj, P["pool_w"], P["pool_scale"], dbr[0])
    cbar = att_combine_bwd(dbr[1], sv["os"], sv["wts"], headsum)
    dqs, dks, dvs, dbias = [], [], [], []
    for g, (win, dil) in enumerate(DIL_GROUPS):
        dq_g, dk_g, dv_g, db_g = att_bwd(proj, biases[g], sv["lses"][g], sv["wts"][g], dbr[1], cbar, g, dil)
        dqs.append(dq_g)
        dks.append(dk_g)
        dvs.append(dv_g)
        dbias.append(db_g)
    G["att_bias"] = dbias
    s5p = P["s5"]
    dy, G["w_glu"], G["b_glu"] = glu_bwd(sv["y"], W["w_glu"], P["b_glu"], dbr[2])
    d_ssm, dbre, dbim, dar, dai, dcre, dcim, G["d_skip"] = s5_bwd(
        proj, sv["hr"], sv["hi"], dy, s5p[2], s5p[3], s5p[0], s5p[1], s5p[4], s5p[5], P["d_skip"])
    G["s5"] = (dar, dai, dbre, dbim, dcre, dcim)
    dzu, dzv, G["sgu_ln_g"], G["sgu_ln_b"], G["w_s"], G["b_s_t"] = sgu_bwd(
        proj, P["sgu_ln_g"], P["sgu_ln_b"], P["w_s"], P["b_s_t"], dbr[3])
    d_qkv = [d.astype(BF16) for d in dqs + dks + dvs]
    dproj = jnp.concatenate([d_pool] + d_qkv + [d_ssm, dzu, dzv] + list(dgates), axis=1)
    sent = emit(("gate_b", "w_glu", "w_up", "w_out", "w_cq", "w_ckv", "w_co"), G)
    G["w_in"] = mm(sv["h1"], dproj, "tn", out_dtypes=[F32], name="mm_dw_in", after=sent)
    sent = emit(("w_in",), G)
    dx0, G["g_mix_pre"] = mm(dproj, W["w_in"], "nt", name="mm_dh1", extras=(sv["x0"], dx1), vecs=(P["g_mix_pre"],),
                             after=sent, **pre_bwd)
    return dx0, G


def _as3d(name, a):
    shape2d, axis = SHARDED[name]
    rows, cols = shape2d
    if axis == 0:
        rows //= 4
    else:
        cols //= 4
    return a.reshape(DEPTH, rows, cols)


def kernel(x, mem, rel_bias, g_mix_pre, g_mix_post, w_in, gate_b, pool_w, pool_scale, a_re, a_im, log_dt, b_re, b_im, c_re, c_im, d_skip, w_glu, b_glu, sgu_ln_g, sgu_ln_b, w_s, b_s, w_up, w_out, g_x_pre, g_x_post, g_mem, w_cq, w_ckv, w_co, g_ff_pre, g_ff_post, w_ff1, w_ff2, loss_target, m_rel_bias, m_g_mix_pre, m_g_mix_post, m_w_in, m_gate_b, m_pool_w, m_pool_scale, m_a_re, m_a_im, m_log_dt, m_b_re, m_b_im, m_c_re, m_c_im, m_d_skip, m_w_glu, m_b_glu, m_sgu_ln_g, m_sgu_ln_b, m_w_s, m_b_s, m_w_up, m_w_out, m_g_x_pre, m_g_x_post, m_g_mem, m_w_cq, m_w_ckv, m_w_co, m_g_ff_pre, m_g_ff_post, m_w_ff1, m_w_ff2, v_rel_bias, v_g_mix_pre, v_g_mix_post, v_w_in, v_gate_b, v_pool_w, v_pool_scale, v_a_re, v_a_im, v_log_dt, v_b_re, v_b_im, v_c_re, v_c_im, v_d_skip, v_w_glu, v_b_glu, v_sgu_ln_g, v_sgu_ln_b, v_w_s, v_b_s, v_w_up, v_w_out, v_g_x_pre, v_g_x_post, v_g_mem, v_w_cq, v_w_ckv, v_w_co, v_g_ff_pre, v_g_ff_post, v_w_ff1, v_w_ff2):
    env = dict(locals())
    weights = {n: env[n] for n in WEIGHT_NAMES}
    mom_m = {n: env["m_" + n] for n in WEIGHT_NAMES}
    mom_v = {n: env["v_" + n] for n in WEIGHT_NAMES}
    x2d = x.reshape(x.shape[1], D_MODEL)
    mem2d = mem.reshape(N_MEM, D_MODEL)
    target = loss_target.reshape(x2d.shape)

    axis_of = {n: SHARDED[n][1] for n in SHARDED_NAMES}
    chip = _chip_index().astype(jnp.int32).reshape(1)
    rest_names = [n for n in SHARDED_NAMES if n != "w_in"]

    def gather_start(l, names, tag, after=()):
        shards = [_as3d(n, weights[n])[l].astype(F32 if n == "gate_b" else MXU_DTYPE) for n in names]
        ax = [axis_of[n] for n in names]
        lands = [lax.empty(tuple(4 * d if i == a else d for i, d in enumerate(s.shape)), s.dtype)
                 for s, a in zip(shards, ax)]
        return (names, ax, tag) + chips_start(shards, lands, ax, "gather", f"gather_start_{tag}", after=after)

    def gather_wait(started, after):
        names, ax, tag, send, recv, shards, lands, _ = started
        _, lands = chips_wait(send, recv, shards, lands, ax, "gather", f"gather_wait_{tag}", after=after)
        W = dict(zip(names, lands))
        if "w_up" in W:
            W["w_up"] = W["w_up"].reshape(4, W_BRANCH, D_MODEL)
        return W

    biases = [att_bias(rel_bias, g, dil) for g, (_, dil) in enumerate(DIL_GROUPS)]
    lanes = np.arange(W_BRANCH) // ATT_HEAD_DIM
    headsum = jnp.asarray(lanes[:, None] == lanes[None, :], dtype=BF16)

    def small_params(l, s5_prepared):
        vec = lambda a: a[l].reshape(1, -1)
        return {
            "g_mix_pre": vec(g_mix_pre), "g_mix_post": vec(g_mix_post), "g_x_pre": vec(g_x_pre), "g_x_post": vec(g_x_post),
            "g_mem": vec(g_mem), "g_ff_pre": vec(g_ff_pre), "g_ff_post": vec(g_ff_post), "pool_w": pool_w[l],
            "pool_scale": vec(pool_scale), "d_skip": vec(d_skip), "b_glu": vec(b_glu), "sgu_ln_g": vec(sgu_ln_g),
            "sgu_ln_b": vec(sgu_ln_b), "w_s": w_s[l], "b_s_t": b_s[l].T, "s5": s5_prepared,
        }

    Ws, Ps, saved, s5_vjps = [], [], [], []
    xl = x2d
    hl = rms_fwd(x2d, g_mix_pre[0].reshape(1, -1), BF16, "rms_pre")
    flying = {"next": gather_start(0, ["w_in"], "0_w_in")}
    for l in range(DEPTH):
        s5_prepared, s5_vjp = jax.vjp(s5_params, a_re[l], a_im[l], log_dt[l], b_re[l], b_im[l], c_re[l], c_im[l])
        token_of = lambda started: (started[7],)
        if l == 0:
            w_in_l = gather_wait(flying["next"], [*biases, hl])["w_in"]
            flying["rest"] = gather_start(0, rest_names, "0_rest", after=[w_in_l])
            first_after = token_of(flying["rest"])

            def rest_of(stage, value):
                if stage != "mixer":
                    return {}, ()
                W = gather_wait(flying["rest"], [value])
                flying["next"] = gather_start(1, SHARDED_NAMES, "1", after=[W["w_out"]])
                return W, token_of(flying["next"])
        else:
            W_l = gather_wait(flying["next"], [xl])
            w_in_l, first_after = W_l["w_in"], ()
            if l + 1 < DEPTH:
                flying["next"] = gather_start(l + 1, SHARDED_NAMES, str(l + 1), after=[w_in_l])
                first_after = token_of(flying["next"])
            rest_of = lambda stage, value, W_l=W_l: (W_l if stage == "mixer" else {}, ())
        P = small_params(l, s5_prepared)
        g_next = g_mix_pre[l + 1].reshape(1, -1) if l + 1 < DEPTH else None
        xl, hl, sv = layer_fwd(xl, hl, mem2d, w_in_l, rest_of, P, biases, g_next, after=first_after)
        Ws.append(sv["W"])
        Ps.append(P)
        saved.append(sv)
        s5_vjps.append(s5_vjp)
    loss_local, dx = loss_and_grad(xl, target)
    loss = lax.psum(loss_local, ("x", "y", "c"))

    scattered = []

    def scatter_start(l, names, srcs):
        ax = [axis_of.get(n) for n in names]
        lands = []
        for s, a in zip(srcs, ax):
            r, c = s.shape
            lands.append(lax.empty((4, r // 4 if a == 0 else r, c // 4 if a == 1 else c), F32))
        tag = f"{l}_{names[0]}"
        send, recv, srcs, lands, token = chips_start(srcs, lands, ax, "scatter", f"grads_start_{tag}")
        scattered.append((l, names, ax, tag, send, recv, srcs, lands))
        return (token,)

    grads = [None] * DEPTH
    for l in reversed(range(DEPTH)):
        emit = lambda names, G, l=l: scatter_start(l, list(names), [G[n] for n in names])
        dx, grads[l] = layer_bwd(dx, mem2d, Ws[l], Ps[l], biases, saved[l], headsum, emit)
    grad_x = dx.reshape(x.shape)

    rep = {}
    stack = lambda key, shape: jnp.stack([grads[l][key] for l in range(DEPTH)]).reshape(shape)
    for n in ("g_mix_pre", "g_mix_post", "g_x_pre", "g_x_post", "g_mem", "g_ff_pre", "g_ff_post"):
        rep[n] = stack(n, (DEPTH, D_MODEL))
    for n in ("pool_scale", "d_skip", "b_glu", "sgu_ln_g", "sgu_ln_b"):
        rep[n] = stack(n, (DEPTH, W_BRANCH))
    rep["pool_w"] = stack("pool_w", pool_w.shape)
    rep["w_s"] = stack("w_s", w_s.shape)
    rep["b_s"] = jnp.stack([grads[l]["b_s_t"][:, :4].T for l in range(DEPTH)])
    s5_grads = [s5_vjps[l](tuple(grads[l]["s5"])) for l in range(DEPTH)]
    for i, n in enumerate(("a_re", "a_im", "log_dt", "b_re", "b_im", "c_re", "c_im")):
        rep[n] = jnp.stack([s5_grads[l][i] for l in range(DEPTH)])
    dbias = [sum(grads[l]["att_bias"][g] for l in range(DEPTH)) for g in range(len(DIL_GROUPS))]
    rep["rel_bias"] = jnp.concatenate([att_bias_grad(dbias[g], dil) for g, (_, dil) in enumerate(DIL_GROUPS)], axis=1)
    rep_shapes = [weights[n].shape for n in REPLICATED_NAMES]
    packed_g = pack_small([rep[n] for n in REPLICATED_NAMES])

    small_sent = scatter_start(0, ["small"], [packed_g])
    stacked = {}

    def collect(record, after):
        l, names, ax, tag, send, recv, srcs, lands = record
        srcs, lands = chips_wait(send, recv, srcs, lands, ax, "scatter", f"grads_wait_{tag}", after=after)
        for n, own, arrived, a in zip(names, srcs, lands, ax):
            if n not in stacked:
                stacked[n] = lax.empty((1 if n == "small" else DEPTH,) + arrived.shape[1:], F32)
            stacked[n] = sum_chips(arrived, own, a, chip, stacked[n], 0 if n == "small" else l, "sum_chips")

    out_g, out_d, out_m, out_v = {}, {}, {}, {}

    def update(names, tag):
        partial = [stacked[n].reshape(-1, stacked[n].shape[-1]) for n in names]
        other = swap_cores(partial, f"swap_cores_{tag}")
        for n, mine, theirs in zip(names, partial, other):
            if n == "small":
                for name, ga, gb in zip(REPLICATED_NAMES, unpack_small(mine, rep_shapes), unpack_small(theirs, rep_shapes)):
                    rows_of = lambda a: a.reshape(-1, a.shape[-1])
                    res = adamw(rows_of(weights[name]), rows_of(ga), rows_of(gb), rows_of(mom_m[name]),
                                rows_of(mom_v[name]), "adamw_small")
                    out_g[name], out_d[name], out_m[name], out_v[name] = [r.reshape(weights[name].shape) for r in res]
            else:
                flat = lambda a: a.reshape(mine.shape)
                res = adamw(flat(weights[n]), mine, theirs, flat(mom_m[n]), flat(mom_v[n]), "adamw")
                out_g[n], out_d[n], out_m[n], out_v[n] = [r.reshape(weights[n].shape) for r in res]

    late = [r for r in scattered if r[1] == ["small"] or (r[0] == 0 and r[1] == ["w_in"])]
    for record in scattered:
        if not any(record is r for r in late):
            collect(record, [dx, *small_sent])
    update(rest_names, "rest")
    collect(late[0], [out_d[n] for n in rest_names])
    update(["w_in"], "w_in")
    collect(late[1], [out_d["w_in"]])
    update(["small"], "small")

    return (loss, grad_x, *[out_g[n] for n in WEIGHT_NAMES], *[out_d[n] for n in WEIGHT_NAMES],
            *[out_m[n] for n in WEIGHT_NAMES], *[out_v[n] for n in WEIGHT_NAMES])
```

```python
import functools
import math

import numpy as np
import jax
import jax.numpy as jnp
from jax import lax
from jax.experimental import pallas as pl
from jax.experimental.pallas import tpu as pltpu

F32 = jnp.float32
BF16 = jnp.bfloat16
MXU_DTYPE = jnp.bfloat16
MESH_ID = pl.DeviceIdType.MESH
VMEM_LIMIT_BYTES = 56 * 1024 * 1024

D_MODEL = 1024
DEPTH = 4
N_MEM = 256
W_BRANCH = 512
POOL_WINDOWS = (2, 4, 8, 16)
POOL_HALO = 16
DIL_GROUPS = ((128, 1), (512, 4), (2048, 16))
BAND = 128
ATT_HEADS = 8
ATT_HEAD_DIM = 64
SSM_GROUP = 16
SSM_GROUPS = 32
SSM_STATE = 64
SSM_COLS = SSM_GROUPS * SSM_STATE
SSM_T = 512
SGU_CHUNK = 128
X_HEADS = 4
X_HEAD_DIM = 128
D_FF = 4096
REL_BUCKETS = 32
REL_MAX_DIST = 2048
EPS = 1e-6
NEG_INF = -1e30
OFF_POOL = 0
OFF_ATT = 512
OFF_SSM = OFF_ATT + 9 * W_BRANCH
OFF_SGU = OFF_SSM + W_BRANCH
OFF_GATE = OFF_SGU + 2 * W_BRANCH
IN_WIDTH = OFF_GATE + 4 * D_MODEL

ADAM_LR = 0.001
ADAM_B1 = 0.9
ADAM_B2 = 0.999
ADAM_EPS = 1e-08
ADAM_WD = 0.01
ADAM_STEP = 10

GELU_C = math.sqrt(2.0 / math.pi)

WEIGHT_NAMES = ['rel_bias', 'g_mix_pre', 'g_mix_post', 'w_in', 'gate_b', 'pool_w', 'pool_scale', 'a_re', 'a_im',
                'log_dt', 'b_re', 'b_im', 'c_re', 'c_im', 'd_skip', 'w_glu', 'b_glu', 'sgu_ln_g', 'sgu_ln_b',
                'w_s', 'b_s', 'w_up', 'w_out', 'g_x_pre', 'g_x_post', 'g_mem', 'w_cq', 'w_ckv', 'w_co',
                'g_ff_pre', 'g_ff_post', 'w_ff1', 'w_ff2']
SHARDED = {
    'w_in': ((D_MODEL, IN_WIDTH), 1),
    'gate_b': ((4, D_MODEL), 1),
    'w_glu': ((W_BRANCH, W_BRANCH), 0),
    'w_up': ((4 * W_BRANCH, D_MODEL), 1),
    'w_out': ((D_MODEL, D_MODEL), 0),
    'w_cq': ((D_MODEL, W_BRANCH), 0),
    'w_ckv': ((D_MODEL, D_MODEL), 0),
    'w_co': ((W_BRANCH, D_MODEL), 1),
    'w_ff1': ((D_MODEL, D_FF), 1),
    'w_ff2': ((D_FF, D_MODEL), 0),
}
SHARDED_NAMES = list(SHARDED)
REPLICATED_NAMES = [n for n in WEIGHT_NAMES if n not in SHARDED]


def _cp(n_axes):
    return pltpu.CompilerParams(dimension_semantics=("arbitrary",) * n_axes, vmem_limit_bytes=VMEM_LIMIT_BYTES)


def _dot(a, b, dims="nn"):
    cd = {"nn": ((1,), (0,)), "nt": ((1,), (1,)), "tn": ((0,), (0,))}[dims]
    return lax.dot_general(a.astype(MXU_DTYPE), b.astype(MXU_DTYPE), (cd, ((), ())), preferred_element_type=F32)


def _gelu(x):
    return 0.5 * x * (1.0 + jnp.tanh(GELU_C * (x + 0.044715 * (x * x * x))))


def _gelu_grad(x):
    t = jnp.tanh(GELU_C * (x + 0.044715 * (x * x * x)))
    return 0.5 * (1.0 + t) + 0.5 * x * (1.0 - t * t) * (GELU_C * (1.0 + 3.0 * 0.044715 * (x * x)))


def _sigmoid(x):
    return 1.0 / (1.0 + jnp.exp(-x))


MM_TILES = {
    "mm_w_in": (2048, 1536, 1024), "mm_dw_in": (1024, 1536, 2048), "mm_dh1": (1024, 1024, 1536),
    "mm_w_ff1": (2048, 1024, 1024), "mm_w_ff2": (1024, 1024, 2048), "mm_w_ff2_last": (1024, 1024, 2048),
    "mm_dw_ff2": (1024, 1024, 2048), "mm_dact": (2048, 1024, 1024), "mm_dw_ff1": (1024, 1024, 2048),
    "mm_dh3": (1024, 1024, 2048), "mm_dh2": (1024, 1024, 512),
}


def mm(a, b, dims, *, out_dtypes, name, tm=None, tn=None, tk=None, extras=(), vecs=(), epi=None, n_sums=0, after=()):
    if dims == "tn":
        K, M = a.shape
        N = b.shape[1]
    else:
        M, K = a.shape
        N = b.shape[1] if dims == "nn" else b.shape[0]
    if tm is None:
        tm, tn, tk = MM_TILES[name]
    tm, tn, tk = min(tm, M), min(tn, N), min(tk, K)
    assert M % tm == 0 and N % tn == 0 and K % tk == 0, (name, M, N, K, tm, tn, tk)
    assert n_sums == 0 or tn == N, name
    nk = K // tk
    ne, no = len(extras) + len(vecs), len(out_dtypes)
    if epi is None:
        epi = lambda acc: (acc,)
    a_spec = (pl.BlockSpec((tk, tm), lambda i, j, k: (k, i)) if dims == "tn"
              else pl.BlockSpec((tm, tk), lambda i, j, k: (i, k)))
    b_spec = (pl.BlockSpec((tn, tk), lambda i, j, k: (j, k)) if dims == "nt"
              else pl.BlockSpec((tk, tn), lambda i, j, k: (k, j)))
    mn_spec = pl.BlockSpec((tm, tn), lambda i, j, k: (i, j))
    vec_spec = pl.BlockSpec((1, tn), lambda i, j, k: (0, j))

    def body(a_ref, b_ref, *rest):
        first_out = ne + len(after)
        extra_refs, out_refs = rest[:ne], rest[first_out:first_out + no]
        sum_refs = rest[first_out + no:first_out + no + n_sums]
        part = _dot(a_ref[...], b_ref[...], dims)

        def finish(acc):
            results = epi(acc, *[e[...] for e in extra_refs])
            for o_ref, r in zip(out_refs, results[:no]):
                o_ref[...] = r.astype(o_ref.dtype)
            for s_ref, r in zip(sum_refs, results[no:]):
                @pl.when(pl.program_id(0) == 0)
                def _():
                    s_ref[...] = r

                @pl.when(pl.program_id(0) > 0)
                def _():
                    s_ref[...] += r

        if nk == 1:
            finish(part)
        else:
            acc_ref = rest[-1]
            k = pl.program_id(2)

            @pl.when(k == 0)
            def _():
                acc_ref[...] = part

            @pl.when(k > 0)
            def _():
                acc_ref[...] += part

            @pl.when(k == nk - 1)
            def _():
                finish(acc_ref[...])

    outs = pl.pallas_call(
        body, name=name, grid=(M // tm, N // tn, nk),
        in_specs=[a_spec, b_spec] + [mn_spec] * len(extras) + [vec_spec] * len(vecs) + [ANY] * len(after),
        out_specs=[mn_spec] * no + [vec_spec] * n_sums,
        out_shape=[jax.ShapeDtypeStruct((M, N), dt) for dt in out_dtypes] + [jax.ShapeDtypeStruct((1, N), F32)] * n_sums,
        scratch_shapes=[pltpu.VMEM((tm, tn), F32)] if nk > 1 else [],
        compiler_params=_cp(3),
    )(a, b, *extras, *vecs, *after)
    return outs[0] if no + n_sums == 1 else outs


ROW_TILE = 512


def rms_fwd(x, g, out_dtype, name, res=None):
    M, D = x.shape
    tm = min(ROW_TILE, M)

    def body(x_ref, g_ref, *rest):
        o_ref = rest[-1]
        xf = x_ref[...]
        y = xf * lax.rsqrt(jnp.mean(xf * xf, axis=-1, keepdims=True) + EPS) * g_ref[...]
        if res is not None:
            y = y + rest[0][...]
        o_ref[...] = y.astype(o_ref.dtype)

    row = pl.BlockSpec((tm, D), lambda i: (i, 0))
    return pl.pallas_call(
        body, name=name, grid=(M // tm,),
        in_specs=[row, pl.BlockSpec((1, D), lambda i: (0, 0))] + ([row] if res is not None else []),
        out_specs=row, out_shape=jax.ShapeDtypeStruct((M, D), out_dtype), compiler_params=_cp(1),
    )(x, g, *([res] if res is not None else []))


def rms_bwd(x, g, dy, dx_dtype, name, add=None, after=()):
    M, D = x.shape
    tm = min(ROW_TILE, M)

    def body(x_ref, g_ref, dy_ref, *rest):
        dx_ref, dg_ref = rest[-2], rest[-1]
        xf = x_ref[...]
        dyf = dy_ref[...].astype(F32)
        r = lax.rsqrt(jnp.mean(xf * xf, axis=-1, keepdims=True) + EPS)
        xn = xf * r
        dxn = dyf * g_ref[...]
        dx = r * (dxn - xn * jnp.mean(dxn * xn, axis=-1, keepdims=True))
        if add is not None:
            dx = dx + rest[0][...]
        dx_ref[...] = dx.astype(dx_ref.dtype)
        dg = jnp.sum(dyf * xn, axis=0, keepdims=True)

        @pl.when(pl.program_id(0) == 0)
        def _():
            dg_ref[...] = dg

        @pl.when(pl.program_id(0) > 0)
        def _():
            dg_ref[...] += dg

    row = pl.BlockSpec((tm, D), lambda i: (i, 0))
    vec = pl.BlockSpec((1, D), lambda i: (0, 0))
    return pl.pallas_call(
        body, name=name, grid=(M // tm,),
        in_specs=[row, vec, row] + ([row] if add is not None else []) + [ANY] * len(after),
        out_specs=[row, vec],
        out_shape=[jax.ShapeDtypeStruct((M, D), dx_dtype), jax.ShapeDtypeStruct((1, D), F32)],
        compiler_params=_cp(1),
    )(x, g, dy, *([add] if add is not None else []), *after)


def loss_and_grad(y, target):
    M, D = y.shape
    tm = ROW_TILE

    def body(y_ref, t_ref, part_ref, dy_ref):
        e = y_ref[...] - t_ref[...]
        dy_ref[...] = e / D
        part_ref[...] = jnp.broadcast_to(0.5 * jnp.sum(jnp.mean(e * e, axis=-1, keepdims=True), axis=0, keepdims=True),
                                         (8, 128))

    row = pl.BlockSpec((tm, D), lambda i: (i, 0))
    part, dy = pl.pallas_call(
        body, name="loss", grid=(M // tm,), in_specs=[row, row],
        out_specs=[pl.BlockSpec((8, 128), lambda i: (i, 0)), row],
        out_shape=[jax.ShapeDtypeStruct((8 * (M // tm), 128), F32), jax.ShapeDtypeStruct((M, D), F32)],
        compiler_params=_cp(1),
    )(y, target)
    return jnp.sum(part[::8, 0]), dy


POOL_ROWS = 512


def _pool_window_sum(xw, gi, roll_of):
    s1 = xw + pltpu.roll(xw, roll_of(1), 0)
    s2 = s1 + pltpu.roll(s1, roll_of(2), 0)
    s3 = s2 + pltpu.roll(s2, roll_of(4), 0)
    s4 = s3 + pltpu.roll(s3, roll_of(8), 0)
    return jnp.where(gi == 0, s1, jnp.where(gi == 1, s2, jnp.where(gi == 2, s3, s4)))


def _pool_cnt(i, gi):
    rows = lax.broadcasted_iota(jnp.int32, (POOL_ROWS, 128), 0) + i * POOL_ROWS
    w = jnp.where(gi == 0, 2, jnp.where(gi == 1, 4, jnp.where(gi == 2, 8, 16)))
    return jnp.minimum(rows + 1, w).astype(F32)


def pool_fwd(proj, pool_w, scale):
    S = proj.shape[0]
    nchunk = S // POOL_ROWS
    slab = POOL_ROWS + POOL_HALO

    def body(x_ref, w_ref, sc_ref, o_ref, pad_ref):
        gi = pl.program_id(0)
        pad_ref[0:POOL_HALO, :] = jnp.zeros((POOL_HALO, 128), F32)
        pad_ref[POOL_HALO:, :] = x_ref[...]
        for i in range(nchunk):
            xw = pad_ref[i * POOL_ROWS:i * POOL_ROWS + slab, :]
            ssum = _pool_window_sum(xw, gi, lambda d: d)[POOL_HALO:, :]
            p = ssum / _pool_cnt(i, gi) - xw[POOL_HALO:, :]
            o_ref[i * POOL_ROWS:(i + 1) * POOL_ROWS, :] = (_dot(p, w_ref[...]) * sc_ref[...]).astype(o_ref.dtype)

    return pl.pallas_call(
        body, name="pool_fwd", grid=(4,),
        in_specs=[pl.BlockSpec((S, 128), lambda g: (0, OFF_POOL // 128 + g)),
                  pl.BlockSpec((None, 128, 128), lambda g: (g, 0, 0)),
                  pl.BlockSpec((1, 128), lambda g: (0, g))],
        out_specs=pl.BlockSpec((S, 128), lambda g: (0, g)),
        out_shape=jax.ShapeDtypeStruct((S, W_BRANCH), BF16),
        scratch_shapes=[pltpu.VMEM((S + POOL_HALO, 128), F32)],
        compiler_params=_cp(1),
    )(proj, pool_w, scale)


def pool_bwd(proj, pool_w, scale, dy):
    S = proj.shape[0]
    nchunk = S // POOL_ROWS
    slab = POOL_ROWS + POOL_HALO

    def body(x_ref, w_ref, sc_ref, dy_ref, dx_ref, dw_ref, dsc_ref, pad_ref, pad2_ref, dp_ref):
        gi = pl.program_id(0)
        pad_ref[0:POOL_HALO, :] = jnp.zeros((POOL_HALO, 128), F32)
        pad_ref[POOL_HALO:, :] = x_ref[...]
        pad2_ref[S:, :] = jnp.zeros((POOL_HALO, 128), F32)
        dw = jnp.zeros((128, 128), F32)
        dsc = jnp.zeros((1, 128), F32)
        for i in range(nchunk):
            xw = pad_ref[i * POOL_ROWS:i * POOL_ROWS + slab, :]
            cnt = _pool_cnt(i, gi)
            p = _pool_window_sum(xw, gi, lambda d: d)[POOL_HALO:, :] / cnt - xw[POOL_HALO:, :]
            dyc = dy_ref[i * POOL_ROWS:(i + 1) * POOL_ROWS, :]
            dsc = dsc + jnp.sum(dyc * _dot(p, w_ref[...]), axis=0, keepdims=True)
            dys = dyc * sc_ref[...]
            dw = dw + _dot(p, dys, "tn")
            dp = _dot(dys, w_ref[...], "nt")
            dp_ref[i * POOL_ROWS:(i + 1) * POOL_ROWS, :] = dp
            pad2_ref[i * POOL_ROWS:(i + 1) * POOL_ROWS, :] = dp / cnt
        dw_ref[...] = dw
        dsc_ref[...] = dsc
        for i in range(nchunk):
            xw = pad2_ref[i * POOL_ROWS:i * POOL_ROWS + slab, :]
            fsum = _pool_window_sum(xw, gi, lambda d: slab - d)[:POOL_ROWS, :]
            rows = slice(i * POOL_ROWS, (i + 1) * POOL_ROWS)
            dx_ref[rows, :] = (fsum - dp_ref[rows, :]).astype(dx_ref.dtype)

    return pl.pallas_call(
        body, name="pool_bwd", grid=(4,),
        in_specs=[pl.BlockSpec((S, 128), lambda g: (0, OFF_POOL // 128 + g)),
                  pl.BlockSpec((None, 128, 128), lambda g: (g, 0, 0)),
                  pl.BlockSpec((1, 128), lambda g: (0, g)),
                  pl.BlockSpec((S, 128), lambda g: (0, g))],
        out_specs=[pl.BlockSpec((S, 128), lambda g: (0, g)),
                   pl.BlockSpec((None, 128, 128), lambda g: (g, 0, 0)),
                   pl.BlockSpec((1, 128), lambda g: (0, g))],
        out_shape=[jax.ShapeDtypeStruct((S, W_BRANCH), BF16), jax.ShapeDtypeStruct((4, 128, 128), F32),
                   jax.ShapeDtypeStruct((1, W_BRANCH), F32)],
        scratch_shapes=[pltpu.VMEM((S + POOL_HALO, 128), F32), pltpu.VMEM((S + POOL_HALO, 128), F32),
                        pltpu.VMEM((S, 128), F32)],
        compiler_params=_cp(1),
    )(proj, pool_w, scale, dy)


def _t5_bucket(n):
    exact = REL_BUCKETS // 2
    nf = np.maximum(n, 1).astype(np.float32)
    large = exact + (np.log(nf / exact) / np.log(REL_MAX_DIST / exact) * (REL_BUCKETS - exact)).astype(np.int32)
    large = np.minimum(large, REL_BUCKETS - 1)
    return np.where(n < exact, n, large).astype(np.int32)


def _band_onehot(dil):
    i = np.arange(BAND)[:, None]
    kk = np.arange(2 * BAND)[None, :]
    dist = BAND + i - kk
    local = (dist >= 0) & (dist <= BAND)
    bucket = _t5_bucket(np.clip(dist, 0, BAND) * dil)
    onehot = (bucket.reshape(-1, 1) == np.arange(REL_BUCKETS)[None, :]).astype(np.float32)
    return onehot, local


def att_bias(rel_bias, g, dil):
    onehot, local = _band_onehot(dil)
    tab = jnp.dot(jnp.asarray(onehot), rel_bias[:, g * ATT_HEADS:(g + 1) * ATT_HEADS], precision=lax.Precision.HIGHEST)
    bias = tab.reshape(BAND, 2 * BAND, ATT_HEADS).transpose(2, 0, 1)
    return jnp.where(jnp.asarray(local)[None], bias, NEG_INF)


def att_bias_grad(dbias, dil):
    onehot, _ = _band_onehot(dil)
    flat = dbias.transpose(1, 2, 0).reshape(BAND * 2 * BAND, ATT_HEADS)
    return jnp.dot(jnp.asarray(onehot).T, flat, precision=lax.Precision.HIGHEST)


def _head_lanes():
    return lax.broadcasted_iota(jnp.int32, (BAND, 128), 1) < ATT_HEAD_DIM


def _att_cols(part, g, hp):
    return (OFF_ATT + part * 3 * W_BRANCH + g * W_BRANCH) // 128 + hp


def _att_pair(q, k, v, bias, lse_b, do, delta_b, hh, head0, mask=None):
    sel = head0 if hh == 0 else jnp.logical_not(head0)
    s = _dot(jnp.where(sel, q, 0.0), k, "nt") * (ATT_HEAD_DIM ** -0.5) + bias
    if mask is not None:
        s = jnp.where(mask, NEG_INF, s)
    c = hh * ATT_HEAD_DIM
    p = jnp.exp(s - lse_b[:, c:c + 1])
    dp = _dot(jnp.where(sel, do, 0.0), v, "nt")
    return p, p * (dp - delta_b[:, c:c + 1])


ATT_BLOCKS = {1: 32, 4: 8, 16: 2}


def _att_rows(r, i, d):
    return pl.ds(r + d * BAND * i, BAND, stride=d) if d > 1 else pl.ds(BAND * i, BAND)


def _att_specs(g, d, nq):
    ch, pb = BAND * d * nq, BAND * d
    cur = lambda part: pl.BlockSpec((ch, 128), lambda hp, n: (n, _att_cols(part, g, hp)))
    prev = lambda part: pl.BlockSpec((pb, 128), lambda hp, n: (jnp.maximum(n * nq - 1, 0), _att_cols(part, g, hp)))
    return [cur(0), cur(1), prev(1), cur(2), prev(2)]


def _att_keys(cur_ref, prev_ref, r, i, d):
    before = cur_ref[_att_rows(r, i - 1, d), :] if i > 0 else prev_ref[_att_rows(r, 0, d), :]
    return jnp.concatenate([before, cur_ref[_att_rows(r, i, d), :]], axis=0).astype(MXU_DTYPE)


def att_fwd(proj, bias, g, d):
    S = proj.shape[0]
    nq = ATT_BLOCKS[d]
    ch = BAND * d * nq

    def body(q_ref, kc_ref, kp_ref, vc_ref, vp_ref, b_ref, o_ref, l_ref):
        n = pl.program_id(1)
        head0 = _head_lanes()
        first = jnp.logical_and(lax.broadcasted_iota(jnp.int32, (BAND, 2 * BAND), 1) < BAND, n == 0)
        for r in range(d):
            for i in range(nq):
                rows = _att_rows(r, i, d)
                q = q_ref[rows, :]
                k = _att_keys(kc_ref, kp_ref, r, i, d)
                v = _att_keys(vc_ref, vp_ref, r, i, d)
                o_h, l_h = [], []
                for hh in range(2):
                    qm = jnp.where(head0 if hh == 0 else jnp.logical_not(head0), q, 0.0)
                    s = _dot(qm, k, "nt") * (ATT_HEAD_DIM ** -0.5) + b_ref[hh]
                    if i == 0:
                        s = jnp.where(first, NEG_INF, s)
                    m = jnp.max(s, axis=-1, keepdims=True)
                    p = jnp.exp(s - m)
                    l = jnp.sum(p, axis=-1, keepdims=True)
                    o_h.append(_dot(p / l, v))
                    l_h.append(jnp.broadcast_to(m + jnp.log(l), (BAND, 128)))
                o_ref[rows, :] = jnp.where(head0, o_h[0], o_h[1])
                l_ref[rows, :] = jnp.where(head0, l_h[0], l_h[1])

    out = pl.BlockSpec((ch, 128), lambda hp, n: (n, hp))
    return pl.pallas_call(
        body, name=f"att_fwd_d{d}", grid=(4, S // ch),
        in_specs=_att_specs(g, d, nq) + [pl.BlockSpec((2, BAND, 2 * BAND), lambda hp, n: (hp, 0, 0))],
        out_specs=[out, out],
        out_shape=[jax.ShapeDtypeStruct((S, W_BRANCH), F32), jax.ShapeDtypeStruct((S, W_BRANCH), F32)],
        compiler_params=_cp(2),
    )(proj, proj, proj, proj, proj, bias)


def att_bwd(proj, bias, lse, wts, dout, cbar, g, d):
    S = proj.shape[0]
    nq = ATT_BLOCKS[d]
    ch, pb = BAND * d * nq, BAND * d
    nb = S // ch
    scale = ATT_HEAD_DIM ** -0.5

    def body(q_ref, kc_ref, kp_ref, vc_ref, vp_ref, b_ref, l_ref, w_ref, do_ref, cb_ref,
             dq_ref, dk_ref, dv_ref, ek_ref, ev_ref, db_ref):
        n = pl.program_id(1)
        head0 = _head_lanes()
        first = jnp.logical_and(lax.broadcasted_iota(jnp.int32, (BAND, 2 * BAND), 1) < BAND, n == 0)

        @pl.when(n == 0)
        def _():
            db_ref[...] = jnp.zeros(db_ref.shape, F32)

        for r in range(d):
            own_k = own_v = None
            for i in range(nq):
                rows = _att_rows(r, i, d)
                q = q_ref[rows, :]
                k = _att_keys(kc_ref, kp_ref, r, i, d)
                v = _att_keys(vc_ref, vp_ref, r, i, d)
                w = w_ref[rows, :]
                do = w * do_ref[rows, :]
                delta = w * cb_ref[rows, :]
                lse_b = l_ref[rows, :]
                dq_h, dk_h, dv_h = [], [], []
                for hh in range(2):
                    p, ds = _att_pair(q, k, v, b_ref[hh], lse_b, do, delta, hh, head0, mask=first if i == 0 else None)
                    db_ref[hh] += ds
                    ds = ds * scale
                    dq_h.append(_dot(ds, k))
                    dk_h.append(_dot(ds, q, "tn"))
                    dv_h.append(_dot(p, do, "tn"))
                dq_ref[rows, :] = jnp.where(head0, dq_h[0], dq_h[1])
                head0_keys = jnp.concatenate([head0, head0], axis=0)
                dk2 = jnp.where(head0_keys, dk_h[0], dk_h[1])
                dv2 = jnp.where(head0_keys, dv_h[0], dv_h[1])
                if i == 0:
                    ek_ref[_att_rows(r, 0, d), :] = dk2[:BAND]
                    ev_ref[_att_rows(r, 0, d), :] = dv2[:BAND]
                else:
                    dk_ref[_att_rows(r, i - 1, d), :] = own_k + dk2[:BAND]
                    dv_ref[_att_rows(r, i - 1, d), :] = own_v + dv2[:BAND]
                own_k, own_v = dk2[BAND:], dv2[BAND:]
            dk_ref[_att_rows(r, nq - 1, d), :] = own_k
            dv_ref[_att_rows(r, nq - 1, d), :] = own_v

    cur = pl.BlockSpec((ch, 128), lambda hp, n: (n, hp))
    edge = pl.BlockSpec((pb, 128), lambda hp, n: (n, hp))
    bias_spec = pl.BlockSpec((2, BAND, 2 * BAND), lambda hp, n: (hp, 0, 0))
    big = jax.ShapeDtypeStruct((S, W_BRANCH), F32)
    small = jax.ShapeDtypeStruct((nb * pb, W_BRANCH), F32)
    dq, dk, dv, ek, ev, db = pl.pallas_call(
        body, name=f"att_bwd_d{d}", grid=(4, nb),
        in_specs=_att_specs(g, d, nq) + [bias_spec, cur, cur, cur, cur],
        out_specs=[cur, cur, cur, edge, edge, bias_spec],
        out_shape=[big, big, big, small, small, jax.ShapeDtypeStruct((ATT_HEADS, BAND, 2 * BAND), F32)],
        compiler_params=_cp(2),
    )(proj, proj, proj, proj, proj, bias, lse, wts, dout, cbar)

    def with_edges(main, edges):
        if nb == 1:
            return main
        main = main.reshape(nb, ch, W_BRANCH)
        add = jnp.pad(edges.reshape(nb, pb, W_BRANCH)[1:], ((0, 1), (ch - pb, 0), (0, 0)))
        return (main + add).reshape(S, W_BRANCH)

    return dq, with_edges(dk, ek), with_edges(dv, ev), db


def att_combine(os_, lses):
    S = os_[0].shape[0]

    def body(o0, o1, o2, l0, l1, l2, out_ref, w0, w1, w2):
        ls = [l0[...], l1[...], l2[...]]
        m = jnp.maximum(jnp.maximum(ls[0], ls[1]), ls[2])
        es = [jnp.exp(l - m) for l in ls]
        den = es[0] + es[1] + es[2]
        ws = [e / den for e in es]
        out_ref[...] = (ws[0] * o0[...] + ws[1] * o1[...] + ws[2] * o2[...]).astype(out_ref.dtype)
        for w_ref, w in zip((w0, w1, w2), ws):
            w_ref[...] = w

    blk = pl.BlockSpec((ROW_TILE, W_BRANCH), lambda i: (i, 0))
    f = jax.ShapeDtypeStruct((S, W_BRANCH), F32)
    return pl.pallas_call(
        body, name="att_combine", grid=(S // ROW_TILE,), in_specs=[blk] * 6, out_specs=[blk] * 4,
        out_shape=[jax.ShapeDtypeStruct((S, W_BRANCH), BF16), f, f, f], compiler_params=_cp(1),
    )(*os_, *lses)


def _split3(x):
    x1 = x.astype(BF16)
    r1 = x - x1.astype(F32)
    x2 = r1.astype(BF16)
    x3 = (r1 - x2.astype(F32)).astype(BF16)
    return x1, x2, x3


def att_combine_bwd(dout, os_, wts, headsum):
    S = dout.shape[0]

    def body(do_ref, o0, o1, o2, w0, w1, w2, e_ref, cb_ref):
        out = w0[...] * o0[...] + w1[...] * o1[...] + w2[...] * o2[...]
        e = e_ref[...]
        acc = jnp.zeros((ROW_TILE, W_BRANCH), F32)
        for term in _split3(do_ref[...] * out):
            acc = acc + jnp.dot(term, e, preferred_element_type=F32)
        cb_ref[...] = acc

    blk = pl.BlockSpec((ROW_TILE, W_BRANCH), lambda i: (i, 0))
    return pl.pallas_call(
        body, name="att_combine_bwd", grid=(S // ROW_TILE,),
        in_specs=[blk] * 7 + [pl.BlockSpec((W_BRANCH, W_BRANCH), lambda i: (0, 0))], out_specs=blk,
        out_shape=jax.ShapeDtypeStruct((S, W_BRANCH), F32), compiler_params=_cp(1),
    )(dout, *os_, *wts, headsum)


def _cmul(ar, ai, br, bi):
    return ar * br - ai * bi, ar * bi + ai * br


SCAN_ROWS = 8
SCAN_GROUPS = SSM_T // SCAN_ROWS


def _log_scan(xr, xi, mr, mi, rows, n, steps, reverse):
    total = xr.shape[0]
    for k in range(steps):
        dd = 1 << k
        keep = rows < n - dd if reverse else rows >= dd
        shift = total - dd if reverse else dd
        ar, ai = _cmul(mr, mi, jnp.where(keep, pltpu.roll(xr, shift, 0), 0.0), jnp.where(keep, pltpu.roll(xi, shift, 0), 0.0))
        xr, xi = xr + ar, xi + ai
        mr, mi = _cmul(mr, mi, mr, mi)
    return xr, xi, mr, mi


def _scan_scratch(n_results):
    return ([pltpu.VMEM((W_BRANCH // 128, SSM_T, 128), F32)] * 2 + [pltpu.VMEM((SCAN_GROUPS, W_BRANCH), F32)] * 2
            + [pltpu.VMEM((SSM_T, W_BRANCH), F32)] * n_results)


def _block_scan(xr, xi, mr, mi, reverse, yr_ref, yi_ref, er_ref, ei_ref, hr_ref, hi_ref):
    cols = xr.shape[1]
    rows = lax.broadcasted_iota(jnp.int32, (SSM_T, cols), 0)
    yr, yi, m8r, m8i = _log_scan(xr, xi, mr, mi, rows & (SCAN_ROWS - 1), SCAN_ROWS, 3, reverse)
    lane_blocks = range(cols // 128)
    for c in lane_blocks:
        yr_ref[c] = yr[:, c * 128:(c + 1) * 128]
        yi_ref[c] = yi[:, c * 128:(c + 1) * 128]
    wide = lambda ref, rows_: jnp.concatenate([ref[c, rows_, :] for c in lane_blocks], axis=1)
    end = pl.ds(0 if reverse else SCAN_ROWS - 1, SCAN_GROUPS, stride=SCAN_ROWS)
    groups = lax.broadcasted_iota(jnp.int32, (SCAN_GROUPS, cols), 0)
    er, ei, _, _ = _log_scan(wide(yr_ref, end), wide(yi_ref, end), m8r, m8i, groups, SCAN_GROUPS,
                             int(math.log2(SCAN_GROUPS)), reverse)
    er_ref[...] = er
    ei_ref[...] = ei
    j = lax.broadcasted_iota(jnp.int32, (SCAN_ROWS, cols), 0)
    dist = SCAN_ROWS - j if reverse else j + 1
    tr, ti = jnp.ones((SCAN_ROWS, cols), F32), jnp.zeros((SCAN_ROWS, cols), F32)
    br, bi = mr, mi
    for bit in range(4):
        nr, ni = _cmul(tr, ti, br, bi)
        take = ((dist >> bit) & 1) == 1
        tr, ti = jnp.where(take, nr, tr), jnp.where(take, ni, ti)
        br, bi = _cmul(br, bi, br, bi)
    for g in range(SCAN_GROUPS):
        before = g + 1 if reverse else g - 1
        rows_g = slice(g * SCAN_ROWS, (g + 1) * SCAN_ROWS)
        if 0 <= before < SCAN_GROUPS:
            ar, ai = _cmul(tr, ti, er_ref[before:before + 1, :], ei_ref[before:before + 1, :])
            hr_ref[rows_g, :] = wide(yr_ref, rows_g) + ar
            hi_ref[rows_g, :] = wide(yi_ref, rows_g) + ai
        else:
            hr_ref[rows_g, :] = wide(yr_ref, rows_g)
            hi_ref[rows_g, :] = wide(yi_ref, rows_g)
    last = 0 if reverse else SCAN_GROUPS - 1
    return er_ref[last:last + 1, :], ei_ref[last:last + 1, :]


def s5_fwd(proj, b_re, b_im, a_re, a_im, c_re, c_im, d_skip):
    S = proj.shape[0]
    nt = S // SSM_T

    def body(u_ref, bre_ref, bim_ref, ar_ref, ai_ref, cre_ref, cim_ref, dsk_ref, hr_ref, hi_ref, y_ref, cr_ref, ci_ref,
             yr_ref, yi_ref, er_ref, ei_ref):
        t = pl.program_id(1)

        @pl.when(t == 0)
        def _():
            cr_ref[...] = jnp.zeros(cr_ref.shape, F32)
            ci_ref[...] = jnp.zeros(ci_ref.shape, F32)

        u = u_ref[...]
        ar, ai = ar_ref[...], ai_ref[...]
        rows = lax.broadcasted_iota(jnp.int32, (SSM_T, W_BRANCH), 0)
        inr, ini = _cmul(ar, ai, cr_ref[0:1, :], ci_ref[0:1, :])
        xr = _dot(u, bre_ref[...]) + jnp.where(rows == 0, inr, 0.0)
        xi = _dot(u, bim_ref[...]) + jnp.where(rows == 0, ini, 0.0)
        endr, endi = _block_scan(xr, xi, ar, ai, False, yr_ref, yi_ref, er_ref, ei_ref, hr_ref, hi_ref)
        cr_ref[...] = jnp.broadcast_to(endr, cr_ref.shape)
        ci_ref[...] = jnp.broadcast_to(endi, ci_ref.shape)
        xr, xi = hr_ref[...], hi_ref[...]
        y_ref[...] = _dot(xr, cre_ref[...]) - _dot(xi, cim_ref[...]) + u * dsk_ref[...]

    u_spec = pl.BlockSpec((SSM_T, 128), lambda j, t: (t, OFF_SSM // 128 + j))
    b_spec = pl.BlockSpec((None, 128, W_BRANCH), lambda j, t: (j, 0, 0))
    a_spec = pl.BlockSpec((1, W_BRANCH), lambda j, t: (0, j))
    c_spec = pl.BlockSpec((None, W_BRANCH, 128), lambda j, t: (j, 0, 0))
    h_spec = pl.BlockSpec((SSM_T, W_BRANCH), lambda j, t: (t, j))
    return pl.pallas_call(
        body, name="s5_fwd", grid=(4, nt),
        in_specs=[u_spec, b_spec, b_spec, a_spec, a_spec, c_spec, c_spec, pl.BlockSpec((1, 128), lambda j, t: (0, j))],
        out_specs=[h_spec, h_spec, pl.BlockSpec((SSM_T, 128), lambda j, t: (t, j))],
        out_shape=[jax.ShapeDtypeStruct((S, SSM_COLS), F32), jax.ShapeDtypeStruct((S, SSM_COLS), F32),
                   jax.ShapeDtypeStruct((S, W_BRANCH), F32)],
        scratch_shapes=[pltpu.VMEM((8, W_BRANCH), F32)] * 2 + _scan_scratch(0),
        compiler_params=_cp(2),
    )(proj, b_re, b_im, a_re, a_im, c_re, c_im, d_skip)


def s5_bwd(proj, hr, hi, dy, b_re, b_im, a_re, a_im, c_re, c_im, d_skip):
    S = proj.shape[0]
    nt = S // SSM_T

    def body(u_ref, hr_ref, hi_ref, hpr_ref, hpi_ref, dy_ref, bre_ref, bim_ref, ar_ref, ai_ref, cre_ref, cim_ref,
             dsk_ref, du_ref, dbre_ref, dbim_ref, dar_ref, dai_ref, dcre_ref, dcim_ref, ddsk_ref, gr_ref, gi_ref,
             yr_ref, yi_ref, er_ref, ei_ref, sr_ref, si_ref):
        step = pl.program_id(1)
        t = nt - 1 - step

        @pl.when(step == 0)
        def _():
            gr_ref[...] = jnp.zeros(gr_ref.shape, F32)
            gi_ref[...] = jnp.zeros(gi_ref.shape, F32)
            for ref in (dbre_ref, dbim_ref, dar_ref, dai_ref, dcre_ref, dcim_ref, ddsk_ref):
                ref[...] = jnp.zeros(ref.shape, F32)

        u = u_ref[...]
        dy = dy_ref[...]
        ar, ai = ar_ref[...], ai_ref[...]
        rows = lax.broadcasted_iota(jnp.int32, (SSM_T, W_BRANCH), 0)
        inr, ini = _cmul(ar, -ai, gr_ref[0:1, :], gi_ref[0:1, :])
        xr = _dot(dy, cre_ref[...], "nt") + jnp.where(rows == SSM_T - 1, inr, 0.0)
        xi = -_dot(dy, cim_ref[...], "nt") + jnp.where(rows == SSM_T - 1, ini, 0.0)
        endr, endi = _block_scan(xr, xi, ar, -ai, True, yr_ref, yi_ref, er_ref, ei_ref, sr_ref, si_ref)
        gr_ref[...] = jnp.broadcast_to(endr, gr_ref.shape)
        gi_ref[...] = jnp.broadcast_to(endi, gi_ref.shape)
        xr, xi = sr_ref[...], si_ref[...]
        hr_blk, hi_blk = hr_ref[...], hi_ref[...]
        keep = (t > 0).astype(F32)
        hpr = jnp.where(rows >= 1, pltpu.roll(hr_blk, 1, 0), hpr_ref[7:8, :] * keep)
        hpi = jnp.where(rows >= 1, pltpu.roll(hi_blk, 1, 0), hpi_ref[7:8, :] * keep)
        dar_ref[...] += jnp.sum(hpr * xr + hpi * xi, axis=0, keepdims=True)
        dai_ref[...] += jnp.sum(hpr * xi - hpi * xr, axis=0, keepdims=True)
        dcre_ref[...] += _dot(hr_blk, dy, "tn")
        dcim_ref[...] -= _dot(hi_blk, dy, "tn")
        du = dy * dsk_ref[...] + _dot(xr, bre_ref[...], "nt") + _dot(xi, bim_ref[...], "nt")
        du_ref[...] = du.astype(du_ref.dtype)
        dbre_ref[...] += _dot(u, xr, "tn")
        dbim_ref[...] += _dot(u, xi, "tn")
        ddsk_ref[...] += jnp.sum(dy * u, axis=0, keepdims=True)

    def rev(t):
        return nt - 1 - t

    u_spec = pl.BlockSpec((SSM_T, 128), lambda j, t: (rev(t), OFF_SSM // 128 + j))
    h_spec = pl.BlockSpec((SSM_T, W_BRANCH), lambda j, t: (rev(t), j))
    hprev_spec = pl.BlockSpec((8, W_BRANCH), lambda j, t: (jnp.maximum(rev(t) * (SSM_T // 8) - 1, 0), j))
    ch_spec = pl.BlockSpec((SSM_T, 128), lambda j, t: (rev(t), j))
    b_spec = pl.BlockSpec((None, 128, W_BRANCH), lambda j, t: (j, 0, 0))
    a_spec = pl.BlockSpec((1, W_BRANCH), lambda j, t: (0, j))
    c_spec = pl.BlockSpec((None, W_BRANCH, 128), lambda j, t: (j, 0, 0))
    d_spec = pl.BlockSpec((1, 128), lambda j, t: (0, j))
    return pl.pallas_call(
        body, name="s5_bwd", grid=(4, nt),
        in_specs=[u_spec, h_spec, h_spec, hprev_spec, hprev_spec, ch_spec, b_spec, b_spec, a_spec, a_spec,
                  c_spec, c_spec, d_spec],
        out_specs=[ch_spec, b_spec, b_spec, a_spec, a_spec, c_spec, c_spec, d_spec],
        out_shape=[jax.ShapeDtypeStruct((S, W_BRANCH), BF16),
                   jax.ShapeDtypeStruct((4, 128, W_BRANCH), F32), jax.ShapeDtypeStruct((4, 128, W_BRANCH), F32),
                   jax.ShapeDtypeStruct((1, SSM_COLS), F32), jax.ShapeDtypeStruct((1, SSM_COLS), F32),
                   jax.ShapeDtypeStruct((4, W_BRANCH, 128), F32), jax.ShapeDtypeStruct((4, W_BRANCH, 128), F32),
                   jax.ShapeDtypeStruct((1, W_BRANCH), F32)],
        scratch_shapes=[pltpu.VMEM((8, W_BRANCH), F32)] * 2 + _scan_scratch(2),
        compiler_params=_cp(2),
    )(proj, hr, hi, hr, hi, dy, b_re, b_im, a_re, a_im, c_re, c_im, d_skip)


def glu_fwd(y, w_glu, b_glu):
    S = y.shape[0]

    def body(y_ref, w_ref, b_ref, o_ref):
        g = _gelu(y_ref[...])
        o_ref[...] = (g * _sigmoid(_dot(g, w_ref[...]) + b_ref[...])).astype(o_ref.dtype)

    blk = pl.BlockSpec((ROW_TILE, W_BRANCH), lambda i: (i, 0))
    return pl.pallas_call(
        body, name="glu_fwd", grid=(S // ROW_TILE,),
        in_specs=[blk, pl.BlockSpec((W_BRANCH, W_BRANCH), lambda i: (0, 0)), pl.BlockSpec((1, W_BRANCH), lambda i: (0, 0))],
        out_specs=blk, out_shape=jax.ShapeDtypeStruct((S, W_BRANCH), BF16), compiler_params=_cp(1),
    )(y, w_glu, b_glu)


def glu_bwd(y, w_glu, b_glu, dout):
    S = y.shape[0]

    def body(y_ref, w_ref, b_ref, do_ref, dy_ref, dw_ref, db_ref):
        yv = y_ref[...]
        do = do_ref[...]
        g = _gelu(yv)
        s = _sigmoid(_dot(g, w_ref[...]) + b_ref[...])
        dz = do * g * s * (1.0 - s)
        dg = do * s + _dot(dz, w_ref[...], "nt")
        dy_ref[...] = dg * _gelu_grad(yv)
        dw = _dot(g, dz, "tn")
        db = jnp.sum(dz, axis=0, keepdims=True)

        @pl.when(pl.program_id(0) == 0)
        def _():
            dw_ref[...] = dw
            db_ref[...] = db

        @pl.when(pl.program_id(0) > 0)
        def _():
            dw_ref[...] += dw
            db_ref[...] += db

    blk = pl.BlockSpec((ROW_TILE, W_BRANCH), lambda i: (i, 0))
    mat = pl.BlockSpec((W_BRANCH, W_BRANCH), lambda i: (0, 0))
    vec = pl.BlockSpec((1, W_BRANCH), lambda i: (0, 0))
    return pl.pallas_call(
        body, name="glu_bwd", grid=(S // ROW_TILE,), in_specs=[blk, mat, vec, blk], out_specs=[blk, mat, vec],
        out_shape=[jax.ShapeDtypeStruct((S, W_BRANCH), F32), jax.ShapeDtypeStruct((W_BRANCH, W_BRANCH), F32),
                   jax.ShapeDtypeStruct((1, W_BRANCH), F32)],
        compiler_params=_cp(1),
    )(y, w_glu, b_glu, dout)


SGU_TILE = 512
SGU_U_BLOCK = OFF_SGU // W_BRANCH
SGU_V_BLOCK = SGU_U_BLOCK + 1


def _sgu_norm(zv):
    v = _gelu(zv)
    mu = jnp.mean(v, axis=-1, keepdims=True)
    vc = v - mu
    rstd = lax.rsqrt(jnp.mean(vc * vc, axis=-1, keepdims=True) + EPS)
    return vc * rstd, rstd


def _tril():
    return lax.broadcasted_iota(jnp.int32, (SGU_CHUNK, SGU_CHUNK), 0) >= lax.broadcasted_iota(jnp.int32, (SGU_CHUNK, SGU_CHUNK), 1)


def sgu_fwd(proj, ln_g, ln_b, w_s, b_s_t):
    S = proj.shape[0]

    def body(zu_ref, zv_ref, g_ref, b_ref, ws_ref, bs_ref, o_ref, vf_ref):
        vn, _ = _sgu_norm(zv_ref[...])
        vf_ref[...] = vn * g_ref[...] + b_ref[...]
        tri = _tril()
        for gi in range(4):
            ws = jnp.where(tri, ws_ref[gi], 0.0)
            cols = slice(gi * 128, (gi + 1) * 128)
            for c in range(SGU_TILE // SGU_CHUNK):
                rows = slice(c * SGU_CHUNK, (c + 1) * SGU_CHUNK)
                sv = _dot(ws, vf_ref[rows, cols]) + bs_ref[:, gi:gi + 1]
                o_ref[rows, cols] = (_gelu(zu_ref[rows, cols]) * sv).astype(o_ref.dtype)

    blk = lambda cb: pl.BlockSpec((SGU_TILE, W_BRANCH), lambda i: (i, cb))
    vec = pl.BlockSpec((1, W_BRANCH), lambda i: (0, 0))
    return pl.pallas_call(
        body, name="sgu_fwd", grid=(S // SGU_TILE,),
        in_specs=[blk(SGU_U_BLOCK), blk(SGU_V_BLOCK), vec, vec, pl.BlockSpec((4, SGU_CHUNK, SGU_CHUNK), lambda i: (0, 0, 0)),
                  pl.BlockSpec((SGU_CHUNK, 4), lambda i: (0, 0))],
        out_specs=blk(0), out_shape=jax.ShapeDtypeStruct((S, W_BRANCH), BF16),
        scratch_shapes=[pltpu.VMEM((SGU_TILE, W_BRANCH), F32)], compiler_params=_cp(1),
    )(proj, proj, ln_g, ln_b, w_s, b_s_t)


def sgu_bwd(proj, ln_g, ln_b, w_s, b_s_t, dout):
    S = proj.shape[0]

    def body(zu_ref, zv_ref, g_ref, b_ref, ws_ref, bs_ref, do_ref, dzu_ref, dzv_ref, dg_ref, db_ref, dws_ref, dbs_ref,
             vf_ref, dvf_ref):
        @pl.when(pl.program_id(0) == 0)
        def _():
            for ref in (dg_ref, db_ref, dws_ref, dbs_ref):
                ref[...] = jnp.zeros(ref.shape, F32)

        vn, rstd = _sgu_norm(zv_ref[...])
        vf_ref[...] = vn * g_ref[...] + b_ref[...]
        tri = _tril()
        lane = lax.broadcasted_iota(jnp.int32, (SGU_CHUNK, 128), 1)
        dbs = jnp.zeros((SGU_CHUNK, 128), F32)
        for gi in range(4):
            ws = jnp.where(tri, ws_ref[gi], 0.0)
            cols = slice(gi * 128, (gi + 1) * 128)
            dws = jnp.zeros((SGU_CHUNK, SGU_CHUNK), F32)
            for c in range(SGU_TILE // SGU_CHUNK):
                rows = slice(c * SGU_CHUNK, (c + 1) * SGU_CHUNK)
                vf = vf_ref[rows, cols]
                zu = zu_ref[rows, cols]
                do = do_ref[rows, cols]
                sv = _dot(ws, vf) + bs_ref[:, gi:gi + 1]
                dzu_ref[rows, cols] = (do * sv * _gelu_grad(zu)).astype(dzu_ref.dtype)
                dsv = do * _gelu(zu)
                dvf_ref[rows, cols] = _dot(ws, dsv, "tn")
                dws = dws + _dot(dsv, vf, "nt")
                dbs = dbs + jnp.where(lane == gi, jnp.sum(dsv, axis=-1, keepdims=True), 0.0)
            dws_ref[gi] += jnp.where(tri, dws, 0.0)
        dbs_ref[...] += dbs
        dvf = dvf_ref[...]
        dg_ref[...] += jnp.sum(dvf * vn, axis=0, keepdims=True)
        db_ref[...] += jnp.sum(dvf, axis=0, keepdims=True)
        dvn = dvf * g_ref[...]
        dv = rstd * (dvn - jnp.mean(dvn, axis=-1, keepdims=True) - vn * jnp.mean(dvn * vn, axis=-1, keepdims=True))
        dzv_ref[...] = (dv * _gelu_grad(zv_ref[...])).astype(dzv_ref.dtype)

    blk = lambda cb: pl.BlockSpec((SGU_TILE, W_BRANCH), lambda i: (i, cb))
    vec = pl.BlockSpec((1, W_BRANCH), lambda i: (0, 0))
    ws_spec = pl.BlockSpec((4, SGU_CHUNK, SGU_CHUNK), lambda i: (0, 0, 0))
    return pl.pallas_call(
        body, name="sgu_bwd", grid=(S // SGU_TILE,),
        in_specs=[blk(SGU_U_BLOCK), blk(SGU_V_BLOCK), vec, vec, ws_spec, pl.BlockSpec((SGU_CHUNK, 4), lambda i: (0, 0)),
                  blk(0)],
        out_specs=[blk(0), blk(0), vec, vec, ws_spec, pl.BlockSpec((SGU_CHUNK, 128), lambda i: (0, 0))],
        out_shape=[jax.ShapeDtypeStruct((S, W_BRANCH), BF16), jax.ShapeDtypeStruct((S, W_BRANCH), BF16),
                   jax.ShapeDtypeStruct((1, W_BRANCH), F32), jax.ShapeDtypeStruct((1, W_BRANCH), F32),
                   jax.ShapeDtypeStruct((4, SGU_CHUNK, SGU_CHUNK), F32), jax.ShapeDtypeStruct((SGU_CHUNK, 128), F32)],
        scratch_shapes=[pltpu.VMEM((SGU_TILE, W_BRANCH), F32), pltpu.VMEM((SGU_TILE, W_BRANCH), F32)],
        compiler_params=_cp(1),
    )(proj, proj, ln_g, ln_b, w_s, b_s_t, dout)


GM_TILE = 512


def _gate_specs(order):
    def spec(i):
        def index(*ids):
            m, n = order(*ids)
            return (m, (OFF_GATE + i * D_MODEL) // GM_TILE + n)
        return pl.BlockSpec((GM_TILE, GM_TILE), index)
    return [spec(i) for i in range(4)]


def merge_fwd(proj, gate_b, branches, w_up):
    S = proj.shape[0]
    order = lambda n, m: (m, n)

    def body(p0, p1, p2, p3, gb_ref, b0, b1, b2, b3, w_ref, o_ref):
        acc = jnp.zeros((GM_TILE, GM_TILE), F32)
        for i, (p_ref, br_ref) in enumerate(zip((p0, p1, p2, p3), (b0, b1, b2, b3))):
            acc = acc + _sigmoid(p_ref[...] + gb_ref[i:i + 1, :]) * _dot(br_ref[...], w_ref[i])
        o_ref[...] = acc.astype(o_ref.dtype)

    br_spec = pl.BlockSpec((GM_TILE, W_BRANCH), lambda n, m: (m, 0))
    return pl.pallas_call(
        body, name="merge_fwd", grid=(D_MODEL // GM_TILE, S // GM_TILE),
        in_specs=_gate_specs(order) + [pl.BlockSpec((4, GM_TILE), lambda n, m: (0, n))] + [br_spec] * 4
        + [pl.BlockSpec((4, W_BRANCH, GM_TILE), lambda n, m: (0, 0, n))],
        out_specs=pl.BlockSpec((GM_TILE, GM_TILE), lambda n, m: (m, n)),
        out_shape=jax.ShapeDtypeStruct((S, D_MODEL), BF16), compiler_params=_cp(2),
    )(proj, proj, proj, proj, gate_b, *branches, w_up)


def merge_bwd(proj, gate_b, branches, w_up, dmerged):
    S = proj.shape[0]
    order = lambda n, m: (m, n)

    def body(p0, p1, p2, p3, gb_ref, b0, b1, b2, b3, w_ref, dm_ref, dp0, dp1, dp2, dp3, du0, du1, du2, du3, dgb_ref):
        dm = dm_ref[...]
        dgb = []
        for i, (p_ref, br_ref, dp_ref, du_ref) in enumerate(
                zip((p0, p1, p2, p3), (b0, b1, b2, b3), (dp0, dp1, dp2, dp3), (du0, du1, du2, du3))):
            gate = _sigmoid(p_ref[...] + gb_ref[i:i + 1, :])
            dpre = dm * _dot(br_ref[...], w_ref[i]) * gate * (1.0 - gate)
            dp_ref[...] = dpre.astype(dp_ref.dtype)
            du_ref[...] = (dm * gate).astype(du_ref.dtype)
            dgb.append(jnp.sum(dpre, axis=0, keepdims=True))
        dgb = jnp.concatenate(dgb, axis=0)

        @pl.when(pl.program_id(1) == 0)
        def _():
            dgb_ref[...] = dgb

        @pl.when(pl.program_id(1) > 0)
        def _():
            dgb_ref[...] += dgb

    br_spec = pl.BlockSpec((GM_TILE, W_BRANCH), lambda n, m: (m, 0))
    mn = pl.BlockSpec((GM_TILE, GM_TILE), lambda n, m: (m, n))
    gb = pl.BlockSpec((4, GM_TILE), lambda n, m: (0, n))
    big = jax.ShapeDtypeStruct((S, D_MODEL), BF16)
    outs = pl.pallas_call(
        body, name="merge_bwd", grid=(D_MODEL // GM_TILE, S // GM_TILE),
        in_specs=_gate_specs(order) + [gb] + [br_spec] * 4
        + [pl.BlockSpec((4, W_BRANCH, GM_TILE), lambda n, m: (0, 0, n)), mn],
        out_specs=[mn] * 8 + [gb], out_shape=[big] * 8 + [jax.ShapeDtypeStruct((4, D_MODEL), F32)],
        compiler_params=_cp(2),
    )(proj, proj, proj, proj, gate_b, *branches, w_up, dmerged)
    return outs[0:4], outs[4:8], outs[8]


def _xatt_probs(q, k):
    s = _dot(q, k, "nt") * (X_HEAD_DIM ** -0.5)
    p = jnp.exp(s - jnp.max(s, axis=-1, keepdims=True))
    return p / jnp.sum(p, axis=-1, keepdims=True)


def xatt_fwd(q, kv):
    S = q.shape[0]

    def body(q_ref, kv_ref, o_ref):
        for h in range(X_HEADS):
            cols = slice(h * X_HEAD_DIM, (h + 1) * X_HEAD_DIM)
            p = _xatt_probs(q_ref[:, cols], kv_ref[:, cols])
            o_ref[:, cols] = _dot(p, kv_ref[:, W_BRANCH + h * X_HEAD_DIM:W_BRANCH + (h + 1) * X_HEAD_DIM]).astype(o_ref.dtype)

    blk = pl.BlockSpec((ROW_TILE, W_BRANCH), lambda i: (i, 0))
    return pl.pallas_call(
        body, name="xatt_fwd", grid=(S // ROW_TILE,),
        in_specs=[blk, pl.BlockSpec((N_MEM, 2 * W_BRANCH), lambda i: (0, 0))], out_specs=blk,
        out_shape=jax.ShapeDtypeStruct((S, W_BRANCH), BF16), compiler_params=_cp(1),
    )(q, kv)


def xatt_bwd(q, kv, do):
    S = q.shape[0]

    def body(q_ref, kv_ref, do_ref, dq_ref, dkv_ref):
        @pl.when(pl.program_id(0) == 0)
        def _():
            dkv_ref[...] = jnp.zeros(dkv_ref.shape, F32)

        for h in range(X_HEADS):
            cols = slice(h * X_HEAD_DIM, (h + 1) * X_HEAD_DIM)
            vcols = slice(W_BRANCH + h * X_HEAD_DIM, W_BRANCH + (h + 1) * X_HEAD_DIM)
            qh, kh, doh = q_ref[:, cols], kv_ref[:, cols], do_ref[:, cols]
            p = _xatt_probs(qh, kh)
            dp = _dot(doh, kv_ref[:, vcols], "nt")
            ds = p * (dp - jnp.sum(dp * p, axis=-1, keepdims=True)) * (X_HEAD_DIM ** -0.5)
            dq_ref[:, cols] = _dot(ds, kh).astype(dq_ref.dtype)
            dkv_ref[:, cols] += _dot(ds, qh, "tn")
            dkv_ref[:, vcols] += _dot(p, doh, "tn")

    blk = pl.BlockSpec((ROW_TILE, W_BRANCH), lambda i: (i, 0))
    kv_spec = pl.BlockSpec((N_MEM, 2 * W_BRANCH), lambda i: (0, 0))
    return pl.pallas_call(
        body, name="xatt_bwd", grid=(S // ROW_TILE,), in_specs=[blk, kv_spec, blk], out_specs=[blk, kv_spec],
        out_shape=[jax.ShapeDtypeStruct((S, W_BRANCH), BF16), jax.ShapeDtypeStruct((N_MEM, 2 * W_BRANCH), F32)],
        compiler_params=_cp(1),
    )(q, kv, do)


def s5_params(a_re, a_im, log_dt, b_re, b_im, c_re, c_im):
    lam_re = jnp.minimum(a_re, -1e-4)
    lam_im = a_im
    dt = jnp.exp(log_dt)[:, None]
    mag = jnp.exp(lam_re * dt)
    ab_re, ab_im = mag * jnp.cos(lam_im * dt), mag * jnp.sin(lam_im * dt)
    den = lam_re * lam_re + lam_im * lam_im
    f_re = ((ab_re - 1.0) * lam_re + ab_im * lam_im) / den
    f_im = (ab_im * lam_re - (ab_re - 1.0) * lam_im) / den
    bb_re = f_re[..., None] * b_re - f_im[..., None] * b_im
    bb_im = f_re[..., None] * b_im + f_im[..., None] * b_re
    eye = jnp.eye(8, dtype=F32)

    def b_blocks(bb):
        t = bb.reshape(4, 8, SSM_STATE, SSM_GROUP).transpose(0, 1, 3, 2)
        return (t[:, :, :, None, :] * eye[None, :, None, :, None]).reshape(4, 128, W_BRANCH)

    def c_blocks(cc):
        t = cc.reshape(4, 8, SSM_GROUP, SSM_STATE).transpose(0, 1, 3, 2)
        return (t[:, :, :, None, :] * eye[None, :, None, :, None]).reshape(4, W_BRANCH, 128)

    return (ab_re.reshape(1, SSM_COLS), ab_im.reshape(1, SSM_COLS), b_blocks(bb_re), b_blocks(bb_im),
            c_blocks(c_re), c_blocks(c_im))


ANY = pl.BlockSpec(memory_space=pl.ANY)


def _chip_index():
    return 2 * lax.axis_index("x") + lax.axis_index("y")


def _peer_chip(j):
    x, y, c = lax.axis_index("x"), lax.axis_index("y"), lax.axis_index("c")
    return ((1 - x) if j & 2 else x, (1 - y) if j & 1 else y, c)


def _piece(ref, axis, s, n):
    size = ref.shape[axis] // n
    idx = [slice(None)] * len(ref.shape)
    idx[axis] = pl.ds(s * size, size)
    return ref.at[tuple(idx)]


HBM_SPEC = pl.BlockSpec(memory_space=pltpu.HBM)
SEM_SPEC = pl.BlockSpec(memory_space=pltpu.SEMAPHORE)
SIDE_EFFECT = pltpu.SideEffectType.DATAFLOW_SIDE_EFFECTING


HALVING_MIN_ROWS = 32


def _rows_half(ref, c):
    rows = ref.shape[0] // 2
    return ref.at[pl.ds(c * rows, rows), :]


def _halved(ref):
    return ref.shape[0] >= HALVING_MIN_ROWS


def _chip_copies(ins, lands, send, recv, axes, mode, k, c, arriving):
    copies = []
    for t in range(len(ins)):
        for j in (1, 2, 3):
            sems = dict(send_sem=send.at[3 * t + j - 1], recv_sem=recv.at[3 * t + j - 1], device_id_type=MESH_ID)
            if mode == "scatter":
                src = ins[t] if axes[t] is None else _piece(ins[t], axes[t], k ^ j, 4)
                dst = lands[t].at[k ^ j if arriving else k]
                device = _peer_chip(j)
            elif mode == "gather":
                src, dst = ins[t], _piece(lands[t], axes[t], k ^ j if arriving else k, 4)
                if _halved(ins[t]):
                    src, dst = _rows_half(src, c), _rows_half(dst, c)
                device = _peer_chip(j)
            else:
                if not _halved(ins[t]):
                    continue
                theirs = _piece(lands[t], axes[t], k ^ j, 4)
                src, dst = _rows_half(theirs, c), _rows_half(theirs, 1 - c if arriving else c)
                device = (lax.axis_index("x"), lax.axis_index("y"), 1 - lax.axis_index("c"))
            copies.append(pltpu.make_async_remote_copy(src_ref=src, dst_ref=dst, device_id=device, **sems))
    return copies


def _own_copies(ins, lands, send, axes, mode, k):
    if mode != "gather":
        return []
    n = len(ins)
    return [pltpu.make_async_copy(ins[t], _piece(lands[t], axes[t], k, 4), send.at[3 * n + t]) for t in range(n)]


def chips_start(ins, lands, axes, mode, name, after=()):
    n, na = len(ins), len(after)

    def body(*refs):
        in_refs, land_refs = refs[:n], refs[n:2 * n]
        send, recv, token = refs[2 * n + na], refs[2 * n + na + 1], refs[-1]
        q, core = _chip_index(), lax.axis_index("c")
        for k in range(4):
            for c in range(2):
                @pl.when(jnp.logical_and(q == k, core == c))
                def _():
                    for copy in _chip_copies(in_refs, land_refs, send, recv, axes, mode, k, c, arriving=False):
                        copy.start()
                    for copy in _own_copies(in_refs, land_refs, send, axes, mode, k):
                        copy.start()
        token[...] = jnp.zeros(token.shape, token.dtype)

    hbm = lambda a: pltpu.HBM(a.shape, a.dtype)
    outs = pl.pallas_call(
        body, name=name, in_specs=[HBM_SPEC] * (2 * n) + [ANY] * na,
        out_specs=[SEM_SPEC, SEM_SPEC] + [HBM_SPEC] * (2 * n) + [pl.BlockSpec(memory_space=pltpu.VMEM)],
        out_shape=[pltpu.SemaphoreType.DMA((4 * n,)), pltpu.SemaphoreType.DMA((3 * n,))]
        + [hbm(a) for a in ins] + [hbm(a) for a in lands] + [jax.ShapeDtypeStruct((8, 128), F32)],
        input_output_aliases={i: 2 + i for i in range(2 * n)},
        compiler_params=pltpu.CompilerParams(has_side_effects=SIDE_EFFECT),
    )(*[pltpu.with_memory_space_constraint(a, pltpu.HBM) for a in list(ins) + list(lands)], *after)
    return outs[0], outs[1], outs[2:2 + n], outs[2 + n:2 + 2 * n], outs[-1]


def chips_wait(send, recv, ins, lands, axes, mode, name, after=()):
    n = len(ins)

    def body(*refs):
        in_refs, land_refs = refs[:n], refs[n:2 * n]
        send_ref, recv_ref = refs[2 * n], refs[2 * n + 1]
        q, core = _chip_index(), lax.axis_index("c")
        for k in range(4):
            for c in range(2):
                @pl.when(jnp.logical_and(q == k, core == c))
                def _():
                    for copy in _chip_copies(in_refs, land_refs, send_ref, recv_ref, axes, mode, k, c, arriving=True):
                        copy.wait_send()
                        copy.wait_recv()
                    for copy in _own_copies(in_refs, land_refs, send_ref, axes, mode, k):
                        copy.wait()

    hbm = lambda a: pltpu.HBM(a.shape, a.dtype)
    outs = pl.pallas_call(
        body, name=name, in_specs=[HBM_SPEC] * (2 * n) + [SEM_SPEC, SEM_SPEC] + [ANY] * len(after),
        out_specs=[HBM_SPEC] * (2 * n), out_shape=[hbm(a) for a in ins] + [hbm(a) for a in lands],
        input_output_aliases={i: i for i in range(2 * n)},
        compiler_params=pltpu.CompilerParams(has_side_effects=SIDE_EFFECT),
    )(*ins, *lands, send, recv, *after)
    return outs[:n], outs[n:]


def swap_cores(arrs, name):
    n = len(arrs)

    def body(*refs):
        ins, outs = refs[:n], refs[n:2 * n]
        send, recv = refs[2 * n:]
        sibling = (lax.axis_index("x"), lax.axis_index("y"), 1 - lax.axis_index("c"))
        copies = [pltpu.make_async_remote_copy(src_ref=ins[t], dst_ref=outs[t], send_sem=send.at[t], recv_sem=recv.at[t],
                                               device_id=sibling, device_id_type=MESH_ID) for t in range(n)]
        for cp in copies:
            cp.start()
        for cp in copies:
            cp.wait()

    return pl.pallas_call(
        body, name=name, in_specs=[ANY] * n, out_specs=[ANY] * n,
        out_shape=[jax.ShapeDtypeStruct(a.shape, a.dtype) for a in arrs],
        scratch_shapes=[pltpu.SemaphoreType.DMA((n,)), pltpu.SemaphoreType.DMA((n,))],
    )(*arrs)


ELEMENTWISE_BLOCK_BYTES = 1 << 20


def _row_tile(rows, cols):
    want = max(8, ELEMENTWISE_BLOCK_BYTES // (4 * 128 * -(-cols // 128)))
    fits = [t for t in range(8, min(rows, want) + 1, 8) if rows % t == 0]
    return fits[-1] if fits else rows


def sum_chips(recv, own, axis, chip, stacked, l, name):
    _, r, c = recv.shape
    tr = _row_tile(r, c)
    nrt = r // tr

    def body(chip_ref, r_ref, own_ref, stacked_ref, o_ref):
        for k in range(4):
            @pl.when(chip_ref[0] == k)
            def _():
                terms = [own_ref[...] if s == k else r_ref[s] for s in range(4)]
                o_ref[...] = ((terms[0] + terms[1]) + terms[2]) + terms[3]

    own_index = {0: lambda i, q: (q[0] * nrt + i, 0), 1: lambda i, q: (i, q[0]), None: lambda i, q: (i, 0)}[axis]
    return pl.pallas_call(
        body, name=name,
        grid_spec=pltpu.PrefetchScalarGridSpec(
            num_scalar_prefetch=1, grid=(nrt,),
            in_specs=[pl.BlockSpec((4, tr, c), lambda i, q: (0, i, 0)), pl.BlockSpec((tr, c), own_index), ANY],
            out_specs=pl.BlockSpec((None, tr, c), lambda i, q: (l, i, 0))),
        out_shape=jax.ShapeDtypeStruct(stacked.shape, F32), input_output_aliases={3: 0}, compiler_params=_cp(1),
    )(chip, recv, own, stacked)


def adamw(w, ga, gb, m, v, name):
    rows, cols = w.shape
    tr = _row_tile(rows, cols)

    def body(w_ref, ga_ref, gb_ref, m_ref, v_ref, g_ref, d_ref, nm_ref, nv_ref):
        g = ga_ref[...] + gb_ref[...]
        nm = ADAM_B1 * m_ref[...] + (1.0 - ADAM_B1) * g
        nv = ADAM_B2 * v_ref[...] + (1.0 - ADAM_B2) * (g * g)
        m_hat = nm / (1.0 - ADAM_B1 ** ADAM_STEP)
        v_hat = nv / (1.0 - ADAM_B2 ** ADAM_STEP)
        g_ref[...] = g
        nm_ref[...] = nm
        nv_ref[...] = nv
        d_ref[...] = -ADAM_LR * (m_hat / (jnp.sqrt(v_hat) + ADAM_EPS) + ADAM_WD * w_ref[...])

    blk = pl.BlockSpec((tr, cols), lambda i: (i, 0))
    f = jax.ShapeDtypeStruct((rows, cols), F32)
    return pl.pallas_call(
        body, name=name, grid=(rows // tr,), in_specs=[blk] * 5, out_specs=[blk] * 4, out_shape=[f] * 4,
        compiler_params=_cp(1),
    )(w, ga, gb, m, v)


PACK_ALIGN = 1024
PACK_ROWS_ALIGN = 2048


def pack_small(arrs):
    parts = []
    for a in arrs:
        flat = a.reshape(-1)
        pad = (-flat.shape[0]) % PACK_ALIGN
        parts.append(jnp.pad(flat, (0, pad)) if pad else flat)
    rows = sum(p.shape[0] for p in parts) // 128
    parts.append(jnp.zeros(((-rows) % PACK_ROWS_ALIGN * 128,), arrs[0].dtype))
    return jnp.concatenate(parts).reshape(-1, 128)


def unpack_small(packed, shapes):
    out, row = [], 0
    for shape in shapes:
        size = int(np.prod(shape))
        rows = -(-size // PACK_ALIGN) * 8
        out.append(packed[row:row + rows].reshape(-1)[:size].reshape(shape))
        row += rows
    return out


def _norm_epilogue(with_next):
    def epi(acc, res, g_post, *g_pre):
        x_new = acc * lax.rsqrt(jnp.mean(acc * acc, axis=-1, keepdims=True) + EPS) * g_post + res
        if not with_next:
            return acc, x_new
        return acc, x_new, x_new * lax.rsqrt(jnp.mean(x_new * x_new, axis=-1, keepdims=True) + EPS) * g_pre[0]
    return epi


def layer_fwd(x, h1, mem, w_in, rest_of, P, biases, g_next, after=()):
    sv = {"x0": x}
    post = dict(tm=512, tn=D_MODEL)
    proj = mm(h1, w_in, "nn", out_dtypes=[F32], name="mm_w_in", after=after)
    a_out = pool_fwd(proj, P["pool_w"], P["pool_scale"])
    os_, lses = [], []
    for g, (win, dil) in enumerate(DIL_GROUPS):
        o, lse = att_fwd(proj, biases[g], g, dil)
        os_.append(o)
        lses.append(lse)
    b_out, w0, w1, w2 = att_combine(os_, lses)
    s5p = P["s5"]
    hr, hi, y = s5_fwd(proj, s5p[2], s5p[3], s5p[0], s5p[1], s5p[4], s5p[5], P["d_skip"])
    d_out = sgu_fwd(proj, P["sgu_ln_g"], P["sgu_ln_b"], P["w_s"], P["b_s_t"])
    W, after_rest = rest_of("mixer", d_out)
    W = dict(W, w_in=w_in)
    c_out = glu_fwd(y, W["w_glu"], P["b_glu"])
    branches = (a_out, b_out, c_out, d_out)
    merged = merge_fwd(proj, W["gate_b"], branches, W["w_up"])
    t1, x1, h2 = mm(merged, W["w_out"], "nn", tk=1024, out_dtypes=[F32, F32, BF16], name="mm_w_out", extras=(x,),
                    vecs=(P["g_mix_post"], P["g_x_pre"]), epi=_norm_epilogue(True), after=after_rest, **post)
    sv.update(h1=h1, proj=proj, os=os_, lses=lses, wts=(w0, w1, w2), hr=hr, hi=hi, y=y, branches=branches,
              merged=merged, t1=t1, x1=x1)

    mem_n = rms_fwd(mem, P["g_mem"], BF16, "rms_mem")
    q = mm(h2, W["w_cq"], "nn", tm=1024, tn=512, tk=1024, out_dtypes=[BF16], name="mm_w_cq")
    kv = mm(mem_n, W["w_ckv"], "nn", tm=256, tn=1024, tk=1024, out_dtypes=[BF16], name="mm_w_ckv")
    ox = xatt_fwd(q, kv)
    t2, x2, h3 = mm(ox, W["w_co"], "nn", tk=512, out_dtypes=[F32, F32, BF16], name="mm_w_co", extras=(x1,),
                    vecs=(P["g_x_post"], P["g_ff_pre"]), epi=_norm_epilogue(True), **post)
    sv.update(h2=h2, mem_n=mem_n, q=q, kv=kv, ox=ox, t2=t2, x2=x2)

    W_ff, after_ff = rest_of("mlp", h3)
    W = dict(W, **W_ff)
    pre, act = mm(h3, W["w_ff1"], "nn", out_dtypes=[F32, BF16], name="mm_w_ff1",
                  epi=lambda acc: (acc, jnp.square(jnp.maximum(acc, 0.0))), after=after_ff)
    if g_next is None:
        (ff, x3), h_next = mm(act, W["w_ff2"], "nn", out_dtypes=[F32, F32], name="mm_w_ff2_last", extras=(x2,),
                              vecs=(P["g_ff_post"],), epi=_norm_epilogue(False)), None
    else:
        ff, x3, h_next = mm(act, W["w_ff2"], "nn", out_dtypes=[F32, F32, BF16], name="mm_w_ff2", extras=(x2,),
                            vecs=(P["g_ff_post"], g_next), epi=_norm_epilogue(True))
    sv.update(h3=h3, pre=pre, act=act, ff=ff, W=W)
    return x3, h_next, sv


def _pre_norm_bwd_epilogue(dh, x, add, g):
    r = lax.rsqrt(jnp.mean(x * x, axis=-1, keepdims=True) + EPS)
    xn = x * r
    dxn = dh * g
    return r * (dxn - xn * jnp.mean(dxn * xn, axis=-1, keepdims=True)) + add, jnp.sum(dh * xn, axis=0, keepdims=True)


def layer_bwd(dx, mem, W, P, biases, sv, headsum, emit, after=()):
    G = {}
    dff, G["g_ff_post"] = rms_bwd(sv["ff"], P["g_ff_post"], dx, BF16, "rms_post_bwd", after=after)
    G["w_ff2"] = mm(sv["act"], dff, "tn", out_dtypes=[F32], name="mm_dw_ff2")
    dpre = mm(dff, W["w_ff2"], "nt", out_dtypes=[BF16], name="mm_dact", extras=(sv["pre"],),
              epi=lambda acc, pre: (acc * (2.0 * jnp.maximum(pre, 0.0)),))
    G["w_ff1"] = mm(sv["h3"], dpre, "tn", out_dtypes=[F32], name="mm_dw_ff1")
    sent = emit(("w_ff1", "w_ff2"), G)
    pre_bwd = dict(out_dtypes=[F32], epi=_pre_norm_bwd_epilogue, n_sums=1)
    dx2, G["g_ff_pre"] = mm(dpre, W["w_ff1"], "nt", name="mm_dh3", extras=(sv["x2"], dx), vecs=(P["g_ff_pre"],),
                            after=sent, **pre_bwd)
    dt2, G["g_x_post"] = rms_bwd(sv["t2"], P["g_x_post"], dx2, BF16, "rms_post_bwd")
    G["w_co"] = mm(sv["ox"], dt2, "tn", tm=512, tn=1024, tk=1024, out_dtypes=[F32], name="mm_dw_co")
    dox = mm(dt2, W["w_co"], "nt", tm=1024, tn=512, tk=1024, out_dtypes=[BF16], name="mm_dox")
    dq, dkv = xatt_bwd(sv["q"], sv["kv"], dox)
    G["w_cq"] = mm(sv["h2"], dq, "tn", tm=1024, tn=512, tk=1024, out_dtypes=[F32], name="mm_dw_cq")
    G["w_ckv"] = mm(sv["mem_n"], dkv, "tn", tm=1024, tn=1024, tk=256, out_dtypes=[F32], name="mm_dw_ckv")
    dmem_n = mm(dkv, W["w_ckv"], "nt", tm=256, tn=1024, tk=1024, out_dtypes=[F32], name="mm_dmem")
    _, G["g_mem"] = rms_bwd(mem, P["g_mem"], dmem_n, BF16, "rms_mem_bwd")
    dx1, G["g_x_pre"] = mm(dq, W["w_cq"], "nt", name="mm_dh2", extras=(sv["x1"], dx2), vecs=(P["g_x_pre"],),
                           **pre_bwd)
    proj = sv["proj"]
    dt1, G["g_mix_post"] = rms_bwd(sv["t1"], P["g_mix_post"], dx1, BF16, "rms_post_bwd")
    G["w_out"] = mm(sv["merged"], dt1, "tn", tm=1024, tn=1024, tk=1024, out_dtypes=[F32], name="mm_dw_out")
    dmerged = mm(dt1, W["w_out"], "nt", tm=1024, tn=1024, tk=1024, out_dtypes=[F32], name="mm_dmerged")
    dgates, dups, G["gate_b"] = merge_bwd(proj, W["gate_b"], sv["branches"], W["w_up"], dmerged)
    dbr, dwup = [], []
    for i in range(4):
        dbr.append(mm(dups[i], W["w_up"][i], "nt", tm=1024, tn=512, tk=1024, out_dtypes=[F32], name="mm_dbranch"))
        dwup.append(mm(sv["branches"][i], dups[i], "tn", tm=512, tn=1024, tk=1024, out_dtypes=[F32], name="mm_dw_up"))
    G["w_up"] = jnp.concatenate(dwup, axis=0)
    d_pool, G["pool_w"], G["pool_scale"] = pool_bwd(proj, P["pool_w"], P["pool_scale"], dbr[0])
    cbar = att_combine_bwd(dbr[1], sv["os"], sv["wts"], headsum)
    dqs, dks, dvs, dbias = [], [], [], []
    for g, (win, dil) in enumerate(DIL_GROUPS):
        dq_g, dk_g, dv_g, db_g = att_bwd(proj, biases[g], sv["lses"][g], sv["wts"][g], dbr[1], cbar, g, dil)
        dqs.append(dq_g)
        dks.append(dk_g)
        dvs.append(dv_g)
        dbias.append(db_g)
    G["att_bias"] = dbias
    s5p = P["s5"]
    dy, G["w_glu"], G["b_glu"] = glu_bwd(sv["y"], W["w_glu"], P["b_glu"], dbr[2])
    d_ssm, dbre, dbim, dar, dai, dcre, dcim, G["d_skip"] = s5_bwd(
        proj, sv["hr"], sv["hi"], dy, s5p[2], s5p[3], s5p[0], s5p[1], s5p[4], s5p[5], P["d_skip"])
    G["s5"] = (dar, dai, dbre, dbim, dcre, dcim)
    dzu, dzv, G["sgu_ln_g"], G["sgu_ln_b"], G["w_s"], G["b_s_t"] = sgu_bwd(
        proj, P["sgu_ln_g"], P["sgu_ln_b"], P["w_s"], P["b_s_t"], dbr[3])
    d_qkv = [d.astype(BF16) for d in dqs + dks + dvs]
    dproj = jnp.concatenate([d_pool] + d_qkv + [d_ssm, dzu, dzv] + list(dgates), axis=1)
    sent = emit(("gate_b", "w_glu", "w_up", "w_out", "w_cq", "w_ckv", "w_co"), G)
    G["w_in"] = mm(sv["h1"], dproj, "tn", out_dtypes=[F32], name="mm_dw_in", after=sent)
    sent = emit(("w_in",), G)
    dx0, G["g_mix_pre"] = mm(dproj, W["w_in"], "nt", name="mm_dh1", extras=(sv["x0"], dx1), vecs=(P["g_mix_pre"],),
                             after=sent, **pre_bwd)
    return dx0, G


def _as3d(name, a):
    shape2d, axis = SHARDED[name]
    rows, cols = shape2d
    if axis == 0:
        rows //= 4
    else:
        cols //= 4
    return a.reshape(DEPTH, rows, cols)


def kernel(x, mem, rel_bias, g_mix_pre, g_mix_post, w_in, gate_b, pool_w, pool_scale, a_re, a_im, log_dt, b_re, b_im, c_re, c_im, d_skip, w_glu, b_glu, sgu_ln_g, sgu_ln_b, w_s, b_s, w_up, w_out, g_x_pre, g_x_post, g_mem, w_cq, w_ckv, w_co, g_ff_pre, g_ff_post, w_ff1, w_ff2, loss_target, m_rel_bias, m_g_mix_pre, m_g_mix_post, m_w_in, m_gate_b, m_pool_w, m_pool_scale, m_a_re, m_a_im, m_log_dt, m_b_re, m_b_im, m_c_re, m_c_im, m_d_skip, m_w_glu, m_b_glu, m_sgu_ln_g, m_sgu_ln_b, m_w_s, m_b_s, m_w_up, m_w_out, m_g_x_pre, m_g_x_post, m_g_mem, m_w_cq, m_w_ckv, m_w_co, m_g_ff_pre, m_g_ff_post, m_w_ff1, m_w_ff2, v_rel_bias, v_g_mix_pre, v_g_mix_post, v_w_in, v_gate_b, v_pool_w, v_pool_scale, v_a_re, v_a_im, v_log_dt, v_b_re, v_b_im, v_c_re, v_c_im, v_d_skip, v_w_glu, v_b_glu, v_sgu_ln_g, v_sgu_ln_b, v_w_s, v_b_s, v_w_up, v_w_out, v_g_x_pre, v_g_x_post, v_g_mem, v_w_cq, v_w_ckv, v_w_co, v_g_ff_pre, v_g_ff_post, v_w_ff1, v_w_ff2):
    env = dict(locals())
    weights = {n: env[n] for n in WEIGHT_NAMES}
    mom_m = {n: env["m_" + n] for n in WEIGHT_NAMES}
    mom_v = {n: env["v_" + n] for n in WEIGHT_NAMES}
    x2d = x.reshape(x.shape[1], D_MODEL)
    mem2d = mem.reshape(N_MEM, D_MODEL)
    target = loss_target.reshape(x2d.shape)

    axis_of = {n: SHARDED[n][1] for n in SHARDED_NAMES}
    chip = _chip_index().astype(jnp.int32).reshape(1)
    rest_names = [n for n in SHARDED_NAMES if n != "w_in"]

    def gather_start(l, names, tag, after=()):
        shards = [_as3d(n, weights[n])[l].astype(F32 if n == "gate_b" else MXU_DTYPE) for n in names]
        ax = [axis_of[n] for n in names]
        lands = [lax.empty(tuple(4 * d if i == a else d for i, d in enumerate(s.shape)), s.dtype)
                 for s, a in zip(shards, ax)]
        return (names, ax, tag) + chips_start(shards, lands, ax, "gather", f"gather_start_{tag}", after=after)

    def gather_wait(started, after):
        names, ax, tag, send, recv, shards, lands, _ = started
        shards, lands = chips_wait(send, recv, shards, lands, ax, "gather", f"gather_wait_{tag}", after=after)
        send, recv, shards, lands, _ = chips_start(shards, lands, ax, "forward", f"gather_forward_{tag}")
        _, lands = chips_wait(send, recv, shards, lands, ax, "forward", f"gather_landed_{tag}")
        W = dict(zip(names, lands))
        if "w_up" in W:
            W["w_up"] = W["w_up"].reshape(4, W_BRANCH, D_MODEL)
        return W

    biases = [att_bias(rel_bias, g, dil) for g, (_, dil) in enumerate(DIL_GROUPS)]
    lanes = np.arange(W_BRANCH) // ATT_HEAD_DIM
    headsum = jnp.asarray(lanes[:, None] == lanes[None, :], dtype=BF16)

    def small_params(l, s5_prepared):
        vec = lambda a: a[l].reshape(1, -1)
        return {
            "g_mix_pre": vec(g_mix_pre), "g_mix_post": vec(g_mix_post), "g_x_pre": vec(g_x_pre), "g_x_post": vec(g_x_post),
            "g_mem": vec(g_mem), "g_ff_pre": vec(g_ff_pre), "g_ff_post": vec(g_ff_post), "pool_w": pool_w[l],
            "pool_scale": vec(pool_scale), "d_skip": vec(d_skip), "b_glu": vec(b_glu), "sgu_ln_g": vec(sgu_ln_g),
            "sgu_ln_b": vec(sgu_ln_b), "w_s": w_s[l], "b_s_t": b_s[l].T, "s5": s5_prepared,
        }

    Ws, Ps, saved, s5_vjps = [], [], [], []
    xl = x2d
    hl = rms_fwd(x2d, g_mix_pre[0].reshape(1, -1), BF16, "rms_pre")
    flying = {"next": gather_start(0, ["w_in"], "0_w_in")}
    for l in range(DEPTH):
        s5_prepared, s5_vjp = jax.vjp(s5_params, a_re[l], a_im[l], log_dt[l], b_re[l], b_im[l], c_re[l], c_im[l])
        token_of = lambda started: (started[7],)
        if l == 0:
            w_in_l = gather_wait(flying["next"], [*biases, hl])["w_in"]
            flying["rest"] = gather_start(0, rest_names, "0_rest", after=[w_in_l])
            first_after = token_of(flying["rest"])

            def rest_of(stage, value):
                if stage != "mixer":
                    return {}, ()
                W = gather_wait(flying["rest"], [value])
                flying["next"] = gather_start(1, SHARDED_NAMES, "1", after=[W["w_out"]])
                return W, token_of(flying["next"])
        else:
            W_l = gather_wait(flying["next"], [xl])
            w_in_l, first_after = W_l["w_in"], ()
            if l + 1 < DEPTH:
                flying["next"] = gather_start(l + 1, SHARDED_NAMES, str(l + 1), after=[w_in_l])
                first_after = token_of(flying["next"])
            rest_of = lambda stage, value, W_l=W_l: (W_l if stage == "mixer" else {}, ())
        P = small_params(l, s5_prepared)
        g_next = g_mix_pre[l + 1].reshape(1, -1) if l + 1 < DEPTH else None
        xl, hl, sv = layer_fwd(xl, hl, mem2d, w_in_l, rest_of, P, biases, g_next, after=first_after)
        Ws.append(sv["W"])
        Ps.append(P)
        saved.append(sv)
        s5_vjps.append(s5_vjp)
    loss_local, dx = loss_and_grad(xl, target)
    loss = lax.psum(loss_local, ("x", "y", "c"))

    scattered = []

    def scatter_start(l, names, srcs):
        ax = [axis_of.get(n) for n in names]
        lands = []
        for s, a in zip(srcs, ax):
            r, c = s.shape
            lands.append(lax.empty((4, r // 4 if a == 0 else r, c // 4 if a == 1 else c), F32))
        tag = f"{l}_{names[0]}"
        send, recv, srcs, lands, token = chips_start(srcs, lands, ax, "scatter", f"grads_start_{tag}")
        scattered.append((l, names, ax, tag, send, recv, srcs, lands))
        return (token,)

    grads = [None] * DEPTH
    for l in reversed(range(DEPTH)):
        emit = lambda names, G, l=l: scatter_start(l, list(names), [G[n] for n in names])
        dx, grads[l] = layer_bwd(dx, mem2d, Ws[l], Ps[l], biases, saved[l], headsum, emit)
    grad_x = dx.reshape(x.shape)

    rep = {}
    stack = lambda key, shape: jnp.stack([grads[l][key] for l in range(DEPTH)]).reshape(shape)
    for n in ("g_mix_pre", "g_mix_post", "g_x_pre", "g_x_post", "g_mem", "g_ff_pre", "g_ff_post"):
        rep[n] = stack(n, (DEPTH, D_MODEL))
    for n in ("pool_scale", "d_skip", "b_glu", "sgu_ln_g", "sgu_ln_b"):
        rep[n] = stack(n, (DEPTH, W_BRANCH))
    rep["pool_w"] = stack("pool_w", pool_w.shape)
    rep["w_s"] = stack("w_s", w_s.shape)
    rep["b_s"] = jnp.stack([grads[l]["b_s_t"][:, :4].T for l in range(DEPTH)])
    s5_grads = [s5_vjps[l](tuple(grads[l]["s5"])) for l in range(DEPTH)]
    for i, n in enumerate(("a_re", "a_im", "log_dt", "b_re", "b_im", "c_re", "c_im")):
        rep[n] = jnp.stack([s5_grads[l][i] for l in range(DEPTH)])
    dbias = [sum(grads[l]["att_bias"][g] for l in range(DEPTH)) for g in range(len(DIL_GROUPS))]
    rep["rel_bias"] = jnp.concatenate([att_bias_grad(dbias[g], dil) for g, (_, dil) in enumerate(DIL_GROUPS)], axis=1)
    rep_shapes = [weights[n].shape for n in REPLICATED_NAMES]
    packed_g = pack_small([rep[n] for n in REPLICATED_NAMES])

    small_sent = scatter_start(0, ["small"], [packed_g])
    stacked = {}

    def collect(record, after):
        l, names, ax, tag, send, recv, srcs, lands = record
        srcs, lands = chips_wait(send, recv, srcs, lands, ax, "scatter", f"grads_wait_{tag}", after=after)
        for n, own, arrived, a in zip(names, srcs, lands, ax):
            if n not in stacked:
                stacked[n] = lax.empty((1 if n == "small" else DEPTH,) + arrived.shape[1:], F32)
            stacked[n] = sum_chips(arrived, own, a, chip, stacked[n], 0 if n == "small" else l, "sum_chips")

    out_g, out_d, out_m, out_v = {}, {}, {}, {}

    def update(names, tag):
        partial = [stacked[n].reshape(-1, stacked[n].shape[-1]) for n in names]
        other = swap_cores(partial, f"swap_cores_{tag}")
        for n, mine, theirs in zip(names, partial, other):
            if n == "small":
                for name, ga, gb in zip(REPLICATED_NAMES, unpack_small(mine, rep_shapes), unpack_small(theirs, rep_shapes)):
                    rows_of = lambda a: a.reshape(-1, a.shape[-1])
                    res = adamw(rows_of(weights[name]), rows_of(ga), rows_of(gb), rows_of(mom_m[name]),
                                rows_of(mom_v[name]), "adamw_small")
                    out_g[name], out_d[name], out_m[name], out_v[name] = [r.reshape(weights[name].shape) for r in res]
            else:
                flat = lambda a: a.reshape(mine.shape)
                res = adamw(flat(weights[n]), mine, theirs, flat(mom_m[n]), flat(mom_v[n]), "adamw")
                out_g[n], out_d[n], out_m[n], out_v[n] = [r.reshape(weights[n].shape) for r in res]

    late = [r for r in scattered if r[1] == ["small"] or (r[0] == 0 and r[1] == ["w_in"])]
    for record in scattered:
        if not any(record is r for r in late):
            collect(record, [dx, *small_sent])
    update(rest_names, "rest")
    collect(late[0], [out_d[n] for n in rest_names])
    update(["w_in"], "w_in")
    collect(late[1], [out_d["w_in"]])
    update(["small"], "small")

    return (loss, grad_x, *[out_g[n] for n in WEIGHT_NAMES], *[out_d[n] for n in WEIGHT_NAMES],
            *[out_m[n] for n in WEIGHT_NAMES], *[out_v[n] for n in WEIGHT_NAMES])
```

```python
import functools
import math

import numpy as np
import jax
import jax.numpy as jnp
from jax import lax
from jax.experimental import pallas as pl
from jax.experimental.pallas import tpu as pltpu

F32 = jnp.float32
BF16 = jnp.bfloat16
MXU_DTYPE = jnp.bfloat16
MESH_ID = pl.DeviceIdType.MESH
VMEM_LIMIT_BYTES = 56 * 1024 * 1024

D_MODEL = 1024
DEPTH = 4
N_MEM = 256
W_BRANCH = 512
POOL_WINDOWS = (2, 4, 8, 16)
POOL_HALO = 16
DIL_GROUPS = ((128, 1), (512, 4), (2048, 16))
BAND = 128
ATT_HEADS = 8
ATT_HEAD_DIM = 64
SSM_GROUP = 16
SSM_GROUPS = 32
SSM_STATE = 64
SSM_COLS = SSM_GROUPS * SSM_STATE
SSM_T = 512
SGU_CHUNK = 128
X_HEADS = 4
X_HEAD_DIM = 128
D_FF = 4096
REL_BUCKETS = 32
REL_MAX_DIST = 2048
EPS = 1e-6
NEG_INF = -1e30
OFF_POOL = 0
OFF_ATT = 512
OFF_SSM = OFF_ATT + 9 * W_BRANCH
OFF_SGU = OFF_SSM + W_BRANCH
OFF_GATE = OFF_SGU + 2 * W_BRANCH
IN_WIDTH = OFF_GATE + 4 * D_MODEL

ADAM_LR = 0.001
ADAM_B1 = 0.9
ADAM_B2 = 0.999
ADAM_EPS = 1e-08
ADAM_WD = 0.01
ADAM_STEP = 10

GELU_C = math.sqrt(2.0 / math.pi)

WEIGHT_NAMES = ['rel_bias', 'g_mix_pre', 'g_mix_post', 'w_in', 'gate_b', 'pool_w', 'pool_scale', 'a_re', 'a_im',
                'log_dt', 'b_re', 'b_im', 'c_re', 'c_im', 'd_skip', 'w_glu', 'b_glu', 'sgu_ln_g', 'sgu_ln_b',
                'w_s', 'b_s', 'w_up', 'w_out', 'g_x_pre', 'g_x_post', 'g_mem', 'w_cq', 'w_ckv', 'w_co',
                'g_ff_pre', 'g_ff_post', 'w_ff1', 'w_ff2']
SHARDED = {
    'w_in': ((D_MODEL, IN_WIDTH), 1),
    'gate_b': ((4, D_MODEL), 1),
    'w_glu': ((W_BRANCH, W_BRANCH), 0),
    'w_up': ((4 * W_BRANCH, D_MODEL), 1),
    'w_out': ((D_MODEL, D_MODEL), 0),
    'w_cq': ((D_MODEL, W_BRANCH), 0),
    'w_ckv': ((D_MODEL, D_MODEL), 0),
    'w_co': ((W_BRANCH, D_MODEL), 1),
    'w_ff1': ((D_MODEL, D_FF), 1),
    'w_ff2': ((D_FF, D_MODEL), 0),
}
SHARDED_NAMES = list(SHARDED)
REPLICATED_NAMES = [n for n in WEIGHT_NAMES if n not in SHARDED]


def _cp(n_axes):
    return pltpu.CompilerParams(dimension_semantics=("arbitrary",) * n_axes, vmem_limit_bytes=VMEM_LIMIT_BYTES)


def _dot(a, b, dims="nn"):
    cd = {"nn": ((1,), (0,)), "nt": ((1,), (1,)), "tn": ((0,), (0,))}[dims]
    return lax.dot_general(a.astype(MXU_DTYPE), b.astype(MXU_DTYPE), (cd, ((), ())), preferred_element_type=F32)


def _gelu(x):
    return 0.5 * x * (1.0 + jnp.tanh(GELU_C * (x + 0.044715 * (x * x * x))))


def _gelu_grad(x):
    t = jnp.tanh(GELU_C * (x + 0.044715 * (x * x * x)))
    return 0.5 * (1.0 + t) + 0.5 * x * (1.0 - t * t) * (GELU_C * (1.0 + 3.0 * 0.044715 * (x * x)))


def _sigmoid(x):
    return 1.0 / (1.0 + jnp.exp(-x))


MM_TILES = {
    "mm_w_in": (2048, 1536, 1024), "mm_dw_in": (1024, 1536, 2048), "mm_dh1": (1024, 1024, 1536),
    "mm_w_ff1": (2048, 1024, 1024), "mm_w_ff2": (1024, 1024, 2048), "mm_w_ff2_last": (1024, 1024, 2048),
    "mm_dw_ff2": (1024, 1024, 2048), "mm_dact": (2048, 1024, 1024), "mm_dw_ff1": (1024, 1024, 2048),
    "mm_dh3": (1024, 1024, 2048), "mm_dh2": (1024, 1024, 512),
}


def mm(a, b, dims, *, out_dtypes, name, tm=None, tn=None, tk=None, extras=(), vecs=(), epi=None, n_sums=0, after=()):
    if dims == "tn":
        K, M = a.shape
        N = b.shape[1]
    else:
        M, K = a.shape
        N = b.shape[1] if dims == "nn" else b.shape[0]
    if tm is None:
        tm, tn, tk = MM_TILES[name]
    tm, tn, tk = min(tm, M), min(tn, N), min(tk, K)
    assert M % tm == 0 and N % tn == 0 and K % tk == 0, (name, M, N, K, tm, tn, tk)
    assert n_sums == 0 or tn == N, name
    nk = K // tk
    ne, no = len(extras) + len(vecs), len(out_dtypes)
    if epi is None:
        epi = lambda acc: (acc,)
    a_spec = (pl.BlockSpec((tk, tm), lambda i, j, k: (k, i)) if dims == "tn"
              else pl.BlockSpec((tm, tk), lambda i, j, k: (i, k)))
    b_spec = (pl.BlockSpec((tn, tk), lambda i, j, k: (j, k)) if dims == "nt"
              else pl.BlockSpec((tk, tn), lambda i, j, k: (k, j)))
    mn_spec = pl.BlockSpec((tm, tn), lambda i, j, k: (i, j))
    vec_spec = pl.BlockSpec((1, tn), lambda i, j, k: (0, j))

    def body(a_ref, b_ref, *rest):
        first_out = ne + len(after)
        extra_refs, out_refs = rest[:ne], rest[first_out:first_out + no]
        sum_refs = rest[first_out + no:first_out + no + n_sums]
        part = _dot(a_ref[...], b_ref[...], dims)

        def finish(acc):
            results = epi(acc, *[e[...] for e in extra_refs])
            for o_ref, r in zip(out_refs, results[:no]):
                o_ref[...] = r.astype(o_ref.dtype)
            for s_ref, r in zip(sum_refs, results[no:]):
                @pl.when(pl.program_id(0) == 0)
                def _():
                    s_ref[...] = r

                @pl.when(pl.program_id(0) > 0)
                def _():
                    s_ref[...] += r

        if nk == 1:
            finish(part)
        else:
            acc_ref = rest[-1]
            k = pl.program_id(2)

            @pl.when(k == 0)
            def _():
                acc_ref[...] = part

            @pl.when(k > 0)
            def _():
                acc_ref[...] += part

            @pl.when(k == nk - 1)
            def _():
                finish(acc_ref[...])

    outs = pl.pallas_call(
        body, name=name, grid=(M // tm, N // tn, nk),
        in_specs=[a_spec, b_spec] + [mn_spec] * len(extras) + [vec_spec] * len(vecs) + [ANY] * len(after),
        out_specs=[mn_spec] * no + [vec_spec] * n_sums,
        out_shape=[jax.ShapeDtypeStruct((M, N), dt) for dt in out_dtypes] + [jax.ShapeDtypeStruct((1, N), F32)] * n_sums,
        scratch_shapes=[pltpu.VMEM((tm, tn), F32)] if nk > 1 else [],
        compiler_params=_cp(3),
    )(a, b, *extras, *vecs, *after)
    return outs[0] if no + n_sums == 1 else outs


ROW_TILE = 512


def rms_fwd(x, g, out_dtype, name, res=None):
    M, D = x.shape
    tm = min(ROW_TILE, M)

    def body(x_ref, g_ref, *rest):
        o_ref = rest[-1]
        xf = x_ref[...]
        y = xf * lax.rsqrt(jnp.mean(xf * xf, axis=-1, keepdims=True) + EPS) * g_ref[...]
        if res is not None:
            y = y + rest[0][...]
        o_ref[...] = y.astype(o_ref.dtype)

    row = pl.BlockSpec((tm, D), lambda i: (i, 0))
    return pl.pallas_call(
        body, name=name, grid=(M // tm,),
        in_specs=[row, pl.BlockSpec((1, D), lambda i: (0, 0))] + ([row] if res is not None else []),
        out_specs=row, out_shape=jax.ShapeDtypeStruct((M, D), out_dtype), compiler_params=_cp(1),
    )(x, g, *([res] if res is not None else []))


def rms_bwd(x, g, dy, dx_dtype, name, add=None, after=()):
    M, D = x.shape
    tm = min(ROW_TILE, M)

    def body(x_ref, g_ref, dy_ref, *rest):
        dx_ref, dg_ref = rest[-2], rest[-1]
        xf = x_ref[...]
        dyf = dy_ref[...].astype(F32)
        r = lax.rsqrt(jnp.mean(xf * xf, axis=-1, keepdims=True) + EPS)
        xn = xf * r
        dxn = dyf * g_ref[...]
        dx = r * (dxn - xn * jnp.mean(dxn * xn, axis=-1, keepdims=True))
        if add is not None:
            dx = dx + rest[0][...]
        dx_ref[...] = dx.astype(dx_ref.dtype)
        dg = jnp.sum(dyf * xn, axis=0, keepdims=True)

        @pl.when(pl.program_id(0) == 0)
        def _():
            dg_ref[...] = dg

        @pl.when(pl.program_id(0) > 0)
        def _():
            dg_ref[...] += dg

    row = pl.BlockSpec((tm, D), lambda i: (i, 0))
    vec = pl.BlockSpec((1, D), lambda i: (0, 0))
    return pl.pallas_call(
        body, name=name, grid=(M // tm,),
        in_specs=[row, vec, row] + ([row] if add is not None else []) + [ANY] * len(after),
        out_specs=[row, vec],
        out_shape=[jax.ShapeDtypeStruct((M, D), dx_dtype), jax.ShapeDtypeStruct((1, D), F32)],
        compiler_params=_cp(1),
    )(x, g, dy, *([add] if add is not None else []), *after)


def loss_and_grad(y, target):
    M, D = y.shape
    tm = ROW_TILE

    def body(y_ref, t_ref, part_ref, dy_ref):
        e = y_ref[...] - t_ref[...]
        dy_ref[...] = e / D
        part_ref[...] = jnp.broadcast_to(0.5 * jnp.sum(jnp.mean(e * e, axis=-1, keepdims=True), axis=0, keepdims=True),
                                         (8, 128))

    row = pl.BlockSpec((tm, D), lambda i: (i, 0))
    part, dy = pl.pallas_call(
        body, name="loss", grid=(M // tm,), in_specs=[row, row],
        out_specs=[pl.BlockSpec((8, 128), lambda i: (i, 0)), row],
        out_shape=[jax.ShapeDtypeStruct((8 * (M // tm), 128), F32), jax.ShapeDtypeStruct((M, D), F32)],
        compiler_params=_cp(1),
    )(y, target)
    return jnp.sum(part[::8, 0]), dy


POOL_ROWS = 512


def _pool_window_sum(xw, gi, roll_of):
    s1 = xw + pltpu.roll(xw, roll_of(1), 0)
    s2 = s1 + pltpu.roll(s1, roll_of(2), 0)
    s3 = s2 + pltpu.roll(s2, roll_of(4), 0)
    s4 = s3 + pltpu.roll(s3, roll_of(8), 0)
    return jnp.where(gi == 0, s1, jnp.where(gi == 1, s2, jnp.where(gi == 2, s3, s4)))


def _pool_cnt(i, gi):
    rows = lax.broadcasted_iota(jnp.int32, (POOL_ROWS, 128), 0) + i * POOL_ROWS
    w = jnp.where(gi == 0, 2, jnp.where(gi == 1, 4, jnp.where(gi == 2, 8, 16)))
    return jnp.minimum(rows + 1, w).astype(F32)


def pool_fwd(proj, pool_w, scale):
    S = proj.shape[0]
    nchunk = S // POOL_ROWS
    slab = POOL_ROWS + POOL_HALO

    def body(x_ref, w_ref, sc_ref, o_ref, pad_ref):
        gi = pl.program_id(0)
        pad_ref[0:POOL_HALO, :] = jnp.zeros((POOL_HALO, 128), F32)
        pad_ref[POOL_HALO:, :] = x_ref[...]
        for i in range(nchunk):
            xw = pad_ref[i * POOL_ROWS:i * POOL_ROWS + slab, :]
            ssum = _pool_window_sum(xw, gi, lambda d: d)[POOL_HALO:, :]
            p = ssum / _pool_cnt(i, gi) - xw[POOL_HALO:, :]
            o_ref[i * POOL_ROWS:(i + 1) * POOL_ROWS, :] = (_dot(p, w_ref[...]) * sc_ref[...]).astype(o_ref.dtype)

    return pl.pallas_call(
        body, name="pool_fwd", grid=(4,),
        in_specs=[pl.BlockSpec((S, 128), lambda g: (0, OFF_POOL // 128 + g)),
                  pl.BlockSpec((None, 128, 128), lambda g: (g, 0, 0)),
                  pl.BlockSpec((1, 128), lambda g: (0, g))],
        out_specs=pl.BlockSpec((S, 128), lambda g: (0, g)),
        out_shape=jax.ShapeDtypeStruct((S, W_BRANCH), BF16),
        scratch_shapes=[pltpu.VMEM((S + POOL_HALO, 128), F32)],
        compiler_params=_cp(1),
    )(proj, pool_w, scale)


def pool_bwd(proj, pool_w, scale, dy):
    S = proj.shape[0]
    nchunk = S // POOL_ROWS
    slab = POOL_ROWS + POOL_HALO

    def body(x_ref, w_ref, sc_ref, dy_ref, dx_ref, dw_ref, dsc_ref, pad_ref, pad2_ref, dp_ref):
        gi = pl.program_id(0)
        pad_ref[0:POOL_HALO, :] = jnp.zeros((POOL_HALO, 128), F32)
        pad_ref[POOL_HALO:, :] = x_ref[...]
        pad2_ref[S:, :] = jnp.zeros((POOL_HALO, 128), F32)
        dw = jnp.zeros((128, 128), F32)
        dsc = jnp.zeros((1, 128), F32)
        for i in range(nchunk):
            xw = pad_ref[i * POOL_ROWS:i * POOL_ROWS + slab, :]
            cnt = _pool_cnt(i, gi)
            p = _pool_window_sum(xw, gi, lambda d: d)[POOL_HALO:, :] / cnt - xw[POOL_HALO:, :]
            dyc = dy_ref[i * POOL_ROWS:(i + 1) * POOL_ROWS, :]
            dsc = dsc + jnp.sum(dyc * _dot(p, w_ref[...]), axis=0, keepdims=True)
            dys = dyc * sc_ref[...]
            dw = dw + _dot(p, dys, "tn")
            dp = _dot(dys, w_ref[...], "nt")
            dp_ref[i * POOL_ROWS:(i + 1) * POOL_ROWS, :] = dp
            pad2_ref[i * POOL_ROWS:(i + 1) * POOL_ROWS, :] = dp / cnt
        dw_ref[...] = dw
        dsc_ref[...] = dsc
        for i in range(nchunk):
            xw = pad2_ref[i * POOL_ROWS:i * POOL_ROWS + slab, :]
            fsum = _pool_window_sum(xw, gi, lambda d: slab - d)[:POOL_ROWS, :]
            rows = slice(i * POOL_ROWS, (i + 1) * POOL_ROWS)
            dx_ref[rows, :] = (fsum - dp_ref[rows, :]).astype(dx_ref.dtype)

    return pl.pallas_call(
        body, name="pool_bwd", grid=(4,),
        in_specs=[pl.BlockSpec((S, 128), lambda g: (0, OFF_POOL // 128 + g)),
                  pl.BlockSpec((None, 128, 128), lambda g: (g, 0, 0)),
                  pl.BlockSpec((1, 128), lambda g: (0, g)),
                  pl.BlockSpec((S, 128), lambda g: (0, g))],
        out_specs=[pl.BlockSpec((S, 128), lambda g: (0, g)),
                   pl.BlockSpec((None, 128, 128), lambda g: (g, 0, 0)),
                   pl.BlockSpec((1, 128), lambda g: (0, g))],
        out_shape=[jax.ShapeDtypeStruct((S, W_BRANCH), BF16), jax.ShapeDtypeStruct((4, 128, 128), F32),
                   jax.ShapeDtypeStruct((1, W_BRANCH), F32)],
        scratch_shapes=[pltpu.VMEM((S + POOL_HALO, 128), F32), pltpu.VMEM((S + POOL_HALO, 128), F32),
                        pltpu.VMEM((S, 128), F32)],
        compiler_params=_cp(1),
    )(proj, pool_w, scale, dy)


def _t5_bucket(n):
    exact = REL_BUCKETS // 2
    nf = np.maximum(n, 1).astype(np.float32)
    large = exact + (np.log(nf / exact) / np.log(REL_MAX_DIST / exact) * (REL_BUCKETS - exact)).astype(np.int32)
    large = np.minimum(large, REL_BUCKETS - 1)
    return np.where(n < exact, n, large).astype(np.int32)


def _band_onehot(dil):
    i = np.arange(BAND)[:, None]
    kk = np.arange(2 * BAND)[None, :]
    dist = BAND + i - kk
    local = (dist >= 0) & (dist <= BAND)
    bucket = _t5_bucket(np.clip(dist, 0, BAND) * dil)
    onehot = (bucket.reshape(-1, 1) == np.arange(REL_BUCKETS)[None, :]).astype(np.float32)
    return onehot, local


def att_bias(rel_bias, g, dil):
    onehot, local = _band_onehot(dil)
    tab = jnp.dot(jnp.asarray(onehot), rel_bias[:, g * ATT_HEADS:(g + 1) * ATT_HEADS], precision=lax.Precision.HIGHEST)
    bias = tab.reshape(BAND, 2 * BAND, ATT_HEADS).transpose(2, 0, 1)
    return jnp.where(jnp.asarray(local)[None], bias, NEG_INF)


def att_bias_grad(dbias, dil):
    onehot, _ = _band_onehot(dil)
    flat = dbias.transpose(1, 2, 0).reshape(BAND * 2 * BAND, ATT_HEADS)
    return jnp.dot(jnp.asarray(onehot).T, flat, precision=lax.Precision.HIGHEST)


def _head_lanes():
    return lax.broadcasted_iota(jnp.int32, (BAND, 128), 1) < ATT_HEAD_DIM


def _att_cols(part, g, hp):
    return (OFF_ATT + part * 3 * W_BRANCH + g * W_BRANCH) // 128 + hp


def _att_pair(q, k, v, bias, lse_b, do, delta_b, hh, head0, mask=None):
    sel = head0 if hh == 0 else jnp.logical_not(head0)
    s = _dot(jnp.where(sel, q, 0.0), k, "nt") * (ATT_HEAD_DIM ** -0.5) + bias
    if mask is not None:
        s = jnp.where(mask, NEG_INF, s)
    c = hh * ATT_HEAD_DIM
    p = jnp.exp(s - lse_b[:, c:c + 1])
    dp = _dot(jnp.where(sel, do, 0.0), v, "nt")
    return p, p * (dp - delta_b[:, c:c + 1])


ATT_BLOCKS = {1: 32, 4: 8, 16: 2}


def _att_rows(r, i, d):
    return pl.ds(r + d * BAND * i, BAND, stride=d) if d > 1 else pl.ds(BAND * i, BAND)


def _att_specs(g, d, nq):
    ch, pb = BAND * d * nq, BAND * d
    cur = lambda part: pl.BlockSpec((ch, 128), lambda hp, n: (n, _att_cols(part, g, hp)))
    prev = lambda part: pl.BlockSpec((pb, 128), lambda hp, n: (jnp.maximum(n * nq - 1, 0), _att_cols(part, g, hp)))
    return [cur(0), cur(1), prev(1), cur(2), prev(2)]


def _att_keys(cur_ref, prev_ref, r, i, d):
    before = cur_ref[_att_rows(r, i - 1, d), :] if i > 0 else prev_ref[_att_rows(r, 0, d), :]
    return jnp.concatenate([before, cur_ref[_att_rows(r, i, d), :]], axis=0).astype(MXU_DTYPE)


def att_fwd(proj, bias, g, d):
    S = proj.shape[0]
    nq = ATT_BLOCKS[d]
    ch = BAND * d * nq

    def body(q_ref, kc_ref, kp_ref, vc_ref, vp_ref, b_ref, o_ref, l_ref):
        n = pl.program_id(1)
        head0 = _head_lanes()
        first = jnp.logical_and(lax.broadcasted_iota(jnp.int32, (BAND, 2 * BAND), 1) < BAND, n == 0)
        for r in range(d):
            for i in range(nq):
                rows = _att_rows(r, i, d)
                q = q_ref[rows, :]
                k = _att_keys(kc_ref, kp_ref, r, i, d)
                v = _att_keys(vc_ref, vp_ref, r, i, d)
                o_h, l_h = [], []
                for hh in range(2):
                    qm = jnp.where(head0 if hh == 0 else jnp.logical_not(head0), q, 0.0)
                    s = _dot(qm, k, "nt") * (ATT_HEAD_DIM ** -0.5) + b_ref[hh]
                    if i == 0:
                        s = jnp.where(first, NEG_INF, s)
                    m = jnp.max(s, axis=-1, keepdims=True)
                    p = jnp.exp(s - m)
                    l = jnp.sum(p, axis=-1, keepdims=True)
                    o_h.append(_dot(p / l, v))
                    l_h.append(jnp.broadcast_to(m + jnp.log(l), (BAND, 128)))
                o_ref[rows, :] = jnp.where(head0, o_h[0], o_h[1])
                l_ref[rows, :] = jnp.where(head0, l_h[0], l_h[1])

    out = pl.BlockSpec((ch, 128), lambda hp, n: (n, hp))
    return pl.pallas_call(
        body, name=f"att_fwd_d{d}", grid=(4, S // ch),
        in_specs=_att_specs(g, d, nq) + [pl.BlockSpec((2, BAND, 2 * BAND), lambda hp, n: (hp, 0, 0))],
        out_specs=[out, out],
        out_shape=[jax.ShapeDtypeStruct((S, W_BRANCH), F32), jax.ShapeDtypeStruct((S, W_BRANCH), F32)],
        compiler_params=_cp(2),
    )(proj, proj, proj, proj, proj, bias)


def att_bwd(proj, bias, lse, wts, dout, cbar, g, d):
    S = proj.shape[0]
    nq = ATT_BLOCKS[d]
    ch, pb = BAND * d * nq, BAND * d
    nb = S // ch
    scale = ATT_HEAD_DIM ** -0.5

    def body(q_ref, kc_ref, kp_ref, vc_ref, vp_ref, b_ref, l_ref, w_ref, do_ref, cb_ref,
             dq_ref, dk_ref, dv_ref, ek_ref, ev_ref, db_ref):
        n = pl.program_id(1)
        head0 = _head_lanes()
        first = jnp.logical_and(lax.broadcasted_iota(jnp.int32, (BAND, 2 * BAND), 1) < BAND, n == 0)

        @pl.when(n == 0)
        def _():
            db_ref[...] = jnp.zeros(db_ref.shape, F32)

        for r in range(d):
            own_k = own_v = None
            for i in range(nq):
                rows = _att_rows(r, i, d)
                q = q_ref[rows, :]
                k = _att_keys(kc_ref, kp_ref, r, i, d)
                v = _att_keys(vc_ref, vp_ref, r, i, d)
                w = w_ref[rows, :]
                do = w * do_ref[rows, :]
                delta = w * cb_ref[rows, :]
                lse_b = l_ref[rows, :]
                dq_h, dk_h, dv_h = [], [], []
                for hh in range(2):
                    p, ds = _att_pair(q, k, v, b_ref[hh], lse_b, do, delta, hh, head0, mask=first if i == 0 else None)
                    db_ref[hh] += ds
                    ds = ds * scale
                    dq_h.append(_dot(ds, k))
                    dk_h.append(_dot(ds, q, "tn"))
                    dv_h.append(_dot(p, do, "tn"))
                dq_ref[rows, :] = jnp.where(head0, dq_h[0], dq_h[1])
                head0_keys = jnp.concatenate([head0, head0], axis=0)
                dk2 = jnp.where(head0_keys, dk_h[0], dk_h[1])
                dv2 = jnp.where(head0_keys, dv_h[0], dv_h[1])
                if i == 0:
                    ek_ref[_att_rows(r, 0, d), :] = dk2[:BAND]
                    ev_ref[_att_rows(r, 0, d), :] = dv2[:BAND]
                else:
                    dk_ref[_att_rows(r, i - 1, d), :] = own_k + dk2[:BAND]
                    dv_ref[_att_rows(r, i - 1, d), :] = own_v + dv2[:BAND]
                own_k, own_v = dk2[BAND:], dv2[BAND:]
            dk_ref[_att_rows(r, nq - 1, d), :] = own_k
            dv_ref[_att_rows(r, nq - 1, d), :] = own_v

    cur = pl.BlockSpec((ch, 128), lambda hp, n: (n, hp))
    edge = pl.BlockSpec((pb, 128), lambda hp, n: (n, hp))
    bias_spec = pl.BlockSpec((2, BAND, 2 * BAND), lambda hp, n: (hp, 0, 0))
    big = jax.ShapeDtypeStruct((S, W_BRANCH), F32)
    small = jax.ShapeDtypeStruct((nb * pb, W_BRANCH), F32)
    dq, dk, dv, ek, ev, db = pl.pallas_call(
        body, name=f"att_bwd_d{d}", grid=(4, nb),
        in_specs=_att_specs(g, d, nq) + [bias_spec, cur, cur, cur, cur],
        out_specs=[cur, cur, cur, edge, edge, bias_spec],
        out_shape=[big, big, big, small, small, jax.ShapeDtypeStruct((ATT_HEADS, BAND, 2 * BAND), F32)],
        compiler_params=_cp(2),
    )(proj, proj, proj, proj, proj, bias, lse, wts, dout, cbar)

    def with_edges(main, edges):
        if nb == 1:
            return main
        main = main.reshape(nb, ch, W_BRANCH)
        add = jnp.pad(edges.reshape(nb, pb, W_BRANCH)[1:], ((0, 1), (ch - pb, 0), (0, 0)))
        return (main + add).reshape(S, W_BRANCH)

    return dq, with_edges(dk, ek), with_edges(dv, ev), db


def att_combine(os_, lses):
    S = os_[0].shape[0]

    def body(o0, o1, o2, l0, l1, l2, out_ref, w0, w1, w2):
        ls = [l0[...], l1[...], l2[...]]
        m = jnp.maximum(jnp.maximum(ls[0], ls[1]), ls[2])
        es = [jnp.exp(l - m) for l in ls]
        den = es[0] + es[1] + es[2]
        ws = [e / den for e in es]
        out_ref[...] = (ws[0] * o0[...] + ws[1] * o1[...] + ws[2] * o2[...]).astype(out_ref.dtype)
        for w_ref, w in zip((w0, w1, w2), ws):
            w_ref[...] = w

    blk = pl.BlockSpec((ROW_TILE, W_BRANCH), lambda i: (i, 0))
    f = jax.ShapeDtypeStruct((S, W_BRANCH), F32)
    return pl.pallas_call(
        body, name="att_combine", grid=(S // ROW_TILE,), in_specs=[blk] * 6, out_specs=[blk] * 4,
        out_shape=[jax.ShapeDtypeStruct((S, W_BRANCH), BF16), f, f, f], compiler_params=_cp(1),
    )(*os_, *lses)


def _split3(x):
    x1 = x.astype(BF16)
    r1 = x - x1.astype(F32)
    x2 = r1.astype(BF16)
    x3 = (r1 - x2.astype(F32)).astype(BF16)
    return x1, x2, x3


def att_combine_bwd(dout, os_, wts, headsum):
    S = dout.shape[0]

    def body(do_ref, o0, o1, o2, w0, w1, w2, e_ref, cb_ref):
        out = w0[...] * o0[...] + w1[...] * o1[...] + w2[...] * o2[...]
        e = e_ref[...]
        acc = jnp.zeros((ROW_TILE, W_BRANCH), F32)
        for term in _split3(do_ref[...] * out):
            acc = acc + jnp.dot(term, e, preferred_element_type=F32)
        cb_ref[...] = acc

    blk = pl.BlockSpec((ROW_TILE, W_BRANCH), lambda i: (i, 0))
    return pl.pallas_call(
        body, name="att_combine_bwd", grid=(S // ROW_TILE,),
        in_specs=[blk] * 7 + [pl.BlockSpec((W_BRANCH, W_BRANCH), lambda i: (0, 0))], out_specs=blk,
        out_shape=jax.ShapeDtypeStruct((S, W_BRANCH), F32), compiler_params=_cp(1),
    )(dout, *os_, *wts, headsum)


def _cmul(ar, ai, br, bi):
    return ar * br - ai * bi, ar * bi + ai * br


SCAN_ROWS = 8
SCAN_GROUPS = SSM_T // SCAN_ROWS


def _log_scan(xr, xi, mr, mi, rows, n, steps, reverse):
    total = xr.shape[0]
    for k in range(steps):
        dd = 1 << k
        keep = rows < n - dd if reverse else rows >= dd
        shift = total - dd if reverse else dd
        ar, ai = _cmul(mr, mi, jnp.where(keep, pltpu.roll(xr, shift, 0), 0.0), jnp.where(keep, pltpu.roll(xi, shift, 0), 0.0))
        xr, xi = xr + ar, xi + ai
        mr, mi = _cmul(mr, mi, mr, mi)
    return xr, xi, mr, mi


def _scan_scratch(n_results):
    return ([pltpu.VMEM((W_BRANCH // 128, SSM_T, 128), F32)] * 2 + [pltpu.VMEM((SCAN_GROUPS, W_BRANCH), F32)] * 2
            + [pltpu.VMEM((SSM_T, W_BRANCH), F32)] * n_results)


def _block_scan(xr, xi, mr, mi, reverse, yr_ref, yi_ref, er_ref, ei_ref, hr_ref, hi_ref):
    cols = xr.shape[1]
    rows = lax.broadcasted_iota(jnp.int32, (SSM_T, cols), 0)
    yr, yi, m8r, m8i = _log_scan(xr, xi, mr, mi, rows & (SCAN_ROWS - 1), SCAN_ROWS, 3, reverse)
    lane_blocks = range(cols // 128)
    for c in lane_blocks:
        yr_ref[c] = yr[:, c * 128:(c + 1) * 128]
        yi_ref[c] = yi[:, c * 128:(c + 1) * 128]
    wide = lambda ref, rows_: jnp.concatenate([ref[c, rows_, :] for c in lane_blocks], axis=1)
    end = pl.ds(0 if reverse else SCAN_ROWS - 1, SCAN_GROUPS, stride=SCAN_ROWS)
    groups = lax.broadcasted_iota(jnp.int32, (SCAN_GROUPS, cols), 0)
    er, ei, _, _ = _log_scan(wide(yr_ref, end), wide(yi_ref, end), m8r, m8i, groups, SCAN_GROUPS,
                             int(math.log2(SCAN_GROUPS)), reverse)
    er_ref[...] = er
    ei_ref[...] = ei
    j = lax.broadcasted_iota(jnp.int32, (SCAN_ROWS, cols), 0)
    dist = SCAN_ROWS - j if reverse else j + 1
    tr, ti = jnp.ones((SCAN_ROWS, cols), F32), jnp.zeros((SCAN_ROWS, cols), F32)
    br, bi = mr, mi
    for bit in range(4):
        nr, ni = _cmul(tr, ti, br, bi)
        take = ((dist >> bit) & 1) == 1
        tr, ti = jnp.where(take, nr, tr), jnp.where(take, ni, ti)
        br, bi = _cmul(br, bi, br, bi)
    for g in range(SCAN_GROUPS):
        before = g + 1 if reverse else g - 1
        rows_g = slice(g * SCAN_ROWS, (g + 1) * SCAN_ROWS)
        if 0 <= before < SCAN_GROUPS:
            ar, ai = _cmul(tr, ti, er_ref[before:before + 1, :], ei_ref[before:before + 1, :])
            hr_ref[rows_g, :] = wide(yr_ref, rows_g) + ar
            hi_ref[rows_g, :] = wide(yi_ref, rows_g) + ai
        else:
            hr_ref[rows_g, :] = wide(yr_ref, rows_g)
            hi_ref[rows_g, :] = wide(yi_ref, rows_g)
    last = 0 if reverse else SCAN_GROUPS - 1
    return er_ref[last:last + 1, :], ei_ref[last:last + 1, :]


def s5_fwd(proj, b_re, b_im, a_re, a_im, c_re, c_im, d_skip):
    S = proj.shape[0]
    nt = S // SSM_T

    def body(u_ref, bre_ref, bim_ref, ar_ref, ai_ref, cre_ref, cim_ref, dsk_ref, hr_ref, hi_ref, y_ref, cr_ref, ci_ref,
             yr_ref, yi_ref, er_ref, ei_ref):
        t = pl.program_id(1)

        @pl.when(t == 0)
        def _():
            cr_ref[...] = jnp.zeros(cr_ref.shape, F32)
            ci_ref[...] = jnp.zeros(ci_ref.shape, F32)

        u = u_ref[...]
        ar, ai = ar_ref[...], ai_ref[...]
        rows = lax.broadcasted_iota(jnp.int32, (SSM_T, W_BRANCH), 0)
        inr, ini = _cmul(ar, ai, cr_ref[0:1, :], ci_ref[0:1, :])
        xr = _dot(u, bre_ref[...]) + jnp.where(rows == 0, inr, 0.0)
        xi = _dot(u, bim_ref[...]) + jnp.where(rows == 0, ini, 0.0)
        endr, endi = _block_scan(xr, xi, ar, ai, False, yr_ref, yi_ref, er_ref, ei_ref, hr_ref, hi_ref)
        cr_ref[...] = jnp.broadcast_to(endr, cr_ref.shape)
        ci_ref[...] = jnp.broadcast_to(endi, ci_ref.shape)
        xr, xi = hr_ref[...], hi_ref[...]
        y_ref[...] = _dot(xr, cre_ref[...]) - _dot(xi, cim_ref[...]) + u * dsk_ref[...]

    u_spec = pl.BlockSpec((SSM_T, 128), lambda j, t: (t, OFF_SSM // 128 + j))
    b_spec = pl.BlockSpec((None, 128, W_BRANCH), lambda j, t: (j, 0, 0))
    a_spec = pl.BlockSpec((1, W_BRANCH), lambda j, t: (0, j))
    c_spec = pl.BlockSpec((None, W_BRANCH, 128), lambda j, t: (j, 0, 0))
    h_spec = pl.BlockSpec((SSM_T, W_BRANCH), lambda j, t: (t, j))
    return pl.pallas_call(
        body, name="s5_fwd", grid=(4, nt),
        in_specs=[u_spec, b_spec, b_spec, a_spec, a_spec, c_spec, c_spec, pl.BlockSpec((1, 128), lambda j, t: (0, j))],
        out_specs=[h_spec, h_spec, pl.BlockSpec((SSM_T, 128), lambda j, t: (t, j))],
        out_shape=[jax.ShapeDtypeStruct((S, SSM_COLS), F32), jax.ShapeDtypeStruct((S, SSM_COLS), F32),
                   jax.ShapeDtypeStruct((S, W_BRANCH), F32)],
        scratch_shapes=[pltpu.VMEM((8, W_BRANCH), F32)] * 2 + _scan_scratch(0),
        compiler_params=_cp(2),
    )(proj, b_re, b_im, a_re, a_im, c_re, c_im, d_skip)


def s5_bwd(proj, hr, hi, dy, b_re, b_im, a_re, a_im, c_re, c_im, d_skip):
    S = proj.shape[0]
    nt = S // SSM_T

    def body(u_ref, hr_ref, hi_ref, hpr_ref, hpi_ref, dy_ref, bre_ref, bim_ref, ar_ref, ai_ref, cre_ref, cim_ref,
             dsk_ref, du_ref, dbre_ref, dbim_ref, dar_ref, dai_ref, dcre_ref, dcim_ref, ddsk_ref, gr_ref, gi_ref,
             yr_ref, yi_ref, er_ref, ei_ref, sr_ref, si_ref):
        step = pl.program_id(1)
        t = nt - 1 - step

        @pl.when(step == 0)
        def _():
            gr_ref[...] = jnp.zeros(gr_ref.shape, F32)
            gi_ref[...] = jnp.zeros(gi_ref.shape, F32)
            for ref in (dbre_ref, dbim_ref, dar_ref, dai_ref, dcre_ref, dcim_ref, ddsk_ref):
                ref[...] = jnp.zeros(ref.shape, F32)

        u = u_ref[...]
        dy = dy_ref[...]
        ar, ai = ar_ref[...], ai_ref[...]
        rows = lax.broadcasted_iota(jnp.int32, (SSM_T, W_BRANCH), 0)
        inr, ini = _cmul(ar, -ai, gr_ref[0:1, :], gi_ref[0:1, :])
        xr = _dot(dy, cre_ref[...], "nt") + jnp.where(rows == SSM_T - 1, inr, 0.0)
        xi = -_dot(dy, cim_ref[...], "nt") + jnp.where(rows == SSM_T - 1, ini, 0.0)
        endr, endi = _block_scan(xr, xi, ar, -ai, True, yr_ref, yi_ref, er_ref, ei_ref, sr_ref, si_ref)
        gr_ref[...] = jnp.broadcast_to(endr, gr_ref.shape)
        gi_ref[...] = jnp.broadcast_to(endi, gi_ref.shape)
        xr, xi = sr_ref[...], si_ref[...]
        hr_blk, hi_blk = hr_ref[...], hi_ref[...]
        keep = (t > 0).astype(F32)
        hpr = jnp.where(rows >= 1, pltpu.roll(hr_blk, 1, 0), hpr_ref[7:8, :] * keep)
        hpi = jnp.where(rows >= 1, pltpu.roll(hi_blk, 1, 0), hpi_ref[7:8, :] * keep)
        dar_ref[...] += jnp.sum(hpr * xr + hpi * xi, axis=0, keepdims=True)
        dai_ref[...] += jnp.sum(hpr * xi - hpi * xr, axis=0, keepdims=True)
        dcre_ref[...] += _dot(hr_blk, dy, "tn")
        dcim_ref[...] -= _dot(hi_blk, dy, "tn")
        du = dy * dsk_ref[...] + _dot(xr, bre_ref[...], "nt") + _dot(xi, bim_ref[...], "nt")
        du_ref[...] = du.astype(du_ref.dtype)
        dbre_ref[...] += _dot(u, xr, "tn")
        dbim_ref[...] += _dot(u, xi, "tn")
        ddsk_ref[...] += jnp.sum(dy * u, axis=0, keepdims=True)

    def rev(t):
        return nt - 1 - t

    u_spec = pl.BlockSpec((SSM_T, 128), lambda j, t: (rev(t), OFF_SSM // 128 + j))
    h_spec = pl.BlockSpec((SSM_T, W_BRANCH), lambda j, t: (rev(t), j))
    hprev_spec = pl.BlockSpec((8, W_BRANCH), lambda j, t: (jnp.maximum(rev(t) * (SSM_T // 8) - 1, 0), j))
    ch_spec = pl.BlockSpec((SSM_T, 128), lambda j, t: (rev(t), j))
    b_spec = pl.BlockSpec((None, 128, W_BRANCH), lambda j, t: (j, 0, 0))
    a_spec = pl.BlockSpec((1, W_BRANCH), lambda j, t: (0, j))
    c_spec = pl.BlockSpec((None, W_BRANCH, 128), lambda j, t: (j, 0, 0))
    d_spec = pl.BlockSpec((1, 128), lambda j, t: (0, j))
    return pl.pallas_call(
        body, name="s5_bwd", grid=(4, nt),
        in_specs=[u_spec, h_spec, h_spec, hprev_spec, hprev_spec, ch_spec, b_spec, b_spec, a_spec, a_spec,
                  c_spec, c_spec, d_spec],
        out_specs=[ch_spec, b_spec, b_spec, a_spec, a_spec, c_spec, c_spec, d_spec],
        out_shape=[jax.ShapeDtypeStruct((S, W_BRANCH), BF16),
                   jax.ShapeDtypeStruct((4, 128, W_BRANCH), F32), jax.ShapeDtypeStruct((4, 128, W_BRANCH), F32),
                   jax.ShapeDtypeStruct((1, SSM_COLS), F32), jax.ShapeDtypeStruct((1, SSM_COLS), F32),
                   jax.ShapeDtypeStruct((4, W_BRANCH, 128), F32), jax.ShapeDtypeStruct((4, W_BRANCH, 128), F32),
                   jax.ShapeDtypeStruct((1, W_BRANCH), F32)],
        scratch_shapes=[pltpu.VMEM((8, W_BRANCH), F32)] * 2 + _scan_scratch(2),
        compiler_params=_cp(2),
    )(proj, hr, hi, hr, hi, dy, b_re, b_im, a_re, a_im, c_re, c_im, d_skip)


def glu_fwd(y, w_glu, b_glu, after=()):
    S = y.shape[0]

    def body(y_ref, w_ref, b_ref, *rest):
        o_ref = rest[-1]
        g = _gelu(y_ref[...])
        o_ref[...] = (g * _sigmoid(_dot(g, w_ref[...]) + b_ref[...])).astype(o_ref.dtype)

    blk = pl.BlockSpec((ROW_TILE, W_BRANCH), lambda i: (i, 0))
    return pl.pallas_call(
        body, name="glu_fwd", grid=(S // ROW_TILE,),
        in_specs=[blk, pl.BlockSpec((W_BRANCH, W_BRANCH), lambda i: (0, 0)), pl.BlockSpec((1, W_BRANCH), lambda i: (0, 0))]
        + [ANY] * len(after),
        out_specs=blk, out_shape=jax.ShapeDtypeStruct((S, W_BRANCH), BF16), compiler_params=_cp(1),
    )(y, w_glu, b_glu, *after)


def glu_bwd(y, w_glu, b_glu, dout):
    S = y.shape[0]

    def body(y_ref, w_ref, b_ref, do_ref, dy_ref, dw_ref, db_ref):
        yv = y_ref[...]
        do = do_ref[...]
        g = _gelu(yv)
        s = _sigmoid(_dot(g, w_ref[...]) + b_ref[...])
        dz = do * g * s * (1.0 - s)
        dg = do * s + _dot(dz, w_ref[...], "nt")
        dy_ref[...] = dg * _gelu_grad(yv)
        dw = _dot(g, dz, "tn")
        db = jnp.sum(dz, axis=0, keepdims=True)

        @pl.when(pl.program_id(0) == 0)
        def _():
            dw_ref[...] = dw
            db_ref[...] = db

        @pl.when(pl.program_id(0) > 0)
        def _():
            dw_ref[...] += dw
            db_ref[...] += db

    blk = pl.BlockSpec((ROW_TILE, W_BRANCH), lambda i: (i, 0))
    mat = pl.BlockSpec((W_BRANCH, W_BRANCH), lambda i: (0, 0))
    vec = pl.BlockSpec((1, W_BRANCH), lambda i: (0, 0))
    return pl.pallas_call(
        body, name="glu_bwd", grid=(S // ROW_TILE,), in_specs=[blk, mat, vec, blk], out_specs=[blk, mat, vec],
        out_shape=[jax.ShapeDtypeStruct((S, W_BRANCH), F32), jax.ShapeDtypeStruct((W_BRANCH, W_BRANCH), F32),
                   jax.ShapeDtypeStruct((1, W_BRANCH), F32)],
        compiler_params=_cp(1),
    )(y, w_glu, b_glu, dout)


SGU_TILE = 512
SGU_U_BLOCK = OFF_SGU // W_BRANCH
SGU_V_BLOCK = SGU_U_BLOCK + 1


def _sgu_norm(zv):
    v = _gelu(zv)
    mu = jnp.mean(v, axis=-1, keepdims=True)
    vc = v - mu
    rstd = lax.rsqrt(jnp.mean(vc * vc, axis=-1, keepdims=True) + EPS)
    return vc * rstd, rstd


def _tril():
    return lax.broadcasted_iota(jnp.int32, (SGU_CHUNK, SGU_CHUNK), 0) >= lax.broadcasted_iota(jnp.int32, (SGU_CHUNK, SGU_CHUNK), 1)


def sgu_fwd(proj, ln_g, ln_b, w_s, b_s_t):
    S = proj.shape[0]

    def body(zu_ref, zv_ref, g_ref, b_ref, ws_ref, bs_ref, o_ref, vf_ref):
        vn, _ = _sgu_norm(zv_ref[...])
        vf_ref[...] = vn * g_ref[...] + b_ref[...]
        tri = _tril()
        for gi in range(4):
            ws = jnp.where(tri, ws_ref[gi], 0.0)
            cols = slice(gi * 128, (gi + 1) * 128)
            for c in range(SGU_TILE // SGU_CHUNK):
                rows = slice(c * SGU_CHUNK, (c + 1) * SGU_CHUNK)
                sv = _dot(ws, vf_ref[rows, cols]) + bs_ref[:, gi:gi + 1]
                o_ref[rows, cols] = (_gelu(zu_ref[rows, cols]) * sv).astype(o_ref.dtype)

    blk = lambda cb: pl.BlockSpec((SGU_TILE, W_BRANCH), lambda i: (i, cb))
    vec = pl.BlockSpec((1, W_BRANCH), lambda i: (0, 0))
    return pl.pallas_call(
        body, name="sgu_fwd", grid=(S // SGU_TILE,),
        in_specs=[blk(SGU_U_BLOCK), blk(SGU_V_BLOCK), vec, vec, pl.BlockSpec((4, SGU_CHUNK, SGU_CHUNK), lambda i: (0, 0, 0)),
                  pl.BlockSpec((SGU_CHUNK, 4), lambda i: (0, 0))],
        out_specs=blk(0), out_shape=jax.ShapeDtypeStruct((S, W_BRANCH), BF16),
        scratch_shapes=[pltpu.VMEM((SGU_TILE, W_BRANCH), F32)], compiler_params=_cp(1),
    )(proj, proj, ln_g, ln_b, w_s, b_s_t)


def sgu_bwd(proj, ln_g, ln_b, w_s, b_s_t, dout):
    S = proj.shape[0]

    def body(zu_ref, zv_ref, g_ref, b_ref, ws_ref, bs_ref, do_ref, dzu_ref, dzv_ref, dg_ref, db_ref, dws_ref, dbs_ref,
             vf_ref, dvf_ref):
        @pl.when(pl.program_id(0) == 0)
        def _():
            for ref in (dg_ref, db_ref, dws_ref, dbs_ref):
                ref[...] = jnp.zeros(ref.shape, F32)

        vn, rstd = _sgu_norm(zv_ref[...])
        vf_ref[...] = vn * g_ref[...] + b_ref[...]
        tri = _tril()
        lane = lax.broadcasted_iota(jnp.int32, (SGU_CHUNK, 128), 1)
        dbs = jnp.zeros((SGU_CHUNK, 128), F32)
        for gi in range(4):
            ws = jnp.where(tri, ws_ref[gi], 0.0)
            cols = slice(gi * 128, (gi + 1) * 128)
            dws = jnp.zeros((SGU_CHUNK, SGU_CHUNK), F32)
            for c in range(SGU_TILE // SGU_CHUNK):
                rows = slice(c * SGU_CHUNK, (c + 1) * SGU_CHUNK)
                vf = vf_ref[rows, cols]
                zu = zu_ref[rows, cols]
                do = do_ref[rows, cols]
                sv = _dot(ws, vf) + bs_ref[:, gi:gi + 1]
                dzu_ref[rows, cols] = (do * sv * _gelu_grad(zu)).astype(dzu_ref.dtype)
                dsv = do * _gelu(zu)
                dvf_ref[rows, cols] = _dot(ws, dsv, "tn")
                dws = dws + _dot(dsv, vf, "nt")
                dbs = dbs + jnp.where(lane == gi, jnp.sum(dsv, axis=-1, keepdims=True), 0.0)
            dws_ref[gi] += jnp.where(tri, dws, 0.0)
        dbs_ref[...] += dbs
        dvf = dvf_ref[...]
        dg_ref[...] += jnp.sum(dvf * vn, axis=0, keepdims=True)
        db_ref[...] += jnp.sum(dvf, axis=0, keepdims=True)
        dvn = dvf * g_ref[...]
        dv = rstd * (dvn - jnp.mean(dvn, axis=-1, keepdims=True) - vn * jnp.mean(dvn * vn, axis=-1, keepdims=True))
        dzv_ref[...] = (dv * _gelu_grad(zv_ref[...])).astype(dzv_ref.dtype)

    blk = lambda cb: pl.BlockSpec((SGU_TILE, W_BRANCH), lambda i: (i, cb))
    vec = pl.BlockSpec((1, W_BRANCH), lambda i: (0, 0))
    ws_spec = pl.BlockSpec((4, SGU_CHUNK, SGU_CHUNK), lambda i: (0, 0, 0))
    return pl.pallas_call(
        body, name="sgu_bwd", grid=(S // SGU_TILE,),
        in_specs=[blk(SGU_U_BLOCK), blk(SGU_V_BLOCK), vec, vec, ws_spec, pl.BlockSpec((SGU_CHUNK, 4), lambda i: (0, 0)),
                  blk(0)],
        out_specs=[blk(0), blk(0), vec, vec, ws_spec, pl.BlockSpec((SGU_CHUNK, 128), lambda i: (0, 0))],
        out_shape=[jax.ShapeDtypeStruct((S, W_BRANCH), BF16), jax.ShapeDtypeStruct((S, W_BRANCH), BF16),
                   jax.ShapeDtypeStruct((1, W_BRANCH), F32), jax.ShapeDtypeStruct((1, W_BRANCH), F32),
                   jax.ShapeDtypeStruct((4, SGU_CHUNK, SGU_CHUNK), F32), jax.ShapeDtypeStruct((SGU_CHUNK, 128), F32)],
        scratch_shapes=[pltpu.VMEM((SGU_TILE, W_BRANCH), F32), pltpu.VMEM((SGU_TILE, W_BRANCH), F32)],
        compiler_params=_cp(1),
    )(proj, proj, ln_g, ln_b, w_s, b_s_t, dout)


GM_TILE = 512


def _gate_specs(order):
    def spec(i):
        def index(*ids):
            m, n = order(*ids)
            return (m, (OFF_GATE + i * D_MODEL) // GM_TILE + n)
        return pl.BlockSpec((GM_TILE, GM_TILE), index)
    return [spec(i) for i in range(4)]


def merge_fwd(proj, gate_b, branches, w_up):
    S = proj.shape[0]
    order = lambda n, m: (m, n)

    def body(p0, p1, p2, p3, gb_ref, b0, b1, b2, b3, w_ref, o_ref):
        acc = jnp.zeros((GM_TILE, GM_TILE), F32)
        for i, (p_ref, br_ref) in enumerate(zip((p0, p1, p2, p3), (b0, b1, b2, b3))):
            acc = acc + _sigmoid(p_ref[...] + gb_ref[i:i + 1, :]) * _dot(br_ref[...], w_ref[i])
        o_ref[...] = acc.astype(o_ref.dtype)

    br_spec = pl.BlockSpec((GM_TILE, W_BRANCH), lambda n, m: (m, 0))
    return pl.pallas_call(
        body, name="merge_fwd", grid=(D_MODEL // GM_TILE, S // GM_TILE),
        in_specs=_gate_specs(order) + [pl.BlockSpec((4, GM_TILE), lambda n, m: (0, n))] + [br_spec] * 4
        + [pl.BlockSpec((4, W_BRANCH, GM_TILE), lambda n, m: (0, 0, n))],
        out_specs=pl.BlockSpec((GM_TILE, GM_TILE), lambda n, m: (m, n)),
        out_shape=jax.ShapeDtypeStruct((S, D_MODEL), BF16), compiler_params=_cp(2),
    )(proj, proj, proj, proj, gate_b, *branches, w_up)


def merge_bwd(proj, gate_b, branches, w_up, dmerged):
    S = proj.shape[0]
    order = lambda n, m: (m, n)

    def body(p0, p1, p2, p3, gb_ref, b0, b1, b2, b3, w_ref, dm_ref, dp0, dp1, dp2, dp3, du0, du1, du2, du3, dgb_ref):
        dm = dm_ref[...]
        dgb = []
        for i, (p_ref, br_ref, dp_ref, du_ref) in enumerate(
                zip((p0, p1, p2, p3), (b0, b1, b2, b3), (dp0, dp1, dp2, dp3), (du0, du1, du2, du3))):
            gate = _sigmoid(p_ref[...] + gb_ref[i:i + 1, :])
            dpre = dm * _dot(br_ref[...], w_ref[i]) * gate * (1.0 - gate)
            dp_ref[...] = dpre.astype(dp_ref.dtype)
            du_ref[...] = (dm * gate).astype(du_ref.dtype)
            dgb.append(jnp.sum(dpre, axis=0, keepdims=True))
        dgb = jnp.concatenate(dgb, axis=0)

        @pl.when(pl.program_id(1) == 0)
        def _():
            dgb_ref[...] = dgb

        @pl.when(pl.program_id(1) > 0)
        def _():
            dgb_ref[...] += dgb

    br_spec = pl.BlockSpec((GM_TILE, W_BRANCH), lambda n, m: (m, 0))
    mn = pl.BlockSpec((GM_TILE, GM_TILE), lambda n, m: (m, n))
    gb = pl.BlockSpec((4, GM_TILE), lambda n, m: (0, n))
    big = jax.ShapeDtypeStruct((S, D_MODEL), BF16)
    outs = pl.pallas_call(
        body, name="merge_bwd", grid=(D_MODEL // GM_TILE, S // GM_TILE),
        in_specs=_gate_specs(order) + [gb] + [br_spec] * 4
        + [pl.BlockSpec((4, W_BRANCH, GM_TILE), lambda n, m: (0, 0, n)), mn],
        out_specs=[mn] * 8 + [gb], out_shape=[big] * 8 + [jax.ShapeDtypeStruct((4, D_MODEL), F32)],
        compiler_params=_cp(2),
    )(proj, proj, proj, proj, gate_b, *branches, w_up, dmerged)
    return outs[0:4], outs[4:8], outs[8]


def _xatt_probs(q, k):
    s = _dot(q, k, "nt") * (X_HEAD_DIM ** -0.5)
    p = jnp.exp(s - jnp.max(s, axis=-1, keepdims=True))
    return p / jnp.sum(p, axis=-1, keepdims=True)


def xatt_fwd(q, kv):
    S = q.shape[0]

    def body(q_ref, kv_ref, o_ref):
        for h in range(X_HEADS):
            cols = slice(h * X_HEAD_DIM, (h + 1) * X_HEAD_DIM)
            p = _xatt_probs(q_ref[:, cols], kv_ref[:, cols])
            o_ref[:, cols] = _dot(p, kv_ref[:, W_BRANCH + h * X_HEAD_DIM:W_BRANCH + (h + 1) * X_HEAD_DIM]).astype(o_ref.dtype)

    blk = pl.BlockSpec((ROW_TILE, W_BRANCH), lambda i: (i, 0))
    return pl.pallas_call(
        body, name="xatt_fwd", grid=(S // ROW_TILE,),
        in_specs=[blk, pl.BlockSpec((N_MEM, 2 * W_BRANCH), lambda i: (0, 0))], out_specs=blk,
        out_shape=jax.ShapeDtypeStruct((S, W_BRANCH), BF16), compiler_params=_cp(1),
    )(q, kv)


def xatt_bwd(q, kv, do):
    S = q.shape[0]

    def body(q_ref, kv_ref, do_ref, dq_ref, dkv_ref):
        @pl.when(pl.program_id(0) == 0)
        def _():
            dkv_ref[...] = jnp.zeros(dkv_ref.shape, F32)

        for h in range(X_HEADS):
            cols = slice(h * X_HEAD_DIM, (h + 1) * X_HEAD_DIM)
            vcols = slice(W_BRANCH + h * X_HEAD_DIM, W_BRANCH + (h + 1) * X_HEAD_DIM)
            qh, kh, doh = q_ref[:, cols], kv_ref[:, cols], do_ref[:, cols]
            p = _xatt_probs(qh, kh)
            dp = _dot(doh, kv_ref[:, vcols], "nt")
            ds = p * (dp - jnp.sum(dp * p, axis=-1, keepdims=True)) * (X_HEAD_DIM ** -0.5)
            dq_ref[:, cols] = _dot(ds, kh).astype(dq_ref.dtype)
            dkv_ref[:, cols] += _dot(ds, qh, "tn")
            dkv_ref[:, vcols] += _dot(p, doh, "tn")

    blk = pl.BlockSpec((ROW_TILE, W_BRANCH), lambda i: (i, 0))
    kv_spec = pl.BlockSpec((N_MEM, 2 * W_BRANCH), lambda i: (0, 0))
    return pl.pallas_call(
        body, name="xatt_bwd", grid=(S // ROW_TILE,), in_specs=[blk, kv_spec, blk], out_specs=[blk, kv_spec],
        out_shape=[jax.ShapeDtypeStruct((S, W_BRANCH), BF16), jax.ShapeDtypeStruct((N_MEM, 2 * W_BRANCH), F32)],
        compiler_params=_cp(1),
    )(q, kv, do)


def s5_params(a_re, a_im, log_dt, b_re, b_im, c_re, c_im):
    lam_re = jnp.minimum(a_re, -1e-4)
    lam_im = a_im
    dt = jnp.exp(log_dt)[:, None]
    mag = jnp.exp(lam_re * dt)
    ab_re, ab_im = mag * jnp.cos(lam_im * dt), mag * jnp.sin(lam_im * dt)
    den = lam_re * lam_re + lam_im * lam_im
    f_re = ((ab_re - 1.0) * lam_re + ab_im * lam_im) / den
    f_im = (ab_im * lam_re - (ab_re - 1.0) * lam_im) / den
    bb_re = f_re[..., None] * b_re - f_im[..., None] * b_im
    bb_im = f_re[..., None] * b_im + f_im[..., None] * b_re
    eye = jnp.eye(8, dtype=F32)

    def b_blocks(bb):
        t = bb.reshape(4, 8, SSM_STATE, SSM_GROUP).transpose(0, 1, 3, 2)
        return (t[:, :, :, None, :] * eye[None, :, None, :, None]).reshape(4, 128, W_BRANCH)

    def c_blocks(cc):
        t = cc.reshape(4, 8, SSM_GROUP, SSM_STATE).transpose(0, 1, 3, 2)
        return (t[:, :, :, None, :] * eye[None, :, None, :, None]).reshape(4, W_BRANCH, 128)

    return (ab_re.reshape(1, SSM_COLS), ab_im.reshape(1, SSM_COLS), b_blocks(bb_re), b_blocks(bb_im),
            c_blocks(c_re), c_blocks(c_im))


ANY = pl.BlockSpec(memory_space=pl.ANY)


def _chip_index():
    return 2 * lax.axis_index("x") + lax.axis_index("y")


def _peer_chip(j):
    x, y, c = lax.axis_index("x"), lax.axis_index("y"), lax.axis_index("c")
    return ((1 - x) if j & 2 else x, (1 - y) if j & 1 else y, c)


def _piece(ref, axis, s, n):
    size = ref.shape[axis] // n
    idx = [slice(None)] * len(ref.shape)
    idx[axis] = pl.ds(s * size, size)
    return ref.at[tuple(idx)]


HBM_SPEC = pl.BlockSpec(memory_space=pltpu.HBM)
SEM_SPEC = pl.BlockSpec(memory_space=pltpu.SEMAPHORE)
SIDE_EFFECT = pltpu.SideEffectType.DATAFLOW_SIDE_EFFECTING


HALVING_MIN_ROWS = 32


def _rows_half(ref, c):
    rows = ref.shape[0] // 2
    return ref.at[pl.ds(c * rows, rows), :]


def _halved(ref):
    return ref.shape[0] >= HALVING_MIN_ROWS


def _chip_copies(ins, lands, send, recv, axes, mode, k, c, arriving):
    copies = []
    for t in range(len(ins)):
        for j in (1, 2, 3):
            sems = dict(send_sem=send.at[3 * t + j - 1], recv_sem=recv.at[3 * t + j - 1], device_id_type=MESH_ID)
            if mode == "scatter":
                src = ins[t] if axes[t] is None else _piece(ins[t], axes[t], k ^ j, 4)
                dst = lands[t].at[k ^ j if arriving else k]
                device = _peer_chip(j)
            elif mode == "gather":
                src, dst = ins[t], _piece(lands[t], axes[t], k ^ j if arriving else k, 4)
                if _halved(ins[t]):
                    src, dst = _rows_half(src, c), _rows_half(dst, c)
                device = _peer_chip(j)
            else:
                if not _halved(ins[t]):
                    continue
                theirs = _piece(lands[t], axes[t], k ^ j, 4)
                src, dst = _rows_half(theirs, c), _rows_half(theirs, 1 - c if arriving else c)
                device = (lax.axis_index("x"), lax.axis_index("y"), 1 - lax.axis_index("c"))
            copies.append(pltpu.make_async_remote_copy(src_ref=src, dst_ref=dst, device_id=device, **sems))
    return copies


def _own_copies(ins, lands, send, axes, mode, k):
    if mode != "gather":
        return []
    n = len(ins)
    return [pltpu.make_async_copy(ins[t], _piece(lands[t], axes[t], k, 4), send.at[3 * n + t]) for t in range(n)]


def chips_start(ins, lands, axes, mode, name, after=()):
    n, na = len(ins), len(after)

    def body(*refs):
        in_refs, land_refs = refs[:n], refs[n:2 * n]
        send, recv, token = refs[2 * n + na], refs[2 * n + na + 1], refs[-1]
        q, core = _chip_index(), lax.axis_index("c")
        for k in range(4):
            for c in range(2):
                @pl.when(jnp.logical_and(q == k, core == c))
                def _():
                    for copy in _chip_copies(in_refs, land_refs, send, recv, axes, mode, k, c, arriving=False):
                        copy.start()
                    for copy in _own_copies(in_refs, land_refs, send, axes, mode, k):
                        copy.start()
        token[...] = jnp.zeros(token.shape, token.dtype)

    hbm = lambda a: pltpu.HBM(a.shape, a.dtype)
    outs = pl.pallas_call(
        body, name=name, in_specs=[HBM_SPEC] * (2 * n) + [ANY] * na,
        out_specs=[SEM_SPEC, SEM_SPEC] + [HBM_SPEC] * (2 * n) + [pl.BlockSpec(memory_space=pltpu.VMEM)],
        out_shape=[pltpu.SemaphoreType.DMA((4 * n,)), pltpu.SemaphoreType.DMA((3 * n,))]
        + [hbm(a) for a in ins] + [hbm(a) for a in lands] + [jax.ShapeDtypeStruct((8, 128), F32)],
        input_output_aliases={i: 2 + i for i in range(2 * n)},
        compiler_params=pltpu.CompilerParams(has_side_effects=SIDE_EFFECT),
    )(*[pltpu.with_memory_space_constraint(a, pltpu.HBM) for a in list(ins) + list(lands)], *after)
    return outs[0], outs[1], outs[2:2 + n], outs[2 + n:2 + 2 * n], outs[-1]


def chips_wait(send, recv, ins, lands, axes, mode, name, after=()):
    n = len(ins)

    def body(*refs):
        in_refs, land_refs = refs[:n], refs[n:2 * n]
        send_ref, recv_ref = refs[2 * n], refs[2 * n + 1]
        q, core = _chip_index(), lax.axis_index("c")
        for k in range(4):
            for c in range(2):
                @pl.when(jnp.logical_and(q == k, core == c))
                def _():
                    for copy in _chip_copies(in_refs, land_refs, send_ref, recv_ref, axes, mode, k, c, arriving=True):
                        copy.wait_send()
                        copy.wait_recv()
                    for copy in _own_copies(in_refs, land_refs, send_ref, axes, mode, k):
                        copy.wait()

    hbm = lambda a: pltpu.HBM(a.shape, a.dtype)
    outs = pl.pallas_call(
        body, name=name, in_specs=[HBM_SPEC] * (2 * n) + [SEM_SPEC, SEM_SPEC] + [ANY] * len(after),
        out_specs=[HBM_SPEC] * (2 * n), out_shape=[hbm(a) for a in ins] + [hbm(a) for a in lands],
        input_output_aliases={i: i for i in range(2 * n)},
        compiler_params=pltpu.CompilerParams(has_side_effects=SIDE_EFFECT),
    )(*ins, *lands, send, recv, *after)
    return outs[:n], outs[n:]


def swap_cores(arrs, name):
    n = len(arrs)

    def body(*refs):
        ins, outs = refs[:n], refs[n:2 * n]
        send, recv = refs[2 * n:]
        sibling = (lax.axis_index("x"), lax.axis_index("y"), 1 - lax.axis_index("c"))
        copies = [pltpu.make_async_remote_copy(src_ref=ins[t], dst_ref=outs[t], send_sem=send.at[t], recv_sem=recv.at[t],
                                               device_id=sibling, device_id_type=MESH_ID) for t in range(n)]
        for cp in copies:
            cp.start()
        for cp in copies:
            cp.wait()

    return pl.pallas_call(
        body, name=name, in_specs=[ANY] * n, out_specs=[ANY] * n,
        out_shape=[jax.ShapeDtypeStruct(a.shape, a.dtype) for a in arrs],
        scratch_shapes=[pltpu.SemaphoreType.DMA((n,)), pltpu.SemaphoreType.DMA((n,))],
    )(*arrs)


ELEMENTWISE_BLOCK_BYTES = 1 << 20


def _row_tile(rows, cols):
    want = max(8, ELEMENTWISE_BLOCK_BYTES // (4 * 128 * -(-cols // 128)))
    fits = [t for t in range(8, min(rows, want) + 1, 8) if rows % t == 0]
    return fits[-1] if fits else rows


def sum_chips(recv, own, axis, chip, stacked, l, name):
    _, r, c = recv.shape
    tr = _row_tile(r, c)
    nrt = r // tr

    def body(chip_ref, r_ref, own_ref, stacked_ref, o_ref):
        for k in range(4):
            @pl.when(chip_ref[0] == k)
            def _():
                terms = [own_ref[...] if s == k else r_ref[s] for s in range(4)]
                o_ref[...] = ((terms[0] + terms[1]) + terms[2]) + terms[3]

    own_index = {0: lambda i, q: (q[0] * nrt + i, 0), 1: lambda i, q: (i, q[0]), None: lambda i, q: (i, 0)}[axis]
    return pl.pallas_call(
        body, name=name,
        grid_spec=pltpu.PrefetchScalarGridSpec(
            num_scalar_prefetch=1, grid=(nrt,),
            in_specs=[pl.BlockSpec((4, tr, c), lambda i, q: (0, i, 0)), pl.BlockSpec((tr, c), own_index), ANY],
            out_specs=pl.BlockSpec((None, tr, c), lambda i, q: (l, i, 0))),
        out_shape=jax.ShapeDtypeStruct(stacked.shape, F32), input_output_aliases={3: 0}, compiler_params=_cp(1),
    )(chip, recv, own, stacked)


def adamw(w, ga, gb, m, v, name):
    rows, cols = w.shape
    tr = _row_tile(rows, cols)

    def body(w_ref, ga_ref, gb_ref, m_ref, v_ref, g_ref, d_ref, nm_ref, nv_ref):
        g = ga_ref[...] + gb_ref[...]
        nm = ADAM_B1 * m_ref[...] + (1.0 - ADAM_B1) * g
        nv = ADAM_B2 * v_ref[...] + (1.0 - ADAM_B2) * (g * g)
        m_hat = nm / (1.0 - ADAM_B1 ** ADAM_STEP)
        v_hat = nv / (1.0 - ADAM_B2 ** ADAM_STEP)
        g_ref[...] = g
        nm_ref[...] = nm
        nv_ref[...] = nv
        d_ref[...] = -ADAM_LR * (m_hat / (jnp.sqrt(v_hat) + ADAM_EPS) + ADAM_WD * w_ref[...])

    blk = pl.BlockSpec((tr, cols), lambda i: (i, 0))
    f = jax.ShapeDtypeStruct((rows, cols), F32)
    return pl.pallas_call(
        body, name=name, grid=(rows // tr,), in_specs=[blk] * 5, out_specs=[blk] * 4, out_shape=[f] * 4,
        compiler_params=_cp(1),
    )(w, ga, gb, m, v)


PACK_ALIGN = 1024
PACK_ROWS_ALIGN = 2048


def pack_small(arrs):
    parts = []
    for a in arrs:
        flat = a.reshape(-1)
        pad = (-flat.shape[0]) % PACK_ALIGN
        parts.append(jnp.pad(flat, (0, pad)) if pad else flat)
    rows = sum(p.shape[0] for p in parts) // 128
    parts.append(jnp.zeros(((-rows) % PACK_ROWS_ALIGN * 128,), arrs[0].dtype))
    return jnp.concatenate(parts).reshape(-1, 128)


def unpack_small(packed, shapes):
    out, row = [], 0
    for shape in shapes:
        size = int(np.prod(shape))
        rows = -(-size // PACK_ALIGN) * 8
        out.append(packed[row:row + rows].reshape(-1)[:size].reshape(shape))
        row += rows
    return out


def _norm_epilogue(with_next):
    def epi(acc, res, g_post, *g_pre):
        x_new = acc * lax.rsqrt(jnp.mean(acc * acc, axis=-1, keepdims=True) + EPS) * g_post + res
        if not with_next:
            return acc, x_new
        return acc, x_new, x_new * lax.rsqrt(jnp.mean(x_new * x_new, axis=-1, keepdims=True) + EPS) * g_pre[0]
    return epi


def layer_fwd(x, h1, mem, w_in, rest_of, P, biases, g_next, after=()):
    sv = {"x0": x}
    post = dict(tm=512, tn=D_MODEL)
    proj = mm(h1, w_in, "nn", out_dtypes=[F32], name="mm_w_in", after=after)
    a_out = pool_fwd(proj, P["pool_w"], P["pool_scale"])
    os_, lses = [], []
    for g, (win, dil) in enumerate(DIL_GROUPS):
        o, lse = att_fwd(proj, biases[g], g, dil)
        os_.append(o)
        lses.append(lse)
    b_out, w0, w1, w2 = att_combine(os_, lses)
    s5p = P["s5"]
    hr, hi, y = s5_fwd(proj, s5p[2], s5p[3], s5p[0], s5p[1], s5p[4], s5p[5], P["d_skip"])
    d_out = sgu_fwd(proj, P["sgu_ln_g"], P["sgu_ln_b"], P["w_s"], P["b_s_t"])
    W, after_rest = rest_of("mixer", d_out)
    W = dict(W, w_in=w_in)
    c_out = glu_fwd(y, W["w_glu"], P["b_glu"], after=after_rest)
    branches = (a_out, b_out, c_out, d_out)
    merged = merge_fwd(proj, W["gate_b"], branches, W["w_up"])
    t1, x1, h2 = mm(merged, W["w_out"], "nn", tk=1024, out_dtypes=[F32, F32, BF16], name="mm_w_out", extras=(x,),
                    vecs=(P["g_mix_post"], P["g_x_pre"]), epi=_norm_epilogue(True), after=after_rest, **post)
    sv.update(h1=h1, proj=proj, os=os_, lses=lses, wts=(w0, w1, w2), hr=hr, hi=hi, y=y, branches=branches,
              merged=merged, t1=t1, x1=x1)

    mem_n = rms_fwd(mem, P["g_mem"], BF16, "rms_mem")
    q = mm(h2, W["w_cq"], "nn", tm=1024, tn=512, tk=1024, out_dtypes=[BF16], name="mm_w_cq")
    kv = mm(mem_n, W["w_ckv"], "nn", tm=256, tn=1024, tk=1024, out_dtypes=[BF16], name="mm_w_ckv")
    ox = xatt_fwd(q, kv)
    t2, x2, h3 = mm(ox, W["w_co"], "nn", tk=512, out_dtypes=[F32, F32, BF16], name="mm_w_co", extras=(x1,),
                    vecs=(P["g_x_post"], P["g_ff_pre"]), epi=_norm_epilogue(True), **post)
    sv.update(h2=h2, mem_n=mem_n, q=q, kv=kv, ox=ox, t2=t2, x2=x2)

    W_ff, after_ff = rest_of("mlp", h3)
    W = dict(W, **W_ff)
    pre, act = mm(h3, W["w_ff1"], "nn", out_dtypes=[F32, BF16], name="mm_w_ff1",
                  epi=lambda acc: (acc, jnp.square(jnp.maximum(acc, 0.0))), after=after_ff)
    _, after_out = rest_of("out", act)
    if g_next is None:
        (ff, x3), h_next = mm(act, W["w_ff2"], "nn", out_dtypes=[F32, F32], name="mm_w_ff2_last", extras=(x2,),
                              vecs=(P["g_ff_post"],), epi=_norm_epilogue(False), after=after_out), None
    else:
        ff, x3, h_next = mm(act, W["w_ff2"], "nn", out_dtypes=[F32, F32, BF16], name="mm_w_ff2", extras=(x2,),
                            vecs=(P["g_ff_post"], g_next), epi=_norm_epilogue(True), after=after_out)
    sv.update(h3=h3, pre=pre, act=act, ff=ff, W=W)
    return x3, h_next, sv


def _pre_norm_bwd_epilogue(dh, x, add, g):
    r = lax.rsqrt(jnp.mean(x * x, axis=-1, keepdims=True) + EPS)
    xn = x * r
    dxn = dh * g
    return r * (dxn - xn * jnp.mean(dxn * xn, axis=-1, keepdims=True)) + add, jnp.sum(dh * xn, axis=0, keepdims=True)


def layer_bwd(dx, mem, W, P, biases, sv, headsum, emit, after=()):
    G = {}
    dff, G["g_ff_post"] = rms_bwd(sv["ff"], P["g_ff_post"], dx, BF16, "rms_post_bwd", after=after)
    G["w_ff2"] = mm(sv["act"], dff, "tn", out_dtypes=[F32], name="mm_dw_ff2")
    dpre = mm(dff, W["w_ff2"], "nt", out_dtypes=[BF16], name="mm_dact", extras=(sv["pre"],),
              epi=lambda acc, pre: (acc * (2.0 * jnp.maximum(pre, 0.0)),))
    G["w_ff1"] = mm(sv["h3"], dpre, "tn", out_dtypes=[F32], name="mm_dw_ff1")
    sent = emit(("w_ff1", "w_ff2"), G)
    pre_bwd = dict(out_dtypes=[F32], epi=_pre_norm_bwd_epilogue, n_sums=1)
    dx2, G["g_ff_pre"] = mm(dpre, W["w_ff1"], "nt", name="mm_dh3", extras=(sv["x2"], dx), vecs=(P["g_ff_pre"],),
                            after=sent, **pre_bwd)
    dt2, G["g_x_post"] = rms_bwd(sv["t2"], P["g_x_post"], dx2, BF16, "rms_post_bwd")
    G["w_co"] = mm(sv["ox"], dt2, "tn", tm=512, tn=1024, tk=1024, out_dtypes=[F32], name="mm_dw_co")
    dox = mm(dt2, W["w_co"], "nt", tm=1024, tn=512, tk=1024, out_dtypes=[BF16], name="mm_dox")
    dq, dkv = xatt_bwd(sv["q"], sv["kv"], dox)
    G["w_cq"] = mm(sv["h2"], dq, "tn", tm=1024, tn=512, tk=1024, out_dtypes=[F32], name="mm_dw_cq")
    G["w_ckv"] = mm(sv["mem_n"], dkv, "tn", tm=1024, tn=1024, tk=256, out_dtypes=[F32], name="mm_dw_ckv")
    dmem_n = mm(dkv, W["w_ckv"], "nt", tm=256, tn=1024, tk=1024, out_dtypes=[F32], name="mm_dmem")
    _, G["g_mem"] = rms_bwd(mem, P["g_mem"], dmem_n, BF16, "rms_mem_bwd")
    dx1, G["g_x_pre"] = mm(dq, W["w_cq"], "nt", name="mm_dh2", extras=(sv["x1"], dx2), vecs=(P["g_x_pre"],),
                           **pre_bwd)
    proj = sv["proj"]
    dt1, G["g_mix_post"] = rms_bwd(sv["t1"], P["g_mix_post"], dx1, BF16, "rms_post_bwd")
    G["w_out"] = mm(sv["merged"], dt1, "tn", tm=1024, tn=1024, tk=1024, out_dtypes=[F32], name="mm_dw_out")
    dmerged = mm(dt1, W["w_out"], "nt", tm=1024, tn=1024, tk=1024, out_dtypes=[F32], name="mm_dmerged")
    dgates, dups, G["gate_b"] = merge_bwd(proj, W["gate_b"], sv["branches"], W["w_up"], dmerged)
    dbr, dwup = [], []
    for i in range(4):
        dbr.append(mm(dups[i], W["w_up"][i], "nt", tm=1024, tn=512, tk=1024, out_dtypes=[F32], name="mm_dbranch"))
        dwup.append(mm(sv["branches"][i], dups[i], "tn", tm=512, tn=1024, tk=1024, out_dtypes=[F32], name="mm_dw_up"))
    G["w_up"] = jnp.concatenate(dwup, axis=0)
    d_pool, G["pool_w"], G["pool_scale"] = pool_bwd(proj, P["pool_w"], P["pool_scale"], dbr[0])
    cbar = att_combine_bwd(dbr[1], sv["os"], sv["wts"], headsum)
    dqs, dks, dvs, dbias = [], [], [], []
    for g, (win, dil) in enumerate(DIL_GROUPS):
        dq_g, dk_g, dv_g, db_g = att_bwd(proj, biases[g], sv["lses"][g], sv["wts"][g], dbr[1], cbar, g, dil)
        dqs.append(dq_g)
        dks.append(dk_g)
        dvs.append(dv_g)
        dbias.append(db_g)
    G["att_bias"] = dbias
    s5p = P["s5"]
    dy, G["w_glu"], G["b_glu"] = glu_bwd(sv["y"], W["w_glu"], P["b_glu"], dbr[2])
    d_ssm, dbre, dbim, dar, dai, dcre, dcim, G["d_skip"] = s5_bwd(
        proj, sv["hr"], sv["hi"], dy, s5p[2], s5p[3], s5p[0], s5p[1], s5p[4], s5p[5], P["d_skip"])
    G["s5"] = (dar, dai, dbre, dbim, dcre, dcim)
    dzu, dzv, G["sgu_ln_g"], G["sgu_ln_b"], G["w_s"], G["b_s_t"] = sgu_bwd(
        proj, P["sgu_ln_g"], P["sgu_ln_b"], P["w_s"], P["b_s_t"], dbr[3])
    d_qkv = [d.astype(BF16) for d in dqs + dks + dvs]
    dproj = jnp.concatenate([d_pool] + d_qkv + [d_ssm, dzu, dzv] + list(dgates), axis=1)
    sent = emit(("gate_b", "w_glu", "w_up", "w_out", "w_cq", "w_ckv", "w_co"), G)
    G["w_in"] = mm(sv["h1"], dproj, "tn", out_dtypes=[F32], name="mm_dw_in", after=sent)
    sent = emit(("w_in",), G)
    dx0, G["g_mix_pre"] = mm(dproj, W["w_in"], "nt", name="mm_dh1", extras=(sv["x0"], dx1), vecs=(P["g_mix_pre"],),
                             after=sent, **pre_bwd)
    return dx0, G


def _as3d(name, a):
    shape2d, axis = SHARDED[name]
    rows, cols = shape2d
    if axis == 0:
        rows //= 4
    else:
        cols //= 4
    return a.reshape(DEPTH, rows, cols)


def kernel(x, mem, rel_bias, g_mix_pre, g_mix_post, w_in, gate_b, pool_w, pool_scale, a_re, a_im, log_dt, b_re, b_im, c_re, c_im, d_skip, w_glu, b_glu, sgu_ln_g, sgu_ln_b, w_s, b_s, w_up, w_out, g_x_pre, g_x_post, g_mem, w_cq, w_ckv, w_co, g_ff_pre, g_ff_post, w_ff1, w_ff2, loss_target, m_rel_bias, m_g_mix_pre, m_g_mix_post, m_w_in, m_gate_b, m_pool_w, m_pool_scale, m_a_re, m_a_im, m_log_dt, m_b_re, m_b_im, m_c_re, m_c_im, m_d_skip, m_w_glu, m_b_glu, m_sgu_ln_g, m_sgu_ln_b, m_w_s, m_b_s, m_w_up, m_w_out, m_g_x_pre, m_g_x_post, m_g_mem, m_w_cq, m_w_ckv, m_w_co, m_g_ff_pre, m_g_ff_post, m_w_ff1, m_w_ff2, v_rel_bias, v_g_mix_pre, v_g_mix_post, v_w_in, v_gate_b, v_pool_w, v_pool_scale, v_a_re, v_a_im, v_log_dt, v_b_re, v_b_im, v_c_re, v_c_im, v_d_skip, v_w_glu, v_b_glu, v_sgu_ln_g, v_sgu_ln_b, v_w_s, v_b_s, v_w_up, v_w_out, v_g_x_pre, v_g_x_post, v_g_mem, v_w_cq, v_w_ckv, v_w_co, v_g_ff_pre, v_g_ff_post, v_w_ff1, v_w_ff2):
    env = dict(locals())
    weights = {n: env[n] for n in WEIGHT_NAMES}
    mom_m = {n: env["m_" + n] for n in WEIGHT_NAMES}
    mom_v = {n: env["v_" + n] for n in WEIGHT_NAMES}
    x2d = x.reshape(x.shape[1], D_MODEL)
    mem2d = mem.reshape(N_MEM, D_MODEL)
    target = loss_target.reshape(x2d.shape)

    axis_of = {n: SHARDED[n][1] for n in SHARDED_NAMES}
    chip = _chip_index().astype(jnp.int32).reshape(1)
    rest_names = [n for n in SHARDED_NAMES if n != "w_in"]

    def gather_start(l, names, tag, after=()):
        shards = [_as3d(n, weights[n])[l].astype(F32 if n == "gate_b" else MXU_DTYPE) for n in names]
        ax = [axis_of[n] for n in names]
        lands = [lax.empty(tuple(4 * d if i == a else d for i, d in enumerate(s.shape)), s.dtype)
                 for s, a in zip(shards, ax)]
        return (names, ax, tag) + chips_start(shards, lands, ax, "gather", f"gather_start_{tag}", after=after)

    def gather_arrive(started, after):
        names, ax, tag, send, recv, shards, lands, _ = started
        shards, lands = chips_wait(send, recv, shards, lands, ax, "gather", f"gather_wait_{tag}", after=after)
        return (names, ax, tag) + chips_start(shards, lands, ax, "forward", f"gather_forward_{tag}")

    def gather_finish(arrived, after=()):
        names, ax, tag, send, recv, shards, lands, _ = arrived
        _, lands = chips_wait(send, recv, shards, lands, ax, "forward", f"gather_landed_{tag}", after=after)
        W = dict(zip(names, lands))
        if "w_up" in W:
            W["w_up"] = W["w_up"].reshape(4, W_BRANCH, D_MODEL)
        return W

    def gather_wait(started, after):
        return gather_finish(gather_arrive(started, after))

    biases = [att_bias(rel_bias, g, dil) for g, (_, dil) in enumerate(DIL_GROUPS)]
    lanes = np.arange(W_BRANCH) // ATT_HEAD_DIM
    headsum = jnp.asarray(lanes[:, None] == lanes[None, :], dtype=BF16)

    def small_params(l, s5_prepared):
        vec = lambda a: a[l].reshape(1, -1)
        return {
            "g_mix_pre": vec(g_mix_pre), "g_mix_post": vec(g_mix_post), "g_x_pre": vec(g_x_pre), "g_x_post": vec(g_x_post),
            "g_mem": vec(g_mem), "g_ff_pre": vec(g_ff_pre), "g_ff_post": vec(g_ff_post), "pool_w": pool_w[l],
            "pool_scale": vec(pool_scale), "d_skip": vec(d_skip), "b_glu": vec(b_glu), "sgu_ln_g": vec(sgu_ln_g),
            "sgu_ln_b": vec(sgu_ln_b), "w_s": w_s[l], "b_s_t": b_s[l].T, "s5": s5_prepared,
        }

    Ws, Ps, saved, s5_vjps = [], [], [], []
    xl = x2d
    hl = rms_fwd(x2d, g_mix_pre[0].reshape(1, -1), BF16, "rms_pre")
    flying = {"next": gather_start(0, ["w_in"], "0_w_in")}
    for l in range(DEPTH):
        s5_prepared, s5_vjp = jax.vjp(s5_params, a_re[l], a_im[l], log_dt[l], b_re[l], b_im[l], c_re[l], c_im[l])
        token_of = lambda started: (started[7],)
        if l == 0:
            w_in_l = gather_wait(flying["next"], [*biases, hl])["w_in"]
            flying["rest"] = gather_start(0, rest_names, "0_rest", after=[w_in_l])
            first_after = token_of(flying["rest"])

            W_l = None
        else:
            W_l = gather_finish(flying["next"], [xl])
            w_in_l, first_after = W_l["w_in"], ()
            if l + 1 < DEPTH:
                flying["next"] = gather_start(l + 1, SHARDED_NAMES, str(l + 1), after=[w_in_l])
                first_after = token_of(flying["next"])

        def rest_of(stage, value, l=l, W_l=W_l):
            if stage == "mixer" and l == 0:
                W = gather_wait(flying["rest"], [value])
                flying["next"] = gather_start(1, SHARDED_NAMES, "1", after=[W["w_out"]])
                return W, token_of(flying["next"])
            if stage == "mixer":
                return W_l, ()
            if stage == "out" and l + 1 < DEPTH:
                flying["next"] = gather_arrive(flying["next"], [value])
                return {}, token_of(flying["next"])
            return {}, ()
        P = small_params(l, s5_prepared)
        g_next = g_mix_pre[l + 1].reshape(1, -1) if l + 1 < DEPTH else None
        xl, hl, sv = layer_fwd(xl, hl, mem2d, w_in_l, rest_of, P, biases, g_next, after=first_after)
        Ws.append(sv["W"])
        Ps.append(P)
        saved.append(sv)
        s5_vjps.append(s5_vjp)
    loss_local, dx = loss_and_grad(xl, target)
    loss = lax.psum(loss_local, ("x", "y", "c"))

    scattered = []

    def scatter_start(l, names, srcs):
        ax = [axis_of.get(n) for n in names]
        lands = []
        for s, a in zip(srcs, ax):
            r, c = s.shape
            lands.append(lax.empty((4, r // 4 if a == 0 else r, c // 4 if a == 1 else c), F32))
        tag = f"{l}_{names[0]}"
        send, recv, srcs, lands, token = chips_start(srcs, lands, ax, "scatter", f"grads_start_{tag}")
        scattered.append((l, names, ax, tag, send, recv, srcs, lands))
        return (token,)

    grads = [None] * DEPTH
    for l in reversed(range(DEPTH)):
        emit = lambda names, G, l=l: scatter_start(l, list(names), [G[n] for n in names])
        dx, grads[l] = layer_bwd(dx, mem2d, Ws[l], Ps[l], biases, saved[l], headsum, emit)
    grad_x = dx.reshape(x.shape)

    rep = {}
    stack = lambda key, shape: jnp.stack([grads[l][key] for l in range(DEPTH)]).reshape(shape)
    for n in ("g_mix_pre", "g_mix_post", "g_x_pre", "g_x_post", "g_mem", "g_ff_pre", "g_ff_post"):
        rep[n] = stack(n, (DEPTH, D_MODEL))
    for n in ("pool_scale", "d_skip", "b_glu", "sgu_ln_g", "sgu_ln_b"):
        rep[n] = stack(n, (DEPTH, W_BRANCH))
    rep["pool_w"] = stack("pool_w", pool_w.shape)
    rep["w_s"] = stack("w_s", w_s.shape)
    rep["b_s"] = jnp.stack([grads[l]["b_s_t"][:, :4].T for l in range(DEPTH)])
    s5_grads = [s5_vjps[l](tuple(grads[l]["s5"])) for l in range(DEPTH)]
    for i, n in enumerate(("a_re", "a_im", "log_dt", "b_re", "b_im", "c_re", "c_im")):
        rep[n] = jnp.stack([s5_grads[l][i] for l in range(DEPTH)])
    dbias = [sum(grads[l]["att_bias"][g] for l in range(DEPTH)) for g in range(len(DIL_GROUPS))]
    rep["rel_bias"] = jnp.concatenate([att_bias_grad(dbias[g], dil) for g, (_, dil) in enumerate(DIL_GROUPS)], axis=1)
    rep_shapes = [weights[n].shape for n in REPLICATED_NAMES]
    packed_g = pack_small([rep[n] for n in REPLICATED_NAMES])

    small_sent = scatter_start(0, ["small"], [packed_g])
    stacked = {}

    def collect(record, after):
        l, names, ax, tag, send, recv, srcs, lands = record
        srcs, lands = chips_wait(send, recv, srcs, lands, ax, "scatter", f"grads_wait_{tag}", after=after)
        for n, own, arrived, a in zip(names, srcs, lands, ax):
            if n not in stacked:
                stacked[n] = lax.empty((1 if n == "small" else DEPTH,) + arrived.shape[1:], F32)
            stacked[n] = sum_chips(arrived, own, a, chip, stacked[n], 0 if n == "small" else l, "sum_chips")

    out_g, out_d, out_m, out_v = {}, {}, {}, {}

    def update(names, tag):
        partial = [stacked[n].reshape(-1, stacked[n].shape[-1]) for n in names]
        other = swap_cores(partial, f"swap_cores_{tag}")
        for n, mine, theirs in zip(names, partial, other):
            if n == "small":
                for name, ga, gb in zip(REPLICATED_NAMES, unpack_small(mine, rep_shapes), unpack_small(theirs, rep_shapes)):
                    rows_of = lambda a: a.reshape(-1, a.shape[-1])
                    res = adamw(rows_of(weights[name]), rows_of(ga), rows_of(gb), rows_of(mom_m[name]),
                                rows_of(mom_v[name]), "adamw_small")
                    out_g[name], out_d[name], out_m[name], out_v[name] = [r.reshape(weights[name].shape) for r in res]
            else:
                flat = lambda a: a.reshape(mine.shape)
                res = adamw(flat(weights[n]), mine, theirs, flat(mom_m[n]), flat(mom_v[n]), "adamw")
                out_g[n], out_d[n], out_m[n], out_v[n] = [r.reshape(weights[n].shape) for r in res]

    late = [r for r in scattered if r[1] == ["small"] or (r[0] == 0 and r[1] == ["w_in"])]
    for record in scattered:
        if not any(record is r for r in late):
            collect(record, [dx, *small_sent])
    update(rest_names, "rest")
    collect(late[0], [out_d[n] for n in rest_names])
    update(["w_in"], "w_in")
    collect(late[1], [out_d["w_in"]])
    update(["small"], "small")

    return (loss, grad_x, *[out_g[n] for n in WEIGHT_NAMES], *[out_d[n] for n in WEIGHT_NAMES],
            *[out_m[n] for n in WEIGHT_NAMES], *[out_v[n] for n in WEIGHT_NAMES])
```

```python
import functools
import math

import numpy as np
import jax
import jax.numpy as jnp
from jax import lax
from jax.experimental import pallas as pl
from jax.experimental.pallas import tpu as pltpu

F32 = jnp.float32
BF16 = jnp.bfloat16
MXU_DTYPE = jnp.bfloat16
MESH_ID = pl.DeviceIdType.MESH
VMEM_LIMIT_BYTES = 56 * 1024 * 1024

D_MODEL = 1024
DEPTH = 4
N_MEM = 256
W_BRANCH = 512
POOL_WINDOWS = (2, 4, 8, 16)
POOL_HALO = 16
DIL_GROUPS = ((128, 1), (512, 4), (2048, 16))
BAND = 128
ATT_HEADS = 8
ATT_HEAD_DIM = 64
SSM_GROUP = 16
SSM_GROUPS = 32
SSM_STATE = 64
SSM_COLS = SSM_GROUPS * SSM_STATE
SSM_T = 512
SGU_CHUNK = 128
X_HEADS = 4
X_HEAD_DIM = 128
D_FF = 4096
REL_BUCKETS = 32
REL_MAX_DIST = 2048
EPS = 1e-6
NEG_INF = -1e30
OFF_POOL = 0
OFF_ATT = 512
OFF_SSM = OFF_ATT + 9 * W_BRANCH
OFF_SGU = OFF_SSM + W_BRANCH
OFF_GATE = OFF_SGU + 2 * W_BRANCH
IN_WIDTH = OFF_GATE + 4 * D_MODEL

ADAM_LR = 0.001
ADAM_B1 = 0.9
ADAM_B2 = 0.999
ADAM_EPS = 1e-08
ADAM_WD = 0.01
ADAM_STEP = 10

GELU_C = math.sqrt(2.0 / math.pi)

WEIGHT_NAMES = ['rel_bias', 'g_mix_pre', 'g_mix_post', 'w_in', 'gate_b', 'pool_w', 'pool_scale', 'a_re', 'a_im',
                'log_dt', 'b_re', 'b_im', 'c_re', 'c_im', 'd_skip', 'w_glu', 'b_glu', 'sgu_ln_g', 'sgu_ln_b',
                'w_s', 'b_s', 'w_up', 'w_out', 'g_x_pre', 'g_x_post', 'g_mem', 'w_cq', 'w_ckv', 'w_co',
                'g_ff_pre', 'g_ff_post', 'w_ff1', 'w_ff2']
SHARDED = {
    'w_in': ((D_MODEL, IN_WIDTH), 1),
    'gate_b': ((4, D_MODEL), 1),
    'w_glu': ((W_BRANCH, W_BRANCH), 0),
    'w_up': ((4 * W_BRANCH, D_MODEL), 1),
    'w_out': ((D_MODEL, D_MODEL), 0),
    'w_cq': ((D_MODEL, W_BRANCH), 0),
    'w_ckv': ((D_MODEL, D_MODEL), 0),
    'w_co': ((W_BRANCH, D_MODEL), 1),
    'w_ff1': ((D_MODEL, D_FF), 1),
    'w_ff2': ((D_FF, D_MODEL), 0),
}
SHARDED_NAMES = list(SHARDED)
REPLICATED_NAMES = [n for n in WEIGHT_NAMES if n not in SHARDED]


def _cp(n_axes):
    return pltpu.CompilerParams(dimension_semantics=("arbitrary",) * n_axes, vmem_limit_bytes=VMEM_LIMIT_BYTES)


def _dot(a, b, dims="nn"):
    cd = {"nn": ((1,), (0,)), "nt": ((1,), (1,)), "tn": ((0,), (0,))}[dims]
    return lax.dot_general(a.astype(MXU_DTYPE), b.astype(MXU_DTYPE), (cd, ((), ())), preferred_element_type=F32)


def _gelu(x):
    return 0.5 * x * (1.0 + jnp.tanh(GELU_C * (x + 0.044715 * (x * x * x))))


def _gelu_grad(x):
    t = jnp.tanh(GELU_C * (x + 0.044715 * (x * x * x)))
    return 0.5 * (1.0 + t) + 0.5 * x * (1.0 - t * t) * (GELU_C * (1.0 + 3.0 * 0.044715 * (x * x)))


def _sigmoid(x):
    return 1.0 / (1.0 + jnp.exp(-x))


MM_TILES = {
    "mm_w_in": (2048, 1536, 1024), "mm_dw_in": (1024, 1536, 2048), "mm_dh1": (1024, 1024, 1536),
    "mm_w_ff1": (2048, 1024, 1024), "mm_w_ff2": (1024, 1024, 2048), "mm_w_ff2_last": (1024, 1024, 2048),
    "mm_dw_ff2": (1024, 1024, 2048), "mm_dact": (2048, 1024, 1024), "mm_dw_ff1": (1024, 1024, 2048),
    "mm_dh3": (1024, 1024, 2048), "mm_dh2": (1024, 1024, 512),
}


def mm(a, b, dims, *, out_dtypes, name, tm=None, tn=None, tk=None, extras=(), vecs=(), epi=None, n_sums=0, after=()):
    if dims == "tn":
        K, M = a.shape
        N = b.shape[1]
    else:
        M, K = a.shape
        N = b.shape[1] if dims == "nn" else b.shape[0]
    if tm is None:
        tm, tn, tk = MM_TILES[name]
    tm, tn, tk = min(tm, M), min(tn, N), min(tk, K)
    assert M % tm == 0 and N % tn == 0 and K % tk == 0, (name, M, N, K, tm, tn, tk)
    assert n_sums == 0 or tn == N, name
    nk = K // tk
    ne, no = len(extras) + len(vecs), len(out_dtypes)
    if epi is None:
        epi = lambda acc: (acc,)
    a_spec = (pl.BlockSpec((tk, tm), lambda i, j, k: (k, i)) if dims == "tn"
              else pl.BlockSpec((tm, tk), lambda i, j, k: (i, k)))
    b_spec = (pl.BlockSpec((tn, tk), lambda i, j, k: (j, k)) if dims == "nt"
              else pl.BlockSpec((tk, tn), lambda i, j, k: (k, j)))
    mn_spec = pl.BlockSpec((tm, tn), lambda i, j, k: (i, j))
    vec_spec = pl.BlockSpec((1, tn), lambda i, j, k: (0, j))

    def body(a_ref, b_ref, *rest):
        first_out = ne + len(after)
        extra_refs, out_refs = rest[:ne], rest[first_out:first_out + no]
        sum_refs = rest[first_out + no:first_out + no + n_sums]
        part = _dot(a_ref[...], b_ref[...], dims)

        def finish(acc):
            results = epi(acc, *[e[...] for e in extra_refs])
            for o_ref, r in zip(out_refs, results[:no]):
                o_ref[...] = r.astype(o_ref.dtype)
            for s_ref, r in zip(sum_refs, results[no:]):
                @pl.when(pl.program_id(0) == 0)
                def _():
                    s_ref[...] = r

                @pl.when(pl.program_id(0) > 0)
                def _():
                    s_ref[...] += r

        if nk == 1:
            finish(part)
        else:
            acc_ref = rest[-1]
            k = pl.program_id(2)

            @pl.when(k == 0)
            def _():
                acc_ref[...] = part

            @pl.when(k > 0)
            def _():
                acc_ref[...] += part

            @pl.when(k == nk - 1)
            def _():
                finish(acc_ref[...])

    outs = pl.pallas_call(
        body, name=name, grid=(M // tm, N // tn, nk),
        in_specs=[a_spec, b_spec] + [mn_spec] * len(extras) + [vec_spec] * len(vecs) + [ANY] * len(after),
        out_specs=[mn_spec] * no + [vec_spec] * n_sums,
        out_shape=[jax.ShapeDtypeStruct((M, N), dt) for dt in out_dtypes] + [jax.ShapeDtypeStruct((1, N), F32)] * n_sums,
        scratch_shapes=[pltpu.VMEM((tm, tn), F32)] if nk > 1 else [],
        compiler_params=_cp(3),
    )(a, b, *extras, *vecs, *after)
    return outs[0] if no + n_sums == 1 else outs


ROW_TILE = 512


def rms_fwd(x, g, out_dtype, name, res=None):
    M, D = x.shape
    tm = min(ROW_TILE, M)

    def body(x_ref, g_ref, *rest):
        o_ref = rest[-1]
        xf = x_ref[...]
        y = xf * lax.rsqrt(jnp.mean(xf * xf, axis=-1, keepdims=True) + EPS) * g_ref[...]
        if res is not None:
            y = y + rest[0][...]
        o_ref[...] = y.astype(o_ref.dtype)

    row = pl.BlockSpec((tm, D), lambda i: (i, 0))
    return pl.pallas_call(
        body, name=name, grid=(M // tm,),
        in_specs=[row, pl.BlockSpec((1, D), lambda i: (0, 0))] + ([row] if res is not None else []),
        out_specs=row, out_shape=jax.ShapeDtypeStruct((M, D), out_dtype), compiler_params=_cp(1),
    )(x, g, *([res] if res is not None else []))


def rms_bwd(x, g, dy, dx_dtype, name, add=None, after=()):
    M, D = x.shape
    tm = min(ROW_TILE, M)

    def body(x_ref, g_ref, dy_ref, *rest):
        dx_ref, dg_ref = rest[-2], rest[-1]
        xf = x_ref[...]
        dyf = dy_ref[...].astype(F32)
        r = lax.rsqrt(jnp.mean(xf * xf, axis=-1, keepdims=True) + EPS)
        xn = xf * r
        dxn = dyf * g_ref[...]
        dx = r * (dxn - xn * jnp.mean(dxn * xn, axis=-1, keepdims=True))
        if add is not None:
            dx = dx + rest[0][...]
        dx_ref[...] = dx.astype(dx_ref.dtype)
        dg = jnp.sum(dyf * xn, axis=0, keepdims=True)

        @pl.when(pl.program_id(0) == 0)
        def _():
            dg_ref[...] = dg

        @pl.when(pl.program_id(0) > 0)
        def _():
            dg_ref[...] += dg

    row = pl.BlockSpec((tm, D), lambda i: (i, 0))
    vec = pl.BlockSpec((1, D), lambda i: (0, 0))
    return pl.pallas_call(
        body, name=name, grid=(M // tm,),
        in_specs=[row, vec, row] + ([row] if add is not None else []) + [ANY] * len(after),
        out_specs=[row, vec],
        out_shape=[jax.ShapeDtypeStruct((M, D), dx_dtype), jax.ShapeDtypeStruct((1, D), F32)],
        compiler_params=_cp(1),
    )(x, g, dy, *([add] if add is not None else []), *after)


def loss_and_grad(y, target):
    M, D = y.shape
    tm = ROW_TILE

    def body(y_ref, t_ref, part_ref, dy_ref):
        e = y_ref[...] - t_ref[...]
        dy_ref[...] = e / D
        part_ref[...] = jnp.broadcast_to(0.5 * jnp.sum(jnp.mean(e * e, axis=-1, keepdims=True), axis=0, keepdims=True),
                                         (8, 128))

    row = pl.BlockSpec((tm, D), lambda i: (i, 0))
    part, dy = pl.pallas_call(
        body, name="loss", grid=(M // tm,), in_specs=[row, row],
        out_specs=[pl.BlockSpec((8, 128), lambda i: (i, 0)), row],
        out_shape=[jax.ShapeDtypeStruct((8 * (M // tm), 128), F32), jax.ShapeDtypeStruct((M, D), F32)],
        compiler_params=_cp(1),
    )(y, target)
    return jnp.sum(part[::8, 0]), dy


POOL_ROWS = 512


def _pool_window_sum(xw, gi, roll_of):
    s1 = xw + pltpu.roll(xw, roll_of(1), 0)
    s2 = s1 + pltpu.roll(s1, roll_of(2), 0)
    s3 = s2 + pltpu.roll(s2, roll_of(4), 0)
    s4 = s3 + pltpu.roll(s3, roll_of(8), 0)
    return jnp.where(gi == 0, s1, jnp.where(gi == 1, s2, jnp.where(gi == 2, s3, s4)))


def _pool_cnt(i, gi):
    rows = lax.broadcasted_iota(jnp.int32, (POOL_ROWS, 128), 0) + i * POOL_ROWS
    w = jnp.where(gi == 0, 2, jnp.where(gi == 1, 4, jnp.where(gi == 2, 8, 16)))
    return jnp.minimum(rows + 1, w).astype(F32)


def pool_fwd(proj, pool_w, scale):
    S = proj.shape[0]
    nchunk = S // POOL_ROWS
    slab = POOL_ROWS + POOL_HALO

    def body(x_ref, w_ref, sc_ref, o_ref, pad_ref):
        gi = pl.program_id(0)
        pad_ref[0:POOL_HALO, :] = jnp.zeros((POOL_HALO, 128), F32)
        pad_ref[POOL_HALO:, :] = x_ref[...]
        for i in range(nchunk):
            xw = pad_ref[i * POOL_ROWS:i * POOL_ROWS + slab, :]
            ssum = _pool_window_sum(xw, gi, lambda d: d)[POOL_HALO:, :]
            p = ssum / _pool_cnt(i, gi) - xw[POOL_HALO:, :]
            o_ref[i * POOL_ROWS:(i + 1) * POOL_ROWS, :] = (_dot(p, w_ref[...]) * sc_ref[...]).astype(o_ref.dtype)

    return pl.pallas_call(
        body, name="pool_fwd", grid=(4,),
        in_specs=[pl.BlockSpec((S, 128), lambda g: (0, OFF_POOL // 128 + g)),
                  pl.BlockSpec((None, 128, 128), lambda g: (g, 0, 0)),
                  pl.BlockSpec((1, 128), lambda g: (0, g))],
        out_specs=pl.BlockSpec((S, 128), lambda g: (0, g)),
        out_shape=jax.ShapeDtypeStruct((S, W_BRANCH), BF16),
        scratch_shapes=[pltpu.VMEM((S + POOL_HALO, 128), F32)],
        compiler_params=_cp(1),
    )(proj, pool_w, scale)


def pool_bwd(proj, pool_w, scale, dy):
    S = proj.shape[0]
    nchunk = S // POOL_ROWS
    slab = POOL_ROWS + POOL_HALO

    def body(x_ref, w_ref, sc_ref, dy_ref, dx_ref, dw_ref, dsc_ref, pad_ref, pad2_ref, dp_ref):
        gi = pl.program_id(0)
        pad_ref[0:POOL_HALO, :] = jnp.zeros((POOL_HALO, 128), F32)
        pad_ref[POOL_HALO:, :] = x_ref[...]
        pad2_ref[S:, :] = jnp.zeros((POOL_HALO, 128), F32)
        dw = jnp.zeros((128, 128), F32)
        dsc = jnp.zeros((1, 128), F32)
        for i in range(nchunk):
            xw = pad_ref[i * POOL_ROWS:i * POOL_ROWS + slab, :]
            cnt = _pool_cnt(i, gi)
            p = _pool_window_sum(xw, gi, lambda d: d)[POOL_HALO:, :] / cnt - xw[POOL_HALO:, :]
            dyc = dy_ref[i * POOL_ROWS:(i + 1) * POOL_ROWS, :]
            dsc = dsc + jnp.sum(dyc * _dot(p, w_ref[...]), axis=0, keepdims=True)
            dys = dyc * sc_ref[...]
            dw = dw + _dot(p, dys, "tn")
            dp = _dot(dys, w_ref[...], "nt")
            dp_ref[i * POOL_ROWS:(i + 1) * POOL_ROWS, :] = dp
            pad2_ref[i * POOL_ROWS:(i + 1) * POOL_ROWS, :] = dp / cnt
        dw_ref[...] = dw
        dsc_ref[...] = dsc
        for i in range(nchunk):
            xw = pad2_ref[i * POOL_ROWS:i * POOL_ROWS + slab, :]
            fsum = _pool_window_sum(xw, gi, lambda d: slab - d)[:POOL_ROWS, :]
            rows = slice(i * POOL_ROWS, (i + 1) * POOL_ROWS)
            dx_ref[rows, :] = (fsum - dp_ref[rows, :]).astype(dx_ref.dtype)

    return pl.pallas_call(
        body, name="pool_bwd", grid=(4,),
        in_specs=[pl.BlockSpec((S, 128), lambda g: (0, OFF_POOL // 128 + g)),
                  pl.BlockSpec((None, 128, 128), lambda g: (g, 0, 0)),
                  pl.BlockSpec((1, 128), lambda g: (0, g)),
                  pl.BlockSpec((S, 128), lambda g: (0, g))],
        out_specs=[pl.BlockSpec((S, 128), lambda g: (0, g)),
                   pl.BlockSpec((None, 128, 128), lambda g: (g, 0, 0)),
                   pl.BlockSpec((1, 128), lambda g: (0, g))],
        out_shape=[jax.ShapeDtypeStruct((S, W_BRANCH), BF16), jax.ShapeDtypeStruct((4, 128, 128), F32),
                   jax.ShapeDtypeStruct((1, W_BRANCH), F32)],
        scratch_shapes=[pltpu.VMEM((S + POOL_HALO, 128), F32), pltpu.VMEM((S + POOL_HALO, 128), F32),
                        pltpu.VMEM((S, 128), F32)],
        compiler_params=_cp(1),
    )(proj, pool_w, scale, dy)


def _t5_bucket(n):
    exact = REL_BUCKETS // 2
    nf = np.maximum(n, 1).astype(np.float32)
    large = exact + (np.log(nf / exact) / np.log(REL_MAX_DIST / exact) * (REL_BUCKETS - exact)).astype(np.int32)
    large = np.minimum(large, REL_BUCKETS - 1)
    return np.where(n < exact, n, large).astype(np.int32)


def _band_onehot(dil):
    i = np.arange(BAND)[:, None]
    kk = np.arange(2 * BAND)[None, :]
    dist = BAND + i - kk
    local = (dist >= 0) & (dist <= BAND)
    bucket = _t5_bucket(np.clip(dist, 0, BAND) * dil)
    onehot = (bucket.reshape(-1, 1) == np.arange(REL_BUCKETS)[None, :]).astype(np.float32)
    return onehot, local


def att_bias(rel_bias, g, dil):
    onehot, local = _band_onehot(dil)
    tab = jnp.dot(jnp.asarray(onehot), rel_bias[:, g * ATT_HEADS:(g + 1) * ATT_HEADS], precision=lax.Precision.HIGHEST)
    bias = tab.reshape(BAND, 2 * BAND, ATT_HEADS).transpose(2, 0, 1)
    return jnp.where(jnp.asarray(local)[None], bias, NEG_INF)


def att_bias_grad(dbias, dil):
    onehot, _ = _band_onehot(dil)
    flat = dbias.transpose(1, 2, 0).reshape(BAND * 2 * BAND, ATT_HEADS)
    return jnp.dot(jnp.asarray(onehot).T, flat, precision=lax.Precision.HIGHEST)


def _head_lanes():
    return lax.broadcasted_iota(jnp.int32, (BAND, 128), 1) < ATT_HEAD_DIM


def _att_cols(part, g, hp):
    return (OFF_ATT + part * 3 * W_BRANCH + g * W_BRANCH) // 128 + hp


def _att_pair(q, k, v, bias, lse_b, do, delta_b, hh, head0, mask=None):
    sel = head0 if hh == 0 else jnp.logical_not(head0)
    s = _dot(jnp.where(sel, q, 0.0), k, "nt") * (ATT_HEAD_DIM ** -0.5) + bias
    if mask is not None:
        s = jnp.where(mask, NEG_INF, s)
    c = hh * ATT_HEAD_DIM
    p = jnp.exp(s - lse_b[:, c:c + 1])
    dp = _dot(jnp.where(sel, do, 0.0), v, "nt")
    return p, p * (dp - delta_b[:, c:c + 1])


ATT_BLOCKS = {1: 32, 4: 8, 16: 2}


def _att_rows(r, i, d):
    return pl.ds(r + d * BAND * i, BAND, stride=d) if d > 1 else pl.ds(BAND * i, BAND)


def _att_specs(g, d, nq):
    ch, pb = BAND * d * nq, BAND * d
    cur = lambda part: pl.BlockSpec((ch, 128), lambda hp, n: (n, _att_cols(part, g, hp)))
    prev = lambda part: pl.BlockSpec((pb, 128), lambda hp, n: (jnp.maximum(n * nq - 1, 0), _att_cols(part, g, hp)))
    return [cur(0), cur(1), prev(1), cur(2), prev(2)]


def _att_keys(cur_ref, prev_ref, r, i, d):
    before = cur_ref[_att_rows(r, i - 1, d), :] if i > 0 else prev_ref[_att_rows(r, 0, d), :]
    return jnp.concatenate([before, cur_ref[_att_rows(r, i, d), :]], axis=0).astype(MXU_DTYPE)


def att_fwd(proj, bias, g, d):
    S = proj.shape[0]
    nq = ATT_BLOCKS[d]
    ch = BAND * d * nq

    def body(q_ref, kc_ref, kp_ref, vc_ref, vp_ref, b_ref, o_ref, l_ref):
        n = pl.program_id(1)
        head0 = _head_lanes()
        first = jnp.logical_and(lax.broadcasted_iota(jnp.int32, (BAND, 2 * BAND), 1) < BAND, n == 0)
        for r in range(d):
            for i in range(nq):
                rows = _att_rows(r, i, d)
                q = q_ref[rows, :]
                k = _att_keys(kc_ref, kp_ref, r, i, d)
                v = _att_keys(vc_ref, vp_ref, r, i, d)
                o_h, l_h = [], []
                for hh in range(2):
                    qm = jnp.where(head0 if hh == 0 else jnp.logical_not(head0), q, 0.0)
                    s = _dot(qm, k, "nt") * (ATT_HEAD_DIM ** -0.5) + b_ref[hh]
                    if i == 0:
                        s = jnp.where(first, NEG_INF, s)
                    m = jnp.max(s, axis=-1, keepdims=True)
                    p = jnp.exp(s - m)
                    l = jnp.sum(p, axis=-1, keepdims=True)
                    o_h.append(_dot(p / l, v))
                    l_h.append(jnp.broadcast_to(m + jnp.log(l), (BAND, 128)))
                o_ref[rows, :] = jnp.where(head0, o_h[0], o_h[1])
                l_ref[rows, :] = jnp.where(head0, l_h[0], l_h[1])

    out = pl.BlockSpec((ch, 128), lambda hp, n: (n, hp))
    return pl.pallas_call(
        body, name=f"att_fwd_d{d}", grid=(4, S // ch),
        in_specs=_att_specs(g, d, nq) + [pl.BlockSpec((2, BAND, 2 * BAND), lambda hp, n: (hp, 0, 0))],
        out_specs=[out, out],
        out_shape=[jax.ShapeDtypeStruct((S, W_BRANCH), F32), jax.ShapeDtypeStruct((S, W_BRANCH), F32)],
        compiler_params=_cp(2),
    )(proj, proj, proj, proj, proj, bias)


def att_bwd(proj, bias, lse, wts, dout, cbar, g, d):
    S = proj.shape[0]
    nq = ATT_BLOCKS[d]
    ch, pb = BAND * d * nq, BAND * d
    nb = S // ch
    scale = ATT_HEAD_DIM ** -0.5

    def body(q_ref, kc_ref, kp_ref, vc_ref, vp_ref, b_ref, l_ref, w_ref, do_ref, cb_ref,
             dq_ref, dk_ref, dv_ref, ek_ref, ev_ref, db_ref):
        n = pl.program_id(1)
        head0 = _head_lanes()
        first = jnp.logical_and(lax.broadcasted_iota(jnp.int32, (BAND, 2 * BAND), 1) < BAND, n == 0)

        @pl.when(n == 0)
        def _():
            db_ref[...] = jnp.zeros(db_ref.shape, F32)

        for r in range(d):
            own_k = own_v = None
            for i in range(nq):
                rows = _att_rows(r, i, d)
                q = q_ref[rows, :]
                k = _att_keys(kc_ref, kp_ref, r, i, d)
                v = _att_keys(vc_ref, vp_ref, r, i, d)
                w = w_ref[rows, :]
                do = w * do_ref[rows, :]
                delta = w * cb_ref[rows, :]
                lse_b = l_ref[rows, :]
                dq_h, dk_h, dv_h = [], [], []
                for hh in range(2):
                    p, ds = _att_pair(q, k, v, b_ref[hh], lse_b, do, delta, hh, head0, mask=first if i == 0 else None)
                    db_ref[hh] += ds
                    ds = ds * scale
                    dq_h.append(_dot(ds, k))
                    dk_h.append(_dot(ds, q, "tn"))
                    dv_h.append(_dot(p, do, "tn"))
                dq_ref[rows, :] = jnp.where(head0, dq_h[0], dq_h[1])
                head0_keys = jnp.concatenate([head0, head0], axis=0)
                dk2 = jnp.where(head0_keys, dk_h[0], dk_h[1])
                dv2 = jnp.where(head0_keys, dv_h[0], dv_h[1])
                if i == 0:
                    ek_ref[_att_rows(r, 0, d), :] = dk2[:BAND]
                    ev_ref[_att_rows(r, 0, d), :] = dv2[:BAND]
                else:
                    dk_ref[_att_rows(r, i - 1, d), :] = own_k + dk2[:BAND]
                    dv_ref[_att_rows(r, i - 1, d), :] = own_v + dv2[:BAND]
                own_k, own_v = dk2[BAND:], dv2[BAND:]
            dk_ref[_att_rows(r, nq - 1, d), :] = own_k
            dv_ref[_att_rows(r, nq - 1, d), :] = own_v

    cur = pl.BlockSpec((ch, 128), lambda hp, n: (n, hp))
    edge = pl.BlockSpec((pb, 128), lambda hp, n: (n, hp))
    bias_spec = pl.BlockSpec((2, BAND, 2 * BAND), lambda hp, n: (hp, 0, 0))
    big = jax.ShapeDtypeStruct((S, W_BRANCH), F32)
    small = jax.ShapeDtypeStruct((nb * pb, W_BRANCH), F32)
    dq, dk, dv, ek, ev, db = pl.pallas_call(
        body, name=f"att_bwd_d{d}", grid=(4, nb),
        in_specs=_att_specs(g, d, nq) + [bias_spec, cur, cur, cur, cur],
        out_specs=[cur, cur, cur, edge, edge, bias_spec],
        out_shape=[big, big, big, small, small, jax.ShapeDtypeStruct((ATT_HEADS, BAND, 2 * BAND), F32)],
        compiler_params=_cp(2),
    )(proj, proj, proj, proj, proj, bias, lse, wts, dout, cbar)

    def with_edges(main, edges):
        if nb == 1:
            return main
        main = main.reshape(nb, ch, W_BRANCH)
        add = jnp.pad(edges.reshape(nb, pb, W_BRANCH)[1:], ((0, 1), (ch - pb, 0), (0, 0)))
        return (main + add).reshape(S, W_BRANCH)

    return dq, with_edges(dk, ek), with_edges(dv, ev), db


def att_combine(os_, lses):
    S = os_[0].shape[0]

    def body(o0, o1, o2, l0, l1, l2, out_ref, w0, w1, w2):
        ls = [l0[...], l1[...], l2[...]]
        m = jnp.maximum(jnp.maximum(ls[0], ls[1]), ls[2])
        es = [jnp.exp(l - m) for l in ls]
        den = es[0] + es[1] + es[2]
        ws = [e / den for e in es]
        out_ref[...] = (ws[0] * o0[...] + ws[1] * o1[...] + ws[2] * o2[...]).astype(out_ref.dtype)
        for w_ref, w in zip((w0, w1, w2), ws):
            w_ref[...] = w

    blk = pl.BlockSpec((ROW_TILE, W_BRANCH), lambda i: (i, 0))
    f = jax.ShapeDtypeStruct((S, W_BRANCH), F32)
    return pl.pallas_call(
        body, name="att_combine", grid=(S // ROW_TILE,), in_specs=[blk] * 6, out_specs=[blk] * 4,
        out_shape=[jax.ShapeDtypeStruct((S, W_BRANCH), BF16), f, f, f], compiler_params=_cp(1),
    )(*os_, *lses)


def _split3(x):
    x1 = x.astype(BF16)
    r1 = x - x1.astype(F32)
    x2 = r1.astype(BF16)
    x3 = (r1 - x2.astype(F32)).astype(BF16)
    return x1, x2, x3


def att_combine_bwd(dout, os_, wts, headsum):
    S = dout.shape[0]

    def body(do_ref, o0, o1, o2, w0, w1, w2, e_ref, cb_ref):
        out = w0[...] * o0[...] + w1[...] * o1[...] + w2[...] * o2[...]
        e = e_ref[...]
        acc = jnp.zeros((ROW_TILE, W_BRANCH), F32)
        for term in _split3(do_ref[...] * out):
            acc = acc + jnp.dot(term, e, preferred_element_type=F32)
        cb_ref[...] = acc

    blk = pl.BlockSpec((ROW_TILE, W_BRANCH), lambda i: (i, 0))
    return pl.pallas_call(
        body, name="att_combine_bwd", grid=(S // ROW_TILE,),
        in_specs=[blk] * 7 + [pl.BlockSpec((W_BRANCH, W_BRANCH), lambda i: (0, 0))], out_specs=blk,
        out_shape=jax.ShapeDtypeStruct((S, W_BRANCH), F32), compiler_params=_cp(1),
    )(dout, *os_, *wts, headsum)


def _cmul(ar, ai, br, bi):
    return ar * br - ai * bi, ar * bi + ai * br


SCAN_ROWS = 8
SCAN_GROUPS = SSM_T // SCAN_ROWS


def _log_scan(xr, xi, mr, mi, rows, n, steps, reverse):
    total = xr.shape[0]
    for k in range(steps):
        dd = 1 << k
        keep = rows < n - dd if reverse else rows >= dd
        shift = total - dd if reverse else dd
        ar, ai = _cmul(mr, mi, jnp.where(keep, pltpu.roll(xr, shift, 0), 0.0), jnp.where(keep, pltpu.roll(xi, shift, 0), 0.0))
        xr, xi = xr + ar, xi + ai
        mr, mi = _cmul(mr, mi, mr, mi)
    return xr, xi, mr, mi


def _scan_scratch(n_results):
    return ([pltpu.VMEM((W_BRANCH // 128, SSM_T, 128), F32)] * 2 + [pltpu.VMEM((SCAN_GROUPS, W_BRANCH), F32)] * 2
            + [pltpu.VMEM((SSM_T, W_BRANCH), F32)] * n_results)


def _block_scan(xr, xi, mr, mi, reverse, yr_ref, yi_ref, er_ref, ei_ref, hr_ref, hi_ref):
    cols = xr.shape[1]
    rows = lax.broadcasted_iota(jnp.int32, (SSM_T, cols), 0)
    yr, yi, m8r, m8i = _log_scan(xr, xi, mr, mi, rows & (SCAN_ROWS - 1), SCAN_ROWS, 3, reverse)
    lane_blocks = range(cols // 128)
    for c in lane_blocks:
        yr_ref[c] = yr[:, c * 128:(c + 1) * 128]
        yi_ref[c] = yi[:, c * 128:(c + 1) * 128]
    wide = lambda ref, rows_: jnp.concatenate([ref[c, rows_, :] for c in lane_blocks], axis=1)
    end = pl.ds(0 if reverse else SCAN_ROWS - 1, SCAN_GROUPS, stride=SCAN_ROWS)
    groups = lax.broadcasted_iota(jnp.int32, (SCAN_GROUPS, cols), 0)
    er, ei, _, _ = _log_scan(wide(yr_ref, end), wide(yi_ref, end), m8r, m8i, groups, SCAN_GROUPS,
                             int(math.log2(SCAN_GROUPS)), reverse)
    er_ref[...] = er
    ei_ref[...] = ei
    j = lax.broadcasted_iota(jnp.int32, (SCAN_ROWS, cols), 0)
    dist = SCAN_ROWS - j if reverse else j + 1
    tr, ti = jnp.ones((SCAN_ROWS, cols), F32), jnp.zeros((SCAN_ROWS, cols), F32)
    br, bi = mr, mi
    for bit in range(4):
        nr, ni = _cmul(tr, ti, br, bi)
        take = ((dist >> bit) & 1) == 1
        tr, ti = jnp.where(take, nr, tr), jnp.where(take, ni, ti)
        br, bi = _cmul(br, bi, br, bi)
    for g in range(SCAN_GROUPS):
        before = g + 1 if reverse else g - 1
        rows_g = slice(g * SCAN_ROWS, (g + 1) * SCAN_ROWS)
        if 0 <= before < SCAN_GROUPS:
            ar, ai = _cmul(tr, ti, er_ref[before:before + 1, :], ei_ref[before:before + 1, :])
            hr_ref[rows_g, :] = wide(yr_ref, rows_g) + ar
            hi_ref[rows_g, :] = wide(yi_ref, rows_g) + ai
        else:
            hr_ref[rows_g, :] = wide(yr_ref, rows_g)
            hi_ref[rows_g, :] = wide(yi_ref, rows_g)
    last = 0 if reverse else SCAN_GROUPS - 1
    return er_ref[last:last + 1, :], ei_ref[last:last + 1, :]


def s5_fwd(proj, b_re, b_im, a_re, a_im, c_re, c_im, d_skip):
    S = proj.shape[0]
    nt = S // SSM_T

    def body(u_ref, bre_ref, bim_ref, ar_ref, ai_ref, cre_ref, cim_ref, dsk_ref, hr_ref, hi_ref, y_ref, cr_ref, ci_ref,
             yr_ref, yi_ref, er_ref, ei_ref):
        t = pl.program_id(1)

        @pl.when(t == 0)
        def _():
            cr_ref[...] = jnp.zeros(cr_ref.shape, F32)
            ci_ref[...] = jnp.zeros(ci_ref.shape, F32)

        u = u_ref[...]
        ar, ai = ar_ref[...], ai_ref[...]
        rows = lax.broadcasted_iota(jnp.int32, (SSM_T, W_BRANCH), 0)
        inr, ini = _cmul(ar, ai, cr_ref[0:1, :], ci_ref[0:1, :])
        xr = _dot(u, bre_ref[...]) + jnp.where(rows == 0, inr, 0.0)
        xi = _dot(u, bim_ref[...]) + jnp.where(rows == 0, ini, 0.0)
        endr, endi = _block_scan(xr, xi, ar, ai, False, yr_ref, yi_ref, er_ref, ei_ref, hr_ref, hi_ref)
        cr_ref[...] = jnp.broadcast_to(endr, cr_ref.shape)
        ci_ref[...] = jnp.broadcast_to(endi, ci_ref.shape)
        xr, xi = hr_ref[...], hi_ref[...]
        y_ref[...] = _dot(xr, cre_ref[...]) - _dot(xi, cim_ref[...]) + u * dsk_ref[...]

    u_spec = pl.BlockSpec((SSM_T, 128), lambda j, t: (t, OFF_SSM // 128 + j))
    b_spec = pl.BlockSpec((None, 128, W_BRANCH), lambda j, t: (j, 0, 0))
    a_spec = pl.BlockSpec((1, W_BRANCH), lambda j, t: (0, j))
    c_spec = pl.BlockSpec((None, W_BRANCH, 128), lambda j, t: (j, 0, 0))
    h_spec = pl.BlockSpec((SSM_T, W_BRANCH), lambda j, t: (t, j))
    return pl.pallas_call(
        body, name="s5_fwd", grid=(4, nt),
        in_specs=[u_spec, b_spec, b_spec, a_spec, a_spec, c_spec, c_spec, pl.BlockSpec((1, 128), lambda j, t: (0, j))],
        out_specs=[h_spec, h_spec, pl.BlockSpec((SSM_T, 128), lambda j, t: (t, j))],
        out_shape=[jax.ShapeDtypeStruct((S, SSM_COLS), F32), jax.ShapeDtypeStruct((S, SSM_COLS), F32),
                   jax.ShapeDtypeStruct((S, W_BRANCH), F32)],
        scratch_shapes=[pltpu.VMEM((8, W_BRANCH), F32)] * 2 + _scan_scratch(0),
        compiler_params=_cp(2),
    )(proj, b_re, b_im, a_re, a_im, c_re, c_im, d_skip)


def s5_bwd(proj, hr, hi, dy, b_re, b_im, a_re, a_im, c_re, c_im, d_skip):
    S = proj.shape[0]
    nt = S // SSM_T

    def body(u_ref, hr_ref, hi_ref, hpr_ref, hpi_ref, dy_ref, bre_ref, bim_ref, ar_ref, ai_ref, cre_ref, cim_ref,
             dsk_ref, du_ref, dbre_ref, dbim_ref, dar_ref, dai_ref, dcre_ref, dcim_ref, ddsk_ref, gr_ref, gi_ref,
             yr_ref, yi_ref, er_ref, ei_ref, sr_ref, si_ref):
        step = pl.program_id(1)
        t = nt - 1 - step

        @pl.when(step == 0)
        def _():
            gr_ref[...] = jnp.zeros(gr_ref.shape, F32)
            gi_ref[...] = jnp.zeros(gi_ref.shape, F32)
            for ref in (dbre_ref, dbim_ref, dar_ref, dai_ref, dcre_ref, dcim_ref, ddsk_ref):
                ref[...] = jnp.zeros(ref.shape, F32)

        u = u_ref[...]
        dy = dy_ref[...]
        ar, ai = ar_ref[...], ai_ref[...]
        rows = lax.broadcasted_iota(jnp.int32, (SSM_T, W_BRANCH), 0)
        inr, ini = _cmul(ar, -ai, gr_ref[0:1, :], gi_ref[0:1, :])
        xr = _dot(dy, cre_ref[...], "nt") + jnp.where(rows == SSM_T - 1, inr, 0.0)
        xi = -_dot(dy, cim_ref[...], "nt") + jnp.where(rows == SSM_T - 1, ini, 0.0)
        endr, endi = _block_scan(xr, xi, ar, -ai, True, yr_ref, yi_ref, er_ref, ei_ref, sr_ref, si_ref)
        gr_ref[...] = jnp.broadcast_to(endr, gr_ref.shape)
        gi_ref[...] = jnp.broadcast_to(endi, gi_ref.shape)
        xr, xi = sr_ref[...], si_ref[...]
        hr_blk, hi_blk = hr_ref[...], hi_ref[...]
        keep = (t > 0).astype(F32)
        hpr = jnp.where(rows >= 1, pltpu.roll(hr_blk, 1, 0), hpr_ref[7:8, :] * keep)
        hpi = jnp.where(rows >= 1, pltpu.roll(hi_blk, 1, 0), hpi_ref[7:8, :] * keep)
        dar_ref[...] += jnp.sum(hpr * xr + hpi * xi, axis=0, keepdims=True)
        dai_ref[...] += jnp.sum(hpr * xi - hpi * xr, axis=0, keepdims=True)
        dcre_ref[...] += _dot(hr_blk, dy, "tn")
        dcim_ref[...] -= _dot(hi_blk, dy, "tn")
        du = dy * dsk_ref[...] + _dot(xr, bre_ref[...], "nt") + _dot(xi, bim_ref[...], "nt")
        du_ref[...] = du.astype(du_ref.dtype)
        dbre_ref[...] += _dot(u, xr, "tn")
        dbim_ref[...] += _dot(u, xi, "tn")
        ddsk_ref[...] += jnp.sum(dy * u, axis=0, keepdims=True)

    def rev(t):
        return nt - 1 - t

    u_spec = pl.BlockSpec((SSM_T, 128), lambda j, t: (rev(t), OFF_SSM // 128 + j))
    h_spec = pl.BlockSpec((SSM_T, W_BRANCH), lambda j, t: (rev(t), j))
    hprev_spec = pl.BlockSpec((8, W_BRANCH), lambda j, t: (jnp.maximum(rev(t) * (SSM_T // 8) - 1, 0), j))
    ch_spec = pl.BlockSpec((SSM_T, 128), lambda j, t: (rev(t), j))
    b_spec = pl.BlockSpec((None, 128, W_BRANCH), lambda j, t: (j, 0, 0))
    a_spec = pl.BlockSpec((1, W_BRANCH), lambda j, t: (0, j))
    c_spec = pl.BlockSpec((None, W_BRANCH, 128), lambda j, t: (j, 0, 0))
    d_spec = pl.BlockSpec((1, 128), lambda j, t: (0, j))
    return pl.pallas_call(
        body, name="s5_bwd", grid=(4, nt),
        in_specs=[u_spec, h_spec, h_spec, hprev_spec, hprev_spec, ch_spec, b_spec, b_spec, a_spec, a_spec,
                  c_spec, c_spec, d_spec],
        out_specs=[ch_spec, b_spec, b_spec, a_spec, a_spec, c_spec, c_spec, d_spec],
        out_shape=[jax.ShapeDtypeStruct((S, W_BRANCH), BF16),
                   jax.ShapeDtypeStruct((4, 128, W_BRANCH), F32), jax.ShapeDtypeStruct((4, 128, W_BRANCH), F32),
                   jax.ShapeDtypeStruct((1, SSM_COLS), F32), jax.ShapeDtypeStruct((1, SSM_COLS), F32),
                   jax.ShapeDtypeStruct((4, W_BRANCH, 128), F32), jax.ShapeDtypeStruct((4, W_BRANCH, 128), F32),
                   jax.ShapeDtypeStruct((1, W_BRANCH), F32)],
        scratch_shapes=[pltpu.VMEM((8, W_BRANCH), F32)] * 2 + _scan_scratch(2),
        compiler_params=_cp(2),
    )(proj, hr, hi, hr, hi, dy, b_re, b_im, a_re, a_im, c_re, c_im, d_skip)


def glu_fwd(y, w_glu, b_glu, after=()):
    S = y.shape[0]

    def body(y_ref, w_ref, b_ref, *rest):
        o_ref = rest[-1]
        g = _gelu(y_ref[...])
        o_ref[...] = (g * _sigmoid(_dot(g, w_ref[...]) + b_ref[...])).astype(o_ref.dtype)

    blk = pl.BlockSpec((ROW_TILE, W_BRANCH), lambda i: (i, 0))
    return pl.pallas_call(
        body, name="glu_fwd", grid=(S // ROW_TILE,),
        in_specs=[blk, pl.BlockSpec((W_BRANCH, W_BRANCH), lambda i: (0, 0)), pl.BlockSpec((1, W_BRANCH), lambda i: (0, 0))]
        + [ANY] * len(after),
        out_specs=blk, out_shape=jax.ShapeDtypeStruct((S, W_BRANCH), BF16), compiler_params=_cp(1),
    )(y, w_glu, b_glu, *after)


def glu_bwd(y, w_glu, b_glu, dout):
    S = y.shape[0]

    def body(y_ref, w_ref, b_ref, do_ref, dy_ref, dw_ref, db_ref):
        yv = y_ref[...]
        do = do_ref[...]
        g = _gelu(yv)
        s = _sigmoid(_dot(g, w_ref[...]) + b_ref[...])
        dz = do * g * s * (1.0 - s)
        dg = do * s + _dot(dz, w_ref[...], "nt")
        dy_ref[...] = dg * _gelu_grad(yv)
        dw = _dot(g, dz, "tn")
        db = jnp.sum(dz, axis=0, keepdims=True)

        @pl.when(pl.program_id(0) == 0)
        def _():
            dw_ref[...] = dw
            db_ref[...] = db

        @pl.when(pl.program_id(0) > 0)
        def _():
            dw_ref[...] += dw
            db_ref[...] += db

    blk = pl.BlockSpec((ROW_TILE, W_BRANCH), lambda i: (i, 0))
    mat = pl.BlockSpec((W_BRANCH, W_BRANCH), lambda i: (0, 0))
    vec = pl.BlockSpec((1, W_BRANCH), lambda i: (0, 0))
    return pl.pallas_call(
        body, name="glu_bwd", grid=(S // ROW_TILE,), in_specs=[blk, mat, vec, blk], out_specs=[blk, mat, vec],
        out_shape=[jax.ShapeDtypeStruct((S, W_BRANCH), F32), jax.ShapeDtypeStruct((W_BRANCH, W_BRANCH), F32),
                   jax.ShapeDtypeStruct((1, W_BRANCH), F32)],
        compiler_params=_cp(1),
    )(y, w_glu, b_glu, dout)


SGU_TILE = 512
SGU_U_BLOCK = OFF_SGU // W_BRANCH
SGU_V_BLOCK = SGU_U_BLOCK + 1


def _sgu_norm(zv):
    v = _gelu(zv)
    mu = jnp.mean(v, axis=-1, keepdims=True)
    vc = v - mu
    rstd = lax.rsqrt(jnp.mean(vc * vc, axis=-1, keepdims=True) + EPS)
    return vc * rstd, rstd


def _tril():
    return lax.broadcasted_iota(jnp.int32, (SGU_CHUNK, SGU_CHUNK), 0) >= lax.broadcasted_iota(jnp.int32, (SGU_CHUNK, SGU_CHUNK), 1)


def sgu_fwd(proj, ln_g, ln_b, w_s, b_s_t):
    S = proj.shape[0]

    def body(zu_ref, zv_ref, g_ref, b_ref, ws_ref, bs_ref, o_ref, vf_ref):
        vn, _ = _sgu_norm(zv_ref[...])
        vf_ref[...] = vn * g_ref[...] + b_ref[...]
        tri = _tril()
        for gi in range(4):
            ws = jnp.where(tri, ws_ref[gi], 0.0)
            cols = slice(gi * 128, (gi + 1) * 128)
            for c in range(SGU_TILE // SGU_CHUNK):
                rows = slice(c * SGU_CHUNK, (c + 1) * SGU_CHUNK)
                sv = _dot(ws, vf_ref[rows, cols]) + bs_ref[:, gi:gi + 1]
                o_ref[rows, cols] = (_gelu(zu_ref[rows, cols]) * sv).astype(o_ref.dtype)

    blk = lambda cb: pl.BlockSpec((SGU_TILE, W_BRANCH), lambda i: (i, cb))
    vec = pl.BlockSpec((1, W_BRANCH), lambda i: (0, 0))
    return pl.pallas_call(
        body, name="sgu_fwd", grid=(S // SGU_TILE,),
        in_specs=[blk(SGU_U_BLOCK), blk(SGU_V_BLOCK), vec, vec, pl.BlockSpec((4, SGU_CHUNK, SGU_CHUNK), lambda i: (0, 0, 0)),
                  pl.BlockSpec((SGU_CHUNK, 4), lambda i: (0, 0))],
        out_specs=blk(0), out_shape=jax.ShapeDtypeStruct((S, W_BRANCH), BF16),
        scratch_shapes=[pltpu.VMEM((SGU_TILE, W_BRANCH), F32)], compiler_params=_cp(1),
    )(proj, proj, ln_g, ln_b, w_s, b_s_t)


def sgu_bwd(proj, ln_g, ln_b, w_s, b_s_t, dout):
    S = proj.shape[0]

    def body(zu_ref, zv_ref, g_ref, b_ref, ws_ref, bs_ref, do_ref, dzu_ref, dzv_ref, dg_ref, db_ref, dws_ref, dbs_ref,
             vf_ref, dvf_ref):
        @pl.when(pl.program_id(0) == 0)
        def _():
            for ref in (dg_ref, db_ref, dws_ref, dbs_ref):
                ref[...] = jnp.zeros(ref.shape, F32)

        vn, rstd = _sgu_norm(zv_ref[...])
        vf_ref[...] = vn * g_ref[...] + b_ref[...]
        tri = _tril()
        lane = lax.broadcasted_iota(jnp.int32, (SGU_CHUNK, 128), 1)
        dbs = jnp.zeros((SGU_CHUNK, 128), F32)
        for gi in range(4):
            ws = jnp.where(tri, ws_ref[gi], 0.0)
            cols = slice(gi * 128, (gi + 1) * 128)
            dws = jnp.zeros((SGU_CHUNK, SGU_CHUNK), F32)
            for c in range(SGU_TILE // SGU_CHUNK):
                rows = slice(c * SGU_CHUNK, (c + 1) * SGU_CHUNK)
                vf = vf_ref[rows, cols]
                zu = zu_ref[rows, cols]
                do = do_ref[rows, cols]
                sv = _dot(ws, vf) + bs_ref[:, gi:gi + 1]
                dzu_ref[rows, cols] = (do * sv * _gelu_grad(zu)).astype(dzu_ref.dtype)
                dsv = do * _gelu(zu)
                dvf_ref[rows, cols] = _dot(ws, dsv, "tn")
                dws = dws + _dot(dsv, vf, "nt")
                dbs = dbs + jnp.where(lane == gi, jnp.sum(dsv, axis=-1, keepdims=True), 0.0)
            dws_ref[gi] += jnp.where(tri, dws, 0.0)
        dbs_ref[...] += dbs
        dvf = dvf_ref[...]
        dg_ref[...] += jnp.sum(dvf * vn, axis=0, keepdims=True)
        db_ref[...] += jnp.sum(dvf, axis=0, keepdims=True)
        dvn = dvf * g_ref[...]
        dv = rstd * (dvn - jnp.mean(dvn, axis=-1, keepdims=True) - vn * jnp.mean(dvn * vn, axis=-1, keepdims=True))
        dzv_ref[...] = (dv * _gelu_grad(zv_ref[...])).astype(dzv_ref.dtype)

    blk = lambda cb: pl.BlockSpec((SGU_TILE, W_BRANCH), lambda i: (i, cb))
    vec = pl.BlockSpec((1, W_BRANCH), lambda i: (0, 0))
    ws_spec = pl.BlockSpec((4, SGU_CHUNK, SGU_CHUNK), lambda i: (0, 0, 0))
    return pl.pallas_call(
        body, name="sgu_bwd", grid=(S // SGU_TILE,),
        in_specs=[blk(SGU_U_BLOCK), blk(SGU_V_BLOCK), vec, vec, ws_spec, pl.BlockSpec((SGU_CHUNK, 4), lambda i: (0, 0)),
                  blk(0)],
        out_specs=[blk(0), blk(0), vec, vec, ws_spec, pl.BlockSpec((SGU_CHUNK, 128), lambda i: (0, 0))],
        out_shape=[jax.ShapeDtypeStruct((S, W_BRANCH), BF16), jax.ShapeDtypeStruct((S, W_BRANCH), BF16),
                   jax.ShapeDtypeStruct((1, W_BRANCH), F32), jax.ShapeDtypeStruct((1, W_BRANCH), F32),
                   jax.ShapeDtypeStruct((4, SGU_CHUNK, SGU_CHUNK), F32), jax.ShapeDtypeStruct((SGU_CHUNK, 128), F32)],
        scratch_shapes=[pltpu.VMEM((SGU_TILE, W_BRANCH), F32), pltpu.VMEM((SGU_TILE, W_BRANCH), F32)],
        compiler_params=_cp(1),
    )(proj, proj, ln_g, ln_b, w_s, b_s_t, dout)


GM_TILE = 512


def _gate_specs(order):
    def spec(i):
        def index(*ids):
            m, n = order(*ids)
            return (m, (OFF_GATE + i * D_MODEL) // GM_TILE + n)
        return pl.BlockSpec((GM_TILE, GM_TILE), index)
    return [spec(i) for i in range(4)]


def merge_fwd(proj, gate_b, branches, w_up):
    S = proj.shape[0]
    order = lambda n, m: (m, n)

    def body(p0, p1, p2, p3, gb_ref, b0, b1, b2, b3, w_ref, o_ref):
        acc = jnp.zeros((GM_TILE, GM_TILE), F32)
        for i, (p_ref, br_ref) in enumerate(zip((p0, p1, p2, p3), (b0, b1, b2, b3))):
            acc = acc + _sigmoid(p_ref[...] + gb_ref[i:i + 1, :]) * _dot(br_ref[...], w_ref[i])
        o_ref[...] = acc.astype(o_ref.dtype)

    br_spec = pl.BlockSpec((GM_TILE, W_BRANCH), lambda n, m: (m, 0))
    return pl.pallas_call(
        body, name="merge_fwd", grid=(D_MODEL // GM_TILE, S // GM_TILE),
        in_specs=_gate_specs(order) + [pl.BlockSpec((4, GM_TILE), lambda n, m: (0, n))] + [br_spec] * 4
        + [pl.BlockSpec((4, W_BRANCH, GM_TILE), lambda n, m: (0, 0, n))],
        out_specs=pl.BlockSpec((GM_TILE, GM_TILE), lambda n, m: (m, n)),
        out_shape=jax.ShapeDtypeStruct((S, D_MODEL), BF16), compiler_params=_cp(2),
    )(proj, proj, proj, proj, gate_b, *branches, w_up)


def merge_bwd(proj, gate_b, branches, w_up, dmerged):
    S = proj.shape[0]
    order = lambda n, m: (m, n)

    def body(p0, p1, p2, p3, gb_ref, b0, b1, b2, b3, w_ref, dm_ref, dp0, dp1, dp2, dp3, du0, du1, du2, du3, dgb_ref):
        dm = dm_ref[...]
        dgb = []
        for i, (p_ref, br_ref, dp_ref, du_ref) in enumerate(
                zip((p0, p1, p2, p3), (b0, b1, b2, b3), (dp0, dp1, dp2, dp3), (du0, du1, du2, du3))):
            gate = _sigmoid(p_ref[...] + gb_ref[i:i + 1, :])
            dpre = dm * _dot(br_ref[...], w_ref[i]) * gate * (1.0 - gate)
            dp_ref[...] = dpre.astype(dp_ref.dtype)
            du_ref[...] = (dm * gate).astype(du_ref.dtype)
            dgb.append(jnp.sum(dpre, axis=0, keepdims=True))
        dgb = jnp.concatenate(dgb, axis=0)

        @pl.when(pl.program_id(1) == 0)
        def _():
            dgb_ref[...] = dgb

        @pl.when(pl.program_id(1) > 0)
        def _():
            dgb_ref[...] += dgb

    br_spec = pl.BlockSpec((GM_TILE, W_BRANCH), lambda n, m: (m, 0))
    mn = pl.BlockSpec((GM_TILE, GM_TILE), lambda n, m: (m, n))
    gb = pl.BlockSpec((4, GM_TILE), lambda n, m: (0, n))
    big = jax.ShapeDtypeStruct((S, D_MODEL), BF16)
    outs = pl.pallas_call(
        body, name="merge_bwd", grid=(D_MODEL // GM_TILE, S // GM_TILE),
        in_specs=_gate_specs(order) + [gb] + [br_spec] * 4
        + [pl.BlockSpec((4, W_BRANCH, GM_TILE), lambda n, m: (0, 0, n)), mn],
        out_specs=[mn] * 8 + [gb], out_shape=[big] * 8 + [jax.ShapeDtypeStruct((4, D_MODEL), F32)],
        compiler_params=_cp(2),
    )(proj, proj, proj, proj, gate_b, *branches, w_up, dmerged)
    return outs[0:4], outs[4:8], outs[8]


def _xatt_probs(q, k):
    s = _dot(q, k, "nt") * (X_HEAD_DIM ** -0.5)
    p = jnp.exp(s - jnp.max(s, axis=-1, keepdims=True))
    return p / jnp.sum(p, axis=-1, keepdims=True)


def xatt_fwd(q, kv):
    S = q.shape[0]

    def body(q_ref, kv_ref, o_ref):
        for h in range(X_HEADS):
            cols = slice(h * X_HEAD_DIM, (h + 1) * X_HEAD_DIM)
            p = _xatt_probs(q_ref[:, cols], kv_ref[:, cols])
            o_ref[:, cols] = _dot(p, kv_ref[:, W_BRANCH + h * X_HEAD_DIM:W_BRANCH + (h + 1) * X_HEAD_DIM]).astype(o_ref.dtype)

    blk = pl.BlockSpec((ROW_TILE, W_BRANCH), lambda i: (i, 0))
    return pl.pallas_call(
        body, name="xatt_fwd", grid=(S // ROW_TILE,),
        in_specs=[blk, pl.BlockSpec((N_MEM, 2 * W_BRANCH), lambda i: (0, 0))], out_specs=blk,
        out_shape=jax.ShapeDtypeStruct((S, W_BRANCH), BF16), compiler_params=_cp(1),
    )(q, kv)


def xatt_bwd(q, kv, do):
    S = q.shape[0]

    def body(q_ref, kv_ref, do_ref, dq_ref, dkv_ref):
        @pl.when(pl.program_id(0) == 0)
        def _():
            dkv_ref[...] = jnp.zeros(dkv_ref.shape, F32)

        for h in range(X_HEADS):
            cols = slice(h * X_HEAD_DIM, (h + 1) * X_HEAD_DIM)
            vcols = slice(W_BRANCH + h * X_HEAD_DIM, W_BRANCH + (h + 1) * X_HEAD_DIM)
            qh, kh, doh = q_ref[:, cols], kv_ref[:, cols], do_ref[:, cols]
            p = _xatt_probs(qh, kh)
            dp = _dot(doh, kv_ref[:, vcols], "nt")
            ds = p * (dp - jnp.sum(dp * p, axis=-1, keepdims=True)) * (X_HEAD_DIM ** -0.5)
            dq_ref[:, cols] = _dot(ds, kh).astype(dq_ref.dtype)
            dkv_ref[:, cols] += _dot(ds, qh, "tn")
            dkv_ref[:, vcols] += _dot(p, doh, "tn")

    blk = pl.BlockSpec((ROW_TILE, W_BRANCH), lambda i: (i, 0))
    kv_spec = pl.BlockSpec((N_MEM, 2 * W_BRANCH), lambda i: (0, 0))
    return pl.pallas_call(
        body, name="xatt_bwd", grid=(S // ROW_TILE,), in_specs=[blk, kv_spec, blk], out_specs=[blk, kv_spec],
        out_shape=[jax.ShapeDtypeStruct((S, W_BRANCH), BF16), jax.ShapeDtypeStruct((N_MEM, 2 * W_BRANCH), F32)],
        compiler_params=_cp(1),
    )(q, kv, do)


def s5_params(a_re, a_im, log_dt, b_re, b_im, c_re, c_im):
    lam_re = jnp.minimum(a_re, -1e-4)
    lam_im = a_im
    dt = jnp.exp(log_dt)[:, None]
    mag = jnp.exp(lam_re * dt)
    ab_re, ab_im = mag * jnp.cos(lam_im * dt), mag * jnp.sin(lam_im * dt)
    den = lam_re * lam_re + lam_im * lam_im
    f_re = ((ab_re - 1.0) * lam_re + ab_im * lam_im) / den
    f_im = (ab_im * lam_re - (ab_re - 1.0) * lam_im) / den
    bb_re = f_re[..., None] * b_re - f_im[..., None] * b_im
    bb_im = f_re[..., None] * b_im + f_im[..., None] * b_re
    eye = jnp.eye(8, dtype=F32)

    def b_blocks(bb):
        t = bb.reshape(4, 8, SSM_STATE, SSM_GROUP).transpose(0, 1, 3, 2)
        return (t[:, :, :, None, :] * eye[None, :, None, :, None]).reshape(4, 128, W_BRANCH)

    def c_blocks(cc):
        t = cc.reshape(4, 8, SSM_GROUP, SSM_STATE).transpose(0, 1, 3, 2)
        return (t[:, :, :, None, :] * eye[None, :, None, :, None]).reshape(4, W_BRANCH, 128)

    return (ab_re.reshape(1, SSM_COLS), ab_im.reshape(1, SSM_COLS), b_blocks(bb_re), b_blocks(bb_im),
            c_blocks(c_re), c_blocks(c_im))


ANY = pl.BlockSpec(memory_space=pl.ANY)


def _chip_index():
    return 2 * lax.axis_index("x") + lax.axis_index("y")


def _peer_chip(j):
    x, y, c = lax.axis_index("x"), lax.axis_index("y"), lax.axis_index("c")
    return ((1 - x) if j & 2 else x, (1 - y) if j & 1 else y, c)


def _piece(ref, axis, s, n):
    size = ref.shape[axis] // n
    idx = [slice(None)] * len(ref.shape)
    idx[axis] = pl.ds(s * size, size)
    return ref.at[tuple(idx)]


HBM_SPEC = pl.BlockSpec(memory_space=pltpu.HBM)
SEM_SPEC = pl.BlockSpec(memory_space=pltpu.SEMAPHORE)
SIDE_EFFECT = pltpu.SideEffectType.DATAFLOW_SIDE_EFFECTING


HALVING_MIN_ROWS = 32


def _rows_half(ref, c):
    rows = ref.shape[0] // 2
    return ref.at[pl.ds(c * rows, rows), :]


def _halved(ref):
    return ref.shape[0] >= HALVING_MIN_ROWS


def _chip_copies(ins, lands, send, recv, axes, mode, k, c, arriving):
    copies = []
    for t in range(len(ins)):
        for j in (1, 2, 3):
            sems = dict(send_sem=send.at[3 * t + j - 1], recv_sem=recv.at[3 * t + j - 1], device_id_type=MESH_ID)
            if mode == "scatter":
                src = ins[t] if axes[t] is None else _piece(ins[t], axes[t], k ^ j, 4)
                dst = lands[t].at[k ^ j if arriving else k]
                device = _peer_chip(j)
            elif mode == "gather":
                src, dst = ins[t], _piece(lands[t], axes[t], k ^ j if arriving else k, 4)
                if _halved(ins[t]):
                    src, dst = _rows_half(src, c), _rows_half(dst, c)
                device = _peer_chip(j)
            else:
                if not _halved(ins[t]):
                    continue
                theirs = _piece(lands[t], axes[t], k ^ j, 4)
                src, dst = _rows_half(theirs, c), _rows_half(theirs, 1 - c if arriving else c)
                device = (lax.axis_index("x"), lax.axis_index("y"), 1 - lax.axis_index("c"))
            copies.append(pltpu.make_async_remote_copy(src_ref=src, dst_ref=dst, device_id=device, **sems))
    return copies


def _own_copies(ins, lands, send, axes, mode, k):
    if mode != "gather":
        return []
    n = len(ins)
    return [pltpu.make_async_copy(ins[t], _piece(lands[t], axes[t], k, 4), send.at[3 * n + t]) for t in range(n)]


def chips_start(ins, lands, axes, mode, name, after=()):
    n, na = len(ins), len(after)

    def body(*refs):
        in_refs, land_refs = refs[:n], refs[n:2 * n]
        send, recv, token = refs[2 * n + na], refs[2 * n + na + 1], refs[-1]
        q, core = _chip_index(), lax.axis_index("c")
        for k in range(4):
            for c in range(2):
                @pl.when(jnp.logical_and(q == k, core == c))
                def _():
                    for copy in _chip_copies(in_refs, land_refs, send, recv, axes, mode, k, c, arriving=False):
                        copy.start()
                    for copy in _own_copies(in_refs, land_refs, send, axes, mode, k):
                        copy.start()
        token[...] = jnp.zeros(token.shape, token.dtype)

    hbm = lambda a: pltpu.HBM(a.shape, a.dtype)
    outs = pl.pallas_call(
        body, name=name, in_specs=[HBM_SPEC] * (2 * n) + [ANY] * na,
        out_specs=[SEM_SPEC, SEM_SPEC] + [HBM_SPEC] * (2 * n) + [pl.BlockSpec(memory_space=pltpu.VMEM)],
        out_shape=[pltpu.SemaphoreType.DMA((4 * n,)), pltpu.SemaphoreType.DMA((3 * n,))]
        + [hbm(a) for a in ins] + [hbm(a) for a in lands] + [jax.ShapeDtypeStruct((8, 128), F32)],
        input_output_aliases={i: 2 + i for i in range(2 * n)},
        compiler_params=pltpu.CompilerParams(has_side_effects=SIDE_EFFECT),
    )(*[pltpu.with_memory_space_constraint(a, pltpu.HBM) for a in list(ins) + list(lands)], *after)
    return outs[0], outs[1], outs[2:2 + n], outs[2 + n:2 + 2 * n], outs[-1]


def chips_wait(send, recv, ins, lands, axes, mode, name, after=()):
    n = len(ins)

    def body(*refs):
        in_refs, land_refs = refs[:n], refs[n:2 * n]
        send_ref, recv_ref = refs[2 * n], refs[2 * n + 1]
        q, core = _chip_index(), lax.axis_index("c")
        for k in range(4):
            for c in range(2):
                @pl.when(jnp.logical_and(q == k, core == c))
                def _():
                    for copy in _chip_copies(in_refs, land_refs, send_ref, recv_ref, axes, mode, k, c, arriving=True):
                        copy.wait_send()
                        copy.wait_recv()
                    for copy in _own_copies(in_refs, land_refs, send_ref, axes, mode, k):
                        copy.wait()

    hbm = lambda a: pltpu.HBM(a.shape, a.dtype)
    outs = pl.pallas_call(
        body, name=name, in_specs=[HBM_SPEC] * (2 * n) + [SEM_SPEC, SEM_SPEC] + [ANY] * len(after),
        out_specs=[HBM_SPEC] * (2 * n), out_shape=[hbm(a) for a in ins] + [hbm(a) for a in lands],
        input_output_aliases={i: i for i in range(2 * n)},
        compiler_params=pltpu.CompilerParams(has_side_effects=SIDE_EFFECT),
    )(*ins, *lands, send, recv, *after)
    return outs[:n], outs[n:]


def swap_cores(arrs, name):
    n = len(arrs)

    def body(*refs):
        ins, outs = refs[:n], refs[n:2 * n]
        send, recv = refs[2 * n:]
        sibling = (lax.axis_index("x"), lax.axis_index("y"), 1 - lax.axis_index("c"))
        copies = [pltpu.make_async_remote_copy(src_ref=ins[t], dst_ref=outs[t], send_sem=send.at[t], recv_sem=recv.at[t],
                                               device_id=sibling, device_id_type=MESH_ID) for t in range(n)]
        for cp in copies:
            cp.start()
        for cp in copies:
            cp.wait()

    return pl.pallas_call(
        body, name=name, in_specs=[ANY] * n, out_specs=[ANY] * n,
        out_shape=[jax.ShapeDtypeStruct(a.shape, a.dtype) for a in arrs],
        scratch_shapes=[pltpu.SemaphoreType.DMA((n,)), pltpu.SemaphoreType.DMA((n,))],
    )(*arrs)


ELEMENTWISE_BLOCK_BYTES = 1 << 20


def _row_tile(rows, cols):
    want = max(8, ELEMENTWISE_BLOCK_BYTES // (4 * 128 * -(-cols // 128)))
    fits = [t for t in range(8, min(rows, want) + 1, 8) if rows % t == 0]
    return fits[-1] if fits else rows


def sum_chips(recv, own, axis, chip, stacked, l, name):
    _, r, c = recv.shape
    tr = _row_tile(r, c)
    nrt = r // tr

    def body(chip_ref, r_ref, own_ref, stacked_ref, o_ref):
        for k in range(4):
            @pl.when(chip_ref[0] == k)
            def _():
                terms = [own_ref[...] if s == k else r_ref[s] for s in range(4)]
                o_ref[...] = ((terms[0] + terms[1]) + terms[2]) + terms[3]

    own_index = {0: lambda i, q: (q[0] * nrt + i, 0), 1: lambda i, q: (i, q[0]), None: lambda i, q: (i, 0)}[axis]
    return pl.pallas_call(
        body, name=name,
        grid_spec=pltpu.PrefetchScalarGridSpec(
            num_scalar_prefetch=1, grid=(nrt,),
            in_specs=[pl.BlockSpec((4, tr, c), lambda i, q: (0, i, 0)), pl.BlockSpec((tr, c), own_index), ANY],
            out_specs=pl.BlockSpec((None, tr, c), lambda i, q: (l, i, 0))),
        out_shape=jax.ShapeDtypeStruct(stacked.shape, F32), input_output_aliases={3: 0}, compiler_params=_cp(1),
    )(chip, recv, own, stacked)


def adamw(w, ga, gb, m, v, name):
    rows, cols = w.shape
    tr = _row_tile(rows, cols)

    def body(w_ref, ga_ref, gb_ref, m_ref, v_ref, g_ref, d_ref, nm_ref, nv_ref):
        g = ga_ref[...] + gb_ref[...]
        nm = ADAM_B1 * m_ref[...] + (1.0 - ADAM_B1) * g
        nv = ADAM_B2 * v_ref[...] + (1.0 - ADAM_B2) * (g * g)
        m_hat = nm / (1.0 - ADAM_B1 ** ADAM_STEP)
        v_hat = nv / (1.0 - ADAM_B2 ** ADAM_STEP)
        g_ref[...] = g
        nm_ref[...] = nm
        nv_ref[...] = nv
        d_ref[...] = -ADAM_LR * (m_hat / (jnp.sqrt(v_hat) + ADAM_EPS) + ADAM_WD * w_ref[...])

    blk = pl.BlockSpec((tr, cols), lambda i: (i, 0))
    f = jax.ShapeDtypeStruct((rows, cols), F32)
    return pl.pallas_call(
        body, name=name, grid=(rows // tr,), in_specs=[blk] * 5, out_specs=[blk] * 4, out_shape=[f] * 4,
        compiler_params=_cp(1),
    )(w, ga, gb, m, v)


PACK_ALIGN = 1024
PACK_ROWS_ALIGN = 2048


def pack_small(arrs):
    parts = []
    for a in arrs:
        flat = a.reshape(-1)
        pad = (-flat.shape[0]) % PACK_ALIGN
        parts.append(jnp.pad(flat, (0, pad)) if pad else flat)
    rows = sum(p.shape[0] for p in parts) // 128
    parts.append(jnp.zeros(((-rows) % PACK_ROWS_ALIGN * 128,), arrs[0].dtype))
    return jnp.concatenate(parts).reshape(-1, 128)


def unpack_small(packed, shapes):
    out, row = [], 0
    for shape in shapes:
        size = int(np.prod(shape))
        rows = -(-size // PACK_ALIGN) * 8
        out.append(packed[row:row + rows].reshape(-1)[:size].reshape(shape))
        row += rows
    return out


def _norm_epilogue(with_next):
    def epi(acc, res, g_post, *g_pre):
        x_new = acc * lax.rsqrt(jnp.mean(acc * acc, axis=-1, keepdims=True) + EPS) * g_post + res
        if not with_next:
            return acc, x_new
        return acc, x_new, x_new * lax.rsqrt(jnp.mean(x_new * x_new, axis=-1, keepdims=True) + EPS) * g_pre[0]
    return epi


def layer_fwd(x, h1, mem, w_in, rest_of, P, biases, g_next, after=()):
    sv = {"x0": x}
    post = dict(tm=512, tn=D_MODEL)
    proj = mm(h1, w_in, "nn", out_dtypes=[F32], name="mm_w_in", after=after)
    a_out = pool_fwd(proj, P["pool_w"], P["pool_scale"])
    os_, lses = [], []
    for g, (win, dil) in enumerate(DIL_GROUPS):
        o, lse = att_fwd(proj, biases[g], g, dil)
        os_.append(o)
        lses.append(lse)
    b_out, w0, w1, w2 = att_combine(os_, lses)
    s5p = P["s5"]
    hr, hi, y = s5_fwd(proj, s5p[2], s5p[3], s5p[0], s5p[1], s5p[4], s5p[5], P["d_skip"])
    d_out = sgu_fwd(proj, P["sgu_ln_g"], P["sgu_ln_b"], P["w_s"], P["b_s_t"])
    W, after_rest = rest_of("mixer", (a_out, b_out, y, d_out))
    W = dict(W, w_in=w_in)
    c_out = glu_fwd(y, W["w_glu"], P["b_glu"], after=after_rest)
    branches = (a_out, b_out, c_out, d_out)
    merged = merge_fwd(proj, W["gate_b"], branches, W["w_up"])
    t1, x1, h2 = mm(merged, W["w_out"], "nn", tk=1024, out_dtypes=[F32, F32, BF16], name="mm_w_out", extras=(x,),
                    vecs=(P["g_mix_post"], P["g_x_pre"]), epi=_norm_epilogue(True), after=after_rest, **post)
    sv.update(h1=h1, proj=proj, os=os_, lses=lses, wts=(w0, w1, w2), hr=hr, hi=hi, y=y, branches=branches,
              merged=merged, t1=t1, x1=x1)

    mem_n = rms_fwd(mem, P["g_mem"], BF16, "rms_mem")
    q = mm(h2, W["w_cq"], "nn", tm=1024, tn=512, tk=1024, out_dtypes=[BF16], name="mm_w_cq")
    kv = mm(mem_n, W["w_ckv"], "nn", tm=256, tn=1024, tk=1024, out_dtypes=[BF16], name="mm_w_ckv")
    ox = xatt_fwd(q, kv)
    t2, x2, h3 = mm(ox, W["w_co"], "nn", tk=512, out_dtypes=[F32, F32, BF16], name="mm_w_co", extras=(x1,),
                    vecs=(P["g_x_post"], P["g_ff_pre"]), epi=_norm_epilogue(True), **post)
    sv.update(h2=h2, mem_n=mem_n, q=q, kv=kv, ox=ox, t2=t2, x2=x2)

    W_ff, after_ff = rest_of("mlp", h3)
    W = dict(W, **W_ff)
    pre, act = mm(h3, W["w_ff1"], "nn", out_dtypes=[F32, BF16], name="mm_w_ff1",
                  epi=lambda acc: (acc, jnp.square(jnp.maximum(acc, 0.0))), after=after_ff)
    _, after_out = rest_of("out", act)
    if g_next is None:
        (ff, x3), h_next = mm(act, W["w_ff2"], "nn", out_dtypes=[F32, F32], name="mm_w_ff2_last", extras=(x2,),
                              vecs=(P["g_ff_post"],), epi=_norm_epilogue(False), after=after_out), None
    else:
        ff, x3, h_next = mm(act, W["w_ff2"], "nn", out_dtypes=[F32, F32, BF16], name="mm_w_ff2", extras=(x2,),
                            vecs=(P["g_ff_post"], g_next), epi=_norm_epilogue(True), after=after_out)
    sv.update(h3=h3, pre=pre, act=act, ff=ff, W=W)
    return x3, h_next, sv


def _pre_norm_bwd_epilogue(dh, x, add, g):
    r = lax.rsqrt(jnp.mean(x * x, axis=-1, keepdims=True) + EPS)
    xn = x * r
    dxn = dh * g
    return r * (dxn - xn * jnp.mean(dxn * xn, axis=-1, keepdims=True)) + add, jnp.sum(dh * xn, axis=0, keepdims=True)


def layer_bwd(dx, mem, W, P, biases, sv, headsum, emit, after=()):
    G = {}
    dff, G["g_ff_post"] = rms_bwd(sv["ff"], P["g_ff_post"], dx, BF16, "rms_post_bwd", after=after)
    G["w_ff2"] = mm(sv["act"], dff, "tn", out_dtypes=[F32], name="mm_dw_ff2")
    dpre = mm(dff, W["w_ff2"], "nt", out_dtypes=[BF16], name="mm_dact", extras=(sv["pre"],),
              epi=lambda acc, pre: (acc * (2.0 * jnp.maximum(pre, 0.0)),))
    G["w_ff1"] = mm(sv["h3"], dpre, "tn", out_dtypes=[F32], name="mm_dw_ff1")
    sent = emit(("w_ff1", "w_ff2"), G)
    pre_bwd = dict(out_dtypes=[F32], epi=_pre_norm_bwd_epilogue, n_sums=1)
    dx2, G["g_ff_pre"] = mm(dpre, W["w_ff1"], "nt", name="mm_dh3", extras=(sv["x2"], dx), vecs=(P["g_ff_pre"],),
                            after=sent, **pre_bwd)
    dt2, G["g_x_post"] = rms_bwd(sv["t2"], P["g_x_post"], dx2, BF16, "rms_post_bwd")
    G["w_co"] = mm(sv["ox"], dt2, "tn", tm=512, tn=1024, tk=1024, out_dtypes=[F32], name="mm_dw_co")
    dox = mm(dt2, W["w_co"], "nt", tm=1024, tn=512, tk=1024, out_dtypes=[BF16], name="mm_dox")
    dq, dkv = xatt_bwd(sv["q"], sv["kv"], dox)
    G["w_cq"] = mm(sv["h2"], dq, "tn", tm=1024, tn=512, tk=1024, out_dtypes=[F32], name="mm_dw_cq")
    G["w_ckv"] = mm(sv["mem_n"], dkv, "tn", tm=1024, tn=1024, tk=256, out_dtypes=[F32], name="mm_dw_ckv")
    dmem_n = mm(dkv, W["w_ckv"], "nt", tm=256, tn=1024, tk=1024, out_dtypes=[F32], name="mm_dmem")
    _, G["g_mem"] = rms_bwd(mem, P["g_mem"], dmem_n, BF16, "rms_mem_bwd")
    dx1, G["g_x_pre"] = mm(dq, W["w_cq"], "nt", name="mm_dh2", extras=(sv["x1"], dx2), vecs=(P["g_x_pre"],),
                           **pre_bwd)
    proj = sv["proj"]
    dt1, G["g_mix_post"] = rms_bwd(sv["t1"], P["g_mix_post"], dx1, BF16, "rms_post_bwd")
    G["w_out"] = mm(sv["merged"], dt1, "tn", tm=1024, tn=1024, tk=1024, out_dtypes=[F32], name="mm_dw_out")
    dmerged = mm(dt1, W["w_out"], "nt", tm=1024, tn=1024, tk=1024, out_dtypes=[F32], name="mm_dmerged")
    dgates, dups, G["gate_b"] = merge_bwd(proj, W["gate_b"], sv["branches"], W["w_up"], dmerged)
    dbr, dwup = [], []
    for i in range(4):
        dbr.append(mm(dups[i], W["w_up"][i], "nt", tm=1024, tn=512, tk=1024, out_dtypes=[F32], name="mm_dbranch"))
        dwup.append(mm(sv["branches"][i], dups[i], "tn", tm=512, tn=1024, tk=1024, out_dtypes=[F32], name="mm_dw_up"))
    G["w_up"] = jnp.concatenate(dwup, axis=0)
    d_pool, G["pool_w"], G["pool_scale"] = pool_bwd(proj, P["pool_w"], P["pool_scale"], dbr[0])
    cbar = att_combine_bwd(dbr[1], sv["os"], sv["wts"], headsum)
    dqs, dks, dvs, dbias = [], [], [], []
    for g, (win, dil) in enumerate(DIL_GROUPS):
        dq_g, dk_g, dv_g, db_g = att_bwd(proj, biases[g], sv["lses"][g], sv["wts"][g], dbr[1], cbar, g, dil)
        dqs.append(dq_g)
        dks.append(dk_g)
        dvs.append(dv_g)
        dbias.append(db_g)
    G["att_bias"] = dbias
    s5p = P["s5"]
    dy, G["w_glu"], G["b_glu"] = glu_bwd(sv["y"], W["w_glu"], P["b_glu"], dbr[2])
    d_ssm, dbre, dbim, dar, dai, dcre, dcim, G["d_skip"] = s5_bwd(
        proj, sv["hr"], sv["hi"], dy, s5p[2], s5p[3], s5p[0], s5p[1], s5p[4], s5p[5], P["d_skip"])
    G["s5"] = (dar, dai, dbre, dbim, dcre, dcim)
    dzu, dzv, G["sgu_ln_g"], G["sgu_ln_b"], G["w_s"], G["b_s_t"] = sgu_bwd(
        proj, P["sgu_ln_g"], P["sgu_ln_b"], P["w_s"], P["b_s_t"], dbr[3])
    d_qkv = [d.astype(BF16) for d in dqs + dks + dvs]
    dproj = jnp.concatenate([d_pool] + d_qkv + [d_ssm, dzu, dzv] + list(dgates), axis=1)
    sent = emit(("gate_b", "w_glu", "w_up", "w_out", "w_cq", "w_ckv", "w_co"), G)
    G["w_in"] = mm(sv["h1"], dproj, "tn", out_dtypes=[F32], name="mm_dw_in", after=sent)
    sent = emit(("w_in",), G)
    dx0, G["g_mix_pre"] = mm(dproj, W["w_in"], "nt", name="mm_dh1", extras=(sv["x0"], dx1), vecs=(P["g_mix_pre"],),
                             after=sent, **pre_bwd)
    return dx0, G


def _as3d(name, a):
    shape2d, axis = SHARDED[name]
    rows, cols = shape2d
    if axis == 0:
        rows //= 4
    else:
        cols //= 4
    return a.reshape(DEPTH, rows, cols)


def kernel(x, mem, rel_bias, g_mix_pre, g_mix_post, w_in, gate_b, pool_w, pool_scale, a_re, a_im, log_dt, b_re, b_im, c_re, c_im, d_skip, w_glu, b_glu, sgu_ln_g, sgu_ln_b, w_s, b_s, w_up, w_out, g_x_pre, g_x_post, g_mem, w_cq, w_ckv, w_co, g_ff_pre, g_ff_post, w_ff1, w_ff2, loss_target, m_rel_bias, m_g_mix_pre, m_g_mix_post, m_w_in, m_gate_b, m_pool_w, m_pool_scale, m_a_re, m_a_im, m_log_dt, m_b_re, m_b_im, m_c_re, m_c_im, m_d_skip, m_w_glu, m_b_glu, m_sgu_ln_g, m_sgu_ln_b, m_w_s, m_b_s, m_w_up, m_w_out, m_g_x_pre, m_g_x_post, m_g_mem, m_w_cq, m_w_ckv, m_w_co, m_g_ff_pre, m_g_ff_post, m_w_ff1, m_w_ff2, v_rel_bias, v_g_mix_pre, v_g_mix_post, v_w_in, v_gate_b, v_pool_w, v_pool_scale, v_a_re, v_a_im, v_log_dt, v_b_re, v_b_im, v_c_re, v_c_im, v_d_skip, v_w_glu, v_b_glu, v_sgu_ln_g, v_sgu_ln_b, v_w_s, v_b_s, v_w_up, v_w_out, v_g_x_pre, v_g_x_post, v_g_mem, v_w_cq, v_w_ckv, v_w_co, v_g_ff_pre, v_g_ff_post, v_w_ff1, v_w_ff2):
    env = dict(locals())
    weights = {n: env[n] for n in WEIGHT_NAMES}
    mom_m = {n: env["m_" + n] for n in WEIGHT_NAMES}
    mom_v = {n: env["v_" + n] for n in WEIGHT_NAMES}
    x2d = x.reshape(x.shape[1], D_MODEL)
    mem2d = mem.reshape(N_MEM, D_MODEL)
    target = loss_target.reshape(x2d.shape)

    axis_of = {n: SHARDED[n][1] for n in SHARDED_NAMES}
    chip = _chip_index().astype(jnp.int32).reshape(1)
    rest_names = [n for n in SHARDED_NAMES if n != "w_in"]

    def gather_start(l, names, tag, after=()):
        shards = [_as3d(n, weights[n])[l].astype(F32 if n == "gate_b" else MXU_DTYPE) for n in names]
        ax = [axis_of[n] for n in names]
        lands = [lax.empty(tuple(4 * d if i == a else d for i, d in enumerate(s.shape)), s.dtype)
                 for s, a in zip(shards, ax)]
        return (names, ax, tag) + chips_start(shards, lands, ax, "gather", f"gather_start_{tag}", after=after)

    def gather_arrive(started, after):
        names, ax, tag, send, recv, shards, lands, _ = started
        shards, lands = chips_wait(send, recv, shards, lands, ax, "gather", f"gather_wait_{tag}", after=after)
        return (names, ax, tag) + chips_start(shards, lands, ax, "forward", f"gather_forward_{tag}")

    def gather_finish(arrived, after=()):
        names, ax, tag, send, recv, shards, lands, _ = arrived
        _, lands = chips_wait(send, recv, shards, lands, ax, "forward", f"gather_landed_{tag}", after=after)
        W = dict(zip(names, lands))
        if "w_up" in W:
            W["w_up"] = W["w_up"].reshape(4, W_BRANCH, D_MODEL)
        return W

    def gather_wait(started, after):
        return gather_finish(gather_arrive(started, after))

    biases = [att_bias(rel_bias, g, dil) for g, (_, dil) in enumerate(DIL_GROUPS)]
    lanes = np.arange(W_BRANCH) // ATT_HEAD_DIM
    headsum = jnp.asarray(lanes[:, None] == lanes[None, :], dtype=BF16)

    def small_params(l, s5_prepared):
        vec = lambda a: a[l].reshape(1, -1)
        return {
            "g_mix_pre": vec(g_mix_pre), "g_mix_post": vec(g_mix_post), "g_x_pre": vec(g_x_pre), "g_x_post": vec(g_x_post),
            "g_mem": vec(g_mem), "g_ff_pre": vec(g_ff_pre), "g_ff_post": vec(g_ff_post), "pool_w": pool_w[l],
            "pool_scale": vec(pool_scale), "d_skip": vec(d_skip), "b_glu": vec(b_glu), "sgu_ln_g": vec(sgu_ln_g),
            "sgu_ln_b": vec(sgu_ln_b), "w_s": w_s[l], "b_s_t": b_s[l].T, "s5": s5_prepared,
        }

    Ws, Ps, saved, s5_vjps = [], [], [], []
    xl = x2d
    hl = rms_fwd(x2d, g_mix_pre[0].reshape(1, -1), BF16, "rms_pre")
    flying = {"next": gather_start(0, ["w_in"], "0_w_in")}
    for l in range(DEPTH):
        s5_prepared, s5_vjp = jax.vjp(s5_params, a_re[l], a_im[l], log_dt[l], b_re[l], b_im[l], c_re[l], c_im[l])
        token_of = lambda started: (started[7],)
        if l == 0:
            w_in_l = gather_wait(flying["next"], [*biases, hl])["w_in"]
            flying["rest"] = gather_start(0, rest_names, "0_rest", after=[w_in_l])
            first_after = token_of(flying["rest"])

            W_l = None
        else:
            W_l = gather_finish(flying["next"], [xl])
            w_in_l, first_after = W_l["w_in"], ()
            if l + 1 < DEPTH:
                flying["next"] = gather_start(l + 1, SHARDED_NAMES, str(l + 1), after=[w_in_l])
                first_after = token_of(flying["next"])

        def rest_of(stage, value, l=l, W_l=W_l):
            if stage == "mixer" and l == 0:
                W = gather_wait(flying["rest"], list(value))
                flying["next"] = gather_start(1, SHARDED_NAMES, "1", after=[W["w_out"]])
                return W, token_of(flying["next"])
            if stage == "mixer":
                return W_l, ()
            if stage == "out" and l + 1 < DEPTH:
                flying["next"] = gather_arrive(flying["next"], [value])
                return {}, token_of(flying["next"])
            return {}, ()
        P = small_params(l, s5_prepared)
        g_next = g_mix_pre[l + 1].reshape(1, -1) if l + 1 < DEPTH else None
        xl, hl, sv = layer_fwd(xl, hl, mem2d, w_in_l, rest_of, P, biases, g_next, after=first_after)
        Ws.append(sv["W"])
        Ps.append(P)
        saved.append(sv)
        s5_vjps.append(s5_vjp)
    loss_local, dx = loss_and_grad(xl, target)
    loss = lax.psum(loss_local, ("x", "y", "c"))

    scattered = []

    def scatter_start(l, names, srcs):
        ax = [axis_of.get(n) for n in names]
        lands = []
        for s, a in zip(srcs, ax):
            r, c = s.shape
            lands.append(lax.empty((4, r // 4 if a == 0 else r, c // 4 if a == 1 else c), F32))
        tag = f"{l}_{names[0]}"
        send, recv, srcs, lands, token = chips_start(srcs, lands, ax, "scatter", f"grads_start_{tag}")
        scattered.append((l, names, ax, tag, send, recv, srcs, lands))
        return (token,)

    grads = [None] * DEPTH
    for l in reversed(range(DEPTH)):
        emit = lambda names, G, l=l: scatter_start(l, list(names), [G[n] for n in names])
        dx, grads[l] = layer_bwd(dx, mem2d, Ws[l], Ps[l], biases, saved[l], headsum, emit)
    grad_x = dx.reshape(x.shape)

    rep = {}
    stack = lambda key, shape: jnp.stack([grads[l][key] for l in range(DEPTH)]).reshape(shape)
    for n in ("g_mix_pre", "g_mix_post", "g_x_pre", "g_x_post", "g_mem", "g_ff_pre", "g_ff_post"):
        rep[n] = stack(n, (DEPTH, D_MODEL))
    for n in ("pool_scale", "d_skip", "b_glu", "sgu_ln_g", "sgu_ln_b"):
        rep[n] = stack(n, (DEPTH, W_BRANCH))
    rep["pool_w"] = stack("pool_w", pool_w.shape)
    rep["w_s"] = stack("w_s", w_s.shape)
    rep["b_s"] = jnp.stack([grads[l]["b_s_t"][:, :4].T for l in range(DEPTH)])
    s5_grads = [s5_vjps[l](tuple(grads[l]["s5"])) for l in range(DEPTH)]
    for i, n in enumerate(("a_re", "a_im", "log_dt", "b_re", "b_im", "c_re", "c_im")):
        rep[n] = jnp.stack([s5_grads[l][i] for l in range(DEPTH)])
    dbias = [sum(grads[l]["att_bias"][g] for l in range(DEPTH)) for g in range(len(DIL_GROUPS))]
    rep["rel_bias"] = jnp.concatenate([att_bias_grad(dbias[g], dil) for g, (_, dil) in enumerate(DIL_GROUPS)], axis=1)
    rep_shapes = [weights[n].shape for n in REPLICATED_NAMES]
    packed_g = pack_small([rep[n] for n in REPLICATED_NAMES])

    small_sent = scatter_start(0, ["small"], [packed_g])
    stacked = {}

    def collect(record, after):
        l, names, ax, tag, send, recv, srcs, lands = record
        srcs, lands = chips_wait(send, recv, srcs, lands, ax, "scatter", f"grads_wait_{tag}", after=after)
        for n, own, arrived, a in zip(names, srcs, lands, ax):
            if n not in stacked:
                stacked[n] = lax.empty((1 if n == "small" else DEPTH,) + arrived.shape[1:], F32)
            stacked[n] = sum_chips(arrived, own, a, chip, stacked[n], 0 if n == "small" else l, "sum_chips")

    out_g, out_d, out_m, out_v = {}, {}, {}, {}

    def update(names, tag):
        partial = [stacked[n].reshape(-1, stacked[n].shape[-1]) for n in names]
        other = swap_cores(partial, f"swap_cores_{tag}")
        for n, mine, theirs in zip(names, partial, other):
            if n == "small":
                for name, ga, gb in zip(REPLICATED_NAMES, unpack_small(mine, rep_shapes), unpack_small(theirs, rep_shapes)):
                    rows_of = lambda a: a.reshape(-1, a.shape[-1])
                    res = adamw(rows_of(weights[name]), rows_of(ga), rows_of(gb), rows_of(mom_m[name]),
                                rows_of(mom_v[name]), "adamw_small")
                    out_g[name], out_d[name], out_m[name], out_v[name] = [r.reshape(weights[name].shape) for r in res]
            else:
                flat = lambda a: a.reshape(mine.shape)
                res = adamw(flat(weights[n]), mine, theirs, flat(mom_m[n]), flat(mom_v[n]), "adamw")
                out_g[n], out_d[n], out_m[n], out_v[n] = [r.reshape(weights[n].shape) for r in res]

    late = [r for r in scattered if r[1] == ["small"] or (r[0] == 0 and r[1] == ["w_in"])]
    for record in scattered:
        if not any(record is r for r in late):
            collect(record, [dx, *small_sent])
    update(rest_names, "rest")
    collect(late[0], [out_d[n] for n in rest_names])
    update(["w_in"], "w_in")
    collect(late[1], [out_d["w_in"]])
    update(["small"], "small")

    return (loss, grad_x, *[out_g[n] for n in WEIGHT_NAMES], *[out_d[n] for n in WEIGHT_NAMES],
            *[out_m[n] for n in WEIGHT_NAMES], *[out_v[n] for n in WEIGHT_NAMES])
```

```python
import functools
import math

import numpy as np
import jax
import jax.numpy as jnp
from jax import lax
from jax.experimental import pallas as pl
from jax.experimental.pallas import tpu as pltpu

F32 = jnp.float32
BF16 = jnp.bfloat16
MXU_DTYPE = jnp.bfloat16
MESH_ID = pl.DeviceIdType.MESH
VMEM_LIMIT_BYTES = 56 * 1024 * 1024

D_MODEL = 1024
DEPTH = 4
N_MEM = 256
W_BRANCH = 512
POOL_WINDOWS = (2, 4, 8, 16)
POOL_HALO = 16
DIL_GROUPS = ((128, 1), (512, 4), (2048, 16))
BAND = 128
ATT_HEADS = 8
ATT_HEAD_DIM = 64
SSM_GROUP = 16
SSM_GROUPS = 32
SSM_STATE = 64
SSM_COLS = SSM_GROUPS * SSM_STATE
SSM_T = 512
SGU_CHUNK = 128
X_HEADS = 4
X_HEAD_DIM = 128
D_FF = 4096
REL_BUCKETS = 32
REL_MAX_DIST = 2048
EPS = 1e-6
NEG_INF = -1e30
OFF_POOL = 0
OFF_ATT = 512
OFF_SSM = OFF_ATT + 9 * W_BRANCH
OFF_SGU = OFF_SSM + W_BRANCH
OFF_GATE = OFF_SGU + 2 * W_BRANCH
IN_WIDTH = OFF_GATE + 4 * D_MODEL

ADAM_LR = 0.001
ADAM_B1 = 0.9
ADAM_B2 = 0.999
ADAM_EPS = 1e-08
ADAM_WD = 0.01
ADAM_STEP = 10

GELU_C = math.sqrt(2.0 / math.pi)

WEIGHT_NAMES = ['rel_bias', 'g_mix_pre', 'g_mix_post', 'w_in', 'gate_b', 'pool_w', 'pool_scale', 'a_re', 'a_im',
                'log_dt', 'b_re', 'b_im', 'c_re', 'c_im', 'd_skip', 'w_glu', 'b_glu', 'sgu_ln_g', 'sgu_ln_b',
                'w_s', 'b_s', 'w_up', 'w_out', 'g_x_pre', 'g_x_post', 'g_mem', 'w_cq', 'w_ckv', 'w_co',
                'g_ff_pre', 'g_ff_post', 'w_ff1', 'w_ff2']
SHARDED = {
    'w_in': ((D_MODEL, IN_WIDTH), 1),
    'gate_b': ((4, D_MODEL), 1),
    'w_glu': ((W_BRANCH, W_BRANCH), 0),
    'w_up': ((4 * W_BRANCH, D_MODEL), 1),
    'w_out': ((D_MODEL, D_MODEL), 0),
    'w_cq': ((D_MODEL, W_BRANCH), 0),
    'w_ckv': ((D_MODEL, D_MODEL), 0),
    'w_co': ((W_BRANCH, D_MODEL), 1),
    'w_ff1': ((D_MODEL, D_FF), 1),
    'w_ff2': ((D_FF, D_MODEL), 0),
}
SHARDED_NAMES = list(SHARDED)
REPLICATED_NAMES = [n for n in WEIGHT_NAMES if n not in SHARDED]


def _cp(n_axes):
    return pltpu.CompilerParams(dimension_semantics=("arbitrary",) * n_axes, vmem_limit_bytes=VMEM_LIMIT_BYTES)


def _dot(a, b, dims="nn"):
    cd = {"nn": ((1,), (0,)), "nt": ((1,), (1,)), "tn": ((0,), (0,))}[dims]
    return lax.dot_general(a.astype(MXU_DTYPE), b.astype(MXU_DTYPE), (cd, ((), ())), preferred_element_type=F32)


def _gelu(x):
    return 0.5 * x * (1.0 + jnp.tanh(GELU_C * (x + 0.044715 * (x * x * x))))


def _gelu_grad(x):
    t = jnp.tanh(GELU_C * (x + 0.044715 * (x * x * x)))
    return 0.5 * (1.0 + t) + 0.5 * x * (1.0 - t * t) * (GELU_C * (1.0 + 3.0 * 0.044715 * (x * x)))


def _sigmoid(x):
    return 1.0 / (1.0 + jnp.exp(-x))


MM_TILES = {
    "mm_w_in": (2048, 1536, 1024), "mm_dw_in": (1024, 1536, 2048), "mm_dh1": (1024, 1024, 1536),
    "mm_w_ff1": (2048, 1024, 1024), "mm_w_ff2": (1024, 1024, 2048), "mm_w_ff2_last": (1024, 1024, 2048),
    "mm_dw_ff2": (1024, 1024, 2048), "mm_dact": (2048, 1024, 1024), "mm_dw_ff1": (1024, 1024, 2048),
    "mm_dh3": (1024, 1024, 2048), "mm_dh2": (1024, 1024, 512),
}


def mm(a, b, dims, *, out_dtypes, name, tm=None, tn=None, tk=None, extras=(), vecs=(), epi=None, n_sums=0, after=()):
    if dims == "tn":
        K, M = a.shape
        N = b.shape[1]
    else:
        M, K = a.shape
        N = b.shape[1] if dims == "nn" else b.shape[0]
    if tm is None:
        tm, tn, tk = MM_TILES[name]
    tm, tn, tk = min(tm, M), min(tn, N), min(tk, K)
    assert M % tm == 0 and N % tn == 0 and K % tk == 0, (name, M, N, K, tm, tn, tk)
    assert n_sums == 0 or tn == N, name
    nk = K // tk
    ne, no = len(extras) + len(vecs), len(out_dtypes)
    if epi is None:
        epi = lambda acc: (acc,)
    a_spec = (pl.BlockSpec((tk, tm), lambda i, j, k: (k, i)) if dims == "tn"
              else pl.BlockSpec((tm, tk), lambda i, j, k: (i, k)))
    b_spec = (pl.BlockSpec((tn, tk), lambda i, j, k: (j, k)) if dims == "nt"
              else pl.BlockSpec((tk, tn), lambda i, j, k: (k, j)))
    mn_spec = pl.BlockSpec((tm, tn), lambda i, j, k: (i, j))
    vec_spec = pl.BlockSpec((1, tn), lambda i, j, k: (0, j))

    def body(a_ref, b_ref, *rest):
        first_out = ne + len(after)
        extra_refs, out_refs = rest[:ne], rest[first_out:first_out + no]
        sum_refs = rest[first_out + no:first_out + no + n_sums]
        part = _dot(a_ref[...], b_ref[...], dims)

        def finish(acc):
            results = epi(acc, *[e[...] for e in extra_refs])
            for o_ref, r in zip(out_refs, results[:no]):
                o_ref[...] = r.astype(o_ref.dtype)
            for s_ref, r in zip(sum_refs, results[no:]):
                @pl.when(pl.program_id(0) == 0)
                def _():
                    s_ref[...] = r

                @pl.when(pl.program_id(0) > 0)
                def _():
                    s_ref[...] += r

        if nk == 1:
            finish(part)
        else:
            acc_ref = rest[-1]
            k = pl.program_id(2)

            @pl.when(k == 0)
            def _():
                acc_ref[...] = part

            @pl.when(k > 0)
            def _():
                acc_ref[...] += part

            @pl.when(k == nk - 1)
            def _():
                finish(acc_ref[...])

    outs = pl.pallas_call(
        body, name=name, grid=(M // tm, N // tn, nk),
        in_specs=[a_spec, b_spec] + [mn_spec] * len(extras) + [vec_spec] * len(vecs) + [ANY] * len(after),
        out_specs=[mn_spec] * no + [vec_spec] * n_sums,
        out_shape=[jax.ShapeDtypeStruct((M, N), dt) for dt in out_dtypes] + [jax.ShapeDtypeStruct((1, N), F32)] * n_sums,
        scratch_shapes=[pltpu.VMEM((tm, tn), F32)] if nk > 1 else [],
        compiler_params=_cp(3),
    )(a, b, *extras, *vecs, *after)
    return outs[0] if no + n_sums == 1 else outs


ROW_TILE = 512


def rms_fwd(x, g, out_dtype, name, res=None):
    M, D = x.shape
    tm = min(ROW_TILE, M)

    def body(x_ref, g_ref, *rest):
        o_ref = rest[-1]
        xf = x_ref[...]
        y = xf * lax.rsqrt(jnp.mean(xf * xf, axis=-1, keepdims=True) + EPS) * g_ref[...]
        if res is not None:
            y = y + rest[0][...]
        o_ref[...] = y.astype(o_ref.dtype)

    row = pl.BlockSpec((tm, D), lambda i: (i, 0))
    return pl.pallas_call(
        body, name=name, grid=(M // tm,),
        in_specs=[row, pl.BlockSpec((1, D), lambda i: (0, 0))] + ([row] if res is not None else []),
        out_specs=row, out_shape=jax.ShapeDtypeStruct((M, D), out_dtype), compiler_params=_cp(1),
    )(x, g, *([res] if res is not None else []))


def rms_bwd(x, g, dy, dx_dtype, name, add=None, after=()):
    M, D = x.shape
    tm = min(ROW_TILE, M)

    def body(x_ref, g_ref, dy_ref, *rest):
        dx_ref, dg_ref = rest[-2], rest[-1]
        xf = x_ref[...]
        dyf = dy_ref[...].astype(F32)
        r = lax.rsqrt(jnp.mean(xf * xf, axis=-1, keepdims=True) + EPS)
        xn = xf * r
        dxn = dyf * g_ref[...]
        dx = r * (dxn - xn * jnp.mean(dxn * xn, axis=-1, keepdims=True))
        if add is not None:
            dx = dx + rest[0][...]
        dx_ref[...] = dx.astype(dx_ref.dtype)
        dg = jnp.sum(dyf * xn, axis=0, keepdims=True)

        @pl.when(pl.program_id(0) == 0)
        def _():
            dg_ref[...] = dg

        @pl.when(pl.program_id(0) > 0)
        def _():
            dg_ref[...] += dg

    row = pl.BlockSpec((tm, D), lambda i: (i, 0))
    vec = pl.BlockSpec((1, D), lambda i: (0, 0))
    return pl.pallas_call(
        body, name=name, grid=(M // tm,),
        in_specs=[row, vec, row] + ([row] if add is not None else []) + [ANY] * len(after),
        out_specs=[row, vec],
        out_shape=[jax.ShapeDtypeStruct((M, D), dx_dtype), jax.ShapeDtypeStruct((1, D), F32)],
        compiler_params=_cp(1),
    )(x, g, dy, *([add] if add is not None else []), *after)


def loss_and_grad(y, target):
    M, D = y.shape
    tm = ROW_TILE

    def body(y_ref, t_ref, part_ref, dy_ref):
        e = y_ref[...] - t_ref[...]
        dy_ref[...] = e / D
        part_ref[...] = jnp.broadcast_to(0.5 * jnp.sum(jnp.mean(e * e, axis=-1, keepdims=True), axis=0, keepdims=True),
                                         (8, 128))

    row = pl.BlockSpec((tm, D), lambda i: (i, 0))
    part, dy = pl.pallas_call(
        body, name="loss", grid=(M // tm,), in_specs=[row, row],
        out_specs=[pl.BlockSpec((8, 128), lambda i: (i, 0)), row],
        out_shape=[jax.ShapeDtypeStruct((8 * (M // tm), 128), F32), jax.ShapeDtypeStruct((M, D), F32)],
        compiler_params=_cp(1),
    )(y, target)
    return jnp.sum(part[::8, 0]), dy


POOL_ROWS = 512


def _pool_window_sum(xw, gi, roll_of):
    s1 = xw + pltpu.roll(xw, roll_of(1), 0)
    s2 = s1 + pltpu.roll(s1, roll_of(2), 0)
    s3 = s2 + pltpu.roll(s2, roll_of(4), 0)
    s4 = s3 + pltpu.roll(s3, roll_of(8), 0)
    return jnp.where(gi == 0, s1, jnp.where(gi == 1, s2, jnp.where(gi == 2, s3, s4)))


def _pool_cnt(i, gi):
    rows = lax.broadcasted_iota(jnp.int32, (POOL_ROWS, 128), 0) + i * POOL_ROWS
    w = jnp.where(gi == 0, 2, jnp.where(gi == 1, 4, jnp.where(gi == 2, 8, 16)))
    return jnp.minimum(rows + 1, w).astype(F32)


def pool_fwd(proj, pool_w, scale):
    S = proj.shape[0]
    nchunk = S // POOL_ROWS
    slab = POOL_ROWS + POOL_HALO

    def body(x_ref, w_ref, sc_ref, o_ref, pad_ref):
        gi = pl.program_id(0)
        pad_ref[0:POOL_HALO, :] = jnp.zeros((POOL_HALO, 128), F32)
        pad_ref[POOL_HALO:, :] = x_ref[...]
        for i in range(nchunk):
            xw = pad_ref[i * POOL_ROWS:i * POOL_ROWS + slab, :]
            ssum = _pool_window_sum(xw, gi, lambda d: d)[POOL_HALO:, :]
            p = ssum / _pool_cnt(i, gi) - xw[POOL_HALO:, :]
            o_ref[i * POOL_ROWS:(i + 1) * POOL_ROWS, :] = (_dot(p, w_ref[...]) * sc_ref[...]).astype(o_ref.dtype)

    return pl.pallas_call(
        body, name="pool_fwd", grid=(4,),
        in_specs=[pl.BlockSpec((S, 128), lambda g: (0, OFF_POOL // 128 + g)),
                  pl.BlockSpec((None, 128, 128), lambda g: (g, 0, 0)),
                  pl.BlockSpec((1, 128), lambda g: (0, g))],
        out_specs=pl.BlockSpec((S, 128), lambda g: (0, g)),
        out_shape=jax.ShapeDtypeStruct((S, W_BRANCH), BF16),
        scratch_shapes=[pltpu.VMEM((S + POOL_HALO, 128), F32)],
        compiler_params=_cp(1),
    )(proj, pool_w, scale)


def pool_bwd(proj, pool_w, scale, dy):
    S = proj.shape[0]
    nchunk = S // POOL_ROWS
    slab = POOL_ROWS + POOL_HALO

    def body(x_ref, w_ref, sc_ref, dy_ref, dx_ref, dw_ref, dsc_ref, pad_ref, pad2_ref, dp_ref):
        gi = pl.program_id(0)
        pad_ref[0:POOL_HALO, :] = jnp.zeros((POOL_HALO, 128), F32)
        pad_ref[POOL_HALO:, :] = x_ref[...]
        pad2_ref[S:, :] = jnp.zeros((POOL_HALO, 128), F32)
        dw = jnp.zeros((128, 128), F32)
        dsc = jnp.zeros((1, 128), F32)
        for i in range(nchunk):
            xw = pad_ref[i * POOL_ROWS:i * POOL_ROWS + slab, :]
            cnt = _pool_cnt(i, gi)
            p = _pool_window_sum(xw, gi, lambda d: d)[POOL_HALO:, :] / cnt - xw[POOL_HALO:, :]
            dyc = dy_ref[i * POOL_ROWS:(i + 1) * POOL_ROWS, :]
            dsc = dsc + jnp.sum(dyc * _dot(p, w_ref[...]), axis=0, keepdims=True)
            dys = dyc * sc_ref[...]
            dw = dw + _dot(p, dys, "tn")
            dp = _dot(dys, w_ref[...], "nt")
            dp_ref[i * POOL_ROWS:(i + 1) * POOL_ROWS, :] = dp
            pad2_ref[i * POOL_ROWS:(i + 1) * POOL_ROWS, :] = dp / cnt
        dw_ref[...] = dw
        dsc_ref[...] = dsc
        for i in range(nchunk):
            xw = pad2_ref[i * POOL_ROWS:i * POOL_ROWS + slab, :]
            fsum = _pool_window_sum(xw, gi, lambda d: slab - d)[:POOL_ROWS, :]
            rows = slice(i * POOL_ROWS, (i + 1) * POOL_ROWS)
            dx_ref[rows, :] = (fsum - dp_ref[rows, :]).astype(dx_ref.dtype)

    return pl.pallas_call(
        body, name="pool_bwd", grid=(4,),
        in_specs=[pl.BlockSpec((S, 128), lambda g: (0, OFF_POOL // 128 + g)),
                  pl.BlockSpec((None, 128, 128), lambda g: (g, 0, 0)),
                  pl.BlockSpec((1, 128), lambda g: (0, g)),
                  pl.BlockSpec((S, 128), lambda g: (0, g))],
        out_specs=[pl.BlockSpec((S, 128), lambda g: (0, g)),
                   pl.BlockSpec((None, 128, 128), lambda g: (g, 0, 0)),
                   pl.BlockSpec((1, 128), lambda g: (0, g))],
        out_shape=[jax.ShapeDtypeStruct((S, W_BRANCH), BF16), jax.ShapeDtypeStruct((4, 128, 128), F32),
                   jax.ShapeDtypeStruct((1, W_BRANCH), F32)],
        scratch_shapes=[pltpu.VMEM((S + POOL_HALO, 128), F32), pltpu.VMEM((S + POOL_HALO, 128), F32),
                        pltpu.VMEM((S, 128), F32)],
        compiler_params=_cp(1),
    )(proj, pool_w, scale, dy)


def _t5_bucket(n):
    exact = REL_BUCKETS // 2
    nf = np.maximum(n, 1).astype(np.float32)
    large = exact + (np.log(nf / exact) / np.log(REL_MAX_DIST / exact) * (REL_BUCKETS - exact)).astype(np.int32)
    large = np.minimum(large, REL_BUCKETS - 1)
    return np.where(n < exact, n, large).astype(np.int32)


def _band_onehot(dil):
    i = np.arange(BAND)[:, None]
    kk = np.arange(2 * BAND)[None, :]
    dist = BAND + i - kk
    local = (dist >= 0) & (dist <= BAND)
    bucket = _t5_bucket(np.clip(dist, 0, BAND) * dil)
    onehot = (bucket.reshape(-1, 1) == np.arange(REL_BUCKETS)[None, :]).astype(np.float32)
    return onehot, local


def att_bias(rel_bias, g, dil):
    onehot, local = _band_onehot(dil)
    tab = jnp.dot(jnp.asarray(onehot), rel_bias[:, g * ATT_HEADS:(g + 1) * ATT_HEADS], precision=lax.Precision.HIGHEST)
    bias = tab.reshape(BAND, 2 * BAND, ATT_HEADS).transpose(2, 0, 1)
    return jnp.where(jnp.asarray(local)[None], bias, NEG_INF)


def att_bias_grad(dbias, dil):
    onehot, _ = _band_onehot(dil)
    flat = dbias.transpose(1, 2, 0).reshape(BAND * 2 * BAND, ATT_HEADS)
    return jnp.dot(jnp.asarray(onehot).T, flat, precision=lax.Precision.HIGHEST)


def _head_lanes():
    return lax.broadcasted_iota(jnp.int32, (BAND, 128), 1) < ATT_HEAD_DIM


def _att_cols(part, g, hp):
    return (OFF_ATT + part * 3 * W_BRANCH + g * W_BRANCH) // 128 + hp


def _att_pair(q, k, v, bias, lse_b, do, delta_b, hh, head0, mask=None):
    sel = head0 if hh == 0 else jnp.logical_not(head0)
    s = _dot(jnp.where(sel, q, 0.0), k, "nt") * (ATT_HEAD_DIM ** -0.5) + bias
    if mask is not None:
        s = jnp.where(mask, NEG_INF, s)
    c = hh * ATT_HEAD_DIM
    p = jnp.exp(s - lse_b[:, c:c + 1])
    dp = _dot(jnp.where(sel, do, 0.0), v, "nt")
    return p, p * (dp - delta_b[:, c:c + 1])


ATT_BLOCKS = {1: 32, 4: 8, 16: 2}


def _att_rows(r, i, d):
    return pl.ds(r + d * BAND * i, BAND, stride=d) if d > 1 else pl.ds(BAND * i, BAND)


def _att_specs(g, d, nq):
    ch, pb = BAND * d * nq, BAND * d
    cur = lambda part: pl.BlockSpec((ch, 128), lambda hp, n: (n, _att_cols(part, g, hp)))
    prev = lambda part: pl.BlockSpec((pb, 128), lambda hp, n: (jnp.maximum(n * nq - 1, 0), _att_cols(part, g, hp)))
    return [cur(0), cur(1), prev(1), cur(2), prev(2)]


def _att_keys(cur_ref, prev_ref, r, i, d):
    before = cur_ref[_att_rows(r, i - 1, d), :] if i > 0 else prev_ref[_att_rows(r, 0, d), :]
    return jnp.concatenate([before, cur_ref[_att_rows(r, i, d), :]], axis=0).astype(MXU_DTYPE)


def att_fwd(proj, bias, g, d):
    S = proj.shape[0]
    nq = ATT_BLOCKS[d]
    ch = BAND * d * nq

    def body(q_ref, kc_ref, kp_ref, vc_ref, vp_ref, b_ref, o_ref, l_ref):
        n = pl.program_id(1)
        head0 = _head_lanes()
        first = jnp.logical_and(lax.broadcasted_iota(jnp.int32, (BAND, 2 * BAND), 1) < BAND, n == 0)
        for r in range(d):
            for i in range(nq):
                rows = _att_rows(r, i, d)
                q = q_ref[rows, :]
                k = _att_keys(kc_ref, kp_ref, r, i, d)
                v = _att_keys(vc_ref, vp_ref, r, i, d)
                o_h, l_h = [], []
                for hh in range(2):
                    qm = jnp.where(head0 if hh == 0 else jnp.logical_not(head0), q, 0.0)
                    s = _dot(qm, k, "nt") * (ATT_HEAD_DIM ** -0.5) + b_ref[hh]
                    if i == 0:
                        s = jnp.where(first, NEG_INF, s)
                    m = jnp.max(s, axis=-1, keepdims=True)
                    p = jnp.exp(s - m)
                    l = jnp.sum(p, axis=-1, keepdims=True)
                    o_h.append(_dot(p / l, v))
                    l_h.append(jnp.broadcast_to(m + jnp.log(l), (BAND, 128)))
                o_ref[rows, :] = jnp.where(head0, o_h[0], o_h[1])
                l_ref[rows, :] = jnp.where(head0, l_h[0], l_h[1])

    out = pl.BlockSpec((ch, 128), lambda hp, n: (n, hp))
    return pl.pallas_call(
        body, name=f"att_fwd_d{d}", grid=(4, S // ch),
        in_specs=_att_specs(g, d, nq) + [pl.BlockSpec((2, BAND, 2 * BAND), lambda hp, n: (hp, 0, 0))],
        out_specs=[out, out],
        out_shape=[jax.ShapeDtypeStruct((S, W_BRANCH), F32), jax.ShapeDtypeStruct((S, W_BRANCH), F32)],
        compiler_params=_cp(2),
    )(proj, proj, proj, proj, proj, bias)


def att_bwd(proj, bias, lse, wts, dout, cbar, g, d):
    S = proj.shape[0]
    nq = ATT_BLOCKS[d]
    ch, pb = BAND * d * nq, BAND * d
    nb = S // ch
    scale = ATT_HEAD_DIM ** -0.5

    def body(q_ref, kc_ref, kp_ref, vc_ref, vp_ref, b_ref, l_ref, w_ref, do_ref, cb_ref,
             dq_ref, dk_ref, dv_ref, ek_ref, ev_ref, db_ref):
        n = pl.program_id(1)
        head0 = _head_lanes()
        first = jnp.logical_and(lax.broadcasted_iota(jnp.int32, (BAND, 2 * BAND), 1) < BAND, n == 0)

        @pl.when(n == 0)
        def _():
            db_ref[...] = jnp.zeros(db_ref.shape, F32)

        for r in range(d):
            own_k = own_v = None
            for i in range(nq):
                rows = _att_rows(r, i, d)
                q = q_ref[rows, :]
                k = _att_keys(kc_ref, kp_ref, r, i, d)
                v = _att_keys(vc_ref, vp_ref, r, i, d)
                w = w_ref[rows, :]
                do = w * do_ref[rows, :]
                delta = w * cb_ref[rows, :]
                lse_b = l_ref[rows, :]
                dq_h, dk_h, dv_h = [], [], []
                for hh in range(2):
                    p, ds = _att_pair(q, k, v, b_ref[hh], lse_b, do, delta, hh, head0, mask=first if i == 0 else None)
                    db_ref[hh] += ds
                    ds = ds * scale
                    dq_h.append(_dot(ds, k))
                    dk_h.append(_dot(ds, q, "tn"))
                    dv_h.append(_dot(p, do, "tn"))
                dq_ref[rows, :] = jnp.where(head0, dq_h[0], dq_h[1])
                head0_keys = jnp.concatenate([head0, head0], axis=0)
                dk2 = jnp.where(head0_keys, dk_h[0], dk_h[1])
                dv2 = jnp.where(head0_keys, dv_h[0], dv_h[1])
                if i == 0:
                    ek_ref[_att_rows(r, 0, d), :] = dk2[:BAND]
                    ev_ref[_att_rows(r, 0, d), :] = dv2[:BAND]
                else:
                    dk_ref[_att_rows(r, i - 1, d), :] = own_k + dk2[:BAND]
                    dv_ref[_att_rows(r, i - 1, d), :] = own_v + dv2[:BAND]
                own_k, own_v = dk2[BAND:], dv2[BAND:]
            dk_ref[_att_rows(r, nq - 1, d), :] = own_k
            dv_ref[_att_rows(r, nq - 1, d), :] = own_v

    cur = pl.BlockSpec((ch, 128), lambda hp, n: (n, hp))
    edge = pl.BlockSpec((pb, 128), lambda hp, n: (n, hp))
    bias_spec = pl.BlockSpec((2, BAND, 2 * BAND), lambda hp, n: (hp, 0, 0))
    big = jax.ShapeDtypeStruct((S, W_BRANCH), F32)
    small = jax.ShapeDtypeStruct((nb * pb, W_BRANCH), F32)
    dq, dk, dv, ek, ev, db = pl.pallas_call(
        body, name=f"att_bwd_d{d}", grid=(4, nb),
        in_specs=_att_specs(g, d, nq) + [bias_spec, cur, cur, cur, cur],
        out_specs=[cur, cur, cur, edge, edge, bias_spec],
        out_shape=[big, big, big, small, small, jax.ShapeDtypeStruct((ATT_HEADS, BAND, 2 * BAND), F32)],
        compiler_params=_cp(2),
    )(proj, proj, proj, proj, proj, bias, lse, wts, dout, cbar)

    def with_edges(main, edges):
        if nb == 1:
            return main
        main = main.reshape(nb, ch, W_BRANCH)
        add = jnp.pad(edges.reshape(nb, pb, W_BRANCH)[1:], ((0, 1), (ch - pb, 0), (0, 0)))
        return (main + add).reshape(S, W_BRANCH)

    return dq, with_edges(dk, ek), with_edges(dv, ev), db


def att_combine(os_, lses):
    S = os_[0].shape[0]

    def body(o0, o1, o2, l0, l1, l2, out_ref, w0, w1, w2):
        ls = [l0[...], l1[...], l2[...]]
        m = jnp.maximum(jnp.maximum(ls[0], ls[1]), ls[2])
        es = [jnp.exp(l - m) for l in ls]
        den = es[0] + es[1] + es[2]
        ws = [e / den for e in es]
        out_ref[...] = (ws[0] * o0[...] + ws[1] * o1[...] + ws[2] * o2[...]).astype(out_ref.dtype)
        for w_ref, w in zip((w0, w1, w2), ws):
            w_ref[...] = w

    blk = pl.BlockSpec((ROW_TILE, W_BRANCH), lambda i: (i, 0))
    f = jax.ShapeDtypeStruct((S, W_BRANCH), F32)
    return pl.pallas_call(
        body, name="att_combine", grid=(S // ROW_TILE,), in_specs=[blk] * 6, out_specs=[blk] * 4,
        out_shape=[jax.ShapeDtypeStruct((S, W_BRANCH), BF16), f, f, f], compiler_params=_cp(1),
    )(*os_, *lses)


def _split3(x):
    x1 = x.astype(BF16)
    r1 = x - x1.astype(F32)
    x2 = r1.astype(BF16)
    x3 = (r1 - x2.astype(F32)).astype(BF16)
    return x1, x2, x3


def att_combine_bwd(dout, os_, wts, headsum):
    S = dout.shape[0]

    def body(do_ref, o0, o1, o2, w0, w1, w2, e_ref, cb_ref):
        out = w0[...] * o0[...] + w1[...] * o1[...] + w2[...] * o2[...]
        e = e_ref[...]
        acc = jnp.zeros((ROW_TILE, W_BRANCH), F32)
        for term in _split3(do_ref[...] * out):
            acc = acc + jnp.dot(term, e, preferred_element_type=F32)
        cb_ref[...] = acc

    blk = pl.BlockSpec((ROW_TILE, W_BRANCH), lambda i: (i, 0))
    return pl.pallas_call(
        body, name="att_combine_bwd", grid=(S // ROW_TILE,),
        in_specs=[blk] * 7 + [pl.BlockSpec((W_BRANCH, W_BRANCH), lambda i: (0, 0))], out_specs=blk,
        out_shape=jax.ShapeDtypeStruct((S, W_BRANCH), F32), compiler_params=_cp(1),
    )(dout, *os_, *wts, headsum)


def _cmul(ar, ai, br, bi):
    return ar * br - ai * bi, ar * bi + ai * br


SCAN_ROWS = 8
SCAN_GROUPS = SSM_T // SCAN_ROWS


def _log_scan(xr, xi, mr, mi, rows, n, steps, reverse):
    total = xr.shape[0]
    for k in range(steps):
        dd = 1 << k
        keep = rows < n - dd if reverse else rows >= dd
        shift = total - dd if reverse else dd
        ar, ai = _cmul(mr, mi, jnp.where(keep, pltpu.roll(xr, shift, 0), 0.0), jnp.where(keep, pltpu.roll(xi, shift, 0), 0.0))
        xr, xi = xr + ar, xi + ai
        mr, mi = _cmul(mr, mi, mr, mi)
    return xr, xi, mr, mi


def _scan_scratch(n_results):
    return ([pltpu.VMEM((W_BRANCH // 128, SSM_T, 128), F32)] * 2 + [pltpu.VMEM((SCAN_GROUPS, W_BRANCH), F32)] * 2
            + [pltpu.VMEM((SSM_T, W_BRANCH), F32)] * n_results)


def _block_scan(xr, xi, mr, mi, reverse, yr_ref, yi_ref, er_ref, ei_ref, hr_ref, hi_ref):
    cols = xr.shape[1]
    rows = lax.broadcasted_iota(jnp.int32, (SSM_T, cols), 0)
    yr, yi, m8r, m8i = _log_scan(xr, xi, mr, mi, rows & (SCAN_ROWS - 1), SCAN_ROWS, 3, reverse)
    lane_blocks = range(cols // 128)
    for c in lane_blocks:
        yr_ref[c] = yr[:, c * 128:(c + 1) * 128]
        yi_ref[c] = yi[:, c * 128:(c + 1) * 128]
    wide = lambda ref, rows_: jnp.concatenate([ref[c, rows_, :] for c in lane_blocks], axis=1)
    end = pl.ds(0 if reverse else SCAN_ROWS - 1, SCAN_GROUPS, stride=SCAN_ROWS)
    groups = lax.broadcasted_iota(jnp.int32, (SCAN_GROUPS, cols), 0)
    er, ei, _, _ = _log_scan(wide(yr_ref, end), wide(yi_ref, end), m8r, m8i, groups, SCAN_GROUPS,
                             int(math.log2(SCAN_GROUPS)), reverse)
    er_ref[...] = er
    ei_ref[...] = ei
    j = lax.broadcasted_iota(jnp.int32, (SCAN_ROWS, cols), 0)
    dist = SCAN_ROWS - j if reverse else j + 1
    tr, ti = jnp.ones((SCAN_ROWS, cols), F32), jnp.zeros((SCAN_ROWS, cols), F32)
    br, bi = mr, mi
    for bit in range(4):
        nr, ni = _cmul(tr, ti, br, bi)
        take = ((dist >> bit) & 1) == 1
        tr, ti = jnp.where(take, nr, tr), jnp.where(take, ni, ti)
        br, bi = _cmul(br, bi, br, bi)
    for g in range(SCAN_GROUPS):
        before = g + 1 if reverse else g - 1
        rows_g = slice(g * SCAN_ROWS, (g + 1) * SCAN_ROWS)
        if 0 <= before < SCAN_GROUPS:
            ar, ai = _cmul(tr, ti, er_ref[before:before + 1, :], ei_ref[before:before + 1, :])
            hr_ref[rows_g, :] = wide(yr_ref, rows_g) + ar
            hi_ref[rows_g, :] = wide(yi_ref, rows_g) + ai
        else:
            hr_ref[rows_g, :] = wide(yr_ref, rows_g)
            hi_ref[rows_g, :] = wide(yi_ref, rows_g)
    last = 0 if reverse else SCAN_GROUPS - 1
    return er_ref[last:last + 1, :], ei_ref[last:last + 1, :]


def s5_fwd(proj, b_re, b_im, a_re, a_im, c_re, c_im, d_skip):
    S = proj.shape[0]
    nt = S // SSM_T

    def body(u_ref, bre_ref, bim_ref, ar_ref, ai_ref, cre_ref, cim_ref, dsk_ref, hr_ref, hi_ref, y_ref, cr_ref, ci_ref,
             yr_ref, yi_ref, er_ref, ei_ref):
        t = pl.program_id(1)

        @pl.when(t == 0)
        def _():
            cr_ref[...] = jnp.zeros(cr_ref.shape, F32)
            ci_ref[...] = jnp.zeros(ci_ref.shape, F32)

        u = u_ref[...]
        ar, ai = ar_ref[...], ai_ref[...]
        rows = lax.broadcasted_iota(jnp.int32, (SSM_T, W_BRANCH), 0)
        inr, ini = _cmul(ar, ai, cr_ref[0:1, :], ci_ref[0:1, :])
        xr = _dot(u, bre_ref[...]) + jnp.where(rows == 0, inr, 0.0)
        xi = _dot(u, bim_ref[...]) + jnp.where(rows == 0, ini, 0.0)
        endr, endi = _block_scan(xr, xi, ar, ai, False, yr_ref, yi_ref, er_ref, ei_ref, hr_ref, hi_ref)
        cr_ref[...] = jnp.broadcast_to(endr, cr_ref.shape)
        ci_ref[...] = jnp.broadcast_to(endi, ci_ref.shape)
        xr, xi = hr_ref[...], hi_ref[...]
        y_ref[...] = _dot(xr, cre_ref[...]) - _dot(xi, cim_ref[...]) + u * dsk_ref[...]

    u_spec = pl.BlockSpec((SSM_T, 128), lambda j, t: (t, OFF_SSM // 128 + j))
    b_spec = pl.BlockSpec((None, 128, W_BRANCH), lambda j, t: (j, 0, 0))
    a_spec = pl.BlockSpec((1, W_BRANCH), lambda j, t: (0, j))
    c_spec = pl.BlockSpec((None, W_BRANCH, 128), lambda j, t: (j, 0, 0))
    h_spec = pl.BlockSpec((SSM_T, W_BRANCH), lambda j, t: (t, j))
    return pl.pallas_call(
        body, name="s5_fwd", grid=(4, nt),
        in_specs=[u_spec, b_spec, b_spec, a_spec, a_spec, c_spec, c_spec, pl.BlockSpec((1, 128), lambda j, t: (0, j))],
        out_specs=[h_spec, h_spec, pl.BlockSpec((SSM_T, 128), lambda j, t: (t, j))],
        out_shape=[jax.ShapeDtypeStruct((S, SSM_COLS), F32), jax.ShapeDtypeStruct((S, SSM_COLS), F32),
                   jax.ShapeDtypeStruct((S, W_BRANCH), F32)],
        scratch_shapes=[pltpu.VMEM((8, W_BRANCH), F32)] * 2 + _scan_scratch(0),
        compiler_params=_cp(2),
    )(proj, b_re, b_im, a_re, a_im, c_re, c_im, d_skip)


def s5_bwd(proj, hr, hi, dy, b_re, b_im, a_re, a_im, c_re, c_im, d_skip):
    S = proj.shape[0]
    nt = S // SSM_T

    def body(u_ref, hr_ref, hi_ref, hpr_ref, hpi_ref, dy_ref, bre_ref, bim_ref, ar_ref, ai_ref, cre_ref, cim_ref,
             dsk_ref, du_ref, dbre_ref, dbim_ref, dar_ref, dai_ref, dcre_ref, dcim_ref, ddsk_ref, gr_ref, gi_ref,
             yr_ref, yi_ref, er_ref, ei_ref, sr_ref, si_ref):
        step = pl.program_id(1)
        t = nt - 1 - step

        @pl.when(step == 0)
        def _():
            gr_ref[...] = jnp.zeros(gr_ref.shape, F32)
            gi_ref[...] = jnp.zeros(gi_ref.shape, F32)
            for ref in (dbre_ref, dbim_ref, dar_ref, dai_ref, dcre_ref, dcim_ref, ddsk_ref):
                ref[...] = jnp.zeros(ref.shape, F32)

        u = u_ref[...]
        dy = dy_ref[...]
        ar, ai = ar_ref[...], ai_ref[...]
        rows = lax.broadcasted_iota(jnp.int32, (SSM_T, W_BRANCH), 0)
        inr, ini = _cmul(ar, -ai, gr_ref[0:1, :], gi_ref[0:1, :])
        xr = _dot(dy, cre_ref[...], "nt") + jnp.where(rows == SSM_T - 1, inr, 0.0)
        xi = -_dot(dy, cim_ref[...], "nt") + jnp.where(rows == SSM_T - 1, ini, 0.0)
        endr, endi = _block_scan(xr, xi, ar, -ai, True, yr_ref, yi_ref, er_ref, ei_ref, sr_ref, si_ref)
        gr_ref[...] = jnp.broadcast_to(endr, gr_ref.shape)
        gi_ref[...] = jnp.broadcast_to(endi, gi_ref.shape)
        xr, xi = sr_ref[...], si_ref[...]
        hr_blk, hi_blk = hr_ref[...], hi_ref[...]
        keep = (t > 0).astype(F32)
        hpr = jnp.where(rows >= 1, pltpu.roll(hr_blk, 1, 0), hpr_ref[7:8, :] * keep)
        hpi = jnp.where(rows >= 1, pltpu.roll(hi_blk, 1, 0), hpi_ref[7:8, :] * keep)
        dar_ref[...] += jnp.sum(hpr * xr + hpi * xi, axis=0, keepdims=True)
        dai_ref[...] += jnp.sum(hpr * xi - hpi * xr, axis=0, keepdims=True)
        dcre_ref[...] += _dot(hr_blk, dy, "tn")
        dcim_ref[...] -= _dot(hi_blk, dy, "tn")
        du = dy * dsk_ref[...] + _dot(xr, bre_ref[...], "nt") + _dot(xi, bim_ref[...], "nt")
        du_ref[...] = du.astype(du_ref.dtype)
        dbre_ref[...] += _dot(u, xr, "tn")
        dbim_ref[...] += _dot(u, xi, "tn")
        ddsk_ref[...] += jnp.sum(dy * u, axis=0, keepdims=True)

    def rev(t):
        return nt - 1 - t

    u_spec = pl.BlockSpec((SSM_T, 128), lambda j, t: (rev(t), OFF_SSM // 128 + j))
    h_spec = pl.BlockSpec((SSM_T, W_BRANCH), lambda j, t: (rev(t), j))
    hprev_spec = pl.BlockSpec((8, W_BRANCH), lambda j, t: (jnp.maximum(rev(t) * (SSM_T // 8) - 1, 0), j))
    ch_spec = pl.BlockSpec((SSM_T, 128), lambda j, t: (rev(t), j))
    b_spec = pl.BlockSpec((None, 128, W_BRANCH), lambda j, t: (j, 0, 0))
    a_spec = pl.BlockSpec((1, W_BRANCH), lambda j, t: (0, j))
    c_spec = pl.BlockSpec((None, W_BRANCH, 128), lambda j, t: (j, 0, 0))
    d_spec = pl.BlockSpec((1, 128), lambda j, t: (0, j))
    return pl.pallas_call(
        body, name="s5_bwd", grid=(4, nt),
        in_specs=[u_spec, h_spec, h_spec, hprev_spec, hprev_spec, ch_spec, b_spec, b_spec, a_spec, a_spec,
                  c_spec, c_spec, d_spec],
        out_specs=[ch_spec, b_spec, b_spec, a_spec, a_spec, c_spec, c_spec, d_spec],
        out_shape=[jax.ShapeDtypeStruct((S, W_BRANCH), BF16),
                   jax.ShapeDtypeStruct((4, 128, W_BRANCH), F32), jax.ShapeDtypeStruct((4, 128, W_BRANCH), F32),
                   jax.ShapeDtypeStruct((1, SSM_COLS), F32), jax.ShapeDtypeStruct((1, SSM_COLS), F32),
                   jax.ShapeDtypeStruct((4, W_BRANCH, 128), F32), jax.ShapeDtypeStruct((4, W_BRANCH, 128), F32),
                   jax.ShapeDtypeStruct((1, W_BRANCH), F32)],
        scratch_shapes=[pltpu.VMEM((8, W_BRANCH), F32)] * 2 + _scan_scratch(2),
        compiler_params=_cp(2),
    )(proj, hr, hi, hr, hi, dy, b_re, b_im, a_re, a_im, c_re, c_im, d_skip)


def glu_fwd(y, w_glu, b_glu, after=()):
    S = y.shape[0]

    def body(y_ref, w_ref, b_ref, *rest):
        o_ref = rest[-1]
        g = _gelu(y_ref[...])
        o_ref[...] = (g * _sigmoid(_dot(g, w_ref[...]) + b_ref[...])).astype(o_ref.dtype)

    blk = pl.BlockSpec((ROW_TILE, W_BRANCH), lambda i: (i, 0))
    return pl.pallas_call(
        body, name="glu_fwd", grid=(S // ROW_TILE,),
        in_specs=[blk, pl.BlockSpec((W_BRANCH, W_BRANCH), lambda i: (0, 0)), pl.BlockSpec((1, W_BRANCH), lambda i: (0, 0))]
        + [ANY] * len(after),
        out_specs=blk, out_shape=jax.ShapeDtypeStruct((S, W_BRANCH), BF16), compiler_params=_cp(1),
    )(y, w_glu, b_glu, *after)


def glu_bwd(y, w_glu, b_glu, dout):
    S = y.shape[0]

    def body(y_ref, w_ref, b_ref, do_ref, dy_ref, dw_ref, db_ref):
        yv = y_ref[...]
        do = do_ref[...]
        g = _gelu(yv)
        s = _sigmoid(_dot(g, w_ref[...]) + b_ref[...])
        dz = do * g * s * (1.0 - s)
        dg = do * s + _dot(dz, w_ref[...], "nt")
        dy_ref[...] = dg * _gelu_grad(yv)
        dw = _dot(g, dz, "tn")
        db = jnp.sum(dz, axis=0, keepdims=True)

        @pl.when(pl.program_id(0) == 0)
        def _():
            dw_ref[...] = dw
            db_ref[...] = db

        @pl.when(pl.program_id(0) > 0)
        def _():
            dw_ref[...] += dw
            db_ref[...] += db

    blk = pl.BlockSpec((ROW_TILE, W_BRANCH), lambda i: (i, 0))
    mat = pl.BlockSpec((W_BRANCH, W_BRANCH), lambda i: (0, 0))
    vec = pl.BlockSpec((1, W_BRANCH), lambda i: (0, 0))
    return pl.pallas_call(
        body, name="glu_bwd", grid=(S // ROW_TILE,), in_specs=[blk, mat, vec, blk], out_specs=[blk, mat, vec],
        out_shape=[jax.ShapeDtypeStruct((S, W_BRANCH), F32), jax.ShapeDtypeStruct((W_BRANCH, W_BRANCH), F32),
                   jax.ShapeDtypeStruct((1, W_BRANCH), F32)],
        compiler_params=_cp(1),
    )(y, w_glu, b_glu, dout)


SGU_TILE = 512
SGU_U_BLOCK = OFF_SGU // W_BRANCH
SGU_V_BLOCK = SGU_U_BLOCK + 1


def _sgu_norm(zv):
    v = _gelu(zv)
    mu = jnp.mean(v, axis=-1, keepdims=True)
    vc = v - mu
    rstd = lax.rsqrt(jnp.mean(vc * vc, axis=-1, keepdims=True) + EPS)
    return vc * rstd, rstd


def _tril():
    return lax.broadcasted_iota(jnp.int32, (SGU_CHUNK, SGU_CHUNK), 0) >= lax.broadcasted_iota(jnp.int32, (SGU_CHUNK, SGU_CHUNK), 1)


def sgu_fwd(proj, ln_g, ln_b, w_s, b_s_t):
    S = proj.shape[0]

    def body(zu_ref, zv_ref, g_ref, b_ref, ws_ref, bs_ref, o_ref, vf_ref):
        vn, _ = _sgu_norm(zv_ref[...])
        vf_ref[...] = vn * g_ref[...] + b_ref[...]
        tri = _tril()
        for gi in range(4):
            ws = jnp.where(tri, ws_ref[gi], 0.0)
            cols = slice(gi * 128, (gi + 1) * 128)
            for c in range(SGU_TILE // SGU_CHUNK):
                rows = slice(c * SGU_CHUNK, (c + 1) * SGU_CHUNK)
                sv = _dot(ws, vf_ref[rows, cols]) + bs_ref[:, gi:gi + 1]
                o_ref[rows, cols] = (_gelu(zu_ref[rows, cols]) * sv).astype(o_ref.dtype)

    blk = lambda cb: pl.BlockSpec((SGU_TILE, W_BRANCH), lambda i: (i, cb))
    vec = pl.BlockSpec((1, W_BRANCH), lambda i: (0, 0))
    return pl.pallas_call(
        body, name="sgu_fwd", grid=(S // SGU_TILE,),
        in_specs=[blk(SGU_U_BLOCK), blk(SGU_V_BLOCK), vec, vec, pl.BlockSpec((4, SGU_CHUNK, SGU_CHUNK), lambda i: (0, 0, 0)),
                  pl.BlockSpec((SGU_CHUNK, 4), lambda i: (0, 0))],
        out_specs=blk(0), out_shape=jax.ShapeDtypeStruct((S, W_BRANCH), BF16),
        scratch_shapes=[pltpu.VMEM((SGU_TILE, W_BRANCH), F32)], compiler_params=_cp(1),
    )(proj, proj, ln_g, ln_b, w_s, b_s_t)


def sgu_bwd(proj, ln_g, ln_b, w_s, b_s_t, dout):
    S = proj.shape[0]

    def body(zu_ref, zv_ref, g_ref, b_ref, ws_ref, bs_ref, do_ref, dzu_ref, dzv_ref, dg_ref, db_ref, dws_ref, dbs_ref,
             vf_ref, dvf_ref):
        @pl.when(pl.program_id(0) == 0)
        def _():
            for ref in (dg_ref, db_ref, dws_ref, dbs_ref):
                ref[...] = jnp.zeros(ref.shape, F32)

        vn, rstd = _sgu_norm(zv_ref[...])
        vf_ref[...] = vn * g_ref[...] + b_ref[...]
        tri = _tril()
        lane = lax.broadcasted_iota(jnp.int32, (SGU_CHUNK, 128), 1)
        dbs = jnp.zeros((SGU_CHUNK, 128), F32)
        for gi in range(4):
            ws = jnp.where(tri, ws_ref[gi], 0.0)
            cols = slice(gi * 128, (gi + 1) * 128)
            dws = jnp.zeros((SGU_CHUNK, SGU_CHUNK), F32)
            for c in range(SGU_TILE // SGU_CHUNK):
                rows = slice(c * SGU_CHUNK, (c + 1) * SGU_CHUNK)
                vf = vf_ref[rows, cols]
                zu = zu_ref[rows, cols]
                do = do_ref[rows, cols]
                sv = _dot(ws, vf) + bs_ref[:, gi:gi + 1]
                dzu_ref[rows, cols] = (do * sv * _gelu_grad(zu)).astype(dzu_ref.dtype)
                dsv = do * _gelu(zu)
                dvf_ref[rows, cols] = _dot(ws, dsv, "tn")
                dws = dws + _dot(dsv, vf, "nt")
                dbs = dbs + jnp.where(lane == gi, jnp.sum(dsv, axis=-1, keepdims=True), 0.0)
            dws_ref[gi] += jnp.where(tri, dws, 0.0)
        dbs_ref[...] += dbs
        dvf = dvf_ref[...]
        dg_ref[...] += jnp.sum(dvf * vn, axis=0, keepdims=True)
        db_ref[...] += jnp.sum(dvf, axis=0, keepdims=True)
        dvn = dvf * g_ref[...]
        dv = rstd * (dvn - jnp.mean(dvn, axis=-1, keepdims=True) - vn * jnp.mean(dvn * vn, axis=-1, keepdims=True))
        dzv_ref[...] = (dv * _gelu_grad(zv_ref[...])).astype(dzv_ref.dtype)

    blk = lambda cb: pl.BlockSpec((SGU_TILE, W_BRANCH), lambda i: (i, cb))
    vec = pl.BlockSpec((1, W_BRANCH), lambda i: (0, 0))
    ws_spec = pl.BlockSpec((4, SGU_CHUNK, SGU_CHUNK), lambda i: (0, 0, 0))
    return pl.pallas_call(
        body, name="sgu_bwd", grid=(S // SGU_TILE,),
        in_specs=[blk(SGU_U_BLOCK), blk(SGU_V_BLOCK), vec, vec, ws_spec, pl.BlockSpec((SGU_CHUNK, 4), lambda i: (0, 0)),
                  blk(0)],
        out_specs=[blk(0), blk(0), vec, vec, ws_spec, pl.BlockSpec((SGU_CHUNK, 128), lambda i: (0, 0))],
        out_shape=[jax.ShapeDtypeStruct((S, W_BRANCH), BF16), jax.ShapeDtypeStruct((S, W_BRANCH), BF16),
                   jax.ShapeDtypeStruct((1, W_BRANCH), F32), jax.ShapeDtypeStruct((1, W_BRANCH), F32),
                   jax.ShapeDtypeStruct((4, SGU_CHUNK, SGU_CHUNK), F32), jax.ShapeDtypeStruct((SGU_CHUNK, 128), F32)],
        scratch_shapes=[pltpu.VMEM((SGU_TILE, W_BRANCH), F32), pltpu.VMEM((SGU_TILE, W_BRANCH), F32)],
        compiler_params=_cp(1),
    )(proj, proj, ln_g, ln_b, w_s, b_s_t, dout)


GM_TILE = 512


def _gate_specs(order):
    def spec(i):
        def index(*ids):
            m, n = order(*ids)
            return (m, (OFF_GATE + i * D_MODEL) // GM_TILE + n)
        return pl.BlockSpec((GM_TILE, GM_TILE), index)
    return [spec(i) for i in range(4)]


def merge_fwd(proj, gate_b, branches, w_up):
    S = proj.shape[0]
    order = lambda n, m: (m, n)

    def body(p0, p1, p2, p3, gb_ref, b0, b1, b2, b3, w_ref, o_ref):
        acc = jnp.zeros((GM_TILE, GM_TILE), F32)
        for i, (p_ref, br_ref) in enumerate(zip((p0, p1, p2, p3), (b0, b1, b2, b3))):
            acc = acc + _sigmoid(p_ref[...] + gb_ref[i:i + 1, :]) * _dot(br_ref[...], w_ref[i])
        o_ref[...] = acc.astype(o_ref.dtype)

    br_spec = pl.BlockSpec((GM_TILE, W_BRANCH), lambda n, m: (m, 0))
    return pl.pallas_call(
        body, name="merge_fwd", grid=(D_MODEL // GM_TILE, S // GM_TILE),
        in_specs=_gate_specs(order) + [pl.BlockSpec((4, GM_TILE), lambda n, m: (0, n))] + [br_spec] * 4
        + [pl.BlockSpec((4, W_BRANCH, GM_TILE), lambda n, m: (0, 0, n))],
        out_specs=pl.BlockSpec((GM_TILE, GM_TILE), lambda n, m: (m, n)),
        out_shape=jax.ShapeDtypeStruct((S, D_MODEL), BF16), compiler_params=_cp(2),
    )(proj, proj, proj, proj, gate_b, *branches, w_up)


def merge_bwd(proj, gate_b, branches, w_up, dmerged):
    S = proj.shape[0]
    order = lambda n, m: (m, n)

    def body(p0, p1, p2, p3, gb_ref, b0, b1, b2, b3, w_ref, dm_ref, dp0, dp1, dp2, dp3, du0, du1, du2, du3, dgb_ref):
        dm = dm_ref[...]
        dgb = []
        for i, (p_ref, br_ref, dp_ref, du_ref) in enumerate(
                zip((p0, p1, p2, p3), (b0, b1, b2, b3), (dp0, dp1, dp2, dp3), (du0, du1, du2, du3))):
            gate = _sigmoid(p_ref[...] + gb_ref[i:i + 1, :])
            dpre = dm * _dot(br_ref[...], w_ref[i]) * gate * (1.0 - gate)
            dp_ref[...] = dpre.astype(dp_ref.dtype)
            du_ref[...] = (dm * gate).astype(du_ref.dtype)
            dgb.append(jnp.sum(dpre, axis=0, keepdims=True))
        dgb = jnp.concatenate(dgb, axis=0)

        @pl.when(pl.program_id(1) == 0)
        def _():
            dgb_ref[...] = dgb

        @pl.when(pl.program_id(1) > 0)
        def _():
            dgb_ref[...] += dgb

    br_spec = pl.BlockSpec((GM_TILE, W_BRANCH), lambda n, m: (m, 0))
    mn = pl.BlockSpec((GM_TILE, GM_TILE), lambda n, m: (m, n))
    gb = pl.BlockSpec((4, GM_TILE), lambda n, m: (0, n))
    big = jax.ShapeDtypeStruct((S, D_MODEL), BF16)
    outs = pl.pallas_call(
        body, name="merge_bwd", grid=(D_MODEL // GM_TILE, S // GM_TILE),
        in_specs=_gate_specs(order) + [gb] + [br_spec] * 4
        + [pl.BlockSpec((4, W_BRANCH, GM_TILE), lambda n, m: (0, 0, n)), mn],
        out_specs=[mn] * 8 + [gb], out_shape=[big] * 8 + [jax.ShapeDtypeStruct((4, D_MODEL), F32)],
        compiler_params=_cp(2),
    )(proj, proj, proj, proj, gate_b, *branches, w_up, dmerged)
    return outs[0:4], outs[4:8], outs[8]


def _xatt_probs(q, k):
    s = _dot(q, k, "nt") * (X_HEAD_DIM ** -0.5)
    p = jnp.exp(s - jnp.max(s, axis=-1, keepdims=True))
    return p / jnp.sum(p, axis=-1, keepdims=True)


def xatt_fwd(q, kv):
    S = q.shape[0]

    def body(q_ref, kv_ref, o_ref):
        for h in range(X_HEADS):
            cols = slice(h * X_HEAD_DIM, (h + 1) * X_HEAD_DIM)
            p = _xatt_probs(q_ref[:, cols], kv_ref[:, cols])
            o_ref[:, cols] = _dot(p, kv_ref[:, W_BRANCH + h * X_HEAD_DIM:W_BRANCH + (h + 1) * X_HEAD_DIM]).astype(o_ref.dtype)

    blk = pl.BlockSpec((ROW_TILE, W_BRANCH), lambda i: (i, 0))
    return pl.pallas_call(
        body, name="xatt_fwd", grid=(S // ROW_TILE,),
        in_specs=[blk, pl.BlockSpec((N_MEM, 2 * W_BRANCH), lambda i: (0, 0))], out_specs=blk,
        out_shape=jax.ShapeDtypeStruct((S, W_BRANCH), BF16), compiler_params=_cp(1),
    )(q, kv)


def xatt_bwd(q, kv, do):
    S = q.shape[0]

    def body(q_ref, kv_ref, do_ref, dq_ref, dkv_ref):
        @pl.when(pl.program_id(0) == 0)
        def _():
            dkv_ref[...] = jnp.zeros(dkv_ref.shape, F32)

        for h in range(X_HEADS):
            cols = slice(h * X_HEAD_DIM, (h + 1) * X_HEAD_DIM)
            vcols = slice(W_BRANCH + h * X_HEAD_DIM, W_BRANCH + (h + 1) * X_HEAD_DIM)
            qh, kh, doh = q_ref[:, cols], kv_ref[:, cols], do_ref[:, cols]
            p = _xatt_probs(qh, kh)
            dp = _dot(doh, kv_ref[:, vcols], "nt")
            ds = p * (dp - jnp.sum(dp * p, axis=-1, keepdims=True)) * (X_HEAD_DIM ** -0.5)
            dq_ref[:, cols] = _dot(ds, kh).astype(dq_ref.dtype)
            dkv_ref[:, cols] += _dot(ds, qh, "tn")
            dkv_ref[:, vcols] += _dot(p, doh, "tn")

    blk = pl.BlockSpec((ROW_TILE, W_BRANCH), lambda i: (i, 0))
    kv_spec = pl.BlockSpec((N_MEM, 2 * W_BRANCH), lambda i: (0, 0))
    return pl.pallas_call(
        body, name="xatt_bwd", grid=(S // ROW_TILE,), in_specs=[blk, kv_spec, blk], out_specs=[blk, kv_spec],
        out_shape=[jax.ShapeDtypeStruct((S, W_BRANCH), BF16), jax.ShapeDtypeStruct((N_MEM, 2 * W_BRANCH), F32)],
        compiler_params=_cp(1),
    )(q, kv, do)


def s5_params(a_re, a_im, log_dt, b_re, b_im, c_re, c_im):
    lam_re = jnp.minimum(a_re, -1e-4)
    lam_im = a_im
    dt = jnp.exp(log_dt)[:, None]
    mag = jnp.exp(lam_re * dt)
    ab_re, ab_im = mag * jnp.cos(lam_im * dt), mag * jnp.sin(lam_im * dt)
    den = lam_re * lam_re + lam_im * lam_im
    f_re = ((ab_re - 1.0) * lam_re + ab_im * lam_im) / den
    f_im = (ab_im * lam_re - (ab_re - 1.0) * lam_im) / den
    bb_re = f_re[..., None] * b_re - f_im[..., None] * b_im
    bb_im = f_re[..., None] * b_im + f_im[..., None] * b_re
    eye = jnp.eye(8, dtype=F32)

    def b_blocks(bb):
        t = bb.reshape(4, 8, SSM_STATE, SSM_GROUP).transpose(0, 1, 3, 2)
        return (t[:, :, :, None, :] * eye[None, :, None, :, None]).reshape(4, 128, W_BRANCH)

    def c_blocks(cc):
        t = cc.reshape(4, 8, SSM_GROUP, SSM_STATE).transpose(0, 1, 3, 2)
        return (t[:, :, :, None, :] * eye[None, :, None, :, None]).reshape(4, W_BRANCH, 128)

    return (ab_re.reshape(1, SSM_COLS), ab_im.reshape(1, SSM_COLS), b_blocks(bb_re), b_blocks(bb_im),
            c_blocks(c_re), c_blocks(c_im))


ANY = pl.BlockSpec(memory_space=pl.ANY)


def _chip_index():
    return 2 * lax.axis_index("x") + lax.axis_index("y")


def _peer_chip(j):
    x, y, c = lax.axis_index("x"), lax.axis_index("y"), lax.axis_index("c")
    return ((1 - x) if j & 2 else x, (1 - y) if j & 1 else y, c)


def _piece(ref, axis, s, n):
    size = ref.shape[axis] // n
    idx = [slice(None)] * len(ref.shape)
    idx[axis] = pl.ds(s * size, size)
    return ref.at[tuple(idx)]


HBM_SPEC = pl.BlockSpec(memory_space=pltpu.HBM)
SEM_SPEC = pl.BlockSpec(memory_space=pltpu.SEMAPHORE)
SIDE_EFFECT = pltpu.SideEffectType.DATAFLOW_SIDE_EFFECTING


HALVING_MIN_ROWS = 32


def _rows_half(ref, c):
    rows = ref.shape[0] // 2
    return ref.at[pl.ds(c * rows, rows), :]


def _halved(ref):
    return ref.shape[0] >= HALVING_MIN_ROWS


def _chip_copies(ins, lands, send, recv, axes, mode, k, c, arriving):
    copies = []
    for t in range(len(ins)):
        for j in (1, 2, 3):
            sems = dict(send_sem=send.at[3 * t + j - 1], recv_sem=recv.at[3 * t + j - 1], device_id_type=MESH_ID)
            if mode == "scatter":
                src = ins[t] if axes[t] is None else _piece(ins[t], axes[t], k ^ j, 4)
                dst = lands[t].at[k ^ j if arriving else k]
                device = _peer_chip(j)
            elif mode == "gather":
                src, dst = ins[t], _piece(lands[t], axes[t], k ^ j if arriving else k, 4)
                if _halved(ins[t]):
                    src, dst = _rows_half(src, c), _rows_half(dst, c)
                device = _peer_chip(j)
            else:
                if not _halved(ins[t]):
                    continue
                theirs = _piece(lands[t], axes[t], k ^ j, 4)
                src, dst = _rows_half(theirs, c), _rows_half(theirs, 1 - c if arriving else c)
                device = (lax.axis_index("x"), lax.axis_index("y"), 1 - lax.axis_index("c"))
            copies.append(pltpu.make_async_remote_copy(src_ref=src, dst_ref=dst, device_id=device, **sems))
    return copies


def _own_copies(ins, lands, send, axes, mode, k):
    if mode != "gather":
        return []
    n = len(ins)
    return [pltpu.make_async_copy(ins[t], _piece(lands[t], axes[t], k, 4), send.at[3 * n + t]) for t in range(n)]


def chips_start(ins, lands, axes, mode, name, after=()):
    n, na = len(ins), len(after)

    def body(*refs):
        in_refs, land_refs = refs[:n], refs[n:2 * n]
        send, recv, token = refs[2 * n + na], refs[2 * n + na + 1], refs[-1]
        q, core = _chip_index(), lax.axis_index("c")
        for k in range(4):
            for c in range(2):
                @pl.when(jnp.logical_and(q == k, core == c))
                def _():
                    for copy in _chip_copies(in_refs, land_refs, send, recv, axes, mode, k, c, arriving=False):
                        copy.start()
                    for copy in _own_copies(in_refs, land_refs, send, axes, mode, k):
                        copy.start()
        token[...] = jnp.zeros(token.shape, token.dtype)

    hbm = lambda a: pltpu.HBM(a.shape, a.dtype)
    outs = pl.pallas_call(
        body, name=name, in_specs=[HBM_SPEC] * (2 * n) + [ANY] * na,
        out_specs=[SEM_SPEC, SEM_SPEC] + [HBM_SPEC] * (2 * n) + [pl.BlockSpec(memory_space=pltpu.VMEM)],
        out_shape=[pltpu.SemaphoreType.DMA((4 * n,)), pltpu.SemaphoreType.DMA((3 * n,))]
        + [hbm(a) for a in ins] + [hbm(a) for a in lands] + [jax.ShapeDtypeStruct((8, 128), F32)],
        input_output_aliases={i: 2 + i for i in range(2 * n)},
        compiler_params=pltpu.CompilerParams(has_side_effects=SIDE_EFFECT),
    )(*[pltpu.with_memory_space_constraint(a, pltpu.HBM) for a in list(ins) + list(lands)], *after)
    return outs[0], outs[1], outs[2:2 + n], outs[2 + n:2 + 2 * n], outs[-1]


def chips_wait(send, recv, ins, lands, axes, mode, name, after=()):
    n = len(ins)

    def body(*refs):
        in_refs, land_refs = refs[:n], refs[n:2 * n]
        send_ref, recv_ref = refs[2 * n], refs[2 * n + 1]
        q, core = _chip_index(), lax.axis_index("c")
        for k in range(4):
            for c in range(2):
                @pl.when(jnp.logical_and(q == k, core == c))
                def _():
                    for copy in _chip_copies(in_refs, land_refs, send_ref, recv_ref, axes, mode, k, c, arriving=True):
                        copy.wait_send()
                        copy.wait_recv()
                    for copy in _own_copies(in_refs, land_refs, send_ref, axes, mode, k):
                        copy.wait()

    hbm = lambda a: pltpu.HBM(a.shape, a.dtype)
    outs = pl.pallas_call(
        body, name=name, in_specs=[HBM_SPEC] * (2 * n) + [SEM_SPEC, SEM_SPEC] + [ANY] * len(after),
        out_specs=[HBM_SPEC] * (2 * n), out_shape=[hbm(a) for a in ins] + [hbm(a) for a in lands],
        input_output_aliases={i: i for i in range(2 * n)},
        compiler_params=pltpu.CompilerParams(has_side_effects=SIDE_EFFECT),
    )(*ins, *lands, send, recv, *after)
    return outs[:n], outs[n:]


def swap_cores(arrs, name):
    n = len(arrs)

    def body(*refs):
        ins, outs = refs[:n], refs[n:2 * n]
        send, recv = refs[2 * n:]
        sibling = (lax.axis_index("x"), lax.axis_index("y"), 1 - lax.axis_index("c"))
        copies = [pltpu.make_async_remote_copy(src_ref=ins[t], dst_ref=outs[t], send_sem=send.at[t], recv_sem=recv.at[t],
                                               device_id=sibling, device_id_type=MESH_ID) for t in range(n)]
        for cp in copies:
            cp.start()
        for cp in copies:
            cp.wait()

    return pl.pallas_call(
        body, name=name, in_specs=[ANY] * n, out_specs=[ANY] * n,
        out_shape=[jax.ShapeDtypeStruct(a.shape, a.dtype) for a in arrs],
        scratch_shapes=[pltpu.SemaphoreType.DMA((n,)), pltpu.SemaphoreType.DMA((n,))],
    )(*arrs)


ELEMENTWISE_BLOCK_BYTES = 1 << 20


def _row_tile(rows, cols):
    want = max(8, ELEMENTWISE_BLOCK_BYTES // (4 * 128 * -(-cols // 128)))
    fits = [t for t in range(8, min(rows, want) + 1, 8) if rows % t == 0]
    return fits[-1] if fits else rows


def sum_chips(recv, own, axis, chip, stacked, l, name):
    _, r, c = recv.shape
    tr = _row_tile(r, c)
    nrt = r // tr

    def body(chip_ref, r_ref, own_ref, stacked_ref, o_ref):
        for k in range(4):
            @pl.when(chip_ref[0] == k)
            def _():
                terms = [own_ref[...] if s == k else r_ref[s] for s in range(4)]
                o_ref[...] = ((terms[0] + terms[1]) + terms[2]) + terms[3]

    own_index = {0: lambda i, q: (q[0] * nrt + i, 0), 1: lambda i, q: (i, q[0]), None: lambda i, q: (i, 0)}[axis]
    return pl.pallas_call(
        body, name=name,
        grid_spec=pltpu.PrefetchScalarGridSpec(
            num_scalar_prefetch=1, grid=(nrt,),
            in_specs=[pl.BlockSpec((4, tr, c), lambda i, q: (0, i, 0)), pl.BlockSpec((tr, c), own_index), ANY],
            out_specs=pl.BlockSpec((None, tr, c), lambda i, q: (l, i, 0))),
        out_shape=jax.ShapeDtypeStruct(stacked.shape, F32), input_output_aliases={3: 0}, compiler_params=_cp(1),
    )(chip, recv, own, stacked)


def adamw(w, ga, gb, m, v, name):
    rows, cols = w.shape
    tr = _row_tile(rows, cols)

    def body(w_ref, ga_ref, gb_ref, m_ref, v_ref, g_ref, d_ref, nm_ref, nv_ref):
        g = ga_ref[...] + gb_ref[...]
        nm = ADAM_B1 * m_ref[...] + (1.0 - ADAM_B1) * g
        nv = ADAM_B2 * v_ref[...] + (1.0 - ADAM_B2) * (g * g)
        m_hat = nm / (1.0 - ADAM_B1 ** ADAM_STEP)
        v_hat = nv / (1.0 - ADAM_B2 ** ADAM_STEP)
        g_ref[...] = g
        nm_ref[...] = nm
        nv_ref[...] = nv
        d_ref[...] = -ADAM_LR * (m_hat / (jnp.sqrt(v_hat) + ADAM_EPS) + ADAM_WD * w_ref[...])

    blk = pl.BlockSpec((tr, cols), lambda i: (i, 0))
    f = jax.ShapeDtypeStruct((rows, cols), F32)
    return pl.pallas_call(
        body, name=name, grid=(rows // tr,), in_specs=[blk] * 5, out_specs=[blk] * 4, out_shape=[f] * 4,
        compiler_params=_cp(1),
    )(w, ga, gb, m, v)


PACK_ALIGN = 1024
PACK_ROWS_ALIGN = 2048


def pack_small(arrs):
    parts = []
    for a in arrs:
        flat = a.reshape(-1)
        pad = (-flat.shape[0]) % PACK_ALIGN
        parts.append(jnp.pad(flat, (0, pad)) if pad else flat)
    rows = sum(p.shape[0] for p in parts) // 128
    parts.append(jnp.zeros(((-rows) % PACK_ROWS_ALIGN * 128,), arrs[0].dtype))
    return jnp.concatenate(parts).reshape(-1, 128)


def unpack_small(packed, shapes):
    out, row = [], 0
    for shape in shapes:
        size = int(np.prod(shape))
        rows = -(-size // PACK_ALIGN) * 8
        out.append(packed[row:row + rows].reshape(-1)[:size].reshape(shape))
        row += rows
    return out


def _norm_epilogue(with_next):
    def epi(acc, res, g_post, *g_pre):
        x_new = acc * lax.rsqrt(jnp.mean(acc * acc, axis=-1, keepdims=True) + EPS) * g_post + res
        if not with_next:
            return acc, x_new
        return acc, x_new, x_new * lax.rsqrt(jnp.mean(x_new * x_new, axis=-1, keepdims=True) + EPS) * g_pre[0]
    return epi


def layer_fwd(x, h1, mem, w_in, rest_of, P, biases, g_next, after=()):
    sv = {"x0": x}
    post = dict(tm=512, tn=D_MODEL)
    proj = mm(h1, w_in, "nn", out_dtypes=[F32], name="mm_w_in", after=after)
    a_out = pool_fwd(proj, P["pool_w"], P["pool_scale"])
    os_, lses = [], []
    for g, (win, dil) in enumerate(DIL_GROUPS):
        o, lse = att_fwd(proj, biases[g], g, dil)
        os_.append(o)
        lses.append(lse)
    b_out, w0, w1, w2 = att_combine(os_, lses)
    s5p = P["s5"]
    hr, hi, y = s5_fwd(proj, s5p[2], s5p[3], s5p[0], s5p[1], s5p[4], s5p[5], P["d_skip"])
    d_out = sgu_fwd(proj, P["sgu_ln_g"], P["sgu_ln_b"], P["w_s"], P["b_s_t"])
    W, after_rest = rest_of("mixer", (a_out, b_out, y, d_out))
    W = dict(W, w_in=w_in)
    c_out = glu_fwd(y, W["w_glu"], P["b_glu"], after=after_rest)
    branches = (a_out, b_out, c_out, d_out)
    merged = merge_fwd(proj, W["gate_b"], branches, W["w_up"])
    t1, x1, h2 = mm(merged, W["w_out"], "nn", tk=1024, out_dtypes=[F32, F32, BF16], name="mm_w_out", extras=(x,),
                    vecs=(P["g_mix_post"], P["g_x_pre"]), epi=_norm_epilogue(True), after=after_rest, **post)
    sv.update(h1=h1, proj=proj, os=os_, lses=lses, wts=(w0, w1, w2), hr=hr, hi=hi, y=y, branches=branches,
              merged=merged, t1=t1, x1=x1)

    mem_n = rms_fwd(mem, P["g_mem"], BF16, "rms_mem")
    q = mm(h2, W["w_cq"], "nn", tm=1024, tn=512, tk=1024, out_dtypes=[BF16], name="mm_w_cq")
    kv = mm(mem_n, W["w_ckv"], "nn", tm=256, tn=1024, tk=1024, out_dtypes=[BF16], name="mm_w_ckv")
    ox = xatt_fwd(q, kv)
    t2, x2, h3 = mm(ox, W["w_co"], "nn", tk=512, out_dtypes=[F32, F32, BF16], name="mm_w_co", extras=(x1,),
                    vecs=(P["g_x_post"], P["g_ff_pre"]), epi=_norm_epilogue(True), **post)
    sv.update(h2=h2, mem_n=mem_n, q=q, kv=kv, ox=ox, t2=t2, x2=x2)

    W_ff, after_ff = rest_of("mlp", h3)
    W = dict(W, **W_ff)
    pre, act = mm(h3, W["w_ff1"], "nn", out_dtypes=[F32, BF16], name="mm_w_ff1",
                  epi=lambda acc: (acc, jnp.square(jnp.maximum(acc, 0.0))), after=after_ff)
    _, after_out = rest_of("out", act)
    if g_next is None:
        (ff, x3), h_next = mm(act, W["w_ff2"], "nn", out_dtypes=[F32, F32], name="mm_w_ff2_last", extras=(x2,),
                              vecs=(P["g_ff_post"],), epi=_norm_epilogue(False), after=after_out), None
    else:
        ff, x3, h_next = mm(act, W["w_ff2"], "nn", out_dtypes=[F32, F32, BF16], name="mm_w_ff2", extras=(x2,),
                            vecs=(P["g_ff_post"], g_next), epi=_norm_epilogue(True), after=after_out)
    sv.update(h3=h3, pre=pre, act=act, ff=ff, W=W)
    return x3, h_next, sv


def _pre_norm_bwd_epilogue(dh, x, add, g):
    r = lax.rsqrt(jnp.mean(x * x, axis=-1, keepdims=True) + EPS)
    xn = x * r
    dxn = dh * g
    return r * (dxn - xn * jnp.mean(dxn * xn, axis=-1, keepdims=True)) + add, jnp.sum(dh * xn, axis=0, keepdims=True)


def layer_bwd(dx, mem, W, P, biases, sv, headsum, emit, after=()):
    G = {}
    dff, G["g_ff_post"] = rms_bwd(sv["ff"], P["g_ff_post"], dx, BF16, "rms_post_bwd", after=after)
    G["w_ff2"] = mm(sv["act"], dff, "tn", out_dtypes=[F32], name="mm_dw_ff2")
    dpre = mm(dff, W["w_ff2"], "nt", out_dtypes=[BF16], name="mm_dact", extras=(sv["pre"],),
              epi=lambda acc, pre: (acc * (2.0 * jnp.maximum(pre, 0.0)),))
    G["w_ff1"] = mm(sv["h3"], dpre, "tn", out_dtypes=[F32], name="mm_dw_ff1")
    sent = emit(("w_ff1", "w_ff2"), G)
    pre_bwd = dict(out_dtypes=[F32], epi=_pre_norm_bwd_epilogue, n_sums=1)
    dx2, G["g_ff_pre"] = mm(dpre, W["w_ff1"], "nt", name="mm_dh3", extras=(sv["x2"], dx), vecs=(P["g_ff_pre"],),
                            after=sent, **pre_bwd)
    dt2, G["g_x_post"] = rms_bwd(sv["t2"], P["g_x_post"], dx2, BF16, "rms_post_bwd")
    G["w_co"] = mm(sv["ox"], dt2, "tn", tm=512, tn=1024, tk=1024, out_dtypes=[F32], name="mm_dw_co")
    dox = mm(dt2, W["w_co"], "nt", tm=1024, tn=512, tk=1024, out_dtypes=[BF16], name="mm_dox")
    dq, dkv = xatt_bwd(sv["q"], sv["kv"], dox)
    G["w_cq"] = mm(sv["h2"], dq, "tn", tm=1024, tn=512, tk=1024, out_dtypes=[F32], name="mm_dw_cq")
    G["w_ckv"] = mm(sv["mem_n"], dkv, "tn", tm=1024, tn=1024, tk=256, out_dtypes=[F32], name="mm_dw_ckv")
    dmem_n = mm(dkv, W["w_ckv"], "nt", tm=256, tn=1024, tk=1024, out_dtypes=[F32], name="mm_dmem")
    _, G["g_mem"] = rms_bwd(mem, P["g_mem"], dmem_n, BF16, "rms_mem_bwd")
    dx1, G["g_x_pre"] = mm(dq, W["w_cq"], "nt", name="mm_dh2", extras=(sv["x1"], dx2), vecs=(P["g_x_pre"],),
                           **pre_bwd)
    proj = sv["proj"]
    dt1, G["g_mix_post"] = rms_bwd(sv["t1"], P["g_mix_post"], dx1, BF16, "rms_post_bwd")
    G["w_out"] = mm(sv["merged"], dt1, "tn", tm=1024, tn=1024, tk=1024, out_dtypes=[F32], name="mm_dw_out")
    dmerged = mm(dt1, W["w_out"], "nt", tm=1024, tn=1024, tk=1024, out_dtypes=[F32], name="mm_dmerged")
    dgates, dups, G["gate_b"] = merge_bwd(proj, W["gate_b"], sv["branches"], W["w_up"], dmerged)
    dbr, dwup = [], []
    for i in range(4):
        dbr.append(mm(dups[i], W["w_up"][i], "nt", tm=1024, tn=512, tk=1024, out_dtypes=[F32], name="mm_dbranch"))
        dwup.append(mm(sv["branches"][i], dups[i], "tn", tm=512, tn=1024, tk=1024, out_dtypes=[F32], name="mm_dw_up"))
    G["w_up"] = jnp.concatenate(dwup, axis=0)
    d_pool, G["pool_w"], G["pool_scale"] = pool_bwd(proj, P["pool_w"], P["pool_scale"], dbr[0])
    cbar = att_combine_bwd(dbr[1], sv["os"], sv["wts"], headsum)
    dqs, dks, dvs, dbias = [], [], [], []
    for g, (win, dil) in enumerate(DIL_GROUPS):
        dq_g, dk_g, dv_g, db_g = att_bwd(proj, biases[g], sv["lses"][g], sv["wts"][g], dbr[1], cbar, g, dil)
        dqs.append(dq_g)
        dks.append(dk_g)
        dvs.append(dv_g)
        dbias.append(db_g)
    G["att_bias"] = dbias
    s5p = P["s5"]
    dy, G["w_glu"], G["b_glu"] = glu_bwd(sv["y"], W["w_glu"], P["b_glu"], dbr[2])
    d_ssm, dbre, dbim, dar, dai, dcre, dcim, G["d_skip"] = s5_bwd(
        proj, sv["hr"], sv["hi"], dy, s5p[2], s5p[3], s5p[0], s5p[1], s5p[4], s5p[5], P["d_skip"])
    G["s5"] = (dar, dai, dbre, dbim, dcre, dcim)
    dzu, dzv, G["sgu_ln_g"], G["sgu_ln_b"], G["w_s"], G["b_s_t"] = sgu_bwd(
        proj, P["sgu_ln_g"], P["sgu_ln_b"], P["w_s"], P["b_s_t"], dbr[3])
    d_qkv = [d.astype(BF16) for d in dqs + dks + dvs]
    dproj = jnp.concatenate([d_pool] + d_qkv + [d_ssm, dzu, dzv] + list(dgates), axis=1)
    sent = emit(("gate_b", "w_glu", "w_up", "w_out", "w_cq", "w_ckv", "w_co"), G)
    G["w_in"] = mm(sv["h1"], dproj, "tn", out_dtypes=[F32], name="mm_dw_in", after=sent)
    sent = emit(("w_in",), G)
    dx0, G["g_mix_pre"] = mm(dproj, W["w_in"], "nt", name="mm_dh1", extras=(sv["x0"], dx1), vecs=(P["g_mix_pre"],),
                             after=sent, **pre_bwd)
    return dx0, G


def _as3d(name, a):
    shape2d, axis = SHARDED[name]
    rows, cols = shape2d
    if axis == 0:
        rows //= 4
    else:
        cols //= 4
    return a.reshape(DEPTH, rows, cols)


def kernel(x, mem, rel_bias, g_mix_pre, g_mix_post, w_in, gate_b, pool_w, pool_scale, a_re, a_im, log_dt, b_re, b_im, c_re, c_im, d_skip, w_glu, b_glu, sgu_ln_g, sgu_ln_b, w_s, b_s, w_up, w_out, g_x_pre, g_x_post, g_mem, w_cq, w_ckv, w_co, g_ff_pre, g_ff_post, w_ff1, w_ff2, loss_target, m_rel_bias, m_g_mix_pre, m_g_mix_post, m_w_in, m_gate_b, m_pool_w, m_pool_scale, m_a_re, m_a_im, m_log_dt, m_b_re, m_b_im, m_c_re, m_c_im, m_d_skip, m_w_glu, m_b_glu, m_sgu_ln_g, m_sgu_ln_b, m_w_s, m_b_s, m_w_up, m_w_out, m_g_x_pre, m_g_x_post, m_g_mem, m_w_cq, m_w_ckv, m_w_co, m_g_ff_pre, m_g_ff_post, m_w_ff1, m_w_ff2, v_rel_bias, v_g_mix_pre, v_g_mix_post, v_w_in, v_gate_b, v_pool_w, v_pool_scale, v_a_re, v_a_im, v_log_dt, v_b_re, v_b_im, v_c_re, v_c_im, v_d_skip, v_w_glu, v_b_glu, v_sgu_ln_g, v_sgu_ln_b, v_w_s, v_b_s, v_w_up, v_w_out, v_g_x_pre, v_g_x_post, v_g_mem, v_w_cq, v_w_ckv, v_w_co, v_g_ff_pre, v_g_ff_post, v_w_ff1, v_w_ff2):
    env = dict(locals())
    weights = {n: env[n] for n in WEIGHT_NAMES}
    mom_m = {n: env["m_" + n] for n in WEIGHT_NAMES}
    mom_v = {n: env["v_" + n] for n in WEIGHT_NAMES}
    x2d = x.reshape(x.shape[1], D_MODEL)
    mem2d = mem.reshape(N_MEM, D_MODEL)
    target = loss_target.reshape(x2d.shape)

    axis_of = {n: SHARDED[n][1] for n in SHARDED_NAMES}
    chip = _chip_index().astype(jnp.int32).reshape(1)
    rest_names = [n for n in SHARDED_NAMES if n != "w_in"]

    def gather_start(l, names, tag, after=()):
        shards = [_as3d(n, weights[n])[l].astype(F32 if n == "gate_b" else MXU_DTYPE) for n in names]
        ax = [axis_of[n] for n in names]
        lands = [lax.empty(tuple(4 * d if i == a else d for i, d in enumerate(s.shape)), s.dtype)
                 for s, a in zip(shards, ax)]
        return (names, ax, tag) + chips_start(shards, lands, ax, "gather", f"gather_start_{tag}", after=after)

    def gather_arrive(started, after):
        names, ax, tag, send, recv, shards, lands, _ = started
        shards, lands = chips_wait(send, recv, shards, lands, ax, "gather", f"gather_wait_{tag}", after=after)
        return (names, ax, tag) + chips_start(shards, lands, ax, "forward", f"gather_forward_{tag}")

    def gather_finish(arrived, after=()):
        names, ax, tag, send, recv, shards, lands, _ = arrived
        _, lands = chips_wait(send, recv, shards, lands, ax, "forward", f"gather_landed_{tag}", after=after)
        W = dict(zip(names, lands))
        if "w_up" in W:
            W["w_up"] = W["w_up"].reshape(4, W_BRANCH, D_MODEL)
        return W

    def gather_wait(started, after):
        return gather_finish(gather_arrive(started, after))

    biases = [att_bias(rel_bias, g, dil) for g, (_, dil) in enumerate(DIL_GROUPS)]
    lanes = np.arange(W_BRANCH) // ATT_HEAD_DIM
    headsum = jnp.asarray(lanes[:, None] == lanes[None, :], dtype=BF16)

    def small_params(l, s5_prepared):
        vec = lambda a: a[l].reshape(1, -1)
        return {
            "g_mix_pre": vec(g_mix_pre), "g_mix_post": vec(g_mix_post), "g_x_pre": vec(g_x_pre), "g_x_post": vec(g_x_post),
            "g_mem": vec(g_mem), "g_ff_pre": vec(g_ff_pre), "g_ff_post": vec(g_ff_post), "pool_w": pool_w[l],
            "pool_scale": vec(pool_scale), "d_skip": vec(d_skip), "b_glu": vec(b_glu), "sgu_ln_g": vec(sgu_ln_g),
            "sgu_ln_b": vec(sgu_ln_b), "w_s": w_s[l], "b_s_t": b_s[l].T, "s5": s5_prepared,
        }

    Ws, Ps, saved, s5_vjps = [], [], [], []
    xl = x2d
    hl = rms_fwd(x2d, g_mix_pre[0].reshape(1, -1), BF16, "rms_pre")
    flying = {"next": gather_start(0, ["w_in"], "0_w_in")}
    for l in range(DEPTH):
        s5_prepared, s5_vjp = jax.vjp(s5_params, a_re[l], a_im[l], log_dt[l], b_re[l], b_im[l], c_re[l], c_im[l])
        token_of = lambda started: (started[7],)
        if l == 0:
            w_in_l = gather_wait(flying["next"], [*biases, hl])["w_in"]
            flying["rest"] = gather_start(0, rest_names, "0_rest", after=[w_in_l])
            flying["next"] = gather_start(1, SHARDED_NAMES, "1", after=token_of(flying["rest"]))
            first_after = token_of(flying["rest"]) + token_of(flying["next"])

            W_l = None
        else:
            W_l = gather_finish(flying["next"], [xl])
            w_in_l, first_after = W_l["w_in"], ()
            if l + 1 < DEPTH:
                flying["next"] = gather_start(l + 1, SHARDED_NAMES, str(l + 1), after=[w_in_l])
                first_after = token_of(flying["next"])

        def rest_of(stage, value, l=l, W_l=W_l):
            if stage == "mixer" and l == 0:
                return gather_wait(flying["rest"], list(value)), ()
            if stage == "mixer":
                return W_l, ()
            if stage == "out" and l + 1 < DEPTH:
                flying["next"] = gather_arrive(flying["next"], [value])
                return {}, token_of(flying["next"])
            return {}, ()
        P = small_params(l, s5_prepared)
        g_next = g_mix_pre[l + 1].reshape(1, -1) if l + 1 < DEPTH else None
        xl, hl, sv = layer_fwd(xl, hl, mem2d, w_in_l, rest_of, P, biases, g_next, after=first_after)
        Ws.append(sv["W"])
        Ps.append(P)
        saved.append(sv)
        s5_vjps.append(s5_vjp)
    loss_local, dx = loss_and_grad(xl, target)
    loss = lax.psum(loss_local, ("x", "y", "c"))

    scattered = []

    def scatter_start(l, names, srcs):
        ax = [axis_of.get(n) for n in names]
        lands = []
        for s, a in zip(srcs, ax):
            r, c = s.shape
            lands.append(lax.empty((4, r // 4 if a == 0 else r, c // 4 if a == 1 else c), F32))
        tag = f"{l}_{names[0]}"
        send, recv, srcs, lands, token = chips_start(srcs, lands, ax, "scatter", f"grads_start_{tag}")
        scattered.append((l, names, ax, tag, send, recv, srcs, lands))
        return (token,)

    grads = [None] * DEPTH
    for l in reversed(range(DEPTH)):
        emit = lambda names, G, l=l: scatter_start(l, list(names), [G[n] for n in names])
        dx, grads[l] = layer_bwd(dx, mem2d, Ws[l], Ps[l], biases, saved[l], headsum, emit)
    grad_x = dx.reshape(x.shape)

    rep = {}
    stack = lambda key, shape: jnp.stack([grads[l][key] for l in range(DEPTH)]).reshape(shape)
    for n in ("g_mix_pre", "g_mix_post", "g_x_pre", "g_x_post", "g_mem", "g_ff_pre", "g_ff_post"):
        rep[n] = stack(n, (DEPTH, D_MODEL))
    for n in ("pool_scale", "d_skip", "b_glu", "sgu_ln_g", "sgu_ln_b"):
        rep[n] = stack(n, (DEPTH, W_BRANCH))
    rep["pool_w"] = stack("pool_w", pool_w.shape)
    rep["w_s"] = stack("w_s", w_s.shape)
    rep["b_s"] = jnp.stack([grads[l]["b_s_t"][:, :4].T for l in range(DEPTH)])
    s5_grads = [s5_vjps[l](tuple(grads[l]["s5"])) for l in range(DEPTH)]
    for i, n in enumerate(("a_re", "a_im", "log_dt", "b_re", "b_im", "c_re", "c_im")):
        rep[n] = jnp.stack([s5_grads[l][i] for l in range(DEPTH)])
    dbias = [sum(grads[l]["att_bias"][g] for l in range(DEPTH)) for g in range(len(DIL_GROUPS))]
    rep["rel_bias"] = jnp.concatenate([att_bias_grad(dbias[g], dil) for g, (_, dil) in enumerate(DIL_GROUPS)], axis=1)
    rep_shapes = [weights[n].shape for n in REPLICATED_NAMES]
    packed_g = pack_small([rep[n] for n in REPLICATED_NAMES])

    small_sent = scatter_start(0, ["small"], [packed_g])
    stacked = {}

    def collect(record, after):
        l, names, ax, tag, send, recv, srcs, lands = record
        srcs, lands = chips_wait(send, recv, srcs, lands, ax, "scatter", f"grads_wait_{tag}", after=after)
        for n, own, arrived, a in zip(names, srcs, lands, ax):
            if n not in stacked:
                stacked[n] = lax.empty((1 if n == "small" else DEPTH,) + arrived.shape[1:], F32)
            stacked[n] = sum_chips(arrived, own, a, chip, stacked[n], 0 if n == "small" else l, "sum_chips")

    out_g, out_d, out_m, out_v = {}, {}, {}, {}

    def update(names, tag):
        partial = [stacked[n].reshape(-1, stacked[n].shape[-1]) for n in names]
        other = swap_cores(partial, f"swap_cores_{tag}")
        for n, mine, theirs in zip(names, partial, other):
            if n == "small":
                for name, ga, gb in zip(REPLICATED_NAMES, unpack_small(mine, rep_shapes), unpack_small(theirs, rep_shapes)):
                    rows_of = lambda a: a.reshape(-1, a.shape[-1])
                    res = adamw(rows_of(weights[name]), rows_of(ga), rows_of(gb), rows_of(mom_m[name]),
                                rows_of(mom_v[name]), "adamw_small")
                    out_g[name], out_d[name], out_m[name], out_v[name] = [r.reshape(weights[name].shape) for r in res]
            else:
                flat = lambda a: a.reshape(mine.shape)
                res = adamw(flat(weights[n]), mine, theirs, flat(mom_m[n]), flat(mom_v[n]), "adamw")
                out_g[n], out_d[n], out_m[n], out_v[n] = [r.reshape(weights[n].shape) for r in res]

    late = [r for r in scattered if r[1] == ["small"] or (r[0] == 0 and r[1] == ["w_in"])]
    for record in scattered:
        if not any(record is r for r in late):
            collect(record, [dx, *small_sent])
    update(rest_names, "rest")
    collect(late[0], [out_d[n] for n in rest_names])
    update(["w_in"], "w_in")
    collect(late[1], [out_d["w_in"]])
    update(["small"], "small")

    return (loss, grad_x, *[out_g[n] for n in WEIGHT_NAMES], *[out_d[n] for n in WEIGHT_NAMES],
            *[out_m[n] for n in WEIGHT_NAMES], *[out_v[n] for n in WEIGHT_NAMES])
```

```python
import functools
import math

import numpy as np
import jax
import jax.numpy as jnp
from jax import lax
from jax.experimental import pallas as pl
from jax.experimental.pallas import tpu as pltpu

F32 = jnp.float32
BF16 = jnp.bfloat16
MXU_DTYPE = jnp.bfloat16
MESH_ID = pl.DeviceIdType.MESH
VMEM_LIMIT_BYTES = 56 * 1024 * 1024

D_MODEL = 1024
DEPTH = 4
N_MEM = 256
W_BRANCH = 512
POOL_WINDOWS = (2, 4, 8, 16)
POOL_HALO = 16
DIL_GROUPS = ((128, 1), (512, 4), (2048, 16))
BAND = 128
ATT_HEADS = 8
ATT_HEAD_DIM = 64
SSM_GROUP = 16
SSM_GROUPS = 32
SSM_STATE = 64
SSM_COLS = SSM_GROUPS * SSM_STATE
SSM_T = 512
SGU_CHUNK = 128
X_HEADS = 4
X_HEAD_DIM = 128
D_FF = 4096
REL_BUCKETS = 32
REL_MAX_DIST = 2048
EPS = 1e-6
NEG_INF = -1e30
OFF_POOL = 0
OFF_ATT = 512
OFF_SSM = OFF_ATT + 9 * W_BRANCH
OFF_SGU = OFF_SSM + W_BRANCH
OFF_GATE = OFF_SGU + 2 * W_BRANCH
IN_WIDTH = OFF_GATE + 4 * D_MODEL

ADAM_LR = 0.001
ADAM_B1 = 0.9
ADAM_B2 = 0.999
ADAM_EPS = 1e-08
ADAM_WD = 0.01
ADAM_STEP = 10

GELU_C = math.sqrt(2.0 / math.pi)

WEIGHT_NAMES = ['rel_bias', 'g_mix_pre', 'g_mix_post', 'w_in', 'gate_b', 'pool_w', 'pool_scale', 'a_re', 'a_im',
                'log_dt', 'b_re', 'b_im', 'c_re', 'c_im', 'd_skip', 'w_glu', 'b_glu', 'sgu_ln_g', 'sgu_ln_b',
                'w_s', 'b_s', 'w_up', 'w_out', 'g_x_pre', 'g_x_post', 'g_mem', 'w_cq', 'w_ckv', 'w_co',
                'g_ff_pre', 'g_ff_post', 'w_ff1', 'w_ff2']
SHARDED = {
    'w_in': ((D_MODEL, IN_WIDTH), 1),
    'gate_b': ((4, D_MODEL), 1),
    'w_glu': ((W_BRANCH, W_BRANCH), 0),
    'w_up': ((4 * W_BRANCH, D_MODEL), 1),
    'w_out': ((D_MODEL, D_MODEL), 0),
    'w_cq': ((D_MODEL, W_BRANCH), 0),
    'w_ckv': ((D_MODEL, D_MODEL), 0),
    'w_co': ((W_BRANCH, D_MODEL), 1),
    'w_ff1': ((D_MODEL, D_FF), 1),
    'w_ff2': ((D_FF, D_MODEL), 0),
}
SHARDED_NAMES = list(SHARDED)
REPLICATED_NAMES = [n for n in WEIGHT_NAMES if n not in SHARDED]


def _cp(n_axes):
    return pltpu.CompilerParams(dimension_semantics=("arbitrary",) * n_axes, vmem_limit_bytes=VMEM_LIMIT_BYTES)


def _dot(a, b, dims="nn"):
    cd = {"nn": ((1,), (0,)), "nt": ((1,), (1,)), "tn": ((0,), (0,))}[dims]
    return lax.dot_general(a.astype(MXU_DTYPE), b.astype(MXU_DTYPE), (cd, ((), ())), preferred_element_type=F32)


def _gelu(x):
    return 0.5 * x * (1.0 + jnp.tanh(GELU_C * (x + 0.044715 * (x * x * x))))


def _gelu_grad(x):
    t = jnp.tanh(GELU_C * (x + 0.044715 * (x * x * x)))
    return 0.5 * (1.0 + t) + 0.5 * x * (1.0 - t * t) * (GELU_C * (1.0 + 3.0 * 0.044715 * (x * x)))


def _sigmoid(x):
    return 1.0 / (1.0 + jnp.exp(-x))


MM_TILES = {
    "mm_w_in": (2048, 1536, 1024), "mm_dw_in": (1024, 1536, 2048), "mm_dh1": (1024, 1024, 1536),
    "mm_w_ff1": (2048, 1024, 1024), "mm_w_ff2": (1024, 1024, 2048), "mm_w_ff2_last": (1024, 1024, 2048),
    "mm_dw_ff2": (1024, 1024, 2048), "mm_dact": (2048, 1024, 1024), "mm_dw_ff1": (1024, 1024, 2048),
    "mm_dh3": (1024, 1024, 2048), "mm_dh2": (1024, 1024, 512),
}


def mm(a, b, dims, *, out_dtypes, name, tm=None, tn=None, tk=None, extras=(), vecs=(), epi=None, n_sums=0, after=()):
    if dims == "tn":
        K, M = a.shape
        N = b.shape[1]
    else:
        M, K = a.shape
        N = b.shape[1] if dims == "nn" else b.shape[0]
    if tm is None:
        tm, tn, tk = MM_TILES[name]
    tm, tn, tk = min(tm, M), min(tn, N), min(tk, K)
    assert M % tm == 0 and N % tn == 0 and K % tk == 0, (name, M, N, K, tm, tn, tk)
    assert n_sums == 0 or tn == N, name
    nk = K // tk
    ne, no = len(extras) + len(vecs), len(out_dtypes)
    if epi is None:
        epi = lambda acc: (acc,)
    a_spec = (pl.BlockSpec((tk, tm), lambda i, j, k: (k, i)) if dims == "tn"
              else pl.BlockSpec((tm, tk), lambda i, j, k: (i, k)))
    b_spec = (pl.BlockSpec((tn, tk), lambda i, j, k: (j, k)) if dims == "nt"
              else pl.BlockSpec((tk, tn), lambda i, j, k: (k, j)))
    mn_spec = pl.BlockSpec((tm, tn), lambda i, j, k: (i, j))
    vec_spec = pl.BlockSpec((1, tn), lambda i, j, k: (0, j))

    def body(a_ref, b_ref, *rest):
        first_out = ne + len(after)
        extra_refs, out_refs = rest[:ne], rest[first_out:first_out + no]
        sum_refs = rest[first_out + no:first_out + no + n_sums]
        part = _dot(a_ref[...], b_ref[...], dims)

        def finish(acc):
            results = epi(acc, *[e[...] for e in extra_refs])
            for o_ref, r in zip(out_refs, results[:no]):
                o_ref[...] = r.astype(o_ref.dtype)
            for s_ref, r in zip(sum_refs, results[no:]):
                @pl.when(pl.program_id(0) == 0)
                def _():
                    s_ref[...] = r

                @pl.when(pl.program_id(0) > 0)
                def _():
                    s_ref[...] += r

        if nk == 1:
            finish(part)
        else:
            acc_ref = rest[-1]
            k = pl.program_id(2)

            @pl.when(k == 0)
            def _():
                acc_ref[...] = part

            @pl.when(k > 0)
            def _():
                acc_ref[...] += part

            @pl.when(k == nk - 1)
            def _():
                finish(acc_ref[...])

    outs = pl.pallas_call(
        body, name=name, grid=(M // tm, N // tn, nk),
        in_specs=[a_spec, b_spec] + [mn_spec] * len(extras) + [vec_spec] * len(vecs) + [ANY] * len(after),
        out_specs=[mn_spec] * no + [vec_spec] * n_sums,
        out_shape=[jax.ShapeDtypeStruct((M, N), dt) for dt in out_dtypes] + [jax.ShapeDtypeStruct((1, N), F32)] * n_sums,
        scratch_shapes=[pltpu.VMEM((tm, tn), F32)] if nk > 1 else [],
        compiler_params=_cp(3),
    )(a, b, *extras, *vecs, *after)
    return outs[0] if no + n_sums == 1 else outs


ROW_TILE = 512


def rms_fwd(x, g, out_dtype, name, res=None):
    M, D = x.shape
    tm = min(ROW_TILE, M)

    def body(x_ref, g_ref, *rest):
        o_ref = rest[-1]
        xf = x_ref[...]
        y = xf * lax.rsqrt(jnp.mean(xf * xf, axis=-1, keepdims=True) + EPS) * g_ref[...]
        if res is not None:
            y = y + rest[0][...]
        o_ref[...] = y.astype(o_ref.dtype)

    row = pl.BlockSpec((tm, D), lambda i: (i, 0))
    return pl.pallas_call(
        body, name=name, grid=(M // tm,),
        in_specs=[row, pl.BlockSpec((1, D), lambda i: (0, 0))] + ([row] if res is not None else []),
        out_specs=row, out_shape=jax.ShapeDtypeStruct((M, D), out_dtype), compiler_params=_cp(1),
    )(x, g, *([res] if res is not None else []))


def rms_bwd(x, g, dy, dx_dtype, name, add=None, after=()):
    M, D = x.shape
    tm = min(ROW_TILE, M)

    def body(x_ref, g_ref, dy_ref, *rest):
        dx_ref, dg_ref = rest[-2], rest[-1]
        xf = x_ref[...]
        dyf = dy_ref[...].astype(F32)
        r = lax.rsqrt(jnp.mean(xf * xf, axis=-1, keepdims=True) + EPS)
        xn = xf * r
        dxn = dyf * g_ref[...]
        dx = r * (dxn - xn * jnp.mean(dxn * xn, axis=-1, keepdims=True))
        if add is not None:
            dx = dx + rest[0][...]
        dx_ref[...] = dx.astype(dx_ref.dtype)
        dg = jnp.sum(dyf * xn, axis=0, keepdims=True)

        @pl.when(pl.program_id(0) == 0)
        def _():
            dg_ref[...] = dg

        @pl.when(pl.program_id(0) > 0)
        def _():
            dg_ref[...] += dg

    row = pl.BlockSpec((tm, D), lambda i: (i, 0))
    vec = pl.BlockSpec((1, D), lambda i: (0, 0))
    return pl.pallas_call(
        body, name=name, grid=(M // tm,),
        in_specs=[row, vec, row] + ([row] if add is not None else []) + [ANY] * len(after),
        out_specs=[row, vec],
        out_shape=[jax.ShapeDtypeStruct((M, D), dx_dtype), jax.ShapeDtypeStruct((1, D), F32)],
        compiler_params=_cp(1),
    )(x, g, dy, *([add] if add is not None else []), *after)


def loss_and_grad(y, target):
    M, D = y.shape
    tm = ROW_TILE

    def body(y_ref, t_ref, part_ref, dy_ref):
        e = y_ref[...] - t_ref[...]
        dy_ref[...] = e / D
        part_ref[...] = jnp.broadcast_to(0.5 * jnp.sum(jnp.mean(e * e, axis=-1, keepdims=True), axis=0, keepdims=True),
                                         (8, 128))

    row = pl.BlockSpec((tm, D), lambda i: (i, 0))
    part, dy = pl.pallas_call(
        body, name="loss", grid=(M // tm,), in_specs=[row, row],
        out_specs=[pl.BlockSpec((8, 128), lambda i: (i, 0)), row],
        out_shape=[jax.ShapeDtypeStruct((8 * (M // tm), 128), F32), jax.ShapeDtypeStruct((M, D), F32)],
        compiler_params=_cp(1),
    )(y, target)
    return jnp.sum(part[::8, 0]), dy


POOL_ROWS = 512


def _pool_window_sum(xw, gi, roll_of):
    s1 = xw + pltpu.roll(xw, roll_of(1), 0)
    s2 = s1 + pltpu.roll(s1, roll_of(2), 0)
    s3 = s2 + pltpu.roll(s2, roll_of(4), 0)
    s4 = s3 + pltpu.roll(s3, roll_of(8), 0)
    return jnp.where(gi == 0, s1, jnp.where(gi == 1, s2, jnp.where(gi == 2, s3, s4)))


def _pool_cnt(i, gi):
    rows = lax.broadcasted_iota(jnp.int32, (POOL_ROWS, 128), 0) + i * POOL_ROWS
    w = jnp.where(gi == 0, 2, jnp.where(gi == 1, 4, jnp.where(gi == 2, 8, 16)))
    return jnp.minimum(rows + 1, w).astype(F32)


def pool_fwd(proj, pool_w, scale):
    S = proj.shape[0]
    nchunk = S // POOL_ROWS
    slab = POOL_ROWS + POOL_HALO

    def body(x_ref, w_ref, sc_ref, o_ref, pad_ref):
        gi = pl.program_id(0)
        pad_ref[0:POOL_HALO, :] = jnp.zeros((POOL_HALO, 128), F32)
        pad_ref[POOL_HALO:, :] = x_ref[...]
        for i in range(nchunk):
            xw = pad_ref[i * POOL_ROWS:i * POOL_ROWS + slab, :]
            ssum = _pool_window_sum(xw, gi, lambda d: d)[POOL_HALO:, :]
            p = ssum / _pool_cnt(i, gi) - xw[POOL_HALO:, :]
            o_ref[i * POOL_ROWS:(i + 1) * POOL_ROWS, :] = (_dot(p, w_ref[...]) * sc_ref[...]).astype(o_ref.dtype)

    return pl.pallas_call(
        body, name="pool_fwd", grid=(4,),
        in_specs=[pl.BlockSpec((S, 128), lambda g: (0, OFF_POOL // 128 + g)),
                  pl.BlockSpec((None, 128, 128), lambda g: (g, 0, 0)),
                  pl.BlockSpec((1, 128), lambda g: (0, g))],
        out_specs=pl.BlockSpec((S, 128), lambda g: (0, g)),
        out_shape=jax.ShapeDtypeStruct((S, W_BRANCH), BF16),
        scratch_shapes=[pltpu.VMEM((S + POOL_HALO, 128), F32)],
        compiler_params=_cp(1),
    )(proj, pool_w, scale)


def pool_bwd(proj, pool_w, scale, dy):
    S = proj.shape[0]
    nchunk = S // POOL_ROWS
    slab = POOL_ROWS + POOL_HALO

    def body(x_ref, w_ref, sc_ref, dy_ref, dx_ref, dw_ref, dsc_ref, pad_ref, pad2_ref, dp_ref):
        gi = pl.program_id(0)
        pad_ref[0:POOL_HALO, :] = jnp.zeros((POOL_HALO, 128), F32)
        pad_ref[POOL_HALO:, :] = x_ref[...]
        pad2_ref[S:, :] = jnp.zeros((POOL_HALO, 128), F32)
        dw = jnp.zeros((128, 128), F32)
        dsc = jnp.zeros((1, 128), F32)
        for i in range(nchunk):
            xw = pad_ref[i * POOL_ROWS:i * POOL_ROWS + slab, :]
            cnt = _pool_cnt(i, gi)
            p = _pool_window_sum(xw, gi, lambda d: d)[POOL_HALO:, :] / cnt - xw[POOL_HALO:, :]
            dyc = dy_ref[i * POOL_ROWS:(i + 1) * POOL_ROWS, :]
            dsc = dsc + jnp.sum(dyc * _dot(p, w_ref[...]), axis=0, keepdims=True)
            dys = dyc * sc_ref[...]
            dw = dw + _dot(p, dys, "tn")
            dp = _dot(dys, w_ref[...], "nt")
            dp_ref[i * POOL_ROWS:(i + 1) * POOL_ROWS, :] = dp
            pad2_ref[i * POOL_ROWS:(i + 1) * POOL_ROWS, :] = dp / cnt
        dw_ref[...] = dw
        dsc_ref[...] = dsc
        for i in range(nchunk):
            xw = pad2_ref[i * POOL_ROWS:i * POOL_ROWS + slab, :]
            fsum = _pool_window_sum(xw, gi, lambda d: slab - d)[:POOL_ROWS, :]
            rows = slice(i * POOL_ROWS, (i + 1) * POOL_ROWS)
            dx_ref[rows, :] = (fsum - dp_ref[rows, :]).astype(dx_ref.dtype)

    return pl.pallas_call(
        body, name="pool_bwd", grid=(4,),
        in_specs=[pl.BlockSpec((S, 128), lambda g: (0, OFF_POOL // 128 + g)),
                  pl.BlockSpec((None, 128, 128), lambda g: (g, 0, 0)),
                  pl.BlockSpec((1, 128), lambda g: (0, g)),
                  pl.BlockSpec((S, 128), lambda g: (0, g))],
        out_specs=[pl.BlockSpec((S, 128), lambda g: (0, g)),
                   pl.BlockSpec((None, 128, 128), lambda g: (g, 0, 0)),
                   pl.BlockSpec((1, 128), lambda g: (0, g))],
        out_shape=[jax.ShapeDtypeStruct((S, W_BRANCH), BF16), jax.ShapeDtypeStruct((4, 128, 128), F32),
                   jax.ShapeDtypeStruct((1, W_BRANCH), F32)],
        scratch_shapes=[pltpu.VMEM((S + POOL_HALO, 128), F32), pltpu.VMEM((S + POOL_HALO, 128), F32),
                        pltpu.VMEM((S, 128), F32)],
        compiler_params=_cp(1),
    )(proj, pool_w, scale, dy)


def _t5_bucket(n):
    exact = REL_BUCKETS // 2
    nf = np.maximum(n, 1).astype(np.float32)
    large = exact + (np.log(nf / exact) / np.log(REL_MAX_DIST / exact) * (REL_BUCKETS - exact)).astype(np.int32)
    large = np.minimum(large, REL_BUCKETS - 1)
    return np.where(n < exact, n, large).astype(np.int32)


def _band_onehot(dil):
    i = np.arange(BAND)[:, None]
    kk = np.arange(2 * BAND)[None, :]
    dist = BAND + i - kk
    local = (dist >= 0) & (dist <= BAND)
    bucket = _t5_bucket(np.clip(dist, 0, BAND) * dil)
    onehot = (bucket.reshape(-1, 1) == np.arange(REL_BUCKETS)[None, :]).astype(np.float32)
    return onehot, local


def att_bias(rel_bias, g, dil):
    onehot, local = _band_onehot(dil)
    tab = jnp.dot(jnp.asarray(onehot), rel_bias[:, g * ATT_HEADS:(g + 1) * ATT_HEADS], precision=lax.Precision.HIGHEST)
    bias = tab.reshape(BAND, 2 * BAND, ATT_HEADS).transpose(2, 0, 1)
    return jnp.where(jnp.asarray(local)[None], bias, NEG_INF)


def att_bias_grad(dbias, dil):
    onehot, _ = _band_onehot(dil)
    flat = dbias.transpose(1, 2, 0).reshape(BAND * 2 * BAND, ATT_HEADS)
    return jnp.dot(jnp.asarray(onehot).T, flat, precision=lax.Precision.HIGHEST)


def _head_lanes():
    return lax.broadcasted_iota(jnp.int32, (BAND, 128), 1) < ATT_HEAD_DIM


def _att_cols(part, g, hp):
    return (OFF_ATT + part * 3 * W_BRANCH + g * W_BRANCH) // 128 + hp


def _att_pair(q, k, v, bias, lse_b, do, delta_b, hh, head0, mask=None):
    sel = head0 if hh == 0 else jnp.logical_not(head0)
    s = _dot(jnp.where(sel, q, 0.0), k, "nt") * (ATT_HEAD_DIM ** -0.5) + bias
    if mask is not None:
        s = jnp.where(mask, NEG_INF, s)
    c = hh * ATT_HEAD_DIM
    p = jnp.exp(s - lse_b[:, c:c + 1])
    dp = _dot(jnp.where(sel, do, 0.0), v, "nt")
    return p, p * (dp - delta_b[:, c:c + 1])


ATT_BLOCKS = {1: 32, 4: 8, 16: 2}


def _att_rows(r, i, d):
    return pl.ds(r + d * BAND * i, BAND, stride=d) if d > 1 else pl.ds(BAND * i, BAND)


def _att_specs(g, d, nq):
    ch, pb = BAND * d * nq, BAND * d
    cur = lambda part: pl.BlockSpec((ch, 128), lambda hp, n: (n, _att_cols(part, g, hp)))
    prev = lambda part: pl.BlockSpec((pb, 128), lambda hp, n: (jnp.maximum(n * nq - 1, 0), _att_cols(part, g, hp)))
    return [cur(0), cur(1), prev(1), cur(2), prev(2)]


def _att_keys(cur_ref, prev_ref, r, i, d):
    before = cur_ref[_att_rows(r, i - 1, d), :] if i > 0 else prev_ref[_att_rows(r, 0, d), :]
    return jnp.concatenate([before, cur_ref[_att_rows(r, i, d), :]], axis=0).astype(MXU_DTYPE)


def att_fwd(proj, bias, g, d):
    S = proj.shape[0]
    nq = ATT_BLOCKS[d]
    ch = BAND * d * nq

    def body(q_ref, kc_ref, kp_ref, vc_ref, vp_ref, b_ref, o_ref, l_ref):
        n = pl.program_id(1)
        head0 = _head_lanes()
        first = jnp.logical_and(lax.broadcasted_iota(jnp.int32, (BAND, 2 * BAND), 1) < BAND, n == 0)
        for r in range(d):
            for i in range(nq):
                rows = _att_rows(r, i, d)
                q = q_ref[rows, :]
                k = _att_keys(kc_ref, kp_ref, r, i, d)
                v = _att_keys(vc_ref, vp_ref, r, i, d)
                o_h, l_h = [], []
                for hh in range(2):
                    qm = jnp.where(head0 if hh == 0 else jnp.logical_not(head0), q, 0.0)
                    s = _dot(qm, k, "nt") * (ATT_HEAD_DIM ** -0.5) + b_ref[hh]
                    if i == 0:
                        s = jnp.where(first, NEG_INF, s)
                    m = jnp.max(s, axis=-1, keepdims=True)
                    p = jnp.exp(s - m)
                    l = jnp.sum(p, axis=-1, keepdims=True)
                    o_h.append(_dot(p / l, v))
                    l_h.append(jnp.broadcast_to(m + jnp.log(l), (BAND, 128)))
                o_ref[rows, :] = jnp.where(head0, o_h[0], o_h[1])
                l_ref[rows, :] = jnp.where(head0, l_h[0], l_h[1])

    out = pl.BlockSpec((ch, 128), lambda hp, n: (n, hp))
    return pl.pallas_call(
        body, name=f"att_fwd_d{d}", grid=(4, S // ch),
        in_specs=_att_specs(g, d, nq) + [pl.BlockSpec((2, BAND, 2 * BAND), lambda hp, n: (hp, 0, 0))],
        out_specs=[out, out],
        out_shape=[jax.ShapeDtypeStruct((S, W_BRANCH), F32), jax.ShapeDtypeStruct((S, W_BRANCH), F32)],
        compiler_params=_cp(2),
    )(proj, proj, proj, proj, proj, bias)


def att_bwd(proj, bias, lse, wts, dout, cbar, g, d):
    S = proj.shape[0]
    nq = ATT_BLOCKS[d]
    ch, pb = BAND * d * nq, BAND * d
    nb = S // ch
    scale = ATT_HEAD_DIM ** -0.5

    def body(q_ref, kc_ref, kp_ref, vc_ref, vp_ref, b_ref, l_ref, w_ref, do_ref, cb_ref,
             dq_ref, dk_ref, dv_ref, ek_ref, ev_ref, db_ref):
        n = pl.program_id(1)
        head0 = _head_lanes()
        first = jnp.logical_and(lax.broadcasted_iota(jnp.int32, (BAND, 2 * BAND), 1) < BAND, n == 0)

        @pl.when(n == 0)
        def _():
            db_ref[...] = jnp.zeros(db_ref.shape, F32)

        for r in range(d):
            own_k = own_v = None
            for i in range(nq):
                rows = _att_rows(r, i, d)
                q = q_ref[rows, :]
                k = _att_keys(kc_ref, kp_ref, r, i, d)
                v = _att_keys(vc_ref, vp_ref, r, i, d)
                w = w_ref[rows, :]
                do = w * do_ref[rows, :]
                delta = w * cb_ref[rows, :]
                lse_b = l_ref[rows, :]
                dq_h, dk_h, dv_h = [], [], []
                for hh in range(2):
                    p, ds = _att_pair(q, k, v, b_ref[hh], lse_b, do, delta, hh, head0, mask=first if i == 0 else None)
                    db_ref[hh] += ds
                    ds = ds * scale
                    dq_h.append(_dot(ds, k))
                    dk_h.append(_dot(ds, q, "tn"))
                    dv_h.append(_dot(p, do, "tn"))
                dq_ref[rows, :] = jnp.where(head0, dq_h[0], dq_h[1])
                head0_keys = jnp.concatenate([head0, head0], axis=0)
                dk2 = jnp.where(head0_keys, dk_h[0], dk_h[1])
                dv2 = jnp.where(head0_keys, dv_h[0], dv_h[1])
                if i == 0:
                    ek_ref[_att_rows(r, 0, d), :] = dk2[:BAND]
                    ev_ref[_att_rows(r, 0, d), :] = dv2[:BAND]
                else:
                    dk_ref[_att_rows(r, i - 1, d), :] = own_k + dk2[:BAND]
                    dv_ref[_att_rows(r, i - 1, d), :] = own_v + dv2[:BAND]
                own_k, own_v = dk2[BAND:], dv2[BAND:]
            dk_ref[_att_rows(r, nq - 1, d), :] = own_k
            dv_ref[_att_rows(r, nq - 1, d), :] = own_v

    cur = pl.BlockSpec((ch, 128), lambda hp, n: (n, hp))
    edge = pl.BlockSpec((pb, 128), lambda hp, n: (n, hp))
    bias_spec = pl.BlockSpec((2, BAND, 2 * BAND), lambda hp, n: (hp, 0, 0))
    big = jax.ShapeDtypeStruct((S, W_BRANCH), F32)
    small = jax.ShapeDtypeStruct((nb * pb, W_BRANCH), F32)
    dq, dk, dv, ek, ev, db = pl.pallas_call(
        body, name=f"att_bwd_d{d}", grid=(4, nb),
        in_specs=_att_specs(g, d, nq) + [bias_spec, cur, cur, cur, cur],
        out_specs=[cur, cur, cur, edge, edge, bias_spec],
        out_shape=[big, big, big, small, small, jax.ShapeDtypeStruct((ATT_HEADS, BAND, 2 * BAND), F32)],
        compiler_params=_cp(2),
    )(proj, proj, proj, proj, proj, bias, lse, wts, dout, cbar)

    def with_edges(main, edges):
        if nb == 1:
            return main
        main = main.reshape(nb, ch, W_BRANCH)
        add = jnp.pad(edges.reshape(nb, pb, W_BRANCH)[1:], ((0, 1), (ch - pb, 0), (0, 0)))
        return (main + add).reshape(S, W_BRANCH)

    return dq, with_edges(dk, ek), with_edges(dv, ev), db


def att_combine(os_, lses):
    S = os_[0].shape[0]

    def body(o0, o1, o2, l0, l1, l2, out_ref, w0, w1, w2):
        ls = [l0[...], l1[...], l2[...]]
        m = jnp.maximum(jnp.maximum(ls[0], ls[1]), ls[2])
        es = [jnp.exp(l - m) for l in ls]
        den = es[0] + es[1] + es[2]
        ws = [e / den for e in es]
        out_ref[...] = (ws[0] * o0[...] + ws[1] * o1[...] + ws[2] * o2[...]).astype(out_ref.dtype)
        for w_ref, w in zip((w0, w1, w2), ws):
            w_ref[...] = w

    blk = pl.BlockSpec((ROW_TILE, W_BRANCH), lambda i: (i, 0))
    f = jax.ShapeDtypeStruct((S, W_BRANCH), F32)
    return pl.pallas_call(
        body, name="att_combine", grid=(S // ROW_TILE,), in_specs=[blk] * 6, out_specs=[blk] * 4,
        out_shape=[jax.ShapeDtypeStruct((S, W_BRANCH), BF16), f, f, f], compiler_params=_cp(1),
    )(*os_, *lses)


def _split3(x):
    x1 = x.astype(BF16)
    r1 = x - x1.astype(F32)
    x2 = r1.astype(BF16)
    x3 = (r1 - x2.astype(F32)).astype(BF16)
    return x1, x2, x3


def att_combine_bwd(dout, os_, wts, headsum):
    S = dout.shape[0]

    def body(do_ref, o0, o1, o2, w0, w1, w2, e_ref, cb_ref):
        out = w0[...] * o0[...] + w1[...] * o1[...] + w2[...] * o2[...]
        e = e_ref[...]
        acc = jnp.zeros((ROW_TILE, W_BRANCH), F32)
        for term in _split3(do_ref[...] * out):
            acc = acc + jnp.dot(term, e, preferred_element_type=F32)
        cb_ref[...] = acc

    blk = pl.BlockSpec((ROW_TILE, W_BRANCH), lambda i: (i, 0))
    return pl.pallas_call(
        body, name="att_combine_bwd", grid=(S // ROW_TILE,),
        in_specs=[blk] * 7 + [pl.BlockSpec((W_BRANCH, W_BRANCH), lambda i: (0, 0))], out_specs=blk,
        out_shape=jax.ShapeDtypeStruct((S, W_BRANCH), F32), compiler_params=_cp(1),
    )(dout, *os_, *wts, headsum)


def _cmul(ar, ai, br, bi):
    return ar * br - ai * bi, ar * bi + ai * br


SCAN_ROWS = 8
SCAN_GROUPS = SSM_T // SCAN_ROWS


def _log_scan(xr, xi, mr, mi, rows, n, steps, reverse):
    total = xr.shape[0]
    for k in range(steps):
        dd = 1 << k
        keep = rows < n - dd if reverse else rows >= dd
        shift = total - dd if reverse else dd
        ar, ai = _cmul(mr, mi, jnp.where(keep, pltpu.roll(xr, shift, 0), 0.0), jnp.where(keep, pltpu.roll(xi, shift, 0), 0.0))
        xr, xi = xr + ar, xi + ai
        mr, mi = _cmul(mr, mi, mr, mi)
    return xr, xi, mr, mi


def _scan_scratch(n_results):
    return ([pltpu.VMEM((W_BRANCH // 128, SSM_T, 128), F32)] * 2 + [pltpu.VMEM((SCAN_GROUPS, W_BRANCH), F32)] * 2
            + [pltpu.VMEM((SSM_T, W_BRANCH), F32)] * n_results)


def _block_scan(xr, xi, mr, mi, reverse, yr_ref, yi_ref, er_ref, ei_ref, hr_ref, hi_ref):
    cols = xr.shape[1]
    rows = lax.broadcasted_iota(jnp.int32, (SSM_T, cols), 0)
    yr, yi, m8r, m8i = _log_scan(xr, xi, mr, mi, rows & (SCAN_ROWS - 1), SCAN_ROWS, 3, reverse)
    lane_blocks = range(cols // 128)
    for c in lane_blocks:
        yr_ref[c] = yr[:, c * 128:(c + 1) * 128]
        yi_ref[c] = yi[:, c * 128:(c + 1) * 128]
    wide = lambda ref, rows_: jnp.concatenate([ref[c, rows_, :] for c in lane_blocks], axis=1)
    end = pl.ds(0 if reverse else SCAN_ROWS - 1, SCAN_GROUPS, stride=SCAN_ROWS)
    groups = lax.broadcasted_iota(jnp.int32, (SCAN_GROUPS, cols), 0)
    er, ei, _, _ = _log_scan(wide(yr_ref, end), wide(yi_ref, end), m8r, m8i, groups, SCAN_GROUPS,
                             int(math.log2(SCAN_GROUPS)), reverse)
    er_ref[...] = er
    ei_ref[...] = ei
    j = lax.broadcasted_iota(jnp.int32, (SCAN_ROWS, cols), 0)
    dist = SCAN_ROWS - j if reverse else j + 1
    tr, ti = jnp.ones((SCAN_ROWS, cols), F32), jnp.zeros((SCAN_ROWS, cols), F32)
    br, bi = mr, mi
    for bit in range(4):
        nr, ni = _cmul(tr, ti, br, bi)
        take = ((dist >> bit) & 1) == 1
        tr, ti = jnp.where(take, nr, tr), jnp.where(take, ni, ti)
        br, bi = _cmul(br, bi, br, bi)
    for g in range(SCAN_GROUPS):
        before = g + 1 if reverse else g - 1
        rows_g = slice(g * SCAN_ROWS, (g + 1) * SCAN_ROWS)
        if 0 <= before < SCAN_GROUPS:
            ar, ai = _cmul(tr, ti, er_ref[before:before + 1, :], ei_ref[before:before + 1, :])
            hr_ref[rows_g, :] = wide(yr_ref, rows_g) + ar
            hi_ref[rows_g, :] = wide(yi_ref, rows_g) + ai
        else:
            hr_ref[rows_g, :] = wide(yr_ref, rows_g)
            hi_ref[rows_g, :] = wide(yi_ref, rows_g)
    last = 0 if reverse else SCAN_GROUPS - 1
    return er_ref[last:last + 1, :], ei_ref[last:last + 1, :]


def s5_fwd(proj, b_re, b_im, a_re, a_im, c_re, c_im, d_skip):
    S = proj.shape[0]
    nt = S // SSM_T

    def body(u_ref, bre_ref, bim_ref, ar_ref, ai_ref, cre_ref, cim_ref, dsk_ref, hr_ref, hi_ref, y_ref, cr_ref, ci_ref,
             yr_ref, yi_ref, er_ref, ei_ref):
        t = pl.program_id(1)

        @pl.when(t == 0)
        def _():
            cr_ref[...] = jnp.zeros(cr_ref.shape, F32)
            ci_ref[...] = jnp.zeros(ci_ref.shape, F32)

        u = u_ref[...]
        ar, ai = ar_ref[...], ai_ref[...]
        rows = lax.broadcasted_iota(jnp.int32, (SSM_T, W_BRANCH), 0)
        inr, ini = _cmul(ar, ai, cr_ref[0:1, :], ci_ref[0:1, :])
        xr = _dot(u, bre_ref[...]) + jnp.where(rows == 0, inr, 0.0)
        xi = _dot(u, bim_ref[...]) + jnp.where(rows == 0, ini, 0.0)
        endr, endi = _block_scan(xr, xi, ar, ai, False, yr_ref, yi_ref, er_ref, ei_ref, hr_ref, hi_ref)
        cr_ref[...] = jnp.broadcast_to(endr, cr_ref.shape)
        ci_ref[...] = jnp.broadcast_to(endi, ci_ref.shape)
        xr, xi = hr_ref[...], hi_ref[...]
        y_ref[...] = _dot(xr, cre_ref[...]) - _dot(xi, cim_ref[...]) + u * dsk_ref[...]

    u_spec = pl.BlockSpec((SSM_T, 128), lambda j, t: (t, OFF_SSM // 128 + j))
    b_spec = pl.BlockSpec((None, 128, W_BRANCH), lambda j, t: (j, 0, 0))
    a_spec = pl.BlockSpec((1, W_BRANCH), lambda j, t: (0, j))
    c_spec = pl.BlockSpec((None, W_BRANCH, 128), lambda j, t: (j, 0, 0))
    h_spec = pl.BlockSpec((SSM_T, W_BRANCH), lambda j, t: (t, j))
    return pl.pallas_call(
        body, name="s5_fwd", grid=(4, nt),
        in_specs=[u_spec, b_spec, b_spec, a_spec, a_spec, c_spec, c_spec, pl.BlockSpec((1, 128), lambda j, t: (0, j))],
        out_specs=[h_spec, h_spec, pl.BlockSpec((SSM_T, 128), lambda j, t: (t, j))],
        out_shape=[jax.ShapeDtypeStruct((S, SSM_COLS), F32), jax.ShapeDtypeStruct((S, SSM_COLS), F32),
                   jax.ShapeDtypeStruct((S, W_BRANCH), F32)],
        scratch_shapes=[pltpu.VMEM((8, W_BRANCH), F32)] * 2 + _scan_scratch(0),
        compiler_params=_cp(2),
    )(proj, b_re, b_im, a_re, a_im, c_re, c_im, d_skip)


def s5_bwd(proj, hr, hi, dy, b_re, b_im, a_re, a_im, c_re, c_im, d_skip):
    S = proj.shape[0]
    nt = S // SSM_T

    def body(u_ref, hr_ref, hi_ref, hpr_ref, hpi_ref, dy_ref, bre_ref, bim_ref, ar_ref, ai_ref, cre_ref, cim_ref,
             dsk_ref, du_ref, dbre_ref, dbim_ref, dar_ref, dai_ref, dcre_ref, dcim_ref, ddsk_ref, gr_ref, gi_ref,
             yr_ref, yi_ref, er_ref, ei_ref, sr_ref, si_ref):
        step = pl.program_id(1)
        t = nt - 1 - step

        @pl.when(step == 0)
        def _():
            gr_ref[...] = jnp.zeros(gr_ref.shape, F32)
            gi_ref[...] = jnp.zeros(gi_ref.shape, F32)
            for ref in (dbre_ref, dbim_ref, dar_ref, dai_ref, dcre_ref, dcim_ref, ddsk_ref):
                ref[...] = jnp.zeros(ref.shape, F32)

        u = u_ref[...]
        dy = dy_ref[...]
        ar, ai = ar_ref[...], ai_ref[...]
        rows = lax.broadcasted_iota(jnp.int32, (SSM_T, W_BRANCH), 0)
        inr, ini = _cmul(ar, -ai, gr_ref[0:1, :], gi_ref[0:1, :])
        xr = _dot(dy, cre_ref[...], "nt") + jnp.where(rows == SSM_T - 1, inr, 0.0)
        xi = -_dot(dy, cim_ref[...], "nt") + jnp.where(rows == SSM_T - 1, ini, 0.0)
        endr, endi = _block_scan(xr, xi, ar, -ai, True, yr_ref, yi_ref, er_ref, ei_ref, sr_ref, si_ref)
        gr_ref[...] = jnp.broadcast_to(endr, gr_ref.shape)
        gi_ref[...] = jnp.broadcast_to(endi, gi_ref.shape)
        xr, xi = sr_ref[...], si_ref[...]
        hr_blk, hi_blk = hr_ref[...], hi_ref[...]
        keep = (t > 0).astype(F32)
        hpr = jnp.where(rows >= 1, pltpu.roll(hr_blk, 1, 0), hpr_ref[7:8, :] * keep)
        hpi = jnp.where(rows >= 1, pltpu.roll(hi_blk, 1, 0), hpi_ref[7:8, :] * keep)
        dar_ref[...] += jnp.sum(hpr * xr + hpi * xi, axis=0, keepdims=True)
        dai_ref[...] += jnp.sum(hpr * xi - hpi * xr, axis=0, keepdims=True)
        dcre_ref[...] += _dot(hr_blk, dy, "tn")
        dcim_ref[...] -= _dot(hi_blk, dy, "tn")
        du = dy * dsk_ref[...] + _dot(xr, bre_ref[...], "nt") + _dot(xi, bim_ref[...], "nt")
        du_ref[...] = du.astype(du_ref.dtype)
        dbre_ref[...] += _dot(u, xr, "tn")
        dbim_ref[...] += _dot(u, xi, "tn")
        ddsk_ref[...] += jnp.sum(dy * u, axis=0, keepdims=True)

    def rev(t):
        return nt - 1 - t

    u_spec = pl.BlockSpec((SSM_T, 128), lambda j, t: (rev(t), OFF_SSM // 128 + j))
    h_spec = pl.BlockSpec((SSM_T, W_BRANCH), lambda j, t: (rev(t), j))
    hprev_spec = pl.BlockSpec((8, W_BRANCH), lambda j, t: (jnp.maximum(rev(t) * (SSM_T // 8) - 1, 0), j))
    ch_spec = pl.BlockSpec((SSM_T, 128), lambda j, t: (rev(t), j))
    b_spec = pl.BlockSpec((None, 128, W_BRANCH), lambda j, t: (j, 0, 0))
    a_spec = pl.BlockSpec((1, W_BRANCH), lambda j, t: (0, j))
    c_spec = pl.BlockSpec((None, W_BRANCH, 128), lambda j, t: (j, 0, 0))
    d_spec = pl.BlockSpec((1, 128), lambda j, t: (0, j))
    return pl.pallas_call(
        body, name="s5_bwd", grid=(4, nt),
        in_specs=[u_spec, h_spec, h_spec, hprev_spec, hprev_spec, ch_spec, b_spec, b_spec, a_spec, a_spec,
                  c_spec, c_spec, d_spec],
        out_specs=[ch_spec, b_spec, b_spec, a_spec, a_spec, c_spec, c_spec, d_spec],
        out_shape=[jax.ShapeDtypeStruct((S, W_BRANCH), BF16),
                   jax.ShapeDtypeStruct((4, 128, W_BRANCH), F32), jax.ShapeDtypeStruct((4, 128, W_BRANCH), F32),
                   jax.ShapeDtypeStruct((1, SSM_COLS), F32), jax.ShapeDtypeStruct((1, SSM_COLS), F32),
                   jax.ShapeDtypeStruct((4, W_BRANCH, 128), F32), jax.ShapeDtypeStruct((4, W_BRANCH, 128), F32),
                   jax.ShapeDtypeStruct((1, W_BRANCH), F32)],
        scratch_shapes=[pltpu.VMEM((8, W_BRANCH), F32)] * 2 + _scan_scratch(2),
        compiler_params=_cp(2),
    )(proj, hr, hi, hr, hi, dy, b_re, b_im, a_re, a_im, c_re, c_im, d_skip)


def glu_fwd(y, w_glu, b_glu, after=()):
    S = y.shape[0]

    def body(y_ref, w_ref, b_ref, *rest):
        o_ref = rest[-1]
        g = _gelu(y_ref[...])
        o_ref[...] = (g * _sigmoid(_dot(g, w_ref[...]) + b_ref[...])).astype(o_ref.dtype)

    blk = pl.BlockSpec((ROW_TILE, W_BRANCH), lambda i: (i, 0))
    return pl.pallas_call(
        body, name="glu_fwd", grid=(S // ROW_TILE,),
        in_specs=[blk, pl.BlockSpec((W_BRANCH, W_BRANCH), lambda i: (0, 0)), pl.BlockSpec((1, W_BRANCH), lambda i: (0, 0))]
        + [ANY] * len(after),
        out_specs=blk, out_shape=jax.ShapeDtypeStruct((S, W_BRANCH), BF16), compiler_params=_cp(1),
    )(y, w_glu, b_glu, *after)


def glu_bwd(y, w_glu, b_glu, dout):
    S = y.shape[0]

    def body(y_ref, w_ref, b_ref, do_ref, dy_ref, dw_ref, db_ref):
        yv = y_ref[...]
        do = do_ref[...]
        g = _gelu(yv)
        s = _sigmoid(_dot(g, w_ref[...]) + b_ref[...])
        dz = do * g * s * (1.0 - s)
        dg = do * s + _dot(dz, w_ref[...], "nt")
        dy_ref[...] = dg * _gelu_grad(yv)
        dw = _dot(g, dz, "tn")
        db = jnp.sum(dz, axis=0, keepdims=True)

        @pl.when(pl.program_id(0) == 0)
        def _():
            dw_ref[...] = dw
            db_ref[...] = db

        @pl.when(pl.program_id(0) > 0)
        def _():
            dw_ref[...] += dw
            db_ref[...] += db

    blk = pl.BlockSpec((ROW_TILE, W_BRANCH), lambda i: (i, 0))
    mat = pl.BlockSpec((W_BRANCH, W_BRANCH), lambda i: (0, 0))
    vec = pl.BlockSpec((1, W_BRANCH), lambda i: (0, 0))
    return pl.pallas_call(
        body, name="glu_bwd", grid=(S // ROW_TILE,), in_specs=[blk, mat, vec, blk], out_specs=[blk, mat, vec],
        out_shape=[jax.ShapeDtypeStruct((S, W_BRANCH), F32), jax.ShapeDtypeStruct((W_BRANCH, W_BRANCH), F32),
                   jax.ShapeDtypeStruct((1, W_BRANCH), F32)],
        compiler_params=_cp(1),
    )(y, w_glu, b_glu, dout)


SGU_TILE = 512
SGU_U_BLOCK = OFF_SGU // W_BRANCH
SGU_V_BLOCK = SGU_U_BLOCK + 1


def _sgu_norm(zv):
    v = _gelu(zv)
    mu = jnp.mean(v, axis=-1, keepdims=True)
    vc = v - mu
    rstd = lax.rsqrt(jnp.mean(vc * vc, axis=-1, keepdims=True) + EPS)
    return vc * rstd, rstd


def _tril():
    return lax.broadcasted_iota(jnp.int32, (SGU_CHUNK, SGU_CHUNK), 0) >= lax.broadcasted_iota(jnp.int32, (SGU_CHUNK, SGU_CHUNK), 1)


def sgu_fwd(proj, ln_g, ln_b, w_s, b_s_t):
    S = proj.shape[0]

    def body(zu_ref, zv_ref, g_ref, b_ref, ws_ref, bs_ref, o_ref, vf_ref):
        vn, _ = _sgu_norm(zv_ref[...])
        vf_ref[...] = vn * g_ref[...] + b_ref[...]
        tri = _tril()
        for gi in range(4):
            ws = jnp.where(tri, ws_ref[gi], 0.0)
            cols = slice(gi * 128, (gi + 1) * 128)
            for c in range(SGU_TILE // SGU_CHUNK):
                rows = slice(c * SGU_CHUNK, (c + 1) * SGU_CHUNK)
                sv = _dot(ws, vf_ref[rows, cols]) + bs_ref[:, gi:gi + 1]
                o_ref[rows, cols] = (_gelu(zu_ref[rows, cols]) * sv).astype(o_ref.dtype)

    blk = lambda cb: pl.BlockSpec((SGU_TILE, W_BRANCH), lambda i: (i, cb))
    vec = pl.BlockSpec((1, W_BRANCH), lambda i: (0, 0))
    return pl.pallas_call(
        body, name="sgu_fwd", grid=(S // SGU_TILE,),
        in_specs=[blk(SGU_U_BLOCK), blk(SGU_V_BLOCK), vec, vec, pl.BlockSpec((4, SGU_CHUNK, SGU_CHUNK), lambda i: (0, 0, 0)),
                  pl.BlockSpec((SGU_CHUNK, 4), lambda i: (0, 0))],
        out_specs=blk(0), out_shape=jax.ShapeDtypeStruct((S, W_BRANCH), BF16),
        scratch_shapes=[pltpu.VMEM((SGU_TILE, W_BRANCH), F32)], compiler_params=_cp(1),
    )(proj, proj, ln_g, ln_b, w_s, b_s_t)


def sgu_bwd(proj, ln_g, ln_b, w_s, b_s_t, dout):
    S = proj.shape[0]

    def body(zu_ref, zv_ref, g_ref, b_ref, ws_ref, bs_ref, do_ref, dzu_ref, dzv_ref, dg_ref, db_ref, dws_ref, dbs_ref,
             vf_ref, dvf_ref):
        @pl.when(pl.program_id(0) == 0)
        def _():
            for ref in (dg_ref, db_ref, dws_ref, dbs_ref):
                ref[...] = jnp.zeros(ref.shape, F32)

        vn, rstd = _sgu_norm(zv_ref[...])
        vf_ref[...] = vn * g_ref[...] + b_ref[...]
        tri = _tril()
        lane = lax.broadcasted_iota(jnp.int32, (SGU_CHUNK, 128), 1)
        dbs = jnp.zeros((SGU_CHUNK, 128), F32)
        for gi in range(4):
            ws = jnp.where(tri, ws_ref[gi], 0.0)
            cols = slice(gi * 128, (gi + 1) * 128)
            dws = jnp.zeros((SGU_CHUNK, SGU_CHUNK), F32)
            for c in range(SGU_TILE // SGU_CHUNK):
                rows = slice(c * SGU_CHUNK, (c + 1) * SGU_CHUNK)
                vf = vf_ref[rows, cols]
                zu = zu_ref[rows, cols]
                do = do_ref[rows, cols]
                sv = _dot(ws, vf) + bs_ref[:, gi:gi + 1]
                dzu_ref[rows, cols] = (do * sv * _gelu_grad(zu)).astype(dzu_ref.dtype)
                dsv = do * _gelu(zu)
                dvf_ref[rows, cols] = _dot(ws, dsv, "tn")
                dws = dws + _dot(dsv, vf, "nt")
                dbs = dbs + jnp.where(lane == gi, jnp.sum(dsv, axis=-1, keepdims=True), 0.0)
            dws_ref[gi] += jnp.where(tri, dws, 0.0)
        dbs_ref[...] += dbs
        dvf = dvf_ref[...]
        dg_ref[...] += jnp.sum(dvf * vn, axis=0, keepdims=True)
        db_ref[...] += jnp.sum(dvf, axis=0, keepdims=True)
        dvn = dvf * g_ref[...]
        dv = rstd * (dvn - jnp.mean(dvn, axis=-1, keepdims=True) - vn * jnp.mean(dvn * vn, axis=-1, keepdims=True))
        dzv_ref[...] = (dv * _gelu_grad(zv_ref[...])).astype(dzv_ref.dtype)

    blk = lambda cb: pl.BlockSpec((SGU_TILE, W_BRANCH), lambda i: (i, cb))
    vec = pl.BlockSpec((1, W_BRANCH), lambda i: (0, 0))
    ws_spec = pl.BlockSpec((4, SGU_CHUNK, SGU_CHUNK), lambda i: (0, 0, 0))
    return pl.pallas_call(
        body, name="sgu_bwd", grid=(S // SGU_TILE,),
        in_specs=[blk(SGU_U_BLOCK), blk(SGU_V_BLOCK), vec, vec, ws_spec, pl.BlockSpec((SGU_CHUNK, 4), lambda i: (0, 0)),
                  blk(0)],
        out_specs=[blk(0), blk(0), vec, vec, ws_spec, pl.BlockSpec((SGU_CHUNK, 128), lambda i: (0, 0))],
        out_shape=[jax.ShapeDtypeStruct((S, W_BRANCH), BF16), jax.ShapeDtypeStruct((S, W_BRANCH), BF16),
                   jax.ShapeDtypeStruct((1, W_BRANCH), F32), jax.ShapeDtypeStruct((1, W_BRANCH), F32),
                   jax.ShapeDtypeStruct((4, SGU_CHUNK, SGU_CHUNK), F32), jax.ShapeDtypeStruct((SGU_CHUNK, 128), F32)],
        scratch_shapes=[pltpu.VMEM((SGU_TILE, W_BRANCH), F32), pltpu.VMEM((SGU_TILE, W_BRANCH), F32)],
        compiler_params=_cp(1),
    )(proj, proj, ln_g, ln_b, w_s, b_s_t, dout)


GM_TILE = 512


def _gate_specs(order):
    def spec(i):
        def index(*ids):
            m, n = order(*ids)
            return (m, (OFF_GATE + i * D_MODEL) // GM_TILE + n)
        return pl.BlockSpec((GM_TILE, GM_TILE), index)
    return [spec(i) for i in range(4)]


def merge_fwd(proj, gate_b, branches, w_up):
    S = proj.shape[0]
    order = lambda n, m: (m, n)

    def body(p0, p1, p2, p3, gb_ref, b0, b1, b2, b3, w_ref, o_ref):
        acc = jnp.zeros((GM_TILE, GM_TILE), F32)
        for i, (p_ref, br_ref) in enumerate(zip((p0, p1, p2, p3), (b0, b1, b2, b3))):
            acc = acc + _sigmoid(p_ref[...] + gb_ref[i:i + 1, :]) * _dot(br_ref[...], w_ref[i])
        o_ref[...] = acc.astype(o_ref.dtype)

    br_spec = pl.BlockSpec((GM_TILE, W_BRANCH), lambda n, m: (m, 0))
    return pl.pallas_call(
        body, name="merge_fwd", grid=(D_MODEL // GM_TILE, S // GM_TILE),
        in_specs=_gate_specs(order) + [pl.BlockSpec((4, GM_TILE), lambda n, m: (0, n))] + [br_spec] * 4
        + [pl.BlockSpec((4, W_BRANCH, GM_TILE), lambda n, m: (0, 0, n))],
        out_specs=pl.BlockSpec((GM_TILE, GM_TILE), lambda n, m: (m, n)),
        out_shape=jax.ShapeDtypeStruct((S, D_MODEL), BF16), compiler_params=_cp(2),
    )(proj, proj, proj, proj, gate_b, *branches, w_up)


def merge_bwd(proj, gate_b, branches, w_up, dmerged):
    S = proj.shape[0]
    order = lambda n, m: (m, n)

    def body(p0, p1, p2, p3, gb_ref, b0, b1, b2, b3, w_ref, dm_ref, dp0, dp1, dp2, dp3, du0, du1, du2, du3, dgb_ref):
        dm = dm_ref[...]
        dgb = []
        for i, (p_ref, br_ref, dp_ref, du_ref) in enumerate(
                zip((p0, p1, p2, p3), (b0, b1, b2, b3), (dp0, dp1, dp2, dp3), (du0, du1, du2, du3))):
            gate = _sigmoid(p_ref[...] + gb_ref[i:i + 1, :])
            dpre = dm * _dot(br_ref[...], w_ref[i]) * gate * (1.0 - gate)
            dp_ref[...] = dpre.astype(dp_ref.dtype)
            du_ref[...] = (dm * gate).astype(du_ref.dtype)
            dgb.append(jnp.sum(dpre, axis=0, keepdims=True))
        dgb = jnp.concatenate(dgb, axis=0)

        @pl.when(pl.program_id(1) == 0)
        def _():
            dgb_ref[...] = dgb

        @pl.when(pl.program_id(1) > 0)
        def _():
            dgb_ref[...] += dgb

    br_spec = pl.BlockSpec((GM_TILE, W_BRANCH), lambda n, m: (m, 0))
    mn = pl.BlockSpec((GM_TILE, GM_TILE), lambda n, m: (m, n))
    gb = pl.BlockSpec((4, GM_TILE), lambda n, m: (0, n))
    big = jax.ShapeDtypeStruct((S, D_MODEL), BF16)
    outs = pl.pallas_call(
        body, name="merge_bwd", grid=(D_MODEL // GM_TILE, S // GM_TILE),
        in_specs=_gate_specs(order) + [gb] + [br_spec] * 4
        + [pl.BlockSpec((4, W_BRANCH, GM_TILE), lambda n, m: (0, 0, n)), mn],
        out_specs=[mn] * 8 + [gb], out_shape=[big] * 8 + [jax.ShapeDtypeStruct((4, D_MODEL), F32)],
        compiler_params=_cp(2),
    )(proj, proj, proj, proj, gate_b, *branches, w_up, dmerged)
    return outs[0:4], outs[4:8], outs[8]


def _xatt_probs(q, k):
    s = _dot(q, k, "nt") * (X_HEAD_DIM ** -0.5)
    p = jnp.exp(s - jnp.max(s, axis=-1, keepdims=True))
    return p / jnp.sum(p, axis=-1, keepdims=True)


def xatt_fwd(q, kv):
    S = q.shape[0]

    def body(q_ref, kv_ref, o_ref):
        for h in range(X_HEADS):
            cols = slice(h * X_HEAD_DIM, (h + 1) * X_HEAD_DIM)
            p = _xatt_probs(q_ref[:, cols], kv_ref[:, cols])
            o_ref[:, cols] = _dot(p, kv_ref[:, W_BRANCH + h * X_HEAD_DIM:W_BRANCH + (h + 1) * X_HEAD_DIM]).astype(o_ref.dtype)

    blk = pl.BlockSpec((ROW_TILE, W_BRANCH), lambda i: (i, 0))
    return pl.pallas_call(
        body, name="xatt_fwd", grid=(S // ROW_TILE,),
        in_specs=[blk, pl.BlockSpec((N_MEM, 2 * W_BRANCH), lambda i: (0, 0))], out_specs=blk,
        out_shape=jax.ShapeDtypeStruct((S, W_BRANCH), BF16), compiler_params=_cp(1),
    )(q, kv)


def xatt_bwd(q, kv, do):
    S = q.shape[0]

    def body(q_ref, kv_ref, do_ref, dq_ref, dkv_ref):
        @pl.when(pl.program_id(0) == 0)
        def _():
            dkv_ref[...] = jnp.zeros(dkv_ref.shape, F32)

        for h in range(X_HEADS):
            cols = slice(h * X_HEAD_DIM, (h + 1) * X_HEAD_DIM)
            vcols = slice(W_BRANCH + h * X_HEAD_DIM, W_BRANCH + (h + 1) * X_HEAD_DIM)
            qh, kh, doh = q_ref[:, cols], kv_ref[:, cols], do_ref[:, cols]
            p = _xatt_probs(qh, kh)
            dp = _dot(doh, kv_ref[:, vcols], "nt")
            ds = p * (dp - jnp.sum(dp * p, axis=-1, keepdims=True)) * (X_HEAD_DIM ** -0.5)
            dq_ref[:, cols] = _dot(ds, kh).astype(dq_ref.dtype)
            dkv_ref[:, cols] += _dot(ds, qh, "tn")
            dkv_ref[:, vcols] += _dot(p, doh, "tn")

    blk = pl.BlockSpec((ROW_TILE, W_BRANCH), lambda i: (i, 0))
    kv_spec = pl.BlockSpec((N_MEM, 2 * W_BRANCH), lambda i: (0, 0))
    return pl.pallas_call(
        body, name="xatt_bwd", grid=(S // ROW_TILE,), in_specs=[blk, kv_spec, blk], out_specs=[blk, kv_spec],
        out_shape=[jax.ShapeDtypeStruct((S, W_BRANCH), BF16), jax.ShapeDtypeStruct((N_MEM, 2 * W_BRANCH), F32)],
        compiler_params=_cp(1),
    )(q, kv, do)


def s5_params(a_re, a_im, log_dt, b_re, b_im, c_re, c_im):
    lam_re = jnp.minimum(a_re, -1e-4)
    lam_im = a_im
    dt = jnp.exp(log_dt)[:, None]
    mag = jnp.exp(lam_re * dt)
    ab_re, ab_im = mag * jnp.cos(lam_im * dt), mag * jnp.sin(lam_im * dt)
    den = lam_re * lam_re + lam_im * lam_im
    f_re = ((ab_re - 1.0) * lam_re + ab_im * lam_im) / den
    f_im = (ab_im * lam_re - (ab_re - 1.0) * lam_im) / den
    bb_re = f_re[..., None] * b_re - f_im[..., None] * b_im
    bb_im = f_re[..., None] * b_im + f_im[..., None] * b_re
    eye = jnp.eye(8, dtype=F32)

    def b_blocks(bb):
        t = bb.reshape(4, 8, SSM_STATE, SSM_GROUP).transpose(0, 1, 3, 2)
        return (t[:, :, :, None, :] * eye[None, :, None, :, None]).reshape(4, 128, W_BRANCH)

    def c_blocks(cc):
        t = cc.reshape(4, 8, SSM_GROUP, SSM_STATE).transpose(0, 1, 3, 2)
        return (t[:, :, :, None, :] * eye[None, :, None, :, None]).reshape(4, W_BRANCH, 128)

    return (ab_re.reshape(1, SSM_COLS), ab_im.reshape(1, SSM_COLS), b_blocks(bb_re), b_blocks(bb_im),
            c_blocks(c_re), c_blocks(c_im))


ANY = pl.BlockSpec(memory_space=pl.ANY)


def _chip_index():
    return 2 * lax.axis_index("x") + lax.axis_index("y")


def _peer_chip(j):
    x, y, c = lax.axis_index("x"), lax.axis_index("y"), lax.axis_index("c")
    return ((1 - x) if j & 2 else x, (1 - y) if j & 1 else y, c)


def _piece(ref, axis, s, n):
    size = ref.shape[axis] // n
    idx = [slice(None)] * len(ref.shape)
    idx[axis] = pl.ds(s * size, size)
    return ref.at[tuple(idx)]


HBM_SPEC = pl.BlockSpec(memory_space=pltpu.HBM)
SEM_SPEC = pl.BlockSpec(memory_space=pltpu.SEMAPHORE)
SIDE_EFFECT = pltpu.SideEffectType.DATAFLOW_SIDE_EFFECTING


HALVING_MIN_ROWS = 32


def _rows_half(ref, c):
    rows = ref.shape[0] // 2
    return ref.at[pl.ds(c * rows, rows), :]


def _halved(ref):
    return ref.shape[0] >= HALVING_MIN_ROWS


def _chip_copies(ins, lands, send, recv, axes, mode, k, c, arriving):
    copies = []
    for t in range(len(ins)):
        for j in (1, 2, 3):
            sems = dict(send_sem=send.at[3 * t + j - 1], recv_sem=recv.at[3 * t + j - 1], device_id_type=MESH_ID)
            if mode == "scatter":
                src = ins[t] if axes[t] is None else _piece(ins[t], axes[t], k ^ j, 4)
                dst = lands[t].at[k ^ j if arriving else k]
                device = _peer_chip(j)
            elif mode == "gather":
                src, dst = ins[t], _piece(lands[t], axes[t], k ^ j if arriving else k, 4)
                if _halved(ins[t]):
                    src, dst = _rows_half(src, c), _rows_half(dst, c)
                device = _peer_chip(j)
            else:
                if not _halved(ins[t]):
                    continue
                theirs = _piece(lands[t], axes[t], k ^ j, 4)
                src, dst = _rows_half(theirs, c), _rows_half(theirs, 1 - c if arriving else c)
                device = (lax.axis_index("x"), lax.axis_index("y"), 1 - lax.axis_index("c"))
            copies.append(pltpu.make_async_remote_copy(src_ref=src, dst_ref=dst, device_id=device, **sems))
    return copies


def _own_copies(ins, lands, send, axes, mode, k):
    if mode != "gather":
        return []
    n = len(ins)
    return [pltpu.make_async_copy(ins[t], _piece(lands[t], axes[t], k, 4), send.at[3 * n + t]) for t in range(n)]


def chips_start(ins, lands, axes, mode, name, after=()):
    n, na = len(ins), len(after)

    def body(*refs):
        in_refs, land_refs = refs[:n], refs[n:2 * n]
        send, recv, token = refs[2 * n + na], refs[2 * n + na + 1], refs[-1]
        q, core = _chip_index(), lax.axis_index("c")
        for k in range(4):
            for c in range(2):
                @pl.when(jnp.logical_and(q == k, core == c))
                def _():
                    for copy in _chip_copies(in_refs, land_refs, send, recv, axes, mode, k, c, arriving=False):
                        copy.start()
                    for copy in _own_copies(in_refs, land_refs, send, axes, mode, k):
                        copy.start()
        token[...] = jnp.zeros(token.shape, token.dtype)

    hbm = lambda a: pltpu.HBM(a.shape, a.dtype)
    outs = pl.pallas_call(
        body, name=name, in_specs=[HBM_SPEC] * (2 * n) + [ANY] * na,
        out_specs=[SEM_SPEC, SEM_SPEC] + [HBM_SPEC] * (2 * n) + [pl.BlockSpec(memory_space=pltpu.VMEM)],
        out_shape=[pltpu.SemaphoreType.DMA((4 * n,)), pltpu.SemaphoreType.DMA((3 * n,))]
        + [hbm(a) for a in ins] + [hbm(a) for a in lands] + [jax.ShapeDtypeStruct((8, 128), F32)],
        input_output_aliases={i: 2 + i for i in range(2 * n)},
        compiler_params=pltpu.CompilerParams(has_side_effects=SIDE_EFFECT),
    )(*[pltpu.with_memory_space_constraint(a, pltpu.HBM) for a in list(ins) + list(lands)], *after)
    return outs[0], outs[1], outs[2:2 + n], outs[2 + n:2 + 2 * n], outs[-1]


def chips_wait(send, recv, ins, lands, axes, mode, name, after=()):
    n = len(ins)

    def body(*refs):
        in_refs, land_refs = refs[:n], refs[n:2 * n]
        send_ref, recv_ref = refs[2 * n], refs[2 * n + 1]
        q, core = _chip_index(), lax.axis_index("c")
        for k in range(4):
            for c in range(2):
                @pl.when(jnp.logical_and(q == k, core == c))
                def _():
                    for copy in _chip_copies(in_refs, land_refs, send_ref, recv_ref, axes, mode, k, c, arriving=True):
                        copy.wait_send()
                        copy.wait_recv()
                    for copy in _own_copies(in_refs, land_refs, send_ref, axes, mode, k):
                        copy.wait()

    hbm = lambda a: pltpu.HBM(a.shape, a.dtype)
    outs = pl.pallas_call(
        body, name=name, in_specs=[HBM_SPEC] * (2 * n) + [SEM_SPEC, SEM_SPEC] + [ANY] * len(after),
        out_specs=[HBM_SPEC] * (2 * n), out_shape=[hbm(a) for a in ins] + [hbm(a) for a in lands],
        input_output_aliases={i: i for i in range(2 * n)},
        compiler_params=pltpu.CompilerParams(has_side_effects=SIDE_EFFECT),
    )(*ins, *lands, send, recv, *after)
    return outs[:n], outs[n:]


def swap_cores(arrs, name):
    n = len(arrs)

    def body(*refs):
        ins, outs = refs[:n], refs[n:2 * n]
        send, recv = refs[2 * n:]
        sibling = (lax.axis_index("x"), lax.axis_index("y"), 1 - lax.axis_index("c"))
        copies = [pltpu.make_async_remote_copy(src_ref=ins[t], dst_ref=outs[t], send_sem=send.at[t], recv_sem=recv.at[t],
                                               device_id=sibling, device_id_type=MESH_ID) for t in range(n)]
        for cp in copies:
            cp.start()
        for cp in copies:
            cp.wait()

    return pl.pallas_call(
        body, name=name, in_specs=[ANY] * n, out_specs=[ANY] * n,
        out_shape=[jax.ShapeDtypeStruct(a.shape, a.dtype) for a in arrs],
        scratch_shapes=[pltpu.SemaphoreType.DMA((n,)), pltpu.SemaphoreType.DMA((n,))],
    )(*arrs)


ELEMENTWISE_BLOCK_BYTES = 1 << 21


def _row_tile(rows, cols):
    want = max(8, ELEMENTWISE_BLOCK_BYTES // (4 * 128 * -(-cols // 128)))
    fits = [t for t in range(8, min(rows, want) + 1, 8) if rows % t == 0]
    return fits[-1] if fits else rows


def sum_chips(recv, own, axis, chip, stacked, l, name):
    _, r, c = recv.shape
    tr = _row_tile(r, c)
    nrt = r // tr

    def body(chip_ref, r_ref, own_ref, stacked_ref, o_ref):
        for k in range(4):
            @pl.when(chip_ref[0] == k)
            def _():
                terms = [own_ref[...] if s == k else r_ref[s] for s in range(4)]
                o_ref[...] = ((terms[0] + terms[1]) + terms[2]) + terms[3]

    own_index = {0: lambda i, q: (q[0] * nrt + i, 0), 1: lambda i, q: (i, q[0]), None: lambda i, q: (i, 0)}[axis]
    return pl.pallas_call(
        body, name=name,
        grid_spec=pltpu.PrefetchScalarGridSpec(
            num_scalar_prefetch=1, grid=(nrt,),
            in_specs=[pl.BlockSpec((4, tr, c), lambda i, q: (0, i, 0)), pl.BlockSpec((tr, c), own_index), ANY],
            out_specs=pl.BlockSpec((None, tr, c), lambda i, q: (l, i, 0))),
        out_shape=jax.ShapeDtypeStruct(stacked.shape, F32), input_output_aliases={3: 0}, compiler_params=_cp(1),
    )(chip, recv, own, stacked)


def adamw(w, ga, gb, m, v, name):
    rows, cols = w.shape
    tr = _row_tile(rows, cols)

    def body(w_ref, ga_ref, gb_ref, m_ref, v_ref, g_ref, d_ref, nm_ref, nv_ref):
        g = ga_ref[...] + gb_ref[...]
        nm = ADAM_B1 * m_ref[...] + (1.0 - ADAM_B1) * g
        nv = ADAM_B2 * v_ref[...] + (1.0 - ADAM_B2) * (g * g)
        m_hat = nm / (1.0 - ADAM_B1 ** ADAM_STEP)
        v_hat = nv / (1.0 - ADAM_B2 ** ADAM_STEP)
        g_ref[...] = g
        nm_ref[...] = nm
        nv_ref[...] = nv
        d_ref[...] = -ADAM_LR * (m_hat / (jnp.sqrt(v_hat) + ADAM_EPS) + ADAM_WD * w_ref[...])

    blk = pl.BlockSpec((tr, cols), lambda i: (i, 0))
    f = jax.ShapeDtypeStruct((rows, cols), F32)
    return pl.pallas_call(
        body, name=name, grid=(rows // tr,), in_specs=[blk] * 5, out_specs=[blk] * 4, out_shape=[f] * 4,
        compiler_params=_cp(1),
    )(w, ga, gb, m, v)


PACK_ALIGN = 1024
PACK_ROWS_ALIGN = 2048


def pack_small(arrs):
    parts = []
    for a in arrs:
        flat = a.reshape(-1)
        pad = (-flat.shape[0]) % PACK_ALIGN
        parts.append(jnp.pad(flat, (0, pad)) if pad else flat)
    rows = sum(p.shape[0] for p in parts) // 128
    parts.append(jnp.zeros(((-rows) % PACK_ROWS_ALIGN * 128,), arrs[0].dtype))
    return jnp.concatenate(parts).reshape(-1, 128)


def unpack_small(packed, shapes):
    out, row = [], 0
    for shape in shapes:
        size = int(np.prod(shape))
        rows = -(-size // PACK_ALIGN) * 8
        out.append(packed[row:row + rows].reshape(-1)[:size].reshape(shape))
        row += rows
    return out


def _norm_epilogue(with_next):
    def epi(acc, res, g_post, *g_pre):
        x_new = acc * lax.rsqrt(jnp.mean(acc * acc, axis=-1, keepdims=True) + EPS) * g_post + res
        if not with_next:
            return acc, x_new
        return acc, x_new, x_new * lax.rsqrt(jnp.mean(x_new * x_new, axis=-1, keepdims=True) + EPS) * g_pre[0]
    return epi


def layer_fwd(x, h1, mem, w_in, rest_of, P, biases, g_next, after=()):
    sv = {"x0": x}
    post = dict(tm=512, tn=D_MODEL)
    proj = mm(h1, w_in, "nn", out_dtypes=[F32], name="mm_w_in", after=after)
    a_out = pool_fwd(proj, P["pool_w"], P["pool_scale"])
    os_, lses = [], []
    for g, (win, dil) in enumerate(DIL_GROUPS):
        o, lse = att_fwd(proj, biases[g], g, dil)
        os_.append(o)
        lses.append(lse)
    b_out, w0, w1, w2 = att_combine(os_, lses)
    s5p = P["s5"]
    hr, hi, y = s5_fwd(proj, s5p[2], s5p[3], s5p[0], s5p[1], s5p[4], s5p[5], P["d_skip"])
    d_out = sgu_fwd(proj, P["sgu_ln_g"], P["sgu_ln_b"], P["w_s"], P["b_s_t"])
    W, after_rest = rest_of("mixer", (a_out, b_out, y, d_out))
    W = dict(W, w_in=w_in)
    c_out = glu_fwd(y, W["w_glu"], P["b_glu"], after=after_rest)
    branches = (a_out, b_out, c_out, d_out)
    merged = merge_fwd(proj, W["gate_b"], branches, W["w_up"])
    t1, x1, h2 = mm(merged, W["w_out"], "nn", tk=1024, out_dtypes=[F32, F32, BF16], name="mm_w_out", extras=(x,),
                    vecs=(P["g_mix_post"], P["g_x_pre"]), epi=_norm_epilogue(True), after=after_rest, **post)
    sv.update(h1=h1, proj=proj, os=os_, lses=lses, wts=(w0, w1, w2), hr=hr, hi=hi, y=y, branches=branches,
              merged=merged, t1=t1, x1=x1)

    mem_n = rms_fwd(mem, P["g_mem"], BF16, "rms_mem")
    q = mm(h2, W["w_cq"], "nn", tm=1024, tn=512, tk=1024, out_dtypes=[BF16], name="mm_w_cq")
    kv = mm(mem_n, W["w_ckv"], "nn", tm=256, tn=1024, tk=1024, out_dtypes=[BF16], name="mm_w_ckv")
    ox = xatt_fwd(q, kv)
    t2, x2, h3 = mm(ox, W["w_co"], "nn", tk=512, out_dtypes=[F32, F32, BF16], name="mm_w_co", extras=(x1,),
                    vecs=(P["g_x_post"], P["g_ff_pre"]), epi=_norm_epilogue(True), **post)
    sv.update(h2=h2, mem_n=mem_n, q=q, kv=kv, ox=ox, t2=t2, x2=x2)

    W_ff, after_ff = rest_of("mlp", h3)
    W = dict(W, **W_ff)
    pre, act = mm(h3, W["w_ff1"], "nn", out_dtypes=[F32, BF16], name="mm_w_ff1",
                  epi=lambda acc: (acc, jnp.square(jnp.maximum(acc, 0.0))), after=after_ff)
    _, after_out = rest_of("out", act)
    if g_next is None:
        (ff, x3), h_next = mm(act, W["w_ff2"], "nn", out_dtypes=[F32, F32], name="mm_w_ff2_last", extras=(x2,),
                              vecs=(P["g_ff_post"],), epi=_norm_epilogue(False), after=after_out), None
    else:
        ff, x3, h_next = mm(act, W["w_ff2"], "nn", out_dtypes=[F32, F32, BF16], name="mm_w_ff2", extras=(x2,),
                            vecs=(P["g_ff_post"], g_next), epi=_norm_epilogue(True), after=after_out)
    sv.update(h3=h3, pre=pre, act=act, ff=ff, W=W)
    return x3, h_next, sv


def _pre_norm_bwd_epilogue(dh, x, add, g):
    r = lax.rsqrt(jnp.mean(x * x, axis=-1, keepdims=True) + EPS)
    xn = x * r
    dxn = dh * g
    return r * (dxn - xn * jnp.mean(dxn * xn, axis=-1, keepdims=True)) + add, jnp.sum(dh * xn, axis=0, keepdims=True)


def layer_bwd(dx, mem, W, P, biases, sv, headsum, emit, after=()):
    G = {}
    dff, G["g_ff_post"] = rms_bwd(sv["ff"], P["g_ff_post"], dx, BF16, "rms_post_bwd", after=after)
    G["w_ff2"] = mm(sv["act"], dff, "tn", out_dtypes=[F32], name="mm_dw_ff2")
    dpre = mm(dff, W["w_ff2"], "nt", out_dtypes=[BF16], name="mm_dact", extras=(sv["pre"],),
              epi=lambda acc, pre: (acc * (2.0 * jnp.maximum(pre, 0.0)),))
    G["w_ff1"] = mm(sv["h3"], dpre, "tn", out_dtypes=[F32], name="mm_dw_ff1")
    sent = emit(("w_ff1", "w_ff2"), G)
    pre_bwd = dict(out_dtypes=[F32], epi=_pre_norm_bwd_epilogue, n_sums=1)
    dx2, G["g_ff_pre"] = mm(dpre, W["w_ff1"], "nt", name="mm_dh3", extras=(sv["x2"], dx), vecs=(P["g_ff_pre"],),
                            after=sent, **pre_bwd)
    dt2, G["g_x_post"] = rms_bwd(sv["t2"], P["g_x_post"], dx2, BF16, "rms_post_bwd")
    G["w_co"] = mm(sv["ox"], dt2, "tn", tm=512, tn=1024, tk=1024, out_dtypes=[F32], name="mm_dw_co")
    dox = mm(dt2, W["w_co"], "nt", tm=1024, tn=512, tk=1024, out_dtypes=[BF16], name="mm_dox")
    dq, dkv = xatt_bwd(sv["q"], sv["kv"], dox)
    G["w_cq"] = mm(sv["h2"], dq, "tn", tm=1024, tn=512, tk=1024, out_dtypes=[F32], name="mm_dw_cq")
    G["w_ckv"] = mm(sv["mem_n"], dkv, "tn", tm=1024, tn=1024, tk=256, out_dtypes=[F32], name="mm_dw_ckv")
    dmem_n = mm(dkv, W["w_ckv"], "nt", tm=256, tn=1024, tk=1024, out_dtypes=[F32], name="mm_dmem")
    _, G["g_mem"] = rms_bwd(mem, P["g_mem"], dmem_n, BF16, "rms_mem_bwd")
    dx1, G["g_x_pre"] = mm(dq, W["w_cq"], "nt", name="mm_dh2", extras=(sv["x1"], dx2), vecs=(P["g_x_pre"],),
                           **pre_bwd)
    proj = sv["proj"]
    dt1, G["g_mix_post"] = rms_bwd(sv["t1"], P["g_mix_post"], dx1, BF16, "rms_post_bwd")
    G["w_out"] = mm(sv["merged"], dt1, "tn", tm=1024, tn=1024, tk=1024, out_dtypes=[F32], name="mm_dw_out")
    dmerged = mm(dt1, W["w_out"], "nt", tm=1024, tn=1024, tk=1024, out_dtypes=[F32], name="mm_dmerged")
    dgates, dups, G["gate_b"] = merge_bwd(proj, W["gate_b"], sv["branches"], W["w_up"], dmerged)
    dbr, dwup = [], []
    for i in range(4):
        dbr.append(mm(dups[i], W["w_up"][i], "nt", tm=1024, tn=512, tk=1024, out_dtypes=[F32], name="mm_dbranch"))
        dwup.append(mm(sv["branches"][i], dups[i], "tn", tm=512, tn=1024, tk=1024, out_dtypes=[F32], name="mm_dw_up"))
    G["w_up"] = jnp.concatenate(dwup, axis=0)
    d_pool, G["pool_w"], G["pool_scale"] = pool_bwd(proj, P["pool_w"], P["pool_scale"], dbr[0])
    cbar = att_combine_bwd(dbr[1], sv["os"], sv["wts"], headsum)
    dqs, dks, dvs, dbias = [], [], [], []
    for g, (win, dil) in enumerate(DIL_GROUPS):
        dq_g, dk_g, dv_g, db_g = att_bwd(proj, biases[g], sv["lses"][g], sv["wts"][g], dbr[1], cbar, g, dil)
        dqs.append(dq_g)
        dks.append(dk_g)
        dvs.append(dv_g)
        dbias.append(db_g)
    G["att_bias"] = dbias
    s5p = P["s5"]
    dy, G["w_glu"], G["b_glu"] = glu_bwd(sv["y"], W["w_glu"], P["b_glu"], dbr[2])
    d_ssm, dbre, dbim, dar, dai, dcre, dcim, G["d_skip"] = s5_bwd(
        proj, sv["hr"], sv["hi"], dy, s5p[2], s5p[3], s5p[0], s5p[1], s5p[4], s5p[5], P["d_skip"])
    G["s5"] = (dar, dai, dbre, dbim, dcre, dcim)
    dzu, dzv, G["sgu_ln_g"], G["sgu_ln_b"], G["w_s"], G["b_s_t"] = sgu_bwd(
        proj, P["sgu_ln_g"], P["sgu_ln_b"], P["w_s"], P["b_s_t"], dbr[3])
    d_qkv = [d.astype(BF16) for d in dqs + dks + dvs]
    dproj = jnp.concatenate([d_pool] + d_qkv + [d_ssm, dzu, dzv] + list(dgates), axis=1)
    sent = emit(("gate_b", "w_glu", "w_up", "w_out", "w_cq", "w_ckv", "w_co"), G)
    G["w_in"] = mm(sv["h1"], dproj, "tn", out_dtypes=[F32], name="mm_dw_in", after=sent)
    sent = emit(("w_in",), G)
    dx0, G["g_mix_pre"] = mm(dproj, W["w_in"], "nt", name="mm_dh1", extras=(sv["x0"], dx1), vecs=(P["g_mix_pre"],),
                             after=sent, **pre_bwd)
    return dx0, G


def _as3d(name, a):
    shape2d, axis = SHARDED[name]
    rows, cols = shape2d
    if axis == 0:
        rows //= 4
    else:
        cols //= 4
    return a.reshape(DEPTH, rows, cols)


def kernel(x, mem, rel_bias, g_mix_pre, g_mix_post, w_in, gate_b, pool_w, pool_scale, a_re, a_im, log_dt, b_re, b_im, c_re, c_im, d_skip, w_glu, b_glu, sgu_ln_g, sgu_ln_b, w_s, b_s, w_up, w_out, g_x_pre, g_x_post, g_mem, w_cq, w_ckv, w_co, g_ff_pre, g_ff_post, w_ff1, w_ff2, loss_target, m_rel_bias, m_g_mix_pre, m_g_mix_post, m_w_in, m_gate_b, m_pool_w, m_pool_scale, m_a_re, m_a_im, m_log_dt, m_b_re, m_b_im, m_c_re, m_c_im, m_d_skip, m_w_glu, m_b_glu, m_sgu_ln_g, m_sgu_ln_b, m_w_s, m_b_s, m_w_up, m_w_out, m_g_x_pre, m_g_x_post, m_g_mem, m_w_cq, m_w_ckv, m_w_co, m_g_ff_pre, m_g_ff_post, m_w_ff1, m_w_ff2, v_rel_bias, v_g_mix_pre, v_g_mix_post, v_w_in, v_gate_b, v_pool_w, v_pool_scale, v_a_re, v_a_im, v_log_dt, v_b_re, v_b_im, v_c_re, v_c_im, v_d_skip, v_w_glu, v_b_glu, v_sgu_ln_g, v_sgu_ln_b, v_w_s, v_b_s, v_w_up, v_w_out, v_g_x_pre, v_g_x_post, v_g_mem, v_w_cq, v_w_ckv, v_w_co, v_g_ff_pre, v_g_ff_post, v_w_ff1, v_w_ff2):
    env = dict(locals())
    weights = {n: env[n] for n in WEIGHT_NAMES}
    mom_m = {n: env["m_" + n] for n in WEIGHT_NAMES}
    mom_v = {n: env["v_" + n] for n in WEIGHT_NAMES}
    x2d = x.reshape(x.shape[1], D_MODEL)
    mem2d = mem.reshape(N_MEM, D_MODEL)
    target = loss_target.reshape(x2d.shape)

    axis_of = {n: SHARDED[n][1] for n in SHARDED_NAMES}
    chip = _chip_index().astype(jnp.int32).reshape(1)
    rest_names = [n for n in SHARDED_NAMES if n != "w_in"]

    def gather_start(l, names, tag, after=()):
        shards = [_as3d(n, weights[n])[l].astype(F32 if n == "gate_b" else MXU_DTYPE) for n in names]
        ax = [axis_of[n] for n in names]
        lands = [lax.empty(tuple(4 * d if i == a else d for i, d in enumerate(s.shape)), s.dtype)
                 for s, a in zip(shards, ax)]
        return (names, ax, tag) + chips_start(shards, lands, ax, "gather", f"gather_start_{tag}", after=after)

    def gather_arrive(started, after):
        names, ax, tag, send, recv, shards, lands, _ = started
        shards, lands = chips_wait(send, recv, shards, lands, ax, "gather", f"gather_wait_{tag}", after=after)
        return (names, ax, tag) + chips_start(shards, lands, ax, "forward", f"gather_forward_{tag}")

    def gather_finish(arrived, after=()):
        names, ax, tag, send, recv, shards, lands, _ = arrived
        _, lands = chips_wait(send, recv, shards, lands, ax, "forward", f"gather_landed_{tag}", after=after)
        W = dict(zip(names, lands))
        if "w_up" in W:
            W["w_up"] = W["w_up"].reshape(4, W_BRANCH, D_MODEL)
        return W

    def gather_wait(started, after):
        return gather_finish(gather_arrive(started, after))

    biases = [att_bias(rel_bias, g, dil) for g, (_, dil) in enumerate(DIL_GROUPS)]
    lanes = np.arange(W_BRANCH) // ATT_HEAD_DIM
    headsum = jnp.asarray(lanes[:, None] == lanes[None, :], dtype=BF16)

    def small_params(l, s5_prepared):
        vec = lambda a: a[l].reshape(1, -1)
        return {
            "g_mix_pre": vec(g_mix_pre), "g_mix_post": vec(g_mix_post), "g_x_pre": vec(g_x_pre), "g_x_post": vec(g_x_post),
            "g_mem": vec(g_mem), "g_ff_pre": vec(g_ff_pre), "g_ff_post": vec(g_ff_post), "pool_w": pool_w[l],
            "pool_scale": vec(pool_scale), "d_skip": vec(d_skip), "b_glu": vec(b_glu), "sgu_ln_g": vec(sgu_ln_g),
            "sgu_ln_b": vec(sgu_ln_b), "w_s": w_s[l], "b_s_t": b_s[l].T, "s5": s5_prepared,
        }

    Ws, Ps, saved, s5_vjps = [], [], [], []
    xl = x2d
    hl = rms_fwd(x2d, g_mix_pre[0].reshape(1, -1), BF16, "rms_pre")
    flying = {"next": gather_start(0, ["w_in"], "0_w_in")}
    for l in range(DEPTH):
        s5_prepared, s5_vjp = jax.vjp(s5_params, a_re[l], a_im[l], log_dt[l], b_re[l], b_im[l], c_re[l], c_im[l])
        token_of = lambda started: (started[7],)
        if l == 0:
            w_in_l = gather_wait(flying["next"], [*biases, hl])["w_in"]
            flying["rest"] = gather_start(0, rest_names, "0_rest", after=[w_in_l])
            flying["next"] = gather_start(1, SHARDED_NAMES, "1", after=token_of(flying["rest"]))
            first_after = token_of(flying["rest"]) + token_of(flying["next"])

            W_l = None
        else:
            W_l = gather_finish(flying["next"], [xl])
            w_in_l, first_after = W_l["w_in"], ()
            if l + 1 < DEPTH:
                flying["next"] = gather_start(l + 1, SHARDED_NAMES, str(l + 1), after=[w_in_l])
                first_after = token_of(flying["next"])

        def rest_of(stage, value, l=l, W_l=W_l):
            if stage == "mixer" and l == 0:
                return gather_wait(flying["rest"], list(value)), ()
            if stage == "mixer":
                return W_l, ()
            if stage == "out" and l + 1 < DEPTH:
                flying["next"] = gather_arrive(flying["next"], [value])
                return {}, token_of(flying["next"])
            return {}, ()
        P = small_params(l, s5_prepared)
        g_next = g_mix_pre[l + 1].reshape(1, -1) if l + 1 < DEPTH else None
        xl, hl, sv = layer_fwd(xl, hl, mem2d, w_in_l, rest_of, P, biases, g_next, after=first_after)
        Ws.append(sv["W"])
        Ps.append(P)
        saved.append(sv)
        s5_vjps.append(s5_vjp)
    loss_local, dx = loss_and_grad(xl, target)
    loss = lax.psum(loss_local, ("x", "y", "c"))

    scattered = []

    def scatter_start(l, names, srcs):
        ax = [axis_of.get(n) for n in names]
        lands = []
        for s, a in zip(srcs, ax):
            r, c = s.shape
            lands.append(lax.empty((4, r // 4 if a == 0 else r, c // 4 if a == 1 else c), F32))
        tag = f"{l}_{names[0]}"
        send, recv, srcs, lands, token = chips_start(srcs, lands, ax, "scatter", f"grads_start_{tag}")
        scattered.append((l, names, ax, tag, send, recv, srcs, lands))
        return (token,)

    grads = [None] * DEPTH
    for l in reversed(range(DEPTH)):
        emit = lambda names, G, l=l: scatter_start(l, list(names), [G[n] for n in names])
        dx, grads[l] = layer_bwd(dx, mem2d, Ws[l], Ps[l], biases, saved[l], headsum, emit)
    grad_x = dx.reshape(x.shape)

    rep = {}
    stack = lambda key, shape: jnp.stack([grads[l][key] for l in range(DEPTH)]).reshape(shape)
    for n in ("g_mix_pre", "g_mix_post", "g_x_pre", "g_x_post", "g_mem", "g_ff_pre", "g_ff_post"):
        rep[n] = stack(n, (DEPTH, D_MODEL))
    for n in ("pool_scale", "d_skip", "b_glu", "sgu_ln_g", "sgu_ln_b"):
        rep[n] = stack(n, (DEPTH, W_BRANCH))
    rep["pool_w"] = stack("pool_w", pool_w.shape)
    rep["w_s"] = stack("w_s", w_s.shape)
    rep["b_s"] = jnp.stack([grads[l]["b_s_t"][:, :4].T for l in range(DEPTH)])
    s5_grads = [s5_vjps[l](tuple(grads[l]["s5"])) for l in range(DEPTH)]
    for i, n in enumerate(("a_re", "a_im", "log_dt", "b_re", "b_im", "c_re", "c_im")):
        rep[n] = jnp.stack([s5_grads[l][i] for l in range(DEPTH)])
    dbias = [sum(grads[l]["att_bias"][g] for l in range(DEPTH)) for g in range(len(DIL_GROUPS))]
    rep["rel_bias"] = jnp.concatenate([att_bias_grad(dbias[g], dil) for g, (_, dil) in enumerate(DIL_GROUPS)], axis=1)
    rep_shapes = [weights[n].shape for n in REPLICATED_NAMES]
    packed_g = pack_small([rep[n] for n in REPLICATED_NAMES])

    small_sent = scatter_start(0, ["small"], [packed_g])
    stacked = {}

    def collect(record, after):
        l, names, ax, tag, send, recv, srcs, lands = record
        srcs, lands = chips_wait(send, recv, srcs, lands, ax, "scatter", f"grads_wait_{tag}", after=after)
        for n, own, arrived, a in zip(names, srcs, lands, ax):
            if n not in stacked:
                stacked[n] = lax.empty((1 if n == "small" else DEPTH,) + arrived.shape[1:], F32)
            stacked[n] = sum_chips(arrived, own, a, chip, stacked[n], 0 if n == "small" else l, "sum_chips")

    out_g, out_d, out_m, out_v = {}, {}, {}, {}

    def update(names, tag):
        partial = [stacked[n].reshape(-1, stacked[n].shape[-1]) for n in names]
        other = swap_cores(partial, f"swap_cores_{tag}")
        for n, mine, theirs in zip(names, partial, other):
            if n == "small":
                for name, ga, gb in zip(REPLICATED_NAMES, unpack_small(mine, rep_shapes), unpack_small(theirs, rep_shapes)):
                    rows_of = lambda a: a.reshape(-1, a.shape[-1])
                    res = adamw(rows_of(weights[name]), rows_of(ga), rows_of(gb), rows_of(mom_m[name]),
                                rows_of(mom_v[name]), "adamw_small")
                    out_g[name], out_d[name], out_m[name], out_v[name] = [r.reshape(weights[name].shape) for r in res]
            else:
                flat = lambda a: a.reshape(mine.shape)
                res = adamw(flat(weights[n]), mine, theirs, flat(mom_m[n]), flat(mom_v[n]), "adamw")
                out_g[n], out_d[n], out_m[n], out_v[n] = [r.reshape(weights[n].shape) for r in res]

    late = [r for r in scattered if r[1] == ["small"] or (r[0] == 0 and r[1] == ["w_in"])]
    for record in scattered:
        if not any(record is r for r in late):
            collect(record, [dx, *small_sent])
    update(rest_names, "rest")
    collect(late[0], [out_d[n] for n in rest_names])
    update(["w_in"], "w_in")
    collect(late[1], [out_d["w_in"]])
    update(["small"], "small")

    return (loss, grad_x, *[out_g[n] for n in WEIGHT_NAMES], *[out_d[n] for n in WEIGHT_NAMES],
            *[out_m[n] for n in WEIGHT_NAMES], *[out_v[n] for n in WEIGHT_NAMES])
```
